```python
import jax, jax.numpy as jnp
from jax import lax
import numpy as np

D_MODEL = 1024
BATCH = 8
SEQ = 16384
DEPTH = 1

D_MIX = D_MODEL
DN_HEADS = 4
DN_HEAD_DIM = D_MODEL // 8
DN_WIDTH = DN_HEADS * DN_HEAD_DIM
CONV_K = 4
CHUNK = 64
AT_HEADS = 8
AT_HEAD_DIM = D_MODEL // 16
AT_WIDTH = AT_HEADS * AT_HEAD_DIM
PATTERNS = ((128, 1), (512, 4), (2048, 16))
Q_BLOCK = 128
ROPE_THETA = 10000.0
EPS = 1e-6
SPLIT_SIZES = (3 * DN_WIDTH, DN_WIDTH, DN_HEADS, DN_HEADS, AT_WIDTH, AT_WIDTH, AT_WIDTH, AT_WIDTH)
IN_COLS = 4 * DN_WIDTH + 2 * DN_HEADS + 4 * AT_WIDTH

kernel_name = "hybrid_deltanet_dilated_swa_adaln"


def rmsnorm(x, w):
    xf = x.astype(jnp.float32)
    xf = xf * lax.rsqrt(jnp.mean(xf * xf, axis=-1, keepdims=True) + EPS)
    return xf.astype(x.dtype) * w


def l2norm(x):
    xf = x.astype(jnp.float32)
    return xf * lax.rsqrt(jnp.sum(xf * xf, axis=-1, keepdims=True) + EPS)


def rope(x, positions):
    hd = x.shape[-1]
    half = hd // 2
    inv_freq = ROPE_THETA ** (-jnp.arange(half, dtype=jnp.float32) / half)
    ang = positions.astype(jnp.float32)[..., None] * inv_freq
    cos = jnp.cos(ang)[:, :, None, :]
    sin = jnp.sin(ang)[:, :, None, :]
    x1, x2 = x[..., :half], x[..., half:]
    out = jnp.concatenate([x1 * cos - x2 * sin, x2 * cos + x1 * sin], axis=-1)
    return out.astype(x.dtype)


def causal_short_conv(x, w):
    K = w.shape[0]
    S = x.shape[1]
    xp = jnp.pad(x, ((0, 0), (K - 1, 0), (0, 0)))
    out = xp[:, 0:S] * w[0]
    for j in range(1, K):
        out = out + xp[:, j:j + S] * w[j]
    return out


def gated_delta_rule(q, k, v, g, beta):
    B, S, H, dk = q.shape
    dv = v.shape[-1]
    nc = S // CHUNK

    def chunks4(t):
        return t.reshape(B, nc, CHUNK, H, t.shape[-1]).transpose(0, 3, 1, 2, 4)

    def chunks3(t):
        return t.reshape(B, nc, CHUNK, H).transpose(0, 3, 1, 2)

    q = chunks4(q) * (dk ** -0.5)
    k = chunks4(k)
    v = chunks4(v)
    beta = chunks3(beta)
    gc = jnp.cumsum(chunks3(g), axis=-1)

    tril = jnp.tril(jnp.ones((CHUNK, CHUNK), dtype=bool))
    strict = tril & ~jnp.eye(CHUNK, dtype=bool)
    decay_mat = jnp.exp(jnp.where(tril, gc[..., :, None] - gc[..., None, :], -jnp.inf))

    kb = k * beta[..., None]
    vb = v * beta[..., None]
    a_low = jnp.where(strict, jnp.einsum('bhncd,bhnsd->bhncs', kb, k) * decay_mat, 0.0)
    ia = a_low + jnp.eye(CHUNK, dtype=jnp.float32)
    u = lax.linalg.triangular_solve(ia, vb, left_side=True, lower=True, unit_diagonal=True)
    w = lax.linalg.triangular_solve(ia, kb * jnp.exp(gc)[..., None],
                                    left_side=True, lower=True, unit_diagonal=True)
    attn_intra = jnp.where(tril, jnp.einsum('bhncd,bhnsd->bhncs', q, k) * decay_mat, 0.0)
    q_dec = q * jnp.exp(gc)[..., None]
    k_dec = k * jnp.exp(gc[..., -1:] - gc)[..., None]
    g_last = jnp.exp(gc[..., -1])

    xs = (jnp.moveaxis(u, 2, 0), jnp.moveaxis(w, 2, 0), jnp.moveaxis(q_dec, 2, 0),
          jnp.moveaxis(k_dec, 2, 0), jnp.moveaxis(attn_intra, 2, 0), jnp.moveaxis(g_last, 2, 0))

    def step(state, inp):
        u_n, w_n, qd_n, kd_n, at_n, gl_n = inp
        v_new = u_n - jnp.einsum('bhcd,bhde->bhce', w_n, state)
        o = jnp.einsum('bhcd,bhde->bhce', qd_n, state) + jnp.einsum('bhcs,bhse->bhce', at_n, v_new)
        state = state * gl_n[..., None, None] + jnp.einsum('bhcd,bhce->bhde', kd_n, v_new)
        return state, o

    s0 = jnp.zeros((B, H, dk, dv), dtype=jnp.float32)
    _, o = lax.scan(step, s0, xs)
    return o.transpose(1, 0, 3, 2, 4).reshape(B, S, H, dv)


def strided_window_attention(q, k, v, dilation, w_sub):
    B, S, H, hd = q.shape
    L = S // dilation
    nb = -(-L // Q_BLOCK)
    Lp = nb * Q_BLOCK

    def split(t):
        t = t.reshape(B, L, dilation, H, hd).transpose(0, 2, 1, 3, 4)
        return jnp.pad(t, ((0, 0), (0, 0), (0, Lp - L), (0, 0), (0, 0)))

    def band(t):
        tp = jnp.pad(t, ((0, 0), (0, 0), (Q_BLOCK, 0), (0, 0), (0, 0)))
        tp = tp.reshape(B, dilation, nb + 1, Q_BLOCK, H, hd)
        return jnp.concatenate([tp[:, :, :-1], tp[:, :, 1:]], axis=3)

    qb = split(q).reshape(B, dilation, nb, Q_BLOCK, H, hd)
    kb = band(split(k))
    vb = band(split(v))

    s = jnp.einsum('bdnqhe,bdnkhe->bdnhqk', qb, kb).astype(jnp.float32) * (hd ** -0.5)
    qi = jnp.arange(Q_BLOCK)[:, None]
    kj = jnp.arange(2 * Q_BLOCK)[None, :]
    rel = Q_BLOCK + qi - kj
    kidx = jnp.arange(nb)[:, None] * Q_BLOCK - Q_BLOCK + jnp.arange(2 * Q_BLOCK)[None, :]
    mask = ((rel >= 0) & (rel <= w_sub))[None] & (kidx >= 0)[:, None, :]
    s = jnp.where(mask[None, None, :, None], s, -jnp.inf)
    m = jnp.max(s, axis=-1, keepdims=True)
    p = jnp.exp(s - m)
    l = jnp.sum(p, axis=-1, keepdims=True)
    o = jnp.einsum('bdnhqk,bdnkhe->bdnqhe', p / l, vb.astype(jnp.float32))
    lse = (m + jnp.log(l))[..., 0]

    o = o.reshape(B, dilation, Lp, H, hd)[:, :, :L].transpose(0, 2, 1, 3, 4).reshape(B, S, H, hd)
    lse = lse.transpose(0, 1, 2, 4, 3).reshape(B, dilation, Lp, H)[:, :, :L]
    lse = lse.transpose(0, 2, 1, 3).reshape(B, S, H)
    return o, lse


def dilated_attention(q, k, v):
    outs, lses = [], []
    for window, dilation in PATTERNS:
        o, lse = strided_window_attention(q, k, v, dilation, window // dilation)
        outs.append(o)
        lses.append(lse)
    wts = jax.nn.softmax(jnp.stack(lses, axis=0), axis=0)
    return jnp.sum(wts[..., None] * jnp.stack(outs, axis=0), axis=0)


def _fwd_setup_inputs(seed: int = 0) -> dict:
    key = jax.random.key(seed)
    ks = jax.random.split(key, 16)
    f32 = jnp.float32
    x = jax.random.normal(ks[0], (BATCH, SEQ, D_MODEL), f32)
    c = jax.random.normal(ks[1], (BATCH, D_MODEL), f32)
    positions = jnp.broadcast_to(jnp.arange(SEQ, dtype=jnp.int32)[None], (BATCH, SEQ))
    w_mod = jax.random.normal(ks[2], (DEPTH, D_MODEL, 3 * D_MODEL), f32) * (0.2 * D_MODEL ** -0.5)
    b_mod = jax.random.normal(ks[3], (DEPTH, 3 * D_MODEL), f32) * 0.01
    norm_w = 1.0 + 0.01 * jax.random.normal(ks[4], (DEPTH, D_MODEL), f32)
    w_in = jax.random.normal(ks[5], (DEPTH, D_MODEL, IN_COLS), f32) * (D_MODEL ** -0.5)
    conv_w = jax.random.normal(ks[6], (DEPTH, CONV_K, 3 * DN_WIDTH), f32) * (CONV_K ** -0.5)
    a_log = jnp.log(jax.random.uniform(ks[7], (DEPTH, DN_HEADS), f32, 1.0, 16.0))
    dt = jnp.exp(jax.random.uniform(ks[8], (DEPTH, DN_HEADS), f32, jnp.log(1e-3), jnp.log(1e-1)))
    dt_bias = dt + jnp.log(-jnp.expm1(-dt))
    dn_norm_w = 1.0 + 0.01 * jax.random.normal(ks[9], (DEPTH, DN_HEAD_DIM), f32)
    at_norm_w = 1.0 + 0.01 * jax.random.normal(ks[10], (DEPTH, AT_HEAD_DIM), f32)
    w_out = jax.random.normal(ks[11], (DEPTH, D_MIX, D_MODEL), f32) * (D_MIX ** -0.5)
    final_norm_w = 1.0 + 0.01 * jax.random.normal(ks[12], (D_MODEL,), f32)
    return {"x": x, "c": c, "positions": positions, "w_mod": w_mod, "b_mod": b_mod,
            "norm_w": norm_w, "w_in": w_in, "conv_w": conv_w, "a_log": a_log,
            "dt_bias": dt_bias, "dn_norm_w": dn_norm_w, "at_norm_w": at_norm_w,
            "w_out": w_out, "final_norm_w": final_norm_w}


def _fwd_reference(x, c, positions, w_mod, b_mod, norm_w, w_in, conv_w, a_log, dt_bias,
              dn_norm_w, at_norm_w, w_out, final_norm_w):
    B, S, _ = x.shape
    offsets = tuple(int(o) for o in np.cumsum(SPLIT_SIZES)[:-1])
    for layer in range(DEPTH):
        mod = jax.nn.silu(c) @ w_mod[layer] + b_mod[layer]
        shift, scale, gate = jnp.split(mod, 3, axis=-1)
        h = rmsnorm(x, norm_w[layer]) * (1.0 + scale[:, None]) + shift[:, None]

        proj = h @ w_in[layer]
        dn_qkv, dn_z, dn_b, dn_a, at_q, at_k, at_v, at_z = jnp.split(proj, offsets, axis=-1)

        dn_qkv = jax.nn.silu(causal_short_conv(dn_qkv, conv_w[layer]))
        dq, dk_, dvv = jnp.split(dn_qkv, 3, axis=-1)
        dq = l2norm(dq.reshape(B, S, DN_HEADS, DN_HEAD_DIM))
        dk_ = l2norm(dk_.reshape(B, S, DN_HEADS, DN_HEAD_DIM))
        dvv = dvv.reshape(B, S, DN_HEADS, DN_HEAD_DIM).astype(jnp.float32)
        beta = jax.nn.sigmoid(dn_b.astype(jnp.float32))
        g = -jnp.exp(a_log[layer].astype(jnp.float32)) * jax.nn.softplus(
            (dn_a + dt_bias[layer]).astype(jnp.float32))
        o_dn = gated_delta_rule(dq, dk_, dvv, g, beta).astype(x.dtype)
        o_dn = rmsnorm(o_dn, dn_norm_w[layer]) * jax.nn.silu(dn_z.reshape(B, S, DN_HEADS, DN_HEAD_DIM))
        o_dn = o_dn.reshape(B, S, DN_WIDTH)

        aq = rope(at_q.reshape(B, S, AT_HEADS, AT_HEAD_DIM), positions)
        ak = rope(at_k.reshape(B, S, AT_HEADS, AT_HEAD_DIM), positions)
        av = at_v.reshape(B, S, AT_HEADS, AT_HEAD_DIM)
        o_at = dilated_attention(aq, ak, av).astype(x.dtype)
        o_at = rmsnorm(o_at, at_norm_w[layer]) * jax.nn.silu(at_z.reshape(B, S, AT_HEADS, AT_HEAD_DIM))
        o_at = o_at.reshape(B, S, AT_WIDTH)

        mix = jnp.concatenate([o_dn, o_at], axis=-1) @ w_out[layer]
        x = x + gate[:, None] * mix
    return rmsnorm(x, final_norm_w)


import jax as _jax
import jax.numpy as _jnp

TWIN_FORMAT = 'train_step'
FWD_PARAMS = ['x', 'c', 'positions', 'w_mod', 'b_mod', 'norm_w', 'w_in', 'conv_w', 'a_log', 'dt_bias', 'dn_norm_w', 'at_norm_w', 'w_out', 'final_norm_w']
TWIN_WEIGHTS = ['w_mod', 'b_mod', 'norm_w', 'w_in', 'conv_w', 'a_log', 'dt_bias', 'dn_norm_w', 'at_norm_w', 'w_out', 'final_norm_w']
TWIN_DIFF_INPUT = 'x'
TWIN_INPUTS = ['x', 'c', 'positions', 'w_mod', 'b_mod', 'norm_w', 'w_in', 'conv_w', 'a_log', 'dt_bias', 'dn_norm_w', 'at_norm_w', 'w_out', 'final_norm_w', 'loss_target', 'm_w_mod', 'm_b_mod', 'm_norm_w', 'm_w_in', 'm_conv_w', 'm_a_log', 'm_dt_bias', 'm_dn_norm_w', 'm_at_norm_w', 'm_w_out', 'm_final_norm_w', 'v_w_mod', 'v_b_mod', 'v_norm_w', 'v_w_in', 'v_conv_w', 'v_a_log', 'v_dt_bias', 'v_dn_norm_w', 'v_at_norm_w', 'v_w_out', 'v_final_norm_w']
TWIN_OUTPUTS = ['loss', 'grad_x', 'grad_w_mod', 'grad_b_mod', 'grad_norm_w', 'grad_w_in', 'grad_conv_w', 'grad_a_log', 'grad_dt_bias', 'grad_dn_norm_w', 'grad_at_norm_w', 'grad_w_out', 'grad_final_norm_w', 'delta_w_mod', 'delta_b_mod', 'delta_norm_w', 'delta_w_in', 'delta_conv_w', 'delta_a_log', 'delta_dt_bias', 'delta_dn_norm_w', 'delta_at_norm_w', 'delta_w_out', 'delta_final_norm_w', 'new_m_w_mod', 'new_m_b_mod', 'new_m_norm_w', 'new_m_w_in', 'new_m_conv_w', 'new_m_a_log', 'new_m_dt_bias', 'new_m_dn_norm_w', 'new_m_at_norm_w', 'new_m_w_out', 'new_m_final_norm_w', 'new_v_w_mod', 'new_v_b_mod', 'new_v_norm_w', 'new_v_w_in', 'new_v_conv_w', 'new_v_a_log', 'new_v_dt_bias', 'new_v_dn_norm_w', 'new_v_at_norm_w', 'new_v_w_out', 'new_v_final_norm_w']
TWIN_LEAF_KINDS = {'loss': 'loss', 'grad_x': 'grad_x', 'grad_w_mod': 'grad_w', 'grad_b_mod': 'grad_w', 'grad_norm_w': 'grad_w', 'grad_w_in': 'grad_w', 'grad_conv_w': 'grad_w', 'grad_a_log': 'grad_w', 'grad_dt_bias': 'grad_w', 'grad_dn_norm_w': 'grad_w', 'grad_at_norm_w': 'grad_w', 'grad_w_out': 'grad_w', 'grad_final_norm_w': 'grad_w', 'delta_w_mod': 'delta_w', 'delta_b_mod': 'delta_w', 'delta_norm_w': 'delta_w', 'delta_w_in': 'delta_w', 'delta_conv_w': 'delta_w', 'delta_a_log': 'delta_w', 'delta_dt_bias': 'delta_w', 'delta_dn_norm_w': 'delta_w', 'delta_at_norm_w': 'delta_w', 'delta_w_out': 'delta_w', 'delta_final_norm_w': 'delta_w', 'new_m_w_mod': 'new_m', 'new_m_b_mod': 'new_m', 'new_m_norm_w': 'new_m', 'new_m_w_in': 'new_m', 'new_m_conv_w': 'new_m', 'new_m_a_log': 'new_m', 'new_m_dt_bias': 'new_m', 'new_m_dn_norm_w': 'new_m', 'new_m_at_norm_w': 'new_m', 'new_m_w_out': 'new_m', 'new_m_final_norm_w': 'new_m', 'new_v_w_mod': 'new_v', 'new_v_b_mod': 'new_v', 'new_v_norm_w': 'new_v', 'new_v_w_in': 'new_v', 'new_v_conv_w': 'new_v', 'new_v_a_log': 'new_v', 'new_v_dt_bias': 'new_v', 'new_v_dn_norm_w': 'new_v', 'new_v_at_norm_w': 'new_v', 'new_v_w_out': 'new_v', 'new_v_final_norm_w': 'new_v'}


def _forward(args):
    return _fwd_reference(*[args[k] for k in FWD_PARAMS])


def _output_shape():
    def fwd():
        inp = _fwd_setup_inputs(0)
        return _fwd_reference(*[inp[k] for k in FWD_PARAMS])
    out = _jax.eval_shape(fwd)
    return out.shape, out.dtype

N_MICROBATCH = 1
ADAM_LR = 0.001
ADAM_B1 = 0.9
ADAM_B2 = 0.999
ADAM_EPS = 1e-08
ADAM_WD = 0.01
ADAM_STEP = 10
PER_EXAMPLE_BATCH_AXIS = {'x': 0, 'c': 0, 'positions': 0, 'loss_target': 0}
SHARED_INPUTS = []
_WEIGHT_DTYPES = {'w_mod': _jnp.float32, 'b_mod': _jnp.float32, 'norm_w': _jnp.float32, 'w_in': _jnp.float32, 'conv_w': _jnp.float32, 'a_log': _jnp.float32, 'dt_bias': _jnp.float32, 'dn_norm_w': _jnp.float32, 'at_norm_w': _jnp.float32, 'w_out': _jnp.float32, 'final_norm_w': _jnp.float32}
MOMENT_SCALE = {'w_mod': 9.211327e-02, 'b_mod': 1.612253e-01, 'norm_w': 4.356304e-02, 'w_in': 2.259856e-02, 'conv_w': 2.118809e-02, 'a_log': 2.856333e-01, 'dt_bias': 2.630398e-01, 'dn_norm_w': 4.986447e-02, 'at_norm_w': 7.525378e-02, 'w_out': 2.590867e-02, 'final_norm_w': 1.280024e+02}


def _to_microbatches(a, axis):
    t = _jnp.moveaxis(a, axis, 0)
    t = t.reshape((N_MICROBATCH, t.shape[0] // N_MICROBATCH) + t.shape[1:])
    return _jnp.moveaxis(t, 1, axis + 1)


def setup_inputs(seed: int = 0) -> dict:
    inp = _fwd_setup_inputs(seed)
    key = _jax.random.fold_in(_jax.random.key(seed), 7919)
    shape, _ = _output_shape()
    out = dict(inp)
    out["loss_target"] = _jax.random.normal(_jax.random.fold_in(key, 0), shape, _jnp.float32)
    for i, name in enumerate(TWIN_WEIGHTS):
        w = inp[name].astype(_jnp.float32)
        if MOMENT_SCALE is None:
            s = _jnp.sqrt(_jnp.mean(_jnp.square(w)) + 1e-30)
        else:
            s = MOMENT_SCALE[name]
        km, kv = _jax.random.split(_jax.random.fold_in(key, i + 1))
        out[name] = w
        out["m_" + name] = s * _jax.random.normal(km, w.shape, _jnp.float32)
        out["v_" + name] = (s * s) * _jax.random.uniform(kv, w.shape, _jnp.float32, 0.5, 1.5)
    if N_MICROBATCH > 1:
        for name, axis in PER_EXAMPLE_BATCH_AXIS.items():
            out[name] = _to_microbatches(out[name], axis)
    return {'x': out['x'], 'c': out['c'], 'positions': out['positions'], 'w_mod': out['w_mod'], 'b_mod': out['b_mod'], 'norm_w': out['norm_w'], 'w_in': out['w_in'], 'conv_w': out['conv_w'], 'a_log': out['a_log'], 'dt_bias': out['dt_bias'], 'dn_norm_w': out['dn_norm_w'], 'at_norm_w': out['at_norm_w'], 'w_out': out['w_out'], 'final_norm_w': out['final_norm_w'], 'loss_target': out['loss_target'], 'm_w_mod': out['m_w_mod'], 'm_b_mod': out['m_b_mod'], 'm_norm_w': out['m_norm_w'], 'm_w_in': out['m_w_in'], 'm_conv_w': out['m_conv_w'], 'm_a_log': out['m_a_log'], 'm_dt_bias': out['m_dt_bias'], 'm_dn_norm_w': out['m_dn_norm_w'], 'm_at_norm_w': out['m_at_norm_w'], 'm_w_out': out['m_w_out'], 'm_final_norm_w': out['m_final_norm_w'], 'v_w_mod': out['v_w_mod'], 'v_b_mod': out['v_b_mod'], 'v_norm_w': out['v_norm_w'], 'v_w_in': out['v_w_in'], 'v_conv_w': out['v_conv_w'], 'v_a_log': out['v_a_log'], 'v_dt_bias': out['v_dt_bias'], 'v_dn_norm_w': out['v_dn_norm_w'], 'v_at_norm_w': out['v_at_norm_w'], 'v_w_out': out['v_w_out'], 'v_final_norm_w': out['v_final_norm_w']}


def _loss(weights, diff, rest, loss_target):
    with _jax.named_scope("forward"):
        args = {**rest, TWIN_DIFF_INPUT: diff, **{k: w.astype(_WEIGHT_DTYPES[k]) for k, w in weights.items()}}
        y = _forward(args)
    with _jax.named_scope("loss_head"):
        err = _jnp.square(y.astype(_jnp.float32) - loss_target)
        return 0.5 * _jnp.sum(_jnp.mean(err, axis=-1)) if err.ndim else 0.5 * err


def _adamw(w, g, m, v):
    m = ADAM_B1 * m + (1.0 - ADAM_B1) * g
    v = ADAM_B2 * v + (1.0 - ADAM_B2) * _jnp.square(g)
    m_hat = m / (1.0 - ADAM_B1 ** ADAM_STEP)
    v_hat = v / (1.0 - ADAM_B2 ** ADAM_STEP)
    delta = -ADAM_LR * (m_hat / (_jnp.sqrt(v_hat) + ADAM_EPS) + ADAM_WD * w)
    return delta, m, v


def reference(x, c, positions, w_mod, b_mod, norm_w, w_in, conv_w, a_log, dt_bias, dn_norm_w, at_norm_w, w_out, final_norm_w, loss_target, m_w_mod, m_b_mod, m_norm_w, m_w_in, m_conv_w, m_a_log, m_dt_bias, m_dn_norm_w, m_at_norm_w, m_w_out, m_final_norm_w, v_w_mod, v_b_mod, v_norm_w, v_w_in, v_conv_w, v_a_log, v_dt_bias, v_dn_norm_w, v_at_norm_w, v_w_out, v_final_norm_w):
    given = dict(x=x, c=c, positions=positions, w_mod=w_mod, b_mod=b_mod, norm_w=norm_w, w_in=w_in, conv_w=conv_w, a_log=a_log, dt_bias=dt_bias, dn_norm_w=dn_norm_w, at_norm_w=at_norm_w, w_out=w_out, final_norm_w=final_norm_w, loss_target=loss_target, m_w_mod=m_w_mod, m_b_mod=m_b_mod, m_norm_w=m_norm_w, m_w_in=m_w_in, m_conv_w=m_conv_w, m_a_log=m_a_log, m_dt_bias=m_dt_bias, m_dn_norm_w=m_dn_norm_w, m_at_norm_w=m_at_norm_w, m_w_out=m_w_out, m_final_norm_w=m_final_norm_w, v_w_mod=v_w_mod, v_b_mod=v_b_mod, v_norm_w=v_norm_w, v_w_in=v_w_in, v_conv_w=v_conv_w, v_a_log=v_a_log, v_dt_bias=v_dt_bias, v_dn_norm_w=v_dn_norm_w, v_at_norm_w=v_at_norm_w, v_w_out=v_w_out, v_final_norm_w=v_final_norm_w)
    weights = {n: given[n] for n in TWIN_WEIGHTS}
    shared = {n: given[n] for n in SHARED_INPUTS}
    per_example = {n: given[n] for n in ['x', 'c', 'positions']}
    grad_fn = _jax.value_and_grad(_loss, argnums=(0, 1))

    def one_microbatch(ex, loss_target):
        ex = dict(ex)
        diff = ex.pop(TWIN_DIFF_INPUT)
        return grad_fn(weights, diff, {**shared, **ex}, loss_target)

    if N_MICROBATCH == 1:
        loss, (grad_w, grad_x) = one_microbatch(per_example, given["loss_target"])
    else:
        def body(carry, xs):
            loss_sum, grad_sum = carry
            l_k, (gw_k, gx_k) = one_microbatch(xs[0], xs[1])
            with _jax.named_scope("update"):
                return (loss_sum + l_k, _jax.tree.map(_jnp.add, grad_sum, gw_k)), gx_k

        init = (_jnp.zeros((), _jnp.float32), _jax.tree.map(_jnp.zeros_like, weights))
        (loss, grad_w), grad_x = _jax.lax.scan(body, init, (per_example, given["loss_target"]))
    with _jax.named_scope("update"):
        delta_w, new_m, new_v = {}, {}, {}
        for n in TWIN_WEIGHTS:
            delta_w[n], new_m[n], new_v[n] = _adamw(weights[n], grad_w[n], given["m_" + n], given["v_" + n])
    return (loss, grad_x, *[grad_w[n] for n in TWIN_WEIGHTS], *[delta_w[n] for n in TWIN_WEIGHTS],
            *[new_m[n] for n in TWIN_WEIGHTS], *[new_v[n] for n in TWIN_WEIGHTS])
```

```python
import functools

import jax
import jax.numpy as jnp
from jax import lax
from jax.experimental import pallas as pl
from jax.experimental.pallas import tpu as pltpu

f32 = jnp.float32
bf16 = jnp.bfloat16
HIGHEST = lax.Precision.HIGHEST
MESH = pl.DeviceIdType.MESH

D_MODEL = 1024
DN_HEADS = 4
DN_DIM = 128
DN_WIDTH = 512
AT_HEADS = 8
AT_DIM = 64
AT_WIDTH = 512
CHUNK = 64
Q_BLOCK = 128
CONV_K = 4
EPS = 1e-6
ROPE_THETA = 10000.0
PATTERN_DILATIONS = (1, 4, 16)
NEG = -1e30

QKV_W = 3 * DN_WIDTH
BA_W = 128
PDN_W = QKV_W + DN_WIDTH + BA_W
PAT_W = 3 * AT_WIDTH
CAT_W = PDN_W + PAT_W + AT_WIDTH
IN_COLS = 4104
N_CHIPS = 4
N_DEV = 8
SHARD_IN = IN_COLS // N_CHIPS
PACK_ROWS = 1296
PACK_CHUNK = 48

ADAM_LR = 0.001
ADAM_B1 = 0.9
ADAM_B2 = 0.999
ADAM_EPS = 1e-08
ADAM_WD = 0.01
ADAM_STEP = 10

VMEM_LIMIT = 56 * 1024 * 1024

NN = ((1,), (0,))
NT = ((1,), (1,))
TN = ((0,), (0,))


def _dot(a, b, dims, exact):
    if exact:
        return lax.dot_general(a, b, (dims, ((), ())), precision=HIGHEST, preferred_element_type=f32)
    return lax.dot_general(a.astype(bf16), b.astype(bf16), (dims, ((), ())), preferred_element_type=f32)


class _Matmuls:
    def __init__(self, exact):
        @jax.custom_vjp
        def nn(a, b):
            return _dot(a, b, NN, exact)

        def nn_fwd(a, b):
            return _dot(a, b, NN, exact), (a, b)

        def nn_bwd(res, g):
            a, b = res
            return _dot(g, b, NT, exact), _dot(a, g, TN, exact)

        nn.defvjp(nn_fwd, nn_bwd)

        @jax.custom_vjp
        def nt(a, b):
            return _dot(a, b, NT, exact)

        def nt_fwd(a, b):
            return _dot(a, b, NT, exact), (a, b)

        def nt_bwd(res, g):
            a, b = res
            return _dot(g, b, NN, exact), _dot(g, a, TN, exact)

        nt.defvjp(nt_fwd, nt_bwd)

        @jax.custom_vjp
        def tn(a, b):
            return _dot(a, b, TN, exact)

        def tn_fwd(a, b):
            return _dot(a, b, TN, exact), (a, b)

        def tn_bwd(res, g):
            a, b = res
            return _dot(b, g, NT, exact), _dot(a, g, NN, exact)

        tn.defvjp(tn_fwd, tn_bwd)
        self.nn, self.nt, self.tn = nn, nt, tn


MM = _Matmuls(exact=False)
MX = _Matmuls(exact=True)


def _params(semantics=None):
    return pltpu.CompilerParams(dimension_semantics=semantics, vmem_limit_bytes=VMEM_LIMIT)


def _silu(x):
    return x * jax.nn.sigmoid(x)


def _group_ones(width, group):
    r = lax.broadcasted_iota(jnp.int32, (width, width), 0) // group
    c = lax.broadcasted_iota(jnp.int32, (width, width), 1) // group
    return (r == c).astype(f32)


def _hnorm(x, nw, scale, shift):
    xn = x * lax.rsqrt(jnp.mean(x * x, axis=-1, keepdims=True) + EPS)
    return xn * nw * (1.0 + scale) + shift


def _rope_tables(pos_col, invf_row, sign_row):
    ang = pos_col.astype(f32) * invf_row
    cos_t = jnp.cos(ang)
    sin_t = jnp.sin(ang) * sign_row
    return jnp.concatenate([cos_t] * 4, axis=1), jnp.concatenate([sin_t] * 4, axis=1)


def _rope_partner(x):
    lane = lax.broadcasted_iota(jnp.int32, x.shape, 1)
    width = x.shape[1]
    return jnp.where((lane % AT_DIM) < AT_DIM // 2, pltpu.roll(x, width - AT_DIM // 2, 1), pltpu.roll(x, AT_DIM // 2, 1))


def _pre_proj(x, pos_col, invf_row, sign_row, norm_w, scale, shift, w_cat):
    seq = x.shape[0]
    tm = 256

    def body(x_ref, pos_ref, invf_ref, sign_ref, nw_ref, sc_ref, sh_ref, w_ref, pdn_ref, pat_ref, zat_ref, h_ref):
        h = _hnorm(x_ref[...], nw_ref[...], sc_ref[...], sh_ref[...]).astype(bf16)
        h_ref[...] = h
        big = jnp.dot(h, w_ref[...], preferred_element_type=f32)
        pdn_ref[...] = big[:, :PDN_W]
        cos_t, sin_t = _rope_tables(pos_ref[...], invf_ref[...], sign_ref[...])
        q = big[:, PDN_W:PDN_W + AT_WIDTH]
        k = big[:, PDN_W + AT_WIDTH:PDN_W + 2 * AT_WIDTH]
        pat_ref[:, 0:AT_WIDTH] = q * cos_t + _rope_partner(q) * sin_t
        pat_ref[:, AT_WIDTH:2 * AT_WIDTH] = k * cos_t + _rope_partner(k) * sin_t
        pat_ref[:, 2 * AT_WIDTH:] = big[:, PDN_W + 2 * AT_WIDTH:PDN_W + PAT_W]
        zat_ref[...] = big[:, PDN_W + PAT_W:]

    row = lambda w: pl.BlockSpec((1, w), lambda i: (0, 0))
    tile = lambda w: pl.BlockSpec((tm, w), lambda i: (i, 0))
    return pl.pallas_call(
        body, name="pre_proj", grid=(seq // tm,),
        in_specs=[tile(D_MODEL), tile(1), row(128), row(128), row(D_MODEL), row(D_MODEL), row(D_MODEL),
                  pl.BlockSpec((D_MODEL, CAT_W), lambda i: (0, 0))],
        out_specs=[tile(PDN_W), tile(PAT_W), tile(AT_WIDTH), tile(D_MODEL)],
        out_shape=[jax.ShapeDtypeStruct((seq, PDN_W), f32), jax.ShapeDtypeStruct((seq, PAT_W), f32),
                   jax.ShapeDtypeStruct((seq, AT_WIDTH), f32), jax.ShapeDtypeStruct((seq, D_MODEL), bf16)],
        compiler_params=_params(("parallel",)),
    )(x, pos_col, invf_row, sign_row, norm_w, scale, shift, w_cat)


def _conv_taps(ext_ref, halo, cur, w8):
    rows = cur.shape[0]
    ext_ref[0:8, :] = halo
    ext_ref[8:8 + rows, :] = cur
    out = ext_ref[pl.ds(5, rows), :] * w8[0:1, :]
    for j in range(1, CONV_K):
        out = out + ext_ref[pl.ds(5 + j, rows), :] * w8[j:j + 1, :]
    return out


def _dn_pre(cq, ck, cv, ba, a_row, dt_row):
    ones = _group_ones(DN_WIDTH, DN_DIM)
    sq, sk, v = _silu(cq), _silu(ck), _silu(cv)
    qn = sq * lax.rsqrt(MX.nn(sq * sq, ones) + EPS)
    kn = sk * lax.rsqrt(MX.nn(sk * sk, ones) + EPS)
    beta_all = jax.nn.sigmoid(ba)
    g_all = -jnp.exp(a_row) * jax.nn.softplus(ba + dt_row)
    return qn, kn, v, beta_all, g_all


def _dn_chunk(q, k, v, beta, g, state):
    ri = lax.broadcasted_iota(jnp.int32, (CHUNK, CHUNK), 0)
    ci = lax.broadcasted_iota(jnp.int32, (CHUNK, CHUNK), 1)
    tril = ri >= ci
    strict = ri > ci
    lower = tril.astype(f32)
    eye = (ri == ci).astype(f32)
    g_wide = jnp.broadcast_to(g, (CHUNK, DN_DIM))
    gc = MX.nn(lower, g_wide)
    gc_sq = MX.nn(lower, jnp.broadcast_to(g, (CHUNK, CHUNK)))
    g_end = MX.nn(jnp.ones((CHUNK, CHUNK), f32), g_wide)
    g_end_state = MX.nn(jnp.ones((DN_DIM, CHUNK), f32), g_wide)
    decay = jnp.exp(jnp.where(tril, gc_sq - gc_sq.T, -jnp.inf))
    q = q * (DN_DIM ** -0.5)
    kb = k * beta
    vb = v * beta
    a_low = jnp.where(strict, MM.nt(kb, k) * decay, 0.0)
    power = -a_low
    inv = eye + power
    for _ in range(5):
        power = MX.nn(power, power)
        inv = inv + MX.nn(inv, power)
    e_gc = jnp.exp(gc)
    u = MX.nn(inv, vb)
    w = MX.nn(inv, kb * e_gc)
    attn = jnp.where(tril, MM.nt(q, k) * decay, 0.0)
    q_dec = q * e_gc
    k_dec = k * jnp.exp(g_end - gc)
    v_new = u - MM.nn(w, state)
    o = MM.nn(q_dec, state) + MM.nn(attn, v_new)
    new_state = state * jnp.exp(g_end_state) + MM.tn(k_dec, v_new)
    return o, new_state


def _dn_specs(nc, reverse):
    chunk_of = (lambda i: nc - 1 - i) if reverse else (lambda i: i)
    cur = pl.BlockSpec((CHUNK, QKV_W), lambda i: (chunk_of(i), 0))
    halo = pl.BlockSpec((8, QKV_W), lambda i: (jnp.maximum(chunk_of(i) * (CHUNK // 8) - 1, 0), 0))
    ba = pl.BlockSpec((CHUNK, BA_W), lambda i: (chunk_of(i), (QKV_W + DN_WIDTH) // BA_W))
    conv = pl.BlockSpec((8, QKV_W), lambda i: (0, 0))
    row = pl.BlockSpec((1, BA_W), lambda i: (0, 0))
    return chunk_of, cur, halo, ba, conv, row


def _dn_forward(pdn, conv_w8, a_row, dt_row):
    seq = pdn.shape[0]
    nc = seq // CHUNK
    chunk_of, cur, halo, ba, conv, row = _dn_specs(nc, reverse=False)

    def body(cur_ref, halo_ref, ba_ref, w_ref, a_ref, dt_ref, o_ref, st_ref, ext_ref, state_ref):
        i = pl.program_id(0)

        @pl.when(i == 0)
        def _():
            state_ref[...] = jnp.zeros_like(state_ref)

        halo_rows = jnp.where(i > 0, halo_ref[...], 0.0)
        c = _conv_taps(ext_ref, halo_rows, cur_ref[...], w_ref[...])
        qn, kn, v, beta_all, g_all = _dn_pre(c[:, :DN_WIDTH], c[:, DN_WIDTH:2 * DN_WIDTH], c[:, 2 * DN_WIDTH:],
                                             ba_ref[...], a_ref[...], dt_ref[...])
        for h in range(DN_HEADS):
            lanes = slice(h * DN_DIM, (h + 1) * DN_DIM)
            state = state_ref[h]
            st_ref[0, h] = state
            o, new_state = _dn_chunk(qn[:, lanes], kn[:, lanes], v[:, lanes], beta_all[:, h:h + 1],
                                     g_all[:, DN_HEADS + h:DN_HEADS + h + 1], state)
            o_ref[:, lanes] = o
            state_ref[h] = new_state

    return pl.pallas_call(
        body, name="dn_forward", grid=(nc,),
        in_specs=[cur, halo, ba, conv, row, row],
        out_specs=[pl.BlockSpec((CHUNK, DN_WIDTH), lambda i: (i, 0)),
                   pl.BlockSpec((1, DN_HEADS, DN_DIM, DN_DIM), lambda i: (i, 0, 0, 0))],
        out_shape=[jax.ShapeDtypeStruct((seq, DN_WIDTH), f32),
                   jax.ShapeDtypeStruct((nc, DN_HEADS, DN_DIM, DN_DIM), f32)],
        scratch_shapes=[pltpu.VMEM((CHUNK + 8, QKV_W), f32), pltpu.VMEM((DN_HEADS, DN_DIM, DN_DIM), f32)],
        compiler_params=_params(("arbitrary",)),
    )(pdn, pdn, pdn, conv_w8, a_row, dt_row)


def _dn_backward(pdn, conv_w8, a_row, dt_row, states, d_o):
    seq = pdn.shape[0]
    nc = seq // CHUNK
    chunk_of, cur, halo, ba, conv, row = _dn_specs(nc, reverse=True)

    def body(cur_ref, halo_ref, ba_ref, w_ref, a_ref, dt_ref, st_ref, do_ref, dconv_ref, dba_ref, drow_ref,
             ext_ref, dstate_ref):
        i = pl.program_id(0)

        @pl.when(i == 0)
        def _():
            dstate_ref[...] = jnp.zeros_like(dstate_ref)
            drow_ref[...] = jnp.zeros_like(drow_ref)

        halo_rows = jnp.where(chunk_of(i) > 0, halo_ref[...], 0.0)
        c = _conv_taps(ext_ref, halo_rows, cur_ref[...], w_ref[...])
        (qn, kn, v, beta_all, g_all), pre_vjp = jax.vjp(
            _dn_pre, c[:, :DN_WIDTH], c[:, DN_WIDTH:2 * DN_WIDTH], c[:, 2 * DN_WIDTH:], ba_ref[...], a_ref[...], dt_ref[...])
        lane = lax.broadcasted_iota(jnp.int32, (CHUNK, BA_W), 1)
        d_beta_all = jnp.zeros((CHUNK, BA_W), f32)
        d_g_all = jnp.zeros((CHUNK, BA_W), f32)
        dq, dk, dv = [], [], []
        for h in range(DN_HEADS):
            lanes = slice(h * DN_DIM, (h + 1) * DN_DIM)
            _, chunk_vjp = jax.vjp(_dn_chunk, qn[:, lanes], kn[:, lanes], v[:, lanes], beta_all[:, h:h + 1],
                                   g_all[:, DN_HEADS + h:DN_HEADS + h + 1], st_ref[0, h])
            dq_h, dk_h, dv_h, dbeta_h, dg_h, dstate = chunk_vjp((do_ref[:, lanes], dstate_ref[h]))
            dstate_ref[h] = dstate
            dq.append(dq_h)
            dk.append(dk_h)
            dv.append(dv_h)
            d_beta_all = d_beta_all + jnp.where(lane == h, dbeta_h, 0.0)
            d_g_all = d_g_all + jnp.where(lane == DN_HEADS + h, dg_h, 0.0)
        dcq, dck, dcv, dba, da_row, ddt_row = pre_vjp(
            (jnp.concatenate(dq, axis=1), jnp.concatenate(dk, axis=1), jnp.concatenate(dv, axis=1), d_beta_all, d_g_all))
        dconv_ref[:, :DN_WIDTH] = dcq
        dconv_ref[:, DN_WIDTH:2 * DN_WIDTH] = dck
        dconv_ref[:, 2 * DN_WIDTH:] = dcv
        dba_ref[...] = dba
        drow_ref[0:1, :] += da_row
        drow_ref[1:2, :] += ddt_row

    return pl.pallas_call(
        body, name="dn_backward", grid=(nc,),
        in_specs=[cur, halo, ba, conv, row, row,
                  pl.BlockSpec((1, DN_HEADS, DN_DIM, DN_DIM), lambda i: (chunk_of(i), 0, 0, 0)),
                  pl.BlockSpec((CHUNK, DN_WIDTH), lambda i: (chunk_of(i), 0))],
        out_specs=[pl.BlockSpec((CHUNK, QKV_W), lambda i: (chunk_of(i), 0)),
                   pl.BlockSpec((CHUNK, BA_W), lambda i: (chunk_of(i), 0)),
                   pl.BlockSpec((8, BA_W), lambda i: (0, 0))],
        out_shape=[jax.ShapeDtypeStruct((seq, QKV_W), f32), jax.ShapeDtypeStruct((seq, BA_W), f32),
                   jax.ShapeDtypeStruct((8, BA_W), f32)],
        scratch_shapes=[pltpu.VMEM((CHUNK + 8, QKV_W), f32), pltpu.VMEM((DN_HEADS, DN_DIM, DN_DIM), f32)],
        compiler_params=_params(("arbitrary",)),
    )(pdn, pdn, pdn, conv_w8, a_row, dt_row, states, d_o)


def _band_masks(block_index):
    qi = lax.broadcasted_iota(jnp.int32, (Q_BLOCK, Q_BLOCK), 0)
    kj = lax.broadcasted_iota(jnp.int32, (Q_BLOCK, Q_BLOCK), 1)
    return (kj >= qi) & (block_index > 0), kj <= qi


def _attn_forward(pat_view, dil):
    length = pat_view.shape[0]
    nb = length // Q_BLOCK
    scale = AT_DIM ** -0.5

    def body(q_ref, kp_ref, kc_ref, vp_ref, vc_ref, o_ref, lse_ref):
        mask_prev, mask_cur = _band_masks(pl.program_id(1))
        for h in range(AT_HEADS):
            lanes = slice(h * AT_DIM, (h + 1) * AT_DIM)
            q = q_ref[:, lanes]
            s_prev = jnp.where(mask_prev, _dot(q, kp_ref[:, lanes], NT, False) * scale, NEG)
            s_cur = jnp.where(mask_cur, _dot(q, kc_ref[:, lanes], NT, False) * scale, NEG)
            m = jnp.maximum(jnp.max(s_prev, axis=1, keepdims=True), jnp.max(s_cur, axis=1, keepdims=True))
            p_prev = jnp.exp(s_prev - m)
            p_cur = jnp.exp(s_cur - m)
            l = jnp.sum(p_prev, axis=1, keepdims=True) + jnp.sum(p_cur, axis=1, keepdims=True)
            o = _dot(p_prev, vp_ref[:, lanes], NN, False) + _dot(p_cur, vc_ref[:, lanes], NN, False)
            o_ref[:, lanes] = o / l
            lse_ref[:, lanes] = jnp.broadcast_to(m + jnp.log(l), (Q_BLOCK, AT_DIM))

    def blk(piece, prev):
        if prev:
            return pl.BlockSpec((Q_BLOCK, AT_WIDTH), lambda r, n: (jnp.maximum(n - 1, 0), 3 * r + piece))
        return pl.BlockSpec((Q_BLOCK, AT_WIDTH), lambda r, n: (n, 3 * r + piece))

    out = pl.BlockSpec((Q_BLOCK, AT_WIDTH), lambda r, n: (n, r))
    return pl.pallas_call(
        body, name=f"attn_forward_d{dil}", grid=(dil, nb),
        in_specs=[blk(0, False), blk(1, True), blk(1, False), blk(2, True), blk(2, False)],
        out_specs=[out, out],
        out_shape=[jax.ShapeDtypeStruct((length, dil * AT_WIDTH), f32)] * 2,
        compiler_params=_params(("parallel", "parallel")),
    )(pat_view, pat_view, pat_view, pat_view, pat_view)


def _attn_backward_q(pat_view, d_out, delta, lse, dil):
    length = pat_view.shape[0]
    nb = length // Q_BLOCK
    scale = AT_DIM ** -0.5

    def body(q_ref, kp_ref, kc_ref, vp_ref, vc_ref, do_ref, dl_ref, lse_ref, dq_ref):
        mask_prev, mask_cur = _band_masks(pl.program_id(1))
        for h in range(AT_HEADS):
            lanes = slice(h * AT_DIM, (h + 1) * AT_DIM)
            q = q_ref[:, lanes]
            do = do_ref[:, lanes]
            lse_h = lse_ref[:, h * AT_DIM:h * AT_DIM + 1]
            dl_h = dl_ref[:, h * AT_DIM:h * AT_DIM + 1]
            dq = jnp.zeros((Q_BLOCK, AT_DIM), f32)
            for mask, k_ref, v_ref in ((mask_prev, kp_ref, vp_ref), (mask_cur, kc_ref, vc_ref)):
                k = k_ref[:, lanes]
                s = _dot(q, k, NT, False) * scale
                p = jnp.exp(jnp.where(mask, s - lse_h, NEG))
                dp = _dot(do, v_ref[:, lanes], NT, False)
                dq = dq + _dot(p * (dp - dl_h), k, NN, False)
            dq_ref[:, lanes] = dq * scale

    def blk(piece, prev):
        if prev:
            return pl.BlockSpec((Q_BLOCK, AT_WIDTH), lambda r, n: (jnp.maximum(n - 1, 0), 3 * r + piece))
        return pl.BlockSpec((Q_BLOCK, AT_WIDTH), lambda r, n: (n, 3 * r + piece))

    one = pl.BlockSpec((Q_BLOCK, AT_WIDTH), lambda r, n: (n, r))
    return pl.pallas_call(
        body, name=f"attn_backward_q_d{dil}", grid=(dil, nb),
        in_specs=[blk(0, False), blk(1, True), blk(1, False), blk(2, True), blk(2, False), one, one, one],
        out_specs=one,
        out_shape=jax.ShapeDtypeStruct((length, dil * AT_WIDTH), f32),
        compiler_params=_params(("parallel", "parallel")),
    )(pat_view, pat_view, pat_view, pat_view, pat_view, d_out, delta, lse)


def _attn_backward_kv(pat_view, d_out, delta, lse, dil):
    length = pat_view.shape[0]
    nb = length // Q_BLOCK
    scale = AT_DIM ** -0.5

    def body(k_ref, v_ref, qa_ref, qb_ref, doa_ref, dob_ref, dla_ref, dlb_ref, lsea_ref, lseb_ref, dk_ref, dv_ref):
        j = pl.program_id(1)
        qi = lax.broadcasted_iota(jnp.int32, (Q_BLOCK, Q_BLOCK), 0)
        kj = lax.broadcasted_iota(jnp.int32, (Q_BLOCK, Q_BLOCK), 1)
        mask_same = kj <= qi
        mask_next = (kj >= qi) & (j + 1 < nb)
        for h in range(AT_HEADS):
            lanes = slice(h * AT_DIM, (h + 1) * AT_DIM)
            one = slice(h * AT_DIM, h * AT_DIM + 1)
            k = k_ref[:, lanes]
            v = v_ref[:, lanes]
            dk = jnp.zeros((Q_BLOCK, AT_DIM), f32)
            dv = jnp.zeros((Q_BLOCK, AT_DIM), f32)
            for mask, q_ref, do_ref, dl_ref, lse_ref in ((mask_same, qa_ref, doa_ref, dla_ref, lsea_ref),
                                                         (mask_next, qb_ref, dob_ref, dlb_ref, lseb_ref)):
                q = q_ref[:, lanes]
                do = do_ref[:, lanes]
                s = _dot(q, k, NT, False) * scale
                p = jnp.exp(jnp.where(mask, s - lse_ref[:, one], NEG))
                dv = dv + _dot(p, do, TN, False)
                dp = _dot(do, v, NT, False)
                dk = dk + _dot(p * (dp - dl_ref[:, one]), q, TN, False)
            dk_ref[:, lanes] = dk * scale
            dv_ref[:, lanes] = dv

    def kv(piece):
        return pl.BlockSpec((Q_BLOCK, AT_WIDTH), lambda r, n: (n, 3 * r + piece))

    q_same = pl.BlockSpec((Q_BLOCK, AT_WIDTH), lambda r, n: (n, 3 * r))
    q_next = pl.BlockSpec((Q_BLOCK, AT_WIDTH), lambda r, n: (jnp.minimum(n + 1, nb - 1), 3 * r))
    same = pl.BlockSpec((Q_BLOCK, AT_WIDTH), lambda r, n: (n, r))
    nxt = pl.BlockSpec((Q_BLOCK, AT_WIDTH), lambda r, n: (jnp.minimum(n + 1, nb - 1), r))
    return pl.pallas_call(
        body, name=f"attn_backward_kv_d{dil}", grid=(dil, nb),
        in_specs=[kv(1), kv(2), q_same, q_next, same, nxt, same, nxt, same, nxt],
        out_specs=[same, same],
        out_shape=[jax.ShapeDtypeStruct((length, dil * AT_WIDTH), f32)] * 2,
        compiler_params=_params(("parallel", "parallel")),
    )(pat_view, pat_view, pat_view, pat_view, d_out, d_out, delta, delta, lse, lse)


def _to_view(a, dil):
    return a if dil == 1 else a.reshape(a.shape[0] // dil, dil * a.shape[1])


def _from_view(a, dil):
    return a if dil == 1 else a.reshape(a.shape[0] * dil, a.shape[1] // dil)


def _gated_norms(o_dn, z_dn, o_at, z_at, dn_w, at_w):
    ms_dn = MX.nn(o_dn * o_dn, _group_ones(DN_WIDTH, DN_DIM)) * (1.0 / DN_DIM)
    a = o_dn * lax.rsqrt(ms_dn + EPS) * dn_w * _silu(z_dn)
    ms_at = MX.nn(o_at * o_at, _group_ones(AT_WIDTH, AT_DIM)) * (1.0 / AT_DIM)
    b = o_at * lax.rsqrt(ms_at + EPS) * at_w * _silu(z_at)
    return a, b


def _residual_loss(x, mix, gate, fin_w, target):
    x2 = x + gate * mix
    y = x2 * lax.rsqrt(jnp.mean(x2 * x2, axis=-1, keepdims=True) + EPS) * fin_w
    err = y - target
    per_token = jnp.sum(err * err, axis=1, keepdims=True) * (1.0 / D_MODEL)
    return 0.5 * jnp.sum(per_token, axis=0, keepdims=True)


TAIL_ROWS = 8


def _tail(x, target, o_dn, pdn, outs, lses, z_at, gate, w_out, dn_w_row, at_w_row, fin_w):
    seq = x.shape[0]
    tm = 256

    def body(x_ref, t_ref, odn_ref, zdn_ref, o1_ref, o4_ref, o16_ref, l1_ref, l4_ref, l16_ref, zat_ref, gate_ref,
             wt_ref, wb_ref, dnw_ref, atw_ref, fw_ref,
             dx2_ref, dodn_ref, dzdn_ref, doat_ref, delta_ref, lse_ref, dzat_ref, gwt_ref, gwb_ref, rows_ref):
        @pl.when(pl.program_id(0) == 0)
        def _():
            gwt_ref[...] = jnp.zeros_like(gwt_ref)
            gwb_ref[...] = jnp.zeros_like(gwb_ref)
            rows_ref[...] = jnp.zeros_like(rows_ref)

        l1, l4, l16 = l1_ref[...], l4_ref[...], l16_ref[...]
        top = jnp.maximum(jnp.maximum(l1, l4), l16)
        e1, e4, e16 = jnp.exp(l1 - top), jnp.exp(l4 - top), jnp.exp(l16 - top)
        den = e1 + e4 + e16
        o_at = (e1 * o1_ref[...] + e4 * o4_ref[...] + e16 * o16_ref[...]) / den
        lse_ref[...] = top + jnp.log(den)

        (a, b), norms_vjp = jax.vjp(_gated_norms, odn_ref[...], zdn_ref[...], o_at, zat_ref[...], dnw_ref[...], atw_ref[...])
        a, b = a.astype(bf16), b.astype(bf16)
        mix = jnp.dot(a, wt_ref[...], preferred_element_type=f32) + jnp.dot(b, wb_ref[...], preferred_element_type=f32)
        loss, loss_vjp = jax.vjp(_residual_loss, x_ref[...], mix, gate_ref[...], fw_ref[...], t_ref[...])
        dx2, dmix, dgate, dfw, _ = loss_vjp(jnp.ones((1, 1), f32))
        dmix = dmix.astype(bf16)
        dwt = lax.dot_general(a, dmix, (TN, ((), ())), preferred_element_type=f32)
        dwb = lax.dot_general(b, dmix, (TN, ((), ())), preferred_element_type=f32)
        da = lax.dot_general(dmix, wt_ref[...], (NT, ((), ())), preferred_element_type=f32)
        db = lax.dot_general(dmix, wb_ref[...], (NT, ((), ())), preferred_element_type=f32)
        dodn, dzdn, doat, dzat, ddnw, datw = norms_vjp((da, db))
        dx2_ref[...] = dx2
        dodn_ref[...] = dodn
        dzdn_ref[...] = dzdn
        doat_ref[...] = doat
        dzat_ref[...] = dzat
        delta_ref[...] = MX.nn(doat * o_at, _group_ones(AT_WIDTH, AT_DIM))
        gwt_ref[...] += dwt
        gwb_ref[...] += dwb
        rows_ref[0:1, :] += jnp.broadcast_to(loss, (1, D_MODEL))
        rows_ref[1:2, :] += dgate
        rows_ref[2:3, :] += dfw
        rows_ref[3:4, 0:DN_WIDTH] += ddnw
        rows_ref[4:5, 0:AT_WIDTH] += datw

    tile = lambda w: pl.BlockSpec((tm, w), lambda i: (i, 0))
    row = lambda w: pl.BlockSpec((1, w), lambda i: (0, 0))
    half_w = pl.BlockSpec((DN_WIDTH, D_MODEL), lambda i: (0, 0))
    sds = lambda w: jax.ShapeDtypeStruct((seq, w), f32)
    return pl.pallas_call(
        body, name="tail", grid=(seq // tm,),
        in_specs=[tile(D_MODEL), tile(D_MODEL), tile(DN_WIDTH),
                  pl.BlockSpec((tm, DN_WIDTH), lambda i: (i, QKV_W // DN_WIDTH)),
                  tile(AT_WIDTH), tile(AT_WIDTH), tile(AT_WIDTH), tile(AT_WIDTH), tile(AT_WIDTH), tile(AT_WIDTH),
                  tile(AT_WIDTH), row(D_MODEL), half_w, pl.BlockSpec((AT_WIDTH, D_MODEL), lambda i: (1, 0)),
                  row(DN_WIDTH), row(AT_WIDTH), row(D_MODEL)],
        out_specs=[tile(D_MODEL), tile(DN_WIDTH), tile(DN_WIDTH), tile(AT_WIDTH), tile(AT_WIDTH), tile(AT_WIDTH),
                   tile(AT_WIDTH), half_w, half_w, pl.BlockSpec((TAIL_ROWS, D_MODEL), lambda i: (0, 0))],
        out_shape=[sds(D_MODEL), sds(DN_WIDTH), sds(DN_WIDTH), sds(AT_WIDTH), sds(AT_WIDTH), sds(AT_WIDTH),
                   sds(AT_WIDTH), jax.ShapeDtypeStruct((DN_WIDTH, D_MODEL), f32),
                   jax.ShapeDtypeStruct((AT_WIDTH, D_MODEL), f32), jax.ShapeDtypeStruct((TAIL_ROWS, D_MODEL), f32)],
        compiler_params=_params(("arbitrary",)),
    )(x, target, o_dn, pdn, outs[0], outs[1], outs[2], lses[0], lses[1], lses[2], z_at, gate, w_out, w_out,
      dn_w_row, at_w_row, fin_w)


PRE_ROWS = 8


def _pre_backward(x, dx2, pos_col, invf_row, sign_row, norm_w, scale, shift, w_cat, conv_w8, pdn, d_conv, d_zdn, d_ba,
                  d_q, d_k, d_v, d_zat):
    seq = x.shape[0]
    tm = 256
    last = seq // tm - 1

    def body(x_ref, dx2_ref, pos_ref, invf_ref, sign_ref, nw_ref, sc_ref, sh_ref, w_ref, cw_ref,
             pre_ref, prehalo_ref, dc_ref, dchalo_ref, dz_ref, dba_ref,
             dq1_ref, dq4_ref, dq16_ref, dk1_ref, dk4_ref, dk16_ref, dv1_ref, dv4_ref, dv16_ref, dzat_ref,
             gx_ref, dproj_ref, rows_ref, crow_ref, ext_ref):
        i = pl.program_id(0)

        @pl.when(i == 0)
        def _():
            rows_ref[...] = jnp.zeros_like(rows_ref)
            crow_ref[...] = jnp.zeros_like(crow_ref)

        dc = dc_ref[...]
        ext_ref[0:tm, :] = dc
        ext_ref[tm:tm + 8, :] = jnp.where(i < last, dchalo_ref[...], 0.0)
        w8 = cw_ref[...]
        d_pre = ext_ref[pl.ds(3, tm), :] * w8[0:1, :]
        for j in range(1, CONV_K):
            d_pre = d_pre + ext_ref[pl.ds(3 - j, tm), :] * w8[j:j + 1, :]
        ext_ref[0:8, :] = jnp.where(i > 0, prehalo_ref[...], 0.0)
        ext_ref[8:8 + tm, :] = pre_ref[...]
        for j in range(CONV_K):
            crow_ref[j:j + 1, :] += jnp.sum(dc * ext_ref[pl.ds(5 + j, tm), :], axis=0, keepdims=True)

        cos_t, sin_t = _rope_tables(pos_ref[...], invf_ref[...], sign_ref[...])
        dq = dq1_ref[...] + dq4_ref[...] + dq16_ref[...]
        dk = dk1_ref[...] + dk4_ref[...] + dk16_ref[...]
        dq = dq * cos_t + _rope_partner(dq * sin_t)
        dk = dk * cos_t + _rope_partner(dk * sin_t)
        dv = dv1_ref[...] + dv4_ref[...] + dv16_ref[...]
        d_proj = jnp.concatenate([d_pre, dz_ref[...], dba_ref[...], dq, dk, dv, dzat_ref[...]], axis=1).astype(bf16)
        dproj_ref[...] = d_proj
        dh = lax.dot_general(d_proj, w_ref[...], (NT, ((), ())), preferred_element_type=f32)
        _, vjp = jax.vjp(_hnorm, x_ref[...], nw_ref[...], sc_ref[...], sh_ref[...])
        dx, dnw, dsc, dsh = vjp(dh)
        gx_ref[...] = dx + dx2_ref[...]
        rows_ref[0:1, :] += dnw
        rows_ref[1:2, :] += dsc
        rows_ref[2:3, :] += dsh

    tile = lambda w: pl.BlockSpec((tm, w), lambda i: (i, 0))
    row = lambda w: pl.BlockSpec((1, w), lambda i: (0, 0))
    step8 = tm // 8
    return pl.pallas_call(
        body, name="pre_backward", grid=(seq // tm,),
        in_specs=[tile(D_MODEL), tile(D_MODEL), tile(1), row(128), row(128), row(D_MODEL), row(D_MODEL), row(D_MODEL),
                  pl.BlockSpec((D_MODEL, CAT_W), lambda i: (0, 0)), pl.BlockSpec((8, QKV_W), lambda i: (0, 0)),
                  tile(QKV_W), pl.BlockSpec((8, QKV_W), lambda i: (jnp.maximum(i * step8 - 1, 0), 0)),
                  tile(QKV_W), pl.BlockSpec((8, QKV_W), lambda i: (jnp.minimum((i + 1) * step8, seq // 8 - 1), 0)),
                  tile(DN_WIDTH), tile(BA_W)] + [tile(AT_WIDTH)] * 10,
        out_specs=[tile(D_MODEL), tile(CAT_W), pl.BlockSpec((PRE_ROWS, D_MODEL), lambda i: (0, 0)),
                   pl.BlockSpec((8, QKV_W), lambda i: (0, 0))],
        out_shape=[jax.ShapeDtypeStruct((seq, D_MODEL), f32), jax.ShapeDtypeStruct((seq, CAT_W), bf16),
                   jax.ShapeDtypeStruct((PRE_ROWS, D_MODEL), f32), jax.ShapeDtypeStruct((8, QKV_W), f32)],
        scratch_shapes=[pltpu.VMEM((tm + 8, QKV_W), f32)],
        compiler_params=_params(("arbitrary",)),
    )(x, dx2, pos_col, invf_row, sign_row, norm_w, scale, shift, w_cat, conv_w8, pdn, pdn, d_conv, d_conv, d_zdn, d_ba,
      d_q[0], d_q[1], d_q[2], d_k[0], d_k[1], d_k[2], d_v[0], d_v[1], d_v[2], d_zat)


def _weight_grad(h, d_proj):
    seq = h.shape[0]
    tk, tn = 1024, 384
    n_k = seq // tk

    def body(h_ref, d_ref, o_ref):
        @pl.when(pl.program_id(1) == 0)
        def _():
            o_ref[...] = jnp.zeros_like(o_ref)

        o_ref[...] += lax.dot_general(h_ref[...], d_ref[...], (TN, ((), ())), preferred_element_type=f32)

    return pl.pallas_call(
        body, name="weight_grad", grid=(CAT_W // tn, n_k),
        in_specs=[pl.BlockSpec((tk, D_MODEL), lambda n, k: (k, 0)), pl.BlockSpec((tk, tn), lambda n, k: (k, n))],
        out_specs=pl.BlockSpec((D_MODEL, tn), lambda n, k: (0, n)),
        out_shape=jax.ShapeDtypeStruct((D_MODEL, CAT_W), f32),
        compiler_params=_params(("parallel", "arbitrary")),
    )(h, d_proj)


def _adamw(w, g, m, v):
    m = ADAM_B1 * m + (1.0 - ADAM_B1) * g
    v = ADAM_B2 * v + (1.0 - ADAM_B2) * (g * g)
    m_hat = m / (1.0 - ADAM_B1 ** ADAM_STEP)
    v_hat = v / (1.0 - ADAM_B2 ** ADAM_STEP)
    delta = -ADAM_LR * (m_hat / (jnp.sqrt(v_hat) + ADAM_EPS) + ADAM_WD * w)
    return delta, m, v


def _mod_shard(c_all, w_mod_s, b_mod_s):
    def body(c_ref, w_ref, b_ref, o_ref):
        o_ref[...] = _dot(_silu(c_ref[...]), w_ref[...], NN, True) + b_ref[...]

    return pl.pallas_call(body, name="mod_shard", out_shape=jax.ShapeDtypeStruct((N_DEV, w_mod_s.shape[1]), f32),
                          compiler_params=_params())(c_all, w_mod_s, b_mod_s)


def _mod_update(c_all, d_mod_s, w, m, v):
    def body(c_ref, d_ref, w_ref, m_ref, v_ref, g_ref, dw_ref, nm_ref, nv_ref):
        g = _dot(_silu(c_ref[...]), d_ref[...], TN, True)
        g_ref[...] = g
        dw_ref[...], nm_ref[...], nv_ref[...] = _adamw(w_ref[...], g, m_ref[...], v_ref[...])

    return pl.pallas_call(body, name="mod_update", out_shape=[jax.ShapeDtypeStruct(w.shape, f32)] * 4,
                          compiler_params=_params())(c_all, d_mod_s, w, m, v)


def _adamw_rows(w, g, m, v, name):
    rows, width = w.shape
    tr = PACK_CHUNK if rows % PACK_CHUNK == 0 else rows

    def body(w_ref, g_ref, m_ref, v_ref, dw_ref, nm_ref, nv_ref):
        dw_ref[...], nm_ref[...], nv_ref[...] = _adamw(w_ref[...], g_ref[...], m_ref[...], v_ref[...])

    spec = pl.BlockSpec((tr, width), lambda i: (i, 0))
    return pl.pallas_call(body, name=name, grid=(rows // tr,), in_specs=[spec] * 4, out_specs=[spec] * 3,
                          out_shape=[jax.ShapeDtypeStruct(w.shape, f32)] * 3,
                          compiler_params=_params(("parallel",)))(w, g, m, v)


SM_NORM, SM_FIN, SM_DN, SM_AT, SM_A, SM_DT, SM_LOSS, SM_MOD = 0, 1024, 2048, 2560, 3072, 3200, 3328, 3456
SM_W = SM_MOD + 3 * D_MODEL
RS_NORM, RS_FIN, RS_DN, RS_AT, RS_A, RS_DT, RS_BMOD = 0, 1024, 2048, 2176, 2304, 2432, 2560
RS_W = RS_BMOD + 3 * D_MODEL


def _small_update(gathered, w, m, v):
    def body(g_ref, w_ref, m_ref, v_ref, grad_ref, dw_ref, nm_ref, nv_ref, loss_ref, dmod_ref):
        total = g_ref[0:1, :]
        for dev in range(1, N_DEV):
            total = total + g_ref[8 * dev:8 * dev + 1, :]
        for dev in range(N_DEV):
            dmod_ref[dev:dev + 1, :] = g_ref[8 * dev:8 * dev + 1, SM_MOD:SM_W]
        dn = total[:, SM_DN:SM_DN + DN_DIM]
        for h in range(1, DN_HEADS):
            dn = dn + total[:, SM_DN + h * DN_DIM:SM_DN + (h + 1) * DN_DIM]
        at = total[:, SM_AT:SM_AT + AT_DIM]
        for h in range(1, AT_HEADS):
            at = at + total[:, SM_AT + h * AT_DIM:SM_AT + (h + 1) * AT_DIM]
        grad_ref[:, RS_NORM:RS_FIN] = total[:, SM_NORM:SM_FIN]
        grad_ref[:, RS_FIN:RS_DN] = total[:, SM_FIN:SM_DN]
        grad_ref[:, RS_DN:RS_AT] = dn
        grad_ref[:, RS_AT:RS_A] = jnp.zeros((1, 128), f32)
        grad_ref[:, RS_AT:RS_AT + AT_DIM] = at
        grad_ref[:, RS_A:RS_DT] = total[:, SM_A:SM_DT]
        grad_ref[:, RS_DT:RS_BMOD] = total[:, SM_DT:SM_LOSS]
        grad_ref[:, RS_BMOD:RS_W] = total[:, SM_MOD:SM_W]
        loss_ref[...] = total[:, SM_LOSS:SM_MOD]
        dw_ref[...], nm_ref[...], nv_ref[...] = _adamw(w_ref[...], grad_ref[...], m_ref[...], v_ref[...])

    row = jax.ShapeDtypeStruct((1, RS_W), f32)
    return pl.pallas_call(
        body, name="small_update",
        out_shape=[row, row, row, row, jax.ShapeDtypeStruct((1, 128), f32), jax.ShapeDtypeStruct((N_DEV, 3 * D_MODEL), f32)],
        compiler_params=_params())(gathered, w, m, v)


def _all_gather_rows(block, name):
    m_per, n = block.shape

    def body(x_ref, out_ref, send_sems, recv_sems, local_sem):
        x, y, c = lax.axis_index("x"), lax.axis_index("y"), lax.axis_index("c")
        me, sibling = (x, y, c), (x, y, 1 - c)
        chips = [(1 - x, y), (x, 1 - y), (1 - x, 1 - y)]

        def rows(px, py, pc):
            return out_ref.at[pl.ds((4 * px + 2 * py + pc) * m_per, m_per), :]

        def copy(k, blk, to, src=None):
            return pltpu.make_async_remote_copy(
                src_ref=rows(*blk) if src is None else src, dst_ref=rows(*blk),
                send_sem=send_sems.at[k], recv_sem=recv_sems.at[k], device_id=to, device_id_type=MESH)

        mine = pltpu.make_async_copy(x_ref, rows(*me), local_sem)
        mine.start()
        first = [copy(0, me, sibling, src=x_ref)]
        first += [copy(1 + j, me, (*chip, c), src=x_ref) for j, chip in enumerate(chips)]
        for cp in first:
            cp.start()
        passed = [copy(4 + j, (*chip, c), sibling) for j, chip in enumerate(chips)]
        for j, chip in enumerate(chips):
            copy(1 + j, (*chip, c), me).wait_recv()
            passed[j].start()
        copy(0, sibling, me).wait_recv()
        for j, chip in enumerate(chips):
            copy(4 + j, (*chip, 1 - c), me).wait_recv()
        for cp in first + passed:
            cp.wait_send()
        mine.wait()

    return pl.pallas_call(
        body, name=name,
        out_shape=jax.ShapeDtypeStruct((N_DEV * m_per, n), block.dtype),
        in_specs=[pl.BlockSpec(memory_space=pltpu.VMEM)],
        out_specs=pl.BlockSpec(memory_space=pltpu.VMEM),
        scratch_shapes=[pltpu.SemaphoreType.DMA((7,)), pltpu.SemaphoreType.DMA((7,)), pltpu.SemaphoreType.DMA],
        compiler_params=pltpu.CompilerParams(vmem_limit_bytes=VMEM_LIMIT),
    )(block)


def _gather_weight_shards(packed):
    rows, width = packed.shape

    def body(src_ref, out_ref, send_sems, recv_sems, local_sem):
        x, y, c = lax.axis_index("x"), lax.axis_index("y"), lax.axis_index("c")
        chips = [(1 - x, y), (x, 1 - y), (1 - x, 1 - y)]

        def copy(k, owner, to):
            slot = out_ref.at[2 * owner[0] + owner[1]]
            return pltpu.make_async_remote_copy(
                src_ref=src_ref, dst_ref=slot, send_sem=send_sems.at[k], recv_sem=recv_sems.at[k],
                device_id=(*to, c), device_id_type=MESH)

        mine = pltpu.make_async_copy(src_ref, out_ref.at[2 * x + y], local_sem)
        mine.start()
        sends = [copy(k, (x, y), chip) for k, chip in enumerate(chips)]
        for cp in sends:
            cp.start()
        for k, chip in enumerate(chips):
            copy(k, chip, (x, y)).wait_recv()
        for cp in sends:
            cp.wait_send()
        mine.wait()

    return pl.pallas_call(
        body, name="gather_weight_shards",
        out_shape=jax.ShapeDtypeStruct((N_CHIPS, rows, width), packed.dtype),
        in_specs=[pl.BlockSpec(memory_space=pl.ANY)],
        out_specs=pl.BlockSpec(memory_space=pl.ANY),
        scratch_shapes=[pltpu.SemaphoreType.DMA((3,)), pltpu.SemaphoreType.DMA((3,)), pltpu.SemaphoreType.DMA],
    )(packed)


def _reduce_weight_grads(grads):
    _, rows, width = grads.shape
    n_chunks = rows // PACK_CHUNK

    def body(g_ref, out_ref, own_ref, land_ref, part_ref, sib_ref, send_sems, recv_sems, local_sem):
        x, y, c = lax.axis_index("x"), lax.axis_index("y"), lax.axis_index("c")
        chips = [(1 - x, y), (x, 1 - y), (1 - x, 1 - y)]

        def copy(k, chip):
            return pltpu.make_async_remote_copy(
                src_ref=g_ref.at[2 * chip[0] + chip[1]], dst_ref=land_ref.at[k],
                send_sem=send_sems.at[k], recv_sem=recv_sems.at[k], device_id=(*chip, c), device_id_type=MESH)

        mine = pltpu.make_async_copy(g_ref.at[2 * x + y], own_ref, local_sem)
        mine.start()
        sends = [copy(k, chip) for k, chip in enumerate(chips)]
        for cp in sends:
            cp.start()
        mine.wait()
        for cp in sends:
            cp.wait_recv()

        def add_landed(i, carry):
            r = pl.ds(pl.multiple_of(i * PACK_CHUNK, 8), PACK_CHUNK)
            part_ref[r, :] = ((own_ref[r, :] + land_ref[0, r, :]) + land_ref[1, r, :]) + land_ref[2, r, :]
            return carry

        lax.fori_loop(0, n_chunks, add_landed, 0)
        swap = pltpu.make_async_remote_copy(
            src_ref=part_ref, dst_ref=sib_ref, send_sem=send_sems.at[3], recv_sem=recv_sems.at[3],
            device_id=(x, y, 1 - c), device_id_type=MESH)
        swap.start()
        swap.wait_recv()

        def add_sibling(i, carry):
            r = pl.ds(pl.multiple_of(i * PACK_CHUNK, 8), PACK_CHUNK)
            out_ref[r, :] = part_ref[r, :] + sib_ref[r, :]
            return carry

        lax.fori_loop(0, n_chunks, add_sibling, 0)
        swap.wait_send()
        for cp in sends:
            cp.wait_send()

    buf = pltpu.VMEM((rows, width), f32)
    return pl.pallas_call(
        body, name="reduce_weight_grads",
        out_shape=jax.ShapeDtypeStruct((rows, width), f32),
        in_specs=[pl.BlockSpec(memory_space=pl.ANY)],
        out_specs=pl.BlockSpec(memory_space=pltpu.VMEM),
        scratch_shapes=[buf, pltpu.VMEM((3, rows, width), f32), buf, buf,
                        pltpu.SemaphoreType.DMA((4,)), pltpu.SemaphoreType.DMA((4,)), pltpu.SemaphoreType.DMA],
        compiler_params=pltpu.CompilerParams(vmem_limit_bytes=VMEM_LIMIT),
    )(grads)


def _pack_shards(w_in_s, w_out_s, conv_s):
    lead = w_in_s.shape[:-2]
    conv_rows = jnp.pad(conv_s.reshape(*lead, CONV_K * 384), [(0, 0)] * len(lead) + [(0, 8 * D_MODEL - CONV_K * 384)])
    return jnp.concatenate([
        w_in_s.reshape(*lead, SHARD_IN, D_MODEL), w_out_s, conv_rows.reshape(*lead, 8, D_MODEL),
        jnp.zeros((*lead, PACK_ROWS - SHARD_IN - 256 - 8, D_MODEL), f32)], axis=-2)


def _unpack_shards(packed):
    lead = packed.shape[:-2]
    w_in_s = packed[..., :SHARD_IN, :].reshape(*lead, D_MODEL, SHARD_IN)
    w_out_s = packed[..., SHARD_IN:SHARD_IN + 256, :]
    conv_s = packed[..., SHARD_IN + 256:SHARD_IN + 264, :].reshape(*lead, 8 * D_MODEL)[..., :CONV_K * 384]
    return w_in_s, w_out_s, conv_s.reshape(*lead, CONV_K, 384)


def _local_step(x, target, pos_col, mod_row, norm_w, w_in, conv_w, a_log, dt_bias, dn_norm_w, at_norm_w, w_out, fin_w):
    shift, scale, gate = mod_row[:, :D_MODEL], mod_row[:, D_MODEL:2 * D_MODEL], mod_row[:, 2 * D_MODEL:]
    half = AT_DIM // 2
    lane = jnp.arange(128)
    inv_freq = ROPE_THETA ** (-jnp.arange(half, dtype=f32) / half)
    invf_row = inv_freq[lane % half].reshape(1, 128)
    sign_row = jnp.where((lane % AT_DIM) < half, -1.0, 1.0).astype(f32).reshape(1, 128)
    ba_w = jnp.pad(w_in[:, 2 * D_MODEL:2 * D_MODEL + 2 * DN_HEADS], ((0, 0), (0, BA_W - 2 * DN_HEADS)))
    w_cat = jnp.concatenate([w_in[:, :2 * D_MODEL], ba_w, w_in[:, 2 * D_MODEL + 2 * DN_HEADS:]], axis=1).astype(bf16)
    conv_w8 = jnp.pad(conv_w, ((0, 8 - CONV_K), (0, 0)))
    a_row = jnp.pad(a_log.reshape(1, DN_HEADS), ((0, 0), (DN_HEADS, BA_W - 2 * DN_HEADS)))
    dt_row = jnp.pad(dt_bias.reshape(1, DN_HEADS), ((0, 0), (DN_HEADS, BA_W - 2 * DN_HEADS)))
    dn_w_row = jnp.tile(dn_norm_w.reshape(1, DN_DIM), (1, DN_HEADS))
    at_w_row = jnp.tile(at_norm_w.reshape(1, AT_DIM), (1, AT_HEADS))
    norm_row = norm_w.reshape(1, D_MODEL)
    fin_row = fin_w.reshape(1, D_MODEL)
    w_out_bf = w_out.astype(bf16)

    pdn, pat, z_at, h = _pre_proj(x, pos_col, invf_row, sign_row, norm_row, scale, shift, w_cat)
    o_dn, states = _dn_forward(pdn, conv_w8, a_row, dt_row)
    views = [_to_view(pat, d) for d in PATTERN_DILATIONS]
    outs, lses = [], []
    for d, view in zip(PATTERN_DILATIONS, views):
        o, lse = _attn_forward(view, d)
        outs.append(_from_view(o, d))
        lses.append(_from_view(lse, d))
    (dx2, d_odn, d_zdn, d_oat, delta, lse_all, d_zat, g_wtop, g_wbot, tail_rows) = _tail(
        x, target, o_dn, pdn, outs, lses, z_at, gate, w_out_bf, dn_w_row, at_w_row, fin_row)
    d_q, d_k, d_v = [], [], []
    for d, view in zip(PATTERN_DILATIONS, views):
        args = (_to_view(d_oat, d), _to_view(delta, d), _to_view(lse_all, d))
        d_q.append(_from_view(_attn_backward_q(view, *args, d), d))
        dk, dv = _attn_backward_kv(view, *args, d)
        d_k.append(_from_view(dk, d))
        d_v.append(_from_view(dv, d))
    d_conv, d_ba, dn_rows = _dn_backward(pdn, conv_w8, a_row, dt_row, states, d_odn)
    grad_x, d_proj, pre_rows, conv_rows = _pre_backward(
        x, dx2, pos_col, invf_row, sign_row, norm_row, scale, shift, w_cat, conv_w8, pdn, d_conv, d_zdn, d_ba,
        d_q, d_k, d_v, d_zat)
    g_cat = _weight_grad(h, d_proj)
    g_w_in = jnp.concatenate([g_cat[:, :2 * D_MODEL], g_cat[:, 2 * D_MODEL:2 * D_MODEL + 2 * DN_HEADS],
                              g_cat[:, 2 * D_MODEL + BA_W:]], axis=1)
    g_w_out = jnp.concatenate([g_wtop, g_wbot], axis=0)
    g_conv = conv_rows[:CONV_K]
    small = jnp.concatenate([
        pre_rows[0:1], tail_rows[2:3], tail_rows[3:4, :DN_WIDTH], tail_rows[4:5, :AT_WIDTH],
        dn_rows[0:1], dn_rows[1:2], tail_rows[0:1, :128],
        pre_rows[2:3], pre_rows[1:2], tail_rows[1:2]], axis=1)
    return grad_x, g_w_in, g_w_out, g_conv, small


def kernel(x, c, positions, w_mod, b_mod, norm_w, w_in, conv_w, a_log, dt_bias, dn_norm_w, at_norm_w, w_out, final_norm_w, loss_target, m_w_mod, m_b_mod, m_norm_w, m_w_in, m_conv_w, m_a_log, m_dt_bias, m_dn_norm_w, m_at_norm_w, m_w_out, m_final_norm_w, v_w_mod, v_b_mod, v_norm_w, v_w_in, v_conv_w, v_a_log, v_dt_bias, v_dn_norm_w, v_at_norm_w, v_w_out, v_final_norm_w):
    seq = x.shape[1]
    ax, ay, ac = lax.axis_index("x"), lax.axis_index("y"), lax.axis_index("c")
    me = 4 * ax + 2 * ay + ac
    chip = 2 * ax + ay

    c_all = _all_gather_rows(jnp.pad(c, ((0, 7), (0, 0))), "gather_c").reshape(N_DEV, 8, D_MODEL)[:, 0]
    b_mod_s = lax.dynamic_slice_in_dim(b_mod, chip * 768, 768, axis=1)
    mod_part = _mod_shard(c_all, w_mod[0], b_mod_s)
    mod_all = _all_gather_rows(mod_part, "gather_mod").reshape(N_CHIPS, 2, N_DEV, 768)[:, 0]
    mod_row = lax.dynamic_index_in_dim(mod_all, me, axis=1, keepdims=False).reshape(1, 3 * D_MODEL)

    shards = _gather_weight_shards(_pack_shards(w_in[0], w_out[0], conv_w[0]))
    w_in_s, w_out_s, conv_s = _unpack_shards(shards)
    w_in_full = jnp.transpose(w_in_s, (1, 0, 2)).reshape(D_MODEL, IN_COLS)
    w_out_full = w_out_s.reshape(D_MODEL, D_MODEL)
    conv_full = jnp.transpose(conv_s, (1, 0, 2)).reshape(CONV_K, QKV_W)

    grad_x, g_w_in, g_w_out, g_conv, small = _local_step(
        x[0], loss_target[0], positions.reshape(seq, 1), mod_row, norm_w, w_in_full, conv_full, a_log, dt_bias,
        dn_norm_w, at_norm_w, w_out_full, final_norm_w)

    g_pack = _pack_shards(jnp.transpose(g_w_in.reshape(D_MODEL, N_CHIPS, SHARD_IN), (1, 0, 2)),
                          g_w_out.reshape(N_CHIPS, 256, D_MODEL),
                          jnp.transpose(g_conv.reshape(CONV_K, N_CHIPS, 384), (1, 0, 2)))
    g_mine = _reduce_weight_grads(g_pack)
    d_pack, m_pack, v_pack = _adamw_rows(_pack_shards(w_in[0], w_out[0], conv_w[0]), g_mine,
                                         _pack_shards(m_w_in[0], m_w_out[0], m_conv_w[0]),
                                         _pack_shards(v_w_in[0], v_w_out[0], v_conv_w[0]), "adamw_packed")
    grad_w_in, grad_w_out, grad_conv_w = _unpack_shards(g_mine)
    delta_w_in, delta_w_out, delta_conv_w = _unpack_shards(d_pack)
    new_m_w_in, new_m_w_out, new_m_conv_w = _unpack_shards(m_pack)
    new_v_w_in, new_v_w_out, new_v_conv_w = _unpack_shards(v_pack)

    gathered = _all_gather_rows(jnp.pad(small, ((0, 7), (0, 0))), "gather_small")

    def small_row(norm, fin, dn, at, a, dt, bmod):
        z = lambda n: jnp.zeros((1, n), f32)
        return jnp.concatenate([norm.reshape(1, -1), fin.reshape(1, -1), dn.reshape(1, -1), at.reshape(1, -1), z(64),
                                z(4), a.reshape(1, -1), z(120), z(4), dt.reshape(1, -1), z(120), bmod.reshape(1, -1)], axis=1)

    g_small, d_small, m_small, v_small, loss_row, d_mod_all = _small_update(
        gathered,
        small_row(norm_w, final_norm_w, dn_norm_w, at_norm_w, a_log, dt_bias, b_mod),
        small_row(m_norm_w, m_final_norm_w, m_dn_norm_w, m_at_norm_w, m_a_log, m_dt_bias, m_b_mod),
        small_row(v_norm_w, v_final_norm_w, v_dn_norm_w, v_at_norm_w, v_a_log, v_dt_bias, v_b_mod))

    def split_small(r):
        return (r[:, RS_BMOD:RS_W], r[:, RS_NORM:RS_FIN], r[:, RS_A + DN_HEADS:RS_A + 2 * DN_HEADS],
                r[:, RS_DT + DN_HEADS:RS_DT + 2 * DN_HEADS], r[:, RS_DN:RS_AT], r[:, RS_AT:RS_AT + AT_DIM],
                r[0, RS_FIN:RS_DN])

    d_mod_s = lax.dynamic_slice_in_dim(d_mod_all, chip * 768, 768, axis=1)
    pad_rows = lambda a: jnp.pad(a, ((0, 128 - N_DEV), (0, 0)))
    grad_w_mod, delta_w_mod, new_m_w_mod, new_v_w_mod = _mod_update(
        pad_rows(c_all), pad_rows(d_mod_s), w_mod[0], m_w_mod[0], v_w_mod[0])

    def ordered(w_mod_leaf, small_row_leaf, w_in_leaf, conv_leaf, w_out_leaf):
        b, n, a, dt, dn, at, fin = split_small(small_row_leaf)
        return [w_mod_leaf[None], b, n, w_in_leaf[None], conv_leaf[None], a, dt, dn, at, w_out_leaf[None], fin]

    loss = loss_row[0, 0]
    return (loss, grad_x[None],
            *ordered(grad_w_mod, g_small, grad_w_in, grad_conv_w, grad_w_out),
            *ordered(delta_w_mod, d_small, delta_w_in, delta_conv_w, delta_w_out),
            *ordered(new_m_w_mod, m_small, new_m_w_in, new_m_conv_w, new_m_w_out),
            *ordered(new_v_w_mod, v_small, new_v_w_in, new_v_conv_w, new_v_w_out))
```

```python
import functools

import jax
import jax.numpy as jnp
from jax import lax
from jax.experimental import pallas as pl
from jax.experimental.pallas import tpu as pltpu

f32 = jnp.float32
bf16 = jnp.bfloat16
HIGHEST = lax.Precision.HIGHEST
MESH = pl.DeviceIdType.MESH

D_MODEL = 1024
DN_HEADS = 4
DN_DIM = 128
DN_WIDTH = 512
AT_HEADS = 8
AT_DIM = 64
AT_WIDTH = 512
CHUNK = 64
Q_BLOCK = 128
CONV_K = 4
EPS = 1e-6
ROPE_THETA = 10000.0
PATTERN_DILATIONS = (1, 4, 16)
NEG = -1e30

QKV_W = 3 * DN_WIDTH
BA_W = 128
PDN_W = QKV_W + DN_WIDTH + BA_W
PAT_W = 3 * AT_WIDTH
CAT_W = PDN_W + PAT_W + AT_WIDTH
IN_COLS = 4104
N_CHIPS = 4
N_DEV = 8
SHARD_IN = IN_COLS // N_CHIPS
PACK_ROWS = 1296
PACK_CHUNK = 48

ADAM_LR = 0.001
ADAM_B1 = 0.9
ADAM_B2 = 0.999
ADAM_EPS = 1e-08
ADAM_WD = 0.01
ADAM_STEP = 10

VMEM_LIMIT = 56 * 1024 * 1024

NN = ((1,), (0,))
NT = ((1,), (1,))
TN = ((0,), (0,))


def _pieces(a, n):
    out = []
    for _ in range(n - 1):
        p = a.astype(bf16)
        out.append(p)
        a = a - p.astype(f32)
    out.append(a.astype(bf16))
    return out


def _dot(a, b, dims, exact):
    raw = lambda p, q: lax.dot_general(p, q, (dims, ((), ())), preferred_element_type=f32)
    if exact == "split":
        (ah, al), (bh, bl) = _pieces(a, 2), _pieces(b, 2)
        return raw(ah, bh) + (raw(ah, bl) + raw(al, bh))
    if exact:
        return lax.dot_general(a, b, (dims, ((), ())), precision=HIGHEST, preferred_element_type=f32)
    return raw(a.astype(bf16), b.astype(bf16))


def _dot_sel(sel, b, dims, sel_side):
    raw = lambda p, q: lax.dot_general(p, q, (dims, ((), ())), preferred_element_type=f32)
    sel = sel.astype(bf16)
    parts = [raw(sel, p) if sel_side == 0 else raw(p, sel) for p in _pieces(b, 3)]
    return (parts[0] + parts[1]) + parts[2]


@jax.custom_vjp
def _sel_left(sel, b):
    return _dot_sel(sel, b, NN, 0)


def _sel_left_fwd(sel, b):
    return _dot_sel(sel, b, NN, 0), sel


def _sel_left_bwd(sel, g):
    return jnp.zeros_like(sel), _dot_sel(sel, g, TN, 0)


_sel_left.defvjp(_sel_left_fwd, _sel_left_bwd)


@jax.custom_vjp
def _sel_right(a, sel):
    return _dot_sel(sel, a, NN, 1)


def _sel_right_fwd(a, sel):
    return _dot_sel(sel, a, NN, 1), sel


def _sel_right_bwd(sel, g):
    return _dot_sel(sel, g, NT, 1), jnp.zeros_like(sel)


_sel_right.defvjp(_sel_right_fwd, _sel_right_bwd)


class _Matmuls:
    def __init__(self, exact):
        @jax.custom_vjp
        def nn(a, b):
            return _dot(a, b, NN, exact)

        def nn_fwd(a, b):
            return _dot(a, b, NN, exact), (a, b)

        def nn_bwd(res, g):
            a, b = res
            return _dot(g, b, NT, exact), _dot(a, g, TN, exact)

        nn.defvjp(nn_fwd, nn_bwd)

        @jax.custom_vjp
        def nt(a, b):
            return _dot(a, b, NT, exact)

        def nt_fwd(a, b):
            return _dot(a, b, NT, exact), (a, b)

        def nt_bwd(res, g):
            a, b = res
            return _dot(g, b, NN, exact), _dot(g, a, TN, exact)

        nt.defvjp(nt_fwd, nt_bwd)

        @jax.custom_vjp
        def tn(a, b):
            return _dot(a, b, TN, exact)

        def tn_fwd(a, b):
            return _dot(a, b, TN, exact), (a, b)

        def tn_bwd(res, g):
            a, b = res
            return _dot(b, g, NT, exact), _dot(a, g, NN, exact)

        tn.defvjp(tn_fwd, tn_bwd)
        self.nn, self.nt, self.tn = nn, nt, tn


MM = _Matmuls(exact=False)
MX = _Matmuls(exact=True)
MS = _Matmuls(exact="split")


def _params(semantics=None):
    return pltpu.CompilerParams(dimension_semantics=semantics, vmem_limit_bytes=VMEM_LIMIT)


def _silu(x):
    return x * jax.nn.sigmoid(x)


def _group_ones(width, group):
    r = lax.broadcasted_iota(jnp.int32, (width, width), 0) // group
    c = lax.broadcasted_iota(jnp.int32, (width, width), 1) // group
    return (r == c).astype(f32)


def _hnorm(x, nw, scale, shift):
    xn = x * lax.rsqrt(jnp.mean(x * x, axis=-1, keepdims=True) + EPS)
    return xn * nw * (1.0 + scale) + shift


def _rope_tables(pos_col, invf_row, sign_row):
    ang = pos_col.astype(f32) * invf_row
    cos_t = jnp.cos(ang)
    sin_t = jnp.sin(ang) * sign_row
    return jnp.concatenate([cos_t] * 4, axis=1), jnp.concatenate([sin_t] * 4, axis=1)


def _rope_partner(x):
    lane = lax.broadcasted_iota(jnp.int32, x.shape, 1)
    width = x.shape[1]
    return jnp.where((lane % AT_DIM) < AT_DIM // 2, pltpu.roll(x, width - AT_DIM // 2, 1), pltpu.roll(x, AT_DIM // 2, 1))


def _pre_proj(x, pos_col, invf_row, sign_row, norm_w, scale, shift, w_cat):
    seq = x.shape[0]
    tm = 256

    def body(x_ref, pos_ref, invf_ref, sign_ref, nw_ref, sc_ref, sh_ref, w_ref, pdn_ref, pat_ref, zat_ref, h_ref):
        h = _hnorm(x_ref[...], nw_ref[...], sc_ref[...], sh_ref[...]).astype(bf16)
        h_ref[...] = h
        big = jnp.dot(h, w_ref[...], preferred_element_type=f32)
        pdn_ref[...] = big[:, :PDN_W]
        cos_t, sin_t = _rope_tables(pos_ref[...], invf_ref[...], sign_ref[...])
        q = big[:, PDN_W:PDN_W + AT_WIDTH]
        k = big[:, PDN_W + AT_WIDTH:PDN_W + 2 * AT_WIDTH]
        pat_ref[:, 0:AT_WIDTH] = q * cos_t + _rope_partner(q) * sin_t
        pat_ref[:, AT_WIDTH:2 * AT_WIDTH] = k * cos_t + _rope_partner(k) * sin_t
        pat_ref[:, 2 * AT_WIDTH:] = big[:, PDN_W + 2 * AT_WIDTH:PDN_W + PAT_W]
        zat_ref[...] = big[:, PDN_W + PAT_W:]

    row = lambda w: pl.BlockSpec((1, w), lambda i: (0, 0))
    tile = lambda w: pl.BlockSpec((tm, w), lambda i: (i, 0))
    return pl.pallas_call(
        body, name="pre_proj", grid=(seq // tm,),
        in_specs=[tile(D_MODEL), tile(1), row(128), row(128), row(D_MODEL), row(D_MODEL), row(D_MODEL),
                  pl.BlockSpec((D_MODEL, CAT_W), lambda i: (0, 0))],
        out_specs=[tile(PDN_W), tile(PAT_W), tile(AT_WIDTH), tile(D_MODEL)],
        out_shape=[jax.ShapeDtypeStruct((seq, PDN_W), f32), jax.ShapeDtypeStruct((seq, PAT_W), f32),
                   jax.ShapeDtypeStruct((seq, AT_WIDTH), f32), jax.ShapeDtypeStruct((seq, D_MODEL), bf16)],
        compiler_params=_params(("parallel",)),
    )(x, pos_col, invf_row, sign_row, norm_w, scale, shift, w_cat)


def _conv_taps(ext_ref, halo, cur, w8):
    rows = cur.shape[0]
    ext_ref[0:8, :] = halo
    ext_ref[8:8 + rows, :] = cur
    out = ext_ref[pl.ds(5, rows), :] * w8[0:1, :]
    for j in range(1, CONV_K):
        out = out + ext_ref[pl.ds(5 + j, rows), :] * w8[j:j + 1, :]
    return out


def _dn_pre(cq, ck, cv, ba, a_row, dt_row):
    ones = _group_ones(DN_WIDTH, DN_DIM)
    sq, sk, v = _silu(cq), _silu(ck), _silu(cv)
    qn = sq * lax.rsqrt(_sel_right(sq * sq, ones) + EPS)
    kn = sk * lax.rsqrt(_sel_right(sk * sk, ones) + EPS)
    beta_all = jax.nn.sigmoid(ba)
    g_all = -jnp.exp(a_row) * jax.nn.softplus(ba + dt_row)
    return qn, kn, v, beta_all, g_all


def _dn_intra(q, k, v, beta, g):
    ri = lax.broadcasted_iota(jnp.int32, (CHUNK, CHUNK), 0)
    ci = lax.broadcasted_iota(jnp.int32, (CHUNK, CHUNK), 1)
    tril = ri >= ci
    strict = ri > ci
    lower = tril.astype(f32)
    eye = (ri == ci).astype(f32)
    g_wide = jnp.broadcast_to(g, (CHUNK, DN_DIM))
    gc = _sel_left(lower, g_wide)
    gc_sq = _sel_left(lower, jnp.broadcast_to(g, (CHUNK, CHUNK)))
    g_end = _sel_left(jnp.ones((CHUNK, CHUNK), f32), g_wide)
    g_end8 = _sel_left(jnp.ones((8, CHUNK), f32), g_wide)
    decay = jnp.exp(jnp.where(tril, gc_sq - gc_sq.T, -jnp.inf))
    q = q * (DN_DIM ** -0.5)
    kb = k * beta
    vb = v * beta
    a_low = jnp.where(strict, MM.nt(kb, k) * decay, 0.0)
    power = -a_low
    inv = eye + power
    for _ in range(5):
        power = MS.nn(power, power)
        inv = inv + MS.nn(inv, power)
    e_gc = jnp.exp(gc)
    u = MS.nn(inv, vb)
    w = MS.nn(inv, kb * e_gc)
    attn = jnp.where(tril, MM.nt(q, k) * decay, 0.0)
    return u, w, q * e_gc, k * jnp.exp(g_end - gc), attn, jnp.exp(g_end8)


def _dn_step(u, w, q_dec, k_dec, attn, e_end, state):
    v_new = u - MM.nn(w, state)
    o = MM.nn(q_dec, state) + MM.nn(attn, v_new)
    return o, state * e_end + MM.tn(k_dec, v_new)


INTRA_CHUNKS = 2
SCAN_CHUNKS = 4


def _intra_specs(nc):
    rows = INTRA_CHUNKS * CHUNK
    cur = pl.BlockSpec((rows, QKV_W), lambda i: (i, 0))
    halo = pl.BlockSpec((8, QKV_W), lambda i: (jnp.maximum(i * (rows // 8) - 1, 0), 0))
    ba = pl.BlockSpec((rows, BA_W), lambda i: (i, (QKV_W + DN_WIDTH) // BA_W))
    conv = pl.BlockSpec((8, QKV_W), lambda i: (0, 0))
    row = pl.BlockSpec((1, BA_W), lambda i: (0, 0))
    wide = pl.BlockSpec((rows, DN_WIDTH), lambda i: (i, 0))
    attn = pl.BlockSpec((INTRA_CHUNKS, DN_HEADS, CHUNK, CHUNK), lambda i: (i, 0, 0, 0))
    e_end = pl.BlockSpec((INTRA_CHUNKS, 8 * DN_HEADS, DN_DIM), lambda i: (i, 0, 0))
    return cur, halo, ba, conv, row, wide, attn, e_end


def _intra_shapes(seq):
    nc = seq // CHUNK
    wide = jax.ShapeDtypeStruct((seq, DN_WIDTH), f32)
    return [wide, wide, wide, wide, jax.ShapeDtypeStruct((nc, DN_HEADS, CHUNK, CHUNK), f32),
            jax.ShapeDtypeStruct((nc, 8 * DN_HEADS, DN_DIM), f32)]


def _dn_intra_forward(pdn, conv_w8, a_row, dt_row):
    seq = pdn.shape[0]
    nc = seq // CHUNK
    cur, halo, ba, conv, row, wide, attn, e_end = _intra_specs(nc)

    def body(cur_ref, halo_ref, ba_ref, w_ref, a_ref, dt_ref, u_ref, w_out_ref, qd_ref, kd_ref, attn_ref, e_ref, ext_ref):
        halo_rows = jnp.where(pl.program_id(0) > 0, halo_ref[...], 0.0)
        c = _conv_taps(ext_ref, halo_rows, cur_ref[...], w_ref[...])
        qn, kn, v, beta_all, g_all = _dn_pre(c[:, :DN_WIDTH], c[:, DN_WIDTH:2 * DN_WIDTH], c[:, 2 * DN_WIDTH:],
                                             ba_ref[...], a_ref[...], dt_ref[...])
        for ci in range(INTRA_CHUNKS):
            rows = slice(ci * CHUNK, (ci + 1) * CHUNK)
            for h in range(DN_HEADS):
                lanes = slice(h * DN_DIM, (h + 1) * DN_DIM)
                u, w, qd, kd, at, e8 = _dn_intra(qn[rows, lanes], kn[rows, lanes], v[rows, lanes], beta_all[rows, h:h + 1],
                                                 g_all[rows, DN_HEADS + h:DN_HEADS + h + 1])
                u_ref[rows, lanes] = u
                w_out_ref[rows, lanes] = w
                qd_ref[rows, lanes] = qd
                kd_ref[rows, lanes] = kd
                attn_ref[ci, h] = at
                e_ref[ci, 8 * h:8 * h + 8, :] = e8

    return pl.pallas_call(
        body, name="dn_intra_forward", grid=(nc // INTRA_CHUNKS,),
        in_specs=[cur, halo, ba, conv, row, row],
        out_specs=[wide, wide, wide, wide, attn, e_end],
        out_shape=_intra_shapes(seq),
        scratch_shapes=[pltpu.VMEM((INTRA_CHUNKS * CHUNK + 8, QKV_W), f32)],
        compiler_params=_params(("parallel",)),
    )(pdn, pdn, pdn, conv_w8, a_row, dt_row)


def _scan_specs(nc, reverse):
    steps = nc // SCAN_CHUNKS
    at = (lambda i: steps - 1 - i) if reverse else (lambda i: i)
    wide = pl.BlockSpec((SCAN_CHUNKS * CHUNK, DN_WIDTH), lambda i: (at(i), 0))
    attn = pl.BlockSpec((SCAN_CHUNKS, DN_HEADS, CHUNK, CHUNK), lambda i: (at(i), 0, 0, 0))
    e_end = pl.BlockSpec((SCAN_CHUNKS, 8 * DN_HEADS, DN_DIM), lambda i: (at(i), 0, 0))
    states = pl.BlockSpec((SCAN_CHUNKS, DN_HEADS, DN_DIM, DN_DIM), lambda i: (at(i), 0, 0, 0))
    return wide, attn, e_end, states


def _dn_scan_forward(u, w, q_dec, k_dec, attn, e_end):
    seq = u.shape[0]
    nc = seq // CHUNK
    wide, attn_spec, e_spec, st_spec = _scan_specs(nc, reverse=False)

    def body(u_ref, w_ref, qd_ref, kd_ref, attn_ref, e_ref, o_ref, st_ref, state_ref):
        @pl.when(pl.program_id(0) == 0)
        def _():
            state_ref[...] = jnp.zeros_like(state_ref)

        for ci in range(SCAN_CHUNKS):
            rows = slice(ci * CHUNK, (ci + 1) * CHUNK)
            for h in range(DN_HEADS):
                lanes = slice(h * DN_DIM, (h + 1) * DN_DIM)
                state = state_ref[h]
                st_ref[ci, h] = state
                o, new_state = _dn_step(u_ref[rows, lanes], w_ref[rows, lanes], qd_ref[rows, lanes], kd_ref[rows, lanes],
                                        attn_ref[ci, h], e_ref[ci, 8 * h:8 * h + 1, :], state)
                o_ref[rows, lanes] = o
                state_ref[h] = new_state

    return pl.pallas_call(
        body, name="dn_scan_forward", grid=(nc // SCAN_CHUNKS,),
        in_specs=[wide, wide, wide, wide, attn_spec, e_spec],
        out_specs=[wide, st_spec],
        out_shape=[jax.ShapeDtypeStruct((seq, DN_WIDTH), f32), jax.ShapeDtypeStruct((nc, DN_HEADS, DN_DIM, DN_DIM), f32)],
        scratch_shapes=[pltpu.VMEM((DN_HEADS, DN_DIM, DN_DIM), f32)],
        compiler_params=_params(("arbitrary",)),
    )(u, w, q_dec, k_dec, attn, e_end)


def _dn_scan_backward(u, w, q_dec, k_dec, attn, e_end, states, d_o):
    seq = u.shape[0]
    nc = seq // CHUNK
    wide, attn_spec, e_spec, st_spec = _scan_specs(nc, reverse=True)

    def body(u_ref, w_ref, qd_ref, kd_ref, attn_ref, e_ref, st_ref, do_ref,
             du_ref, dw_ref, dqd_ref, dkd_ref, dattn_ref, de_ref, dstate_ref):
        @pl.when(pl.program_id(0) == 0)
        def _():
            dstate_ref[...] = jnp.zeros_like(dstate_ref)

        for ci in reversed(range(SCAN_CHUNKS)):
            rows = slice(ci * CHUNK, (ci + 1) * CHUNK)
            for h in range(DN_HEADS):
                lanes = slice(h * DN_DIM, (h + 1) * DN_DIM)
                _, step_vjp = jax.vjp(_dn_step, u_ref[rows, lanes], w_ref[rows, lanes], qd_ref[rows, lanes],
                                      kd_ref[rows, lanes], attn_ref[ci, h], e_ref[ci, 8 * h:8 * h + 1, :], st_ref[ci, h])
                du, dw, dqd, dkd, dattn, de, dstate = step_vjp((do_ref[rows, lanes], dstate_ref[h]))
                dstate_ref[h] = dstate
                du_ref[rows, lanes] = du
                dw_ref[rows, lanes] = dw
                dqd_ref[rows, lanes] = dqd
                dkd_ref[rows, lanes] = dkd
                dattn_ref[ci, h] = dattn
                first_row = lax.broadcasted_iota(jnp.int32, (8, DN_DIM), 0) == 0
                de_ref[ci, 8 * h:8 * h + 8, :] = jnp.where(first_row, jnp.broadcast_to(de, (8, DN_DIM)), 0.0)

    return pl.pallas_call(
        body, name="dn_scan_backward", grid=(nc // SCAN_CHUNKS,),
        in_specs=[wide, wide, wide, wide, attn_spec, e_spec, st_spec, wide],
        out_specs=[wide, wide, wide, wide, attn_spec, e_spec],
        out_shape=_intra_shapes(seq),
        scratch_shapes=[pltpu.VMEM((DN_HEADS, DN_DIM, DN_DIM), f32)],
        compiler_params=_params(("arbitrary",)),
    )(u, w, q_dec, k_dec, attn, e_end, states, d_o)


def _dn_intra_backward(pdn, conv_w8, a_row, dt_row, d_u, d_w, d_qd, d_kd, d_attn, d_e):
    seq = pdn.shape[0]
    nc = seq // CHUNK
    rows_per_step = INTRA_CHUNKS * CHUNK
    cur, halo, ba, conv, row, wide, attn, e_end = _intra_specs(nc)

    def body(cur_ref, halo_ref, ba_ref, w_ref, a_ref, dt_ref, du_ref, dw_ref, dqd_ref, dkd_ref, dattn_ref, de_ref,
             dconv_ref, dba_ref, drow_ref, ext_ref):
        @pl.when(pl.program_id(0) == 0)
        def _():
            drow_ref[...] = jnp.zeros_like(drow_ref)

        halo_rows = jnp.where(pl.program_id(0) > 0, halo_ref[...], 0.0)
        c = _conv_taps(ext_ref, halo_rows, cur_ref[...], w_ref[...])
        (qn, kn, v, beta_all, g_all), pre_vjp = jax.vjp(
            _dn_pre, c[:, :DN_WIDTH], c[:, DN_WIDTH:2 * DN_WIDTH], c[:, 2 * DN_WIDTH:], ba_ref[...], a_ref[...], dt_ref[...])
        lane = lax.broadcasted_iota(jnp.int32, (CHUNK, BA_W), 1)
        dq_rows, dk_rows, dv_rows, dbeta_rows, dg_rows = [], [], [], [], []
        for ci in range(INTRA_CHUNKS):
            rows = slice(ci * CHUNK, (ci + 1) * CHUNK)
            d_beta_all = jnp.zeros((CHUNK, BA_W), f32)
            d_g_all = jnp.zeros((CHUNK, BA_W), f32)
            dq, dk, dv = [], [], []
            for h in range(DN_HEADS):
                lanes = slice(h * DN_DIM, (h + 1) * DN_DIM)
                _, intra_vjp = jax.vjp(_dn_intra, qn[rows, lanes], kn[rows, lanes], v[rows, lanes], beta_all[rows, h:h + 1],
                                       g_all[rows, DN_HEADS + h:DN_HEADS + h + 1])
                dq_h, dk_h, dv_h, dbeta_h, dg_h = intra_vjp(
                    (du_ref[rows, lanes], dw_ref[rows, lanes], dqd_ref[rows, lanes], dkd_ref[rows, lanes],
                     dattn_ref[ci, h], de_ref[ci, 8 * h:8 * h + 8, :]))
                dq.append(dq_h)
                dk.append(dk_h)
                dv.append(dv_h)
                d_beta_all = d_beta_all + jnp.where(lane == h, dbeta_h, 0.0)
                d_g_all = d_g_all + jnp.where(lane == DN_HEADS + h, dg_h, 0.0)
            dq_rows.append(jnp.concatenate(dq, axis=1))
            dk_rows.append(jnp.concatenate(dk, axis=1))
            dv_rows.append(jnp.concatenate(dv, axis=1))
            dbeta_rows.append(d_beta_all)
            dg_rows.append(d_g_all)
        stack = lambda parts: jnp.concatenate(parts, axis=0)
        dcq, dck, dcv, dba, da_row, ddt_row = pre_vjp(
            (stack(dq_rows), stack(dk_rows), stack(dv_rows), stack(dbeta_rows), stack(dg_rows)))
        dconv_ref[:, :DN_WIDTH] = dcq
        dconv_ref[:, DN_WIDTH:2 * DN_WIDTH] = dck
        dconv_ref[:, 2 * DN_WIDTH:] = dcv
        dba_ref[...] = dba
        drow_ref[0:1, :] += da_row
        drow_ref[1:2, :] += ddt_row

    return pl.pallas_call(
        body, name="dn_intra_backward", grid=(nc // INTRA_CHUNKS,),
        in_specs=[cur, halo, ba, conv, row, row, wide, wide, wide, wide, attn, e_end],
        out_specs=[pl.BlockSpec((rows_per_step, QKV_W), lambda i: (i, 0)),
                   pl.BlockSpec((rows_per_step, BA_W), lambda i: (i, 0)),
                   pl.BlockSpec((8, BA_W), lambda i: (0, 0))],
        out_shape=[jax.ShapeDtypeStruct((seq, QKV_W), f32), jax.ShapeDtypeStruct((seq, BA_W), f32),
                   jax.ShapeDtypeStruct((8, BA_W), f32)],
        scratch_shapes=[pltpu.VMEM((rows_per_step + 8, QKV_W), f32)],
        compiler_params=_params(("arbitrary",)),
    )(pdn, pdn, pdn, conv_w8, a_row, dt_row, d_u, d_w, d_qd, d_kd, d_attn, d_e)


def _band_masks(block_index):
    qi = lax.broadcasted_iota(jnp.int32, (Q_BLOCK, Q_BLOCK), 0)
    kj = lax.broadcasted_iota(jnp.int32, (Q_BLOCK, Q_BLOCK), 1)
    return (kj >= qi) & (block_index > 0), kj <= qi


def _attn_forward(pat_view, dil):
    length = pat_view.shape[0]
    nb = length // Q_BLOCK
    scale = AT_DIM ** -0.5

    def body(q_ref, kp_ref, kc_ref, vp_ref, vc_ref, o_ref, lse_ref):
        mask_prev, mask_cur = _band_masks(pl.program_id(1))
        for h in range(AT_HEADS):
            lanes = slice(h * AT_DIM, (h + 1) * AT_DIM)
            q = q_ref[:, lanes]
            s_prev = jnp.where(mask_prev, _dot(q, kp_ref[:, lanes], NT, False) * scale, NEG)
            s_cur = jnp.where(mask_cur, _dot(q, kc_ref[:, lanes], NT, False) * scale, NEG)
            m = jnp.maximum(jnp.max(s_prev, axis=1, keepdims=True), jnp.max(s_cur, axis=1, keepdims=True))
            p_prev = jnp.exp(s_prev - m)
            p_cur = jnp.exp(s_cur - m)
            l = jnp.sum(p_prev, axis=1, keepdims=True) + jnp.sum(p_cur, axis=1, keepdims=True)
            o = _dot(p_prev, vp_ref[:, lanes], NN, False) + _dot(p_cur, vc_ref[:, lanes], NN, False)
            o_ref[:, lanes] = o / l
            lse_ref[:, lanes] = jnp.broadcast_to(m + jnp.log(l), (Q_BLOCK, AT_DIM))

    def blk(piece, prev):
        if prev:
            return pl.BlockSpec((Q_BLOCK, AT_WIDTH), lambda r, n: (jnp.maximum(n - 1, 0), 3 * r + piece))
        return pl.BlockSpec((Q_BLOCK, AT_WIDTH), lambda r, n: (n, 3 * r + piece))

    out = pl.BlockSpec((Q_BLOCK, AT_WIDTH), lambda r, n: (n, r))
    return pl.pallas_call(
        body, name=f"attn_forward_d{dil}", grid=(dil, nb),
        in_specs=[blk(0, False), blk(1, True), blk(1, False), blk(2, True), blk(2, False)],
        out_specs=[out, out],
        out_shape=[jax.ShapeDtypeStruct((length, dil * AT_WIDTH), f32)] * 2,
        compiler_params=_params(("parallel", "parallel")),
    )(pat_view, pat_view, pat_view, pat_view, pat_view)


def _attn_backward_q(pat_view, d_out, delta, lse, dil):
    length = pat_view.shape[0]
    nb = length // Q_BLOCK
    scale = AT_DIM ** -0.5

    def body(q_ref, kp_ref, kc_ref, vp_ref, vc_ref, do_ref, dl_ref, lse_ref, dq_ref):
        mask_prev, mask_cur = _band_masks(pl.program_id(1))
        for h in range(AT_HEADS):
            lanes = slice(h * AT_DIM, (h + 1) * AT_DIM)
            q = q_ref[:, lanes]
            do = do_ref[:, lanes]
            lse_h = lse_ref[:, h * AT_DIM:h * AT_DIM + 1]
            dl_h = dl_ref[:, h * AT_DIM:h * AT_DIM + 1]
            dq = jnp.zeros((Q_BLOCK, AT_DIM), f32)
            for mask, k_ref, v_ref in ((mask_prev, kp_ref, vp_ref), (mask_cur, kc_ref, vc_ref)):
                k = k_ref[:, lanes]
                s = _dot(q, k, NT, False) * scale
                p = jnp.exp(jnp.where(mask, s - lse_h, NEG))
                dp = _dot(do, v_ref[:, lanes], NT, False)
                dq = dq + _dot(p * (dp - dl_h), k, NN, False)
            dq_ref[:, lanes] = dq * scale

    def blk(piece, prev):
        if prev:
            return pl.BlockSpec((Q_BLOCK, AT_WIDTH), lambda r, n: (jnp.maximum(n - 1, 0), 3 * r + piece))
        return pl.BlockSpec((Q_BLOCK, AT_WIDTH), lambda r, n: (n, 3 * r + piece))

    one = pl.BlockSpec((Q_BLOCK, AT_WIDTH), lambda r, n: (n, r))
    return pl.pallas_call(
        body, name=f"attn_backward_q_d{dil}", grid=(dil, nb),
        in_specs=[blk(0, False), blk(1, True), blk(1, False), blk(2, True), blk(2, False), one, one, one],
        out_specs=one,
        out_shape=jax.ShapeDtypeStruct((length, dil * AT_WIDTH), f32),
        compiler_params=_params(("parallel", "parallel")),
    )(pat_view, pat_view, pat_view, pat_view, pat_view, d_out, delta, lse)


def _attn_backward_kv(pat_view, d_out, delta, lse, dil):
    length = pat_view.shape[0]
    nb = length // Q_BLOCK
    scale = AT_DIM ** -0.5

    def body(k_ref, v_ref, qa_ref, qb_ref, doa_ref, dob_ref, dla_ref, dlb_ref, lsea_ref, lseb_ref, dk_ref, dv_ref):
        j = pl.program_id(1)
        qi = lax.broadcasted_iota(jnp.int32, (Q_BLOCK, Q_BLOCK), 0)
        kj = lax.broadcasted_iota(jnp.int32, (Q_BLOCK, Q_BLOCK), 1)
        mask_same = kj <= qi
        mask_next = (kj >= qi) & (j + 1 < nb)
        for h in range(AT_HEADS):
            lanes = slice(h * AT_DIM, (h + 1) * AT_DIM)
            one = slice(h * AT_DIM, h * AT_DIM + 1)
            k = k_ref[:, lanes]
            v = v_ref[:, lanes]
            dk = jnp.zeros((Q_BLOCK, AT_DIM), f32)
            dv = jnp.zeros((Q_BLOCK, AT_DIM), f32)
            for mask, q_ref, do_ref, dl_ref, lse_ref in ((mask_same, qa_ref, doa_ref, dla_ref, lsea_ref),
                                                         (mask_next, qb_ref, dob_ref, dlb_ref, lseb_ref)):
                q = q_ref[:, lanes]
                do = do_ref[:, lanes]
                s = _dot(q, k, NT, False) * scale
                p = jnp.exp(jnp.where(mask, s - lse_ref[:, one], NEG))
                dv = dv + _dot(p, do, TN, False)
                dp = _dot(do, v, NT, False)
                dk = dk + _dot(p * (dp - dl_ref[:, one]), q, TN, False)
            dk_ref[:, lanes] = dk * scale
            dv_ref[:, lanes] = dv

    def kv(piece):
        return pl.BlockSpec((Q_BLOCK, AT_WIDTH), lambda r, n: (n, 3 * r + piece))

    q_same = pl.BlockSpec((Q_BLOCK, AT_WIDTH), lambda r, n: (n, 3 * r))
    q_next = pl.BlockSpec((Q_BLOCK, AT_WIDTH), lambda r, n: (jnp.minimum(n + 1, nb - 1), 3 * r))
    same = pl.BlockSpec((Q_BLOCK, AT_WIDTH), lambda r, n: (n, r))
    nxt = pl.BlockSpec((Q_BLOCK, AT_WIDTH), lambda r, n: (jnp.minimum(n + 1, nb - 1), r))
    return pl.pallas_call(
        body, name=f"attn_backward_kv_d{dil}", grid=(dil, nb),
        in_specs=[kv(1), kv(2), q_same, q_next, same, nxt, same, nxt, same, nxt],
        out_specs=[same, same],
        out_shape=[jax.ShapeDtypeStruct((length, dil * AT_WIDTH), f32)] * 2,
        compiler_params=_params(("parallel", "parallel")),
    )(pat_view, pat_view, pat_view, pat_view, d_out, d_out, delta, delta, lse, lse)


def _to_view(a, dil):
    return a if dil == 1 else a.reshape(a.shape[0] // dil, dil * a.shape[1])


def _from_view(a, dil):
    return a if dil == 1 else a.reshape(a.shape[0] * dil, a.shape[1] // dil)


def _gated_norms(o_dn, z_dn, o_at, z_at, dn_w, at_w):
    ms_dn = _sel_right(o_dn * o_dn, _group_ones(DN_WIDTH, DN_DIM)) * (1.0 / DN_DIM)
    a = o_dn * lax.rsqrt(ms_dn + EPS) * dn_w * _silu(z_dn)
    ms_at = _sel_right(o_at * o_at, _group_ones(AT_WIDTH, AT_DIM)) * (1.0 / AT_DIM)
    b = o_at * lax.rsqrt(ms_at + EPS) * at_w * _silu(z_at)
    return a, b


def _residual_loss(x, mix, gate, fin_w, target):
    x2 = x + gate * mix
    y = x2 * lax.rsqrt(jnp.mean(x2 * x2, axis=-1, keepdims=True) + EPS) * fin_w
    err = y - target
    per_token = jnp.sum(err * err, axis=1, keepdims=True) * (1.0 / D_MODEL)
    return 0.5 * jnp.sum(per_token, axis=0, keepdims=True)


TAIL_ROWS = 8


def _tail(x, target, o_dn, pdn, outs, lses, z_at, gate, w_out, dn_w_row, at_w_row, fin_w):
    seq = x.shape[0]
    tm = 256

    def body(x_ref, t_ref, odn_ref, zdn_ref, o1_ref, o4_ref, o16_ref, l1_ref, l4_ref, l16_ref, zat_ref, gate_ref,
             wt_ref, wb_ref, dnw_ref, atw_ref, fw_ref,
             dx2_ref, dodn_ref, dzdn_ref, doat_ref, delta_ref, lse_ref, dzat_ref, gwt_ref, gwb_ref, rows_ref):
        @pl.when(pl.program_id(0) == 0)
        def _():
            gwt_ref[...] = jnp.zeros_like(gwt_ref)
            gwb_ref[...] = jnp.zeros_like(gwb_ref)
            rows_ref[...] = jnp.zeros_like(rows_ref)

        l1, l4, l16 = l1_ref[...], l4_ref[...], l16_ref[...]
        top = jnp.maximum(jnp.maximum(l1, l4), l16)
        e1, e4, e16 = jnp.exp(l1 - top), jnp.exp(l4 - top), jnp.exp(l16 - top)
        den = e1 + e4 + e16
        o_at = (e1 * o1_ref[...] + e4 * o4_ref[...] + e16 * o16_ref[...]) / den
        lse_ref[...] = top + jnp.log(den)

        (a, b), norms_vjp = jax.vjp(_gated_norms, odn_ref[...], zdn_ref[...], o_at, zat_ref[...], dnw_ref[...], atw_ref[...])
        a, b = a.astype(bf16), b.astype(bf16)
        mix = jnp.dot(a, wt_ref[...], preferred_element_type=f32) + jnp.dot(b, wb_ref[...], preferred_element_type=f32)
        loss, loss_vjp = jax.vjp(_residual_loss, x_ref[...], mix, gate_ref[...], fw_ref[...], t_ref[...])
        dx2, dmix, dgate, dfw, _ = loss_vjp(jnp.ones((1, 1), f32))
        dmix = dmix.astype(bf16)
        dwt = lax.dot_general(a, dmix, (TN, ((), ())), preferred_element_type=f32)
        dwb = lax.dot_general(b, dmix, (TN, ((), ())), preferred_element_type=f32)
        da = lax.dot_general(dmix, wt_ref[...], (NT, ((), ())), preferred_element_type=f32)
        db = lax.dot_general(dmix, wb_ref[...], (NT, ((), ())), preferred_element_type=f32)
        dodn, dzdn, doat, dzat, ddnw, datw = norms_vjp((da, db))
        dx2_ref[...] = dx2
        dodn_ref[...] = dodn
        dzdn_ref[...] = dzdn
        doat_ref[...] = doat
        dzat_ref[...] = dzat
        delta_ref[...] = _sel_right(doat * o_at, _group_ones(AT_WIDTH, AT_DIM))
        gwt_ref[...] += dwt
        gwb_ref[...] += dwb
        rows_ref[0:1, :] += jnp.broadcast_to(loss, (1, D_MODEL))
        rows_ref[1:2, :] += dgate
        rows_ref[2:3, :] += dfw
        rows_ref[3:4, 0:DN_WIDTH] += ddnw
        rows_ref[4:5, 0:AT_WIDTH] += datw

    tile = lambda w: pl.BlockSpec((tm, w), lambda i: (i, 0))
    row = lambda w: pl.BlockSpec((1, w), lambda i: (0, 0))
    half_w = pl.BlockSpec((DN_WIDTH, D_MODEL), lambda i: (0, 0))
    sds = lambda w: jax.ShapeDtypeStruct((seq, w), f32)
    return pl.pallas_call(
        body, name="tail", grid=(seq // tm,),
        in_specs=[tile(D_MODEL), tile(D_MODEL), tile(DN_WIDTH),
                  pl.BlockSpec((tm, DN_WIDTH), lambda i: (i, QKV_W // DN_WIDTH)),
                  tile(AT_WIDTH), tile(AT_WIDTH), tile(AT_WIDTH), tile(AT_WIDTH), tile(AT_WIDTH), tile(AT_WIDTH),
                  tile(AT_WIDTH), row(D_MODEL), half_w, pl.BlockSpec((AT_WIDTH, D_MODEL), lambda i: (1, 0)),
                  row(DN_WIDTH), row(AT_WIDTH), row(D_MODEL)],
        out_specs=[tile(D_MODEL), tile(DN_WIDTH), tile(DN_WIDTH), tile(AT_WIDTH), tile(AT_WIDTH), tile(AT_WIDTH),
                   tile(AT_WIDTH), half_w, half_w, pl.BlockSpec((TAIL_ROWS, D_MODEL), lambda i: (0, 0))],
        out_shape=[sds(D_MODEL), sds(DN_WIDTH), sds(DN_WIDTH), sds(AT_WIDTH), sds(AT_WIDTH), sds(AT_WIDTH),
                   sds(AT_WIDTH), jax.ShapeDtypeStruct((DN_WIDTH, D_MODEL), f32),
                   jax.ShapeDtypeStruct((AT_WIDTH, D_MODEL), f32), jax.ShapeDtypeStruct((TAIL_ROWS, D_MODEL), f32)],
        compiler_params=_params(("arbitrary",)),
    )(x, target, o_dn, pdn, outs[0], outs[1], outs[2], lses[0], lses[1], lses[2], z_at, gate, w_out, w_out,
      dn_w_row, at_w_row, fin_w)


PRE_ROWS = 8


def _pre_backward(x, dx2, pos_col, invf_row, sign_row, norm_w, scale, shift, w_cat, conv_w8, pdn, d_conv, d_zdn, d_ba,
                  d_q, d_k, d_v, d_zat):
    seq = x.shape[0]
    tm = 256
    last = seq // tm - 1

    def body(x_ref, dx2_ref, pos_ref, invf_ref, sign_ref, nw_ref, sc_ref, sh_ref, w_ref, cw_ref,
             pre_ref, prehalo_ref, dc_ref, dchalo_ref, dz_ref, dba_ref,
             dq1_ref, dq4_ref, dq16_ref, dk1_ref, dk4_ref, dk16_ref, dv1_ref, dv4_ref, dv16_ref, dzat_ref,
             gx_ref, dproj_ref, rows_ref, crow_ref, ext_ref):
        i = pl.program_id(0)

        @pl.when(i == 0)
        def _():
            rows_ref[...] = jnp.zeros_like(rows_ref)
            crow_ref[...] = jnp.zeros_like(crow_ref)

        dc = dc_ref[...]
        ext_ref[0:tm, :] = dc
        ext_ref[tm:tm + 8, :] = jnp.where(i < last, dchalo_ref[...], 0.0)
        w8 = cw_ref[...]
        d_pre = ext_ref[pl.ds(3, tm), :] * w8[0:1, :]
        for j in range(1, CONV_K):
            d_pre = d_pre + ext_ref[pl.ds(3 - j, tm), :] * w8[j:j + 1, :]
        ext_ref[0:8, :] = jnp.where(i > 0, prehalo_ref[...], 0.0)
        ext_ref[8:8 + tm, :] = pre_ref[...]
        for j in range(CONV_K):
            crow_ref[j:j + 1, :] += jnp.sum(dc * ext_ref[pl.ds(5 + j, tm), :], axis=0, keepdims=True)

        cos_t, sin_t = _rope_tables(pos_ref[...], invf_ref[...], sign_ref[...])
        dq = dq1_ref[...] + dq4_ref[...] + dq16_ref[...]
        dk = dk1_ref[...] + dk4_ref[...] + dk16_ref[...]
        dq = dq * cos_t + _rope_partner(dq * sin_t)
        dk = dk * cos_t + _rope_partner(dk * sin_t)
        dv = dv1_ref[...] + dv4_ref[...] + dv16_ref[...]
        d_proj = jnp.concatenate([d_pre, dz_ref[...], dba_ref[...], dq, dk, dv, dzat_ref[...]], axis=1).astype(bf16)
        dproj_ref[...] = d_proj
        dh = lax.dot_general(d_proj, w_ref[...], (NT, ((), ())), preferred_element_type=f32)
        _, vjp = jax.vjp(_hnorm, x_ref[...], nw_ref[...], sc_ref[...], sh_ref[...])
        dx, dnw, dsc, dsh = vjp(dh)
        gx_ref[...] = dx + dx2_ref[...]
        rows_ref[0:1, :] += dnw
        rows_ref[1:2, :] += dsc
        rows_ref[2:3, :] += dsh

    tile = lambda w: pl.BlockSpec((tm, w), lambda i: (i, 0))
    row = lambda w: pl.BlockSpec((1, w), lambda i: (0, 0))
    step8 = tm // 8
    return pl.pallas_call(
        body, name="pre_backward", grid=(seq // tm,),
        in_specs=[tile(D_MODEL), tile(D_MODEL), tile(1), row(128), row(128), row(D_MODEL), row(D_MODEL), row(D_MODEL),
                  pl.BlockSpec((D_MODEL, CAT_W), lambda i: (0, 0)), pl.BlockSpec((8, QKV_W), lambda i: (0, 0)),
                  tile(QKV_W), pl.BlockSpec((8, QKV_W), lambda i: (jnp.maximum(i * step8 - 1, 0), 0)),
                  tile(QKV_W), pl.BlockSpec((8, QKV_W), lambda i: (jnp.minimum((i + 1) * step8, seq // 8 - 1), 0)),
                  tile(DN_WIDTH), tile(BA_W)] + [tile(AT_WIDTH)] * 10,
        out_specs=[tile(D_MODEL), tile(CAT_W), pl.BlockSpec((PRE_ROWS, D_MODEL), lambda i: (0, 0)),
                   pl.BlockSpec((8, QKV_W), lambda i: (0, 0))],
        out_shape=[jax.ShapeDtypeStruct((seq, D_MODEL), f32), jax.ShapeDtypeStruct((seq, CAT_W), bf16),
                   jax.ShapeDtypeStruct((PRE_ROWS, D_MODEL), f32), jax.ShapeDtypeStruct((8, QKV_W), f32)],
        scratch_shapes=[pltpu.VMEM((tm + 8, QKV_W), f32)],
        compiler_params=_params(("arbitrary",)),
    )(x, dx2, pos_col, invf_row, sign_row, norm_w, scale, shift, w_cat, conv_w8, pdn, pdn, d_conv, d_conv, d_zdn, d_ba,
      d_q[0], d_q[1], d_q[2], d_k[0], d_k[1], d_k[2], d_v[0], d_v[1], d_v[2], d_zat)


def _weight_grad(h, d_proj):
    seq = h.shape[0]
    tk, tn = 1024, 384
    n_k = seq // tk

    def body(h_ref, d_ref, o_ref):
        @pl.when(pl.program_id(1) == 0)
        def _():
            o_ref[...] = jnp.zeros_like(o_ref)

        o_ref[...] += lax.dot_general(h_ref[...], d_ref[...], (TN, ((), ())), preferred_element_type=f32)

    return pl.pallas_call(
        body, name="weight_grad", grid=(CAT_W // tn, n_k),
        in_specs=[pl.BlockSpec((tk, D_MODEL), lambda n, k: (k, 0)), pl.BlockSpec((tk, tn), lambda n, k: (k, n))],
        out_specs=pl.BlockSpec((D_MODEL, tn), lambda n, k: (0, n)),
        out_shape=jax.ShapeDtypeStruct((D_MODEL, CAT_W), f32),
        compiler_params=_params(("parallel", "arbitrary")),
    )(h, d_proj)


def _adamw(w, g, m, v):
    m = ADAM_B1 * m + (1.0 - ADAM_B1) * g
    v = ADAM_B2 * v + (1.0 - ADAM_B2) * (g * g)
    m_hat = m / (1.0 - ADAM_B1 ** ADAM_STEP)
    v_hat = v / (1.0 - ADAM_B2 ** ADAM_STEP)
    delta = -ADAM_LR * (m_hat / (jnp.sqrt(v_hat) + ADAM_EPS) + ADAM_WD * w)
    return delta, m, v


def _mod_shard(c_all, w_mod_s, b_mod_s):
    def body(c_ref, w_ref, b_ref, o_ref):
        o_ref[...] = _dot(_silu(c_ref[...]), w_ref[...], NN, True) + b_ref[...]

    return pl.pallas_call(body, name="mod_shard", out_shape=jax.ShapeDtypeStruct((N_DEV, w_mod_s.shape[1]), f32),
                          compiler_params=_params())(c_all, w_mod_s, b_mod_s)


def _mod_update(c_all, d_mod_s, w, m, v):
    def body(c_ref, d_ref, w_ref, m_ref, v_ref, g_ref, dw_ref, nm_ref, nv_ref):
        g = _dot(_silu(c_ref[...]), d_ref[...], TN, True)
        g_ref[...] = g
        dw_ref[...], nm_ref[...], nv_ref[...] = _adamw(w_ref[...], g, m_ref[...], v_ref[...])

    return pl.pallas_call(body, name="mod_update", out_shape=[jax.ShapeDtypeStruct(w.shape, f32)] * 4,
                          compiler_params=_params())(c_all, d_mod_s, w, m, v)


def _adamw_rows(w, g, m, v, name):
    rows, width = w.shape
    tr = PACK_CHUNK if rows % PACK_CHUNK == 0 else rows

    def body(w_ref, g_ref, m_ref, v_ref, dw_ref, nm_ref, nv_ref):
        dw_ref[...], nm_ref[...], nv_ref[...] = _adamw(w_ref[...], g_ref[...], m_ref[...], v_ref[...])

    spec = pl.BlockSpec((tr, width), lambda i: (i, 0))
    return pl.pallas_call(body, name=name, grid=(rows // tr,), in_specs=[spec] * 4, out_specs=[spec] * 3,
                          out_shape=[jax.ShapeDtypeStruct(w.shape, f32)] * 3,
                          compiler_params=_params(("parallel",)))(w, g, m, v)


SM_NORM, SM_FIN, SM_DN, SM_AT, SM_A, SM_DT, SM_LOSS, SM_MOD = 0, 1024, 2048, 2560, 3072, 3200, 3328, 3456
SM_W = SM_MOD + 3 * D_MODEL
RS_NORM, RS_FIN, RS_DN, RS_AT, RS_A, RS_DT, RS_BMOD = 0, 1024, 2048, 2176, 2304, 2432, 2560
RS_W = RS_BMOD + 3 * D_MODEL


def _small_update(gathered, w, m, v):
    def body(g_ref, w_ref, m_ref, v_ref, grad_ref, dw_ref, nm_ref, nv_ref, loss_ref, dmod_ref):
        total = g_ref[0:1, :]
        for dev in range(1, N_DEV):
            total = total + g_ref[8 * dev:8 * dev + 1, :]
        for dev in range(N_DEV):
            dmod_ref[dev:dev + 1, :] = g_ref[8 * dev:8 * dev + 1, SM_MOD:SM_W]
        dn = total[:, SM_DN:SM_DN + DN_DIM]
        for h in range(1, DN_HEADS):
            dn = dn + total[:, SM_DN + h * DN_DIM:SM_DN + (h + 1) * DN_DIM]
        at = total[:, SM_AT:SM_AT + AT_DIM]
        for h in range(1, AT_HEADS):
            at = at + total[:, SM_AT + h * AT_DIM:SM_AT + (h + 1) * AT_DIM]
        grad_ref[:, RS_NORM:RS_FIN] = total[:, SM_NORM:SM_FIN]
        grad_ref[:, RS_FIN:RS_DN] = total[:, SM_FIN:SM_DN]
        grad_ref[:, RS_DN:RS_AT] = dn
        grad_ref[:, RS_AT:RS_A] = jnp.zeros((1, 128), f32)
        grad_ref[:, RS_AT:RS_AT + AT_DIM] = at
        grad_ref[:, RS_A:RS_DT] = total[:, SM_A:SM_DT]
        grad_ref[:, RS_DT:RS_BMOD] = total[:, SM_DT:SM_LOSS]
        grad_ref[:, RS_BMOD:RS_W] = total[:, SM_MOD:SM_W]
        loss_ref[...] = total[:, SM_LOSS:SM_MOD]
        dw_ref[...], nm_ref[...], nv_ref[...] = _adamw(w_ref[...], grad_ref[...], m_ref[...], v_ref[...])

    row = jax.ShapeDtypeStruct((1, RS_W), f32)
    return pl.pallas_call(
        body, name="small_update",
        out_shape=[row, row, row, row, jax.ShapeDtypeStruct((1, 128), f32), jax.ShapeDtypeStruct((N_DEV, 3 * D_MODEL), f32)],
        compiler_params=_params())(gathered, w, m, v)


def _all_gather_rows(block, name):
    m_per, n = block.shape

    def body(x_ref, out_ref, send_sems, recv_sems, local_sem):
        x, y, c = lax.axis_index("x"), lax.axis_index("y"), lax.axis_index("c")
        me, sibling = (x, y, c), (x, y, 1 - c)
        chips = [(1 - x, y), (x, 1 - y), (1 - x, 1 - y)]

        def rows(px, py, pc):
            return out_ref.at[pl.ds((4 * px + 2 * py + pc) * m_per, m_per), :]

        def copy(k, blk, to, src=None):
            return pltpu.make_async_remote_copy(
                src_ref=rows(*blk) if src is None else src, dst_ref=rows(*blk),
                send_sem=send_sems.at[k], recv_sem=recv_sems.at[k], device_id=to, device_id_type=MESH)

        mine = pltpu.make_async_copy(x_ref, rows(*me), local_sem)
        mine.start()
        first = [copy(0, me, sibling, src=x_ref)]
        first += [copy(1 + j, me, (*chip, c), src=x_ref) for j, chip in enumerate(chips)]
        for cp in first:
            cp.start()
        passed = [copy(4 + j, (*chip, c), sibling) for j, chip in enumerate(chips)]
        for j, chip in enumerate(chips):
            copy(1 + j, (*chip, c), me).wait_recv()
            passed[j].start()
        copy(0, sibling, me).wait_recv()
        for j, chip in enumerate(chips):
            copy(4 + j, (*chip, 1 - c), me).wait_recv()
        for cp in first + passed:
            cp.wait_send()
        mine.wait()

    return pl.pallas_call(
        body, name=name,
        out_shape=jax.ShapeDtypeStruct((N_DEV * m_per, n), block.dtype),
        in_specs=[pl.BlockSpec(memory_space=pltpu.VMEM)],
        out_specs=pl.BlockSpec(memory_space=pltpu.VMEM),
        scratch_shapes=[pltpu.SemaphoreType.DMA((7,)), pltpu.SemaphoreType.DMA((7,)), pltpu.SemaphoreType.DMA],
        compiler_params=pltpu.CompilerParams(vmem_limit_bytes=VMEM_LIMIT),
    )(block)


def _gather_weight_shards(packed):
    rows, width = packed.shape

    def body(src_ref, out_ref, send_sems, recv_sems, local_sem):
        x, y, c = lax.axis_index("x"), lax.axis_index("y"), lax.axis_index("c")
        chips = [(1 - x, y), (x, 1 - y), (1 - x, 1 - y)]

        def copy(k, owner, to):
            slot = out_ref.at[2 * owner[0] + owner[1]]
            return pltpu.make_async_remote_copy(
                src_ref=src_ref, dst_ref=slot, send_sem=send_sems.at[k], recv_sem=recv_sems.at[k],
                device_id=(*to, c), device_id_type=MESH)

        mine = pltpu.make_async_copy(src_ref, out_ref.at[2 * x + y], local_sem)
        mine.start()
        sends = [copy(k, (x, y), chip) for k, chip in enumerate(chips)]
        for cp in sends:
            cp.start()
        for k, chip in enumerate(chips):
            copy(k, chip, (x, y)).wait_recv()
        for cp in sends:
            cp.wait_send()
        mine.wait()

    return pl.pallas_call(
        body, name="gather_weight_shards",
        out_shape=jax.ShapeDtypeStruct((N_CHIPS, rows, width), packed.dtype),
        in_specs=[pl.BlockSpec(memory_space=pl.ANY)],
        out_specs=pl.BlockSpec(memory_space=pl.ANY),
        scratch_shapes=[pltpu.SemaphoreType.DMA((3,)), pltpu.SemaphoreType.DMA((3,)), pltpu.SemaphoreType.DMA],
    )(packed)


def _reduce_weight_grads(grads):
    _, rows, width = grads.shape
    n_chunks = rows // PACK_CHUNK

    def body(g_ref, out_ref, own_ref, land_ref, part_ref, sib_ref, send_sems, recv_sems, local_sem):
        x, y, c = lax.axis_index("x"), lax.axis_index("y"), lax.axis_index("c")
        chips = [(1 - x, y), (x, 1 - y), (1 - x, 1 - y)]

        def copy(k, chip):
            return pltpu.make_async_remote_copy(
                src_ref=g_ref.at[2 * chip[0] + chip[1]], dst_ref=land_ref.at[k],
                send_sem=send_sems.at[k], recv_sem=recv_sems.at[k], device_id=(*chip, c), device_id_type=MESH)

        mine = pltpu.make_async_copy(g_ref.at[2 * x + y], own_ref, local_sem)
        mine.start()
        sends = [copy(k, chip) for k, chip in enumerate(chips)]
        for cp in sends:
            cp.start()
        mine.wait()
        for cp in sends:
            cp.wait_recv()

        def add_landed(i, carry):
            r = pl.ds(pl.multiple_of(i * PACK_CHUNK, 8), PACK_CHUNK)
            part_ref[r, :] = ((own_ref[r, :] + land_ref[0, r, :]) + land_ref[1, r, :]) + land_ref[2, r, :]
            return carry

        lax.fori_loop(0, n_chunks, add_landed, 0)
        swap = pltpu.make_async_remote_copy(
            src_ref=part_ref, dst_ref=sib_ref, send_sem=send_sems.at[3], recv_sem=recv_sems.at[3],
            device_id=(x, y, 1 - c), device_id_type=MESH)
        swap.start()
        swap.wait_recv()

        def add_sibling(i, carry):
            r = pl.ds(pl.multiple_of(i * PACK_CHUNK, 8), PACK_CHUNK)
            out_ref[r, :] = part_ref[r, :] + sib_ref[r, :]
            return carry

        lax.fori_loop(0, n_chunks, add_sibling, 0)
        swap.wait_send()
        for cp in sends:
            cp.wait_send()

    buf = pltpu.VMEM((rows, width), f32)
    return pl.pallas_call(
        body, name="reduce_weight_grads",
        out_shape=jax.ShapeDtypeStruct((rows, width), f32),
        in_specs=[pl.BlockSpec(memory_space=pl.ANY)],
        out_specs=pl.BlockSpec(memory_space=pltpu.VMEM),
        scratch_shapes=[buf, pltpu.VMEM((3, rows, width), f32), buf, buf,
                        pltpu.SemaphoreType.DMA((4,)), pltpu.SemaphoreType.DMA((4,)), pltpu.SemaphoreType.DMA],
        compiler_params=pltpu.CompilerParams(vmem_limit_bytes=VMEM_LIMIT),
    )(grads)


def _pack_shards(w_in_s, w_out_s, conv_s):
    lead = w_in_s.shape[:-2]
    conv_rows = jnp.pad(conv_s.reshape(*lead, CONV_K * 384), [(0, 0)] * len(lead) + [(0, 8 * D_MODEL - CONV_K * 384)])
    return jnp.concatenate([
        w_in_s.reshape(*lead, SHARD_IN, D_MODEL), w_out_s, conv_rows.reshape(*lead, 8, D_MODEL),
        jnp.zeros((*lead, PACK_ROWS - SHARD_IN - 256 - 8, D_MODEL), f32)], axis=-2)


def _unpack_shards(packed):
    lead = packed.shape[:-2]
    w_in_s = packed[..., :SHARD_IN, :].reshape(*lead, D_MODEL, SHARD_IN)
    w_out_s = packed[..., SHARD_IN:SHARD_IN + 256, :]
    conv_s = packed[..., SHARD_IN + 256:SHARD_IN + 264, :].reshape(*lead, 8 * D_MODEL)[..., :CONV_K * 384]
    return w_in_s, w_out_s, conv_s.reshape(*lead, CONV_K, 384)


def _local_step(x, target, pos_col, mod_row, norm_w, w_in, conv_w, a_log, dt_bias, dn_norm_w, at_norm_w, w_out, fin_w):
    shift, scale, gate = mod_row[:, :D_MODEL], mod_row[:, D_MODEL:2 * D_MODEL], mod_row[:, 2 * D_MODEL:]
    half = AT_DIM // 2
    lane = jnp.arange(128)
    inv_freq = ROPE_THETA ** (-jnp.arange(half, dtype=f32) / half)
    invf_row = inv_freq[lane % half].reshape(1, 128)
    sign_row = jnp.where((lane % AT_DIM) < half, -1.0, 1.0).astype(f32).reshape(1, 128)
    ba_w = jnp.pad(w_in[:, 2 * D_MODEL:2 * D_MODEL + 2 * DN_HEADS], ((0, 0), (0, BA_W - 2 * DN_HEADS)))
    w_cat = jnp.concatenate([w_in[:, :2 * D_MODEL], ba_w, w_in[:, 2 * D_MODEL + 2 * DN_HEADS:]], axis=1).astype(bf16)
    conv_w8 = jnp.pad(conv_w, ((0, 8 - CONV_K), (0, 0)))
    a_row = jnp.pad(a_log.reshape(1, DN_HEADS), ((0, 0), (DN_HEADS, BA_W - 2 * DN_HEADS)))
    dt_row = jnp.pad(dt_bias.reshape(1, DN_HEADS), ((0, 0), (DN_HEADS, BA_W - 2 * DN_HEADS)))
    dn_w_row = jnp.tile(dn_norm_w.reshape(1, DN_DIM), (1, DN_HEADS))
    at_w_row = jnp.tile(at_norm_w.reshape(1, AT_DIM), (1, AT_HEADS))
    norm_row = norm_w.reshape(1, D_MODEL)
    fin_row = fin_w.reshape(1, D_MODEL)
    w_out_bf = w_out.astype(bf16)

    pdn, pat, z_at, h = _pre_proj(x, pos_col, invf_row, sign_row, norm_row, scale, shift, w_cat)
    dn_parts = _dn_intra_forward(pdn, conv_w8, a_row, dt_row)
    o_dn, states = _dn_scan_forward(*dn_parts)
    views = [_to_view(pat, d) for d in PATTERN_DILATIONS]
    outs, lses = [], []
    for d, view in zip(PATTERN_DILATIONS, views):
        o, lse = _attn_forward(view, d)
        outs.append(_from_view(o, d))
        lses.append(_from_view(lse, d))
    (dx2, d_odn, d_zdn, d_oat, delta, lse_all, d_zat, g_wtop, g_wbot, tail_rows) = _tail(
        x, target, o_dn, pdn, outs, lses, z_at, gate, w_out_bf, dn_w_row, at_w_row, fin_row)
    d_q, d_k, d_v = [], [], []
    for d, view in zip(PATTERN_DILATIONS, views):
        args = (_to_view(d_oat, d), _to_view(delta, d), _to_view(lse_all, d))
        d_q.append(_from_view(_attn_backward_q(view, *args, d), d))
        dk, dv = _attn_backward_kv(view, *args, d)
        d_k.append(_from_view(dk, d))
        d_v.append(_from_view(dv, d))
    d_parts = _dn_scan_backward(*dn_parts, states, d_odn)
    d_conv, d_ba, dn_rows = _dn_intra_backward(pdn, conv_w8, a_row, dt_row, *d_parts)
    grad_x, d_proj, pre_rows, conv_rows = _pre_backward(
        x, dx2, pos_col, invf_row, sign_row, norm_row, scale, shift, w_cat, conv_w8, pdn, d_conv, d_zdn, d_ba,
        d_q, d_k, d_v, d_zat)
    g_cat = _weight_grad(h, d_proj)
    g_w_in = jnp.concatenate([g_cat[:, :2 * D_MODEL], g_cat[:, 2 * D_MODEL:2 * D_MODEL + 2 * DN_HEADS],
                              g_cat[:, 2 * D_MODEL + BA_W:]], axis=1)
    g_w_out = jnp.concatenate([g_wtop, g_wbot], axis=0)
    g_conv = conv_rows[:CONV_K]
    small = jnp.concatenate([
        pre_rows[0:1], tail_rows[2:3], tail_rows[3:4, :DN_WIDTH], tail_rows[4:5, :AT_WIDTH],
        dn_rows[0:1], dn_rows[1:2], tail_rows[0:1, :128],
        pre_rows[2:3], pre_rows[1:2], tail_rows[1:2]], axis=1)
    return grad_x, g_w_in, g_w_out, g_conv, small


def kernel(x, c, positions, w_mod, b_mod, norm_w, w_in, conv_w, a_log, dt_bias, dn_norm_w, at_norm_w, w_out, final_norm_w, loss_target, m_w_mod, m_b_mod, m_norm_w, m_w_in, m_conv_w, m_a_log, m_dt_bias, m_dn_norm_w, m_at_norm_w, m_w_out, m_final_norm_w, v_w_mod, v_b_mod, v_norm_w, v_w_in, v_conv_w, v_a_log, v_dt_bias, v_dn_norm_w, v_at_norm_w, v_w_out, v_final_norm_w):
    seq = x.shape[1]
    ax, ay, ac = lax.axis_index("x"), lax.axis_index("y"), lax.axis_index("c")
    me = 4 * ax + 2 * ay + ac
    chip = 2 * ax + ay

    c_all = _all_gather_rows(jnp.pad(c, ((0, 7), (0, 0))), "gather_c").reshape(N_DEV, 8, D_MODEL)[:, 0]
    b_mod_s = lax.dynamic_slice_in_dim(b_mod, chip * 768, 768, axis=1)
    mod_part = _mod_shard(c_all, w_mod[0], b_mod_s)
    mod_all = _all_gather_rows(mod_part, "gather_mod").reshape(N_CHIPS, 2, N_DEV, 768)[:, 0]
    mod_row = lax.dynamic_index_in_dim(mod_all, me, axis=1, keepdims=False).reshape(1, 3 * D_MODEL)

    shards = _gather_weight_shards(_pack_shards(w_in[0], w_out[0], conv_w[0]))
    w_in_s, w_out_s, conv_s = _unpack_shards(shards)
    w_in_full = jnp.transpose(w_in_s, (1, 0, 2)).reshape(D_MODEL, IN_COLS)
    w_out_full = w_out_s.reshape(D_MODEL, D_MODEL)
    conv_full = jnp.transpose(conv_s, (1, 0, 2)).reshape(CONV_K, QKV_W)

    grad_x, g_w_in, g_w_out, g_conv, small = _local_step(
        x[0], loss_target[0], positions.reshape(seq, 1), mod_row, norm_w, w_in_full, conv_full, a_log, dt_bias,
        dn_norm_w, at_norm_w, w_out_full, final_norm_w)

    g_pack = _pack_shards(jnp.transpose(g_w_in.reshape(D_MODEL, N_CHIPS, SHARD_IN), (1, 0, 2)),
                          g_w_out.reshape(N_CHIPS, 256, D_MODEL),
                          jnp.transpose(g_conv.reshape(CONV_K, N_CHIPS, 384), (1, 0, 2)))
    g_mine = _reduce_weight_grads(g_pack)
    d_pack, m_pack, v_pack = _adamw_rows(_pack_shards(w_in[0], w_out[0], conv_w[0]), g_mine,
                                         _pack_shards(m_w_in[0], m_w_out[0], m_conv_w[0]),
                                         _pack_shards(v_w_in[0], v_w_out[0], v_conv_w[0]), "adamw_packed")
    grad_w_in, grad_w_out, grad_conv_w = _unpack_shards(g_mine)
    delta_w_in, delta_w_out, delta_conv_w = _unpack_shards(d_pack)
    new_m_w_in, new_m_w_out, new_m_conv_w = _unpack_shards(m_pack)
    new_v_w_in, new_v_w_out, new_v_conv_w = _unpack_shards(v_pack)

    gathered = _all_gather_rows(jnp.pad(small, ((0, 7), (0, 0))), "gather_small")

    def small_row(norm, fin, dn, at, a, dt, bmod):
        z = lambda n: jnp.zeros((1, n), f32)
        return jnp.concatenate([norm.reshape(1, -1), fin.reshape(1, -1), dn.reshape(1, -1), at.reshape(1, -1), z(64),
                                z(4), a.reshape(1, -1), z(120), z(4), dt.reshape(1, -1), z(120), bmod.reshape(1, -1)], axis=1)

    g_small, d_small, m_small, v_small, loss_row, d_mod_all = _small_update(
        gathered,
        small_row(norm_w, final_norm_w, dn_norm_w, at_norm_w, a_log, dt_bias, b_mod),
        small_row(m_norm_w, m_final_norm_w, m_dn_norm_w, m_at_norm_w, m_a_log, m_dt_bias, m_b_mod),
        small_row(v_norm_w, v_final_norm_w, v_dn_norm_w, v_at_norm_w, v_a_log, v_dt_bias, v_b_mod))

    def split_small(r):
        return (r[:, RS_BMOD:RS_W], r[:, RS_NORM:RS_FIN], r[:, RS_A + DN_HEADS:RS_A + 2 * DN_HEADS],
                r[:, RS_DT + DN_HEADS:RS_DT + 2 * DN_HEADS], r[:, RS_DN:RS_AT], r[:, RS_AT:RS_AT + AT_DIM],
                r[0, RS_FIN:RS_DN])

    d_mod_s = lax.dynamic_slice_in_dim(d_mod_all, chip * 768, 768, axis=1)
    pad_rows = lambda a: jnp.pad(a, ((0, 128 - N_DEV), (0, 0)))
    grad_w_mod, delta_w_mod, new_m_w_mod, new_v_w_mod = _mod_update(
        pad_rows(c_all), pad_rows(d_mod_s), w_mod[0], m_w_mod[0], v_w_mod[0])

    def ordered(w_mod_leaf, small_row_leaf, w_in_leaf, conv_leaf, w_out_leaf):
        b, n, a, dt, dn, at, fin = split_small(small_row_leaf)
        return [w_mod_leaf[None], b, n, w_in_leaf[None], conv_leaf[None], a, dt, dn, at, w_out_leaf[None], fin]

    loss = loss_row[0, 0]
    return (loss, grad_x[None],
            *ordered(grad_w_mod, g_small, grad_w_in, grad_conv_w, grad_w_out),
            *ordered(delta_w_mod, d_small, delta_w_in, delta_conv_w, delta_w_out),
            *ordered(new_m_w_mod, m_small, new_m_w_in, new_m_conv_w, new_m_w_out),
            *ordered(new_v_w_mod, v_small, new_v_w_in, new_v_conv_w, new_v_w_out))
```

```python
import functools

import jax
import jax.numpy as jnp
from jax import lax
from jax.experimental import pallas as pl
from jax.experimental.pallas import tpu as pltpu

f32 = jnp.float32
bf16 = jnp.bfloat16
HIGHEST = lax.Precision.HIGHEST
MESH = pl.DeviceIdType.MESH

D_MODEL = 1024
DN_HEADS = 4
DN_DIM = 128
DN_WIDTH = 512
AT_HEADS = 8
AT_DIM = 64
AT_WIDTH = 512
CHUNK = 64
Q_BLOCK = 128
CONV_K = 4
EPS = 1e-6
ROPE_THETA = 10000.0
PATTERN_DILATIONS = (1, 4, 16)
NEG = -1e30

QKV_W = 3 * DN_WIDTH
BA_W = 128
PDN_W = QKV_W + DN_WIDTH + BA_W
PAT_W = 3 * AT_WIDTH
CAT_W = PDN_W + PAT_W + AT_WIDTH
IN_COLS = 4104
N_CHIPS = 4
N_DEV = 8
SHARD_IN = IN_COLS // N_CHIPS
PACK_ROWS = 1296
PACK_CHUNK = 48

ADAM_LR = 0.001
ADAM_B1 = 0.9
ADAM_B2 = 0.999
ADAM_EPS = 1e-08
ADAM_WD = 0.01
ADAM_STEP = 10

VMEM_LIMIT = 56 * 1024 * 1024

NN = ((1,), (0,))
NT = ((1,), (1,))
TN = ((0,), (0,))


def _pieces(a, n):
    out = []
    for _ in range(n - 1):
        p = a.astype(bf16)
        out.append(p)
        a = a - p.astype(f32)
    out.append(a.astype(bf16))
    return out


def _dot(a, b, dims, exact):
    raw = lambda p, q: lax.dot_general(p, q, (dims, ((), ())), preferred_element_type=f32)
    if exact == "split":
        (ah, al), (bh, bl) = _pieces(a, 2), _pieces(b, 2)
        return raw(ah, bh) + (raw(ah, bl) + raw(al, bh))
    if exact:
        return lax.dot_general(a, b, (dims, ((), ())), precision=HIGHEST, preferred_element_type=f32)
    return raw(a.astype(bf16), b.astype(bf16))


def _dot_sel(sel, b, dims, sel_side):
    raw = lambda p, q: lax.dot_general(p, q, (dims, ((), ())), preferred_element_type=f32)
    sel = sel.astype(bf16)
    parts = [raw(sel, p) if sel_side == 0 else raw(p, sel) for p in _pieces(b, 3)]
    return (parts[0] + parts[1]) + parts[2]


@jax.custom_vjp
def _sel_left(sel, b):
    return _dot_sel(sel, b, NN, 0)


def _sel_left_fwd(sel, b):
    return _dot_sel(sel, b, NN, 0), sel


def _sel_left_bwd(sel, g):
    return jnp.zeros_like(sel), _dot_sel(sel, g, TN, 0)


_sel_left.defvjp(_sel_left_fwd, _sel_left_bwd)


@jax.custom_vjp
def _sel_right(a, sel):
    return _dot_sel(sel, a, NN, 1)


def _sel_right_fwd(a, sel):
    return _dot_sel(sel, a, NN, 1), sel


def _sel_right_bwd(sel, g):
    return _dot_sel(sel, g, NT, 1), jnp.zeros_like(sel)


_sel_right.defvjp(_sel_right_fwd, _sel_right_bwd)


class _Matmuls:
    def __init__(self, exact):
        @jax.custom_vjp
        def nn(a, b):
            return _dot(a, b, NN, exact)

        def nn_fwd(a, b):
            return _dot(a, b, NN, exact), (a, b)

        def nn_bwd(res, g):
            a, b = res
            return _dot(g, b, NT, exact), _dot(a, g, TN, exact)

        nn.defvjp(nn_fwd, nn_bwd)

        @jax.custom_vjp
        def nt(a, b):
            return _dot(a, b, NT, exact)

        def nt_fwd(a, b):
            return _dot(a, b, NT, exact), (a, b)

        def nt_bwd(res, g):
            a, b = res
            return _dot(g, b, NN, exact), _dot(g, a, TN, exact)

        nt.defvjp(nt_fwd, nt_bwd)

        @jax.custom_vjp
        def tn(a, b):
            return _dot(a, b, TN, exact)

        def tn_fwd(a, b):
            return _dot(a, b, TN, exact), (a, b)

        def tn_bwd(res, g):
            a, b = res
            return _dot(b, g, NT, exact), _dot(a, g, NN, exact)

        tn.defvjp(tn_fwd, tn_bwd)
        self.nn, self.nt, self.tn = nn, nt, tn


MM = _Matmuls(exact=False)
MX = _Matmuls(exact=True)
MS = _Matmuls(exact="split")


def _params(semantics=None):
    return pltpu.CompilerParams(dimension_semantics=semantics, vmem_limit_bytes=VMEM_LIMIT)


def _silu(x):
    return x * jax.nn.sigmoid(x)


def _group_ones(width, group):
    r = lax.broadcasted_iota(jnp.int32, (width, width), 0) // group
    c = lax.broadcasted_iota(jnp.int32, (width, width), 1) // group
    return (r == c).astype(f32)


def _hnorm(x, nw, scale, shift):
    xn = x * lax.rsqrt(jnp.mean(x * x, axis=-1, keepdims=True) + EPS)
    return xn * nw * (1.0 + scale) + shift


def _rope_tables(pos_col, invf_row, sign_row):
    ang = pos_col.astype(f32) * invf_row
    cos_t = jnp.cos(ang)
    sin_t = jnp.sin(ang) * sign_row
    return jnp.concatenate([cos_t] * 4, axis=1), jnp.concatenate([sin_t] * 4, axis=1)


def _rope_partner(x):
    lane = lax.broadcasted_iota(jnp.int32, x.shape, 1)
    width = x.shape[1]
    return jnp.where((lane % AT_DIM) < AT_DIM // 2, pltpu.roll(x, width - AT_DIM // 2, 1), pltpu.roll(x, AT_DIM // 2, 1))


def _pre_proj(x, pos_col, invf_row, sign_row, norm_w, scale, shift, w_cat):
    seq = x.shape[0]
    tm = 256

    def body(x_ref, pos_ref, invf_ref, sign_ref, nw_ref, sc_ref, sh_ref, w_ref, pdn_ref, pat_ref, zat_ref, h_ref):
        h = _hnorm(x_ref[...], nw_ref[...], sc_ref[...], sh_ref[...]).astype(bf16)
        h_ref[...] = h
        big = jnp.dot(h, w_ref[...], preferred_element_type=f32)
        pdn_ref[...] = big[:, :PDN_W]
        cos_t, sin_t = _rope_tables(pos_ref[...], invf_ref[...], sign_ref[...])
        q = big[:, PDN_W:PDN_W + AT_WIDTH]
        k = big[:, PDN_W + AT_WIDTH:PDN_W + 2 * AT_WIDTH]
        pat_ref[:, 0:AT_WIDTH] = q * cos_t + _rope_partner(q) * sin_t
        pat_ref[:, AT_WIDTH:2 * AT_WIDTH] = k * cos_t + _rope_partner(k) * sin_t
        pat_ref[:, 2 * AT_WIDTH:] = big[:, PDN_W + 2 * AT_WIDTH:PDN_W + PAT_W]
        zat_ref[...] = big[:, PDN_W + PAT_W:]

    row = lambda w: pl.BlockSpec((1, w), lambda i: (0, 0))
    tile = lambda w: pl.BlockSpec((tm, w), lambda i: (i, 0))
    return pl.pallas_call(
        body, name="pre_proj", grid=(seq // tm,),
        in_specs=[tile(D_MODEL), tile(1), row(128), row(128), row(D_MODEL), row(D_MODEL), row(D_MODEL),
                  pl.BlockSpec((D_MODEL, CAT_W), lambda i: (0, 0))],
        out_specs=[tile(PDN_W), tile(PAT_W), tile(AT_WIDTH), tile(D_MODEL)],
        out_shape=[jax.ShapeDtypeStruct((seq, PDN_W), f32), jax.ShapeDtypeStruct((seq, PAT_W), f32),
                   jax.ShapeDtypeStruct((seq, AT_WIDTH), f32), jax.ShapeDtypeStruct((seq, D_MODEL), bf16)],
        compiler_params=_params(("parallel",)),
    )(x, pos_col, invf_row, sign_row, norm_w, scale, shift, w_cat)


def _conv_taps(ext_ref, halo, cur, w8):
    rows = cur.shape[0]
    ext_ref[0:8, :] = halo
    ext_ref[8:8 + rows, :] = cur
    out = ext_ref[pl.ds(5, rows), :] * w8[0:1, :]
    for j in range(1, CONV_K):
        out = out + ext_ref[pl.ds(5 + j, rows), :] * w8[j:j + 1, :]
    return out


def _dn_pre(cq, ck, cv, ba, a_row, dt_row):
    ones = _group_ones(DN_WIDTH, DN_DIM)
    sq, sk, v = _silu(cq), _silu(ck), _silu(cv)
    qn = sq * lax.rsqrt(_sel_right(sq * sq, ones) + EPS)
    kn = sk * lax.rsqrt(_sel_right(sk * sk, ones) + EPS)
    beta_all = jax.nn.sigmoid(ba)
    g_all = -jnp.exp(a_row) * jax.nn.softplus(ba + dt_row)
    return qn, kn, v, beta_all, g_all


def _dn_intra(qs, ks, vs, betas, gs):
    ri = lax.broadcasted_iota(jnp.int32, (CHUNK, CHUNK), 0)
    ci = lax.broadcasted_iota(jnp.int32, (CHUNK, CHUNK), 1)
    tril = ri >= ci
    strict = ri > ci
    lower = tril.astype(f32)
    eye = (ri == ci).astype(f32)
    each = lambda fn, *lists: [fn(*args) for args in zip(*lists)]
    g_wide = each(lambda g: jnp.broadcast_to(g, (CHUNK, DN_DIM)), gs)
    gc = each(lambda g: _sel_left(lower, g), g_wide)
    gc_sq = each(lambda g: _sel_left(lower, jnp.broadcast_to(g, (CHUNK, CHUNK))), gs)
    g_end = each(lambda g: _sel_left(jnp.ones((CHUNK, CHUNK), f32), g), g_wide)
    g_end8 = each(lambda g: _sel_left(jnp.ones((8, CHUNK), f32), g), g_wide)
    decay = each(lambda s: jnp.exp(jnp.where(tril, s - s.T, -jnp.inf)), gc_sq)
    qs = each(lambda q: q * (DN_DIM ** -0.5), qs)
    kb = each(lambda k, b: k * b, ks, betas)
    vb = each(lambda v, b: v * b, vs, betas)
    kk = each(MM.nt, kb, ks)
    qk = each(MM.nt, qs, ks)
    power = each(lambda p, d: -jnp.where(strict, p * d, 0.0), kk, decay)
    inv = each(lambda p: eye + p, power)
    for _ in range(5):
        power = each(MS.nn, power, power)
        inv = each(lambda x, p: x + MS.nn(x, p), inv, power)
    e_gc = each(jnp.exp, gc)
    u = each(MS.nn, inv, vb)
    w = each(lambda x, k, e: MS.nn(x, k * e), inv, kb, e_gc)
    attn = each(lambda p, d: jnp.where(tril, p * d, 0.0), qk, decay)
    q_dec = each(lambda q, e: q * e, qs, e_gc)
    k_dec = each(lambda k, ge, c: k * jnp.exp(ge - c), ks, g_end, gc)
    return u, w, q_dec, k_dec, attn, each(jnp.exp, g_end8)


def _dn_step(us, ws, q_decs, k_decs, attns, e_ends, states):
    each = lambda fn, *lists: [fn(*args) for args in zip(*lists)]
    v_new = each(lambda u, w, s: u - MM.nn(w, s), us, ws, states)
    qs = each(MM.nn, q_decs, states)
    o = each(lambda a, b, c: a + MM.nn(b, c), qs, attns, v_new)
    new_states = each(lambda s, e, k, v: s * e + MM.tn(k, v), states, e_ends, k_decs, v_new)
    return o, new_states


INTRA_CHUNKS = 2
SCAN_CHUNKS = 4


def _intra_specs(nc):
    rows = INTRA_CHUNKS * CHUNK
    cur = pl.BlockSpec((rows, QKV_W), lambda i: (i, 0))
    halo = pl.BlockSpec((8, QKV_W), lambda i: (jnp.maximum(i * (rows // 8) - 1, 0), 0))
    ba = pl.BlockSpec((rows, BA_W), lambda i: (i, (QKV_W + DN_WIDTH) // BA_W))
    conv = pl.BlockSpec((8, QKV_W), lambda i: (0, 0))
    row = pl.BlockSpec((1, BA_W), lambda i: (0, 0))
    wide = pl.BlockSpec((rows, DN_WIDTH), lambda i: (i, 0))
    attn = pl.BlockSpec((INTRA_CHUNKS, DN_HEADS, CHUNK, CHUNK), lambda i: (i, 0, 0, 0))
    e_end = pl.BlockSpec((INTRA_CHUNKS, 8 * DN_HEADS, DN_DIM), lambda i: (i, 0, 0))
    return cur, halo, ba, conv, row, wide, attn, e_end


def _intra_items():
    return [(ci, h, slice(ci * CHUNK, (ci + 1) * CHUNK), slice(h * DN_DIM, (h + 1) * DN_DIM))
            for ci in range(INTRA_CHUNKS) for h in range(DN_HEADS)]


def _intra_inputs(items, qn, kn, v, beta_all, g_all):
    return ([qn[rows, lanes] for _, _, rows, lanes in items], [kn[rows, lanes] for _, _, rows, lanes in items],
            [v[rows, lanes] for _, _, rows, lanes in items], [beta_all[rows, h:h + 1] for _, h, rows, _ in items],
            [g_all[rows, DN_HEADS + h:DN_HEADS + h + 1] for _, h, rows, _ in items])


def _intra_shapes(seq):
    nc = seq // CHUNK
    wide = jax.ShapeDtypeStruct((seq, DN_WIDTH), f32)
    return [wide, wide, wide, wide, jax.ShapeDtypeStruct((nc, DN_HEADS, CHUNK, CHUNK), f32),
            jax.ShapeDtypeStruct((nc, 8 * DN_HEADS, DN_DIM), f32)]


def _dn_intra_forward(pdn, conv_w8, a_row, dt_row):
    seq = pdn.shape[0]
    nc = seq // CHUNK
    cur, halo, ba, conv, row, wide, attn, e_end = _intra_specs(nc)

    def body(cur_ref, halo_ref, ba_ref, w_ref, a_ref, dt_ref, u_ref, w_out_ref, qd_ref, kd_ref, attn_ref, e_ref, ext_ref):
        halo_rows = jnp.where(pl.program_id(0) > 0, halo_ref[...], 0.0)
        c = _conv_taps(ext_ref, halo_rows, cur_ref[...], w_ref[...])
        qn, kn, v, beta_all, g_all = _dn_pre(c[:, :DN_WIDTH], c[:, DN_WIDTH:2 * DN_WIDTH], c[:, 2 * DN_WIDTH:],
                                             ba_ref[...], a_ref[...], dt_ref[...])
        items = _intra_items()
        u, w, qd, kd, at, e8 = _dn_intra(*_intra_inputs(items, qn, kn, v, beta_all, g_all))
        for n, (ci, h, rows, lanes) in enumerate(items):
            u_ref[rows, lanes] = u[n]
            w_out_ref[rows, lanes] = w[n]
            qd_ref[rows, lanes] = qd[n]
            kd_ref[rows, lanes] = kd[n]
            attn_ref[ci, h] = at[n]
            e_ref[ci, 8 * h:8 * h + 8, :] = e8[n]

    return pl.pallas_call(
        body, name="dn_intra_forward", grid=(nc // INTRA_CHUNKS,),
        in_specs=[cur, halo, ba, conv, row, row],
        out_specs=[wide, wide, wide, wide, attn, e_end],
        out_shape=_intra_shapes(seq),
        scratch_shapes=[pltpu.VMEM((INTRA_CHUNKS * CHUNK + 8, QKV_W), f32)],
        compiler_params=_params(("parallel",)),
    )(pdn, pdn, pdn, conv_w8, a_row, dt_row)


def _scan_specs(nc, reverse):
    steps = nc // SCAN_CHUNKS
    at = (lambda i: steps - 1 - i) if reverse else (lambda i: i)
    wide = pl.BlockSpec((SCAN_CHUNKS * CHUNK, DN_WIDTH), lambda i: (at(i), 0))
    attn = pl.BlockSpec((SCAN_CHUNKS, DN_HEADS, CHUNK, CHUNK), lambda i: (at(i), 0, 0, 0))
    e_end = pl.BlockSpec((SCAN_CHUNKS, 8 * DN_HEADS, DN_DIM), lambda i: (at(i), 0, 0))
    states = pl.BlockSpec((SCAN_CHUNKS, DN_HEADS, DN_DIM, DN_DIM), lambda i: (at(i), 0, 0, 0))
    return wide, attn, e_end, states


def _step_inputs(ci, rows, lanes, u_ref, w_ref, qd_ref, kd_ref, attn_ref, e_ref):
    heads = range(DN_HEADS)
    return ([u_ref[rows, lanes[h]] for h in heads], [w_ref[rows, lanes[h]] for h in heads],
            [qd_ref[rows, lanes[h]] for h in heads], [kd_ref[rows, lanes[h]] for h in heads],
            [attn_ref[ci, h] for h in heads], [e_ref[ci, 8 * h:8 * h + 1, :] for h in heads])


def _dn_scan_forward(u, w, q_dec, k_dec, attn, e_end):
    seq = u.shape[0]
    nc = seq // CHUNK
    wide, attn_spec, e_spec, st_spec = _scan_specs(nc, reverse=False)

    def body(u_ref, w_ref, qd_ref, kd_ref, attn_ref, e_ref, o_ref, st_ref, state_ref):
        @pl.when(pl.program_id(0) == 0)
        def _():
            state_ref[...] = jnp.zeros_like(state_ref)

        heads = range(DN_HEADS)
        lanes = [slice(h * DN_DIM, (h + 1) * DN_DIM) for h in heads]
        states = [state_ref[h] for h in heads]
        for ci in range(SCAN_CHUNKS):
            rows = slice(ci * CHUNK, (ci + 1) * CHUNK)
            for h in heads:
                st_ref[ci, h] = states[h]
            o, states = _dn_step(*_step_inputs(ci, rows, lanes, u_ref, w_ref, qd_ref, kd_ref, attn_ref, e_ref), states)
            for h in heads:
                o_ref[rows, lanes[h]] = o[h]
        for h in heads:
            state_ref[h] = states[h]

    return pl.pallas_call(
        body, name="dn_scan_forward", grid=(nc // SCAN_CHUNKS,),
        in_specs=[wide, wide, wide, wide, attn_spec, e_spec],
        out_specs=[wide, st_spec],
        out_shape=[jax.ShapeDtypeStruct((seq, DN_WIDTH), f32), jax.ShapeDtypeStruct((nc, DN_HEADS, DN_DIM, DN_DIM), f32)],
        scratch_shapes=[pltpu.VMEM((DN_HEADS, DN_DIM, DN_DIM), f32)],
        compiler_params=_params(("arbitrary",)),
    )(u, w, q_dec, k_dec, attn, e_end)


def _dn_scan_backward(u, w, q_dec, k_dec, attn, e_end, states, d_o):
    seq = u.shape[0]
    nc = seq // CHUNK
    wide, attn_spec, e_spec, st_spec = _scan_specs(nc, reverse=True)

    def body(u_ref, w_ref, qd_ref, kd_ref, attn_ref, e_ref, st_ref, do_ref,
             du_ref, dw_ref, dqd_ref, dkd_ref, dattn_ref, de_ref, dstate_ref):
        @pl.when(pl.program_id(0) == 0)
        def _():
            dstate_ref[...] = jnp.zeros_like(dstate_ref)

        heads = range(DN_HEADS)
        lanes = [slice(h * DN_DIM, (h + 1) * DN_DIM) for h in heads]
        first_row = lax.broadcasted_iota(jnp.int32, (8, DN_DIM), 0) == 0
        dstates = [dstate_ref[h] for h in heads]
        for ci in reversed(range(SCAN_CHUNKS)):
            rows = slice(ci * CHUNK, (ci + 1) * CHUNK)
            _, step_vjp = jax.vjp(_dn_step, *_step_inputs(ci, rows, lanes, u_ref, w_ref, qd_ref, kd_ref, attn_ref, e_ref),
                                  [st_ref[ci, h] for h in heads])
            du, dw, dqd, dkd, dattn, de, dstates = step_vjp(([do_ref[rows, lanes[h]] for h in heads], dstates))
            for h in heads:
                du_ref[rows, lanes[h]] = du[h]
                dw_ref[rows, lanes[h]] = dw[h]
                dqd_ref[rows, lanes[h]] = dqd[h]
                dkd_ref[rows, lanes[h]] = dkd[h]
                dattn_ref[ci, h] = dattn[h]
                de_ref[ci, 8 * h:8 * h + 8, :] = jnp.where(first_row, jnp.broadcast_to(de[h], (8, DN_DIM)), 0.0)
        for h in heads:
            dstate_ref[h] = dstates[h]

    return pl.pallas_call(
        body, name="dn_scan_backward", grid=(nc // SCAN_CHUNKS,),
        in_specs=[wide, wide, wide, wide, attn_spec, e_spec, st_spec, wide],
        out_specs=[wide, wide, wide, wide, attn_spec, e_spec],
        out_shape=_intra_shapes(seq),
        scratch_shapes=[pltpu.VMEM((DN_HEADS, DN_DIM, DN_DIM), f32)],
        compiler_params=_params(("arbitrary",)),
    )(u, w, q_dec, k_dec, attn, e_end, states, d_o)


def _dn_intra_backward(pdn, conv_w8, a_row, dt_row, d_u, d_w, d_qd, d_kd, d_attn, d_e):
    seq = pdn.shape[0]
    nc = seq // CHUNK
    rows_per_step = INTRA_CHUNKS * CHUNK
    cur, halo, ba, conv, row, wide, attn, e_end = _intra_specs(nc)

    def body(cur_ref, halo_ref, ba_ref, w_ref, a_ref, dt_ref, du_ref, dw_ref, dqd_ref, dkd_ref, dattn_ref, de_ref,
             dconv_ref, dba_ref, drow_ref, ext_ref):
        @pl.when(pl.program_id(0) == 0)
        def _():
            drow_ref[...] = jnp.zeros_like(drow_ref)

        halo_rows = jnp.where(pl.program_id(0) > 0, halo_ref[...], 0.0)
        c = _conv_taps(ext_ref, halo_rows, cur_ref[...], w_ref[...])
        (qn, kn, v, beta_all, g_all), pre_vjp = jax.vjp(
            _dn_pre, c[:, :DN_WIDTH], c[:, DN_WIDTH:2 * DN_WIDTH], c[:, 2 * DN_WIDTH:], ba_ref[...], a_ref[...], dt_ref[...])
        lane = lax.broadcasted_iota(jnp.int32, (CHUNK, BA_W), 1)
        items = _intra_items()
        _, intra_vjp = jax.vjp(_dn_intra, *_intra_inputs(items, qn, kn, v, beta_all, g_all))
        dq, dk, dv, dbeta, dg = intra_vjp((
            [du_ref[rows, lanes] for _, _, rows, lanes in items], [dw_ref[rows, lanes] for _, _, rows, lanes in items],
            [dqd_ref[rows, lanes] for _, _, rows, lanes in items], [dkd_ref[rows, lanes] for _, _, rows, lanes in items],
            [dattn_ref[ci, h] for ci, h, _, _ in items], [de_ref[ci, 8 * h:8 * h + 8, :] for ci, h, _, _ in items]))
        dq_rows, dk_rows, dv_rows, dbeta_rows, dg_rows = [], [], [], [], []
        for ci in range(INTRA_CHUNKS):
            of_chunk = [n for n, item in enumerate(items) if item[0] == ci]
            d_beta_all = jnp.zeros((CHUNK, BA_W), f32)
            d_g_all = jnp.zeros((CHUNK, BA_W), f32)
            for n in of_chunk:
                h = items[n][1]
                d_beta_all = d_beta_all + jnp.where(lane == h, dbeta[n], 0.0)
                d_g_all = d_g_all + jnp.where(lane == DN_HEADS + h, dg[n], 0.0)
            dq_rows.append(jnp.concatenate([dq[n] for n in of_chunk], axis=1))
            dk_rows.append(jnp.concatenate([dk[n] for n in of_chunk], axis=1))
            dv_rows.append(jnp.concatenate([dv[n] for n in of_chunk], axis=1))
            dbeta_rows.append(d_beta_all)
            dg_rows.append(d_g_all)
        stack = lambda parts: jnp.concatenate(parts, axis=0)
        dcq, dck, dcv, dba, da_row, ddt_row = pre_vjp(
            (stack(dq_rows), stack(dk_rows), stack(dv_rows), stack(dbeta_rows), stack(dg_rows)))
        dconv_ref[:, :DN_WIDTH] = dcq
        dconv_ref[:, DN_WIDTH:2 * DN_WIDTH] = dck
        dconv_ref[:, 2 * DN_WIDTH:] = dcv
        dba_ref[...] = dba
        drow_ref[0:1, :] += da_row
        drow_ref[1:2, :] += ddt_row

    return pl.pallas_call(
        body, name="dn_intra_backward", grid=(nc // INTRA_CHUNKS,),
        in_specs=[cur, halo, ba, conv, row, row, wide, wide, wide, wide, attn, e_end],
        out_specs=[pl.BlockSpec((rows_per_step, QKV_W), lambda i: (i, 0)),
                   pl.BlockSpec((rows_per_step, BA_W), lambda i: (i, 0)),
                   pl.BlockSpec((8, BA_W), lambda i: (0, 0))],
        out_shape=[jax.ShapeDtypeStruct((seq, QKV_W), f32), jax.ShapeDtypeStruct((seq, BA_W), f32),
                   jax.ShapeDtypeStruct((8, BA_W), f32)],
        scratch_shapes=[pltpu.VMEM((rows_per_step + 8, QKV_W), f32)],
        compiler_params=_params(("arbitrary",)),
    )(pdn, pdn, pdn, conv_w8, a_row, dt_row, d_u, d_w, d_qd, d_kd, d_attn, d_e)


def _band_masks(block_index):
    qi = lax.broadcasted_iota(jnp.int32, (Q_BLOCK, Q_BLOCK), 0)
    kj = lax.broadcasted_iota(jnp.int32, (Q_BLOCK, Q_BLOCK), 1)
    return (kj >= qi) & (block_index > 0), kj <= qi


def _attn_forward(pat_view, dil):
    length = pat_view.shape[0]
    nb = length // Q_BLOCK
    scale = AT_DIM ** -0.5

    def body(q_ref, kp_ref, kc_ref, vp_ref, vc_ref, o_ref, lse_ref):
        mask_prev, mask_cur = _band_masks(pl.program_id(1))
        heads = [slice(h * AT_DIM, (h + 1) * AT_DIM) for h in range(AT_HEADS)]
        s_prev = [_dot(q_ref[:, lanes], kp_ref[:, lanes], NT, False) for lanes in heads]
        s_cur = [_dot(q_ref[:, lanes], kc_ref[:, lanes], NT, False) for lanes in heads]
        p_prev, p_cur, norm = [], [], []
        for h, lanes in enumerate(heads):
            sp = jnp.where(mask_prev, s_prev[h] * scale, NEG)
            sc = jnp.where(mask_cur, s_cur[h] * scale, NEG)
            m = jnp.maximum(jnp.max(sp, axis=1, keepdims=True), jnp.max(sc, axis=1, keepdims=True))
            pp = jnp.exp(sp - m)
            pc = jnp.exp(sc - m)
            l = jnp.sum(pp, axis=1, keepdims=True) + jnp.sum(pc, axis=1, keepdims=True)
            lse_ref[:, lanes] = jnp.broadcast_to(m + jnp.log(l), (Q_BLOCK, AT_DIM))
            p_prev.append(pp.astype(bf16))
            p_cur.append(pc.astype(bf16))
            norm.append(1.0 / l)
        o_prev = [_dot(p_prev[h], vp_ref[:, lanes], NN, False) for h, lanes in enumerate(heads)]
        o_cur = [_dot(p_cur[h], vc_ref[:, lanes], NN, False) for h, lanes in enumerate(heads)]
        for h, lanes in enumerate(heads):
            o_ref[:, lanes] = (o_prev[h] + o_cur[h]) * norm[h]

    def blk(piece, prev):
        if prev:
            return pl.BlockSpec((Q_BLOCK, AT_WIDTH), lambda r, n: (jnp.maximum(n - 1, 0), 3 * r + piece))
        return pl.BlockSpec((Q_BLOCK, AT_WIDTH), lambda r, n: (n, 3 * r + piece))

    out = pl.BlockSpec((Q_BLOCK, AT_WIDTH), lambda r, n: (n, r))
    return pl.pallas_call(
        body, name=f"attn_forward_d{dil}", grid=(dil, nb),
        in_specs=[blk(0, False), blk(1, True), blk(1, False), blk(2, True), blk(2, False)],
        out_specs=[out, out],
        out_shape=[jax.ShapeDtypeStruct((length, dil * AT_WIDTH), f32)] * 2,
        compiler_params=_params(("parallel", "parallel")),
    )(pat_view, pat_view, pat_view, pat_view, pat_view)


def _attn_backward_q(pat_view, d_out, delta, lse, dil):
    length = pat_view.shape[0]
    nb = length // Q_BLOCK
    scale = AT_DIM ** -0.5

    def body(q_ref, kp_ref, kc_ref, vp_ref, vc_ref, do_ref, dl_ref, lse_ref, dq_ref):
        mask_prev, mask_cur = _band_masks(pl.program_id(1))
        heads = [slice(h * AT_DIM, (h + 1) * AT_DIM) for h in range(AT_HEADS)]
        sides = ((mask_prev, kp_ref, vp_ref), (mask_cur, kc_ref, vc_ref))
        s = [[_dot(q_ref[:, lanes], k_ref[:, lanes], NT, False) for lanes in heads] for _, k_ref, _ in sides]
        dp = [[_dot(do_ref[:, lanes], v_ref[:, lanes], NT, False) for lanes in heads] for _, _, v_ref in sides]
        ds = [[], []]
        for h, lanes in enumerate(heads):
            lse_h = lse_ref[:, h * AT_DIM:h * AT_DIM + 1]
            dl_h = dl_ref[:, h * AT_DIM:h * AT_DIM + 1]
            for side, (mask, _, _) in enumerate(sides):
                p = jnp.exp(jnp.where(mask, s[side][h] * scale - lse_h, NEG))
                ds[side].append((p * (dp[side][h] - dl_h)).astype(bf16))
        dq = [[_dot(ds[side][h], k_ref[:, lanes], NN, False) for h, lanes in enumerate(heads)]
              for side, (_, k_ref, _) in enumerate(sides)]
        for h, lanes in enumerate(heads):
            dq_ref[:, lanes] = (dq[0][h] + dq[1][h]) * scale

    def blk(piece, prev):
        if prev:
            return pl.BlockSpec((Q_BLOCK, AT_WIDTH), lambda r, n: (jnp.maximum(n - 1, 0), 3 * r + piece))
        return pl.BlockSpec((Q_BLOCK, AT_WIDTH), lambda r, n: (n, 3 * r + piece))

    one = pl.BlockSpec((Q_BLOCK, AT_WIDTH), lambda r, n: (n, r))
    return pl.pallas_call(
        body, name=f"attn_backward_q_d{dil}", grid=(dil, nb),
        in_specs=[blk(0, False), blk(1, True), blk(1, False), blk(2, True), blk(2, False), one, one, one],
        out_specs=one,
        out_shape=jax.ShapeDtypeStruct((length, dil * AT_WIDTH), f32),
        compiler_params=_params(("parallel", "parallel")),
    )(pat_view, pat_view, pat_view, pat_view, pat_view, d_out, delta, lse)


def _attn_backward_kv(pat_view, d_out, delta, lse, dil):
    length = pat_view.shape[0]
    nb = length // Q_BLOCK
    scale = AT_DIM ** -0.5

    def body(k_ref, v_ref, qa_ref, qb_ref, doa_ref, dob_ref, dla_ref, dlb_ref, lsea_ref, lseb_ref, dk_ref, dv_ref):
        j = pl.program_id(1)
        qi = lax.broadcasted_iota(jnp.int32, (Q_BLOCK, Q_BLOCK), 0)
        kj = lax.broadcasted_iota(jnp.int32, (Q_BLOCK, Q_BLOCK), 1)
        mask_same = kj <= qi
        mask_next = (kj >= qi) & (j + 1 < nb)
        heads = [slice(h * AT_DIM, (h + 1) * AT_DIM) for h in range(AT_HEADS)]
        sides = ((mask_same, qa_ref, doa_ref, dla_ref, lsea_ref), (mask_next, qb_ref, dob_ref, dlb_ref, lseb_ref))
        s = [[_dot(q_ref[:, lanes], k_ref[:, lanes], NT, False) for lanes in heads] for _, q_ref, _, _, _ in sides]
        dp = [[_dot(do_ref[:, lanes], v_ref[:, lanes], NT, False) for lanes in heads] for _, _, do_ref, _, _ in sides]
        p, ds = [[], []], [[], []]
        for h, lanes in enumerate(heads):
            one = slice(h * AT_DIM, h * AT_DIM + 1)
            for side, (mask, _, _, dl_ref, lse_ref) in enumerate(sides):
                prob = jnp.exp(jnp.where(mask, s[side][h] * scale - lse_ref[:, one], NEG))
                p[side].append(prob.astype(bf16))
                ds[side].append((prob * (dp[side][h] - dl_ref[:, one])).astype(bf16))
        dv = [[_dot(p[side][h], do_ref[:, lanes], TN, False) for h, lanes in enumerate(heads)]
              for side, (_, _, do_ref, _, _) in enumerate(sides)]
        dk = [[_dot(ds[side][h], q_ref[:, lanes], TN, False) for h, lanes in enumerate(heads)]
              for side, (_, q_ref, _, _, _) in enumerate(sides)]
        for h, lanes in enumerate(heads):
            dk_ref[:, lanes] = (dk[0][h] + dk[1][h]) * scale
            dv_ref[:, lanes] = dv[0][h] + dv[1][h]

    def kv(piece):
        return pl.BlockSpec((Q_BLOCK, AT_WIDTH), lambda r, n: (n, 3 * r + piece))

    q_same = pl.BlockSpec((Q_BLOCK, AT_WIDTH), lambda r, n: (n, 3 * r))
    q_next = pl.BlockSpec((Q_BLOCK, AT_WIDTH), lambda r, n: (jnp.minimum(n + 1, nb - 1), 3 * r))
    same = pl.BlockSpec((Q_BLOCK, AT_WIDTH), lambda r, n: (n, r))
    nxt = pl.BlockSpec((Q_BLOCK, AT_WIDTH), lambda r, n: (jnp.minimum(n + 1, nb - 1), r))
    return pl.pallas_call(
        body, name=f"attn_backward_kv_d{dil}", grid=(dil, nb),
        in_specs=[kv(1), kv(2), q_same, q_next, same, nxt, same, nxt, same, nxt],
        out_specs=[same, same],
        out_shape=[jax.ShapeDtypeStruct((length, dil * AT_WIDTH), f32)] * 2,
        compiler_params=_params(("parallel", "parallel")),
    )(pat_view, pat_view, pat_view, pat_view, d_out, d_out, delta, delta, lse, lse)


def _to_view(a, dil):
    return a if dil == 1 else a.reshape(a.shape[0] // dil, dil * a.shape[1])


def _from_view(a, dil):
    return a if dil == 1 else a.reshape(a.shape[0] * dil, a.shape[1] // dil)


def _gated_norms(o_dn, z_dn, o_at, z_at, dn_w, at_w):
    ms_dn = _sel_right(o_dn * o_dn, _group_ones(DN_WIDTH, DN_DIM)) * (1.0 / DN_DIM)
    a = o_dn * lax.rsqrt(ms_dn + EPS) * dn_w * _silu(z_dn)
    ms_at = _sel_right(o_at * o_at, _group_ones(AT_WIDTH, AT_DIM)) * (1.0 / AT_DIM)
    b = o_at * lax.rsqrt(ms_at + EPS) * at_w * _silu(z_at)
    return a, b


def _residual_loss(x, mix, gate, fin_w, target):
    x2 = x + gate * mix
    y = x2 * lax.rsqrt(jnp.mean(x2 * x2, axis=-1, keepdims=True) + EPS) * fin_w
    err = y - target
    per_token = jnp.sum(err * err, axis=1, keepdims=True) * (1.0 / D_MODEL)
    return 0.5 * jnp.sum(per_token, axis=0, keepdims=True)


TAIL_ROWS = 8


def _tail(x, target, o_dn, pdn, outs, lses, z_at, gate, w_out, dn_w_row, at_w_row, fin_w):
    seq = x.shape[0]
    tm = 256

    def body(x_ref, t_ref, odn_ref, zdn_ref, o1_ref, o4_ref, o16_ref, l1_ref, l4_ref, l16_ref, zat_ref, gate_ref,
             wt_ref, wb_ref, dnw_ref, atw_ref, fw_ref,
             dx2_ref, dodn_ref, dzdn_ref, doat_ref, delta_ref, lse_ref, dzat_ref, gwt_ref, gwb_ref, rows_ref):
        @pl.when(pl.program_id(0) == 0)
        def _():
            gwt_ref[...] = jnp.zeros_like(gwt_ref)
            gwb_ref[...] = jnp.zeros_like(gwb_ref)
            rows_ref[...] = jnp.zeros_like(rows_ref)

        l1, l4, l16 = l1_ref[...], l4_ref[...], l16_ref[...]
        top = jnp.maximum(jnp.maximum(l1, l4), l16)
        e1, e4, e16 = jnp.exp(l1 - top), jnp.exp(l4 - top), jnp.exp(l16 - top)
        den = e1 + e4 + e16
        o_at = (e1 * o1_ref[...] + e4 * o4_ref[...] + e16 * o16_ref[...]) / den
        lse_ref[...] = top + jnp.log(den)

        (a, b), norms_vjp = jax.vjp(_gated_norms, odn_ref[...], zdn_ref[...], o_at, zat_ref[...], dnw_ref[...], atw_ref[...])
        a, b = a.astype(bf16), b.astype(bf16)
        mix = jnp.dot(a, wt_ref[...], preferred_element_type=f32) + jnp.dot(b, wb_ref[...], preferred_element_type=f32)
        loss, loss_vjp = jax.vjp(_residual_loss, x_ref[...], mix, gate_ref[...], fw_ref[...], t_ref[...])
        dx2, dmix, dgate, dfw, _ = loss_vjp(jnp.ones((1, 1), f32))
        dmix = dmix.astype(bf16)
        dwt = lax.dot_general(a, dmix, (TN, ((), ())), preferred_element_type=f32)
        dwb = lax.dot_general(b, dmix, (TN, ((), ())), preferred_element_type=f32)
        da = lax.dot_general(dmix, wt_ref[...], (NT, ((), ())), preferred_element_type=f32)
        db = lax.dot_general(dmix, wb_ref[...], (NT, ((), ())), preferred_element_type=f32)
        dodn, dzdn, doat, dzat, ddnw, datw = norms_vjp((da, db))
        dx2_ref[...] = dx2
        dodn_ref[...] = dodn
        dzdn_ref[...] = dzdn
        doat_ref[...] = doat
        dzat_ref[...] = dzat
        delta_ref[...] = _sel_right(doat * o_at, _group_ones(AT_WIDTH, AT_DIM))
        gwt_ref[...] += dwt
        gwb_ref[...] += dwb
        rows_ref[0:1, :] += jnp.broadcast_to(loss, (1, D_MODEL))
        rows_ref[1:2, :] += dgate
        rows_ref[2:3, :] += dfw
        rows_ref[3:4, 0:DN_WIDTH] += ddnw
        rows_ref[4:5, 0:AT_WIDTH] += datw

    tile = lambda w: pl.BlockSpec((tm, w), lambda i: (i, 0))
    row = lambda w: pl.BlockSpec((1, w), lambda i: (0, 0))
    half_w = pl.BlockSpec((DN_WIDTH, D_MODEL), lambda i: (0, 0))
    sds = lambda w: jax.ShapeDtypeStruct((seq, w), f32)
    return pl.pallas_call(
        body, name="tail", grid=(seq // tm,),
        in_specs=[tile(D_MODEL), tile(D_MODEL), tile(DN_WIDTH),
                  pl.BlockSpec((tm, DN_WIDTH), lambda i: (i, QKV_W // DN_WIDTH)),
                  tile(AT_WIDTH), tile(AT_WIDTH), tile(AT_WIDTH), tile(AT_WIDTH), tile(AT_WIDTH), tile(AT_WIDTH),
                  tile(AT_WIDTH), row(D_MODEL), half_w, pl.BlockSpec((AT_WIDTH, D_MODEL), lambda i: (1, 0)),
                  row(DN_WIDTH), row(AT_WIDTH), row(D_MODEL)],
        out_specs=[tile(D_MODEL), tile(DN_WIDTH), tile(DN_WIDTH), tile(AT_WIDTH), tile(AT_WIDTH), tile(AT_WIDTH),
                   tile(AT_WIDTH), half_w, half_w, pl.BlockSpec((TAIL_ROWS, D_MODEL), lambda i: (0, 0))],
        out_shape=[sds(D_MODEL), sds(DN_WIDTH), sds(DN_WIDTH), sds(AT_WIDTH), sds(AT_WIDTH), sds(AT_WIDTH),
                   sds(AT_WIDTH), jax.ShapeDtypeStruct((DN_WIDTH, D_MODEL), f32),
                   jax.ShapeDtypeStruct((AT_WIDTH, D_MODEL), f32), jax.ShapeDtypeStruct((TAIL_ROWS, D_MODEL), f32)],
        compiler_params=_params(("arbitrary",)),
    )(x, target, o_dn, pdn, outs[0], outs[1], outs[2], lses[0], lses[1], lses[2], z_at, gate, w_out, w_out,
      dn_w_row, at_w_row, fin_w)


PRE_ROWS = 8


def _pre_backward(x, dx2, pos_col, invf_row, sign_row, norm_w, scale, shift, w_cat, conv_w8, pdn, d_conv, d_zdn, d_ba,
                  d_q, d_k, d_v, d_zat):
    seq = x.shape[0]
    tm = 256
    last = seq // tm - 1

    def body(x_ref, dx2_ref, pos_ref, invf_ref, sign_ref, nw_ref, sc_ref, sh_ref, w_ref, cw_ref,
             pre_ref, prehalo_ref, dc_ref, dchalo_ref, dz_ref, dba_ref,
             dq1_ref, dq4_ref, dq16_ref, dk1_ref, dk4_ref, dk16_ref, dv1_ref, dv4_ref, dv16_ref, dzat_ref,
             gx_ref, dproj_ref, rows_ref, crow_ref, ext_ref):
        i = pl.program_id(0)

        @pl.when(i == 0)
        def _():
            rows_ref[...] = jnp.zeros_like(rows_ref)
            crow_ref[...] = jnp.zeros_like(crow_ref)

        dc = dc_ref[...]
        ext_ref[0:tm, :] = dc
        ext_ref[tm:tm + 8, :] = jnp.where(i < last, dchalo_ref[...], 0.0)
        w8 = cw_ref[...]
        d_pre = ext_ref[pl.ds(3, tm), :] * w8[0:1, :]
        for j in range(1, CONV_K):
            d_pre = d_pre + ext_ref[pl.ds(3 - j, tm), :] * w8[j:j + 1, :]
        ext_ref[0:8, :] = jnp.where(i > 0, prehalo_ref[...], 0.0)
        ext_ref[8:8 + tm, :] = pre_ref[...]
        for j in range(CONV_K):
            crow_ref[j:j + 1, :] += jnp.sum(dc * ext_ref[pl.ds(5 + j, tm), :], axis=0, keepdims=True)

        cos_t, sin_t = _rope_tables(pos_ref[...], invf_ref[...], sign_ref[...])
        dq = dq1_ref[...] + dq4_ref[...] + dq16_ref[...]
        dk = dk1_ref[...] + dk4_ref[...] + dk16_ref[...]
        dq = dq * cos_t + _rope_partner(dq * sin_t)
        dk = dk * cos_t + _rope_partner(dk * sin_t)
        dv = dv1_ref[...] + dv4_ref[...] + dv16_ref[...]
        d_proj = jnp.concatenate([d_pre, dz_ref[...], dba_ref[...], dq, dk, dv, dzat_ref[...]], axis=1).astype(bf16)
        dproj_ref[...] = d_proj
        dh = lax.dot_general(d_proj, w_ref[...], (NT, ((), ())), preferred_element_type=f32)
        _, vjp = jax.vjp(_hnorm, x_ref[...], nw_ref[...], sc_ref[...], sh_ref[...])
        dx, dnw, dsc, dsh = vjp(dh)
        gx_ref[...] = dx + dx2_ref[...]
        rows_ref[0:1, :] += dnw
        rows_ref[1:2, :] += dsc
        rows_ref[2:3, :] += dsh

    tile = lambda w: pl.BlockSpec((tm, w), lambda i: (i, 0))
    row = lambda w: pl.BlockSpec((1, w), lambda i: (0, 0))
    step8 = tm // 8
    return pl.pallas_call(
        body, name="pre_backward", grid=(seq // tm,),
        in_specs=[tile(D_MODEL), tile(D_MODEL), tile(1), row(128), row(128), row(D_MODEL), row(D_MODEL), row(D_MODEL),
                  pl.BlockSpec((D_MODEL, CAT_W), lambda i: (0, 0)), pl.BlockSpec((8, QKV_W), lambda i: (0, 0)),
                  tile(QKV_W), pl.BlockSpec((8, QKV_W), lambda i: (jnp.maximum(i * step8 - 1, 0), 0)),
                  tile(QKV_W), pl.BlockSpec((8, QKV_W), lambda i: (jnp.minimum((i + 1) * step8, seq // 8 - 1), 0)),
                  tile(DN_WIDTH), tile(BA_W)] + [tile(AT_WIDTH)] * 10,
        out_specs=[tile(D_MODEL), tile(CAT_W), pl.BlockSpec((PRE_ROWS, D_MODEL), lambda i: (0, 0)),
                   pl.BlockSpec((8, QKV_W), lambda i: (0, 0))],
        out_shape=[jax.ShapeDtypeStruct((seq, D_MODEL), f32), jax.ShapeDtypeStruct((seq, CAT_W), bf16),
                   jax.ShapeDtypeStruct((PRE_ROWS, D_MODEL), f32), jax.ShapeDtypeStruct((8, QKV_W), f32)],
        scratch_shapes=[pltpu.VMEM((tm + 8, QKV_W), f32)],
        compiler_params=_params(("arbitrary",)),
    )(x, dx2, pos_col, invf_row, sign_row, norm_w, scale, shift, w_cat, conv_w8, pdn, pdn, d_conv, d_conv, d_zdn, d_ba,
      d_q[0], d_q[1], d_q[2], d_k[0], d_k[1], d_k[2], d_v[0], d_v[1], d_v[2], d_zat)


def _weight_grad(h, d_proj):
    seq = h.shape[0]
    tk, tn = 1024, 384
    n_k = seq // tk

    def body(h_ref, d_ref, o_ref):
        @pl.when(pl.program_id(1) == 0)
        def _():
            o_ref[...] = jnp.zeros_like(o_ref)

        o_ref[...] += lax.dot_general(h_ref[...], d_ref[...], (TN, ((), ())), preferred_element_type=f32)

    return pl.pallas_call(
        body, name="weight_grad", grid=(CAT_W // tn, n_k),
        in_specs=[pl.BlockSpec((tk, D_MODEL), lambda n, k: (k, 0)), pl.BlockSpec((tk, tn), lambda n, k: (k, n))],
        out_specs=pl.BlockSpec((D_MODEL, tn), lambda n, k: (0, n)),
        out_shape=jax.ShapeDtypeStruct((D_MODEL, CAT_W), f32),
        compiler_params=_params(("parallel", "arbitrary")),
    )(h, d_proj)


def _adamw(w, g, m, v):
    m = ADAM_B1 * m + (1.0 - ADAM_B1) * g
    v = ADAM_B2 * v + (1.0 - ADAM_B2) * (g * g)
    m_hat = m / (1.0 - ADAM_B1 ** ADAM_STEP)
    v_hat = v / (1.0 - ADAM_B2 ** ADAM_STEP)
    delta = -ADAM_LR * (m_hat / (jnp.sqrt(v_hat) + ADAM_EPS) + ADAM_WD * w)
    return delta, m, v


def _mod_shard(c_all, w_mod_s, b_mod_s):
    def body(c_ref, w_ref, b_ref, o_ref):
        o_ref[...] = _dot(_silu(c_ref[...]), w_ref[...], NN, True) + b_ref[...]

    return pl.pallas_call(body, name="mod_shard", out_shape=jax.ShapeDtypeStruct((N_DEV, w_mod_s.shape[1]), f32),
                          compiler_params=_params())(c_all, w_mod_s, b_mod_s)


def _mod_update(c_all, d_mod_s, w, m, v):
    def body(c_ref, d_ref, w_ref, m_ref, v_ref, g_ref, dw_ref, nm_ref, nv_ref):
        g = _dot(_silu(c_ref[...]), d_ref[...], TN, True)
        g_ref[...] = g
        dw_ref[...], nm_ref[...], nv_ref[...] = _adamw(w_ref[...], g, m_ref[...], v_ref[...])

    return pl.pallas_call(body, name="mod_update", out_shape=[jax.ShapeDtypeStruct(w.shape, f32)] * 4,
                          compiler_params=_params())(c_all, d_mod_s, w, m, v)


def _adamw_rows(w, g, m, v, name):
    rows, width = w.shape
    tr = PACK_CHUNK if rows % PACK_CHUNK == 0 else rows

    def body(w_ref, g_ref, m_ref, v_ref, dw_ref, nm_ref, nv_ref):
        dw_ref[...], nm_ref[...], nv_ref[...] = _adamw(w_ref[...], g_ref[...], m_ref[...], v_ref[...])

    spec = pl.BlockSpec((tr, width), lambda i: (i, 0))
    return pl.pallas_call(body, name=name, grid=(rows // tr,), in_specs=[spec] * 4, out_specs=[spec] * 3,
                          out_shape=[jax.ShapeDtypeStruct(w.shape, f32)] * 3,
                          compiler_params=_params(("parallel",)))(w, g, m, v)


SM_NORM, SM_FIN, SM_DN, SM_AT, SM_A, SM_DT, SM_LOSS, SM_MOD = 0, 1024, 2048, 2560, 3072, 3200, 3328, 3456
SM_W = SM_MOD + 3 * D_MODEL
RS_NORM, RS_FIN, RS_DN, RS_AT, RS_A, RS_DT, RS_BMOD = 0, 1024, 2048, 2176, 2304, 2432, 2560
RS_W = RS_BMOD + 3 * D_MODEL


def _small_update(gathered, w, m, v):
    def body(g_ref, w_ref, m_ref, v_ref, grad_ref, dw_ref, nm_ref, nv_ref, loss_ref, dmod_ref):
        total = g_ref[0:1, :]
        for dev in range(1, N_DEV):
            total = total + g_ref[8 * dev:8 * dev + 1, :]
        for dev in range(N_DEV):
            dmod_ref[dev:dev + 1, :] = g_ref[8 * dev:8 * dev + 1, SM_MOD:SM_W]
        dn = total[:, SM_DN:SM_DN + DN_DIM]
        for h in range(1, DN_HEADS):
            dn = dn + total[:, SM_DN + h * DN_DIM:SM_DN + (h + 1) * DN_DIM]
        at = total[:, SM_AT:SM_AT + AT_DIM]
        for h in range(1, AT_HEADS):
            at = at + total[:, SM_AT + h * AT_DIM:SM_AT + (h + 1) * AT_DIM]
        grad_ref[:, RS_NORM:RS_FIN] = total[:, SM_NORM:SM_FIN]
        grad_ref[:, RS_FIN:RS_DN] = total[:, SM_FIN:SM_DN]
        grad_ref[:, RS_DN:RS_AT] = dn
        grad_ref[:, RS_AT:RS_A] = jnp.zeros((1, 128), f32)
        grad_ref[:, RS_AT:RS_AT + AT_DIM] = at
        grad_ref[:, RS_A:RS_DT] = total[:, SM_A:SM_DT]
        grad_ref[:, RS_DT:RS_BMOD] = total[:, SM_DT:SM_LOSS]
        grad_ref[:, RS_BMOD:RS_W] = total[:, SM_MOD:SM_W]
        loss_ref[...] = total[:, SM_LOSS:SM_MOD]
        dw_ref[...], nm_ref[...], nv_ref[...] = _adamw(w_ref[...], grad_ref[...], m_ref[...], v_ref[...])

    row = jax.ShapeDtypeStruct((1, RS_W), f32)
    return pl.pallas_call(
        body, name="small_update",
        out_shape=[row, row, row, row, jax.ShapeDtypeStruct((1, 128), f32), jax.ShapeDtypeStruct((N_DEV, 3 * D_MODEL), f32)],
        compiler_params=_params())(gathered, w, m, v)


def _all_gather_rows(block, name):
    m_per, n = block.shape

    def body(x_ref, out_ref, send_sems, recv_sems, local_sem):
        x, y, c = lax.axis_index("x"), lax.axis_index("y"), lax.axis_index("c")
        me, sibling = (x, y, c), (x, y, 1 - c)
        chips = [(1 - x, y), (x, 1 - y), (1 - x, 1 - y)]

        def rows(px, py, pc):
            return out_ref.at[pl.ds((4 * px + 2 * py + pc) * m_per, m_per), :]

        def copy(k, blk, to, src=None):
            return pltpu.make_async_remote_copy(
                src_ref=rows(*blk) if src is None else src, dst_ref=rows(*blk),
                send_sem=send_sems.at[k], recv_sem=recv_sems.at[k], device_id=to, device_id_type=MESH)

        mine = pltpu.make_async_copy(x_ref, rows(*me), local_sem)
        mine.start()
        first = [copy(0, me, sibling, src=x_ref)]
        first += [copy(1 + j, me, (*chip, c), src=x_ref) for j, chip in enumerate(chips)]
        for cp in first:
            cp.start()
        passed = [copy(4 + j, (*chip, c), sibling) for j, chip in enumerate(chips)]
        for j, chip in enumerate(chips):
            copy(1 + j, (*chip, c), me).wait_recv()
            passed[j].start()
        copy(0, sibling, me).wait_recv()
        for j, chip in enumerate(chips):
            copy(4 + j, (*chip, 1 - c), me).wait_recv()
        for cp in first + passed:
            cp.wait_send()
        mine.wait()

    return pl.pallas_call(
        body, name=name,
        out_shape=jax.ShapeDtypeStruct((N_DEV * m_per, n), block.dtype),
        in_specs=[pl.BlockSpec(memory_space=pltpu.VMEM)],
        out_specs=pl.BlockSpec(memory_space=pltpu.VMEM),
        scratch_shapes=[pltpu.SemaphoreType.DMA((7,)), pltpu.SemaphoreType.DMA((7,)), pltpu.SemaphoreType.DMA],
        compiler_params=pltpu.CompilerParams(vmem_limit_bytes=VMEM_LIMIT),
    )(block)


def _gather_weight_shards(packed):
    rows, width = packed.shape

    def body(src_ref, out_ref, send_sems, recv_sems, local_sem):
        x, y, c = lax.axis_index("x"), lax.axis_index("y"), lax.axis_index("c")
        chips = [(1 - x, y), (x, 1 - y), (1 - x, 1 - y)]

        def copy(k, owner, to):
            slot = out_ref.at[2 * owner[0] + owner[1]]
            return pltpu.make_async_remote_copy(
                src_ref=src_ref, dst_ref=slot, send_sem=send_sems.at[k], recv_sem=recv_sems.at[k],
                device_id=(*to, c), device_id_type=MESH)

        mine = pltpu.make_async_copy(src_ref, out_ref.at[2 * x + y], local_sem)
        mine.start()
        sends = [copy(k, (x, y), chip) for k, chip in enumerate(chips)]
        for cp in sends:
            cp.start()
        for k, chip in enumerate(chips):
            copy(k, chip, (x, y)).wait_recv()
        for cp in sends:
            cp.wait_send()
        mine.wait()

    return pl.pallas_call(
        body, name="gather_weight_shards",
        out_shape=jax.ShapeDtypeStruct((N_CHIPS, rows, width), packed.dtype),
        in_specs=[pl.BlockSpec(memory_space=pl.ANY)],
        out_specs=pl.BlockSpec(memory_space=pl.ANY),
        scratch_shapes=[pltpu.SemaphoreType.DMA((3,)), pltpu.SemaphoreType.DMA((3,)), pltpu.SemaphoreType.DMA],
    )(packed)


def _reduce_weight_grads(grads):
    _, rows, width = grads.shape
    n_chunks = rows // PACK_CHUNK

    def body(g_ref, out_ref, own_ref, land_ref, part_ref, sib_ref, send_sems, recv_sems, local_sem):
        x, y, c = lax.axis_index("x"), lax.axis_index("y"), lax.axis_index("c")
        chips = [(1 - x, y), (x, 1 - y), (1 - x, 1 - y)]

        def copy(k, chip):
            return pltpu.make_async_remote_copy(
                src_ref=g_ref.at[2 * chip[0] + chip[1]], dst_ref=land_ref.at[k],
                send_sem=send_sems.at[k], recv_sem=recv_sems.at[k], device_id=(*chip, c), device_id_type=MESH)

        mine = pltpu.make_async_copy(g_ref.at[2 * x + y], own_ref, local_sem)
        mine.start()
        sends = [copy(k, chip) for k, chip in enumerate(chips)]
        for cp in sends:
            cp.start()
        mine.wait()
        for cp in sends:
            cp.wait_recv()

        def add_landed(i, carry):
            r = pl.ds(pl.multiple_of(i * PACK_CHUNK, 8), PACK_CHUNK)
            part_ref[r, :] = ((own_ref[r, :] + land_ref[0, r, :]) + land_ref[1, r, :]) + land_ref[2, r, :]
            return carry

        lax.fori_loop(0, n_chunks, add_landed, 0)
        swap = pltpu.make_async_remote_copy(
            src_ref=part_ref, dst_ref=sib_ref, send_sem=send_sems.at[3], recv_sem=recv_sems.at[3],
            device_id=(x, y, 1 - c), device_id_type=MESH)
        swap.start()
        swap.wait_recv()

        def add_sibling(i, carry):
            r = pl.ds(pl.multiple_of(i * PACK_CHUNK, 8), PACK_CHUNK)
            out_ref[r, :] = part_ref[r, :] + sib_ref[r, :]
            return carry

        lax.fori_loop(0, n_chunks, add_sibling, 0)
        swap.wait_send()
        for cp in sends:
            cp.wait_send()

    buf = pltpu.VMEM((rows, width), f32)
    return pl.pallas_call(
        body, name="reduce_weight_grads",
        out_shape=jax.ShapeDtypeStruct((rows, width), f32),
        in_specs=[pl.BlockSpec(memory_space=pl.ANY)],
        out_specs=pl.BlockSpec(memory_space=pltpu.VMEM),
        scratch_shapes=[buf, pltpu.VMEM((3, rows, width), f32), buf, buf,
                        pltpu.SemaphoreType.DMA((4,)), pltpu.SemaphoreType.DMA((4,)), pltpu.SemaphoreType.DMA],
        compiler_params=pltpu.CompilerParams(vmem_limit_bytes=VMEM_LIMIT),
    )(grads)


def _pack_shards(w_in_s, w_out_s, conv_s):
    lead = w_in_s.shape[:-2]
    conv_rows = jnp.pad(conv_s.reshape(*lead, CONV_K * 384), [(0, 0)] * len(lead) + [(0, 8 * D_MODEL - CONV_K * 384)])
    return jnp.concatenate([
        w_in_s.reshape(*lead, SHARD_IN, D_MODEL), w_out_s, conv_rows.reshape(*lead, 8, D_MODEL),
        jnp.zeros((*lead, PACK_ROWS - SHARD_IN - 256 - 8, D_MODEL), f32)], axis=-2)


def _unpack_shards(packed):
    lead = packed.shape[:-2]
    w_in_s = packed[..., :SHARD_IN, :].reshape(*lead, D_MODEL, SHARD_IN)
    w_out_s = packed[..., SHARD_IN:SHARD_IN + 256, :]
    conv_s = packed[..., SHARD_IN + 256:SHARD_IN + 264, :].reshape(*lead, 8 * D_MODEL)[..., :CONV_K * 384]
    return w_in_s, w_out_s, conv_s.reshape(*lead, CONV_K, 384)


def _local_step(x, target, pos_col, mod_row, norm_w, w_in, conv_w, a_log, dt_bias, dn_norm_w, at_norm_w, w_out, fin_w):
    shift, scale, gate = mod_row[:, :D_MODEL], mod_row[:, D_MODEL:2 * D_MODEL], mod_row[:, 2 * D_MODEL:]
    half = AT_DIM // 2
    lane = jnp.arange(128)
    inv_freq = ROPE_THETA ** (-jnp.arange(half, dtype=f32) / half)
    invf_row = inv_freq[lane % half].reshape(1, 128)
    sign_row = jnp.where((lane % AT_DIM) < half, -1.0, 1.0).astype(f32).reshape(1, 128)
    ba_w = jnp.pad(w_in[:, 2 * D_MODEL:2 * D_MODEL + 2 * DN_HEADS], ((0, 0), (0, BA_W - 2 * DN_HEADS)))
    w_cat = jnp.concatenate([w_in[:, :2 * D_MODEL], ba_w, w_in[:, 2 * D_MODEL + 2 * DN_HEADS:]], axis=1).astype(bf16)
    conv_w8 = jnp.pad(conv_w, ((0, 8 - CONV_K), (0, 0)))
    a_row = jnp.pad(a_log.reshape(1, DN_HEADS), ((0, 0), (DN_HEADS, BA_W - 2 * DN_HEADS)))
    dt_row = jnp.pad(dt_bias.reshape(1, DN_HEADS), ((0, 0), (DN_HEADS, BA_W - 2 * DN_HEADS)))
    dn_w_row = jnp.tile(dn_norm_w.reshape(1, DN_DIM), (1, DN_HEADS))
    at_w_row = jnp.tile(at_norm_w.reshape(1, AT_DIM), (1, AT_HEADS))
    norm_row = norm_w.reshape(1, D_MODEL)
    fin_row = fin_w.reshape(1, D_MODEL)
    w_out_bf = w_out.astype(bf16)

    pdn, pat, z_at, h = _pre_proj(x, pos_col, invf_row, sign_row, norm_row, scale, shift, w_cat)
    dn_parts = _dn_intra_forward(pdn, conv_w8, a_row, dt_row)
    o_dn, states = _dn_scan_forward(*dn_parts)
    views = [_to_view(pat, d) for d in PATTERN_DILATIONS]
    outs, lses = [], []
    for d, view in zip(PATTERN_DILATIONS, views):
        o, lse = _attn_forward(view, d)
        outs.append(_from_view(o, d))
        lses.append(_from_view(lse, d))
    (dx2, d_odn, d_zdn, d_oat, delta, lse_all, d_zat, g_wtop, g_wbot, tail_rows) = _tail(
        x, target, o_dn, pdn, outs, lses, z_at, gate, w_out_bf, dn_w_row, at_w_row, fin_row)
    d_q, d_k, d_v = [], [], []
    for d, view in zip(PATTERN_DILATIONS, views):
        args = (_to_view(d_oat, d), _to_view(delta, d), _to_view(lse_all, d))
        d_q.append(_from_view(_attn_backward_q(view, *args, d), d))
        dk, dv = _attn_backward_kv(view, *args, d)
        d_k.append(_from_view(dk, d))
        d_v.append(_from_view(dv, d))
    d_parts = _dn_scan_backward(*dn_parts, states, d_odn)
    d_conv, d_ba, dn_rows = _dn_intra_backward(pdn, conv_w8, a_row, dt_row, *d_parts)
    grad_x, d_proj, pre_rows, conv_rows = _pre_backward(
        x, dx2, pos_col, invf_row, sign_row, norm_row, scale, shift, w_cat, conv_w8, pdn, d_conv, d_zdn, d_ba,
        d_q, d_k, d_v, d_zat)
    g_cat = _weight_grad(h, d_proj)
    g_w_in = jnp.concatenate([g_cat[:, :2 * D_MODEL], g_cat[:, 2 * D_MODEL:2 * D_MODEL + 2 * DN_HEADS],
                              g_cat[:, 2 * D_MODEL + BA_W:]], axis=1)
    g_w_out = jnp.concatenate([g_wtop, g_wbot], axis=0)
    g_conv = conv_rows[:CONV_K]
    small = jnp.concatenate([
        pre_rows[0:1], tail_rows[2:3], tail_rows[3:4, :DN_WIDTH], tail_rows[4:5, :AT_WIDTH],
        dn_rows[0:1], dn_rows[1:2], tail_rows[0:1, :128],
        pre_rows[2:3], pre_rows[1:2], tail_rows[1:2]], axis=1)
    return grad_x, g_w_in, g_w_out, g_conv, small


def kernel(x, c, positions, w_mod, b_mod, norm_w, w_in, conv_w, a_log, dt_bias, dn_norm_w, at_norm_w, w_out, final_norm_w, loss_target, m_w_mod, m_b_mod, m_norm_w, m_w_in, m_conv_w, m_a_log, m_dt_bias, m_dn_norm_w, m_at_norm_w, m_w_out, m_final_norm_w, v_w_mod, v_b_mod, v_norm_w, v_w_in, v_conv_w, v_a_log, v_dt_bias, v_dn_norm_w, v_at_norm_w, v_w_out, v_final_norm_w):
    seq = x.shape[1]
    ax, ay, ac = lax.axis_index("x"), lax.axis_index("y"), lax.axis_index("c")
    me = 4 * ax + 2 * ay + ac
    chip = 2 * ax + ay

    c_all = _all_gather_rows(jnp.pad(c, ((0, 7), (0, 0))), "gather_c").reshape(N_DEV, 8, D_MODEL)[:, 0]
    b_mod_s = lax.dynamic_slice_in_dim(b_mod, chip * 768, 768, axis=1)
    mod_part = _mod_shard(c_all, w_mod[0], b_mod_s)
    mod_all = _all_gather_rows(mod_part, "gather_mod").reshape(N_CHIPS, 2, N_DEV, 768)[:, 0]
    mod_row = lax.dynamic_index_in_dim(mod_all, me, axis=1, keepdims=False).reshape(1, 3 * D_MODEL)

    shards = _gather_weight_shards(_pack_shards(w_in[0], w_out[0], conv_w[0]))
    w_in_s, w_out_s, conv_s = _unpack_shards(shards)
    w_in_full = jnp.transpose(w_in_s, (1, 0, 2)).reshape(D_MODEL, IN_COLS)
    w_out_full = w_out_s.reshape(D_MODEL, D_MODEL)
    conv_full = jnp.transpose(conv_s, (1, 0, 2)).reshape(CONV_K, QKV_W)

    grad_x, g_w_in, g_w_out, g_conv, small = _local_step(
        x[0], loss_target[0], positions.reshape(seq, 1), mod_row, norm_w, w_in_full, conv_full, a_log, dt_bias,
        dn_norm_w, at_norm_w, w_out_full, final_norm_w)

    g_pack = _pack_shards(jnp.transpose(g_w_in.reshape(D_MODEL, N_CHIPS, SHARD_IN), (1, 0, 2)),
                          g_w_out.reshape(N_CHIPS, 256, D_MODEL),
                          jnp.transpose(g_conv.reshape(CONV_K, N_CHIPS, 384), (1, 0, 2)))
    g_mine = _reduce_weight_grads(g_pack)
    d_pack, m_pack, v_pack = _adamw_rows(_pack_shards(w_in[0], w_out[0], conv_w[0]), g_mine,
                                         _pack_shards(m_w_in[0], m_w_out[0], m_conv_w[0]),
                                         _pack_shards(v_w_in[0], v_w_out[0], v_conv_w[0]), "adamw_packed")
    grad_w_in, grad_w_out, grad_conv_w = _unpack_shards(g_mine)
    delta_w_in, delta_w_out, delta_conv_w = _unpack_shards(d_pack)
    new_m_w_in, new_m_w_out, new_m_conv_w = _unpack_shards(m_pack)
    new_v_w_in, new_v_w_out, new_v_conv_w = _unpack_shards(v_pack)

    gathered = _all_gather_rows(jnp.pad(small, ((0, 7), (0, 0))), "gather_small")

    def small_row(norm, fin, dn, at, a, dt, bmod):
        z = lambda n: jnp.zeros((1, n), f32)
        return jnp.concatenate([norm.reshape(1, -1), fin.reshape(1, -1), dn.reshape(1, -1), at.reshape(1, -1), z(64),
                                z(4), a.reshape(1, -1), z(120), z(4), dt.reshape(1, -1), z(120), bmod.reshape(1, -1)], axis=1)

    g_small, d_small, m_small, v_small, loss_row, d_mod_all = _small_update(
        gathered,
        small_row(norm_w, final_norm_w, dn_norm_w, at_norm_w, a_log, dt_bias, b_mod),
        small_row(m_norm_w, m_final_norm_w, m_dn_norm_w, m_at_norm_w, m_a_log, m_dt_bias, m_b_mod),
        small_row(v_norm_w, v_final_norm_w, v_dn_norm_w, v_at_norm_w, v_a_log, v_dt_bias, v_b_mod))

    def split_small(r):
        return (r[:, RS_BMOD:RS_W], r[:, RS_NORM:RS_FIN], r[:, RS_A + DN_HEADS:RS_A + 2 * DN_HEADS],
                r[:, RS_DT + DN_HEADS:RS_DT + 2 * DN_HEADS], r[:, RS_DN:RS_AT], r[:, RS_AT:RS_AT + AT_DIM],
                r[0, RS_FIN:RS_DN])

    d_mod_s = lax.dynamic_slice_in_dim(d_mod_all, chip * 768, 768, axis=1)
    pad_rows = lambda a: jnp.pad(a, ((0, 128 - N_DEV), (0, 0)))
    grad_w_mod, delta_w_mod, new_m_w_mod, new_v_w_mod = _mod_update(
        pad_rows(c_all), pad_rows(d_mod_s), w_mod[0], m_w_mod[0], v_w_mod[0])

    def ordered(w_mod_leaf, small_row_leaf, w_in_leaf, conv_leaf, w_out_leaf):
        b, n, a, dt, dn, at, fin = split_small(small_row_leaf)
        return [w_mod_leaf[None], b, n, w_in_leaf[None], conv_leaf[None], a, dt, dn, at, w_out_leaf[None], fin]

    loss = loss_row[0, 0]
    return (loss, grad_x[None],
            *ordered(grad_w_mod, g_small, grad_w_in, grad_conv_w, grad_w_out),
            *ordered(delta_w_mod, d_small, delta_w_in, delta_conv_w, delta_w_out),
            *ordered(new_m_w_mod, m_small, new_m_w_in, new_m_conv_w, new_m_w_out),
            *ordered(new_v_w_mod, v_small, new_v_w_in, new_v_conv_w, new_v_w_out))
```

```python
import functools

import jax
import jax.numpy as jnp
from jax import lax
from jax.experimental import pallas as pl
from jax.experimental.pallas import tpu as pltpu

f32 = jnp.float32
bf16 = jnp.bfloat16
HIGHEST = lax.Precision.HIGHEST
MESH = pl.DeviceIdType.MESH

D_MODEL = 1024
DN_HEADS = 4
DN_DIM = 128
DN_WIDTH = 512
AT_HEADS = 8
AT_DIM = 64
AT_WIDTH = 512
CHUNK = 64
Q_BLOCK = 128
CONV_K = 4
EPS = 1e-6
ROPE_THETA = 10000.0
PATTERN_DILATIONS = (1, 4, 16)
NEG = -1e30

QKV_W = 3 * DN_WIDTH
BA_W = 128
PDN_W = QKV_W + DN_WIDTH + BA_W
PAT_W = 3 * AT_WIDTH
CAT_W = PDN_W + PAT_W + AT_WIDTH
IN_COLS = 4104
N_CHIPS = 4
N_DEV = 8
SHARD_IN = IN_COLS // N_CHIPS
PACK_ROWS = 1296
PACK_CHUNK = 48

ADAM_LR = 0.001
ADAM_B1 = 0.9
ADAM_B2 = 0.999
ADAM_EPS = 1e-08
ADAM_WD = 0.01
ADAM_STEP = 10

VMEM_LIMIT = 56 * 1024 * 1024

NN = ((1,), (0,))
NT = ((1,), (1,))
TN = ((0,), (0,))


def _pieces(a, n):
    out = []
    for _ in range(n - 1):
        p = a.astype(bf16)
        out.append(p)
        a = a - p.astype(f32)
    out.append(a.astype(bf16))
    return out


def _dot(a, b, dims, exact):
    raw = lambda p, q: lax.dot_general(p, q, (dims, ((), ())), preferred_element_type=f32)
    if exact == "split":
        (ah, al), (bh, bl) = _pieces(a, 2), _pieces(b, 2)
        return raw(ah, bh) + (raw(ah, bl) + raw(al, bh))
    if exact:
        return lax.dot_general(a, b, (dims, ((), ())), precision=HIGHEST, preferred_element_type=f32)
    return raw(a.astype(bf16), b.astype(bf16))


def _dot_sel(sel, b, dims, sel_side):
    raw = lambda p, q: lax.dot_general(p, q, (dims, ((), ())), preferred_element_type=f32)
    sel = sel.astype(bf16)
    parts = [raw(sel, p) if sel_side == 0 else raw(p, sel) for p in _pieces(b, 3)]
    return (parts[0] + parts[1]) + parts[2]


@jax.custom_vjp
def _sel_left(sel, b):
    return _dot_sel(sel, b, NN, 0)


def _sel_left_fwd(sel, b):
    return _dot_sel(sel, b, NN, 0), sel


def _sel_left_bwd(sel, g):
    return jnp.zeros_like(sel), _dot_sel(sel, g, TN, 0)


_sel_left.defvjp(_sel_left_fwd, _sel_left_bwd)


@jax.custom_vjp
def _sel_right(a, sel):
    return _dot_sel(sel, a, NN, 1)


def _sel_right_fwd(a, sel):
    return _dot_sel(sel, a, NN, 1), sel


def _sel_right_bwd(sel, g):
    return _dot_sel(sel, g, NT, 1), jnp.zeros_like(sel)


_sel_right.defvjp(_sel_right_fwd, _sel_right_bwd)


class _Matmuls:
    def __init__(self, exact, back):
        @jax.custom_vjp
        def nn(a, b):
            return _dot(a, b, NN, exact)

        def nn_fwd(a, b):
            return _dot(a, b, NN, exact), (a, b)

        def nn_bwd(res, g):
            a, b = res
            return _dot(g, b, NT, back), _dot(a, g, TN, back)

        nn.defvjp(nn_fwd, nn_bwd)

        @jax.custom_vjp
        def nt(a, b):
            return _dot(a, b, NT, exact)

        def nt_fwd(a, b):
            return _dot(a, b, NT, exact), (a, b)

        def nt_bwd(res, g):
            a, b = res
            return _dot(g, b, NN, back), _dot(g, a, TN, back)

        nt.defvjp(nt_fwd, nt_bwd)

        @jax.custom_vjp
        def tn(a, b):
            return _dot(a, b, TN, exact)

        def tn_fwd(a, b):
            return _dot(a, b, TN, exact), (a, b)

        def tn_bwd(res, g):
            a, b = res
            return _dot(b, g, NT, back), _dot(a, g, NN, back)

        tn.defvjp(tn_fwd, tn_bwd)
        self.nn, self.nt, self.tn = nn, nt, tn


MM = _Matmuls(exact=False, back=False)
MS = _Matmuls(exact="split", back=False)


def _each(fn, *lists):
    return [fn(*args) for args in zip(*lists)]


def _inverse_products(a_lows):
    ri = lax.broadcasted_iota(jnp.int32, (CHUNK, CHUNK), 0)
    ci = lax.broadcasted_iota(jnp.int32, (CHUNK, CHUNK), 1)
    eye = (ri == ci).astype(f32)
    power = _each(lambda a: -a, a_lows)
    inv = _each(lambda p: eye + p, power)
    for _ in range(5):
        power = _each(lambda p: _dot(p, p, NN, "split"), power)
        inv = _each(lambda x, p: x + _dot(x, p, NN, "split"), inv, power)
    return inv


def _inverse_cotangents(invs, gs):
    left = _each(lambda t, g: _dot(t, g, TN, "split"), invs, gs)
    return _each(lambda l, t: -_dot(l, t, NT, "split"), left, invs)


@jax.custom_vjp
def _unit_lower_inverses(a_lows):
    return _inverse_products(a_lows)


def _unit_lower_inverses_fwd(a_lows):
    invs = _inverse_products(a_lows)
    return invs, invs


def _unit_lower_inverses_bwd(invs, gs):
    return (_inverse_cotangents(invs, gs),)


_unit_lower_inverses.defvjp(_unit_lower_inverses_fwd, _unit_lower_inverses_bwd)


@jax.custom_vjp
def _known_inverses(a_lows, invs):
    return invs


def _known_inverses_fwd(a_lows, invs):
    return invs, invs


def _known_inverses_bwd(invs, gs):
    return _inverse_cotangents(invs, gs), _each(jnp.zeros_like, invs)


_known_inverses.defvjp(_known_inverses_fwd, _known_inverses_bwd)


def _params(semantics=None):
    return pltpu.CompilerParams(dimension_semantics=semantics, vmem_limit_bytes=VMEM_LIMIT)


def _silu(x):
    return x * jax.nn.sigmoid(x)


def _group_ones(width, group):
    r = lax.broadcasted_iota(jnp.int32, (width, width), 0) // group
    c = lax.broadcasted_iota(jnp.int32, (width, width), 1) // group
    return (r == c).astype(f32)


def _hnorm(x, nw, scale, shift):
    xn = x * lax.rsqrt(jnp.mean(x * x, axis=-1, keepdims=True) + EPS)
    return xn * nw * (1.0 + scale) + shift


def _rope_tables(pos_col, invf_row, sign_row):
    ang = pos_col.astype(f32) * invf_row
    cos_t = jnp.cos(ang)
    sin_t = jnp.sin(ang) * sign_row
    return jnp.concatenate([cos_t] * 4, axis=1), jnp.concatenate([sin_t] * 4, axis=1)


def _rope_partner(x):
    lane = lax.broadcasted_iota(jnp.int32, x.shape, 1)
    width = x.shape[1]
    return jnp.where((lane % AT_DIM) < AT_DIM // 2, pltpu.roll(x, width - AT_DIM // 2, 1), pltpu.roll(x, AT_DIM // 2, 1))


def _pre_proj(x, pos_col, invf_row, sign_row, norm_w, scale, shift, w_cat):
    seq = x.shape[0]
    tm = 256

    def body(x_ref, pos_ref, invf_ref, sign_ref, nw_ref, sc_ref, sh_ref, w_ref, pdn_ref, pat_ref, zat_ref, h_ref):
        h = _hnorm(x_ref[...], nw_ref[...], sc_ref[...], sh_ref[...]).astype(bf16)
        h_ref[...] = h
        big = jnp.dot(h, w_ref[...], preferred_element_type=f32)
        pdn_ref[...] = big[:, :PDN_W]
        cos_t, sin_t = _rope_tables(pos_ref[...], invf_ref[...], sign_ref[...])
        q = big[:, PDN_W:PDN_W + AT_WIDTH]
        k = big[:, PDN_W + AT_WIDTH:PDN_W + 2 * AT_WIDTH]
        pat_ref[:, 0:AT_WIDTH] = q * cos_t + _rope_partner(q) * sin_t
        pat_ref[:, AT_WIDTH:2 * AT_WIDTH] = k * cos_t + _rope_partner(k) * sin_t
        pat_ref[:, 2 * AT_WIDTH:] = big[:, PDN_W + 2 * AT_WIDTH:PDN_W + PAT_W]
        zat_ref[...] = big[:, PDN_W + PAT_W:]

    row = lambda w: pl.BlockSpec((1, w), lambda i: (0, 0))
    tile = lambda w: pl.BlockSpec((tm, w), lambda i: (i, 0))
    return pl.pallas_call(
        body, name="pre_proj", grid=(seq // tm,),
        in_specs=[tile(D_MODEL), tile(1), row(128), row(128), row(D_MODEL), row(D_MODEL), row(D_MODEL),
                  pl.BlockSpec((D_MODEL, CAT_W), lambda i: (0, 0))],
        out_specs=[tile(PDN_W), tile(PAT_W), tile(AT_WIDTH), tile(D_MODEL)],
        out_shape=[jax.ShapeDtypeStruct((seq, PDN_W), f32), jax.ShapeDtypeStruct((seq, PAT_W), f32),
                   jax.ShapeDtypeStruct((seq, AT_WIDTH), f32), jax.ShapeDtypeStruct((seq, D_MODEL), bf16)],
        compiler_params=_params(("parallel",)),
    )(x, pos_col, invf_row, sign_row, norm_w, scale, shift, w_cat)


def _conv_taps(ext_ref, halo, cur, w8):
    rows = cur.shape[0]
    ext_ref[0:8, :] = halo
    ext_ref[8:8 + rows, :] = cur
    out = ext_ref[pl.ds(5, rows), :] * w8[0:1, :]
    for j in range(1, CONV_K):
        out = out + ext_ref[pl.ds(5 + j, rows), :] * w8[j:j + 1, :]
    return out


def _dn_pre(cq, ck, cv, ba, a_row, dt_row):
    ones = _group_ones(DN_WIDTH, DN_DIM)
    sq, sk, v = _silu(cq), _silu(ck), _silu(cv)
    qn = sq * lax.rsqrt(_sel_right(sq * sq, ones) + EPS)
    kn = sk * lax.rsqrt(_sel_right(sk * sk, ones) + EPS)
    beta_all = jax.nn.sigmoid(ba)
    g_all = -jnp.exp(a_row) * jax.nn.softplus(ba + dt_row)
    return qn, kn, v, beta_all, g_all


def _dn_intra(qs, ks, vs, betas, gs, known_invs=None):
    ri = lax.broadcasted_iota(jnp.int32, (CHUNK, CHUNK), 0)
    ci = lax.broadcasted_iota(jnp.int32, (CHUNK, CHUNK), 1)
    tril = ri >= ci
    strict = ri > ci
    lower = tril.astype(f32)
    each = _each
    g_wide = each(lambda g: jnp.broadcast_to(g, (CHUNK, DN_DIM)), gs)
    gc = each(lambda g: _sel_left(lower, g), g_wide)
    gc_sq = each(lambda g: _sel_left(lower, jnp.broadcast_to(g, (CHUNK, CHUNK))), gs)
    g_end = each(lambda g: _sel_left(jnp.ones((CHUNK, CHUNK), f32), g), g_wide)
    g_end8 = each(lambda g: _sel_left(jnp.ones((8, CHUNK), f32), g), g_wide)
    decay = each(lambda s: jnp.exp(jnp.where(tril, s - s.T, -jnp.inf)), gc_sq)
    qs = each(lambda q: q * (DN_DIM ** -0.5), qs)
    kb = each(lambda k, b: k * b, ks, betas)
    vb = each(lambda v, b: v * b, vs, betas)
    kk = each(MM.nt, kb, ks)
    qk = each(MM.nt, qs, ks)
    a_low = each(lambda p, d: jnp.where(strict, p * d, 0.0), kk, decay)
    inv = _unit_lower_inverses(a_low) if known_invs is None else _known_inverses(a_low, known_invs)
    e_gc = each(jnp.exp, gc)
    u = each(MS.nn, inv, vb)
    w = each(lambda x, k, e: MS.nn(x, k * e), inv, kb, e_gc)
    attn = each(lambda p, d: jnp.where(tril, p * d, 0.0), qk, decay)
    q_dec = each(lambda q, e: q * e, qs, e_gc)
    k_dec = each(lambda k, ge, c: k * jnp.exp(ge - c), ks, g_end, gc)
    return u, w, q_dec, k_dec, attn, each(jnp.exp, g_end8), inv


def _dn_step(us, ws, q_decs, k_decs, attns, e_ends, states):
    each = _each
    v_new = each(lambda u, w, s: u - MM.nn(w, s), us, ws, states)
    qs = each(MM.nn, q_decs, states)
    o = each(lambda a, b, c: a + MM.nn(b, c), qs, attns, v_new)
    new_states = each(lambda s, e, k, v: s * e + MM.tn(k, v), states, e_ends, k_decs, v_new)
    return o, new_states


INTRA_CHUNKS = 2
SCAN_CHUNKS = 4


def _intra_specs(nc):
    rows = INTRA_CHUNKS * CHUNK
    cur = pl.BlockSpec((rows, QKV_W), lambda i: (i, 0))
    halo = pl.BlockSpec((8, QKV_W), lambda i: (jnp.maximum(i * (rows // 8) - 1, 0), 0))
    ba = pl.BlockSpec((rows, BA_W), lambda i: (i, (QKV_W + DN_WIDTH) // BA_W))
    conv = pl.BlockSpec((8, QKV_W), lambda i: (0, 0))
    row = pl.BlockSpec((1, BA_W), lambda i: (0, 0))
    wide = pl.BlockSpec((rows, DN_WIDTH), lambda i: (i, 0))
    attn = pl.BlockSpec((INTRA_CHUNKS, DN_HEADS, CHUNK, CHUNK), lambda i: (i, 0, 0, 0))
    e_end = pl.BlockSpec((INTRA_CHUNKS, 8 * DN_HEADS, DN_DIM), lambda i: (i, 0, 0))
    return cur, halo, ba, conv, row, wide, attn, e_end


def _intra_items():
    return [(ci, h, slice(ci * CHUNK, (ci + 1) * CHUNK), slice(h * DN_DIM, (h + 1) * DN_DIM))
            for ci in range(INTRA_CHUNKS) for h in range(DN_HEADS)]


def _intra_inputs(items, qn, kn, v, beta_all, g_all):
    return ([qn[rows, lanes] for _, _, rows, lanes in items], [kn[rows, lanes] for _, _, rows, lanes in items],
            [v[rows, lanes] for _, _, rows, lanes in items], [beta_all[rows, h:h + 1] for _, h, rows, _ in items],
            [g_all[rows, DN_HEADS + h:DN_HEADS + h + 1] for _, h, rows, _ in items])


def _intra_shapes(seq):
    nc = seq // CHUNK
    wide = jax.ShapeDtypeStruct((seq, DN_WIDTH), f32)
    return [wide, wide, wide, wide, jax.ShapeDtypeStruct((nc, DN_HEADS, CHUNK, CHUNK), f32),
            jax.ShapeDtypeStruct((nc, 8 * DN_HEADS, DN_DIM), f32)]


def _dn_intra_forward(pdn, conv_w8, a_row, dt_row):
    seq = pdn.shape[0]
    nc = seq // CHUNK
    cur, halo, ba, conv, row, wide, attn, e_end = _intra_specs(nc)

    def body(cur_ref, halo_ref, ba_ref, w_ref, a_ref, dt_ref, u_ref, w_out_ref, qd_ref, kd_ref, attn_ref, e_ref, inv_ref,
             ext_ref):
        halo_rows = jnp.where(pl.program_id(0) > 0, halo_ref[...], 0.0)
        c = _conv_taps(ext_ref, halo_rows, cur_ref[...], w_ref[...])
        qn, kn, v, beta_all, g_all = _dn_pre(c[:, :DN_WIDTH], c[:, DN_WIDTH:2 * DN_WIDTH], c[:, 2 * DN_WIDTH:],
                                             ba_ref[...], a_ref[...], dt_ref[...])
        items = _intra_items()
        u, w, qd, kd, at, e8, inv = _dn_intra(*_intra_inputs(items, qn, kn, v, beta_all, g_all))
        for n, (ci, h, rows, lanes) in enumerate(items):
            u_ref[rows, lanes] = u[n]
            w_out_ref[rows, lanes] = w[n]
            qd_ref[rows, lanes] = qd[n]
            kd_ref[rows, lanes] = kd[n]
            attn_ref[ci, h] = at[n]
            e_ref[ci, 8 * h:8 * h + 8, :] = e8[n]
            inv_ref[ci, h] = inv[n]

    return pl.pallas_call(
        body, name="dn_intra_forward", grid=(nc // INTRA_CHUNKS,),
        in_specs=[cur, halo, ba, conv, row, row],
        out_specs=[wide, wide, wide, wide, attn, e_end, attn],
        out_shape=_intra_shapes(seq) + [jax.ShapeDtypeStruct((nc, DN_HEADS, CHUNK, CHUNK), f32)],
        scratch_shapes=[pltpu.VMEM((INTRA_CHUNKS * CHUNK + 8, QKV_W), f32)],
        compiler_params=_params(("parallel",)),
    )(pdn, pdn, pdn, conv_w8, a_row, dt_row)


def _scan_specs(nc, reverse):
    steps = nc // SCAN_CHUNKS
    at = (lambda i: steps - 1 - i) if reverse else (lambda i: i)
    wide = pl.BlockSpec((SCAN_CHUNKS * CHUNK, DN_WIDTH), lambda i: (at(i), 0))
    attn = pl.BlockSpec((SCAN_CHUNKS, DN_HEADS, CHUNK, CHUNK), lambda i: (at(i), 0, 0, 0))
    e_end = pl.BlockSpec((SCAN_CHUNKS, 8 * DN_HEADS, DN_DIM), lambda i: (at(i), 0, 0))
    states = pl.BlockSpec((SCAN_CHUNKS, DN_HEADS, DN_DIM, DN_DIM), lambda i: (at(i), 0, 0, 0))
    return wide, attn, e_end, states


def _step_inputs(ci, rows, lanes, u_ref, w_ref, qd_ref, kd_ref, attn_ref, e_ref):
    heads = range(DN_HEADS)
    return ([u_ref[rows, lanes[h]] for h in heads], [w_ref[rows, lanes[h]] for h in heads],
            [qd_ref[rows, lanes[h]] for h in heads], [kd_ref[rows, lanes[h]] for h in heads],
            [attn_ref[ci, h] for h in heads], [e_ref[ci, 8 * h:8 * h + 1, :] for h in heads])


def _dn_scan_forward(u, w, q_dec, k_dec, attn, e_end):
    seq = u.shape[0]
    nc = seq // CHUNK
    wide, attn_spec, e_spec, st_spec = _scan_specs(nc, reverse=False)

    def body(u_ref, w_ref, qd_ref, kd_ref, attn_ref, e_ref, o_ref, st_ref, state_ref):
        @pl.when(pl.program_id(0) == 0)
        def _():
            state_ref[...] = jnp.zeros_like(state_ref)

        heads = range(DN_HEADS)
        lanes = [slice(h * DN_DIM, (h + 1) * DN_DIM) for h in heads]
        states = [state_ref[h] for h in heads]
        for ci in range(SCAN_CHUNKS):
            rows = slice(ci * CHUNK, (ci + 1) * CHUNK)
            for h in heads:
                st_ref[ci, h] = states[h]
            o, states = _dn_step(*_step_inputs(ci, rows, lanes, u_ref, w_ref, qd_ref, kd_ref, attn_ref, e_ref), states)
            for h in heads:
                o_ref[rows, lanes[h]] = o[h]
        for h in heads:
            state_ref[h] = states[h]

    return pl.pallas_call(
        body, name="dn_scan_forward", grid=(nc // SCAN_CHUNKS,),
        in_specs=[wide, wide, wide, wide, attn_spec, e_spec],
        out_specs=[wide, st_spec],
        out_shape=[jax.ShapeDtypeStruct((seq, DN_WIDTH), f32), jax.ShapeDtypeStruct((nc, DN_HEADS, DN_DIM, DN_DIM), f32)],
        scratch_shapes=[pltpu.VMEM((DN_HEADS, DN_DIM, DN_DIM), f32)],
        compiler_params=_params(("arbitrary",)),
    )(u, w, q_dec, k_dec, attn, e_end)


def _dn_scan_backward(u, w, q_dec, k_dec, attn, e_end, states, d_o):
    seq = u.shape[0]
    nc = seq // CHUNK
    wide, attn_spec, e_spec, st_spec = _scan_specs(nc, reverse=True)

    def body(u_ref, w_ref, qd_ref, kd_ref, attn_ref, e_ref, st_ref, do_ref,
             du_ref, dw_ref, dqd_ref, dkd_ref, dattn_ref, de_ref, dstate_ref):
        @pl.when(pl.program_id(0) == 0)
        def _():
            dstate_ref[...] = jnp.zeros_like(dstate_ref)

        heads = range(DN_HEADS)
        lanes = [slice(h * DN_DIM, (h + 1) * DN_DIM) for h in heads]
        first_row = lax.broadcasted_iota(jnp.int32, (8, DN_DIM), 0) == 0
        dstates = [dstate_ref[h] for h in heads]
        for ci in reversed(range(SCAN_CHUNKS)):
            rows = slice(ci * CHUNK, (ci + 1) * CHUNK)
            _, step_vjp = jax.vjp(_dn_step, *_step_inputs(ci, rows, lanes, u_ref, w_ref, qd_ref, kd_ref, attn_ref, e_ref),
                                  [st_ref[ci, h] for h in heads])
            du, dw, dqd, dkd, dattn, de, dstates = step_vjp(([do_ref[rows, lanes[h]] for h in heads], dstates))
            for h in heads:
                du_ref[rows, lanes[h]] = du[h]
                dw_ref[rows, lanes[h]] = dw[h]
                dqd_ref[rows, lanes[h]] = dqd[h]
                dkd_ref[rows, lanes[h]] = dkd[h]
                dattn_ref[ci, h] = dattn[h]
                de_ref[ci, 8 * h:8 * h + 8, :] = jnp.where(first_row, jnp.broadcast_to(de[h], (8, DN_DIM)), 0.0)
        for h in heads:
            dstate_ref[h] = dstates[h]

    return pl.pallas_call(
        body, name="dn_scan_backward", grid=(nc // SCAN_CHUNKS,),
        in_specs=[wide, wide, wide, wide, attn_spec, e_spec, st_spec, wide],
        out_specs=[wide, wide, wide, wide, attn_spec, e_spec],
        out_shape=_intra_shapes(seq),
        scratch_shapes=[pltpu.VMEM((DN_HEADS, DN_DIM, DN_DIM), f32)],
        compiler_params=_params(("arbitrary",)),
    )(u, w, q_dec, k_dec, attn, e_end, states, d_o)


def _dn_intra_backward(pdn, conv_w8, a_row, dt_row, invs, d_u, d_w, d_qd, d_kd, d_attn, d_e):
    seq = pdn.shape[0]
    nc = seq // CHUNK
    rows_per_step = INTRA_CHUNKS * CHUNK
    cur, halo, ba, conv, row, wide, attn, e_end = _intra_specs(nc)

    def body(cur_ref, halo_ref, ba_ref, w_ref, a_ref, dt_ref, inv_ref, du_ref, dw_ref, dqd_ref, dkd_ref, dattn_ref, de_ref,
             dconv_ref, dba_ref, drow_ref, ext_ref):
        @pl.when(pl.program_id(0) == 0)
        def _():
            drow_ref[...] = jnp.zeros_like(drow_ref)

        halo_rows = jnp.where(pl.program_id(0) > 0, halo_ref[...], 0.0)
        c = _conv_taps(ext_ref, halo_rows, cur_ref[...], w_ref[...])
        (qn, kn, v, beta_all, g_all), pre_vjp = jax.vjp(
            _dn_pre, c[:, :DN_WIDTH], c[:, DN_WIDTH:2 * DN_WIDTH], c[:, 2 * DN_WIDTH:], ba_ref[...], a_ref[...], dt_ref[...])
        lane = lax.broadcasted_iota(jnp.int32, (CHUNK, BA_W), 1)
        items = _intra_items()
        _, intra_vjp = jax.vjp(_dn_intra, *_intra_inputs(items, qn, kn, v, beta_all, g_all),
                               [inv_ref[ci, h] for ci, h, _, _ in items])
        dq, dk, dv, dbeta, dg, _ = intra_vjp((
            [du_ref[rows, lanes] for _, _, rows, lanes in items], [dw_ref[rows, lanes] for _, _, rows, lanes in items],
            [dqd_ref[rows, lanes] for _, _, rows, lanes in items], [dkd_ref[rows, lanes] for _, _, rows, lanes in items],
            [dattn_ref[ci, h] for ci, h, _, _ in items], [de_ref[ci, 8 * h:8 * h + 8, :] for ci, h, _, _ in items],
            [jnp.zeros((CHUNK, CHUNK), f32) for _ in items]))
        dq_rows, dk_rows, dv_rows, dbeta_rows, dg_rows = [], [], [], [], []
        for ci in range(INTRA_CHUNKS):
            of_chunk = [n for n, item in enumerate(items) if item[0] == ci]
            d_beta_all = jnp.zeros((CHUNK, BA_W), f32)
            d_g_all = jnp.zeros((CHUNK, BA_W), f32)
            for n in of_chunk:
                h = items[n][1]
                d_beta_all = d_beta_all + jnp.where(lane == h, dbeta[n], 0.0)
                d_g_all = d_g_all + jnp.where(lane == DN_HEADS + h, dg[n], 0.0)
            dq_rows.append(jnp.concatenate([dq[n] for n in of_chunk], axis=1))
            dk_rows.append(jnp.concatenate([dk[n] for n in of_chunk], axis=1))
            dv_rows.append(jnp.concatenate([dv[n] for n in of_chunk], axis=1))
            dbeta_rows.append(d_beta_all)
            dg_rows.append(d_g_all)
        stack = lambda parts: jnp.concatenate(parts, axis=0)
        dcq, dck, dcv, dba, da_row, ddt_row = pre_vjp(
            (stack(dq_rows), stack(dk_rows), stack(dv_rows), stack(dbeta_rows), stack(dg_rows)))
        dconv_ref[:, :DN_WIDTH] = dcq
        dconv_ref[:, DN_WIDTH:2 * DN_WIDTH] = dck
        dconv_ref[:, 2 * DN_WIDTH:] = dcv
        dba_ref[...] = dba
        drow_ref[0:1, :] += da_row
        drow_ref[1:2, :] += ddt_row

    return pl.pallas_call(
        body, name="dn_intra_backward", grid=(nc // INTRA_CHUNKS,),
        in_specs=[cur, halo, ba, conv, row, row, attn, wide, wide, wide, wide, attn, e_end],
        out_specs=[pl.BlockSpec((rows_per_step, QKV_W), lambda i: (i, 0)),
                   pl.BlockSpec((rows_per_step, BA_W), lambda i: (i, 0)),
                   pl.BlockSpec((8, BA_W), lambda i: (0, 0))],
        out_shape=[jax.ShapeDtypeStruct((seq, QKV_W), f32), jax.ShapeDtypeStruct((seq, BA_W), f32),
                   jax.ShapeDtypeStruct((8, BA_W), f32)],
        scratch_shapes=[pltpu.VMEM((rows_per_step + 8, QKV_W), f32)],
        compiler_params=_params(("arbitrary",)),
    )(pdn, pdn, pdn, conv_w8, a_row, dt_row, invs, d_u, d_w, d_qd, d_kd, d_attn, d_e)


def _band_masks(block_index):
    qi = lax.broadcasted_iota(jnp.int32, (Q_BLOCK, Q_BLOCK), 0)
    kj = lax.broadcasted_iota(jnp.int32, (Q_BLOCK, Q_BLOCK), 1)
    return (kj >= qi) & (block_index > 0), kj <= qi


def _attn_forward(pat_view, dil):
    length = pat_view.shape[0]
    nb = length // Q_BLOCK
    scale = AT_DIM ** -0.5

    def body(q_ref, kp_ref, kc_ref, vp_ref, vc_ref, o_ref, lse_ref):
        mask_prev, mask_cur = _band_masks(pl.program_id(1))
        heads = [slice(h * AT_DIM, (h + 1) * AT_DIM) for h in range(AT_HEADS)]
        s_prev = [_dot(q_ref[:, lanes], kp_ref[:, lanes], NT, False) for lanes in heads]
        s_cur = [_dot(q_ref[:, lanes], kc_ref[:, lanes], NT, False) for lanes in heads]
        p_prev, p_cur, norm = [], [], []
        for h, lanes in enumerate(heads):
            sp = jnp.where(mask_prev, s_prev[h] * scale, NEG)
            sc = jnp.where(mask_cur, s_cur[h] * scale, NEG)
            m = jnp.maximum(jnp.max(sp, axis=1, keepdims=True), jnp.max(sc, axis=1, keepdims=True))
            pp = jnp.exp(sp - m)
            pc = jnp.exp(sc - m)
            l = jnp.sum(pp, axis=1, keepdims=True) + jnp.sum(pc, axis=1, keepdims=True)
            lse_ref[:, lanes] = jnp.broadcast_to(m + jnp.log(l), (Q_BLOCK, AT_DIM))
            p_prev.append(pp.astype(bf16))
            p_cur.append(pc.astype(bf16))
            norm.append(1.0 / l)
        o_prev = [_dot(p_prev[h], vp_ref[:, lanes], NN, False) for h, lanes in enumerate(heads)]
        o_cur = [_dot(p_cur[h], vc_ref[:, lanes], NN, False) for h, lanes in enumerate(heads)]
        for h, lanes in enumerate(heads):
            o_ref[:, lanes] = (o_prev[h] + o_cur[h]) * norm[h]

    def blk(piece, prev):
        if prev:
            return pl.BlockSpec((Q_BLOCK, AT_WIDTH), lambda r, n: (jnp.maximum(n - 1, 0), 3 * r + piece))
        return pl.BlockSpec((Q_BLOCK, AT_WIDTH), lambda r, n: (n, 3 * r + piece))

    out = pl.BlockSpec((Q_BLOCK, AT_WIDTH), lambda r, n: (n, r))
    return pl.pallas_call(
        body, name=f"attn_forward_d{dil}", grid=(dil, nb),
        in_specs=[blk(0, False), blk(1, True), blk(1, False), blk(2, True), blk(2, False)],
        out_specs=[out, out],
        out_shape=[jax.ShapeDtypeStruct((length, dil * AT_WIDTH), f32)] * 2,
        compiler_params=_params(("parallel", "parallel")),
    )(pat_view, pat_view, pat_view, pat_view, pat_view)


def _attn_backward_q(pat_view, d_out, delta, lse, dil):
    length = pat_view.shape[0]
    nb = length // Q_BLOCK
    scale = AT_DIM ** -0.5

    def body(q_ref, kp_ref, kc_ref, vp_ref, vc_ref, do_ref, dl_ref, lse_ref, dq_ref):
        mask_prev, mask_cur = _band_masks(pl.program_id(1))
        heads = [slice(h * AT_DIM, (h + 1) * AT_DIM) for h in range(AT_HEADS)]
        sides = ((mask_prev, kp_ref, vp_ref), (mask_cur, kc_ref, vc_ref))
        s = [[_dot(q_ref[:, lanes], k_ref[:, lanes], NT, False) for lanes in heads] for _, k_ref, _ in sides]
        dp = [[_dot(do_ref[:, lanes], v_ref[:, lanes], NT, False) for lanes in heads] for _, _, v_ref in sides]
        ds = [[], []]
        for h, lanes in enumerate(heads):
            lse_h = lse_ref[:, h * AT_DIM:h * AT_DIM + 1]
            dl_h = dl_ref[:, h * AT_DIM:h * AT_DIM + 1]
            for side, (mask, _, _) in enumerate(sides):
                p = jnp.exp(jnp.where(mask, s[side][h] * scale - lse_h, NEG))
                ds[side].append((p * (dp[side][h] - dl_h)).astype(bf16))
        dq = [[_dot(ds[side][h], k_ref[:, lanes], NN, False) for h, lanes in enumerate(heads)]
              for side, (_, k_ref, _) in enumerate(sides)]
        for h, lanes in enumerate(heads):
            dq_ref[:, lanes] = (dq[0][h] + dq[1][h]) * scale

    def blk(piece, prev):
        if prev:
            return pl.BlockSpec((Q_BLOCK, AT_WIDTH), lambda r, n: (jnp.maximum(n - 1, 0), 3 * r + piece))
        return pl.BlockSpec((Q_BLOCK, AT_WIDTH), lambda r, n: (n, 3 * r + piece))

    one = pl.BlockSpec((Q_BLOCK, AT_WIDTH), lambda r, n: (n, r))
    return pl.pallas_call(
        body, name=f"attn_backward_q_d{dil}", grid=(dil, nb),
        in_specs=[blk(0, False), blk(1, True), blk(1, False), blk(2, True), blk(2, False), one, one, one],
        out_specs=one,
        out_shape=jax.ShapeDtypeStruct((length, dil * AT_WIDTH), f32),
        compiler_params=_params(("parallel", "parallel")),
    )(pat_view, pat_view, pat_view, pat_view, pat_view, d_out, delta, lse)


def _attn_backward_kv(pat_view, d_out, delta, lse, dil):
    length = pat_view.shape[0]
    nb = length // Q_BLOCK
    scale = AT_DIM ** -0.5

    def body(k_ref, v_ref, qa_ref, qb_ref, doa_ref, dob_ref, dla_ref, dlb_ref, lsea_ref, lseb_ref, dk_ref, dv_ref):
        j = pl.program_id(1)
        qi = lax.broadcasted_iota(jnp.int32, (Q_BLOCK, Q_BLOCK), 0)
        kj = lax.broadcasted_iota(jnp.int32, (Q_BLOCK, Q_BLOCK), 1)
        mask_same = kj <= qi
        mask_next = (kj >= qi) & (j + 1 < nb)
        heads = [slice(h * AT_DIM, (h + 1) * AT_DIM) for h in range(AT_HEADS)]
        sides = ((mask_same, qa_ref, doa_ref, dla_ref, lsea_ref), (mask_next, qb_ref, dob_ref, dlb_ref, lseb_ref))
        s = [[_dot(q_ref[:, lanes], k_ref[:, lanes], NT, False) for lanes in heads] for _, q_ref, _, _, _ in sides]
        dp = [[_dot(do_ref[:, lanes], v_ref[:, lanes], NT, False) for lanes in heads] for _, _, do_ref, _, _ in sides]
        p, ds = [[], []], [[], []]
        for h, lanes in enumerate(heads):
            one = slice(h * AT_DIM, h * AT_DIM + 1)
            for side, (mask, _, _, dl_ref, lse_ref) in enumerate(sides):
                prob = jnp.exp(jnp.where(mask, s[side][h] * scale - lse_ref[:, one], NEG))
                p[side].append(prob.astype(bf16))
                ds[side].append((prob * (dp[side][h] - dl_ref[:, one])).astype(bf16))
        dv = [[_dot(p[side][h], do_ref[:, lanes], TN, False) for h, lanes in enumerate(heads)]
              for side, (_, _, do_ref, _, _) in enumerate(sides)]
        dk = [[_dot(ds[side][h], q_ref[:, lanes], TN, False) for h, lanes in enumerate(heads)]
              for side, (_, q_ref, _, _, _) in enumerate(sides)]
        for h, lanes in enumerate(heads):
            dk_ref[:, lanes] = (dk[0][h] + dk[1][h]) * scale
            dv_ref[:, lanes] = dv[0][h] + dv[1][h]

    def kv(piece):
        return pl.BlockSpec((Q_BLOCK, AT_WIDTH), lambda r, n: (n, 3 * r + piece))

    q_same = pl.BlockSpec((Q_BLOCK, AT_WIDTH), lambda r, n: (n, 3 * r))
    q_next = pl.BlockSpec((Q_BLOCK, AT_WIDTH), lambda r, n: (jnp.minimum(n + 1, nb - 1), 3 * r))
    same = pl.BlockSpec((Q_BLOCK, AT_WIDTH), lambda r, n: (n, r))
    nxt = pl.BlockSpec((Q_BLOCK, AT_WIDTH), lambda r, n: (jnp.minimum(n + 1, nb - 1), r))
    return pl.pallas_call(
        body, name=f"attn_backward_kv_d{dil}", grid=(dil, nb),
        in_specs=[kv(1), kv(2), q_same, q_next, same, nxt, same, nxt, same, nxt],
        out_specs=[same, same],
        out_shape=[jax.ShapeDtypeStruct((length, dil * AT_WIDTH), f32)] * 2,
        compiler_params=_params(("parallel", "parallel")),
    )(pat_view, pat_view, pat_view, pat_view, d_out, d_out, delta, delta, lse, lse)


def _to_view(a, dil):
    return a if dil == 1 else a.reshape(a.shape[0] // dil, dil * a.shape[1])


def _from_view(a, dil):
    return a if dil == 1 else a.reshape(a.shape[0] * dil, a.shape[1] // dil)


def _gated_norms(o_dn, z_dn, o_at, z_at, dn_w, at_w):
    ms_dn = _sel_right(o_dn * o_dn, _group_ones(DN_WIDTH, DN_DIM)) * (1.0 / DN_DIM)
    a = o_dn * lax.rsqrt(ms_dn + EPS) * dn_w * _silu(z_dn)
    ms_at = _sel_right(o_at * o_at, _group_ones(AT_WIDTH, AT_DIM)) * (1.0 / AT_DIM)
    b = o_at * lax.rsqrt(ms_at + EPS) * at_w * _silu(z_at)
    return a, b


def _residual_loss(x, mix, gate, fin_w, target):
    x2 = x + gate * mix
    y = x2 * lax.rsqrt(jnp.mean(x2 * x2, axis=-1, keepdims=True) + EPS) * fin_w
    err = y - target
    per_token = jnp.sum(err * err, axis=1, keepdims=True) * (1.0 / D_MODEL)
    return 0.5 * jnp.sum(per_token, axis=0, keepdims=True)


TAIL_ROWS = 8


def _tail(x, target, o_dn, pdn, outs, lses, z_at, gate, w_out, dn_w_row, at_w_row, fin_w):
    seq = x.shape[0]
    tm = 256

    def body(x_ref, t_ref, odn_ref, zdn_ref, o1_ref, o4_ref, o16_ref, l1_ref, l4_ref, l16_ref, zat_ref, gate_ref,
             wt_ref, wb_ref, dnw_ref, atw_ref, fw_ref,
             dx2_ref, dodn_ref, dzdn_ref, doat_ref, delta_ref, lse_ref, dzat_ref, gwt_ref, gwb_ref, rows_ref):
        @pl.when(pl.program_id(0) == 0)
        def _():
            gwt_ref[...] = jnp.zeros_like(gwt_ref)
            gwb_ref[...] = jnp.zeros_like(gwb_ref)
            rows_ref[...] = jnp.zeros_like(rows_ref)

        l1, l4, l16 = l1_ref[...], l4_ref[...], l16_ref[...]
        top = jnp.maximum(jnp.maximum(l1, l4), l16)
        e1, e4, e16 = jnp.exp(l1 - top), jnp.exp(l4 - top), jnp.exp(l16 - top)
        den = e1 + e4 + e16
        o_at = (e1 * o1_ref[...] + e4 * o4_ref[...] + e16 * o16_ref[...]) / den
        lse_ref[...] = top + jnp.log(den)

        (a, b), norms_vjp = jax.vjp(_gated_norms, odn_ref[...], zdn_ref[...], o_at, zat_ref[...], dnw_ref[...], atw_ref[...])
        a, b = a.astype(bf16), b.astype(bf16)
        mix = jnp.dot(a, wt_ref[...], preferred_element_type=f32) + jnp.dot(b, wb_ref[...], preferred_element_type=f32)
        loss, loss_vjp = jax.vjp(_residual_loss, x_ref[...], mix, gate_ref[...], fw_ref[...], t_ref[...])
        dx2, dmix, dgate, dfw, _ = loss_vjp(jnp.ones((1, 1), f32))
        dmix = dmix.astype(bf16)
        dwt = lax.dot_general(a, dmix, (TN, ((), ())), preferred_element_type=f32)
        dwb = lax.dot_general(b, dmix, (TN, ((), ())), preferred_element_type=f32)
        da = lax.dot_general(dmix, wt_ref[...], (NT, ((), ())), preferred_element_type=f32)
        db = lax.dot_general(dmix, wb_ref[...], (NT, ((), ())), preferred_element_type=f32)
        dodn, dzdn, doat, dzat, ddnw, datw = norms_vjp((da, db))
        dx2_ref[...] = dx2
        dodn_ref[...] = dodn
        dzdn_ref[...] = dzdn
        doat_ref[...] = doat
        dzat_ref[...] = dzat
        delta_ref[...] = _sel_right(doat * o_at, _group_ones(AT_WIDTH, AT_DIM))
        gwt_ref[...] += dwt
        gwb_ref[...] += dwb
        rows_ref[0:1, :] += jnp.broadcast_to(loss, (1, D_MODEL))
        rows_ref[1:2, :] += dgate
        rows_ref[2:3, :] += dfw
        rows_ref[3:4, 0:DN_WIDTH] += ddnw
        rows_ref[4:5, 0:AT_WIDTH] += datw

    tile = lambda w: pl.BlockSpec((tm, w), lambda i: (i, 0))
    row = lambda w: pl.BlockSpec((1, w), lambda i: (0, 0))
    half_w = pl.BlockSpec((DN_WIDTH, D_MODEL), lambda i: (0, 0))
    sds = lambda w: jax.ShapeDtypeStruct((seq, w), f32)
    return pl.pallas_call(
        body, name="tail", grid=(seq // tm,),
        in_specs=[tile(D_MODEL), tile(D_MODEL), tile(DN_WIDTH),
                  pl.BlockSpec((tm, DN_WIDTH), lambda i: (i, QKV_W // DN_WIDTH)),
                  tile(AT_WIDTH), tile(AT_WIDTH), tile(AT_WIDTH), tile(AT_WIDTH), tile(AT_WIDTH), tile(AT_WIDTH),
                  tile(AT_WIDTH), row(D_MODEL), half_w, pl.BlockSpec((AT_WIDTH, D_MODEL), lambda i: (1, 0)),
                  row(DN_WIDTH), row(AT_WIDTH), row(D_MODEL)],
        out_specs=[tile(D_MODEL), tile(DN_WIDTH), tile(DN_WIDTH), tile(AT_WIDTH), tile(AT_WIDTH), tile(AT_WIDTH),
                   tile(AT_WIDTH), half_w, half_w, pl.BlockSpec((TAIL_ROWS, D_MODEL), lambda i: (0, 0))],
        out_shape=[sds(D_MODEL), sds(DN_WIDTH), sds(DN_WIDTH), sds(AT_WIDTH), sds(AT_WIDTH), sds(AT_WIDTH),
                   sds(AT_WIDTH), jax.ShapeDtypeStruct((DN_WIDTH, D_MODEL), f32),
                   jax.ShapeDtypeStruct((AT_WIDTH, D_MODEL), f32), jax.ShapeDtypeStruct((TAIL_ROWS, D_MODEL), f32)],
        compiler_params=_params(("arbitrary",)),
    )(x, target, o_dn, pdn, outs[0], outs[1], outs[2], lses[0], lses[1], lses[2], z_at, gate, w_out, w_out,
      dn_w_row, at_w_row, fin_w)


PRE_ROWS = 8


def _pre_backward(x, dx2, pos_col, invf_row, sign_row, norm_w, scale, shift, w_cat, conv_w8, pdn, d_conv, d_zdn, d_ba,
                  d_q, d_k, d_v, d_zat):
    seq = x.shape[0]
    tm = 256
    last = seq // tm - 1

    def body(x_ref, dx2_ref, pos_ref, invf_ref, sign_ref, nw_ref, sc_ref, sh_ref, w_ref, cw_ref,
             pre_ref, prehalo_ref, dc_ref, dchalo_ref, dz_ref, dba_ref,
             dq1_ref, dq4_ref, dq16_ref, dk1_ref, dk4_ref, dk16_ref, dv1_ref, dv4_ref, dv16_ref, dzat_ref,
             gx_ref, dproj_ref, rows_ref, crow_ref, ext_ref):
        i = pl.program_id(0)

        @pl.when(i == 0)
        def _():
            rows_ref[...] = jnp.zeros_like(rows_ref)
            crow_ref[...] = jnp.zeros_like(crow_ref)

        dc = dc_ref[...]
        ext_ref[0:tm, :] = dc
        ext_ref[tm:tm + 8, :] = jnp.where(i < last, dchalo_ref[...], 0.0)
        w8 = cw_ref[...]
        d_pre = ext_ref[pl.ds(3, tm), :] * w8[0:1, :]
        for j in range(1, CONV_K):
            d_pre = d_pre + ext_ref[pl.ds(3 - j, tm), :] * w8[j:j + 1, :]
        ext_ref[0:8, :] = jnp.where(i > 0, prehalo_ref[...], 0.0)
        ext_ref[8:8 + tm, :] = pre_ref[...]
        for j in range(CONV_K):
            crow_ref[j:j + 1, :] += jnp.sum(dc * ext_ref[pl.ds(5 + j, tm), :], axis=0, keepdims=True)

        cos_t, sin_t = _rope_tables(pos_ref[...], invf_ref[...], sign_ref[...])
        dq = dq1_ref[...] + dq4_ref[...] + dq16_ref[...]
        dk = dk1_ref[...] + dk4_ref[...] + dk16_ref[...]
        dq = dq * cos_t + _rope_partner(dq * sin_t)
        dk = dk * cos_t + _rope_partner(dk * sin_t)
        dv = dv1_ref[...] + dv4_ref[...] + dv16_ref[...]
        d_proj = jnp.concatenate([d_pre, dz_ref[...], dba_ref[...], dq, dk, dv, dzat_ref[...]], axis=1).astype(bf16)
        dproj_ref[...] = d_proj
        dh = lax.dot_general(d_proj, w_ref[...], (NT, ((), ())), preferred_element_type=f32)
        _, vjp = jax.vjp(_hnorm, x_ref[...], nw_ref[...], sc_ref[...], sh_ref[...])
        dx, dnw, dsc, dsh = vjp(dh)
        gx_ref[...] = dx + dx2_ref[...]
        rows_ref[0:1, :] += dnw
        rows_ref[1:2, :] += dsc
        rows_ref[2:3, :] += dsh

    tile = lambda w: pl.BlockSpec((tm, w), lambda i: (i, 0))
    row = lambda w: pl.BlockSpec((1, w), lambda i: (0, 0))
    step8 = tm // 8
    return pl.pallas_call(
        body, name="pre_backward", grid=(seq // tm,),
        in_specs=[tile(D_MODEL), tile(D_MODEL), tile(1), row(128), row(128), row(D_MODEL), row(D_MODEL), row(D_MODEL),
                  pl.BlockSpec((D_MODEL, CAT_W), lambda i: (0, 0)), pl.BlockSpec((8, QKV_W), lambda i: (0, 0)),
                  tile(QKV_W), pl.BlockSpec((8, QKV_W), lambda i: (jnp.maximum(i * step8 - 1, 0), 0)),
                  tile(QKV_W), pl.BlockSpec((8, QKV_W), lambda i: (jnp.minimum((i + 1) * step8, seq // 8 - 1), 0)),
                  tile(DN_WIDTH), tile(BA_W)] + [tile(AT_WIDTH)] * 10,
        out_specs=[tile(D_MODEL), tile(CAT_W), pl.BlockSpec((PRE_ROWS, D_MODEL), lambda i: (0, 0)),
                   pl.BlockSpec((8, QKV_W), lambda i: (0, 0))],
        out_shape=[jax.ShapeDtypeStruct((seq, D_MODEL), f32), jax.ShapeDtypeStruct((seq, CAT_W), bf16),
                   jax.ShapeDtypeStruct((PRE_ROWS, D_MODEL), f32), jax.ShapeDtypeStruct((8, QKV_W), f32)],
        scratch_shapes=[pltpu.VMEM((tm + 8, QKV_W), f32)],
        compiler_params=_params(("arbitrary",)),
    )(x, dx2, pos_col, invf_row, sign_row, norm_w, scale, shift, w_cat, conv_w8, pdn, pdn, d_conv, d_conv, d_zdn, d_ba,
      d_q[0], d_q[1], d_q[2], d_k[0], d_k[1], d_k[2], d_v[0], d_v[1], d_v[2], d_zat)


def _weight_grad(h, d_proj):
    seq = h.shape[0]
    tk, tn = 1024, 384
    n_k = seq // tk

    def body(h_ref, d_ref, o_ref):
        @pl.when(pl.program_id(1) == 0)
        def _():
            o_ref[...] = jnp.zeros_like(o_ref)

        o_ref[...] += lax.dot_general(h_ref[...], d_ref[...], (TN, ((), ())), preferred_element_type=f32)

    return pl.pallas_call(
        body, name="weight_grad", grid=(CAT_W // tn, n_k),
        in_specs=[pl.BlockSpec((tk, D_MODEL), lambda n, k: (k, 0)), pl.BlockSpec((tk, tn), lambda n, k: (k, n))],
        out_specs=pl.BlockSpec((D_MODEL, tn), lambda n, k: (0, n)),
        out_shape=jax.ShapeDtypeStruct((D_MODEL, CAT_W), f32),
        compiler_params=_params(("parallel", "arbitrary")),
    )(h, d_proj)


def _adamw(w, g, m, v):
    m = ADAM_B1 * m + (1.0 - ADAM_B1) * g
    v = ADAM_B2 * v + (1.0 - ADAM_B2) * (g * g)
    m_hat = m / (1.0 - ADAM_B1 ** ADAM_STEP)
    v_hat = v / (1.0 - ADAM_B2 ** ADAM_STEP)
    delta = -ADAM_LR * (m_hat / (jnp.sqrt(v_hat) + ADAM_EPS) + ADAM_WD * w)
    return delta, m, v


def _mod_shard(c_all, w_mod_s, b_mod_s):
    def body(c_ref, w_ref, b_ref, o_ref):
        o_ref[...] = _dot(_silu(c_ref[...]), w_ref[...], NN, True) + b_ref[...]

    return pl.pallas_call(body, name="mod_shard", out_shape=jax.ShapeDtypeStruct((N_DEV, w_mod_s.shape[1]), f32),
                          compiler_params=_params())(c_all, w_mod_s, b_mod_s)


def _mod_update(c_all, d_mod_s, w, m, v):
    def body(c_ref, d_ref, w_ref, m_ref, v_ref, g_ref, dw_ref, nm_ref, nv_ref):
        g = _dot(_silu(c_ref[...]), d_ref[...], TN, True)
        g_ref[...] = g
        dw_ref[...], nm_ref[...], nv_ref[...] = _adamw(w_ref[...], g, m_ref[...], v_ref[...])

    return pl.pallas_call(body, name="mod_update", out_shape=[jax.ShapeDtypeStruct(w.shape, f32)] * 4,
                          compiler_params=_params())(c_all, d_mod_s, w, m, v)


def _adamw_rows(w, g, m, v, name):
    rows, width = w.shape
    tr = PACK_CHUNK if rows % PACK_CHUNK == 0 else rows

    def body(w_ref, g_ref, m_ref, v_ref, dw_ref, nm_ref, nv_ref):
        dw_ref[...], nm_ref[...], nv_ref[...] = _adamw(w_ref[...], g_ref[...], m_ref[...], v_ref[...])

    spec = pl.BlockSpec((tr, width), lambda i: (i, 0))
    return pl.pallas_call(body, name=name, grid=(rows // tr,), in_specs=[spec] * 4, out_specs=[spec] * 3,
                          out_shape=[jax.ShapeDtypeStruct(w.shape, f32)] * 3,
                          compiler_params=_params(("parallel",)))(w, g, m, v)


SM_NORM, SM_FIN, SM_DN, SM_AT, SM_A, SM_DT, SM_LOSS, SM_MOD = 0, 1024, 2048, 2560, 3072, 3200, 3328, 3456
SM_W = SM_MOD + 3 * D_MODEL
RS_NORM, RS_FIN, RS_DN, RS_AT, RS_A, RS_DT, RS_BMOD = 0, 1024, 2048, 2176, 2304, 2432, 2560
RS_W = RS_BMOD + 3 * D_MODEL


def _small_update(gathered, w, m, v):
    def body(g_ref, w_ref, m_ref, v_ref, grad_ref, dw_ref, nm_ref, nv_ref, loss_ref, dmod_ref):
        total = g_ref[0:1, :]
        for dev in range(1, N_DEV):
            total = total + g_ref[8 * dev:8 * dev + 1, :]
        for dev in range(N_DEV):
            dmod_ref[dev:dev + 1, :] = g_ref[8 * dev:8 * dev + 1, SM_MOD:SM_W]
        dn = total[:, SM_DN:SM_DN + DN_DIM]
        for h in range(1, DN_HEADS):
            dn = dn + total[:, SM_DN + h * DN_DIM:SM_DN + (h + 1) * DN_DIM]
        at = total[:, SM_AT:SM_AT + AT_DIM]
        for h in range(1, AT_HEADS):
            at = at + total[:, SM_AT + h * AT_DIM:SM_AT + (h + 1) * AT_DIM]
        grad_ref[:, RS_NORM:RS_FIN] = total[:, SM_NORM:SM_FIN]
        grad_ref[:, RS_FIN:RS_DN] = total[:, SM_FIN:SM_DN]
        grad_ref[:, RS_DN:RS_AT] = dn
        grad_ref[:, RS_AT:RS_A] = jnp.zeros((1, 128), f32)
        grad_ref[:, RS_AT:RS_AT + AT_DIM] = at
        grad_ref[:, RS_A:RS_DT] = total[:, SM_A:SM_DT]
        grad_ref[:, RS_DT:RS_BMOD] = total[:, SM_DT:SM_LOSS]
        grad_ref[:, RS_BMOD:RS_W] = total[:, SM_MOD:SM_W]
        loss_ref[...] = total[:, SM_LOSS:SM_MOD]
        dw_ref[...], nm_ref[...], nv_ref[...] = _adamw(w_ref[...], grad_ref[...], m_ref[...], v_ref[...])

    row = jax.ShapeDtypeStruct((1, RS_W), f32)
    return pl.pallas_call(
        body, name="small_update",
        out_shape=[row, row, row, row, jax.ShapeDtypeStruct((1, 128), f32), jax.ShapeDtypeStruct((N_DEV, 3 * D_MODEL), f32)],
        compiler_params=_params())(gathered, w, m, v)


def _all_gather_rows(block, name):
    m_per, n = block.shape

    def body(x_ref, out_ref, send_sems, recv_sems, local_sem):
        x, y, c = lax.axis_index("x"), lax.axis_index("y"), lax.axis_index("c")
        me, sibling = (x, y, c), (x, y, 1 - c)
        chips = [(1 - x, y), (x, 1 - y), (1 - x, 1 - y)]

        def rows(px, py, pc):
            return out_ref.at[pl.ds((4 * px + 2 * py + pc) * m_per, m_per), :]

        def copy(k, blk, to, src=None):
            return pltpu.make_async_remote_copy(
                src_ref=rows(*blk) if src is None else src, dst_ref=rows(*blk),
                send_sem=send_sems.at[k], recv_sem=recv_sems.at[k], device_id=to, device_id_type=MESH)

        mine = pltpu.make_async_copy(x_ref, rows(*me), local_sem)
        mine.start()
        first = [copy(0, me, sibling, src=x_ref)]
        first += [copy(1 + j, me, (*chip, c), src=x_ref) for j, chip in enumerate(chips)]
        for cp in first:
            cp.start()
        passed = [copy(4 + j, (*chip, c), sibling) for j, chip in enumerate(chips)]
        for j, chip in enumerate(chips):
            copy(1 + j, (*chip, c), me).wait_recv()
            passed[j].start()
        copy(0, sibling, me).wait_recv()
        for j, chip in enumerate(chips):
            copy(4 + j, (*chip, 1 - c), me).wait_recv()
        for cp in first + passed:
            cp.wait_send()
        mine.wait()

    return pl.pallas_call(
        body, name=name,
        out_shape=jax.ShapeDtypeStruct((N_DEV * m_per, n), block.dtype),
        in_specs=[pl.BlockSpec(memory_space=pltpu.VMEM)],
        out_specs=pl.BlockSpec(memory_space=pltpu.VMEM),
        scratch_shapes=[pltpu.SemaphoreType.DMA((7,)), pltpu.SemaphoreType.DMA((7,)), pltpu.SemaphoreType.DMA],
        compiler_params=pltpu.CompilerParams(vmem_limit_bytes=VMEM_LIMIT),
    )(block)


def _gather_weight_shards(packed):
    rows, width = packed.shape

    def body(src_ref, out_ref, send_sems, recv_sems, local_sem):
        x, y, c = lax.axis_index("x"), lax.axis_index("y"), lax.axis_index("c")
        chips = [(1 - x, y), (x, 1 - y), (1 - x, 1 - y)]

        def copy(k, owner, to):
            slot = out_ref.at[2 * owner[0] + owner[1]]
            return pltpu.make_async_remote_copy(
                src_ref=src_ref, dst_ref=slot, send_sem=send_sems.at[k], recv_sem=recv_sems.at[k],
                device_id=(*to, c), device_id_type=MESH)

        mine = pltpu.make_async_copy(src_ref, out_ref.at[2 * x + y], local_sem)
        mine.start()
        sends = [copy(k, (x, y), chip) for k, chip in enumerate(chips)]
        for cp in sends:
            cp.start()
        for k, chip in enumerate(chips):
            copy(k, chip, (x, y)).wait_recv()
        for cp in sends:
            cp.wait_send()
        mine.wait()

    return pl.pallas_call(
        body, name="gather_weight_shards",
        out_shape=jax.ShapeDtypeStruct((N_CHIPS, rows, width), packed.dtype),
        in_specs=[pl.BlockSpec(memory_space=pl.ANY)],
        out_specs=pl.BlockSpec(memory_space=pl.ANY),
        scratch_shapes=[pltpu.SemaphoreType.DMA((3,)), pltpu.SemaphoreType.DMA((3,)), pltpu.SemaphoreType.DMA],
    )(packed)


def _reduce_weight_grads(grads, grads_bf):
    _, rows, width = grads.shape
    n_chunks = rows // PACK_CHUNK

    def body(g_ref, gbf_ref, out_ref, own_ref, land_ref, part_ref, sib_ref, send_sems, recv_sems, local_sem):
        x, y, c = lax.axis_index("x"), lax.axis_index("y"), lax.axis_index("c")
        chips = [(1 - x, y), (x, 1 - y), (1 - x, 1 - y)]

        def copy(k, chip):
            return pltpu.make_async_remote_copy(
                src_ref=gbf_ref.at[2 * chip[0] + chip[1]], dst_ref=land_ref.at[k],
                send_sem=send_sems.at[k], recv_sem=recv_sems.at[k], device_id=(*chip, c), device_id_type=MESH)

        mine = pltpu.make_async_copy(g_ref.at[2 * x + y], own_ref, local_sem)
        mine.start()
        sends = [copy(k, chip) for k, chip in enumerate(chips)]
        for cp in sends:
            cp.start()
        mine.wait()
        for cp in sends:
            cp.wait_recv()

        def add_landed(i, carry):
            r = pl.ds(pl.multiple_of(i * PACK_CHUNK, 16), PACK_CHUNK)
            landed = [land_ref[k, r, :].astype(f32) for k in range(3)]
            part_ref[r, :] = ((own_ref[r, :] + landed[0]) + landed[1]) + landed[2]
            return carry

        lax.fori_loop(0, n_chunks, add_landed, 0)
        swap = pltpu.make_async_remote_copy(
            src_ref=part_ref, dst_ref=sib_ref, send_sem=send_sems.at[3], recv_sem=recv_sems.at[3],
            device_id=(x, y, 1 - c), device_id_type=MESH)
        swap.start()
        swap.wait_recv()

        def add_sibling(i, carry):
            r = pl.ds(pl.multiple_of(i * PACK_CHUNK, 8), PACK_CHUNK)
            out_ref[r, :] = part_ref[r, :] + sib_ref[r, :]
            return carry

        lax.fori_loop(0, n_chunks, add_sibling, 0)
        swap.wait_send()
        for cp in sends:
            cp.wait_send()

    buf = pltpu.VMEM((rows, width), f32)
    return pl.pallas_call(
        body, name="reduce_weight_grads",
        out_shape=jax.ShapeDtypeStruct((rows, width), f32),
        in_specs=[pl.BlockSpec(memory_space=pl.ANY), pl.BlockSpec(memory_space=pl.ANY)],
        out_specs=pl.BlockSpec(memory_space=pltpu.VMEM),
        scratch_shapes=[buf, pltpu.VMEM((3, rows, width), bf16), buf, buf,
                        pltpu.SemaphoreType.DMA((4,)), pltpu.SemaphoreType.DMA((4,)), pltpu.SemaphoreType.DMA],
        compiler_params=pltpu.CompilerParams(vmem_limit_bytes=VMEM_LIMIT),
    )(grads, grads_bf)


def _pack_shards(w_in_s, w_out_s, conv_s):
    lead = w_in_s.shape[:-2]
    conv_rows = jnp.pad(conv_s.reshape(*lead, CONV_K * 384), [(0, 0)] * len(lead) + [(0, 8 * D_MODEL - CONV_K * 384)])
    return jnp.concatenate([
        w_in_s.reshape(*lead, SHARD_IN, D_MODEL), w_out_s, conv_rows.reshape(*lead, 8, D_MODEL),
        jnp.zeros((*lead, PACK_ROWS - SHARD_IN - 256 - 8, D_MODEL), f32)], axis=-2)


CONV_WORDS = 2 * CONV_K * 384


def _pack_weights_bf16(w_in_s, w_out_s, conv_s):
    conv_bits = lax.bitcast_convert_type(conv_s, bf16).reshape(CONV_WORDS)
    return jnp.concatenate([
        w_in_s.astype(bf16).reshape(SHARD_IN, D_MODEL), w_out_s.astype(bf16),
        jnp.pad(conv_bits, (0, 8 * D_MODEL - CONV_WORDS)).reshape(8, D_MODEL),
        jnp.zeros((PACK_ROWS - SHARD_IN - 256 - 8, D_MODEL), bf16)], axis=0)


def _unpack_weights_bf16(shards):
    w_in_s = shards[:, :SHARD_IN].reshape(N_CHIPS, D_MODEL, SHARD_IN)
    w_out = shards[:, SHARD_IN:SHARD_IN + 256].reshape(D_MODEL, D_MODEL)
    conv_bits = shards[:, SHARD_IN + 256:SHARD_IN + 264].reshape(N_CHIPS, 8 * D_MODEL)[:, :CONV_WORDS]
    conv_s = lax.bitcast_convert_type(conv_bits.reshape(N_CHIPS, CONV_K, 384, 2), f32)
    return (jnp.transpose(w_in_s, (1, 0, 2)).reshape(D_MODEL, IN_COLS), w_out,
            jnp.transpose(conv_s, (1, 0, 2)).reshape(CONV_K, QKV_W))


def _unpack_shards(packed):
    lead = packed.shape[:-2]
    w_in_s = packed[..., :SHARD_IN, :].reshape(*lead, D_MODEL, SHARD_IN)
    w_out_s = packed[..., SHARD_IN:SHARD_IN + 256, :]
    conv_s = packed[..., SHARD_IN + 256:SHARD_IN + 264, :].reshape(*lead, 8 * D_MODEL)[..., :CONV_K * 384]
    return w_in_s, w_out_s, conv_s.reshape(*lead, CONV_K, 384)


def _local_step(x, target, pos_col, mod_row, norm_w, w_in, conv_w, a_log, dt_bias, dn_norm_w, at_norm_w, w_out, fin_w):
    shift, scale, gate = mod_row[:, :D_MODEL], mod_row[:, D_MODEL:2 * D_MODEL], mod_row[:, 2 * D_MODEL:]
    half = AT_DIM // 2
    lane = jnp.arange(128)
    inv_freq = ROPE_THETA ** (-jnp.arange(half, dtype=f32) / half)
    invf_row = inv_freq[lane % half].reshape(1, 128)
    sign_row = jnp.where((lane % AT_DIM) < half, -1.0, 1.0).astype(f32).reshape(1, 128)
    ba_w = jnp.pad(w_in[:, 2 * D_MODEL:2 * D_MODEL + 2 * DN_HEADS], ((0, 0), (0, BA_W - 2 * DN_HEADS)))
    w_cat = jnp.concatenate([w_in[:, :2 * D_MODEL], ba_w, w_in[:, 2 * D_MODEL + 2 * DN_HEADS:]], axis=1).astype(bf16)
    conv_w8 = jnp.pad(conv_w, ((0, 8 - CONV_K), (0, 0)))
    a_row = jnp.pad(a_log.reshape(1, DN_HEADS), ((0, 0), (DN_HEADS, BA_W - 2 * DN_HEADS)))
    dt_row = jnp.pad(dt_bias.reshape(1, DN_HEADS), ((0, 0), (DN_HEADS, BA_W - 2 * DN_HEADS)))
    dn_w_row = jnp.tile(dn_norm_w.reshape(1, DN_DIM), (1, DN_HEADS))
    at_w_row = jnp.tile(at_norm_w.reshape(1, AT_DIM), (1, AT_HEADS))
    norm_row = norm_w.reshape(1, D_MODEL)
    fin_row = fin_w.reshape(1, D_MODEL)
    w_out_bf = w_out.astype(bf16)

    pdn, pat, z_at, h = _pre_proj(x, pos_col, invf_row, sign_row, norm_row, scale, shift, w_cat)
    *dn_parts, dn_invs = _dn_intra_forward(pdn, conv_w8, a_row, dt_row)
    o_dn, states = _dn_scan_forward(*dn_parts)
    views = [_to_view(pat, d) for d in PATTERN_DILATIONS]
    outs, lses = [], []
    for d, view in zip(PATTERN_DILATIONS, views):
        o, lse = _attn_forward(view, d)
        outs.append(_from_view(o, d))
        lses.append(_from_view(lse, d))
    (dx2, d_odn, d_zdn, d_oat, delta, lse_all, d_zat, g_wtop, g_wbot, tail_rows) = _tail(
        x, target, o_dn, pdn, outs, lses, z_at, gate, w_out_bf, dn_w_row, at_w_row, fin_row)
    d_q, d_k, d_v = [], [], []
    for d, view in zip(PATTERN_DILATIONS, views):
        args = (_to_view(d_oat, d), _to_view(delta, d), _to_view(lse_all, d))
        d_q.append(_from_view(_attn_backward_q(view, *args, d), d))
        dk, dv = _attn_backward_kv(view, *args, d)
        d_k.append(_from_view(dk, d))
        d_v.append(_from_view(dv, d))
    d_parts = _dn_scan_backward(*dn_parts, states, d_odn)
    d_conv, d_ba, dn_rows = _dn_intra_backward(pdn, conv_w8, a_row, dt_row, dn_invs, *d_parts)
    grad_x, d_proj, pre_rows, conv_rows = _pre_backward(
        x, dx2, pos_col, invf_row, sign_row, norm_row, scale, shift, w_cat, conv_w8, pdn, d_conv, d_zdn, d_ba,
        d_q, d_k, d_v, d_zat)
    g_cat = _weight_grad(h, d_proj)
    g_w_in = jnp.concatenate([g_cat[:, :2 * D_MODEL], g_cat[:, 2 * D_MODEL:2 * D_MODEL + 2 * DN_HEADS],
                              g_cat[:, 2 * D_MODEL + BA_W:]], axis=1)
    g_w_out = jnp.concatenate([g_wtop, g_wbot], axis=0)
    g_conv = conv_rows[:CONV_K]
    small = jnp.concatenate([
        pre_rows[0:1], tail_rows[2:3], tail_rows[3:4, :DN_WIDTH], tail_rows[4:5, :AT_WIDTH],
        dn_rows[0:1], dn_rows[1:2], tail_rows[0:1, :128],
        pre_rows[2:3], pre_rows[1:2], tail_rows[1:2]], axis=1)
    return grad_x, g_w_in, g_w_out, g_conv, small


def kernel(x, c, positions, w_mod, b_mod, norm_w, w_in, conv_w, a_log, dt_bias, dn_norm_w, at_norm_w, w_out, final_norm_w, loss_target, m_w_mod, m_b_mod, m_norm_w, m_w_in, m_conv_w, m_a_log, m_dt_bias, m_dn_norm_w, m_at_norm_w, m_w_out, m_final_norm_w, v_w_mod, v_b_mod, v_norm_w, v_w_in, v_conv_w, v_a_log, v_dt_bias, v_dn_norm_w, v_at_norm_w, v_w_out, v_final_norm_w):
    seq = x.shape[1]
    ax, ay, ac = lax.axis_index("x"), lax.axis_index("y"), lax.axis_index("c")
    me = 4 * ax + 2 * ay + ac
    chip = 2 * ax + ay

    c_all = _all_gather_rows(jnp.pad(c, ((0, 7), (0, 0))), "gather_c").reshape(N_DEV, 8, D_MODEL)[:, 0]
    b_mod_s = lax.dynamic_slice_in_dim(b_mod, chip * 768, 768, axis=1)
    mod_part = _mod_shard(c_all, w_mod[0], b_mod_s)
    mod_all = _all_gather_rows(mod_part, "gather_mod").reshape(N_CHIPS, 2, N_DEV, 768)[:, 0]
    mod_row = lax.dynamic_index_in_dim(mod_all, me, axis=1, keepdims=False).reshape(1, 3 * D_MODEL)

    shards = _gather_weight_shards(_pack_weights_bf16(w_in[0], w_out[0], conv_w[0]))
    w_in_full, w_out_full, conv_full = _unpack_weights_bf16(shards)

    grad_x, g_w_in, g_w_out, g_conv, small = _local_step(
        x[0], loss_target[0], positions.reshape(seq, 1), mod_row, norm_w, w_in_full, conv_full, a_log, dt_bias,
        dn_norm_w, at_norm_w, w_out_full, final_norm_w)

    g_pack = _pack_shards(jnp.transpose(g_w_in.reshape(D_MODEL, N_CHIPS, SHARD_IN), (1, 0, 2)),
                          g_w_out.reshape(N_CHIPS, 256, D_MODEL),
                          jnp.transpose(g_conv.reshape(CONV_K, N_CHIPS, 384), (1, 0, 2)))
    g_mine = _reduce_weight_grads(g_pack, g_pack.astype(bf16))
    d_pack, m_pack, v_pack = _adamw_rows(_pack_shards(w_in[0], w_out[0], conv_w[0]), g_mine,
                                         _pack_shards(m_w_in[0], m_w_out[0], m_conv_w[0]),
                                         _pack_shards(v_w_in[0], v_w_out[0], v_conv_w[0]), "adamw_packed")
    grad_w_in, grad_w_out, grad_conv_w = _unpack_shards(g_mine)
    delta_w_in, delta_w_out, delta_conv_w = _unpack_shards(d_pack)
    new_m_w_in, new_m_w_out, new_m_conv_w = _unpack_shards(m_pack)
    new_v_w_in, new_v_w_out, new_v_conv_w = _unpack_shards(v_pack)

    gathered = _all_gather_rows(jnp.pad(small, ((0, 7), (0, 0))), "gather_small")

    def small_row(norm, fin, dn, at, a, dt, bmod):
        z = lambda n: jnp.zeros((1, n), f32)
        return jnp.concatenate([norm.reshape(1, -1), fin.reshape(1, -1), dn.reshape(1, -1), at.reshape(1, -1), z(64),
                                z(4), a.reshape(1, -1), z(120), z(4), dt.reshape(1, -1), z(120), bmod.reshape(1, -1)], axis=1)

    g_small, d_small, m_small, v_small, loss_row, d_mod_all = _small_update(
        gathered,
        small_row(norm_w, final_norm_w, dn_norm_w, at_norm_w, a_log, dt_bias, b_mod),
        small_row(m_norm_w, m_final_norm_w, m_dn_norm_w, m_at_norm_w, m_a_log, m_dt_bias, m_b_mod),
        small_row(v_norm_w, v_final_norm_w, v_dn_norm_w, v_at_norm_w, v_a_log, v_dt_bias, v_b_mod))

    def split_small(r):
        return (r[:, RS_BMOD:RS_W], r[:, RS_NORM:RS_FIN], r[:, RS_A + DN_HEADS:RS_A + 2 * DN_HEADS],
                r[:, RS_DT + DN_HEADS:RS_DT + 2 * DN_HEADS], r[:, RS_DN:RS_AT], r[:, RS_AT:RS_AT + AT_DIM],
                r[0, RS_FIN:RS_DN])

    d_mod_s = lax.dynamic_slice_in_dim(d_mod_all, chip * 768, 768, axis=1)
    pad_rows = lambda a: jnp.pad(a, ((0, 128 - N_DEV), (0, 0)))
    grad_w_mod, delta_w_mod, new_m_w_mod, new_v_w_mod = _mod_update(
        pad_rows(c_all), pad_rows(d_mod_s), w_mod[0], m_w_mod[0], v_w_mod[0])

    def ordered(w_mod_leaf, small_row_leaf, w_in_leaf, conv_leaf, w_out_leaf):
        b, n, a, dt, dn, at, fin = split_small(small_row_leaf)
        return [w_mod_leaf[None], b, n, w_in_leaf[None], conv_leaf[None], a, dt, dn, at, w_out_leaf[None], fin]

    loss = loss_row[0, 0]
    return (loss, grad_x[None],
            *ordered(grad_w_mod, g_small, grad_w_in, grad_conv_w, grad_w_out),
            *ordered(delta_w_mod, d_small, delta_w_in, delta_conv_w, delta_w_out),
            *ordered(new_m_w_mod, m_small, new_m_w_in, new_m_conv_w, new_m_w_out),
            *ordered(new_v_w_mod, v_small, new_v_w_in, new_v_conv_w, new_v_w_out))
```

```python
import functools

import jax
import jax.numpy as jnp
from jax import lax
from jax.experimental import pallas as pl
from jax.experimental.pallas import tpu as pltpu

f32 = jnp.float32
bf16 = jnp.bfloat16
HIGHEST = lax.Precision.HIGHEST
MESH = pl.DeviceIdType.MESH

D_MODEL = 1024
DN_HEADS = 4
DN_DIM = 128
DN_WIDTH = 512
AT_HEADS = 8
AT_DIM = 64
AT_WIDTH = 512
CHUNK = 64
Q_BLOCK = 128
CONV_K = 4
EPS = 1e-6
ROPE_THETA = 10000.0
PATTERN_DILATIONS = (1, 4, 16)
NEG = -1e30

QKV_W = 3 * DN_WIDTH
BA_W = 128
PDN_W = QKV_W + DN_WIDTH + BA_W
PAT_W = 3 * AT_WIDTH
CAT_W = PDN_W + PAT_W + AT_WIDTH
IN_COLS = 4104
N_CHIPS = 4
N_DEV = 8
SHARD_IN = IN_COLS // N_CHIPS
PACK_ROWS = 1296
PACK_CHUNK = 48

ADAM_LR = 0.001
ADAM_B1 = 0.9
ADAM_B2 = 0.999
ADAM_EPS = 1e-08
ADAM_WD = 0.01
ADAM_STEP = 10

VMEM_LIMIT = 56 * 1024 * 1024

NN = ((1,), (0,))
NT = ((1,), (1,))
TN = ((0,), (0,))


def _pieces(a, n):
    out = []
    for _ in range(n - 1):
        p = a.astype(bf16)
        out.append(p)
        a = a - p.astype(f32)
    out.append(a.astype(bf16))
    return out


def _dot(a, b, dims, exact):
    raw = lambda p, q: lax.dot_general(p, q, (dims, ((), ())), preferred_element_type=f32)
    if exact == "split":
        (ah, al), (bh, bl) = _pieces(a, 2), _pieces(b, 2)
        return raw(ah, bh) + (raw(ah, bl) + raw(al, bh))
    if exact:
        return lax.dot_general(a, b, (dims, ((), ())), precision=HIGHEST, preferred_element_type=f32)
    return raw(a.astype(bf16), b.astype(bf16))


def _dot_sel(sel, b, dims, sel_side):
    raw = lambda p, q: lax.dot_general(p, q, (dims, ((), ())), preferred_element_type=f32)
    sel = sel.astype(bf16)
    parts = [raw(sel, p) if sel_side == 0 else raw(p, sel) for p in _pieces(b, 3)]
    return (parts[0] + parts[1]) + parts[2]


@jax.custom_vjp
def _sel_left(sel, b):
    return _dot_sel(sel, b, NN, 0)


def _sel_left_fwd(sel, b):
    return _dot_sel(sel, b, NN, 0), sel


def _sel_left_bwd(sel, g):
    return jnp.zeros_like(sel), _dot_sel(sel, g, TN, 0)


_sel_left.defvjp(_sel_left_fwd, _sel_left_bwd)


@jax.custom_vjp
def _sel_right(a, sel):
    return _dot_sel(sel, a, NN, 1)


def _sel_right_fwd(a, sel):
    return _dot_sel(sel, a, NN, 1), sel


def _sel_right_bwd(sel, g):
    return _dot_sel(sel, g, NT, 1), jnp.zeros_like(sel)


_sel_right.defvjp(_sel_right_fwd, _sel_right_bwd)


class _Matmuls:
    def __init__(self, exact, back):
        @jax.custom_vjp
        def nn(a, b):
            return _dot(a, b, NN, exact)

        def nn_fwd(a, b):
            return _dot(a, b, NN, exact), (a, b)

        def nn_bwd(res, g):
            a, b = res
            return _dot(g, b, NT, back), _dot(a, g, TN, back)

        nn.defvjp(nn_fwd, nn_bwd)

        @jax.custom_vjp
        def nt(a, b):
            return _dot(a, b, NT, exact)

        def nt_fwd(a, b):
            return _dot(a, b, NT, exact), (a, b)

        def nt_bwd(res, g):
            a, b = res
            return _dot(g, b, NN, back), _dot(g, a, TN, back)

        nt.defvjp(nt_fwd, nt_bwd)

        @jax.custom_vjp
        def tn(a, b):
            return _dot(a, b, TN, exact)

        def tn_fwd(a, b):
            return _dot(a, b, TN, exact), (a, b)

        def tn_bwd(res, g):
            a, b = res
            return _dot(b, g, NT, back), _dot(a, g, NN, back)

        tn.defvjp(tn_fwd, tn_bwd)
        self.nn, self.nt, self.tn = nn, nt, tn


MM = _Matmuls(exact=False, back=False)
MS = _Matmuls(exact="split", back=False)


def _each(fn, *lists):
    return [fn(*args) for args in zip(*lists)]


def _inverse_products(a_lows):
    ri = lax.broadcasted_iota(jnp.int32, (CHUNK, CHUNK), 0)
    ci = lax.broadcasted_iota(jnp.int32, (CHUNK, CHUNK), 1)
    eye = (ri == ci).astype(f32)
    power = _each(lambda a: -a, a_lows)
    inv = _each(lambda p: eye + p, power)
    for _ in range(5):
        power = _each(lambda p: _dot(p, p, NN, "split"), power)
        inv = _each(lambda x, p: x + _dot(x, p, NN, "split"), inv, power)
    return inv


def _inverse_cotangents(invs, gs):
    left = _each(lambda t, g: _dot(t, g, TN, "split"), invs, gs)
    return _each(lambda l, t: -_dot(l, t, NT, "split"), left, invs)


@jax.custom_vjp
def _unit_lower_inverses(a_lows):
    return _inverse_products(a_lows)


def _unit_lower_inverses_fwd(a_lows):
    invs = _inverse_products(a_lows)
    return invs, invs


def _unit_lower_inverses_bwd(invs, gs):
    return (_inverse_cotangents(invs, gs),)


_unit_lower_inverses.defvjp(_unit_lower_inverses_fwd, _unit_lower_inverses_bwd)


@jax.custom_vjp
def _known_inverses(a_lows, invs):
    return invs


def _known_inverses_fwd(a_lows, invs):
    return invs, invs


def _known_inverses_bwd(invs, gs):
    return _inverse_cotangents(invs, gs), _each(jnp.zeros_like, invs)


_known_inverses.defvjp(_known_inverses_fwd, _known_inverses_bwd)


def _params(semantics=None):
    return pltpu.CompilerParams(dimension_semantics=semantics, vmem_limit_bytes=VMEM_LIMIT)


def _silu(x):
    return x * jax.nn.sigmoid(x)


def _group_ones(width, group):
    r = lax.broadcasted_iota(jnp.int32, (width, width), 0) // group
    c = lax.broadcasted_iota(jnp.int32, (width, width), 1) // group
    return (r == c).astype(f32)


def _hnorm(x, nw, scale, shift):
    xn = x * lax.rsqrt(jnp.mean(x * x, axis=-1, keepdims=True) + EPS)
    return xn * nw * (1.0 + scale) + shift


def _rope_tables(pos_col, invf_row, sign_row):
    ang = pos_col.astype(f32) * invf_row
    cos_t = jnp.cos(ang)
    sin_t = jnp.sin(ang) * sign_row
    return jnp.concatenate([cos_t] * 4, axis=1), jnp.concatenate([sin_t] * 4, axis=1)


def _rope_partner(x):
    lane = lax.broadcasted_iota(jnp.int32, x.shape, 1)
    width = x.shape[1]
    return jnp.where((lane % AT_DIM) < AT_DIM // 2, pltpu.roll(x, width - AT_DIM // 2, 1), pltpu.roll(x, AT_DIM // 2, 1))


TOKEN_TILE = 256
LANES = 128


def _stage_lanes(stage_ref, first, value):
    for j in range(value.shape[1] // LANES):
        stage_ref[first + j] = value[:, LANES * j:LANES * (j + 1)]


def _stage_to_view(stage_ref, view_ref, dil):
    chunks, rows, _ = stage_ref.shape
    for r in range(dil):
        for j in range(chunks):
            col = (r * chunks + j) * LANES
            view_ref[:, col:col + LANES] = stage_ref.at[j][pl.ds(r, rows // dil, stride=dil), :]


def _view_to_value(stage_ref, view_ref, dil):
    chunks, rows, _ = stage_ref.shape
    for r in range(dil):
        for j in range(chunks):
            col = (r * chunks + j) * LANES
            stage_ref.at[j][pl.ds(r, rows // dil, stride=dil), :] = view_ref[:, col:col + LANES]
    return jnp.concatenate([stage_ref[j] for j in range(chunks)], axis=1)


def _view_spec(width, dil):
    return pl.BlockSpec((TOKEN_TILE // dil, dil * width), lambda i: (i, 0))


def _view_shape(seq, width, dil):
    return jax.ShapeDtypeStruct((seq // dil, dil * width), f32)


def _stage(width):
    return pltpu.VMEM((width // LANES, TOKEN_TILE, LANES), f32)


def _pre_proj(x, pos_col, invf_row, sign_row, norm_w, scale, shift, w_cat):
    seq = x.shape[0]
    tm = TOKEN_TILE

    def body(x_ref, pos_ref, invf_ref, sign_ref, nw_ref, sc_ref, sh_ref, w_ref,
             pdn_ref, pat1_ref, pat4_ref, pat16_ref, zat_ref, h_ref, stage_ref):
        h = _hnorm(x_ref[...], nw_ref[...], sc_ref[...], sh_ref[...]).astype(bf16)
        h_ref[...] = h
        big = jnp.dot(h, w_ref[...], preferred_element_type=f32)
        pdn_ref[...] = big[:, :PDN_W]
        cos_t, sin_t = _rope_tables(pos_ref[...], invf_ref[...], sign_ref[...])
        q = big[:, PDN_W:PDN_W + AT_WIDTH]
        k = big[:, PDN_W + AT_WIDTH:PDN_W + 2 * AT_WIDTH]
        parts = (q * cos_t + _rope_partner(q) * sin_t, k * cos_t + _rope_partner(k) * sin_t,
                 big[:, PDN_W + 2 * AT_WIDTH:PDN_W + PAT_W])
        for p, part in enumerate(parts):
            pat1_ref[:, p * AT_WIDTH:(p + 1) * AT_WIDTH] = part
            _stage_lanes(stage_ref, p * (AT_WIDTH // LANES), part)
        _stage_to_view(stage_ref, pat4_ref, 4)
        _stage_to_view(stage_ref, pat16_ref, 16)
        zat_ref[...] = big[:, PDN_W + PAT_W:]

    row = lambda w: pl.BlockSpec((1, w), lambda i: (0, 0))
    tile = lambda w: pl.BlockSpec((tm, w), lambda i: (i, 0))
    return pl.pallas_call(
        body, name="pre_proj", grid=(seq // tm,),
        in_specs=[tile(D_MODEL), tile(1), row(128), row(128), row(D_MODEL), row(D_MODEL), row(D_MODEL),
                  pl.BlockSpec((D_MODEL, CAT_W), lambda i: (0, 0))],
        out_specs=[tile(PDN_W), tile(PAT_W), _view_spec(PAT_W, 4), _view_spec(PAT_W, 16), tile(AT_WIDTH), tile(D_MODEL)],
        out_shape=[jax.ShapeDtypeStruct((seq, PDN_W), f32), _view_shape(seq, PAT_W, 1), _view_shape(seq, PAT_W, 4),
                   _view_shape(seq, PAT_W, 16), jax.ShapeDtypeStruct((seq, AT_WIDTH), f32),
                   jax.ShapeDtypeStruct((seq, D_MODEL), bf16)],
        scratch_shapes=[_stage(PAT_W)],
        compiler_params=_params(("parallel",)),
    )(x, pos_col, invf_row, sign_row, norm_w, scale, shift, w_cat)


def _conv_taps(ext_ref, halo, cur, w8):
    rows = cur.shape[0]
    ext_ref[0:8, :] = halo
    ext_ref[8:8 + rows, :] = cur
    out = ext_ref[pl.ds(5, rows), :] * w8[0:1, :]
    for j in range(1, CONV_K):
        out = out + ext_ref[pl.ds(5 + j, rows), :] * w8[j:j + 1, :]
    return out


def _dn_pre(cq, ck, cv, ba, a_row, dt_row):
    ones = _group_ones(DN_WIDTH, DN_DIM)
    sq, sk, v = _silu(cq), _silu(ck), _silu(cv)
    qn = sq * lax.rsqrt(_sel_right(sq * sq, ones) + EPS)
    kn = sk * lax.rsqrt(_sel_right(sk * sk, ones) + EPS)
    beta_all = jax.nn.sigmoid(ba)
    g_all = -jnp.exp(a_row) * jax.nn.softplus(ba + dt_row)
    return qn, kn, v, beta_all, g_all


def _dn_intra(qs, ks, vs, betas, gs, known_invs=None):
    ri = lax.broadcasted_iota(jnp.int32, (CHUNK, CHUNK), 0)
    ci = lax.broadcasted_iota(jnp.int32, (CHUNK, CHUNK), 1)
    tril = ri >= ci
    strict = ri > ci
    lower = tril.astype(f32)
    each = _each
    g_wide = each(lambda g: jnp.broadcast_to(g, (CHUNK, DN_DIM)), gs)
    gc = each(lambda g: _sel_left(lower, g), g_wide)
    gc_sq = each(lambda g: _sel_left(lower, jnp.broadcast_to(g, (CHUNK, CHUNK))), gs)
    g_end = each(lambda g: _sel_left(jnp.ones((CHUNK, CHUNK), f32), g), g_wide)
    g_end8 = each(lambda g: _sel_left(jnp.ones((8, CHUNK), f32), g), g_wide)
    decay = each(lambda s: jnp.exp(jnp.where(tril, s - s.T, -jnp.inf)), gc_sq)
    qs = each(lambda q: q * (DN_DIM ** -0.5), qs)
    kb = each(lambda k, b: k * b, ks, betas)
    vb = each(lambda v, b: v * b, vs, betas)
    kk = each(MM.nt, kb, ks)
    qk = each(MM.nt, qs, ks)
    a_low = each(lambda p, d: jnp.where(strict, p * d, 0.0), kk, decay)
    inv = _unit_lower_inverses(a_low) if known_invs is None else _known_inverses(a_low, known_invs)
    e_gc = each(jnp.exp, gc)
    u = each(MS.nn, inv, vb)
    w = each(lambda x, k, e: MS.nn(x, k * e), inv, kb, e_gc)
    attn = each(lambda p, d: jnp.where(tril, p * d, 0.0), qk, decay)
    q_dec = each(lambda q, e: q * e, qs, e_gc)
    k_dec = each(lambda k, ge, c: k * jnp.exp(ge - c), ks, g_end, gc)
    return u, w, q_dec, k_dec, attn, each(jnp.exp, g_end8), inv


def _dn_step(us, ws, q_decs, k_decs, attns, e_ends, states):
    each = _each
    v_new = each(lambda u, w, s: u - MM.nn(w, s), us, ws, states)
    qs = each(MM.nn, q_decs, states)
    o = each(lambda a, b, c: a + MM.nn(b, c), qs, attns, v_new)
    new_states = each(lambda s, e, k, v: s * e + MM.tn(k, v), states, e_ends, k_decs, v_new)
    return o, new_states


INTRA_CHUNKS = 2
SCAN_CHUNKS = 4


def _intra_specs(nc):
    rows = INTRA_CHUNKS * CHUNK
    cur = pl.BlockSpec((rows, QKV_W), lambda i: (i, 0))
    halo = pl.BlockSpec((8, QKV_W), lambda i: (jnp.maximum(i * (rows // 8) - 1, 0), 0))
    ba = pl.BlockSpec((rows, BA_W), lambda i: (i, (QKV_W + DN_WIDTH) // BA_W))
    conv = pl.BlockSpec((8, QKV_W), lambda i: (0, 0))
    row = pl.BlockSpec((1, BA_W), lambda i: (0, 0))
    wide = pl.BlockSpec((rows, DN_WIDTH), lambda i: (i, 0))
    attn = pl.BlockSpec((INTRA_CHUNKS, DN_HEADS, CHUNK, CHUNK), lambda i: (i, 0, 0, 0))
    e_end = pl.BlockSpec((INTRA_CHUNKS, 8 * DN_HEADS, DN_DIM), lambda i: (i, 0, 0))
    return cur, halo, ba, conv, row, wide, attn, e_end


def _intra_items():
    return [(ci, h, slice(ci * CHUNK, (ci + 1) * CHUNK), slice(h * DN_DIM, (h + 1) * DN_DIM))
            for ci in range(INTRA_CHUNKS) for h in range(DN_HEADS)]


def _intra_inputs(items, qn, kn, v, beta_all, g_all):
    return ([qn[rows, lanes] for _, _, rows, lanes in items], [kn[rows, lanes] for _, _, rows, lanes in items],
            [v[rows, lanes] for _, _, rows, lanes in items], [beta_all[rows, h:h + 1] for _, h, rows, _ in items],
            [g_all[rows, DN_HEADS + h:DN_HEADS + h + 1] for _, h, rows, _ in items])


def _intra_shapes(seq):
    nc = seq // CHUNK
    wide = jax.ShapeDtypeStruct((seq, DN_WIDTH), f32)
    return [wide, wide, wide, wide, jax.ShapeDtypeStruct((nc, DN_HEADS, CHUNK, CHUNK), f32),
            jax.ShapeDtypeStruct((nc, 8 * DN_HEADS, DN_DIM), f32)]


def _dn_intra_forward(pdn, conv_w8, a_row, dt_row):
    seq = pdn.shape[0]
    nc = seq // CHUNK
    cur, halo, ba, conv, row, wide, attn, e_end = _intra_specs(nc)

    def body(cur_ref, halo_ref, ba_ref, w_ref, a_ref, dt_ref, u_ref, w_out_ref, qd_ref, kd_ref, attn_ref, e_ref, inv_ref,
             ext_ref):
        halo_rows = jnp.where(pl.program_id(0) > 0, halo_ref[...], 0.0)
        c = _conv_taps(ext_ref, halo_rows, cur_ref[...], w_ref[...])
        qn, kn, v, beta_all, g_all = _dn_pre(c[:, :DN_WIDTH], c[:, DN_WIDTH:2 * DN_WIDTH], c[:, 2 * DN_WIDTH:],
                                             ba_ref[...], a_ref[...], dt_ref[...])
        items = _intra_items()
        u, w, qd, kd, at, e8, inv = _dn_intra(*_intra_inputs(items, qn, kn, v, beta_all, g_all))
        for n, (ci, h, rows, lanes) in enumerate(items):
            u_ref[rows, lanes] = u[n]
            w_out_ref[rows, lanes] = w[n]
            qd_ref[rows, lanes] = qd[n]
            kd_ref[rows, lanes] = kd[n]
            attn_ref[ci, h] = at[n]
            e_ref[ci, 8 * h:8 * h + 8, :] = e8[n]
            inv_ref[ci, h] = inv[n]

    return pl.pallas_call(
        body, name="dn_intra_forward", grid=(nc // INTRA_CHUNKS,),
        in_specs=[cur, halo, ba, conv, row, row],
        out_specs=[wide, wide, wide, wide, attn, e_end, attn],
        out_shape=_intra_shapes(seq) + [jax.ShapeDtypeStruct((nc, DN_HEADS, CHUNK, CHUNK), f32)],
        scratch_shapes=[pltpu.VMEM((INTRA_CHUNKS * CHUNK + 8, QKV_W), f32)],
        compiler_params=_params(("parallel",)),
    )(pdn, pdn, pdn, conv_w8, a_row, dt_row)


def _scan_specs(nc, reverse):
    steps = nc // SCAN_CHUNKS
    at = (lambda i: steps - 1 - i) if reverse else (lambda i: i)
    wide = pl.BlockSpec((SCAN_CHUNKS * CHUNK, DN_WIDTH), lambda i: (at(i), 0))
    attn = pl.BlockSpec((SCAN_CHUNKS, DN_HEADS, CHUNK, CHUNK), lambda i: (at(i), 0, 0, 0))
    e_end = pl.BlockSpec((SCAN_CHUNKS, 8 * DN_HEADS, DN_DIM), lambda i: (at(i), 0, 0))
    states = pl.BlockSpec((SCAN_CHUNKS, DN_HEADS, DN_DIM, DN_DIM), lambda i: (at(i), 0, 0, 0))
    return wide, attn, e_end, states


def _step_inputs(ci, rows, lanes, u_ref, w_ref, qd_ref, kd_ref, attn_ref, e_ref):
    heads = range(DN_HEADS)
    return ([u_ref[rows, lanes[h]] for h in heads], [w_ref[rows, lanes[h]] for h in heads],
            [qd_ref[rows, lanes[h]] for h in heads], [kd_ref[rows, lanes[h]] for h in heads],
            [attn_ref[ci, h] for h in heads], [e_ref[ci, 8 * h:8 * h + 1, :] for h in heads])


def _dn_scan_forward(u, w, q_dec, k_dec, attn, e_end):
    seq = u.shape[0]
    nc = seq // CHUNK
    wide, attn_spec, e_spec, st_spec = _scan_specs(nc, reverse=False)

    def body(u_ref, w_ref, qd_ref, kd_ref, attn_ref, e_ref, o_ref, st_ref, state_ref):
        @pl.when(pl.program_id(0) == 0)
        def _():
            state_ref[...] = jnp.zeros_like(state_ref)

        heads = range(DN_HEADS)
        lanes = [slice(h * DN_DIM, (h + 1) * DN_DIM) for h in heads]
        states = [state_ref[h] for h in heads]
        for ci in range(SCAN_CHUNKS):
            rows = slice(ci * CHUNK, (ci + 1) * CHUNK)
            for h in heads:
                st_ref[ci, h] = states[h]
            o, states = _dn_step(*_step_inputs(ci, rows, lanes, u_ref, w_ref, qd_ref, kd_ref, attn_ref, e_ref), states)
            for h in heads:
                o_ref[rows, lanes[h]] = o[h]
        for h in heads:
            state_ref[h] = states[h]

    return pl.pallas_call(
        body, name="dn_scan_forward", grid=(nc // SCAN_CHUNKS,),
        in_specs=[wide, wide, wide, wide, attn_spec, e_spec],
        out_specs=[wide, st_spec],
        out_shape=[jax.ShapeDtypeStruct((seq, DN_WIDTH), f32), jax.ShapeDtypeStruct((nc, DN_HEADS, DN_DIM, DN_DIM), f32)],
        scratch_shapes=[pltpu.VMEM((DN_HEADS, DN_DIM, DN_DIM), f32)],
        compiler_params=_params(("arbitrary",)),
    )(u, w, q_dec, k_dec, attn, e_end)


def _dn_scan_backward(u, w, q_dec, k_dec, attn, e_end, states, d_o):
    seq = u.shape[0]
    nc = seq // CHUNK
    wide, attn_spec, e_spec, st_spec = _scan_specs(nc, reverse=True)

    def body(u_ref, w_ref, qd_ref, kd_ref, attn_ref, e_ref, st_ref, do_ref,
             du_ref, dw_ref, dqd_ref, dkd_ref, dattn_ref, de_ref, dstate_ref):
        @pl.when(pl.program_id(0) == 0)
        def _():
            dstate_ref[...] = jnp.zeros_like(dstate_ref)

        heads = range(DN_HEADS)
        lanes = [slice(h * DN_DIM, (h + 1) * DN_DIM) for h in heads]
        first_row = lax.broadcasted_iota(jnp.int32, (8, DN_DIM), 0) == 0
        dstates = [dstate_ref[h] for h in heads]
        for ci in reversed(range(SCAN_CHUNKS)):
            rows = slice(ci * CHUNK, (ci + 1) * CHUNK)
            _, step_vjp = jax.vjp(_dn_step, *_step_inputs(ci, rows, lanes, u_ref, w_ref, qd_ref, kd_ref, attn_ref, e_ref),
                                  [st_ref[ci, h] for h in heads])
            du, dw, dqd, dkd, dattn, de, dstates = step_vjp(([do_ref[rows, lanes[h]] for h in heads], dstates))
            for h in heads:
                du_ref[rows, lanes[h]] = du[h]
                dw_ref[rows, lanes[h]] = dw[h]
                dqd_ref[rows, lanes[h]] = dqd[h]
                dkd_ref[rows, lanes[h]] = dkd[h]
                dattn_ref[ci, h] = dattn[h]
                de_ref[ci, 8 * h:8 * h + 8, :] = jnp.where(first_row, jnp.broadcast_to(de[h], (8, DN_DIM)), 0.0)
        for h in heads:
            dstate_ref[h] = dstates[h]

    return pl.pallas_call(
        body, name="dn_scan_backward", grid=(nc // SCAN_CHUNKS,),
        in_specs=[wide, wide, wide, wide, attn_spec, e_spec, st_spec, wide],
        out_specs=[wide, wide, wide, wide, attn_spec, e_spec],
        out_shape=_intra_shapes(seq),
        scratch_shapes=[pltpu.VMEM((DN_HEADS, DN_DIM, DN_DIM), f32)],
        compiler_params=_params(("arbitrary",)),
    )(u, w, q_dec, k_dec, attn, e_end, states, d_o)


def _dn_intra_backward(pdn, conv_w8, a_row, dt_row, invs, d_u, d_w, d_qd, d_kd, d_attn, d_e):
    seq = pdn.shape[0]
    nc = seq // CHUNK
    rows_per_step = INTRA_CHUNKS * CHUNK
    cur, halo, ba, conv, row, wide, attn, e_end = _intra_specs(nc)

    def body(cur_ref, halo_ref, ba_ref, w_ref, a_ref, dt_ref, inv_ref, du_ref, dw_ref, dqd_ref, dkd_ref, dattn_ref, de_ref,
             dconv_ref, dba_ref, drow_ref, ext_ref):
        @pl.when(pl.program_id(0) == 0)
        def _():
            drow_ref[...] = jnp.zeros_like(drow_ref)

        halo_rows = jnp.where(pl.program_id(0) > 0, halo_ref[...], 0.0)
        c = _conv_taps(ext_ref, halo_rows, cur_ref[...], w_ref[...])
        (qn, kn, v, beta_all, g_all), pre_vjp = jax.vjp(
            _dn_pre, c[:, :DN_WIDTH], c[:, DN_WIDTH:2 * DN_WIDTH], c[:, 2 * DN_WIDTH:], ba_ref[...], a_ref[...], dt_ref[...])
        lane = lax.broadcasted_iota(jnp.int32, (CHUNK, BA_W), 1)
        items = _intra_items()
        _, intra_vjp = jax.vjp(_dn_intra, *_intra_inputs(items, qn, kn, v, beta_all, g_all),
                               [inv_ref[ci, h] for ci, h, _, _ in items])
        dq, dk, dv, dbeta, dg, _ = intra_vjp((
            [du_ref[rows, lanes] for _, _, rows, lanes in items], [dw_ref[rows, lanes] for _, _, rows, lanes in items],
            [dqd_ref[rows, lanes] for _, _, rows, lanes in items], [dkd_ref[rows, lanes] for _, _, rows, lanes in items],
            [dattn_ref[ci, h] for ci, h, _, _ in items], [de_ref[ci, 8 * h:8 * h + 8, :] for ci, h, _, _ in items],
            [jnp.zeros((CHUNK, CHUNK), f32) for _ in items]))
        dq_rows, dk_rows, dv_rows, dbeta_rows, dg_rows = [], [], [], [], []
        for ci in range(INTRA_CHUNKS):
            of_chunk = [n for n, item in enumerate(items) if item[0] == ci]
            d_beta_all = jnp.zeros((CHUNK, BA_W), f32)
            d_g_all = jnp.zeros((CHUNK, BA_W), f32)
            for n in of_chunk:
                h = items[n][1]
                d_beta_all = d_beta_all + jnp.where(lane == h, dbeta[n], 0.0)
                d_g_all = d_g_all + jnp.where(lane == DN_HEADS + h, dg[n], 0.0)
            dq_rows.append(jnp.concatenate([dq[n] for n in of_chunk], axis=1))
            dk_rows.append(jnp.concatenate([dk[n] for n in of_chunk], axis=1))
            dv_rows.append(jnp.concatenate([dv[n] for n in of_chunk], axis=1))
            dbeta_rows.append(d_beta_all)
            dg_rows.append(d_g_all)
        stack = lambda parts: jnp.concatenate(parts, axis=0)
        dcq, dck, dcv, dba, da_row, ddt_row = pre_vjp(
            (stack(dq_rows), stack(dk_rows), stack(dv_rows), stack(dbeta_rows), stack(dg_rows)))
        dconv_ref[:, :DN_WIDTH] = dcq
        dconv_ref[:, DN_WIDTH:2 * DN_WIDTH] = dck
        dconv_ref[:, 2 * DN_WIDTH:] = dcv
        dba_ref[...] = dba
        drow_ref[0:1, :] += da_row
        drow_ref[1:2, :] += ddt_row

    return pl.pallas_call(
        body, name="dn_intra_backward", grid=(nc // INTRA_CHUNKS,),
        in_specs=[cur, halo, ba, conv, row, row, attn, wide, wide, wide, wide, attn, e_end],
        out_specs=[pl.BlockSpec((rows_per_step, QKV_W), lambda i: (i, 0)),
                   pl.BlockSpec((rows_per_step, BA_W), lambda i: (i, 0)),
                   pl.BlockSpec((8, BA_W), lambda i: (0, 0))],
        out_shape=[jax.ShapeDtypeStruct((seq, QKV_W), f32), jax.ShapeDtypeStruct((seq, BA_W), f32),
                   jax.ShapeDtypeStruct((8, BA_W), f32)],
        scratch_shapes=[pltpu.VMEM((rows_per_step + 8, QKV_W), f32)],
        compiler_params=_params(("arbitrary",)),
    )(pdn, pdn, pdn, conv_w8, a_row, dt_row, invs, d_u, d_w, d_qd, d_kd, d_attn, d_e)


def _band_masks(block_index):
    qi = lax.broadcasted_iota(jnp.int32, (Q_BLOCK, Q_BLOCK), 0)
    kj = lax.broadcasted_iota(jnp.int32, (Q_BLOCK, Q_BLOCK), 1)
    return (kj >= qi) & (block_index > 0), kj <= qi


def _attn_forward(pat_view, dil):
    length = pat_view.shape[0]
    nb = length // Q_BLOCK
    scale = AT_DIM ** -0.5

    def body(q_ref, kp_ref, kc_ref, vp_ref, vc_ref, o_ref, lse_ref):
        mask_prev, mask_cur = _band_masks(pl.program_id(1))
        heads = [slice(h * AT_DIM, (h + 1) * AT_DIM) for h in range(AT_HEADS)]
        s_prev = [_dot(q_ref[:, lanes], kp_ref[:, lanes], NT, False) for lanes in heads]
        s_cur = [_dot(q_ref[:, lanes], kc_ref[:, lanes], NT, False) for lanes in heads]
        p_prev, p_cur, norm = [], [], []
        for h, lanes in enumerate(heads):
            sp = jnp.where(mask_prev, s_prev[h] * scale, NEG)
            sc = jnp.where(mask_cur, s_cur[h] * scale, NEG)
            m = jnp.maximum(jnp.max(sp, axis=1, keepdims=True), jnp.max(sc, axis=1, keepdims=True))
            pp = jnp.exp(sp - m)
            pc = jnp.exp(sc - m)
            l = jnp.sum(pp, axis=1, keepdims=True) + jnp.sum(pc, axis=1, keepdims=True)
            lse_ref[:, lanes] = jnp.broadcast_to(m + jnp.log(l), (Q_BLOCK, AT_DIM))
            p_prev.append(pp.astype(bf16))
            p_cur.append(pc.astype(bf16))
            norm.append(1.0 / l)
        o_prev = [_dot(p_prev[h], vp_ref[:, lanes], NN, False) for h, lanes in enumerate(heads)]
        o_cur = [_dot(p_cur[h], vc_ref[:, lanes], NN, False) for h, lanes in enumerate(heads)]
        for h, lanes in enumerate(heads):
            o_ref[:, lanes] = (o_prev[h] + o_cur[h]) * norm[h]

    def blk(piece, prev):
        if prev:
            return pl.BlockSpec((Q_BLOCK, AT_WIDTH), lambda r, n: (jnp.maximum(n - 1, 0), 3 * r + piece))
        return pl.BlockSpec((Q_BLOCK, AT_WIDTH), lambda r, n: (n, 3 * r + piece))

    out = pl.BlockSpec((Q_BLOCK, AT_WIDTH), lambda r, n: (n, r))
    return pl.pallas_call(
        body, name=f"attn_forward_d{dil}", grid=(dil, nb),
        in_specs=[blk(0, False), blk(1, True), blk(1, False), blk(2, True), blk(2, False)],
        out_specs=[out, out],
        out_shape=[jax.ShapeDtypeStruct((length, dil * AT_WIDTH), f32)] * 2,
        compiler_params=_params(("parallel", "parallel")),
    )(pat_view, pat_view, pat_view, pat_view, pat_view)


def _attn_backward_q(pat_view, d_out, delta, lse, dil):
    length = pat_view.shape[0]
    nb = length // Q_BLOCK
    scale = AT_DIM ** -0.5

    def body(q_ref, kp_ref, kc_ref, vp_ref, vc_ref, do_ref, dl_ref, lse_ref, dq_ref):
        mask_prev, mask_cur = _band_masks(pl.program_id(1))
        heads = [slice(h * AT_DIM, (h + 1) * AT_DIM) for h in range(AT_HEADS)]
        sides = ((mask_prev, kp_ref, vp_ref), (mask_cur, kc_ref, vc_ref))
        s = [[_dot(q_ref[:, lanes], k_ref[:, lanes], NT, False) for lanes in heads] for _, k_ref, _ in sides]
        dp = [[_dot(do_ref[:, lanes], v_ref[:, lanes], NT, False) for lanes in heads] for _, _, v_ref in sides]
        ds = [[], []]
        for h, lanes in enumerate(heads):
            lse_h = lse_ref[:, h * AT_DIM:h * AT_DIM + 1]
            dl_h = dl_ref[:, h * AT_DIM:h * AT_DIM + 1]
            for side, (mask, _, _) in enumerate(sides):
                p = jnp.exp(jnp.where(mask, s[side][h] * scale - lse_h, NEG))
                ds[side].append((p * (dp[side][h] - dl_h)).astype(bf16))
        dq = [[_dot(ds[side][h], k_ref[:, lanes], NN, False) for h, lanes in enumerate(heads)]
              for side, (_, k_ref, _) in enumerate(sides)]
        for h, lanes in enumerate(heads):
            dq_ref[:, lanes] = (dq[0][h] + dq[1][h]) * scale

    def blk(piece, prev):
        if prev:
            return pl.BlockSpec((Q_BLOCK, AT_WIDTH), lambda r, n: (jnp.maximum(n - 1, 0), 3 * r + piece))
        return pl.BlockSpec((Q_BLOCK, AT_WIDTH), lambda r, n: (n, 3 * r + piece))

    one = pl.BlockSpec((Q_BLOCK, AT_WIDTH), lambda r, n: (n, r))
    return pl.pallas_call(
        body, name=f"attn_backward_q_d{dil}", grid=(dil, nb),
        in_specs=[blk(0, False), blk(1, True), blk(1, False), blk(2, True), blk(2, False), one, one, one],
        out_specs=one,
        out_shape=jax.ShapeDtypeStruct((length, dil * AT_WIDTH), f32),
        compiler_params=_params(("parallel", "parallel")),
    )(pat_view, pat_view, pat_view, pat_view, pat_view, d_out, delta, lse)


def _attn_backward_kv(pat_view, d_out, delta, lse, dil):
    length = pat_view.shape[0]
    nb = length // Q_BLOCK
    scale = AT_DIM ** -0.5

    def body(k_ref, v_ref, qa_ref, qb_ref, doa_ref, dob_ref, dla_ref, dlb_ref, lsea_ref, lseb_ref, dk_ref, dv_ref):
        j = pl.program_id(1)
        qi = lax.broadcasted_iota(jnp.int32, (Q_BLOCK, Q_BLOCK), 0)
        kj = lax.broadcasted_iota(jnp.int32, (Q_BLOCK, Q_BLOCK), 1)
        mask_same = kj <= qi
        mask_next = (kj >= qi) & (j + 1 < nb)
        heads = [slice(h * AT_DIM, (h + 1) * AT_DIM) for h in range(AT_HEADS)]
        sides = ((mask_same, qa_ref, doa_ref, dla_ref, lsea_ref), (mask_next, qb_ref, dob_ref, dlb_ref, lseb_ref))
        s = [[_dot(q_ref[:, lanes], k_ref[:, lanes], NT, False) for lanes in heads] for _, q_ref, _, _, _ in sides]
        dp = [[_dot(do_ref[:, lanes], v_ref[:, lanes], NT, False) for lanes in heads] for _, _, do_ref, _, _ in sides]
        p, ds = [[], []], [[], []]
        for h, lanes in enumerate(heads):
            one = slice(h * AT_DIM, h * AT_DIM + 1)
            for side, (mask, _, _, dl_ref, lse_ref) in enumerate(sides):
                prob = jnp.exp(jnp.where(mask, s[side][h] * scale - lse_ref[:, one], NEG))
                p[side].append(prob.astype(bf16))
                ds[side].append((prob * (dp[side][h] - dl_ref[:, one])).astype(bf16))
        dv = [[_dot(p[side][h], do_ref[:, lanes], TN, False) for h, lanes in enumerate(heads)]
              for side, (_, _, do_ref, _, _) in enumerate(sides)]
        dk = [[_dot(ds[side][h], q_ref[:, lanes], TN, False) for h, lanes in enumerate(heads)]
              for side, (_, q_ref, _, _, _) in enumerate(sides)]
        for h, lanes in enumerate(heads):
            dk_ref[:, lanes] = (dk[0][h] + dk[1][h]) * scale
            dv_ref[:, lanes] = dv[0][h] + dv[1][h]

    def kv(piece):
        return pl.BlockSpec((Q_BLOCK, AT_WIDTH), lambda r, n: (n, 3 * r + piece))

    q_same = pl.BlockSpec((Q_BLOCK, AT_WIDTH), lambda r, n: (n, 3 * r))
    q_next = pl.BlockSpec((Q_BLOCK, AT_WIDTH), lambda r, n: (jnp.minimum(n + 1, nb - 1), 3 * r))
    same = pl.BlockSpec((Q_BLOCK, AT_WIDTH), lambda r, n: (n, r))
    nxt = pl.BlockSpec((Q_BLOCK, AT_WIDTH), lambda r, n: (jnp.minimum(n + 1, nb - 1), r))
    return pl.pallas_call(
        body, name=f"attn_backward_kv_d{dil}", grid=(dil, nb),
        in_specs=[kv(1), kv(2), q_same, q_next, same, nxt, same, nxt, same, nxt],
        out_specs=[same, same],
        out_shape=[jax.ShapeDtypeStruct((length, dil * AT_WIDTH), f32)] * 2,
        compiler_params=_params(("parallel", "parallel")),
    )(pat_view, pat_view, pat_view, pat_view, d_out, d_out, delta, delta, lse, lse)


def _gated_norms(o_dn, z_dn, o_at, z_at, dn_w, at_w):
    ms_dn = _sel_right(o_dn * o_dn, _group_ones(DN_WIDTH, DN_DIM)) * (1.0 / DN_DIM)
    a = o_dn * lax.rsqrt(ms_dn + EPS) * dn_w * _silu(z_dn)
    ms_at = _sel_right(o_at * o_at, _group_ones(AT_WIDTH, AT_DIM)) * (1.0 / AT_DIM)
    b = o_at * lax.rsqrt(ms_at + EPS) * at_w * _silu(z_at)
    return a, b


def _residual_loss(x, mix, gate, fin_w, target):
    x2 = x + gate * mix
    y = x2 * lax.rsqrt(jnp.mean(x2 * x2, axis=-1, keepdims=True) + EPS) * fin_w
    err = y - target
    per_token = jnp.sum(err * err, axis=1, keepdims=True) * (1.0 / D_MODEL)
    return 0.5 * jnp.sum(per_token, axis=0, keepdims=True)


TAIL_ROWS = 8


def _tail(x, target, o_dn, pdn, outs, lses, z_at, gate, w_out, dn_w_row, at_w_row, fin_w):
    seq = x.shape[0]
    tm = TOKEN_TILE

    def body(x_ref, t_ref, odn_ref, zdn_ref, o1_ref, o4_ref, o16_ref, l1_ref, l4_ref, l16_ref, zat_ref, gate_ref,
             wt_ref, wb_ref, dnw_ref, atw_ref, fw_ref,
             dx2_ref, dodn_ref, dzdn_ref, doat1_ref, doat4_ref, doat16_ref, delta1_ref, delta4_ref, delta16_ref,
             lse1_ref, lse4_ref, lse16_ref, dzat_ref, gwt_ref, gwb_ref, rows_ref, *stages):
        @pl.when(pl.program_id(0) == 0)
        def _():
            gwt_ref[...] = jnp.zeros_like(gwt_ref)
            gwb_ref[...] = jnp.zeros_like(gwb_ref)
            rows_ref[...] = jnp.zeros_like(rows_ref)

        l1, l4, l16 = l1_ref[...], _view_to_value(stages[0], l4_ref, 4), _view_to_value(stages[1], l16_ref, 16)
        o4, o16 = _view_to_value(stages[2], o4_ref, 4), _view_to_value(stages[3], o16_ref, 16)
        top = jnp.maximum(jnp.maximum(l1, l4), l16)
        e1, e4, e16 = jnp.exp(l1 - top), jnp.exp(l4 - top), jnp.exp(l16 - top)
        den = e1 + e4 + e16
        o_at = (e1 * o1_ref[...] + e4 * o4 + e16 * o16) / den
        lse = top + jnp.log(den)

        (a, b), norms_vjp = jax.vjp(_gated_norms, odn_ref[...], zdn_ref[...], o_at, zat_ref[...], dnw_ref[...], atw_ref[...])
        a, b = a.astype(bf16), b.astype(bf16)
        mix = jnp.dot(a, wt_ref[...], preferred_element_type=f32) + jnp.dot(b, wb_ref[...], preferred_element_type=f32)
        loss, loss_vjp = jax.vjp(_residual_loss, x_ref[...], mix, gate_ref[...], fw_ref[...], t_ref[...])
        dx2, dmix, dgate, dfw, _ = loss_vjp(jnp.ones((1, 1), f32))
        dmix = dmix.astype(bf16)
        dwt = lax.dot_general(a, dmix, (TN, ((), ())), preferred_element_type=f32)
        dwb = lax.dot_general(b, dmix, (TN, ((), ())), preferred_element_type=f32)
        da = lax.dot_general(dmix, wt_ref[...], (NT, ((), ())), preferred_element_type=f32)
        db = lax.dot_general(dmix, wb_ref[...], (NT, ((), ())), preferred_element_type=f32)
        dodn, dzdn, doat, dzat, ddnw, datw = norms_vjp((da, db))
        dx2_ref[...] = dx2
        dodn_ref[...] = dodn
        dzdn_ref[...] = dzdn
        dzat_ref[...] = dzat
        delta = _sel_right(doat * o_at, _group_ones(AT_WIDTH, AT_DIM))
        for stage_ref, value, refs in ((stages[0], doat, (doat1_ref, doat4_ref, doat16_ref)),
                                       (stages[1], delta, (delta1_ref, delta4_ref, delta16_ref)),
                                       (stages[2], lse, (lse1_ref, lse4_ref, lse16_ref))):
            refs[0][...] = value
            _stage_lanes(stage_ref, 0, value)
            _stage_to_view(stage_ref, refs[1], 4)
            _stage_to_view(stage_ref, refs[2], 16)
        gwt_ref[...] += dwt
        gwb_ref[...] += dwb
        rows_ref[0:1, :] += jnp.broadcast_to(loss, (1, D_MODEL))
        rows_ref[1:2, :] += dgate
        rows_ref[2:3, :] += dfw
        rows_ref[3:4, 0:DN_WIDTH] += ddnw
        rows_ref[4:5, 0:AT_WIDTH] += datw

    tile = lambda w: pl.BlockSpec((tm, w), lambda i: (i, 0))
    row = lambda w: pl.BlockSpec((1, w), lambda i: (0, 0))
    half_w = pl.BlockSpec((DN_WIDTH, D_MODEL), lambda i: (0, 0))
    sds = lambda w: jax.ShapeDtypeStruct((seq, w), f32)
    views = [_view_spec(AT_WIDTH, d) for d in PATTERN_DILATIONS]
    view_shapes = [_view_shape(seq, AT_WIDTH, d) for d in PATTERN_DILATIONS]
    return pl.pallas_call(
        body, name="tail", grid=(seq // tm,),
        in_specs=[tile(D_MODEL), tile(D_MODEL), tile(DN_WIDTH),
                  pl.BlockSpec((tm, DN_WIDTH), lambda i: (i, QKV_W // DN_WIDTH)),
                  *views, *views, tile(AT_WIDTH), row(D_MODEL), half_w, pl.BlockSpec((AT_WIDTH, D_MODEL), lambda i: (1, 0)),
                  row(DN_WIDTH), row(AT_WIDTH), row(D_MODEL)],
        out_specs=[tile(D_MODEL), tile(DN_WIDTH), tile(DN_WIDTH), *views, *views, *views,
                   tile(AT_WIDTH), half_w, half_w, pl.BlockSpec((TAIL_ROWS, D_MODEL), lambda i: (0, 0))],
        out_shape=[sds(D_MODEL), sds(DN_WIDTH), sds(DN_WIDTH), *view_shapes, *view_shapes, *view_shapes,
                   sds(AT_WIDTH), jax.ShapeDtypeStruct((DN_WIDTH, D_MODEL), f32),
                   jax.ShapeDtypeStruct((AT_WIDTH, D_MODEL), f32), jax.ShapeDtypeStruct((TAIL_ROWS, D_MODEL), f32)],
        scratch_shapes=[_stage(AT_WIDTH)] * 4,
        compiler_params=_params(("arbitrary",)),
    )(x, target, o_dn, pdn, outs[0], outs[1], outs[2], lses[0], lses[1], lses[2], z_at, gate, w_out, w_out,
      dn_w_row, at_w_row, fin_w)


PRE_ROWS = 8


def _pre_backward(x, dx2, pos_col, invf_row, sign_row, norm_w, scale, shift, w_cat, conv_w8, pdn, d_conv, d_zdn, d_ba,
                  d_q, d_k, d_v, d_zat):
    seq = x.shape[0]
    tm = TOKEN_TILE
    last = seq // tm - 1

    def body(x_ref, dx2_ref, pos_ref, invf_ref, sign_ref, nw_ref, sc_ref, sh_ref, w_ref, cw_ref,
             pre_ref, prehalo_ref, dc_ref, dchalo_ref, dz_ref, dba_ref,
             dq1_ref, dq4_ref, dq16_ref, dk1_ref, dk4_ref, dk16_ref, dv1_ref, dv4_ref, dv16_ref, dzat_ref,
             gx_ref, dproj_ref, rows_ref, crow_ref, ext_ref, *stages):
        i = pl.program_id(0)

        @pl.when(i == 0)
        def _():
            rows_ref[...] = jnp.zeros_like(rows_ref)
            crow_ref[...] = jnp.zeros_like(crow_ref)

        dc = dc_ref[...]
        ext_ref[0:tm, :] = dc
        ext_ref[tm:tm + 8, :] = jnp.where(i < last, dchalo_ref[...], 0.0)
        w8 = cw_ref[...]
        d_pre = ext_ref[pl.ds(3, tm), :] * w8[0:1, :]
        for j in range(1, CONV_K):
            d_pre = d_pre + ext_ref[pl.ds(3 - j, tm), :] * w8[j:j + 1, :]
        ext_ref[0:8, :] = jnp.where(i > 0, prehalo_ref[...], 0.0)
        ext_ref[8:8 + tm, :] = pre_ref[...]
        for j in range(CONV_K):
            crow_ref[j:j + 1, :] += jnp.sum(dc * ext_ref[pl.ds(5 + j, tm), :], axis=0, keepdims=True)

        cos_t, sin_t = _rope_tables(pos_ref[...], invf_ref[...], sign_ref[...])
        dq = dq1_ref[...] + _view_to_value(stages[0], dq4_ref, 4) + _view_to_value(stages[1], dq16_ref, 16)
        dk = dk1_ref[...] + _view_to_value(stages[2], dk4_ref, 4) + _view_to_value(stages[3], dk16_ref, 16)
        dq = dq * cos_t + _rope_partner(dq * sin_t)
        dk = dk * cos_t + _rope_partner(dk * sin_t)
        dv = dv1_ref[...] + _view_to_value(stages[4], dv4_ref, 4) + _view_to_value(stages[5], dv16_ref, 16)
        d_proj = jnp.concatenate([d_pre, dz_ref[...], dba_ref[...], dq, dk, dv, dzat_ref[...]], axis=1).astype(bf16)
        dproj_ref[...] = d_proj
        dh = lax.dot_general(d_proj, w_ref[...], (NT, ((), ())), preferred_element_type=f32)
        _, vjp = jax.vjp(_hnorm, x_ref[...], nw_ref[...], sc_ref[...], sh_ref[...])
        dx, dnw, dsc, dsh = vjp(dh)
        gx_ref[...] = dx + dx2_ref[...]
        rows_ref[0:1, :] += dnw
        rows_ref[1:2, :] += dsc
        rows_ref[2:3, :] += dsh

    tile = lambda w: pl.BlockSpec((tm, w), lambda i: (i, 0))
    row = lambda w: pl.BlockSpec((1, w), lambda i: (0, 0))
    step8 = tm // 8
    return pl.pallas_call(
        body, name="pre_backward", grid=(seq // tm,),
        in_specs=[tile(D_MODEL), tile(D_MODEL), tile(1), row(128), row(128), row(D_MODEL), row(D_MODEL), row(D_MODEL),
                  pl.BlockSpec((D_MODEL, CAT_W), lambda i: (0, 0)), pl.BlockSpec((8, QKV_W), lambda i: (0, 0)),
                  tile(QKV_W), pl.BlockSpec((8, QKV_W), lambda i: (jnp.maximum(i * step8 - 1, 0), 0)),
                  tile(QKV_W), pl.BlockSpec((8, QKV_W), lambda i: (jnp.minimum((i + 1) * step8, seq // 8 - 1), 0)),
                  tile(DN_WIDTH), tile(BA_W)] + [_view_spec(AT_WIDTH, d) for d in PATTERN_DILATIONS] * 3 + [tile(AT_WIDTH)],
        out_specs=[tile(D_MODEL), tile(CAT_W), pl.BlockSpec((PRE_ROWS, D_MODEL), lambda i: (0, 0)),
                   pl.BlockSpec((8, QKV_W), lambda i: (0, 0))],
        out_shape=[jax.ShapeDtypeStruct((seq, D_MODEL), f32), jax.ShapeDtypeStruct((seq, CAT_W), bf16),
                   jax.ShapeDtypeStruct((PRE_ROWS, D_MODEL), f32), jax.ShapeDtypeStruct((8, QKV_W), f32)],
        scratch_shapes=[pltpu.VMEM((tm + 8, QKV_W), f32)] + [_stage(AT_WIDTH)] * 6,
        compiler_params=_params(("arbitrary",)),
    )(x, dx2, pos_col, invf_row, sign_row, norm_w, scale, shift, w_cat, conv_w8, pdn, pdn, d_conv, d_conv, d_zdn, d_ba,
      d_q[0], d_q[1], d_q[2], d_k[0], d_k[1], d_k[2], d_v[0], d_v[1], d_v[2], d_zat)


def _weight_grad(h, d_proj):
    seq = h.shape[0]
    tk, tn = 1024, 384
    n_k = seq // tk

    def body(h_ref, d_ref, o_ref):
        @pl.when(pl.program_id(1) == 0)
        def _():
            o_ref[...] = jnp.zeros_like(o_ref)

        o_ref[...] += lax.dot_general(h_ref[...], d_ref[...], (TN, ((), ())), preferred_element_type=f32)

    return pl.pallas_call(
        body, name="weight_grad", grid=(CAT_W // tn, n_k),
        in_specs=[pl.BlockSpec((tk, D_MODEL), lambda n, k: (k, 0)), pl.BlockSpec((tk, tn), lambda n, k: (k, n))],
        out_specs=pl.BlockSpec((D_MODEL, tn), lambda n, k: (0, n)),
        out_shape=jax.ShapeDtypeStruct((D_MODEL, CAT_W), f32),
        compiler_params=_params(("parallel", "arbitrary")),
    )(h, d_proj)


def _adamw(w, g, m, v):
    m = ADAM_B1 * m + (1.0 - ADAM_B1) * g
    v = ADAM_B2 * v + (1.0 - ADAM_B2) * (g * g)
    m_hat = m / (1.0 - ADAM_B1 ** ADAM_STEP)
    v_hat = v / (1.0 - ADAM_B2 ** ADAM_STEP)
    delta = -ADAM_LR * (m_hat / (jnp.sqrt(v_hat) + ADAM_EPS) + ADAM_WD * w)
    return delta, m, v


def _mod_shard(c_all, w_mod_s, b_mod_s):
    def body(c_ref, w_ref, b_ref, o_ref):
        o_ref[...] = _dot(_silu(c_ref[...]), w_ref[...], NN, True) + b_ref[...]

    return pl.pallas_call(body, name="mod_shard", out_shape=jax.ShapeDtypeStruct((N_DEV, w_mod_s.shape[1]), f32),
                          compiler_params=_params())(c_all, w_mod_s, b_mod_s)


def _mod_update(c_all, d_mod_s, w, m, v):
    def body(c_ref, d_ref, w_ref, m_ref, v_ref, g_ref, dw_ref, nm_ref, nv_ref):
        g = _dot(_silu(c_ref[...]), d_ref[...], TN, True)
        g_ref[...] = g
        dw_ref[...], nm_ref[...], nv_ref[...] = _adamw(w_ref[...], g, m_ref[...], v_ref[...])

    return pl.pallas_call(body, name="mod_update", out_shape=[jax.ShapeDtypeStruct(w.shape, f32)] * 4,
                          compiler_params=_params())(c_all, d_mod_s, w, m, v)


def _adamw_rows(w, g, m, v, name):
    rows, width = w.shape
    tr = PACK_CHUNK if rows % PACK_CHUNK == 0 else rows

    def body(w_ref, g_ref, m_ref, v_ref, dw_ref, nm_ref, nv_ref):
        dw_ref[...], nm_ref[...], nv_ref[...] = _adamw(w_ref[...], g_ref[...], m_ref[...], v_ref[...])

    spec = pl.BlockSpec((tr, width), lambda i: (i, 0))
    return pl.pallas_call(body, name=name, grid=(rows // tr,), in_specs=[spec] * 4, out_specs=[spec] * 3,
                          out_shape=[jax.ShapeDtypeStruct(w.shape, f32)] * 3,
                          compiler_params=_params(("parallel",)))(w, g, m, v)


SM_NORM, SM_FIN, SM_DN, SM_AT, SM_A, SM_DT, SM_LOSS, SM_MOD = 0, 1024, 2048, 2560, 3072, 3200, 3328, 3456
SM_W = SM_MOD + 3 * D_MODEL
RS_NORM, RS_FIN, RS_DN, RS_AT, RS_A, RS_DT, RS_BMOD = 0, 1024, 2048, 2176, 2304, 2432, 2560
RS_W = RS_BMOD + 3 * D_MODEL


def _small_update(gathered, w, m, v):
    def body(g_ref, w_ref, m_ref, v_ref, grad_ref, dw_ref, nm_ref, nv_ref, loss_ref, dmod_ref):
        total = g_ref[0:1, :]
        for dev in range(1, N_DEV):
            total = total + g_ref[8 * dev:8 * dev + 1, :]
        for dev in range(N_DEV):
            dmod_ref[dev:dev + 1, :] = g_ref[8 * dev:8 * dev + 1, SM_MOD:SM_W]
        dn = total[:, SM_DN:SM_DN + DN_DIM]
        for h in range(1, DN_HEADS):
            dn = dn + total[:, SM_DN + h * DN_DIM:SM_DN + (h + 1) * DN_DIM]
        at = total[:, SM_AT:SM_AT + AT_DIM]
        for h in range(1, AT_HEADS):
            at = at + total[:, SM_AT + h * AT_DIM:SM_AT + (h + 1) * AT_DIM]
        grad_ref[:, RS_NORM:RS_FIN] = total[:, SM_NORM:SM_FIN]
        grad_ref[:, RS_FIN:RS_DN] = total[:, SM_FIN:SM_DN]
        grad_ref[:, RS_DN:RS_AT] = dn
        grad_ref[:, RS_AT:RS_A] = jnp.zeros((1, 128), f32)
        grad_ref[:, RS_AT:RS_AT + AT_DIM] = at
        grad_ref[:, RS_A:RS_DT] = total[:, SM_A:SM_DT]
        grad_ref[:, RS_DT:RS_BMOD] = total[:, SM_DT:SM_LOSS]
        grad_ref[:, RS_BMOD:RS_W] = total[:, SM_MOD:SM_W]
        loss_ref[...] = total[:, SM_LOSS:SM_MOD]
        dw_ref[...], nm_ref[...], nv_ref[...] = _adamw(w_ref[...], grad_ref[...], m_ref[...], v_ref[...])

    row = jax.ShapeDtypeStruct((1, RS_W), f32)
    return pl.pallas_call(
        body, name="small_update",
        out_shape=[row, row, row, row, jax.ShapeDtypeStruct((1, 128), f32), jax.ShapeDtypeStruct((N_DEV, 3 * D_MODEL), f32)],
        compiler_params=_params())(gathered, w, m, v)


def _all_gather_rows(block, name):
    m_per, n = block.shape

    def body(x_ref, out_ref, send_sems, recv_sems, local_sem):
        x, y, c = lax.axis_index("x"), lax.axis_index("y"), lax.axis_index("c")
        me, sibling = (x, y, c), (x, y, 1 - c)
        chips = [(1 - x, y), (x, 1 - y), (1 - x, 1 - y)]

        def rows(px, py, pc):
            return out_ref.at[pl.ds((4 * px + 2 * py + pc) * m_per, m_per), :]

        def copy(k, blk, to, src=None):
            return pltpu.make_async_remote_copy(
                src_ref=rows(*blk) if src is None else src, dst_ref=rows(*blk),
                send_sem=send_sems.at[k], recv_sem=recv_sems.at[k], device_id=to, device_id_type=MESH)

        mine = pltpu.make_async_copy(x_ref, rows(*me), local_sem)
        mine.start()
        first = [copy(0, me, sibling, src=x_ref)]
        first += [copy(1 + j, me, (*chip, c), src=x_ref) for j, chip in enumerate(chips)]
        for cp in first:
            cp.start()
        passed = [copy(4 + j, (*chip, c), sibling) for j, chip in enumerate(chips)]
        for j, chip in enumerate(chips):
            copy(1 + j, (*chip, c), me).wait_recv()
            passed[j].start()
        copy(0, sibling, me).wait_recv()
        for j, chip in enumerate(chips):
            copy(4 + j, (*chip, 1 - c), me).wait_recv()
        for cp in first + passed:
            cp.wait_send()
        mine.wait()

    return pl.pallas_call(
        body, name=name,
        out_shape=jax.ShapeDtypeStruct((N_DEV * m_per, n), block.dtype),
        in_specs=[pl.BlockSpec(memory_space=pltpu.VMEM)],
        out_specs=pl.BlockSpec(memory_space=pltpu.VMEM),
        scratch_shapes=[pltpu.SemaphoreType.DMA((7,)), pltpu.SemaphoreType.DMA((7,)), pltpu.SemaphoreType.DMA],
        compiler_params=pltpu.CompilerParams(vmem_limit_bytes=VMEM_LIMIT),
    )(block)


def _gather_weight_shards(packed):
    rows, width = packed.shape

    def body(src_ref, out_ref, send_sems, recv_sems, local_sem):
        x, y, c = lax.axis_index("x"), lax.axis_index("y"), lax.axis_index("c")
        chips = [(1 - x, y), (x, 1 - y), (1 - x, 1 - y)]

        def copy(k, owner, to):
            slot = out_ref.at[2 * owner[0] + owner[1]]
            return pltpu.make_async_remote_copy(
                src_ref=src_ref, dst_ref=slot, send_sem=send_sems.at[k], recv_sem=recv_sems.at[k],
                device_id=(*to, c), device_id_type=MESH)

        mine = pltpu.make_async_copy(src_ref, out_ref.at[2 * x + y], local_sem)
        mine.start()
        sends = [copy(k, (x, y), chip) for k, chip in enumerate(chips)]
        for cp in sends:
            cp.start()
        for k, chip in enumerate(chips):
            copy(k, chip, (x, y)).wait_recv()
        for cp in sends:
            cp.wait_send()
        mine.wait()

    return pl.pallas_call(
        body, name="gather_weight_shards",
        out_shape=jax.ShapeDtypeStruct((N_CHIPS, rows, width), packed.dtype),
        in_specs=[pl.BlockSpec(memory_space=pl.ANY)],
        out_specs=pl.BlockSpec(memory_space=pl.ANY),
        scratch_shapes=[pltpu.SemaphoreType.DMA((3,)), pltpu.SemaphoreType.DMA((3,)), pltpu.SemaphoreType.DMA],
    )(packed)


def _reduce_weight_grads(grads, grads_bf):
    _, rows, width = grads.shape
    n_chunks = rows // PACK_CHUNK

    def body(g_ref, gbf_ref, out_ref, own_ref, land_ref, part_ref, sib_ref, send_sems, recv_sems, local_sem):
        x, y, c = lax.axis_index("x"), lax.axis_index("y"), lax.axis_index("c")
        chips = [(1 - x, y), (x, 1 - y), (1 - x, 1 - y)]

        def copy(k, chip):
            return pltpu.make_async_remote_copy(
                src_ref=gbf_ref.at[2 * chip[0] + chip[1]], dst_ref=land_ref.at[k],
                send_sem=send_sems.at[k], recv_sem=recv_sems.at[k], device_id=(*chip, c), device_id_type=MESH)

        mine = pltpu.make_async_copy(g_ref.at[2 * x + y], own_ref, local_sem)
        mine.start()
        sends = [copy(k, chip) for k, chip in enumerate(chips)]
        for cp in sends:
            cp.start()
        mine.wait()
        for cp in sends:
            cp.wait_recv()

        def add_landed(i, carry):
            r = pl.ds(pl.multiple_of(i * PACK_CHUNK, 16), PACK_CHUNK)
            landed = [land_ref[k, r, :].astype(f32) for k in range(3)]
            part_ref[r, :] = ((own_ref[r, :] + landed[0]) + landed[1]) + landed[2]
            return carry

        lax.fori_loop(0, n_chunks, add_landed, 0)
        swap = pltpu.make_async_remote_copy(
            src_ref=part_ref, dst_ref=sib_ref, send_sem=send_sems.at[3], recv_sem=recv_sems.at[3],
            device_id=(x, y, 1 - c), device_id_type=MESH)
        swap.start()
        swap.wait_recv()

        def add_sibling(i, carry):
            r = pl.ds(pl.multiple_of(i * PACK_CHUNK, 8), PACK_CHUNK)
            out_ref[r, :] = part_ref[r, :] + sib_ref[r, :]
            return carry

        lax.fori_loop(0, n_chunks, add_sibling, 0)
        swap.wait_send()
        for cp in sends:
            cp.wait_send()

    buf = pltpu.VMEM((rows, width), f32)
    return pl.pallas_call(
        body, name="reduce_weight_grads",
        out_shape=jax.ShapeDtypeStruct((rows, width), f32),
        in_specs=[pl.BlockSpec(memory_space=pl.ANY), pl.BlockSpec(memory_space=pl.ANY)],
        out_specs=pl.BlockSpec(memory_space=pltpu.VMEM),
        scratch_shapes=[buf, pltpu.VMEM((3, rows, width), bf16), buf, buf,
                        pltpu.SemaphoreType.DMA((4,)), pltpu.SemaphoreType.DMA((4,)), pltpu.SemaphoreType.DMA],
        compiler_params=pltpu.CompilerParams(vmem_limit_bytes=VMEM_LIMIT),
    )(grads, grads_bf)


def _pack_shards(w_in_s, w_out_s, conv_s):
    lead = w_in_s.shape[:-2]
    conv_rows = jnp.pad(conv_s.reshape(*lead, CONV_K * 384), [(0, 0)] * len(lead) + [(0, 8 * D_MODEL - CONV_K * 384)])
    return jnp.concatenate([
        w_in_s.reshape(*lead, SHARD_IN, D_MODEL), w_out_s, conv_rows.reshape(*lead, 8, D_MODEL),
        jnp.zeros((*lead, PACK_ROWS - SHARD_IN - 256 - 8, D_MODEL), f32)], axis=-2)


CONV_WORDS = 2 * CONV_K * 384


def _pack_weights_bf16(w_in_s, w_out_s, conv_s):
    conv_bits = lax.bitcast_convert_type(conv_s, bf16).reshape(CONV_WORDS)
    return jnp.concatenate([
        w_in_s.astype(bf16).reshape(SHARD_IN, D_MODEL), w_out_s.astype(bf16),
        jnp.pad(conv_bits, (0, 8 * D_MODEL - CONV_WORDS)).reshape(8, D_MODEL),
        jnp.zeros((PACK_ROWS - SHARD_IN - 256 - 8, D_MODEL), bf16)], axis=0)


def _unpack_weights_bf16(shards):
    w_in_s = shards[:, :SHARD_IN].reshape(N_CHIPS, D_MODEL, SHARD_IN)
    w_out = shards[:, SHARD_IN:SHARD_IN + 256].reshape(D_MODEL, D_MODEL)
    conv_bits = shards[:, SHARD_IN + 256:SHARD_IN + 264].reshape(N_CHIPS, 8 * D_MODEL)[:, :CONV_WORDS]
    conv_s = lax.bitcast_convert_type(conv_bits.reshape(N_CHIPS, CONV_K, 384, 2), f32)
    return (jnp.transpose(w_in_s, (1, 0, 2)).reshape(D_MODEL, IN_COLS), w_out,
            jnp.transpose(conv_s, (1, 0, 2)).reshape(CONV_K, QKV_W))


def _unpack_shards(packed):
    lead = packed.shape[:-2]
    w_in_s = packed[..., :SHARD_IN, :].reshape(*lead, D_MODEL, SHARD_IN)
    w_out_s = packed[..., SHARD_IN:SHARD_IN + 256, :]
    conv_s = packed[..., SHARD_IN + 256:SHARD_IN + 264, :].reshape(*lead, 8 * D_MODEL)[..., :CONV_K * 384]
    return w_in_s, w_out_s, conv_s.reshape(*lead, CONV_K, 384)


def _local_step(x, target, pos_col, mod_row, norm_w, w_in, conv_w, a_log, dt_bias, dn_norm_w, at_norm_w, w_out, fin_w):
    shift, scale, gate = mod_row[:, :D_MODEL], mod_row[:, D_MODEL:2 * D_MODEL], mod_row[:, 2 * D_MODEL:]
    half = AT_DIM // 2
    lane = jnp.arange(128)
    inv_freq = ROPE_THETA ** (-jnp.arange(half, dtype=f32) / half)
    invf_row = inv_freq[lane % half].reshape(1, 128)
    sign_row = jnp.where((lane % AT_DIM) < half, -1.0, 1.0).astype(f32).reshape(1, 128)
    ba_w = jnp.pad(w_in[:, 2 * D_MODEL:2 * D_MODEL + 2 * DN_HEADS], ((0, 0), (0, BA_W - 2 * DN_HEADS)))
    w_cat = jnp.concatenate([w_in[:, :2 * D_MODEL], ba_w, w_in[:, 2 * D_MODEL + 2 * DN_HEADS:]], axis=1).astype(bf16)
    conv_w8 = jnp.pad(conv_w, ((0, 8 - CONV_K), (0, 0)))
    a_row = jnp.pad(a_log.reshape(1, DN_HEADS), ((0, 0), (DN_HEADS, BA_W - 2 * DN_HEADS)))
    dt_row = jnp.pad(dt_bias.reshape(1, DN_HEADS), ((0, 0), (DN_HEADS, BA_W - 2 * DN_HEADS)))
    dn_w_row = jnp.tile(dn_norm_w.reshape(1, DN_DIM), (1, DN_HEADS))
    at_w_row = jnp.tile(at_norm_w.reshape(1, AT_DIM), (1, AT_HEADS))
    norm_row = norm_w.reshape(1, D_MODEL)
    fin_row = fin_w.reshape(1, D_MODEL)
    w_out_bf = w_out.astype(bf16)

    pdn, *views, z_at, h = _pre_proj(x, pos_col, invf_row, sign_row, norm_row, scale, shift, w_cat)
    *dn_parts, dn_invs = _dn_intra_forward(pdn, conv_w8, a_row, dt_row)
    o_dn, states = _dn_scan_forward(*dn_parts)
    outs, lses = [], []
    for d, view in zip(PATTERN_DILATIONS, views):
        o, lse = _attn_forward(view, d)
        outs.append(o)
        lses.append(lse)
    (dx2, d_odn, d_zdn, *in_views, d_zat, g_wtop, g_wbot, tail_rows) = _tail(
        x, target, o_dn, pdn, outs, lses, z_at, gate, w_out_bf, dn_w_row, at_w_row, fin_row)
    d_oat, delta, lse_all = in_views[0:3], in_views[3:6], in_views[6:9]
    d_q, d_k, d_v = [], [], []
    for n, (d, view) in enumerate(zip(PATTERN_DILATIONS, views)):
        args = (d_oat[n], delta[n], lse_all[n])
        d_q.append(_attn_backward_q(view, *args, d))
        dk, dv = _attn_backward_kv(view, *args, d)
        d_k.append(dk)
        d_v.append(dv)
    d_parts = _dn_scan_backward(*dn_parts, states, d_odn)
    d_conv, d_ba, dn_rows = _dn_intra_backward(pdn, conv_w8, a_row, dt_row, dn_invs, *d_parts)
    grad_x, d_proj, pre_rows, conv_rows = _pre_backward(
        x, dx2, pos_col, invf_row, sign_row, norm_row, scale, shift, w_cat, conv_w8, pdn, d_conv, d_zdn, d_ba,
        d_q, d_k, d_v, d_zat)
    g_cat = _weight_grad(h, d_proj)
    g_w_in = jnp.concatenate([g_cat[:, :2 * D_MODEL], g_cat[:, 2 * D_MODEL:2 * D_MODEL + 2 * DN_HEADS],
                              g_cat[:, 2 * D_MODEL + BA_W:]], axis=1)
    g_w_out = jnp.concatenate([g_wtop, g_wbot], axis=0)
    g_conv = conv_rows[:CONV_K]
    small = jnp.concatenate([
        pre_rows[0:1], tail_rows[2:3], tail_rows[3:4, :DN_WIDTH], tail_rows[4:5, :AT_WIDTH],
        dn_rows[0:1], dn_rows[1:2], tail_rows[0:1, :128],
        pre_rows[2:3], pre_rows[1:2], tail_rows[1:2]], axis=1)
    return grad_x, g_w_in, g_w_out, g_conv, small


def kernel(x, c, positions, w_mod, b_mod, norm_w, w_in, conv_w, a_log, dt_bias, dn_norm_w, at_norm_w, w_out, final_norm_w, loss_target, m_w_mod, m_b_mod, m_norm_w, m_w_in, m_conv_w, m_a_log, m_dt_bias, m_dn_norm_w, m_at_norm_w, m_w_out, m_final_norm_w, v_w_mod, v_b_mod, v_norm_w, v_w_in, v_conv_w, v_a_log, v_dt_bias, v_dn_norm_w, v_at_norm_w, v_w_out, v_final_norm_w):
    seq = x.shape[1]
    ax, ay, ac = lax.axis_index("x"), lax.axis_index("y"), lax.axis_index("c")
    me = 4 * ax + 2 * ay + ac
    chip = 2 * ax + ay

    c_all = _all_gather_rows(jnp.pad(c, ((0, 7), (0, 0))), "gather_c").reshape(N_DEV, 8, D_MODEL)[:, 0]
    b_mod_s = lax.dynamic_slice_in_dim(b_mod, chip * 768, 768, axis=1)
    mod_part = _mod_shard(c_all, w_mod[0], b_mod_s)
    mod_all = _all_gather_rows(mod_part, "gather_mod").reshape(N_CHIPS, 2, N_DEV, 768)[:, 0]
    mod_row = lax.dynamic_index_in_dim(mod_all, me, axis=1, keepdims=False).reshape(1, 3 * D_MODEL)

    shards = _gather_weight_shards(_pack_weights_bf16(w_in[0], w_out[0], conv_w[0]))
    w_in_full, w_out_full, conv_full = _unpack_weights_bf16(shards)

    grad_x, g_w_in, g_w_out, g_conv, small = _local_step(
        x[0], loss_target[0], positions.reshape(seq, 1), mod_row, norm_w, w_in_full, conv_full, a_log, dt_bias,
        dn_norm_w, at_norm_w, w_out_full, final_norm_w)

    g_pack = _pack_shards(jnp.transpose(g_w_in.reshape(D_MODEL, N_CHIPS, SHARD_IN), (1, 0, 2)),
                          g_w_out.reshape(N_CHIPS, 256, D_MODEL),
                          jnp.transpose(g_conv.reshape(CONV_K, N_CHIPS, 384), (1, 0, 2)))
    g_mine = _reduce_weight_grads(g_pack, g_pack.astype(bf16))
    d_pack, m_pack, v_pack = _adamw_rows(_pack_shards(w_in[0], w_out[0], conv_w[0]), g_mine,
                                         _pack_shards(m_w_in[0], m_w_out[0], m_conv_w[0]),
                                         _pack_shards(v_w_in[0], v_w_out[0], v_conv_w[0]), "adamw_packed")
    grad_w_in, grad_w_out, grad_conv_w = _unpack_shards(g_mine)
    delta_w_in, delta_w_out, delta_conv_w = _unpack_shards(d_pack)
    new_m_w_in, new_m_w_out, new_m_conv_w = _unpack_shards(m_pack)
    new_v_w_in, new_v_w_out, new_v_conv_w = _unpack_shards(v_pack)

    gathered = _all_gather_rows(jnp.pad(small, ((0, 7), (0, 0))), "gather_small")

    def small_row(norm, fin, dn, at, a, dt, bmod):
        z = lambda n: jnp.zeros((1, n), f32)
        return jnp.concatenate([norm.reshape(1, -1), fin.reshape(1, -1), dn.reshape(1, -1), at.reshape(1, -1), z(64),
                                z(4), a.reshape(1, -1), z(120), z(4), dt.reshape(1, -1), z(120), bmod.reshape(1, -1)], axis=1)

    g_small, d_small, m_small, v_small, loss_row, d_mod_all = _small_update(
        gathered,
        small_row(norm_w, final_norm_w, dn_norm_w, at_norm_w, a_log, dt_bias, b_mod),
        small_row(m_norm_w, m_final_norm_w, m_dn_norm_w, m_at_norm_w, m_a_log, m_dt_bias, m_b_mod),
        small_row(v_norm_w, v_final_norm_w, v_dn_norm_w, v_at_norm_w, v_a_log, v_dt_bias, v_b_mod))

    def split_small(r):
        return (r[:, RS_BMOD:RS_W], r[:, RS_NORM:RS_FIN], r[:, RS_A + DN_HEADS:RS_A + 2 * DN_HEADS],
                r[:, RS_DT + DN_HEADS:RS_DT + 2 * DN_HEADS], r[:, RS_DN:RS_AT], r[:, RS_AT:RS_AT + AT_DIM],
                r[0, RS_FIN:RS_DN])

    d_mod_s = lax.dynamic_slice_in_dim(d_mod_all, chip * 768, 768, axis=1)
    pad_rows = lambda a: jnp.pad(a, ((0, 128 - N_DEV), (0, 0)))
    grad_w_mod, delta_w_mod, new_m_w_mod, new_v_w_mod = _mod_update(
        pad_rows(c_all), pad_rows(d_mod_s), w_mod[0], m_w_mod[0], v_w_mod[0])

    def ordered(w_mod_leaf, small_row_leaf, w_in_leaf, conv_leaf, w_out_leaf):
        b, n, a, dt, dn, at, fin = split_small(small_row_leaf)
        return [w_mod_leaf[None], b, n, w_in_leaf[None], conv_leaf[None], a, dt, dn, at, w_out_leaf[None], fin]

    loss = loss_row[0, 0]
    return (loss, grad_x[None],
            *ordered(grad_w_mod, g_small, grad_w_in, grad_conv_w, grad_w_out),
            *ordered(delta_w_mod, d_small, delta_w_in, delta_conv_w, delta_w_out),
            *ordered(new_m_w_mod, m_small, new_m_w_in, new_m_conv_w, new_m_w_out),
            *ordered(new_v_w_mod, v_small, new_v_w_in, new_v_conv_w, new_v_w_out))
```

```python
import functools

import jax
import jax.numpy as jnp
from jax import lax
from jax.experimental import pallas as pl
from jax.experimental.pallas import tpu as pltpu

f32 = jnp.float32
bf16 = jnp.bfloat16
HIGHEST = lax.Precision.HIGHEST
MESH = pl.DeviceIdType.MESH

D_MODEL = 1024
DN_HEADS = 4
DN_DIM = 128
DN_WIDTH = 512
AT_HEADS = 8
AT_DIM = 64
AT_WIDTH = 512
CHUNK = 64
Q_BLOCK = 128
CONV_K = 4
EPS = 1e-6
ROPE_THETA = 10000.0
PATTERN_DILATIONS = (1, 4, 16)
NEG = -1e30

QKV_W = 3 * DN_WIDTH
BA_W = 128
PDN_W = QKV_W + DN_WIDTH + BA_W
PAT_W = 3 * AT_WIDTH
CAT_W = PDN_W + PAT_W + AT_WIDTH
IN_COLS = 4104
N_CHIPS = 4
N_DEV = 8
SHARD_IN = IN_COLS // N_CHIPS
PACK_ROWS = 1296
PACK_CHUNK = 48

ADAM_LR = 0.001
ADAM_B1 = 0.9
ADAM_B2 = 0.999
ADAM_EPS = 1e-08
ADAM_WD = 0.01
ADAM_STEP = 10

VMEM_LIMIT = 56 * 1024 * 1024

NN = ((1,), (0,))
NT = ((1,), (1,))
TN = ((0,), (0,))


def _pieces(a, n):
    out = []
    for _ in range(n - 1):
        p = a.astype(bf16)
        out.append(p)
        a = a - p.astype(f32)
    out.append(a.astype(bf16))
    return out


def _dot(a, b, dims, exact):
    raw = lambda p, q: lax.dot_general(p, q, (dims, ((), ())), preferred_element_type=f32)
    if exact == "split":
        (ah, al), (bh, bl) = _pieces(a, 2), _pieces(b, 2)
        return raw(ah, bh) + (raw(ah, bl) + raw(al, bh))
    if exact:
        return lax.dot_general(a, b, (dims, ((), ())), precision=HIGHEST, preferred_element_type=f32)
    return raw(a.astype(bf16), b.astype(bf16))


def _dot_sel(sel, b, dims, sel_side):
    raw = lambda p, q: lax.dot_general(p, q, (dims, ((), ())), preferred_element_type=f32)
    sel = sel.astype(bf16)
    parts = [raw(sel, p) if sel_side == 0 else raw(p, sel) for p in _pieces(b, 3)]
    return (parts[0] + parts[1]) + parts[2]


@jax.custom_vjp
def _sel_left(sel, b):
    return _dot_sel(sel, b, NN, 0)


def _sel_left_fwd(sel, b):
    return _dot_sel(sel, b, NN, 0), sel


def _sel_left_bwd(sel, g):
    return jnp.zeros_like(sel), _dot_sel(sel, g, TN, 0)


_sel_left.defvjp(_sel_left_fwd, _sel_left_bwd)


@jax.custom_vjp
def _sel_right(a, sel):
    return _dot_sel(sel, a, NN, 1)


def _sel_right_fwd(a, sel):
    return _dot_sel(sel, a, NN, 1), sel


def _sel_right_bwd(sel, g):
    return _dot_sel(sel, g, NT, 1), jnp.zeros_like(sel)


_sel_right.defvjp(_sel_right_fwd, _sel_right_bwd)


class _Matmuls:
    def __init__(self, exact, back):
        @jax.custom_vjp
        def nn(a, b):
            return _dot(a, b, NN, exact)

        def nn_fwd(a, b):
            return _dot(a, b, NN, exact), (a, b)

        def nn_bwd(res, g):
            a, b = res
            return _dot(g, b, NT, back), _dot(a, g, TN, back)

        nn.defvjp(nn_fwd, nn_bwd)

        @jax.custom_vjp
        def nt(a, b):
            return _dot(a, b, NT, exact)

        def nt_fwd(a, b):
            return _dot(a, b, NT, exact), (a, b)

        def nt_bwd(res, g):
            a, b = res
            return _dot(g, b, NN, back), _dot(g, a, TN, back)

        nt.defvjp(nt_fwd, nt_bwd)

        @jax.custom_vjp
        def tn(a, b):
            return _dot(a, b, TN, exact)

        def tn_fwd(a, b):
            return _dot(a, b, TN, exact), (a, b)

        def tn_bwd(res, g):
            a, b = res
            return _dot(b, g, NT, back), _dot(a, g, NN, back)

        tn.defvjp(tn_fwd, tn_bwd)
        self.nn, self.nt, self.tn = nn, nt, tn


MM = _Matmuls(exact=False, back=False)
MS = _Matmuls(exact="split", back=False)


def _each(fn, *lists):
    return [fn(*args) for args in zip(*lists)]


def _inverse_products(a_lows):
    ri = lax.broadcasted_iota(jnp.int32, (CHUNK, CHUNK), 0)
    ci = lax.broadcasted_iota(jnp.int32, (CHUNK, CHUNK), 1)
    eye = (ri == ci).astype(f32)
    power = _each(lambda a: -a, a_lows)
    inv = _each(lambda p: eye + p, power)
    for _ in range(5):
        power = _each(lambda p: _dot(p, p, NN, "split"), power)
        inv = _each(lambda x, p: x + _dot(x, p, NN, "split"), inv, power)
    return inv


def _inverse_cotangents(invs, gs):
    left = _each(lambda t, g: _dot(t, g, TN, "split"), invs, gs)
    return _each(lambda l, t: -_dot(l, t, NT, "split"), left, invs)


@jax.custom_vjp
def _unit_lower_inverses(a_lows):
    return _inverse_products(a_lows)


def _unit_lower_inverses_fwd(a_lows):
    invs = _inverse_products(a_lows)
    return invs, invs


def _unit_lower_inverses_bwd(invs, gs):
    return (_inverse_cotangents(invs, gs),)


_unit_lower_inverses.defvjp(_unit_lower_inverses_fwd, _unit_lower_inverses_bwd)


@jax.custom_vjp
def _known_inverses(a_lows, invs):
    return invs


def _known_inverses_fwd(a_lows, invs):
    return invs, invs


def _known_inverses_bwd(invs, gs):
    return _inverse_cotangents(invs, gs), _each(jnp.zeros_like, invs)


_known_inverses.defvjp(_known_inverses_fwd, _known_inverses_bwd)


def _params(semantics=None):
    return pltpu.CompilerParams(dimension_semantics=semantics, vmem_limit_bytes=VMEM_LIMIT)


def _silu(x):
    return x * jax.nn.sigmoid(x)


def _group_ones(width, group):
    r = lax.broadcasted_iota(jnp.int32, (width, width), 0) // group
    c = lax.broadcasted_iota(jnp.int32, (width, width), 1) // group
    return (r == c).astype(f32)


def _hnorm(x, nw, scale, shift):
    xn = x * lax.rsqrt(jnp.mean(x * x, axis=-1, keepdims=True) + EPS)
    return xn * nw * (1.0 + scale) + shift


def _rope_tables(pos_col, invf_row, sign_row):
    ang = pos_col.astype(f32) * invf_row
    cos_t = jnp.cos(ang)
    sin_t = jnp.sin(ang) * sign_row
    return jnp.concatenate([cos_t] * 4, axis=1), jnp.concatenate([sin_t] * 4, axis=1)


def _rope_partner(x):
    lane = lax.broadcasted_iota(jnp.int32, x.shape, 1)
    width = x.shape[1]
    return jnp.where((lane % AT_DIM) < AT_DIM // 2, pltpu.roll(x, width - AT_DIM // 2, 1), pltpu.roll(x, AT_DIM // 2, 1))


TOKEN_TILE = 256
LANES = 128


def _stage_lanes(stage_ref, first, value):
    for j in range(value.shape[1] // LANES):
        stage_ref[first + j] = value[:, LANES * j:LANES * (j + 1)]


def _stage_to_view(stage_ref, view_ref, dil):
    chunks, rows, _ = stage_ref.shape
    for r in range(dil):
        for j in range(chunks):
            col = (r * chunks + j) * LANES
            view_ref[:, col:col + LANES] = stage_ref.at[j][pl.ds(r, rows // dil, stride=dil), :]


def _view_to_value(stage_ref, view_ref, dil):
    chunks, rows, _ = stage_ref.shape
    for r in range(dil):
        for j in range(chunks):
            col = (r * chunks + j) * LANES
            stage_ref.at[j][pl.ds(r, rows // dil, stride=dil), :] = view_ref[:, col:col + LANES]
    return jnp.concatenate([stage_ref[j] for j in range(chunks)], axis=1)


def _view_spec(width, dil):
    return pl.BlockSpec((TOKEN_TILE // dil, dil * width), lambda i: (i, 0))


def _view_shape(seq, width, dil):
    return jax.ShapeDtypeStruct((seq // dil, dil * width), f32)


def _stage(width):
    return pltpu.VMEM((width // LANES, TOKEN_TILE, LANES), f32)


def _pre_proj(x, pos_col, invf_row, sign_row, norm_w, scale, shift, w_cat):
    seq = x.shape[0]
    tm = TOKEN_TILE

    def body(x_ref, pos_ref, invf_ref, sign_ref, nw_ref, sc_ref, sh_ref, w_ref,
             pdn_ref, pat1_ref, pat4_ref, pat16_ref, zat_ref, h_ref, stage_ref):
        h = _hnorm(x_ref[...], nw_ref[...], sc_ref[...], sh_ref[...]).astype(bf16)
        h_ref[...] = h
        big = jnp.dot(h, w_ref[...], preferred_element_type=f32)
        pdn_ref[...] = big[:, :PDN_W]
        cos_t, sin_t = _rope_tables(pos_ref[...], invf_ref[...], sign_ref[...])
        q = big[:, PDN_W:PDN_W + AT_WIDTH]
        k = big[:, PDN_W + AT_WIDTH:PDN_W + 2 * AT_WIDTH]
        parts = (q * cos_t + _rope_partner(q) * sin_t, k * cos_t + _rope_partner(k) * sin_t,
                 big[:, PDN_W + 2 * AT_WIDTH:PDN_W + PAT_W])
        for p, part in enumerate(parts):
            pat1_ref[:, p * AT_WIDTH:(p + 1) * AT_WIDTH] = part
            _stage_lanes(stage_ref, p * (AT_WIDTH // LANES), part)
        _stage_to_view(stage_ref, pat4_ref, 4)
        _stage_to_view(stage_ref, pat16_ref, 16)
        zat_ref[...] = big[:, PDN_W + PAT_W:]

    row = lambda w: pl.BlockSpec((1, w), lambda i: (0, 0))
    tile = lambda w: pl.BlockSpec((tm, w), lambda i: (i, 0))
    return pl.pallas_call(
        body, name="pre_proj", grid=(seq // tm,),
        in_specs=[tile(D_MODEL), tile(1), row(128), row(128), row(D_MODEL), row(D_MODEL), row(D_MODEL),
                  pl.BlockSpec((D_MODEL, CAT_W), lambda i: (0, 0))],
        out_specs=[tile(PDN_W), tile(PAT_W), _view_spec(PAT_W, 4), _view_spec(PAT_W, 16), tile(AT_WIDTH), tile(D_MODEL)],
        out_shape=[jax.ShapeDtypeStruct((seq, PDN_W), f32), _view_shape(seq, PAT_W, 1), _view_shape(seq, PAT_W, 4),
                   _view_shape(seq, PAT_W, 16), jax.ShapeDtypeStruct((seq, AT_WIDTH), f32),
                   jax.ShapeDtypeStruct((seq, D_MODEL), bf16)],
        scratch_shapes=[_stage(PAT_W)],
        compiler_params=_params(("parallel",)),
    )(x, pos_col, invf_row, sign_row, norm_w, scale, shift, w_cat)


def _conv_taps(ext_ref, halo, cur, w8):
    rows = cur.shape[0]
    ext_ref[0:8, :] = halo
    ext_ref[8:8 + rows, :] = cur
    out = ext_ref[pl.ds(5, rows), :] * w8[0:1, :]
    for j in range(1, CONV_K):
        out = out + ext_ref[pl.ds(5 + j, rows), :] * w8[j:j + 1, :]
    return out


def _dn_pre(cq, ck, cv, ba, a_row, dt_row):
    ones = _group_ones(DN_WIDTH, DN_DIM)
    sq, sk, v = _silu(cq), _silu(ck), _silu(cv)
    qn = sq * lax.rsqrt(_sel_right(sq * sq, ones) + EPS)
    kn = sk * lax.rsqrt(_sel_right(sk * sk, ones) + EPS)
    beta_all = jax.nn.sigmoid(ba)
    g_all = -jnp.exp(a_row) * jax.nn.softplus(ba + dt_row)
    return qn, kn, v, beta_all, g_all


def _dn_intra(qs, ks, vs, betas, gs, known_invs=None):
    ri = lax.broadcasted_iota(jnp.int32, (CHUNK, CHUNK), 0)
    ci = lax.broadcasted_iota(jnp.int32, (CHUNK, CHUNK), 1)
    tril = ri >= ci
    strict = ri > ci
    lower = tril.astype(f32)
    each = _each
    g_wide = each(lambda g: jnp.broadcast_to(g, (CHUNK, DN_DIM)), gs)
    gc = each(lambda g: _sel_left(lower, g), g_wide)
    gc_sq = each(lambda g: _sel_left(lower, jnp.broadcast_to(g, (CHUNK, CHUNK))), gs)
    g_end = each(lambda g: _sel_left(jnp.ones((CHUNK, CHUNK), f32), g), g_wide)
    g_end8 = each(lambda g: _sel_left(jnp.ones((8, CHUNK), f32), g), g_wide)
    decay = each(lambda s: jnp.exp(jnp.where(tril, s - s.T, -jnp.inf)), gc_sq)
    qs = each(lambda q: q * (DN_DIM ** -0.5), qs)
    kb = each(lambda k, b: k * b, ks, betas)
    vb = each(lambda v, b: v * b, vs, betas)
    kk = each(MM.nt, kb, ks)
    qk = each(MM.nt, qs, ks)
    a_low = each(lambda p, d: jnp.where(strict, p * d, 0.0), kk, decay)
    inv = _unit_lower_inverses(a_low) if known_invs is None else _known_inverses(a_low, known_invs)
    e_gc = each(jnp.exp, gc)
    u = each(MS.nn, inv, vb)
    w = each(lambda x, k, e: MS.nn(x, k * e), inv, kb, e_gc)
    attn = each(lambda p, d: jnp.where(tril, p * d, 0.0), qk, decay)
    q_dec = each(lambda q, e: q * e, qs, e_gc)
    k_dec = each(lambda k, ge, c: k * jnp.exp(ge - c), ks, g_end, gc)
    return u, w, q_dec, k_dec, attn, each(jnp.exp, g_end8), inv


def _dn_step(us, ws, q_decs, k_decs, attns, e_ends, states):
    each = _each
    v_new = each(lambda u, w, s: u - MM.nn(w, s), us, ws, states)
    qs = each(MM.nn, q_decs, states)
    o = each(lambda a, b, c: a + MM.nn(b, c), qs, attns, v_new)
    new_states = each(lambda s, e, k, v: s * e + MM.tn(k, v), states, e_ends, k_decs, v_new)
    return o, new_states


INTRA_CHUNKS = 2
SCAN_CHUNKS = 4


def _intra_specs(nc):
    rows = INTRA_CHUNKS * CHUNK
    cur = pl.BlockSpec((rows, QKV_W), lambda i: (i, 0))
    halo = pl.BlockSpec((8, QKV_W), lambda i: (jnp.maximum(i * (rows // 8) - 1, 0), 0))
    ba = pl.BlockSpec((rows, BA_W), lambda i: (i, (QKV_W + DN_WIDTH) // BA_W))
    conv = pl.BlockSpec((8, QKV_W), lambda i: (0, 0))
    row = pl.BlockSpec((1, BA_W), lambda i: (0, 0))
    wide = pl.BlockSpec((rows, DN_WIDTH), lambda i: (i, 0))
    attn = pl.BlockSpec((INTRA_CHUNKS, DN_HEADS, CHUNK, CHUNK), lambda i: (i, 0, 0, 0))
    e_end = pl.BlockSpec((INTRA_CHUNKS, 8 * DN_HEADS, DN_DIM), lambda i: (i, 0, 0))
    return cur, halo, ba, conv, row, wide, attn, e_end


def _intra_items():
    return [(ci, h, slice(ci * CHUNK, (ci + 1) * CHUNK), slice(h * DN_DIM, (h + 1) * DN_DIM))
            for ci in range(INTRA_CHUNKS) for h in range(DN_HEADS)]


def _intra_inputs(items, qn, kn, v, beta_all, g_all):
    return ([qn[rows, lanes] for _, _, rows, lanes in items], [kn[rows, lanes] for _, _, rows, lanes in items],
            [v[rows, lanes] for _, _, rows, lanes in items], [beta_all[rows, h:h + 1] for _, h, rows, _ in items],
            [g_all[rows, DN_HEADS + h:DN_HEADS + h + 1] for _, h, rows, _ in items])


def _intra_shapes(seq):
    nc = seq // CHUNK
    wide = jax.ShapeDtypeStruct((seq, DN_WIDTH), f32)
    return [wide, wide, wide, wide, jax.ShapeDtypeStruct((nc, DN_HEADS, CHUNK, CHUNK), f32),
            jax.ShapeDtypeStruct((nc, 8 * DN_HEADS, DN_DIM), f32)]


def _dn_intra_forward(pdn, conv_w8, a_row, dt_row):
    seq = pdn.shape[0]
    nc = seq // CHUNK
    cur, halo, ba, conv, row, wide, attn, e_end = _intra_specs(nc)

    def body(cur_ref, halo_ref, ba_ref, w_ref, a_ref, dt_ref, u_ref, w_out_ref, qd_ref, kd_ref, attn_ref, e_ref, inv_ref,
             ext_ref):
        halo_rows = jnp.where(pl.program_id(0) > 0, halo_ref[...], 0.0)
        c = _conv_taps(ext_ref, halo_rows, cur_ref[...], w_ref[...])
        qn, kn, v, beta_all, g_all = _dn_pre(c[:, :DN_WIDTH], c[:, DN_WIDTH:2 * DN_WIDTH], c[:, 2 * DN_WIDTH:],
                                             ba_ref[...], a_ref[...], dt_ref[...])
        items = _intra_items()
        u, w, qd, kd, at, e8, inv = _dn_intra(*_intra_inputs(items, qn, kn, v, beta_all, g_all))
        for n, (ci, h, rows, lanes) in enumerate(items):
            u_ref[rows, lanes] = u[n]
            w_out_ref[rows, lanes] = w[n]
            qd_ref[rows, lanes] = qd[n]
            kd_ref[rows, lanes] = kd[n]
            attn_ref[ci, h] = at[n]
            e_ref[ci, 8 * h:8 * h + 8, :] = e8[n]
            inv_ref[ci, h] = inv[n]

    return pl.pallas_call(
        body, name="dn_intra_forward", grid=(nc // INTRA_CHUNKS,),
        in_specs=[cur, halo, ba, conv, row, row],
        out_specs=[wide, wide, wide, wide, attn, e_end, attn],
        out_shape=_intra_shapes(seq) + [jax.ShapeDtypeStruct((nc, DN_HEADS, CHUNK, CHUNK), f32)],
        scratch_shapes=[pltpu.VMEM((INTRA_CHUNKS * CHUNK + 8, QKV_W), f32)],
        compiler_params=_params(("parallel",)),
    )(pdn, pdn, pdn, conv_w8, a_row, dt_row)


def _scan_specs(nc, reverse):
    steps = nc // SCAN_CHUNKS
    at = (lambda i: steps - 1 - i) if reverse else (lambda i: i)
    wide = pl.BlockSpec((SCAN_CHUNKS * CHUNK, DN_WIDTH), lambda i: (at(i), 0))
    attn = pl.BlockSpec((SCAN_CHUNKS, DN_HEADS, CHUNK, CHUNK), lambda i: (at(i), 0, 0, 0))
    e_end = pl.BlockSpec((SCAN_CHUNKS, 8 * DN_HEADS, DN_DIM), lambda i: (at(i), 0, 0))
    states = pl.BlockSpec((SCAN_CHUNKS, DN_HEADS, DN_DIM, DN_DIM), lambda i: (at(i), 0, 0, 0))
    return wide, attn, e_end, states


def _step_inputs(ci, rows, lanes, u_ref, w_ref, qd_ref, kd_ref, attn_ref, e_ref):
    heads = range(DN_HEADS)
    return ([u_ref[rows, lanes[h]] for h in heads], [w_ref[rows, lanes[h]] for h in heads],
            [qd_ref[rows, lanes[h]] for h in heads], [kd_ref[rows, lanes[h]] for h in heads],
            [attn_ref[ci, h] for h in heads], [e_ref[ci, 8 * h:8 * h + 1, :] for h in heads])


def _dn_scan_forward(u, w, q_dec, k_dec, attn, e_end):
    seq = u.shape[0]
    nc = seq // CHUNK
    wide, attn_spec, e_spec, st_spec = _scan_specs(nc, reverse=False)

    def body(u_ref, w_ref, qd_ref, kd_ref, attn_ref, e_ref, o_ref, st_ref, state_ref):
        @pl.when(pl.program_id(0) == 0)
        def _():
            state_ref[...] = jnp.zeros_like(state_ref)

        heads = range(DN_HEADS)
        lanes = [slice(h * DN_DIM, (h + 1) * DN_DIM) for h in heads]
        states = [state_ref[h] for h in heads]
        for ci in range(SCAN_CHUNKS):
            rows = slice(ci * CHUNK, (ci + 1) * CHUNK)
            for h in heads:
                st_ref[ci, h] = states[h]
            o, states = _dn_step(*_step_inputs(ci, rows, lanes, u_ref, w_ref, qd_ref, kd_ref, attn_ref, e_ref), states)
            for h in heads:
                o_ref[rows, lanes[h]] = o[h]
        for h in heads:
            state_ref[h] = states[h]

    return pl.pallas_call(
        body, name="dn_scan_forward", grid=(nc // SCAN_CHUNKS,),
        in_specs=[wide, wide, wide, wide, attn_spec, e_spec],
        out_specs=[wide, st_spec],
        out_shape=[jax.ShapeDtypeStruct((seq, DN_WIDTH), f32), jax.ShapeDtypeStruct((nc, DN_HEADS, DN_DIM, DN_DIM), f32)],
        scratch_shapes=[pltpu.VMEM((DN_HEADS, DN_DIM, DN_DIM), f32)],
        compiler_params=_params(("arbitrary",)),
    )(u, w, q_dec, k_dec, attn, e_end)


def _dn_scan_backward(u, w, q_dec, k_dec, attn, e_end, states, d_o):
    seq = u.shape[0]
    nc = seq // CHUNK
    wide, attn_spec, e_spec, st_spec = _scan_specs(nc, reverse=True)

    def body(u_ref, w_ref, qd_ref, kd_ref, attn_ref, e_ref, st_ref, do_ref,
             du_ref, dw_ref, dqd_ref, dkd_ref, dattn_ref, de_ref, dstate_ref):
        @pl.when(pl.program_id(0) == 0)
        def _():
            dstate_ref[...] = jnp.zeros_like(dstate_ref)

        heads = range(DN_HEADS)
        lanes = [slice(h * DN_DIM, (h + 1) * DN_DIM) for h in heads]
        first_row = lax.broadcasted_iota(jnp.int32, (8, DN_DIM), 0) == 0
        dstates = [dstate_ref[h] for h in heads]
        for ci in reversed(range(SCAN_CHUNKS)):
            rows = slice(ci * CHUNK, (ci + 1) * CHUNK)
            _, step_vjp = jax.vjp(_dn_step, *_step_inputs(ci, rows, lanes, u_ref, w_ref, qd_ref, kd_ref, attn_ref, e_ref),
                                  [st_ref[ci, h] for h in heads])
            du, dw, dqd, dkd, dattn, de, dstates = step_vjp(([do_ref[rows, lanes[h]] for h in heads], dstates))
            for h in heads:
                du_ref[rows, lanes[h]] = du[h]
                dw_ref[rows, lanes[h]] = dw[h]
                dqd_ref[rows, lanes[h]] = dqd[h]
                dkd_ref[rows, lanes[h]] = dkd[h]
                dattn_ref[ci, h] = dattn[h]
                de_ref[ci, 8 * h:8 * h + 8, :] = jnp.where(first_row, jnp.broadcast_to(de[h], (8, DN_DIM)), 0.0)
        for h in heads:
            dstate_ref[h] = dstates[h]

    return pl.pallas_call(
        body, name="dn_scan_backward", grid=(nc // SCAN_CHUNKS,),
        in_specs=[wide, wide, wide, wide, attn_spec, e_spec, st_spec, wide],
        out_specs=[wide, wide, wide, wide, attn_spec, e_spec],
        out_shape=_intra_shapes(seq),
        scratch_shapes=[pltpu.VMEM((DN_HEADS, DN_DIM, DN_DIM), f32)],
        compiler_params=_params(("arbitrary",)),
    )(u, w, q_dec, k_dec, attn, e_end, states, d_o)


def _dn_intra_backward(pdn, conv_w8, a_row, dt_row, invs, d_u, d_w, d_qd, d_kd, d_attn, d_e):
    seq = pdn.shape[0]
    nc = seq // CHUNK
    rows_per_step = INTRA_CHUNKS * CHUNK
    cur, halo, ba, conv, row, wide, attn, e_end = _intra_specs(nc)

    def body(cur_ref, halo_ref, ba_ref, w_ref, a_ref, dt_ref, inv_ref, du_ref, dw_ref, dqd_ref, dkd_ref, dattn_ref, de_ref,
             dconv_ref, dba_ref, drow_ref, ext_ref):
        @pl.when(pl.program_id(0) == 0)
        def _():
            drow_ref[...] = jnp.zeros_like(drow_ref)

        halo_rows = jnp.where(pl.program_id(0) > 0, halo_ref[...], 0.0)
        c = _conv_taps(ext_ref, halo_rows, cur_ref[...], w_ref[...])
        (qn, kn, v, beta_all, g_all), pre_vjp = jax.vjp(
            _dn_pre, c[:, :DN_WIDTH], c[:, DN_WIDTH:2 * DN_WIDTH], c[:, 2 * DN_WIDTH:], ba_ref[...], a_ref[...], dt_ref[...])
        lane = lax.broadcasted_iota(jnp.int32, (CHUNK, BA_W), 1)
        items = _intra_items()
        _, intra_vjp = jax.vjp(_dn_intra, *_intra_inputs(items, qn, kn, v, beta_all, g_all),
                               [inv_ref[ci, h] for ci, h, _, _ in items])
        dq, dk, dv, dbeta, dg, _ = intra_vjp((
            [du_ref[rows, lanes] for _, _, rows, lanes in items], [dw_ref[rows, lanes] for _, _, rows, lanes in items],
            [dqd_ref[rows, lanes] for _, _, rows, lanes in items], [dkd_ref[rows, lanes] for _, _, rows, lanes in items],
            [dattn_ref[ci, h] for ci, h, _, _ in items], [de_ref[ci, 8 * h:8 * h + 8, :] for ci, h, _, _ in items],
            [jnp.zeros((CHUNK, CHUNK), f32) for _ in items]))
        dq_rows, dk_rows, dv_rows, dbeta_rows, dg_rows = [], [], [], [], []
        for ci in range(INTRA_CHUNKS):
            of_chunk = [n for n, item in enumerate(items) if item[0] == ci]
            d_beta_all = jnp.zeros((CHUNK, BA_W), f32)
            d_g_all = jnp.zeros((CHUNK, BA_W), f32)
            for n in of_chunk:
                h = items[n][1]
                d_beta_all = d_beta_all + jnp.where(lane == h, dbeta[n], 0.0)
                d_g_all = d_g_all + jnp.where(lane == DN_HEADS + h, dg[n], 0.0)
            dq_rows.append(jnp.concatenate([dq[n] for n in of_chunk], axis=1))
            dk_rows.append(jnp.concatenate([dk[n] for n in of_chunk], axis=1))
            dv_rows.append(jnp.concatenate([dv[n] for n in of_chunk], axis=1))
            dbeta_rows.append(d_beta_all)
            dg_rows.append(d_g_all)
        stack = lambda parts: jnp.concatenate(parts, axis=0)
        dcq, dck, dcv, dba, da_row, ddt_row = pre_vjp(
            (stack(dq_rows), stack(dk_rows), stack(dv_rows), stack(dbeta_rows), stack(dg_rows)))
        dconv_ref[:, :DN_WIDTH] = dcq
        dconv_ref[:, DN_WIDTH:2 * DN_WIDTH] = dck
        dconv_ref[:, 2 * DN_WIDTH:] = dcv
        dba_ref[...] = dba
        drow_ref[0:1, :] += da_row
        drow_ref[1:2, :] += ddt_row

    return pl.pallas_call(
        body, name="dn_intra_backward", grid=(nc // INTRA_CHUNKS,),
        in_specs=[cur, halo, ba, conv, row, row, attn, wide, wide, wide, wide, attn, e_end],
        out_specs=[pl.BlockSpec((rows_per_step, QKV_W), lambda i: (i, 0)),
                   pl.BlockSpec((rows_per_step, BA_W), lambda i: (i, 0)),
                   pl.BlockSpec((8, BA_W), lambda i: (0, 0))],
        out_shape=[jax.ShapeDtypeStruct((seq, QKV_W), f32), jax.ShapeDtypeStruct((seq, BA_W), f32),
                   jax.ShapeDtypeStruct((8, BA_W), f32)],
        scratch_shapes=[pltpu.VMEM((rows_per_step + 8, QKV_W), f32)],
        compiler_params=_params(("arbitrary",)),
    )(pdn, pdn, pdn, conv_w8, a_row, dt_row, invs, d_u, d_w, d_qd, d_kd, d_attn, d_e)


def _band_masks(block_index):
    qi = lax.broadcasted_iota(jnp.int32, (Q_BLOCK, Q_BLOCK), 0)
    kj = lax.broadcasted_iota(jnp.int32, (Q_BLOCK, Q_BLOCK), 1)
    return (kj >= qi) & (block_index > 0), kj <= qi


def _low_half():
    return lax.broadcasted_iota(jnp.int32, (Q_BLOCK, LANES), 1) < AT_DIM


def _head_pairs(ref, split):
    low = _low_half()
    slabs = [ref[:, pair * LANES:(pair + 1) * LANES].astype(bf16) for pair in range(AT_HEADS // 2)]
    if not split:
        return slabs
    zero = jnp.zeros((Q_BLOCK, LANES), bf16)
    return [jnp.where(low, slab, zero) if h == 0 else jnp.where(low, zero, slab) for slab in slabs for h in range(2)]


def _stack_blocks(first, second):
    return [jnp.concatenate([a, b], axis=0) for a, b in zip(first, second)]


def _attn_forward(pat_view, dil):
    length = pat_view.shape[0]
    nb = length // Q_BLOCK
    scale = AT_DIM ** -0.5

    def body(q_ref, kp_ref, kc_ref, vp_ref, vc_ref, o_ref, lse_ref):
        mask = jnp.concatenate(_band_masks(pl.program_id(1)), axis=1)
        low = _low_half()
        heads = range(AT_HEADS)
        q = _head_pairs(q_ref, split=True)
        k = _stack_blocks(_head_pairs(kp_ref, split=False), _head_pairs(kc_ref, split=False))
        v = _stack_blocks(_head_pairs(vp_ref, split=False), _head_pairs(vc_ref, split=False))
        s = [_dot(q[h], k[h // 2], NT, False) for h in heads]
        p, top = [], []
        for h in heads:
            masked = jnp.where(mask, s[h] * scale, NEG)
            m = jnp.max(masked, axis=1, keepdims=True)
            p.append(jnp.exp(masked - m).astype(bf16))
            top.append(m)
        ones = jnp.ones((2 * Q_BLOCK, LANES), bf16)
        l = [_dot(p[h], ones, NN, False) for h in heads]
        o = [_dot(p[h], v[h // 2], NN, False) for h in heads]
        for pair in range(AT_HEADS // 2):
            even, odd = 2 * pair, 2 * pair + 1
            slab = slice(pair * LANES, (pair + 1) * LANES)
            l_pair = jnp.where(low, l[even], l[odd])
            lse_ref[:, slab] = jnp.where(low, top[even], top[odd]) + jnp.log(l_pair)
            o_ref[:, slab] = jnp.where(low, o[even], o[odd]) / l_pair

    def blk(piece, prev):
        if prev:
            return pl.BlockSpec((Q_BLOCK, AT_WIDTH), lambda r, n: (jnp.maximum(n - 1, 0), 3 * r + piece))
        return pl.BlockSpec((Q_BLOCK, AT_WIDTH), lambda r, n: (n, 3 * r + piece))

    out = pl.BlockSpec((Q_BLOCK, AT_WIDTH), lambda r, n: (n, r))
    return pl.pallas_call(
        body, name=f"attn_forward_d{dil}", grid=(dil, nb),
        in_specs=[blk(0, False), blk(1, True), blk(1, False), blk(2, True), blk(2, False)],
        out_specs=[out, out],
        out_shape=[jax.ShapeDtypeStruct((length, dil * AT_WIDTH), f32)] * 2,
        compiler_params=_params(("parallel", "parallel")),
    )(pat_view, pat_view, pat_view, pat_view, pat_view)


def _attn_backward_q(pat_view, d_out, delta, lse, dil):
    length = pat_view.shape[0]
    nb = length // Q_BLOCK
    scale = AT_DIM ** -0.5

    def body(q_ref, kp_ref, kc_ref, vp_ref, vc_ref, do_ref, dl_ref, lse_ref, dq_ref):
        mask = jnp.concatenate(_band_masks(pl.program_id(1)), axis=1)
        low = _low_half()
        heads = range(AT_HEADS)
        q, do = _head_pairs(q_ref, split=True), _head_pairs(do_ref, split=True)
        k = _stack_blocks(_head_pairs(kp_ref, split=False), _head_pairs(kc_ref, split=False))
        v = _stack_blocks(_head_pairs(vp_ref, split=False), _head_pairs(vc_ref, split=False))
        s = [_dot(q[h], k[h // 2], NT, False) for h in heads]
        dp = [_dot(do[h], v[h // 2], NT, False) for h in heads]
        ds = []
        for h in heads:
            lse_h = lse_ref[:, h * AT_DIM:h * AT_DIM + 1]
            dl_h = dl_ref[:, h * AT_DIM:h * AT_DIM + 1]
            p = jnp.exp(jnp.where(mask, s[h] * scale - lse_h, NEG))
            ds.append((p * (dp[h] - dl_h)).astype(bf16))
        dq = [_dot(ds[h], k[h // 2], NN, False) for h in heads]
        for pair in range(AT_HEADS // 2):
            dq_ref[:, pair * LANES:(pair + 1) * LANES] = jnp.where(low, dq[2 * pair], dq[2 * pair + 1]) * scale

    def blk(piece, prev):
        if prev:
            return pl.BlockSpec((Q_BLOCK, AT_WIDTH), lambda r, n: (jnp.maximum(n - 1, 0), 3 * r + piece))
        return pl.BlockSpec((Q_BLOCK, AT_WIDTH), lambda r, n: (n, 3 * r + piece))

    one = pl.BlockSpec((Q_BLOCK, AT_WIDTH), lambda r, n: (n, r))
    return pl.pallas_call(
        body, name=f"attn_backward_q_d{dil}", grid=(dil, nb),
        in_specs=[blk(0, False), blk(1, True), blk(1, False), blk(2, True), blk(2, False), one, one, one],
        out_specs=one,
        out_shape=jax.ShapeDtypeStruct((length, dil * AT_WIDTH), f32),
        compiler_params=_params(("parallel", "parallel")),
    )(pat_view, pat_view, pat_view, pat_view, pat_view, d_out, delta, lse)


def _attn_backward_kv(pat_view, d_out, delta, lse, dil):
    length = pat_view.shape[0]
    nb = length // Q_BLOCK
    scale = AT_DIM ** -0.5

    def body(k_ref, v_ref, qa_ref, qb_ref, doa_ref, dob_ref, dla_ref, dlb_ref, lsea_ref, lseb_ref, dk_ref, dv_ref):
        j = pl.program_id(1)
        kj = lax.broadcasted_iota(jnp.int32, (Q_BLOCK, Q_BLOCK), 0)
        qi = lax.broadcasted_iota(jnp.int32, (Q_BLOCK, Q_BLOCK), 1)
        mask = jnp.concatenate([kj <= qi,
                                (kj >= qi) & (j + 1 < nb)],
                               axis=1)
        low = _low_half()
        heads = range(AT_HEADS)
        k, v = _head_pairs(k_ref, split=True), _head_pairs(v_ref, split=True)
        q = _stack_blocks(_head_pairs(qa_ref, split=False), _head_pairs(qb_ref, split=False))
        do = _stack_blocks(_head_pairs(doa_ref, split=False), _head_pairs(dob_ref, split=False))
        row = lax.broadcasted_iota(jnp.int32, (8 * AT_HEADS, AT_WIDTH), 0)
        col = lax.broadcasted_iota(jnp.int32, (8 * AT_HEADS, AT_WIDTH), 1)
        pick = (col == AT_DIM * (row // 8)).astype(f32)
        lse_rows = _dot_sel(pick, jnp.concatenate([lsea_ref[...], lseb_ref[...]], axis=0), NT, 0)
        dl_rows = _dot_sel(pick, jnp.concatenate([dla_ref[...], dlb_ref[...]], axis=0), NT, 0)
        s_t = [_dot(k[h], q[h // 2], NT, False) for h in heads]
        dp_t = [_dot(v[h], do[h // 2], NT, False) for h in heads]
        p_t, ds_t = [], []
        for h in heads:
            prob = jnp.exp(jnp.where(mask, s_t[h] * scale - lse_rows[8 * h:8 * h + 1, :], NEG))
            p_t.append(prob.astype(bf16))
            ds_t.append((prob * (dp_t[h] - dl_rows[8 * h:8 * h + 1, :])).astype(bf16))
        dv = [_dot(p_t[h], do[h // 2], NN, False) for h in heads]
        dk = [_dot(ds_t[h], q[h // 2], NN, False) for h in heads]
        for pair in range(AT_HEADS // 2):
            slab = slice(pair * LANES, (pair + 1) * LANES)
            dk_ref[:, slab] = jnp.where(low, dk[2 * pair], dk[2 * pair + 1]) * scale
            dv_ref[:, slab] = jnp.where(low, dv[2 * pair], dv[2 * pair + 1])

    def kv(piece):
        return pl.BlockSpec((Q_BLOCK, AT_WIDTH), lambda r, n: (n, 3 * r + piece))

    q_same = pl.BlockSpec((Q_BLOCK, AT_WIDTH), lambda r, n: (n, 3 * r))
    q_next = pl.BlockSpec((Q_BLOCK, AT_WIDTH), lambda r, n: (jnp.minimum(n + 1, nb - 1), 3 * r))
    same = pl.BlockSpec((Q_BLOCK, AT_WIDTH), lambda r, n: (n, r))
    nxt = pl.BlockSpec((Q_BLOCK, AT_WIDTH), lambda r, n: (jnp.minimum(n + 1, nb - 1), r))
    return pl.pallas_call(
        body, name=f"attn_backward_kv_d{dil}", grid=(dil, nb),
        in_specs=[kv(1), kv(2), q_same, q_next, same, nxt, same, nxt, same, nxt],
        out_specs=[same, same],
        out_shape=[jax.ShapeDtypeStruct((length, dil * AT_WIDTH), f32)] * 2,
        compiler_params=_params(("parallel", "parallel")),
    )(pat_view, pat_view, pat_view, pat_view, d_out, d_out, delta, delta, lse, lse)


def _gated_norms(o_dn, z_dn, o_at, z_at, dn_w, at_w):
    ms_dn = _sel_right(o_dn * o_dn, _group_ones(DN_WIDTH, DN_DIM)) * (1.0 / DN_DIM)
    a = o_dn * lax.rsqrt(ms_dn + EPS) * dn_w * _silu(z_dn)
    ms_at = _sel_right(o_at * o_at, _group_ones(AT_WIDTH, AT_DIM)) * (1.0 / AT_DIM)
    b = o_at * lax.rsqrt(ms_at + EPS) * at_w * _silu(z_at)
    return a, b


def _residual_loss(x, mix, gate, fin_w, target):
    x2 = x + gate * mix
    y = x2 * lax.rsqrt(jnp.mean(x2 * x2, axis=-1, keepdims=True) + EPS) * fin_w
    err = y - target
    per_token = jnp.sum(err * err, axis=1, keepdims=True) * (1.0 / D_MODEL)
    return 0.5 * jnp.sum(per_token, axis=0, keepdims=True)


TAIL_ROWS = 8


def _tail(x, target, o_dn, pdn, outs, lses, z_at, gate, w_out, dn_w_row, at_w_row, fin_w):
    seq = x.shape[0]
    tm = TOKEN_TILE

    def body(x_ref, t_ref, odn_ref, zdn_ref, o1_ref, o4_ref, o16_ref, l1_ref, l4_ref, l16_ref, zat_ref, gate_ref,
             wt_ref, wb_ref, dnw_ref, atw_ref, fw_ref,
             dx2_ref, dodn_ref, dzdn_ref, doat1_ref, doat4_ref, doat16_ref, delta1_ref, delta4_ref, delta16_ref,
             lse1_ref, lse4_ref, lse16_ref, dzat_ref, gwt_ref, gwb_ref, rows_ref, *stages):
        @pl.when(pl.program_id(0) == 0)
        def _():
            gwt_ref[...] = jnp.zeros_like(gwt_ref)
            gwb_ref[...] = jnp.zeros_like(gwb_ref)
            rows_ref[...] = jnp.zeros_like(rows_ref)

        l1, l4, l16 = l1_ref[...], _view_to_value(stages[0], l4_ref, 4), _view_to_value(stages[1], l16_ref, 16)
        o4, o16 = _view_to_value(stages[2], o4_ref, 4), _view_to_value(stages[3], o16_ref, 16)
        top = jnp.maximum(jnp.maximum(l1, l4), l16)
        e1, e4, e16 = jnp.exp(l1 - top), jnp.exp(l4 - top), jnp.exp(l16 - top)
        den = e1 + e4 + e16
        o_at = (e1 * o1_ref[...] + e4 * o4 + e16 * o16) / den
        lse = top + jnp.log(den)

        (a, b), norms_vjp = jax.vjp(_gated_norms, odn_ref[...], zdn_ref[...], o_at, zat_ref[...], dnw_ref[...], atw_ref[...])
        a, b = a.astype(bf16), b.astype(bf16)
        mix = jnp.dot(a, wt_ref[...], preferred_element_type=f32) + jnp.dot(b, wb_ref[...], preferred_element_type=f32)
        loss, loss_vjp = jax.vjp(_residual_loss, x_ref[...], mix, gate_ref[...], fw_ref[...], t_ref[...])
        dx2, dmix, dgate, dfw, _ = loss_vjp(jnp.ones((1, 1), f32))
        dmix = dmix.astype(bf16)
        dwt = lax.dot_general(a, dmix, (TN, ((), ())), preferred_element_type=f32)
        dwb = lax.dot_general(b, dmix, (TN, ((), ())), preferred_element_type=f32)
        da = lax.dot_general(dmix, wt_ref[...], (NT, ((), ())), preferred_element_type=f32)
        db = lax.dot_general(dmix, wb_ref[...], (NT, ((), ())), preferred_element_type=f32)
        dodn, dzdn, doat, dzat, ddnw, datw = norms_vjp((da, db))
        dx2_ref[...] = dx2
        dodn_ref[...] = dodn
        dzdn_ref[...] = dzdn
        dzat_ref[...] = dzat
        delta = _sel_right(doat * o_at, _group_ones(AT_WIDTH, AT_DIM))
        for stage_ref, value, refs in ((stages[0], doat, (doat1_ref, doat4_ref, doat16_ref)),
                                       (stages[1], delta, (delta1_ref, delta4_ref, delta16_ref)),
                                       (stages[2], lse, (lse1_ref, lse4_ref, lse16_ref))):
            refs[0][...] = value
            _stage_lanes(stage_ref, 0, value)
            _stage_to_view(stage_ref, refs[1], 4)
            _stage_to_view(stage_ref, refs[2], 16)
        gwt_ref[...] += dwt
        gwb_ref[...] += dwb
        rows_ref[0:1, :] += jnp.broadcast_to(loss, (1, D_MODEL))
        rows_ref[1:2, :] += dgate
        rows_ref[2:3, :] += dfw
        rows_ref[3:4, 0:DN_WIDTH] += ddnw
        rows_ref[4:5, 0:AT_WIDTH] += datw

    tile = lambda w: pl.BlockSpec((tm, w), lambda i: (i, 0))
    row = lambda w: pl.BlockSpec((1, w), lambda i: (0, 0))
    half_w = pl.BlockSpec((DN_WIDTH, D_MODEL), lambda i: (0, 0))
    sds = lambda w: jax.ShapeDtypeStruct((seq, w), f32)
    views = [_view_spec(AT_WIDTH, d) for d in PATTERN_DILATIONS]
    view_shapes = [_view_shape(seq, AT_WIDTH, d) for d in PATTERN_DILATIONS]
    return pl.pallas_call(
        body, name="tail", grid=(seq // tm,),
        in_specs=[tile(D_MODEL), tile(D_MODEL), tile(DN_WIDTH),
                  pl.BlockSpec((tm, DN_WIDTH), lambda i: (i, QKV_W // DN_WIDTH)),
                  *views, *views, tile(AT_WIDTH), row(D_MODEL), half_w, pl.BlockSpec((AT_WIDTH, D_MODEL), lambda i: (1, 0)),
                  row(DN_WIDTH), row(AT_WIDTH), row(D_MODEL)],
        out_specs=[tile(D_MODEL), tile(DN_WIDTH), tile(DN_WIDTH), *views, *views, *views,
                   tile(AT_WIDTH), half_w, half_w, pl.BlockSpec((TAIL_ROWS, D_MODEL), lambda i: (0, 0))],
        out_shape=[sds(D_MODEL), sds(DN_WIDTH), sds(DN_WIDTH), *view_shapes, *view_shapes, *view_shapes,
                   sds(AT_WIDTH), jax.ShapeDtypeStruct((DN_WIDTH, D_MODEL), f32),
                   jax.ShapeDtypeStruct((AT_WIDTH, D_MODEL), f32), jax.ShapeDtypeStruct((TAIL_ROWS, D_MODEL), f32)],
        scratch_shapes=[_stage(AT_WIDTH)] * 4,
        compiler_params=_params(("arbitrary",)),
    )(x, target, o_dn, pdn, outs[0], outs[1], outs[2], lses[0], lses[1], lses[2], z_at, gate, w_out, w_out,
      dn_w_row, at_w_row, fin_w)


PRE_ROWS = 8


def _pre_backward(x, dx2, pos_col, invf_row, sign_row, norm_w, scale, shift, w_cat, conv_w8, pdn, d_conv, d_zdn, d_ba,
                  d_q, d_k, d_v, d_zat):
    seq = x.shape[0]
    tm = TOKEN_TILE
    last = seq // tm - 1

    def body(x_ref, dx2_ref, pos_ref, invf_ref, sign_ref, nw_ref, sc_ref, sh_ref, w_ref, cw_ref,
             pre_ref, prehalo_ref, dc_ref, dchalo_ref, dz_ref, dba_ref,
             dq1_ref, dq4_ref, dq16_ref, dk1_ref, dk4_ref, dk16_ref, dv1_ref, dv4_ref, dv16_ref, dzat_ref,
             gx_ref, dproj_ref, rows_ref, crow_ref, ext_ref, *stages):
        i = pl.program_id(0)

        @pl.when(i == 0)
        def _():
            rows_ref[...] = jnp.zeros_like(rows_ref)
            crow_ref[...] = jnp.zeros_like(crow_ref)

        dc = dc_ref[...]
        ext_ref[0:tm, :] = dc
        ext_ref[tm:tm + 8, :] = jnp.where(i < last, dchalo_ref[...], 0.0)
        w8 = cw_ref[...]
        d_pre = ext_ref[pl.ds(3, tm), :] * w8[0:1, :]
        for j in range(1, CONV_K):
            d_pre = d_pre + ext_ref[pl.ds(3 - j, tm), :] * w8[j:j + 1, :]
        ext_ref[0:8, :] = jnp.where(i > 0, prehalo_ref[...], 0.0)
        ext_ref[8:8 + tm, :] = pre_ref[...]
        for j in range(CONV_K):
            crow_ref[j:j + 1, :] += jnp.sum(dc * ext_ref[pl.ds(5 + j, tm), :], axis=0, keepdims=True)

        cos_t, sin_t = _rope_tables(pos_ref[...], invf_ref[...], sign_ref[...])
        dq = dq1_ref[...] + _view_to_value(stages[0], dq4_ref, 4) + _view_to_value(stages[1], dq16_ref, 16)
        dk = dk1_ref[...] + _view_to_value(stages[2], dk4_ref, 4) + _view_to_value(stages[3], dk16_ref, 16)
        dq = dq * cos_t + _rope_partner(dq * sin_t)
        dk = dk * cos_t + _rope_partner(dk * sin_t)
        dv = dv1_ref[...] + _view_to_value(stages[4], dv4_ref, 4) + _view_to_value(stages[5], dv16_ref, 16)
        d_proj = jnp.concatenate([d_pre, dz_ref[...], dba_ref[...], dq, dk, dv, dzat_ref[...]], axis=1).astype(bf16)
        dproj_ref[...] = d_proj
        dh = lax.dot_general(d_proj, w_ref[...], (NT, ((), ())), preferred_element_type=f32)
        _, vjp = jax.vjp(_hnorm, x_ref[...], nw_ref[...], sc_ref[...], sh_ref[...])
        dx, dnw, dsc, dsh = vjp(dh)
        gx_ref[...] = dx + dx2_ref[...]
        rows_ref[0:1, :] += dnw
        rows_ref[1:2, :] += dsc
        rows_ref[2:3, :] += dsh

    tile = lambda w: pl.BlockSpec((tm, w), lambda i: (i, 0))
    row = lambda w: pl.BlockSpec((1, w), lambda i: (0, 0))
    step8 = tm // 8
    return pl.pallas_call(
        body, name="pre_backward", grid=(seq // tm,),
        in_specs=[tile(D_MODEL), tile(D_MODEL), tile(1), row(128), row(128), row(D_MODEL), row(D_MODEL), row(D_MODEL),
                  pl.BlockSpec((D_MODEL, CAT_W), lambda i: (0, 0)), pl.BlockSpec((8, QKV_W), lambda i: (0, 0)),
                  tile(QKV_W), pl.BlockSpec((8, QKV_W), lambda i: (jnp.maximum(i * step8 - 1, 0), 0)),
                  tile(QKV_W), pl.BlockSpec((8, QKV_W), lambda i: (jnp.minimum((i + 1) * step8, seq // 8 - 1), 0)),
                  tile(DN_WIDTH), tile(BA_W)] + [_view_spec(AT_WIDTH, d) for d in PATTERN_DILATIONS] * 3 + [tile(AT_WIDTH)],
        out_specs=[tile(D_MODEL), tile(CAT_W), pl.BlockSpec((PRE_ROWS, D_MODEL), lambda i: (0, 0)),
                   pl.BlockSpec((8, QKV_W), lambda i: (0, 0))],
        out_shape=[jax.ShapeDtypeStruct((seq, D_MODEL), f32), jax.ShapeDtypeStruct((seq, CAT_W), bf16),
                   jax.ShapeDtypeStruct((PRE_ROWS, D_MODEL), f32), jax.ShapeDtypeStruct((8, QKV_W), f32)],
        scratch_shapes=[pltpu.VMEM((tm + 8, QKV_W), f32)] + [_stage(AT_WIDTH)] * 6,
        compiler_params=_params(("arbitrary",)),
    )(x, dx2, pos_col, invf_row, sign_row, norm_w, scale, shift, w_cat, conv_w8, pdn, pdn, d_conv, d_conv, d_zdn, d_ba,
      d_q[0], d_q[1], d_q[2], d_k[0], d_k[1], d_k[2], d_v[0], d_v[1], d_v[2], d_zat)


def _weight_grad(h, d_proj):
    seq = h.shape[0]
    tk, tn = 1024, 384
    n_k = seq // tk

    def body(h_ref, d_ref, o_ref):
        @pl.when(pl.program_id(1) == 0)
        def _():
            o_ref[...] = jnp.zeros_like(o_ref)

        o_ref[...] += lax.dot_general(h_ref[...], d_ref[...], (TN, ((), ())), preferred_element_type=f32)

    return pl.pallas_call(
        body, name="weight_grad", grid=(CAT_W // tn, n_k),
        in_specs=[pl.BlockSpec((tk, D_MODEL), lambda n, k: (k, 0)), pl.BlockSpec((tk, tn), lambda n, k: (k, n))],
        out_specs=pl.BlockSpec((D_MODEL, tn), lambda n, k: (0, n)),
        out_shape=jax.ShapeDtypeStruct((D_MODEL, CAT_W), f32),
        compiler_params=_params(("parallel", "arbitrary")),
    )(h, d_proj)


def _adamw(w, g, m, v):
    m = ADAM_B1 * m + (1.0 - ADAM_B1) * g
    v = ADAM_B2 * v + (1.0 - ADAM_B2) * (g * g)
    m_hat = m / (1.0 - ADAM_B1 ** ADAM_STEP)
    v_hat = v / (1.0 - ADAM_B2 ** ADAM_STEP)
    delta = -ADAM_LR * (m_hat / (jnp.sqrt(v_hat) + ADAM_EPS) + ADAM_WD * w)
    return delta, m, v


def _mod_shard(c_all, w_mod_s, b_mod_s):
    def body(c_ref, w_ref, b_ref, o_ref):
        o_ref[...] = _dot(_silu(c_ref[...]), w_ref[...], NN, True) + b_ref[...]

    return pl.pallas_call(body, name="mod_shard", out_shape=jax.ShapeDtypeStruct((N_DEV, w_mod_s.shape[1]), f32),
                          compiler_params=_params())(c_all, w_mod_s, b_mod_s)


def _mod_update(c_all, d_mod_s, w, m, v):
    def body(c_ref, d_ref, w_ref, m_ref, v_ref, g_ref, dw_ref, nm_ref, nv_ref):
        g = _dot(_silu(c_ref[...]), d_ref[...], TN, True)
        g_ref[...] = g
        dw_ref[...], nm_ref[...], nv_ref[...] = _adamw(w_ref[...], g, m_ref[...], v_ref[...])

    return pl.pallas_call(body, name="mod_update", out_shape=[jax.ShapeDtypeStruct(w.shape, f32)] * 4,
                          compiler_params=_params())(c_all, d_mod_s, w, m, v)


def _adamw_rows(w, g, m, v, name):
    rows, width = w.shape
    tr = PACK_CHUNK if rows % PACK_CHUNK == 0 else rows

    def body(w_ref, g_ref, m_ref, v_ref, dw_ref, nm_ref, nv_ref):
        dw_ref[...], nm_ref[...], nv_ref[...] = _adamw(w_ref[...], g_ref[...], m_ref[...], v_ref[...])

    spec = pl.BlockSpec((tr, width), lambda i: (i, 0))
    return pl.pallas_call(body, name=name, grid=(rows // tr,), in_specs=[spec] * 4, out_specs=[spec] * 3,
                          out_shape=[jax.ShapeDtypeStruct(w.shape, f32)] * 3,
                          compiler_params=_params(("parallel",)))(w, g, m, v)


SM_NORM, SM_FIN, SM_DN, SM_AT, SM_A, SM_DT, SM_LOSS, SM_MOD = 0, 1024, 2048, 2560, 3072, 3200, 3328, 3456
SM_W = SM_MOD + 3 * D_MODEL
RS_NORM, RS_FIN, RS_DN, RS_AT, RS_A, RS_DT, RS_BMOD = 0, 1024, 2048, 2176, 2304, 2432, 2560
RS_W = RS_BMOD + 3 * D_MODEL


def _small_update(gathered, w, m, v):
    def body(g_ref, w_ref, m_ref, v_ref, grad_ref, dw_ref, nm_ref, nv_ref, loss_ref, dmod_ref):
        total = g_ref[0:1, :]
        for dev in range(1, N_DEV):
            total = total + g_ref[8 * dev:8 * dev + 1, :]
        for dev in range(N_DEV):
            dmod_ref[dev:dev + 1, :] = g_ref[8 * dev:8 * dev + 1, SM_MOD:SM_W]
        dn = total[:, SM_DN:SM_DN + DN_DIM]
        for h in range(1, DN_HEADS):
            dn = dn + total[:, SM_DN + h * DN_DIM:SM_DN + (h + 1) * DN_DIM]
        at = total[:, SM_AT:SM_AT + AT_DIM]
        for h in range(1, AT_HEADS):
            at = at + total[:, SM_AT + h * AT_DIM:SM_AT + (h + 1) * AT_DIM]
        grad_ref[:, RS_NORM:RS_FIN] = total[:, SM_NORM:SM_FIN]
        grad_ref[:, RS_FIN:RS_DN] = total[:, SM_FIN:SM_DN]
        grad_ref[:, RS_DN:RS_AT] = dn
        grad_ref[:, RS_AT:RS_A] = jnp.zeros((1, 128), f32)
        grad_ref[:, RS_AT:RS_AT + AT_DIM] = at
        grad_ref[:, RS_A:RS_DT] = total[:, SM_A:SM_DT]
        grad_ref[:, RS_DT:RS_BMOD] = total[:, SM_DT:SM_LOSS]
        grad_ref[:, RS_BMOD:RS_W] = total[:, SM_MOD:SM_W]
        loss_ref[...] = total[:, SM_LOSS:SM_MOD]
        dw_ref[...], nm_ref[...], nv_ref[...] = _adamw(w_ref[...], grad_ref[...], m_ref[...], v_ref[...])

    row = jax.ShapeDtypeStruct((1, RS_W), f32)
    return pl.pallas_call(
        body, name="small_update",
        out_shape=[row, row, row, row, jax.ShapeDtypeStruct((1, 128), f32), jax.ShapeDtypeStruct((N_DEV, 3 * D_MODEL), f32)],
        compiler_params=_params())(gathered, w, m, v)


def _all_gather_rows(block, name):
    m_per, n = block.shape

    def body(x_ref, out_ref, send_sems, recv_sems, local_sem):
        x, y, c = lax.axis_index("x"), lax.axis_index("y"), lax.axis_index("c")
        me, sibling = (x, y, c), (x, y, 1 - c)
        chips = [(1 - x, y), (x, 1 - y), (1 - x, 1 - y)]

        def rows(px, py, pc):
            return out_ref.at[pl.ds((4 * px + 2 * py + pc) * m_per, m_per), :]

        def copy(k, blk, to, src=None):
            return pltpu.make_async_remote_copy(
                src_ref=rows(*blk) if src is None else src, dst_ref=rows(*blk),
                send_sem=send_sems.at[k], recv_sem=recv_sems.at[k], device_id=to, device_id_type=MESH)

        mine = pltpu.make_async_copy(x_ref, rows(*me), local_sem)
        mine.start()
        first = [copy(0, me, sibling, src=x_ref)]
        first += [copy(1 + j, me, (*chip, c), src=x_ref) for j, chip in enumerate(chips)]
        for cp in first:
            cp.start()
        passed = [copy(4 + j, (*chip, c), sibling) for j, chip in enumerate(chips)]
        for j, chip in enumerate(chips):
            copy(1 + j, (*chip, c), me).wait_recv()
            passed[j].start()
        copy(0, sibling, me).wait_recv()
        for j, chip in enumerate(chips):
            copy(4 + j, (*chip, 1 - c), me).wait_recv()
        for cp in first + passed:
            cp.wait_send()
        mine.wait()

    return pl.pallas_call(
        body, name=name,
        out_shape=jax.ShapeDtypeStruct((N_DEV * m_per, n), block.dtype),
        in_specs=[pl.BlockSpec(memory_space=pltpu.VMEM)],
        out_specs=pl.BlockSpec(memory_space=pltpu.VMEM),
        scratch_shapes=[pltpu.SemaphoreType.DMA((7,)), pltpu.SemaphoreType.DMA((7,)), pltpu.SemaphoreType.DMA],
        compiler_params=pltpu.CompilerParams(vmem_limit_bytes=VMEM_LIMIT),
    )(block)


def _gather_weight_shards(packed):
    rows, width = packed.shape

    def body(src_ref, out_ref, send_sems, recv_sems, local_sem):
        x, y, c = lax.axis_index("x"), lax.axis_index("y"), lax.axis_index("c")
        chips = [(1 - x, y), (x, 1 - y), (1 - x, 1 - y)]

        def copy(k, owner, to):
            slot = out_ref.at[2 * owner[0] + owner[1]]
            return pltpu.make_async_remote_copy(
                src_ref=src_ref, dst_ref=slot, send_sem=send_sems.at[k], recv_sem=recv_sems.at[k],
                device_id=(*to, c), device_id_type=MESH)

        mine = pltpu.make_async_copy(src_ref, out_ref.at[2 * x + y], local_sem)
        mine.start()
        sends = [copy(k, (x, y), chip) for k, chip in enumerate(chips)]
        for cp in sends:
            cp.start()
        for k, chip in enumerate(chips):
            copy(k, chip, (x, y)).wait_recv()
        for cp in sends:
            cp.wait_send()
        mine.wait()

    return pl.pallas_call(
        body, name="gather_weight_shards",
        out_shape=jax.ShapeDtypeStruct((N_CHIPS, rows, width), packed.dtype),
        in_specs=[pl.BlockSpec(memory_space=pl.ANY)],
        out_specs=pl.BlockSpec(memory_space=pl.ANY),
        scratch_shapes=[pltpu.SemaphoreType.DMA((3,)), pltpu.SemaphoreType.DMA((3,)), pltpu.SemaphoreType.DMA],
    )(packed)


def _reduce_weight_grads(grads, grads_bf):
    _, rows, width = grads.shape
    n_chunks = rows // PACK_CHUNK

    def body(g_ref, gbf_ref, out_ref, own_ref, land_ref, part_ref, sib_ref, send_sems, recv_sems, local_sem):
        x, y, c = lax.axis_index("x"), lax.axis_index("y"), lax.axis_index("c")
        chips = [(1 - x, y), (x, 1 - y), (1 - x, 1 - y)]

        def copy(k, chip):
            return pltpu.make_async_remote_copy(
                src_ref=gbf_ref.at[2 * chip[0] + chip[1]], dst_ref=land_ref.at[k],
                send_sem=send_sems.at[k], recv_sem=recv_sems.at[k], device_id=(*chip, c), device_id_type=MESH)

        mine = pltpu.make_async_copy(g_ref.at[2 * x + y], own_ref, local_sem)
        mine.start()
        sends = [copy(k, chip) for k, chip in enumerate(chips)]
        for cp in sends:
            cp.start()
        mine.wait()
        for cp in sends:
            cp.wait_recv()

        def add_landed(i, carry):
            r = pl.ds(pl.multiple_of(i * PACK_CHUNK, 16), PACK_CHUNK)
            landed = [land_ref[k, r, :].astype(f32) for k in range(3)]
            part_ref[r, :] = ((own_ref[r, :] + landed[0]) + landed[1]) + landed[2]
            return carry

        lax.fori_loop(0, n_chunks, add_landed, 0)
        swap = pltpu.make_async_remote_copy(
            src_ref=part_ref, dst_ref=sib_ref, send_sem=send_sems.at[3], recv_sem=recv_sems.at[3],
            device_id=(x, y, 1 - c), device_id_type=MESH)
        swap.start()
        swap.wait_recv()

        def add_sibling(i, carry):
            r = pl.ds(pl.multiple_of(i * PACK_CHUNK, 8), PACK_CHUNK)
            out_ref[r, :] = part_ref[r, :] + sib_ref[r, :]
            return carry

        lax.fori_loop(0, n_chunks, add_sibling, 0)
        swap.wait_send()
        for cp in sends:
            cp.wait_send()

    buf = pltpu.VMEM((rows, width), f32)
    return pl.pallas_call(
        body, name="reduce_weight_grads",
        out_shape=jax.ShapeDtypeStruct((rows, width), f32),
        in_specs=[pl.BlockSpec(memory_space=pl.ANY), pl.BlockSpec(memory_space=pl.ANY)],
        out_specs=pl.BlockSpec(memory_space=pltpu.VMEM),
        scratch_shapes=[buf, pltpu.VMEM((3, rows, width), bf16), buf, buf,
                        pltpu.SemaphoreType.DMA((4,)), pltpu.SemaphoreType.DMA((4,)), pltpu.SemaphoreType.DMA],
        compiler_params=pltpu.CompilerParams(vmem_limit_bytes=VMEM_LIMIT),
    )(grads, grads_bf)


def _pack_shards(w_in_s, w_out_s, conv_s):
    lead = w_in_s.shape[:-2]
    conv_rows = jnp.pad(conv_s.reshape(*lead, CONV_K * 384), [(0, 0)] * len(lead) + [(0, 8 * D_MODEL - CONV_K * 384)])
    return jnp.concatenate([
        w_in_s.reshape(*lead, SHARD_IN, D_MODEL), w_out_s, conv_rows.reshape(*lead, 8, D_MODEL),
        jnp.zeros((*lead, PACK_ROWS - SHARD_IN - 256 - 8, D_MODEL), f32)], axis=-2)


CONV_WORDS = 2 * CONV_K * 384


def _pack_weights_bf16(w_in_s, w_out_s, conv_s):
    conv_bits = lax.bitcast_convert_type(conv_s, bf16).reshape(CONV_WORDS)
    return jnp.concatenate([
        w_in_s.astype(bf16).reshape(SHARD_IN, D_MODEL), w_out_s.astype(bf16),
        jnp.pad(conv_bits, (0, 8 * D_MODEL - CONV_WORDS)).reshape(8, D_MODEL),
        jnp.zeros((PACK_ROWS - SHARD_IN - 256 - 8, D_MODEL), bf16)], axis=0)


def _unpack_weights_bf16(shards):
    w_in_s = shards[:, :SHARD_IN].reshape(N_CHIPS, D_MODEL, SHARD_IN)
    w_out = shards[:, SHARD_IN:SHARD_IN + 256].reshape(D_MODEL, D_MODEL)
    conv_bits = shards[:, SHARD_IN + 256:SHARD_IN + 264].reshape(N_CHIPS, 8 * D_MODEL)[:, :CONV_WORDS]
    conv_s = lax.bitcast_convert_type(conv_bits.reshape(N_CHIPS, CONV_K, 384, 2), f32)
    return (jnp.transpose(w_in_s, (1, 0, 2)).reshape(D_MODEL, IN_COLS), w_out,
            jnp.transpose(conv_s, (1, 0, 2)).reshape(CONV_K, QKV_W))


def _unpack_shards(packed):
    lead = packed.shape[:-2]
    w_in_s = packed[..., :SHARD_IN, :].reshape(*lead, D_MODEL, SHARD_IN)
    w_out_s = packed[..., SHARD_IN:SHARD_IN + 256, :]
    conv_s = packed[..., SHARD_IN + 256:SHARD_IN + 264, :].reshape(*lead, 8 * D_MODEL)[..., :CONV_K * 384]
    return w_in_s, w_out_s, conv_s.reshape(*lead, CONV_K, 384)


def _local_step(x, target, pos_col, mod_row, norm_w, w_in, conv_w, a_log, dt_bias, dn_norm_w, at_norm_w, w_out, fin_w):
    shift, scale, gate = mod_row[:, :D_MODEL], mod_row[:, D_MODEL:2 * D_MODEL], mod_row[:, 2 * D_MODEL:]
    half = AT_DIM // 2
    lane = jnp.arange(128)
    inv_freq = ROPE_THETA ** (-jnp.arange(half, dtype=f32) / half)
    invf_row = inv_freq[lane % half].reshape(1, 128)
    sign_row = jnp.where((lane % AT_DIM) < half, -1.0, 1.0).astype(f32).reshape(1, 128)
    ba_w = jnp.pad(w_in[:, 2 * D_MODEL:2 * D_MODEL + 2 * DN_HEADS], ((0, 0), (0, BA_W - 2 * DN_HEADS)))
    w_cat = jnp.concatenate([w_in[:, :2 * D_MODEL], ba_w, w_in[:, 2 * D_MODEL + 2 * DN_HEADS:]], axis=1).astype(bf16)
    conv_w8 = jnp.pad(conv_w, ((0, 8 - CONV_K), (0, 0)))
    a_row = jnp.pad(a_log.reshape(1, DN_HEADS), ((0, 0), (DN_HEADS, BA_W - 2 * DN_HEADS)))
    dt_row = jnp.pad(dt_bias.reshape(1, DN_HEADS), ((0, 0), (DN_HEADS, BA_W - 2 * DN_HEADS)))
    dn_w_row = jnp.tile(dn_norm_w.reshape(1, DN_DIM), (1, DN_HEADS))
    at_w_row = jnp.tile(at_norm_w.reshape(1, AT_DIM), (1, AT_HEADS))
    norm_row = norm_w.reshape(1, D_MODEL)
    fin_row = fin_w.reshape(1, D_MODEL)
    w_out_bf = w_out.astype(bf16)

    pdn, *views, z_at, h = _pre_proj(x, pos_col, invf_row, sign_row, norm_row, scale, shift, w_cat)
    *dn_parts, dn_invs = _dn_intra_forward(pdn, conv_w8, a_row, dt_row)
    o_dn, states = _dn_scan_forward(*dn_parts)
    outs, lses = [], []
    for d, view in zip(PATTERN_DILATIONS, views):
        o, lse = _attn_forward(view, d)
        outs.append(o)
        lses.append(lse)
    (dx2, d_odn, d_zdn, *in_views, d_zat, g_wtop, g_wbot, tail_rows) = _tail(
        x, target, o_dn, pdn, outs, lses, z_at, gate, w_out_bf, dn_w_row, at_w_row, fin_row)
    d_oat, delta, lse_all = in_views[0:3], in_views[3:6], in_views[6:9]
    d_q, d_k, d_v = [], [], []
    for n, (d, view) in enumerate(zip(PATTERN_DILATIONS, views)):
        args = (d_oat[n], delta[n], lse_all[n])
        d_q.append(_attn_backward_q(view, *args, d))
        dk, dv = _attn_backward_kv(view, *args, d)
        d_k.append(dk)
        d_v.append(dv)
    d_parts = _dn_scan_backward(*dn_parts, states, d_odn)
    d_conv, d_ba, dn_rows = _dn_intra_backward(pdn, conv_w8, a_row, dt_row, dn_invs, *d_parts)
    grad_x, d_proj, pre_rows, conv_rows = _pre_backward(
        x, dx2, pos_col, invf_row, sign_row, norm_row, scale, shift, w_cat, conv_w8, pdn, d_conv, d_zdn, d_ba,
        d_q, d_k, d_v, d_zat)
    g_cat = _weight_grad(h, d_proj)
    g_w_in = jnp.concatenate([g_cat[:, :2 * D_MODEL], g_cat[:, 2 * D_MODEL:2 * D_MODEL + 2 * DN_HEADS],
                              g_cat[:, 2 * D_MODEL + BA_W:]], axis=1)
    g_w_out = jnp.concatenate([g_wtop, g_wbot], axis=0)
    g_conv = conv_rows[:CONV_K]
    small = jnp.concatenate([
        pre_rows[0:1], tail_rows[2:3], tail_rows[3:4, :DN_WIDTH], tail_rows[4:5, :AT_WIDTH],
        dn_rows[0:1], dn_rows[1:2], tail_rows[0:1, :128],
        pre_rows[2:3], pre_rows[1:2], tail_rows[1:2]], axis=1)
    return grad_x, g_w_in, g_w_out, g_conv, small


def kernel(x, c, positions, w_mod, b_mod, norm_w, w_in, conv_w, a_log, dt_bias, dn_norm_w, at_norm_w, w_out, final_norm_w, loss_target, m_w_mod, m_b_mod, m_norm_w, m_w_in, m_conv_w, m_a_log, m_dt_bias, m_dn_norm_w, m_at_norm_w, m_w_out, m_final_norm_w, v_w_mod, v_b_mod, v_norm_w, v_w_in, v_conv_w, v_a_log, v_dt_bias, v_dn_norm_w, v_at_norm_w, v_w_out, v_final_norm_w):
    seq = x.shape[1]
    ax, ay, ac = lax.axis_index("x"), lax.axis_index("y"), lax.axis_index("c")
    me = 4 * ax + 2 * ay + ac
    chip = 2 * ax + ay

    c_all = _all_gather_rows(jnp.pad(c, ((0, 7), (0, 0))), "gather_c").reshape(N_DEV, 8, D_MODEL)[:, 0]
    b_mod_s = lax.dynamic_slice_in_dim(b_mod, chip * 768, 768, axis=1)
    mod_part = _mod_shard(c_all, w_mod[0], b_mod_s)
    mod_all = _all_gather_rows(mod_part, "gather_mod").reshape(N_CHIPS, 2, N_DEV, 768)[:, 0]
    mod_row = lax.dynamic_index_in_dim(mod_all, me, axis=1, keepdims=False).reshape(1, 3 * D_MODEL)

    shards = _gather_weight_shards(_pack_weights_bf16(w_in[0], w_out[0], conv_w[0]))
    w_in_full, w_out_full, conv_full = _unpack_weights_bf16(shards)

    grad_x, g_w_in, g_w_out, g_conv, small = _local_step(
        x[0], loss_target[0], positions.reshape(seq, 1), mod_row, norm_w, w_in_full, conv_full, a_log, dt_bias,
        dn_norm_w, at_norm_w, w_out_full, final_norm_w)

    g_pack = _pack_shards(jnp.transpose(g_w_in.reshape(D_MODEL, N_CHIPS, SHARD_IN), (1, 0, 2)),
                          g_w_out.reshape(N_CHIPS, 256, D_MODEL),
                          jnp.transpose(g_conv.reshape(CONV_K, N_CHIPS, 384), (1, 0, 2)))
    g_mine = _reduce_weight_grads(g_pack, g_pack.astype(bf16))
    d_pack, m_pack, v_pack = _adamw_rows(_pack_shards(w_in[0], w_out[0], conv_w[0]), g_mine,
                                         _pack_shards(m_w_in[0], m_w_out[0], m_conv_w[0]),
                                         _pack_shards(v_w_in[0], v_w_out[0], v_conv_w[0]), "adamw_packed")
    grad_w_in, grad_w_out, grad_conv_w = _unpack_shards(g_mine)
    delta_w_in, delta_w_out, delta_conv_w = _unpack_shards(d_pack)
    new_m_w_in, new_m_w_out, new_m_conv_w = _unpack_shards(m_pack)
    new_v_w_in, new_v_w_out, new_v_conv_w = _unpack_shards(v_pack)

    gathered = _all_gather_rows(jnp.pad(small, ((0, 7), (0, 0))), "gather_small")

    def small_row(norm, fin, dn, at, a, dt, bmod):
        z = lambda n: jnp.zeros((1, n), f32)
        return jnp.concatenate([norm.reshape(1, -1), fin.reshape(1, -1), dn.reshape(1, -1), at.reshape(1, -1), z(64),
                                z(4), a.reshape(1, -1), z(120), z(4), dt.reshape(1, -1), z(120), bmod.reshape(1, -1)], axis=1)

    g_small, d_small, m_small, v_small, loss_row, d_mod_all = _small_update(
        gathered,
        small_row(norm_w, final_norm_w, dn_norm_w, at_norm_w, a_log, dt_bias, b_mod),
        small_row(m_norm_w, m_final_norm_w, m_dn_norm_w, m_at_norm_w, m_a_log, m_dt_bias, m_b_mod),
        small_row(v_norm_w, v_final_norm_w, v_dn_norm_w, v_at_norm_w, v_a_log, v_dt_bias, v_b_mod))

    def split_small(r):
        return (r[:, RS_BMOD:RS_W], r[:, RS_NORM:RS_FIN], r[:, RS_A + DN_HEADS:RS_A + 2 * DN_HEADS],
                r[:, RS_DT + DN_HEADS:RS_DT + 2 * DN_HEADS], r[:, RS_DN:RS_AT], r[:, RS_AT:RS_AT + AT_DIM],
                r[0, RS_FIN:RS_DN])

    d_mod_s = lax.dynamic_slice_in_dim(d_mod_all, chip * 768, 768, axis=1)
    pad_rows = lambda a: jnp.pad(a, ((0, 128 - N_DEV), (0, 0)))
    grad_w_mod, delta_w_mod, new_m_w_mod, new_v_w_mod = _mod_update(
        pad_rows(c_all), pad_rows(d_mod_s), w_mod[0], m_w_mod[0], v_w_mod[0])

    def ordered(w_mod_leaf, small_row_leaf, w_in_leaf, conv_leaf, w_out_leaf):
        b, n, a, dt, dn, at, fin = split_small(small_row_leaf)
        return [w_mod_leaf[None], b, n, w_in_leaf[None], conv_leaf[None], a, dt, dn, at, w_out_leaf[None], fin]

    loss = loss_row[0, 0]
    return (loss, grad_x[None],
            *ordered(grad_w_mod, g_small, grad_w_in, grad_conv_w, grad_w_out),
            *ordered(delta_w_mod, d_small, delta_w_in, delta_conv_w, delta_w_out),
            *ordered(new_m_w_mod, m_small, new_m_w_in, new_m_conv_w, new_m_w_out),
            *ordered(new_v_w_mod, v_small, new_v_w_in, new_v_conv_w, new_v_w_out))
```

```python
import functools

import jax
import jax.numpy as jnp
from jax import lax
from jax.experimental import pallas as pl
from jax.experimental.pallas import tpu as pltpu

f32 = jnp.float32
bf16 = jnp.bfloat16
HIGHEST = lax.Precision.HIGHEST
MESH = pl.DeviceIdType.MESH

D_MODEL = 1024
DN_HEADS = 4
DN_DIM = 128
DN_WIDTH = 512
AT_HEADS = 8
AT_DIM = 64
AT_WIDTH = 512
CHUNK = 64
Q_BLOCK = 128
CONV_K = 4
EPS = 1e-6
ROPE_THETA = 10000.0
PATTERN_DILATIONS = (1, 4, 16)
NEG = -1e30

QKV_W = 3 * DN_WIDTH
BA_W = 128
PDN_W = QKV_W + DN_WIDTH + BA_W
PAT_W = 3 * AT_WIDTH
CAT_W = PDN_W + PAT_W + AT_WIDTH
IN_COLS = 4104
N_CHIPS = 4
N_DEV = 8
SHARD_IN = IN_COLS // N_CHIPS
PACK_ROWS = 1296
PACK_CHUNK = 48

ADAM_LR = 0.001
ADAM_B1 = 0.9
ADAM_B2 = 0.999
ADAM_EPS = 1e-08
ADAM_WD = 0.01
ADAM_STEP = 10

VMEM_LIMIT = 56 * 1024 * 1024

NN = ((1,), (0,))
NT = ((1,), (1,))
TN = ((0,), (0,))


def _pieces(a, n):
    out = []
    for _ in range(n - 1):
        p = a.astype(bf16)
        out.append(p)
        a = a - p.astype(f32)
    out.append(a.astype(bf16))
    return out


def _dot(a, b, dims, exact):
    raw = lambda p, q: lax.dot_general(p, q, (dims, ((), ())), preferred_element_type=f32)
    if exact == "split":
        (ah, al), (bh, bl) = _pieces(a, 2), _pieces(b, 2)
        return raw(ah, bh) + (raw(ah, bl) + raw(al, bh))
    if exact:
        return lax.dot_general(a, b, (dims, ((), ())), precision=HIGHEST, preferred_element_type=f32)
    return raw(a.astype(bf16), b.astype(bf16))


def _dot_sel(sel, b, dims, sel_side):
    raw = lambda p, q: lax.dot_general(p, q, (dims, ((), ())), preferred_element_type=f32)
    sel = sel.astype(bf16)
    parts = [raw(sel, p) if sel_side == 0 else raw(p, sel) for p in _pieces(b, 3)]
    return (parts[0] + parts[1]) + parts[2]


@jax.custom_vjp
def _sel_left(sel, b):
    return _dot_sel(sel, b, NN, 0)


def _sel_left_fwd(sel, b):
    return _dot_sel(sel, b, NN, 0), sel


def _sel_left_bwd(sel, g):
    return jnp.zeros_like(sel), _dot_sel(sel, g, TN, 0)


_sel_left.defvjp(_sel_left_fwd, _sel_left_bwd)


@jax.custom_vjp
def _sel_right(a, sel):
    return _dot_sel(sel, a, NN, 1)


def _sel_right_fwd(a, sel):
    return _dot_sel(sel, a, NN, 1), sel


def _sel_right_bwd(sel, g):
    return _dot_sel(sel, g, NT, 1), jnp.zeros_like(sel)


_sel_right.defvjp(_sel_right_fwd, _sel_right_bwd)


class _Matmuls:
    def __init__(self, exact, back):
        @jax.custom_vjp
        def nn(a, b):
            return _dot(a, b, NN, exact)

        def nn_fwd(a, b):
            return _dot(a, b, NN, exact), (a, b)

        def nn_bwd(res, g):
            a, b = res
            return _dot(g, b, NT, back), _dot(a, g, TN, back)

        nn.defvjp(nn_fwd, nn_bwd)

        @jax.custom_vjp
        def nt(a, b):
            return _dot(a, b, NT, exact)

        def nt_fwd(a, b):
            return _dot(a, b, NT, exact), (a, b)

        def nt_bwd(res, g):
            a, b = res
            return _dot(g, b, NN, back), _dot(g, a, TN, back)

        nt.defvjp(nt_fwd, nt_bwd)

        @jax.custom_vjp
        def tn(a, b):
            return _dot(a, b, TN, exact)

        def tn_fwd(a, b):
            return _dot(a, b, TN, exact), (a, b)

        def tn_bwd(res, g):
            a, b = res
            return _dot(b, g, NT, back), _dot(a, g, NN, back)

        tn.defvjp(tn_fwd, tn_bwd)
        self.nn, self.nt, self.tn = nn, nt, tn


MM = _Matmuls(exact=False, back=False)
MS = _Matmuls(exact="split", back=False)


def _each(fn, *lists):
    return [fn(*args) for args in zip(*lists)]


def _inverse_products(a_lows):
    ri = lax.broadcasted_iota(jnp.int32, (CHUNK, CHUNK), 0)
    ci = lax.broadcasted_iota(jnp.int32, (CHUNK, CHUNK), 1)
    eye = (ri == ci).astype(f32)
    power = _each(lambda a: -a, a_lows)
    inv = _each(lambda p: eye + p, power)
    for _ in range(5):
        power = _each(lambda p: _dot(p, p, NN, "split"), power)
        inv = _each(lambda x, p: x + _dot(x, p, NN, "split"), inv, power)
    return inv


def _inverse_cotangents(invs, gs):
    left = _each(lambda t, g: _dot(t, g, TN, "split"), invs, gs)
    return _each(lambda l, t: -_dot(l, t, NT, "split"), left, invs)


@jax.custom_vjp
def _unit_lower_inverses(a_lows):
    return _inverse_products(a_lows)


def _unit_lower_inverses_fwd(a_lows):
    invs = _inverse_products(a_lows)
    return invs, invs


def _unit_lower_inverses_bwd(invs, gs):
    return (_inverse_cotangents(invs, gs),)


_unit_lower_inverses.defvjp(_unit_lower_inverses_fwd, _unit_lower_inverses_bwd)


@jax.custom_vjp
def _known_inverses(a_lows, invs):
    return invs


def _known_inverses_fwd(a_lows, invs):
    return invs, invs


def _known_inverses_bwd(invs, gs):
    return _inverse_cotangents(invs, gs), _each(jnp.zeros_like, invs)


_known_inverses.defvjp(_known_inverses_fwd, _known_inverses_bwd)


def _params(semantics=None):
    return pltpu.CompilerParams(dimension_semantics=semantics, vmem_limit_bytes=VMEM_LIMIT)


def _silu(x):
    return x * jax.nn.sigmoid(x)


def _group_ones(width, group):
    r = lax.broadcasted_iota(jnp.int32, (width, width), 0) // group
    c = lax.broadcasted_iota(jnp.int32, (width, width), 1) // group
    return (r == c).astype(f32)


def _hnorm(x, nw, scale, shift):
    xn = x * lax.rsqrt(jnp.mean(x * x, axis=-1, keepdims=True) + EPS)
    return xn * nw * (1.0 + scale) + shift


def _rope_tables(pos_col, invf_row, sign_row):
    ang = pos_col.astype(f32) * invf_row
    cos_t = jnp.cos(ang)
    sin_t = jnp.sin(ang) * sign_row
    return jnp.concatenate([cos_t] * 4, axis=1), jnp.concatenate([sin_t] * 4, axis=1)


def _rope_partner(x):
    lane = lax.broadcasted_iota(jnp.int32, x.shape, 1)
    width = x.shape[1]
    return jnp.where((lane % AT_DIM) < AT_DIM // 2, pltpu.roll(x, width - AT_DIM // 2, 1), pltpu.roll(x, AT_DIM // 2, 1))


TOKEN_TILE = 256
LANES = 128


def _stage_lanes(stage_ref, first, value):
    for j in range(value.shape[1] // LANES):
        stage_ref[first + j] = value[:, LANES * j:LANES * (j + 1)]


def _stage_to_view(stage_ref, view_ref, dil):
    chunks, rows, _ = stage_ref.shape
    for r in range(dil):
        for j in range(chunks):
            col = (r * chunks + j) * LANES
            view_ref[:, col:col + LANES] = stage_ref.at[j][pl.ds(r, rows // dil, stride=dil), :].astype(view_ref.dtype)


def _view_to_value(stage_ref, view_ref, dil):
    chunks, rows, _ = stage_ref.shape
    for r in range(dil):
        for j in range(chunks):
            col = (r * chunks + j) * LANES
            stage_ref.at[j][pl.ds(r, rows // dil, stride=dil), :] = view_ref[:, col:col + LANES].astype(f32)
    return jnp.concatenate([stage_ref[j] for j in range(chunks)], axis=1)


def _view_spec(width, dil):
    return pl.BlockSpec((TOKEN_TILE // dil, dil * width), lambda i: (i, 0))


def _view_shape(seq, width, dil, dtype=f32):
    return jax.ShapeDtypeStruct((seq // dil, dil * width), dtype)


HEAD_LANES = 128


def _repeat_heads(compact):
    row = lax.broadcasted_iota(jnp.int32, (HEAD_LANES, AT_WIDTH), 0)
    col = lax.broadcasted_iota(jnp.int32, (HEAD_LANES, AT_WIDTH), 1)
    return _dot_sel((row == col // AT_DIM).astype(f32), compact, NN, 1)


def _stage(width):
    return pltpu.VMEM((width // LANES, TOKEN_TILE, LANES), f32)


def _pre_proj(x, pos_col, invf_row, sign_row, norm_w, scale, shift, w_cat):
    seq = x.shape[0]
    tm = TOKEN_TILE

    def body(x_ref, pos_ref, invf_ref, sign_ref, nw_ref, sc_ref, sh_ref, w_ref,
             pdn_ref, pat1_ref, pat4_ref, pat16_ref, zat_ref, h_ref, stage_ref):
        h = _hnorm(x_ref[...], nw_ref[...], sc_ref[...], sh_ref[...]).astype(bf16)
        h_ref[...] = h
        big = jnp.dot(h, w_ref[...], preferred_element_type=f32)
        pdn_ref[...] = big[:, :PDN_W]
        cos_t, sin_t = _rope_tables(pos_ref[...], invf_ref[...], sign_ref[...])
        q = big[:, PDN_W:PDN_W + AT_WIDTH]
        k = big[:, PDN_W + AT_WIDTH:PDN_W + 2 * AT_WIDTH]
        parts = (q * cos_t + _rope_partner(q) * sin_t, k * cos_t + _rope_partner(k) * sin_t,
                 big[:, PDN_W + 2 * AT_WIDTH:PDN_W + PAT_W])
        for p, part in enumerate(parts):
            pat1_ref[:, p * AT_WIDTH:(p + 1) * AT_WIDTH] = part.astype(bf16)
            _stage_lanes(stage_ref, p * (AT_WIDTH // LANES), part)
        _stage_to_view(stage_ref, pat4_ref, 4)
        _stage_to_view(stage_ref, pat16_ref, 16)
        zat_ref[...] = big[:, PDN_W + PAT_W:]

    row = lambda w: pl.BlockSpec((1, w), lambda i: (0, 0))
    tile = lambda w: pl.BlockSpec((tm, w), lambda i: (i, 0))
    return pl.pallas_call(
        body, name="pre_proj", grid=(seq // tm,),
        in_specs=[tile(D_MODEL), tile(1), row(128), row(128), row(D_MODEL), row(D_MODEL), row(D_MODEL),
                  pl.BlockSpec((D_MODEL, CAT_W), lambda i: (0, 0))],
        out_specs=[tile(PDN_W), tile(PAT_W), _view_spec(PAT_W, 4), _view_spec(PAT_W, 16), tile(AT_WIDTH), tile(D_MODEL)],
        out_shape=[jax.ShapeDtypeStruct((seq, PDN_W), f32), _view_shape(seq, PAT_W, 1, bf16), _view_shape(seq, PAT_W, 4, bf16),
                   _view_shape(seq, PAT_W, 16, bf16), jax.ShapeDtypeStruct((seq, AT_WIDTH), f32),
                   jax.ShapeDtypeStruct((seq, D_MODEL), bf16)],
        scratch_shapes=[_stage(PAT_W)],
        compiler_params=_params(("parallel",)),
    )(x, pos_col, invf_row, sign_row, norm_w, scale, shift, w_cat)


def _conv_taps(ext_ref, halo, cur, w8):
    rows = cur.shape[0]
    ext_ref[0:8, :] = halo
    ext_ref[8:8 + rows, :] = cur
    out = ext_ref[pl.ds(5, rows), :] * w8[0:1, :]
    for j in range(1, CONV_K):
        out = out + ext_ref[pl.ds(5 + j, rows), :] * w8[j:j + 1, :]
    return out


def _dn_pre(cq, ck, cv, ba, a_row, dt_row):
    ones = _group_ones(DN_WIDTH, DN_DIM)
    sq, sk, v = _silu(cq), _silu(ck), _silu(cv)
    qn = sq * lax.rsqrt(_sel_right(sq * sq, ones) + EPS)
    kn = sk * lax.rsqrt(_sel_right(sk * sk, ones) + EPS)
    beta_all = jax.nn.sigmoid(ba)
    g_all = -jnp.exp(a_row) * jax.nn.softplus(ba + dt_row)
    return qn, kn, v, beta_all, g_all


def _dn_intra(qs, ks, vs, betas, gs, known_invs=None):
    ri = lax.broadcasted_iota(jnp.int32, (CHUNK, CHUNK), 0)
    ci = lax.broadcasted_iota(jnp.int32, (CHUNK, CHUNK), 1)
    tril = ri >= ci
    strict = ri > ci
    lower = tril.astype(f32)
    each = _each
    g_wide = each(lambda g: jnp.broadcast_to(g, (CHUNK, DN_DIM)), gs)
    gc = each(lambda g: _sel_left(lower, g), g_wide)
    gc_sq = each(lambda g: _sel_left(lower, jnp.broadcast_to(g, (CHUNK, CHUNK))), gs)
    g_end = each(lambda g: _sel_left(jnp.ones((CHUNK, CHUNK), f32), g), g_wide)
    g_end8 = each(lambda g: _sel_left(jnp.ones((8, CHUNK), f32), g), g_wide)
    decay = each(lambda s: jnp.exp(jnp.where(tril, s - s.T, -jnp.inf)), gc_sq)
    qs = each(lambda q: q * (DN_DIM ** -0.5), qs)
    kb = each(lambda k, b: k * b, ks, betas)
    vb = each(lambda v, b: v * b, vs, betas)
    kk = each(MM.nt, kb, ks)
    qk = each(MM.nt, qs, ks)
    a_low = each(lambda p, d: jnp.where(strict, p * d, 0.0), kk, decay)
    inv = _unit_lower_inverses(a_low) if known_invs is None else _known_inverses(a_low, known_invs)
    e_gc = each(jnp.exp, gc)
    u = each(MS.nn, inv, vb)
    w = each(lambda x, k, e: MS.nn(x, k * e), inv, kb, e_gc)
    attn = each(lambda p, d: jnp.where(tril, p * d, 0.0), qk, decay)
    q_dec = each(lambda q, e: q * e, qs, e_gc)
    k_dec = each(lambda k, ge, c: k * jnp.exp(ge - c), ks, g_end, gc)
    return u, w, q_dec, k_dec, attn, each(jnp.exp, g_end8), inv


def _dn_step(us, ws, q_decs, k_decs, attns, e_ends, states):
    each = _each
    v_new = each(lambda u, w, s: u - MM.nn(w, s), us, ws, states)
    qs = each(MM.nn, q_decs, states)
    o = each(lambda a, b, c: a + MM.nn(b, c), qs, attns, v_new)
    new_states = each(lambda s, e, k, v: s * e + MM.tn(k, v), states, e_ends, k_decs, v_new)
    return o, new_states


INTRA_CHUNKS = 2
SCAN_CHUNKS = 4


def _intra_specs(nc):
    rows = INTRA_CHUNKS * CHUNK
    cur = pl.BlockSpec((rows, QKV_W), lambda i: (i, 0))
    halo = pl.BlockSpec((8, QKV_W), lambda i: (jnp.maximum(i * (rows // 8) - 1, 0), 0))
    ba = pl.BlockSpec((rows, BA_W), lambda i: (i, (QKV_W + DN_WIDTH) // BA_W))
    conv = pl.BlockSpec((8, QKV_W), lambda i: (0, 0))
    row = pl.BlockSpec((1, BA_W), lambda i: (0, 0))
    wide = pl.BlockSpec((rows, DN_WIDTH), lambda i: (i, 0))
    attn = pl.BlockSpec((INTRA_CHUNKS, DN_HEADS, CHUNK, CHUNK), lambda i: (i, 0, 0, 0))
    e_end = pl.BlockSpec((INTRA_CHUNKS, 8 * DN_HEADS, DN_DIM), lambda i: (i, 0, 0))
    return cur, halo, ba, conv, row, wide, attn, e_end


def _intra_items():
    return [(ci, h, slice(ci * CHUNK, (ci + 1) * CHUNK), slice(h * DN_DIM, (h + 1) * DN_DIM))
            for ci in range(INTRA_CHUNKS) for h in range(DN_HEADS)]


def _intra_inputs(items, qn, kn, v, beta_all, g_all):
    return ([qn[rows, lanes] for _, _, rows, lanes in items], [kn[rows, lanes] for _, _, rows, lanes in items],
            [v[rows, lanes] for _, _, rows, lanes in items], [beta_all[rows, h:h + 1] for _, h, rows, _ in items],
            [g_all[rows, DN_HEADS + h:DN_HEADS + h + 1] for _, h, rows, _ in items])


def _intra_shapes(seq, operand_dtype):
    nc = seq // CHUNK
    wide = jax.ShapeDtypeStruct((seq, DN_WIDTH), f32)
    operand = jax.ShapeDtypeStruct((seq, DN_WIDTH), operand_dtype)
    return [wide, operand, operand, operand, jax.ShapeDtypeStruct((nc, DN_HEADS, CHUNK, CHUNK), f32),
            jax.ShapeDtypeStruct((nc, 8 * DN_HEADS, DN_DIM), f32)]


def _dn_intra_forward(pdn, conv_w8, a_row, dt_row):
    seq = pdn.shape[0]
    nc = seq // CHUNK
    cur, halo, ba, conv, row, wide, attn, e_end = _intra_specs(nc)

    def body(cur_ref, halo_ref, ba_ref, w_ref, a_ref, dt_ref, u_ref, w_out_ref, qd_ref, kd_ref, attn_ref, e_ref, inv_ref,
             ext_ref):
        halo_rows = jnp.where(pl.program_id(0) > 0, halo_ref[...], 0.0)
        c = _conv_taps(ext_ref, halo_rows, cur_ref[...], w_ref[...])
        qn, kn, v, beta_all, g_all = _dn_pre(c[:, :DN_WIDTH], c[:, DN_WIDTH:2 * DN_WIDTH], c[:, 2 * DN_WIDTH:],
                                             ba_ref[...], a_ref[...], dt_ref[...])
        items = _intra_items()
        u, w, qd, kd, at, e8, inv = _dn_intra(*_intra_inputs(items, qn, kn, v, beta_all, g_all))
        for n, (ci, h, rows, lanes) in enumerate(items):
            u_ref[rows, lanes] = u[n]
            w_out_ref[rows, lanes] = w[n].astype(bf16)
            qd_ref[rows, lanes] = qd[n].astype(bf16)
            kd_ref[rows, lanes] = kd[n].astype(bf16)
            attn_ref[ci, h] = at[n]
            e_ref[ci, 8 * h:8 * h + 8, :] = e8[n]
            inv_ref[ci, h] = inv[n]

    return pl.pallas_call(
        body, name="dn_intra_forward", grid=(nc // INTRA_CHUNKS,),
        in_specs=[cur, halo, ba, conv, row, row],
        out_specs=[wide, wide, wide, wide, attn, e_end, attn],
        out_shape=_intra_shapes(seq, bf16) + [jax.ShapeDtypeStruct((nc, DN_HEADS, CHUNK, CHUNK), f32)],
        scratch_shapes=[pltpu.VMEM((INTRA_CHUNKS * CHUNK + 8, QKV_W), f32)],
        compiler_params=_params(("parallel",)),
    )(pdn, pdn, pdn, conv_w8, a_row, dt_row)


def _scan_specs(nc, reverse):
    steps = nc // SCAN_CHUNKS
    at = (lambda i: steps - 1 - i) if reverse else (lambda i: i)
    wide = pl.BlockSpec((SCAN_CHUNKS * CHUNK, DN_WIDTH), lambda i: (at(i), 0))
    attn = pl.BlockSpec((SCAN_CHUNKS, DN_HEADS, CHUNK, CHUNK), lambda i: (at(i), 0, 0, 0))
    e_end = pl.BlockSpec((SCAN_CHUNKS, 8 * DN_HEADS, DN_DIM), lambda i: (at(i), 0, 0))
    states = pl.BlockSpec((SCAN_CHUNKS, DN_HEADS, DN_DIM, DN_DIM), lambda i: (at(i), 0, 0, 0))
    return wide, attn, e_end, states


def _step_inputs(ci, rows, lanes, u_ref, w_ref, qd_ref, kd_ref, attn_ref, e_ref):
    heads = range(DN_HEADS)
    return ([u_ref[rows, lanes[h]] for h in heads], [w_ref[rows, lanes[h]].astype(f32) for h in heads],
            [qd_ref[rows, lanes[h]].astype(f32) for h in heads], [kd_ref[rows, lanes[h]].astype(f32) for h in heads],
            [attn_ref[ci, h] for h in heads], [e_ref[ci, 8 * h:8 * h + 1, :] for h in heads])


def _dn_scan_forward(u, w, q_dec, k_dec, attn, e_end):
    seq = u.shape[0]
    nc = seq // CHUNK
    wide, attn_spec, e_spec, st_spec = _scan_specs(nc, reverse=False)

    def body(u_ref, w_ref, qd_ref, kd_ref, attn_ref, e_ref, o_ref, st_ref, state_ref):
        @pl.when(pl.program_id(0) == 0)
        def _():
            state_ref[...] = jnp.zeros_like(state_ref)

        heads = range(DN_HEADS)
        lanes = [slice(h * DN_DIM, (h + 1) * DN_DIM) for h in heads]
        states = [state_ref[h] for h in heads]
        for ci in range(SCAN_CHUNKS):
            rows = slice(ci * CHUNK, (ci + 1) * CHUNK)
            for h in heads:
                st_ref[ci, h] = states[h].astype(bf16)
            o, states = _dn_step(*_step_inputs(ci, rows, lanes, u_ref, w_ref, qd_ref, kd_ref, attn_ref, e_ref), states)
            for h in heads:
                o_ref[rows, lanes[h]] = o[h]
        for h in heads:
            state_ref[h] = states[h]

    return pl.pallas_call(
        body, name="dn_scan_forward", grid=(nc // SCAN_CHUNKS,),
        in_specs=[wide, wide, wide, wide, attn_spec, e_spec],
        out_specs=[wide, st_spec],
        out_shape=[jax.ShapeDtypeStruct((seq, DN_WIDTH), f32), jax.ShapeDtypeStruct((nc, DN_HEADS, DN_DIM, DN_DIM), bf16)],
        scratch_shapes=[pltpu.VMEM((DN_HEADS, DN_DIM, DN_DIM), f32)],
        compiler_params=_params(("arbitrary",)),
    )(u, w, q_dec, k_dec, attn, e_end)


def _dn_scan_backward(u, w, q_dec, k_dec, attn, e_end, states, d_o):
    seq = u.shape[0]
    nc = seq // CHUNK
    wide, attn_spec, e_spec, st_spec = _scan_specs(nc, reverse=True)

    def body(u_ref, w_ref, qd_ref, kd_ref, attn_ref, e_ref, st_ref, do_ref,
             du_ref, dw_ref, dqd_ref, dkd_ref, dattn_ref, de_ref, dstate_ref):
        @pl.when(pl.program_id(0) == 0)
        def _():
            dstate_ref[...] = jnp.zeros_like(dstate_ref)

        heads = range(DN_HEADS)
        lanes = [slice(h * DN_DIM, (h + 1) * DN_DIM) for h in heads]
        first_row = lax.broadcasted_iota(jnp.int32, (8, DN_DIM), 0) == 0
        dstates = [dstate_ref[h] for h in heads]
        for ci in reversed(range(SCAN_CHUNKS)):
            rows = slice(ci * CHUNK, (ci + 1) * CHUNK)
            _, step_vjp = jax.vjp(_dn_step, *_step_inputs(ci, rows, lanes, u_ref, w_ref, qd_ref, kd_ref, attn_ref, e_ref),
                                  [st_ref[ci, h].astype(f32) for h in heads])
            du, dw, dqd, dkd, dattn, de, dstates = step_vjp(([do_ref[rows, lanes[h]] for h in heads], dstates))
            for h in heads:
                du_ref[rows, lanes[h]] = du[h]
                dw_ref[rows, lanes[h]] = dw[h]
                dqd_ref[rows, lanes[h]] = dqd[h]
                dkd_ref[rows, lanes[h]] = dkd[h]
                dattn_ref[ci, h] = dattn[h]
                de_ref[ci, 8 * h:8 * h + 8, :] = jnp.where(first_row, jnp.broadcast_to(de[h], (8, DN_DIM)), 0.0)
        for h in heads:
            dstate_ref[h] = dstates[h]

    return pl.pallas_call(
        body, name="dn_scan_backward", grid=(nc // SCAN_CHUNKS,),
        in_specs=[wide, wide, wide, wide, attn_spec, e_spec, st_spec, wide],
        out_specs=[wide, wide, wide, wide, attn_spec, e_spec],
        out_shape=_intra_shapes(seq, f32),
        scratch_shapes=[pltpu.VMEM((DN_HEADS, DN_DIM, DN_DIM), f32)],
        compiler_params=_params(("arbitrary",)),
    )(u, w, q_dec, k_dec, attn, e_end, states, d_o)


def _dn_intra_backward(pdn, conv_w8, a_row, dt_row, invs, d_u, d_w, d_qd, d_kd, d_attn, d_e):
    seq = pdn.shape[0]
    nc = seq // CHUNK
    rows_per_step = INTRA_CHUNKS * CHUNK
    cur, halo, ba, conv, row, wide, attn, e_end = _intra_specs(nc)

    def body(cur_ref, halo_ref, ba_ref, w_ref, a_ref, dt_ref, inv_ref, du_ref, dw_ref, dqd_ref, dkd_ref, dattn_ref, de_ref,
             dconv_ref, dba_ref, drow_ref, ext_ref):
        @pl.when(pl.program_id(0) == 0)
        def _():
            drow_ref[...] = jnp.zeros_like(drow_ref)

        halo_rows = jnp.where(pl.program_id(0) > 0, halo_ref[...], 0.0)
        c = _conv_taps(ext_ref, halo_rows, cur_ref[...], w_ref[...])
        (qn, kn, v, beta_all, g_all), pre_vjp = jax.vjp(
            _dn_pre, c[:, :DN_WIDTH], c[:, DN_WIDTH:2 * DN_WIDTH], c[:, 2 * DN_WIDTH:], ba_ref[...], a_ref[...], dt_ref[...])
        lane = lax.broadcasted_iota(jnp.int32, (CHUNK, BA_W), 1)
        items = _intra_items()
        _, intra_vjp = jax.vjp(_dn_intra, *_intra_inputs(items, qn, kn, v, beta_all, g_all),
                               [inv_ref[ci, h] for ci, h, _, _ in items])
        dq, dk, dv, dbeta, dg, _ = intra_vjp((
            [du_ref[rows, lanes] for _, _, rows, lanes in items], [dw_ref[rows, lanes] for _, _, rows, lanes in items],
            [dqd_ref[rows, lanes] for _, _, rows, lanes in items], [dkd_ref[rows, lanes] for _, _, rows, lanes in items],
            [dattn_ref[ci, h] for ci, h, _, _ in items], [de_ref[ci, 8 * h:8 * h + 8, :] for ci, h, _, _ in items],
            [jnp.zeros((CHUNK, CHUNK), f32) for _ in items]))
        dq_rows, dk_rows, dv_rows, dbeta_rows, dg_rows = [], [], [], [], []
        for ci in range(INTRA_CHUNKS):
            of_chunk = [n for n, item in enumerate(items) if item[0] == ci]
            d_beta_all = jnp.zeros((CHUNK, BA_W), f32)
            d_g_all = jnp.zeros((CHUNK, BA_W), f32)
            for n in of_chunk:
                h = items[n][1]
                d_beta_all = d_beta_all + jnp.where(lane == h, dbeta[n], 0.0)
                d_g_all = d_g_all + jnp.where(lane == DN_HEADS + h, dg[n], 0.0)
            dq_rows.append(jnp.concatenate([dq[n] for n in of_chunk], axis=1))
            dk_rows.append(jnp.concatenate([dk[n] for n in of_chunk], axis=1))
            dv_rows.append(jnp.concatenate([dv[n] for n in of_chunk], axis=1))
            dbeta_rows.append(d_beta_all)
            dg_rows.append(d_g_all)
        stack = lambda parts: jnp.concatenate(parts, axis=0)
        dcq, dck, dcv, dba, da_row, ddt_row = pre_vjp(
            (stack(dq_rows), stack(dk_rows), stack(dv_rows), stack(dbeta_rows), stack(dg_rows)))
        dconv_ref[:, :DN_WIDTH] = dcq
        dconv_ref[:, DN_WIDTH:2 * DN_WIDTH] = dck
        dconv_ref[:, 2 * DN_WIDTH:] = dcv
        dba_ref[...] = dba
        drow_ref[0:1, :] += da_row
        drow_ref[1:2, :] += ddt_row

    return pl.pallas_call(
        body, name="dn_intra_backward", grid=(nc // INTRA_CHUNKS,),
        in_specs=[cur, halo, ba, conv, row, row, attn, wide, wide, wide, wide, attn, e_end],
        out_specs=[pl.BlockSpec((rows_per_step, QKV_W), lambda i: (i, 0)),
                   pl.BlockSpec((rows_per_step, BA_W), lambda i: (i, 0)),
                   pl.BlockSpec((8, BA_W), lambda i: (0, 0))],
        out_shape=[jax.ShapeDtypeStruct((seq, QKV_W), f32), jax.ShapeDtypeStruct((seq, BA_W), f32),
                   jax.ShapeDtypeStruct((8, BA_W), f32)],
        scratch_shapes=[pltpu.VMEM((rows_per_step + 8, QKV_W), f32)],
        compiler_params=_params(("arbitrary",)),
    )(pdn, pdn, pdn, conv_w8, a_row, dt_row, invs, d_u, d_w, d_qd, d_kd, d_attn, d_e)


def _band_masks(block_index):
    qi = lax.broadcasted_iota(jnp.int32, (Q_BLOCK, Q_BLOCK), 0)
    kj = lax.broadcasted_iota(jnp.int32, (Q_BLOCK, Q_BLOCK), 1)
    return (kj >= qi) & (block_index > 0), kj <= qi


def _low_half():
    return lax.broadcasted_iota(jnp.int32, (Q_BLOCK, LANES), 1) < AT_DIM


def _head_pairs(ref, split):
    low = _low_half()
    slabs = [ref[:, pair * LANES:(pair + 1) * LANES].astype(bf16) for pair in range(AT_HEADS // 2)]
    if not split:
        return slabs
    zero = jnp.zeros((Q_BLOCK, LANES), bf16)
    return [jnp.where(low, slab, zero) if h == 0 else jnp.where(low, zero, slab) for slab in slabs for h in range(2)]


def _stack_blocks(first, second):
    return [jnp.concatenate([a, b], axis=0) for a, b in zip(first, second)]


def _attn_forward(pat_view, dil):
    length = pat_view.shape[0]
    nb = length // Q_BLOCK
    scale = AT_DIM ** -0.5

    def body(q_ref, kp_ref, kc_ref, vp_ref, vc_ref, o_ref, lse_ref):
        mask = jnp.concatenate(_band_masks(pl.program_id(1)), axis=1)
        low = _low_half()
        heads = range(AT_HEADS)
        q = _head_pairs(q_ref, split=True)
        k = _stack_blocks(_head_pairs(kp_ref, split=False), _head_pairs(kc_ref, split=False))
        v = _stack_blocks(_head_pairs(vp_ref, split=False), _head_pairs(vc_ref, split=False))
        s = [_dot(q[h], k[h // 2], NT, False) for h in heads]
        p, top = [], []
        for h in heads:
            masked = jnp.where(mask, s[h] * scale, NEG)
            m = jnp.max(masked, axis=1, keepdims=True)
            p.append(jnp.exp(masked - m).astype(bf16))
            top.append(m)
        ones = jnp.ones((2 * Q_BLOCK, LANES), bf16)
        l = [_dot(p[h], ones, NN, False) for h in heads]
        o = [_dot(p[h], v[h // 2], NN, False) for h in heads]
        for pair in range(AT_HEADS // 2):
            even, odd = 2 * pair, 2 * pair + 1
            slab = slice(pair * LANES, (pair + 1) * LANES)
            o_ref[:, slab] = jnp.where(low, o[even], o[odd]) / jnp.where(low, l[even], l[odd])
        lane = lax.broadcasted_iota(jnp.int32, (Q_BLOCK, HEAD_LANES), 1)
        lse = jnp.zeros((Q_BLOCK, HEAD_LANES), f32)
        for h in heads:
            lse = jnp.where(lane == h, top[h] + jnp.log(l[h]), lse)
        lse_ref[...] = lse

    def blk(piece, prev):
        if prev:
            return pl.BlockSpec((Q_BLOCK, AT_WIDTH), lambda r, n: (jnp.maximum(n - 1, 0), 3 * r + piece))
        return pl.BlockSpec((Q_BLOCK, AT_WIDTH), lambda r, n: (n, 3 * r + piece))

    out = pl.BlockSpec((Q_BLOCK, AT_WIDTH), lambda r, n: (n, r))
    return pl.pallas_call(
        body, name=f"attn_forward_d{dil}", grid=(dil, nb),
        in_specs=[blk(0, False), blk(1, True), blk(1, False), blk(2, True), blk(2, False)],
        out_specs=[out, pl.BlockSpec((Q_BLOCK, HEAD_LANES), lambda r, n: (n, r))],
        out_shape=[jax.ShapeDtypeStruct((length, dil * AT_WIDTH), f32),
                   jax.ShapeDtypeStruct((length, dil * HEAD_LANES), f32)],
        compiler_params=_params(("parallel", "parallel")),
    )(pat_view, pat_view, pat_view, pat_view, pat_view)


def _attn_backward_q(pat_view, d_out, delta, lse, dil):
    length = pat_view.shape[0]
    nb = length // Q_BLOCK
    scale = AT_DIM ** -0.5

    def body(q_ref, kp_ref, kc_ref, vp_ref, vc_ref, do_ref, dl_ref, lse_ref, dq_ref):
        mask = jnp.concatenate(_band_masks(pl.program_id(1)), axis=1)
        low = _low_half()
        heads = range(AT_HEADS)
        q, do = _head_pairs(q_ref, split=True), _head_pairs(do_ref, split=True)
        k = _stack_blocks(_head_pairs(kp_ref, split=False), _head_pairs(kc_ref, split=False))
        v = _stack_blocks(_head_pairs(vp_ref, split=False), _head_pairs(vc_ref, split=False))
        s = [_dot(q[h], k[h // 2], NT, False) for h in heads]
        dp = [_dot(do[h], v[h // 2], NT, False) for h in heads]
        ds = []
        for h in heads:
            p = jnp.exp(jnp.where(mask, s[h] * scale - lse_ref[:, h:h + 1], NEG))
            ds.append((p * (dp[h] - dl_ref[:, h:h + 1])).astype(bf16))
        dq = [_dot(ds[h], k[h // 2], NN, False) for h in heads]
        for pair in range(AT_HEADS // 2):
            dq_ref[:, pair * LANES:(pair + 1) * LANES] = (
                jnp.where(low, dq[2 * pair], dq[2 * pair + 1]) * scale).astype(bf16)

    def blk(piece, prev):
        if prev:
            return pl.BlockSpec((Q_BLOCK, AT_WIDTH), lambda r, n: (jnp.maximum(n - 1, 0), 3 * r + piece))
        return pl.BlockSpec((Q_BLOCK, AT_WIDTH), lambda r, n: (n, 3 * r + piece))

    one = pl.BlockSpec((Q_BLOCK, AT_WIDTH), lambda r, n: (n, r))
    compact = pl.BlockSpec((Q_BLOCK, HEAD_LANES), lambda r, n: (n, r))
    return pl.pallas_call(
        body, name=f"attn_backward_q_d{dil}", grid=(dil, nb),
        in_specs=[blk(0, False), blk(1, True), blk(1, False), blk(2, True), blk(2, False), one, compact, compact],
        out_specs=one,
        out_shape=jax.ShapeDtypeStruct((length, dil * AT_WIDTH), bf16),
        compiler_params=_params(("parallel", "parallel")),
    )(pat_view, pat_view, pat_view, pat_view, pat_view, d_out, delta, lse)


def _attn_backward_kv(pat_view, d_out, delta, lse, dil):
    length = pat_view.shape[0]
    nb = length // Q_BLOCK
    scale = AT_DIM ** -0.5

    def body(k_ref, v_ref, qa_ref, qb_ref, doa_ref, dob_ref, dla_ref, dlb_ref, lsea_ref, lseb_ref, dk_ref, dv_ref):
        j = pl.program_id(1)
        kj = lax.broadcasted_iota(jnp.int32, (Q_BLOCK, Q_BLOCK), 0)
        qi = lax.broadcasted_iota(jnp.int32, (Q_BLOCK, Q_BLOCK), 1)
        mask = jnp.concatenate([kj <= qi,
                                (kj >= qi) & (j + 1 < nb)],
                               axis=1)
        low = _low_half()
        heads = range(AT_HEADS)
        k, v = _head_pairs(k_ref, split=True), _head_pairs(v_ref, split=True)
        q = _stack_blocks(_head_pairs(qa_ref, split=False), _head_pairs(qb_ref, split=False))
        do = _stack_blocks(_head_pairs(doa_ref, split=False), _head_pairs(dob_ref, split=False))
        row = lax.broadcasted_iota(jnp.int32, (8 * AT_HEADS, HEAD_LANES), 0)
        col = lax.broadcasted_iota(jnp.int32, (8 * AT_HEADS, HEAD_LANES), 1)
        pick = (col == row // 8).astype(f32)
        lse_rows = _dot_sel(pick, jnp.concatenate([lsea_ref[...], lseb_ref[...]], axis=0), NT, 0)
        dl_rows = _dot_sel(pick, jnp.concatenate([dla_ref[...], dlb_ref[...]], axis=0), NT, 0)
        s_t = [_dot(k[h], q[h // 2], NT, False) for h in heads]
        dp_t = [_dot(v[h], do[h // 2], NT, False) for h in heads]
        p_t, ds_t = [], []
        for h in heads:
            prob = jnp.exp(jnp.where(mask, s_t[h] * scale - lse_rows[8 * h:8 * h + 1, :], NEG))
            p_t.append(prob.astype(bf16))
            ds_t.append((prob * (dp_t[h] - dl_rows[8 * h:8 * h + 1, :])).astype(bf16))
        dv = [_dot(p_t[h], do[h // 2], NN, False) for h in heads]
        dk = [_dot(ds_t[h], q[h // 2], NN, False) for h in heads]
        for pair in range(AT_HEADS // 2):
            slab = slice(pair * LANES, (pair + 1) * LANES)
            dk_ref[:, slab] = (jnp.where(low, dk[2 * pair], dk[2 * pair + 1]) * scale).astype(bf16)
            dv_ref[:, slab] = jnp.where(low, dv[2 * pair], dv[2 * pair + 1]).astype(bf16)

    def kv(piece):
        return pl.BlockSpec((Q_BLOCK, AT_WIDTH), lambda r, n: (n, 3 * r + piece))

    q_same = pl.BlockSpec((Q_BLOCK, AT_WIDTH), lambda r, n: (n, 3 * r))
    q_next = pl.BlockSpec((Q_BLOCK, AT_WIDTH), lambda r, n: (jnp.minimum(n + 1, nb - 1), 3 * r))
    same = pl.BlockSpec((Q_BLOCK, AT_WIDTH), lambda r, n: (n, r))
    nxt = pl.BlockSpec((Q_BLOCK, AT_WIDTH), lambda r, n: (jnp.minimum(n + 1, nb - 1), r))
    c_same = pl.BlockSpec((Q_BLOCK, HEAD_LANES), lambda r, n: (n, r))
    c_nxt = pl.BlockSpec((Q_BLOCK, HEAD_LANES), lambda r, n: (jnp.minimum(n + 1, nb - 1), r))
    return pl.pallas_call(
        body, name=f"attn_backward_kv_d{dil}", grid=(dil, nb),
        in_specs=[kv(1), kv(2), q_same, q_next, same, nxt, c_same, c_nxt, c_same, c_nxt],
        out_specs=[same, same],
        out_shape=[jax.ShapeDtypeStruct((length, dil * AT_WIDTH), bf16)] * 2,
        compiler_params=_params(("parallel", "parallel")),
    )(pat_view, pat_view, pat_view, pat_view, d_out, d_out, delta, delta, lse, lse)


def _gated_norms(o_dn, z_dn, o_at, z_at, dn_w, at_w):
    ms_dn = _sel_right(o_dn * o_dn, _group_ones(DN_WIDTH, DN_DIM)) * (1.0 / DN_DIM)
    a = o_dn * lax.rsqrt(ms_dn + EPS) * dn_w * _silu(z_dn)
    ms_at = _sel_right(o_at * o_at, _group_ones(AT_WIDTH, AT_DIM)) * (1.0 / AT_DIM)
    b = o_at * lax.rsqrt(ms_at + EPS) * at_w * _silu(z_at)
    return a, b


def _residual_loss(x, mix, gate, fin_w, target):
    x2 = x + gate * mix
    y = x2 * lax.rsqrt(jnp.mean(x2 * x2, axis=-1, keepdims=True) + EPS) * fin_w
    err = y - target
    per_token = jnp.sum(err * err, axis=1, keepdims=True) * (1.0 / D_MODEL)
    return 0.5 * jnp.sum(per_token, axis=0, keepdims=True)


TAIL_ROWS = 8


def _tail(x, target, o_dn, pdn, outs, lses, z_at, gate, w_out, dn_w_row, at_w_row, fin_w):
    seq = x.shape[0]
    tm = TOKEN_TILE

    def body(x_ref, t_ref, odn_ref, zdn_ref, o1_ref, o4_ref, o16_ref, l1_ref, l4_ref, l16_ref, zat_ref, gate_ref,
             wt_ref, wb_ref, dnw_ref, atw_ref, fw_ref,
             dx2_ref, dodn_ref, dzdn_ref, doat1_ref, doat4_ref, doat16_ref, delta1_ref, delta4_ref, delta16_ref,
             lse1_ref, lse4_ref, lse16_ref, dzat_ref, gwt_ref, gwb_ref, rows_ref, *stages):
        @pl.when(pl.program_id(0) == 0)
        def _():
            gwt_ref[...] = jnp.zeros_like(gwt_ref)
            gwb_ref[...] = jnp.zeros_like(gwb_ref)
            rows_ref[...] = jnp.zeros_like(rows_ref)

        l1, l4, l16 = l1_ref[...], _view_to_value(stages[0], l4_ref, 4), _view_to_value(stages[1], l16_ref, 16)
        o4, o16 = _view_to_value(stages[2], o4_ref, 4), _view_to_value(stages[3], o16_ref, 16)
        top = jnp.maximum(jnp.maximum(l1, l4), l16)
        e1, e4, e16 = jnp.exp(l1 - top), jnp.exp(l4 - top), jnp.exp(l16 - top)
        den = e1 + e4 + e16
        lse = top + jnp.log(den)
        o_at = (_repeat_heads(e1 / den) * o1_ref[...] + _repeat_heads(e4 / den) * o4 + _repeat_heads(e16 / den) * o16)

        (a, b), norms_vjp = jax.vjp(_gated_norms, odn_ref[...], zdn_ref[...], o_at, zat_ref[...], dnw_ref[...], atw_ref[...])
        a, b = a.astype(bf16), b.astype(bf16)
        mix = jnp.dot(a, wt_ref[...], preferred_element_type=f32) + jnp.dot(b, wb_ref[...], preferred_element_type=f32)
        loss, loss_vjp = jax.vjp(_residual_loss, x_ref[...], mix, gate_ref[...], fw_ref[...], t_ref[...])
        dx2, dmix, dgate, dfw, _ = loss_vjp(jnp.ones((1, 1), f32))
        dmix = dmix.astype(bf16)
        dwt = lax.dot_general(a, dmix, (TN, ((), ())), preferred_element_type=f32)
        dwb = lax.dot_general(b, dmix, (TN, ((), ())), preferred_element_type=f32)
        da = lax.dot_general(dmix, wt_ref[...], (NT, ((), ())), preferred_element_type=f32)
        db = lax.dot_general(dmix, wb_ref[...], (NT, ((), ())), preferred_element_type=f32)
        dodn, dzdn, doat, dzat, ddnw, datw = norms_vjp((da, db))
        dx2_ref[...] = dx2
        dodn_ref[...] = dodn
        dzdn_ref[...] = dzdn
        dzat_ref[...] = dzat
        lane_head = (lax.broadcasted_iota(jnp.int32, (AT_WIDTH, HEAD_LANES), 0) // AT_DIM
                     == lax.broadcasted_iota(jnp.int32, (AT_WIDTH, HEAD_LANES), 1))
        delta = _dot_sel(lane_head.astype(f32), doat * o_at, NN, 1)
        for stage_ref, value, refs in ((stages[2], doat, (doat1_ref, doat4_ref, doat16_ref)),
                                       (stages[0], delta, (delta1_ref, delta4_ref, delta16_ref)),
                                       (stages[1], lse, (lse1_ref, lse4_ref, lse16_ref))):
            refs[0][...] = value.astype(refs[0].dtype)
            _stage_lanes(stage_ref, 0, value)
            _stage_to_view(stage_ref, refs[1], 4)
            _stage_to_view(stage_ref, refs[2], 16)
        gwt_ref[...] += dwt
        gwb_ref[...] += dwb
        rows_ref[0:1, :] += jnp.broadcast_to(loss, (1, D_MODEL))
        rows_ref[1:2, :] += dgate
        rows_ref[2:3, :] += dfw
        rows_ref[3:4, 0:DN_WIDTH] += ddnw
        rows_ref[4:5, 0:AT_WIDTH] += datw

    tile = lambda w: pl.BlockSpec((tm, w), lambda i: (i, 0))
    row = lambda w: pl.BlockSpec((1, w), lambda i: (0, 0))
    half_w = pl.BlockSpec((DN_WIDTH, D_MODEL), lambda i: (0, 0))
    sds = lambda w: jax.ShapeDtypeStruct((seq, w), f32)
    views = [_view_spec(AT_WIDTH, d) for d in PATTERN_DILATIONS]
    compact = [_view_spec(HEAD_LANES, d) for d in PATTERN_DILATIONS]
    doat_shapes = [_view_shape(seq, AT_WIDTH, d, bf16) for d in PATTERN_DILATIONS]
    compact_shapes = [_view_shape(seq, HEAD_LANES, d) for d in PATTERN_DILATIONS]
    return pl.pallas_call(
        body, name="tail", grid=(seq // tm,),
        in_specs=[tile(D_MODEL), tile(D_MODEL), tile(DN_WIDTH),
                  pl.BlockSpec((tm, DN_WIDTH), lambda i: (i, QKV_W // DN_WIDTH)),
                  *views, *compact, tile(AT_WIDTH), row(D_MODEL), half_w, pl.BlockSpec((AT_WIDTH, D_MODEL), lambda i: (1, 0)),
                  row(DN_WIDTH), row(AT_WIDTH), row(D_MODEL)],
        out_specs=[tile(D_MODEL), tile(DN_WIDTH), tile(DN_WIDTH), *views, *compact, *compact,
                   tile(AT_WIDTH), half_w, half_w, pl.BlockSpec((TAIL_ROWS, D_MODEL), lambda i: (0, 0))],
        out_shape=[sds(D_MODEL), sds(DN_WIDTH), sds(DN_WIDTH), *doat_shapes, *compact_shapes, *compact_shapes,
                   sds(AT_WIDTH), jax.ShapeDtypeStruct((DN_WIDTH, D_MODEL), f32),
                   jax.ShapeDtypeStruct((AT_WIDTH, D_MODEL), f32), jax.ShapeDtypeStruct((TAIL_ROWS, D_MODEL), f32)],
        scratch_shapes=[_stage(HEAD_LANES)] * 2 + [_stage(AT_WIDTH)] * 2,
        compiler_params=_params(("arbitrary",)),
    )(x, target, o_dn, pdn, outs[0], outs[1], outs[2], lses[0], lses[1], lses[2], z_at, gate, w_out, w_out,
      dn_w_row, at_w_row, fin_w)


PRE_ROWS = 8


def _pre_backward(x, dx2, pos_col, invf_row, sign_row, norm_w, scale, shift, w_cat, conv_w8, pdn, d_conv, d_zdn, d_ba,
                  d_q, d_k, d_v, d_zat):
    seq = x.shape[0]
    tm = TOKEN_TILE
    last = seq // tm - 1

    def body(x_ref, dx2_ref, pos_ref, invf_ref, sign_ref, nw_ref, sc_ref, sh_ref, w_ref, cw_ref,
             pre_ref, prehalo_ref, dc_ref, dchalo_ref, dz_ref, dba_ref,
             dq1_ref, dq4_ref, dq16_ref, dk1_ref, dk4_ref, dk16_ref, dv1_ref, dv4_ref, dv16_ref, dzat_ref,
             gx_ref, dproj_ref, rows_ref, crow_ref, ext_ref, *stages):
        i = pl.program_id(0)

        @pl.when(i == 0)
        def _():
            rows_ref[...] = jnp.zeros_like(rows_ref)
            crow_ref[...] = jnp.zeros_like(crow_ref)

        dc = dc_ref[...]
        ext_ref[0:tm, :] = dc
        ext_ref[tm:tm + 8, :] = jnp.where(i < last, dchalo_ref[...], 0.0)
        w8 = cw_ref[...]
        d_pre = ext_ref[pl.ds(3, tm), :] * w8[0:1, :]
        for j in range(1, CONV_K):
            d_pre = d_pre + ext_ref[pl.ds(3 - j, tm), :] * w8[j:j + 1, :]
        ext_ref[0:8, :] = jnp.where(i > 0, prehalo_ref[...], 0.0)
        ext_ref[8:8 + tm, :] = pre_ref[...]
        for j in range(CONV_K):
            crow_ref[j:j + 1, :] += jnp.sum(dc * ext_ref[pl.ds(5 + j, tm), :], axis=0, keepdims=True)

        cos_t, sin_t = _rope_tables(pos_ref[...], invf_ref[...], sign_ref[...])
        dq = dq1_ref[...] + _view_to_value(stages[0], dq4_ref, 4) + _view_to_value(stages[1], dq16_ref, 16)
        dk = dk1_ref[...] + _view_to_value(stages[2], dk4_ref, 4) + _view_to_value(stages[3], dk16_ref, 16)
        dq = dq * cos_t + _rope_partner(dq * sin_t)
        dk = dk * cos_t + _rope_partner(dk * sin_t)
        dv = dv1_ref[...] + _view_to_value(stages[4], dv4_ref, 4) + _view_to_value(stages[5], dv16_ref, 16)
        d_proj = jnp.concatenate([d_pre, dz_ref[...], dba_ref[...], dq, dk, dv, dzat_ref[...]], axis=1).astype(bf16)
        dproj_ref[...] = d_proj
        dh = lax.dot_general(d_proj, w_ref[...], (NT, ((), ())), preferred_element_type=f32)
        _, vjp = jax.vjp(_hnorm, x_ref[...], nw_ref[...], sc_ref[...], sh_ref[...])
        dx, dnw, dsc, dsh = vjp(dh)
        gx_ref[...] = dx + dx2_ref[...]
        rows_ref[0:1, :] += dnw
        rows_ref[1:2, :] += dsc
        rows_ref[2:3, :] += dsh

    tile = lambda w: pl.BlockSpec((tm, w), lambda i: (i, 0))
    row = lambda w: pl.BlockSpec((1, w), lambda i: (0, 0))
    step8 = tm // 8
    return pl.pallas_call(
        body, name="pre_backward", grid=(seq // tm,),
        in_specs=[tile(D_MODEL), tile(D_MODEL), tile(1), row(128), row(128), row(D_MODEL), row(D_MODEL), row(D_MODEL),
                  pl.BlockSpec((D_MODEL, CAT_W), lambda i: (0, 0)), pl.BlockSpec((8, QKV_W), lambda i: (0, 0)),
                  tile(QKV_W), pl.BlockSpec((8, QKV_W), lambda i: (jnp.maximum(i * step8 - 1, 0), 0)),
                  tile(QKV_W), pl.BlockSpec((8, QKV_W), lambda i: (jnp.minimum((i + 1) * step8, seq // 8 - 1), 0)),
                  tile(DN_WIDTH), tile(BA_W)] + [_view_spec(AT_WIDTH, d) for d in PATTERN_DILATIONS] * 3 + [tile(AT_WIDTH)],
        out_specs=[tile(D_MODEL), tile(CAT_W), pl.BlockSpec((PRE_ROWS, D_MODEL), lambda i: (0, 0)),
                   pl.BlockSpec((8, QKV_W), lambda i: (0, 0))],
        out_shape=[jax.ShapeDtypeStruct((seq, D_MODEL), f32), jax.ShapeDtypeStruct((seq, CAT_W), bf16),
                   jax.ShapeDtypeStruct((PRE_ROWS, D_MODEL), f32), jax.ShapeDtypeStruct((8, QKV_W), f32)],
        scratch_shapes=[pltpu.VMEM((tm + 8, QKV_W), f32)] + [_stage(AT_WIDTH)] * 6,
        compiler_params=_params(("arbitrary",)),
    )(x, dx2, pos_col, invf_row, sign_row, norm_w, scale, shift, w_cat, conv_w8, pdn, pdn, d_conv, d_conv, d_zdn, d_ba,
      d_q[0], d_q[1], d_q[2], d_k[0], d_k[1], d_k[2], d_v[0], d_v[1], d_v[2], d_zat)


def _weight_grad(h, d_proj):
    seq = h.shape[0]
    tk, tn = 1024, 384
    n_k = seq // tk

    def body(h_ref, d_ref, o_ref):
        @pl.when(pl.program_id(1) == 0)
        def _():
            o_ref[...] = jnp.zeros_like(o_ref)

        o_ref[...] += lax.dot_general(h_ref[...], d_ref[...], (TN, ((), ())), preferred_element_type=f32)

    return pl.pallas_call(
        body, name="weight_grad", grid=(CAT_W // tn, n_k),
        in_specs=[pl.BlockSpec((tk, D_MODEL), lambda n, k: (k, 0)), pl.BlockSpec((tk, tn), lambda n, k: (k, n))],
        out_specs=pl.BlockSpec((D_MODEL, tn), lambda n, k: (0, n)),
        out_shape=jax.ShapeDtypeStruct((D_MODEL, CAT_W), f32),
        compiler_params=_params(("parallel", "arbitrary")),
    )(h, d_proj)


def _adamw(w, g, m, v):
    m = ADAM_B1 * m + (1.0 - ADAM_B1) * g
    v = ADAM_B2 * v + (1.0 - ADAM_B2) * (g * g)
    m_hat = m / (1.0 - ADAM_B1 ** ADAM_STEP)
    v_hat = v / (1.0 - ADAM_B2 ** ADAM_STEP)
    delta = -ADAM_LR * (m_hat / (jnp.sqrt(v_hat) + ADAM_EPS) + ADAM_WD * w)
    return delta, m, v


def _mod_shard(c_all, w_mod_s, b_mod_s):
    def body(c_ref, w_ref, b_ref, o_ref):
        o_ref[...] = _dot(_silu(c_ref[...]), w_ref[...], NN, True) + b_ref[...]

    return pl.pallas_call(body, name="mod_shard", out_shape=jax.ShapeDtypeStruct((N_DEV, w_mod_s.shape[1]), f32),
                          compiler_params=_params())(c_all, w_mod_s, b_mod_s)


def _mod_update(c_all, d_mod_s, w, m, v):
    def body(c_ref, d_ref, w_ref, m_ref, v_ref, g_ref, dw_ref, nm_ref, nv_ref):
        g = _dot(_silu(c_ref[...]), d_ref[...], TN, True)
        g_ref[...] = g
        dw_ref[...], nm_ref[...], nv_ref[...] = _adamw(w_ref[...], g, m_ref[...], v_ref[...])

    return pl.pallas_call(body, name="mod_update", out_shape=[jax.ShapeDtypeStruct(w.shape, f32)] * 4,
                          compiler_params=_params())(c_all, d_mod_s, w, m, v)


def _adamw_rows(w, g, m, v, name):
    rows, width = w.shape
    tr = PACK_CHUNK if rows % PACK_CHUNK == 0 else rows

    def body(w_ref, g_ref, m_ref, v_ref, dw_ref, nm_ref, nv_ref):
        dw_ref[...], nm_ref[...], nv_ref[...] = _adamw(w_ref[...], g_ref[...], m_ref[...], v_ref[...])

    spec = pl.BlockSpec((tr, width), lambda i: (i, 0))
    return pl.pallas_call(body, name=name, grid=(rows // tr,), in_specs=[spec] * 4, out_specs=[spec] * 3,
                          out_shape=[jax.ShapeDtypeStruct(w.shape, f32)] * 3,
                          compiler_params=_params(("parallel",)))(w, g, m, v)


SM_NORM, SM_FIN, SM_DN, SM_AT, SM_A, SM_DT, SM_LOSS, SM_MOD = 0, 1024, 2048, 2560, 3072, 3200, 3328, 3456
SM_W = SM_MOD + 3 * D_MODEL
RS_NORM, RS_FIN, RS_DN, RS_AT, RS_A, RS_DT, RS_BMOD = 0, 1024, 2048, 2176, 2304, 2432, 2560
RS_W = RS_BMOD + 3 * D_MODEL


def _small_update(gathered, w, m, v):
    def body(g_ref, w_ref, m_ref, v_ref, grad_ref, dw_ref, nm_ref, nv_ref, loss_ref, dmod_ref):
        total = g_ref[0:1, :]
        for dev in range(1, N_DEV):
            total = total + g_ref[8 * dev:8 * dev + 1, :]
        for dev in range(N_DEV):
            dmod_ref[dev:dev + 1, :] = g_ref[8 * dev:8 * dev + 1, SM_MOD:SM_W]
        dn = total[:, SM_DN:SM_DN + DN_DIM]
        for h in range(1, DN_HEADS):
            dn = dn + total[:, SM_DN + h * DN_DIM:SM_DN + (h + 1) * DN_DIM]
        at = total[:, SM_AT:SM_AT + AT_DIM]
        for h in range(1, AT_HEADS):
            at = at + total[:, SM_AT + h * AT_DIM:SM_AT + (h + 1) * AT_DIM]
        grad_ref[:, RS_NORM:RS_FIN] = total[:, SM_NORM:SM_FIN]
        grad_ref[:, RS_FIN:RS_DN] = total[:, SM_FIN:SM_DN]
        grad_ref[:, RS_DN:RS_AT] = dn
        grad_ref[:, RS_AT:RS_A] = jnp.zeros((1, 128), f32)
        grad_ref[:, RS_AT:RS_AT + AT_DIM] = at
        grad_ref[:, RS_A:RS_DT] = total[:, SM_A:SM_DT]
        grad_ref[:, RS_DT:RS_BMOD] = total[:, SM_DT:SM_LOSS]
        grad_ref[:, RS_BMOD:RS_W] = total[:, SM_MOD:SM_W]
        loss_ref[...] = total[:, SM_LOSS:SM_MOD]
        dw_ref[...], nm_ref[...], nv_ref[...] = _adamw(w_ref[...], grad_ref[...], m_ref[...], v_ref[...])

    row = jax.ShapeDtypeStruct((1, RS_W), f32)
    return pl.pallas_call(
        body, name="small_update",
        out_shape=[row, row, row, row, jax.ShapeDtypeStruct((1, 128), f32), jax.ShapeDtypeStruct((N_DEV, 3 * D_MODEL), f32)],
        compiler_params=_params())(gathered, w, m, v)


def _all_gather_rows(block, name):
    m_per, n = block.shape

    def body(x_ref, out_ref, send_sems, recv_sems, local_sem):
        x, y, c = lax.axis_index("x"), lax.axis_index("y"), lax.axis_index("c")
        me, sibling = (x, y, c), (x, y, 1 - c)
        chips = [(1 - x, y), (x, 1 - y), (1 - x, 1 - y)]

        def rows(px, py, pc):
            return out_ref.at[pl.ds((4 * px + 2 * py + pc) * m_per, m_per), :]

        def copy(k, blk, to, src=None):
            return pltpu.make_async_remote_copy(
                src_ref=rows(*blk) if src is None else src, dst_ref=rows(*blk),
                send_sem=send_sems.at[k], recv_sem=recv_sems.at[k], device_id=to, device_id_type=MESH)

        mine = pltpu.make_async_copy(x_ref, rows(*me), local_sem)
        mine.start()
        first = [copy(0, me, sibling, src=x_ref)]
        first += [copy(1 + j, me, (*chip, c), src=x_ref) for j, chip in enumerate(chips)]
        for cp in first:
            cp.start()
        passed = [copy(4 + j, (*chip, c), sibling) for j, chip in enumerate(chips)]
        for j, chip in enumerate(chips):
            copy(1 + j, (*chip, c), me).wait_recv()
            passed[j].start()
        copy(0, sibling, me).wait_recv()
        for j, chip in enumerate(chips):
            copy(4 + j, (*chip, 1 - c), me).wait_recv()
        for cp in first + passed:
            cp.wait_send()
        mine.wait()

    return pl.pallas_call(
        body, name=name,
        out_shape=jax.ShapeDtypeStruct((N_DEV * m_per, n), block.dtype),
        in_specs=[pl.BlockSpec(memory_space=pltpu.VMEM)],
        out_specs=pl.BlockSpec(memory_space=pltpu.VMEM),
        scratch_shapes=[pltpu.SemaphoreType.DMA((7,)), pltpu.SemaphoreType.DMA((7,)), pltpu.SemaphoreType.DMA],
        compiler_params=pltpu.CompilerParams(vmem_limit_bytes=VMEM_LIMIT),
    )(block)


def _gather_weight_shards(packed):
    rows, width = packed.shape

    def body(src_ref, out_ref, send_sems, recv_sems, local_sem):
        x, y, c = lax.axis_index("x"), lax.axis_index("y"), lax.axis_index("c")
        chips = [(1 - x, y), (x, 1 - y), (1 - x, 1 - y)]

        def copy(k, owner, to):
            slot = out_ref.at[2 * owner[0] + owner[1]]
            return pltpu.make_async_remote_copy(
                src_ref=src_ref, dst_ref=slot, send_sem=send_sems.at[k], recv_sem=recv_sems.at[k],
                device_id=(*to, c), device_id_type=MESH)

        mine = pltpu.make_async_copy(src_ref, out_ref.at[2 * x + y], local_sem)
        mine.start()
        sends = [copy(k, (x, y), chip) for k, chip in enumerate(chips)]
        for cp in sends:
            cp.start()
        for k, chip in enumerate(chips):
            copy(k, chip, (x, y)).wait_recv()
        for cp in sends:
            cp.wait_send()
        mine.wait()

    return pl.pallas_call(
        body, name="gather_weight_shards",
        out_shape=jax.ShapeDtypeStruct((N_CHIPS, rows, width), packed.dtype),
        in_specs=[pl.BlockSpec(memory_space=pl.ANY)],
        out_specs=pl.BlockSpec(memory_space=pl.ANY),
        scratch_shapes=[pltpu.SemaphoreType.DMA((3,)), pltpu.SemaphoreType.DMA((3,)), pltpu.SemaphoreType.DMA],
    )(packed)


def _reduce_weight_grads(grads, grads_bf):
    _, rows, width = grads.shape
    n_chunks = rows // PACK_CHUNK

    def body(g_ref, gbf_ref, out_ref, own_ref, land_ref, part_ref, sib_ref, send_sems, recv_sems, local_sem):
        x, y, c = lax.axis_index("x"), lax.axis_index("y"), lax.axis_index("c")
        chips = [(1 - x, y), (x, 1 - y), (1 - x, 1 - y)]

        def copy(k, chip):
            return pltpu.make_async_remote_copy(
                src_ref=gbf_ref.at[2 * chip[0] + chip[1]], dst_ref=land_ref.at[k],
                send_sem=send_sems.at[k], recv_sem=recv_sems.at[k], device_id=(*chip, c), device_id_type=MESH)

        mine = pltpu.make_async_copy(g_ref.at[2 * x + y], own_ref, local_sem)
        mine.start()
        sends = [copy(k, chip) for k, chip in enumerate(chips)]
        for cp in sends:
            cp.start()
        mine.wait()
        for cp in sends:
            cp.wait_recv()

        def add_landed(i, carry):
            r = pl.ds(pl.multiple_of(i * PACK_CHUNK, 16), PACK_CHUNK)
            landed = [land_ref[k, r, :].astype(f32) for k in range(3)]
            part_ref[r, :] = ((own_ref[r, :] + landed[0]) + landed[1]) + landed[2]
            return carry

        lax.fori_loop(0, n_chunks, add_landed, 0)
        swap = pltpu.make_async_remote_copy(
            src_ref=part_ref, dst_ref=sib_ref, send_sem=send_sems.at[3], recv_sem=recv_sems.at[3],
            device_id=(x, y, 1 - c), device_id_type=MESH)
        swap.start()
        swap.wait_recv()

        def add_sibling(i, carry):
            r = pl.ds(pl.multiple_of(i * PACK_CHUNK, 8), PACK_CHUNK)
            out_ref[r, :] = part_ref[r, :] + sib_ref[r, :]
            return carry

        lax.fori_loop(0, n_chunks, add_sibling, 0)
        swap.wait_send()
        for cp in sends:
            cp.wait_send()

    buf = pltpu.VMEM((rows, width), f32)
    return pl.pallas_call(
        body, name="reduce_weight_grads",
        out_shape=jax.ShapeDtypeStruct((rows, width), f32),
        in_specs=[pl.BlockSpec(memory_space=pl.ANY), pl.BlockSpec(memory_space=pl.ANY)],
        out_specs=pl.BlockSpec(memory_space=pltpu.VMEM),
        scratch_shapes=[buf, pltpu.VMEM((3, rows, width), bf16), buf, buf,
                        pltpu.SemaphoreType.DMA((4,)), pltpu.SemaphoreType.DMA((4,)), pltpu.SemaphoreType.DMA],
        compiler_params=pltpu.CompilerParams(vmem_limit_bytes=VMEM_LIMIT),
    )(grads, grads_bf)


def _pack_shards(w_in_s, w_out_s, conv_s):
    lead = w_in_s.shape[:-2]
    conv_rows = jnp.pad(conv_s.reshape(*lead, CONV_K * 384), [(0, 0)] * len(lead) + [(0, 8 * D_MODEL - CONV_K * 384)])
    return jnp.concatenate([
        w_in_s.reshape(*lead, SHARD_IN, D_MODEL), w_out_s, conv_rows.reshape(*lead, 8, D_MODEL),
        jnp.zeros((*lead, PACK_ROWS - SHARD_IN - 256 - 8, D_MODEL), f32)], axis=-2)


CONV_WORDS = 2 * CONV_K * 384


def _pack_weights_bf16(w_in_s, w_out_s, conv_s):
    conv_bits = lax.bitcast_convert_type(conv_s, bf16).reshape(CONV_WORDS)
    return jnp.concatenate([
        w_in_s.astype(bf16).reshape(SHARD_IN, D_MODEL), w_out_s.astype(bf16),
        jnp.pad(conv_bits, (0, 8 * D_MODEL - CONV_WORDS)).reshape(8, D_MODEL),
        jnp.zeros((PACK_ROWS - SHARD_IN - 256 - 8, D_MODEL), bf16)], axis=0)


def _unpack_weights_bf16(shards):
    w_in_s = shards[:, :SHARD_IN].reshape(N_CHIPS, D_MODEL, SHARD_IN)
    w_out = shards[:, SHARD_IN:SHARD_IN + 256].reshape(D_MODEL, D_MODEL)
    conv_bits = shards[:, SHARD_IN + 256:SHARD_IN + 264].reshape(N_CHIPS, 8 * D_MODEL)[:, :CONV_WORDS]
    conv_s = lax.bitcast_convert_type(conv_bits.reshape(N_CHIPS, CONV_K, 384, 2), f32)
    return (jnp.transpose(w_in_s, (1, 0, 2)).reshape(D_MODEL, IN_COLS), w_out,
            jnp.transpose(conv_s, (1, 0, 2)).reshape(CONV_K, QKV_W))


def _unpack_shards(packed):
    lead = packed.shape[:-2]
    w_in_s = packed[..., :SHARD_IN, :].reshape(*lead, D_MODEL, SHARD_IN)
    w_out_s = packed[..., SHARD_IN:SHARD_IN + 256, :]
    conv_s = packed[..., SHARD_IN + 256:SHARD_IN + 264, :].reshape(*lead, 8 * D_MODEL)[..., :CONV_K * 384]
    return w_in_s, w_out_s, conv_s.reshape(*lead, CONV_K, 384)


def _local_step(x, target, pos_col, mod_row, norm_w, w_in, conv_w, a_log, dt_bias, dn_norm_w, at_norm_w, w_out, fin_w):
    shift, scale, gate = mod_row[:, :D_MODEL], mod_row[:, D_MODEL:2 * D_MODEL], mod_row[:, 2 * D_MODEL:]
    half = AT_DIM // 2
    lane = jnp.arange(128)
    inv_freq = ROPE_THETA ** (-jnp.arange(half, dtype=f32) / half)
    invf_row = inv_freq[lane % half].reshape(1, 128)
    sign_row = jnp.where((lane % AT_DIM) < half, -1.0, 1.0).astype(f32).reshape(1, 128)
    ba_w = jnp.pad(w_in[:, 2 * D_MODEL:2 * D_MODEL + 2 * DN_HEADS], ((0, 0), (0, BA_W - 2 * DN_HEADS)))
    w_cat = jnp.concatenate([w_in[:, :2 * D_MODEL], ba_w, w_in[:, 2 * D_MODEL + 2 * DN_HEADS:]], axis=1).astype(bf16)
    conv_w8 = jnp.pad(conv_w, ((0, 8 - CONV_K), (0, 0)))
    a_row = jnp.pad(a_log.reshape(1, DN_HEADS), ((0, 0), (DN_HEADS, BA_W - 2 * DN_HEADS)))
    dt_row = jnp.pad(dt_bias.reshape(1, DN_HEADS), ((0, 0), (DN_HEADS, BA_W - 2 * DN_HEADS)))
    dn_w_row = jnp.tile(dn_norm_w.reshape(1, DN_DIM), (1, DN_HEADS))
    at_w_row = jnp.tile(at_norm_w.reshape(1, AT_DIM), (1, AT_HEADS))
    norm_row = norm_w.reshape(1, D_MODEL)
    fin_row = fin_w.reshape(1, D_MODEL)
    w_out_bf = w_out.astype(bf16)

    pdn, *views, z_at, h = _pre_proj(x, pos_col, invf_row, sign_row, norm_row, scale, shift, w_cat)
    *dn_parts, dn_invs = _dn_intra_forward(pdn, conv_w8, a_row, dt_row)
    o_dn, states = _dn_scan_forward(*dn_parts)
    outs, lses = [], []
    for d, view in zip(PATTERN_DILATIONS, views):
        o, lse = _attn_forward(view, d)
        outs.append(o)
        lses.append(lse)
    (dx2, d_odn, d_zdn, *in_views, d_zat, g_wtop, g_wbot, tail_rows) = _tail(
        x, target, o_dn, pdn, outs, lses, z_at, gate, w_out_bf, dn_w_row, at_w_row, fin_row)
    d_oat, delta, lse_all = in_views[0:3], in_views[3:6], in_views[6:9]
    d_q, d_k, d_v = [], [], []
    for n, (d, view) in enumerate(zip(PATTERN_DILATIONS, views)):
        args = (d_oat[n], delta[n], lse_all[n])
        d_q.append(_attn_backward_q(view, *args, d))
        dk, dv = _attn_backward_kv(view, *args, d)
        d_k.append(dk)
        d_v.append(dv)
    d_parts = _dn_scan_backward(*dn_parts, states, d_odn)
    d_conv, d_ba, dn_rows = _dn_intra_backward(pdn, conv_w8, a_row, dt_row, dn_invs, *d_parts)
    grad_x, d_proj, pre_rows, conv_rows = _pre_backward(
        x, dx2, pos_col, invf_row, sign_row, norm_row, scale, shift, w_cat, conv_w8, pdn, d_conv, d_zdn, d_ba,
        d_q, d_k, d_v, d_zat)
    g_cat = _weight_grad(h, d_proj)
    g_w_in = jnp.concatenate([g_cat[:, :2 * D_MODEL], g_cat[:, 2 * D_MODEL:2 * D_MODEL + 2 * DN_HEADS],
                              g_cat[:, 2 * D_MODEL + BA_W:]], axis=1)
    g_w_out = jnp.concatenate([g_wtop, g_wbot], axis=0)
    g_conv = conv_rows[:CONV_K]
    small = jnp.concatenate([
        pre_rows[0:1], tail_rows[2:3], tail_rows[3:4, :DN_WIDTH], tail_rows[4:5, :AT_WIDTH],
        dn_rows[0:1], dn_rows[1:2], tail_rows[0:1, :128],
        pre_rows[2:3], pre_rows[1:2], tail_rows[1:2]], axis=1)
    return grad_x, g_w_in, g_w_out, g_conv, small


def kernel(x, c, positions, w_mod, b_mod, norm_w, w_in, conv_w, a_log, dt_bias, dn_norm_w, at_norm_w, w_out, final_norm_w, loss_target, m_w_mod, m_b_mod, m_norm_w, m_w_in, m_conv_w, m_a_log, m_dt_bias, m_dn_norm_w, m_at_norm_w, m_w_out, m_final_norm_w, v_w_mod, v_b_mod, v_norm_w, v_w_in, v_conv_w, v_a_log, v_dt_bias, v_dn_norm_w, v_at_norm_w, v_w_out, v_final_norm_w):
    seq = x.shape[1]
    ax, ay, ac = lax.axis_index("x"), lax.axis_index("y"), lax.axis_index("c")
    me = 4 * ax + 2 * ay + ac
    chip = 2 * ax + ay

    c_all = _all_gather_rows(jnp.pad(c, ((0, 7), (0, 0))), "gather_c").reshape(N_DEV, 8, D_MODEL)[:, 0]
    b_mod_s = lax.dynamic_slice_in_dim(b_mod, chip * 768, 768, axis=1)
    mod_part = _mod_shard(c_all, w_mod[0], b_mod_s)
    mod_all = _all_gather_rows(mod_part, "gather_mod").reshape(N_CHIPS, 2, N_DEV, 768)[:, 0]
    mod_row = lax.dynamic_index_in_dim(mod_all, me, axis=1, keepdims=False).reshape(1, 3 * D_MODEL)

    shards = _gather_weight_shards(_pack_weights_bf16(w_in[0], w_out[0], conv_w[0]))
    w_in_full, w_out_full, conv_full = _unpack_weights_bf16(shards)

    grad_x, g_w_in, g_w_out, g_conv, small = _local_step(
        x[0], loss_target[0], positions.reshape(seq, 1), mod_row, norm_w, w_in_full, conv_full, a_log, dt_bias,
        dn_norm_w, at_norm_w, w_out_full, final_norm_w)

    g_pack = _pack_shards(jnp.transpose(g_w_in.reshape(D_MODEL, N_CHIPS, SHARD_IN), (1, 0, 2)),
                          g_w_out.reshape(N_CHIPS, 256, D_MODEL),
                          jnp.transpose(g_conv.reshape(CONV_K, N_CHIPS, 384), (1, 0, 2)))
    g_mine = _reduce_weight_grads(g_pack, g_pack.astype(bf16))
    d_pack, m_pack, v_pack = _adamw_rows(_pack_shards(w_in[0], w_out[0], conv_w[0]), g_mine,
                                         _pack_shards(m_w_in[0], m_w_out[0], m_conv_w[0]),
                                         _pack_shards(v_w_in[0], v_w_out[0], v_conv_w[0]), "adamw_packed")
    grad_w_in, grad_w_out, grad_conv_w = _unpack_shards(g_mine)
    delta_w_in, delta_w_out, delta_conv_w = _unpack_shards(d_pack)
    new_m_w_in, new_m_w_out, new_m_conv_w = _unpack_shards(m_pack)
    new_v_w_in, new_v_w_out, new_v_conv_w = _unpack_shards(v_pack)

    gathered = _all_gather_rows(jnp.pad(small, ((0, 7), (0, 0))), "gather_small")

    def small_row(norm, fin, dn, at, a, dt, bmod):
        z = lambda n: jnp.zeros((1, n), f32)
        return jnp.concatenate([norm.reshape(1, -1), fin.reshape(1, -1), dn.reshape(1, -1), at.reshape(1, -1), z(64),
                                z(4), a.reshape(1, -1), z(120), z(4), dt.reshape(1, -1), z(120), bmod.reshape(1, -1)], axis=1)

    g_small, d_small, m_small, v_small, loss_row, d_mod_all = _small_update(
        gathered,
        small_row(norm_w, final_norm_w, dn_norm_w, at_norm_w, a_log, dt_bias, b_mod),
        small_row(m_norm_w, m_final_norm_w, m_dn_norm_w, m_at_norm_w, m_a_log, m_dt_bias, m_b_mod),
        small_row(v_norm_w, v_final_norm_w, v_dn_norm_w, v_at_norm_w, v_a_log, v_dt_bias, v_b_mod))

    def split_small(r):
        return (r[:, RS_BMOD:RS_W], r[:, RS_NORM:RS_FIN], r[:, RS_A + DN_HEADS:RS_A + 2 * DN_HEADS],
                r[:, RS_DT + DN_HEADS:RS_DT + 2 * DN_HEADS], r[:, RS_DN:RS_AT], r[:, RS_AT:RS_AT + AT_DIM],
                r[0, RS_FIN:RS_DN])

    d_mod_s = lax.dynamic_slice_in_dim(d_mod_all, chip * 768, 768, axis=1)
    pad_rows = lambda a: jnp.pad(a, ((0, 128 - N_DEV), (0, 0)))
    grad_w_mod, delta_w_mod, new_m_w_mod, new_v_w_mod = _mod_update(
        pad_rows(c_all), pad_rows(d_mod_s), w_mod[0], m_w_mod[0], v_w_mod[0])

    def ordered(w_mod_leaf, small_row_leaf, w_in_leaf, conv_leaf, w_out_leaf):
        b, n, a, dt, dn, at, fin = split_small(small_row_leaf)
        return [w_mod_leaf[None], b, n, w_in_leaf[None], conv_leaf[None], a, dt, dn, at, w_out_leaf[None], fin]

    loss = loss_row[0, 0]
    return (loss, grad_x[None],
            *ordered(grad_w_mod, g_small, grad_w_in, grad_conv_w, grad_w_out),
            *ordered(delta_w_mod, d_small, delta_w_in, delta_conv_w, delta_w_out),
            *ordered(new_m_w_mod, m_small, new_m_w_in, new_m_conv_w, new_m_w_out),
            *ordered(new_v_w_mod, v_small, new_v_w_in, new_v_conv_w, new_v_w_out))
```

```python
import functools

import jax
import jax.numpy as jnp
from jax import lax
from jax.experimental import pallas as pl
from jax.experimental.pallas import tpu as pltpu

f32 = jnp.float32
bf16 = jnp.bfloat16
HIGHEST = lax.Precision.HIGHEST
MESH = pl.DeviceIdType.MESH

D_MODEL = 1024
DN_HEADS = 4
DN_DIM = 128
DN_WIDTH = 512
AT_HEADS = 8
AT_DIM = 64
AT_WIDTH = 512
CHUNK = 64
Q_BLOCK = 128
CONV_K = 4
EPS = 1e-6
ROPE_THETA = 10000.0
PATTERN_DILATIONS = (1, 4, 16)
NEG = -1e30

QKV_W = 3 * DN_WIDTH
BA_W = 128
PDN_W = QKV_W + DN_WIDTH + BA_W
PAT_W = 3 * AT_WIDTH
CAT_W = PDN_W + PAT_W + AT_WIDTH
IN_COLS = 4104
N_CHIPS = 4
N_DEV = 8
SHARD_IN = IN_COLS // N_CHIPS
PACK_ROWS = 1296
PACK_CHUNK = 48

ADAM_LR = 0.001
ADAM_B1 = 0.9
ADAM_B2 = 0.999
ADAM_EPS = 1e-08
ADAM_WD = 0.01
ADAM_STEP = 10

VMEM_LIMIT = 56 * 1024 * 1024

NN = ((1,), (0,))
NT = ((1,), (1,))
TN = ((0,), (0,))


def _pieces(a, n):
    out = []
    for _ in range(n - 1):
        p = a.astype(bf16)
        out.append(p)
        a = a - p.astype(f32)
    out.append(a.astype(bf16))
    return out


def _dot(a, b, dims, exact):
    raw = lambda p, q: lax.dot_general(p, q, (dims, ((), ())), preferred_element_type=f32)
    if exact == "split":
        (ah, al), (bh, bl) = _pieces(a, 2), _pieces(b, 2)
        return raw(ah, bh) + (raw(ah, bl) + raw(al, bh))
    if exact:
        return lax.dot_general(a, b, (dims, ((), ())), precision=HIGHEST, preferred_element_type=f32)
    return raw(a.astype(bf16), b.astype(bf16))


def _dot_sel(sel, b, dims, sel_side):
    raw = lambda p, q: lax.dot_general(p, q, (dims, ((), ())), preferred_element_type=f32)
    sel = sel.astype(bf16)
    parts = [raw(sel, p) if sel_side == 0 else raw(p, sel) for p in _pieces(b, 3)]
    return (parts[0] + parts[1]) + parts[2]


@jax.custom_vjp
def _sel_left(sel, b):
    return _dot_sel(sel, b, NN, 0)


def _sel_left_fwd(sel, b):
    return _dot_sel(sel, b, NN, 0), sel


def _sel_left_bwd(sel, g):
    return jnp.zeros_like(sel), _dot_sel(sel, g, TN, 0)


_sel_left.defvjp(_sel_left_fwd, _sel_left_bwd)


@jax.custom_vjp
def _sel_right(a, sel):
    return _dot_sel(sel, a, NN, 1)


def _sel_right_fwd(a, sel):
    return _dot_sel(sel, a, NN, 1), sel


def _sel_right_bwd(sel, g):
    return _dot_sel(sel, g, NT, 1), jnp.zeros_like(sel)


_sel_right.defvjp(_sel_right_fwd, _sel_right_bwd)


class _Matmuls:
    def __init__(self, exact, back):
        @jax.custom_vjp
        def nn(a, b):
            return _dot(a, b, NN, exact)

        def nn_fwd(a, b):
            return _dot(a, b, NN, exact), (a, b)

        def nn_bwd(res, g):
            a, b = res
            return _dot(g, b, NT, back), _dot(a, g, TN, back)

        nn.defvjp(nn_fwd, nn_bwd)

        @jax.custom_vjp
        def nt(a, b):
            return _dot(a, b, NT, exact)

        def nt_fwd(a, b):
            return _dot(a, b, NT, exact), (a, b)

        def nt_bwd(res, g):
            a, b = res
            return _dot(g, b, NN, back), _dot(g, a, TN, back)

        nt.defvjp(nt_fwd, nt_bwd)

        @jax.custom_vjp
        def tn(a, b):
            return _dot(a, b, TN, exact)

        def tn_fwd(a, b):
            return _dot(a, b, TN, exact), (a, b)

        def tn_bwd(res, g):
            a, b = res
            return _dot(b, g, NT, back), _dot(a, g, NN, back)

        tn.defvjp(tn_fwd, tn_bwd)
        self.nn, self.nt, self.tn = nn, nt, tn


MM = _Matmuls(exact=False, back=False)
MS = _Matmuls(exact="split", back=False)


def _each(fn, *lists):
    return [fn(*args) for args in zip(*lists)]


def _inverse_products(a_lows):
    ri = lax.broadcasted_iota(jnp.int32, (CHUNK, CHUNK), 0)
    ci = lax.broadcasted_iota(jnp.int32, (CHUNK, CHUNK), 1)
    eye = (ri == ci).astype(f32)
    power = _each(lambda a: -a, a_lows)
    inv = _each(lambda p: eye + p, power)
    for _ in range(5):
        power = _each(lambda p: _dot(p, p, NN, "split"), power)
        inv = _each(lambda x, p: x + _dot(x, p, NN, "split"), inv, power)
    return inv


def _inverse_cotangents(invs, gs):
    left = _each(lambda t, g: _dot(t, g, TN, "split"), invs, gs)
    return _each(lambda l, t: -_dot(l, t, NT, "split"), left, invs)


@jax.custom_vjp
def _unit_lower_inverses(a_lows):
    return _inverse_products(a_lows)


def _unit_lower_inverses_fwd(a_lows):
    invs = _inverse_products(a_lows)
    return invs, invs


def _unit_lower_inverses_bwd(invs, gs):
    return (_inverse_cotangents(invs, gs),)


_unit_lower_inverses.defvjp(_unit_lower_inverses_fwd, _unit_lower_inverses_bwd)


@jax.custom_vjp
def _known_inverses(a_lows, invs):
    return invs


def _known_inverses_fwd(a_lows, invs):
    return invs, invs


def _known_inverses_bwd(invs, gs):
    return _inverse_cotangents(invs, gs), _each(jnp.zeros_like, invs)


_known_inverses.defvjp(_known_inverses_fwd, _known_inverses_bwd)


def _params(semantics=None):
    return pltpu.CompilerParams(dimension_semantics=semantics, vmem_limit_bytes=VMEM_LIMIT)


def _silu(x):
    return x * jax.nn.sigmoid(x)


def _group_ones(width, group):
    r = lax.broadcasted_iota(jnp.int32, (width, width), 0) // group
    c = lax.broadcasted_iota(jnp.int32, (width, width), 1) // group
    return (r == c).astype(f32)


def _slab_sums(x):
    rows, width = x.shape
    sums = [jnp.sum(x[:, j:j + 128], axis=1, keepdims=True) for j in range(0, width, 128)]
    return jnp.concatenate([jnp.broadcast_to(s, (rows, 128)) for s in sums], axis=1)


def _hnorm(x, nw, scale, shift):
    xn = x * lax.rsqrt(jnp.mean(x * x, axis=-1, keepdims=True) + EPS)
    return xn * nw * (1.0 + scale) + shift


def _rope_tables(pos_col, invf_row, sign_row):
    ang = pos_col.astype(f32) * invf_row
    cos_t = jnp.cos(ang)
    sin_t = jnp.sin(ang) * sign_row
    return jnp.concatenate([cos_t] * 4, axis=1), jnp.concatenate([sin_t] * 4, axis=1)


def _rope_partner(x):
    lane = lax.broadcasted_iota(jnp.int32, x.shape, 1)
    width = x.shape[1]
    return jnp.where((lane % AT_DIM) < AT_DIM // 2, pltpu.roll(x, width - AT_DIM // 2, 1), pltpu.roll(x, AT_DIM // 2, 1))


TOKEN_TILE = 256
LANES = 128


def _stage_lanes(stage_ref, first, value):
    for j in range(value.shape[1] // LANES):
        stage_ref[first + j] = value[:, LANES * j:LANES * (j + 1)]


def _stage_to_view(stage_ref, view_ref, dil):
    chunks, rows, _ = stage_ref.shape
    for r in range(dil):
        for j in range(chunks):
            col = (r * chunks + j) * LANES
            view_ref[:, col:col + LANES] = stage_ref.at[j][pl.ds(r, rows // dil, stride=dil), :].astype(view_ref.dtype)


def _view_to_value(stage_ref, view_ref, dil):
    chunks, rows, _ = stage_ref.shape
    for r in range(dil):
        for j in range(chunks):
            col = (r * chunks + j) * LANES
            stage_ref.at[j][pl.ds(r, rows // dil, stride=dil), :] = view_ref[:, col:col + LANES].astype(f32)
    return jnp.concatenate([stage_ref[j] for j in range(chunks)], axis=1)


def _view_spec(width, dil):
    return pl.BlockSpec((TOKEN_TILE // dil, dil * width), lambda i: (i, 0))


def _view_shape(seq, width, dil, dtype=f32):
    return jax.ShapeDtypeStruct((seq // dil, dil * width), dtype)


HEAD_LANES = 128


def _repeat_heads(compact):
    row = lax.broadcasted_iota(jnp.int32, (HEAD_LANES, AT_WIDTH), 0)
    col = lax.broadcasted_iota(jnp.int32, (HEAD_LANES, AT_WIDTH), 1)
    return _dot_sel((row == col // AT_DIM).astype(f32), compact, NN, 1)


def _stage(width):
    return pltpu.VMEM((width // LANES, TOKEN_TILE, LANES), f32)


def _pre_proj(x, pos_col, invf_row, sign_row, norm_w, scale, shift, w_cat):
    seq = x.shape[0]
    tm = TOKEN_TILE

    def body(x_ref, pos_ref, invf_ref, sign_ref, nw_ref, sc_ref, sh_ref, w_ref,
             pdn_ref, pat1_ref, pat4_ref, pat16_ref, zat_ref, h_ref, stage_ref):
        h = _hnorm(x_ref[...], nw_ref[...], sc_ref[...], sh_ref[...]).astype(bf16)
        h_ref[...] = h
        big = jnp.dot(h, w_ref[...], preferred_element_type=f32)
        pdn_ref[...] = big[:, :PDN_W]
        cos_t, sin_t = _rope_tables(pos_ref[...], invf_ref[...], sign_ref[...])
        q = big[:, PDN_W:PDN_W + AT_WIDTH]
        k = big[:, PDN_W + AT_WIDTH:PDN_W + 2 * AT_WIDTH]
        parts = (q * cos_t + _rope_partner(q) * sin_t, k * cos_t + _rope_partner(k) * sin_t,
                 big[:, PDN_W + 2 * AT_WIDTH:PDN_W + PAT_W])
        for p, part in enumerate(parts):
            pat1_ref[:, p * AT_WIDTH:(p + 1) * AT_WIDTH] = part.astype(bf16)
            _stage_lanes(stage_ref, p * (AT_WIDTH // LANES), part)
        _stage_to_view(stage_ref, pat4_ref, 4)
        _stage_to_view(stage_ref, pat16_ref, 16)
        zat_ref[...] = big[:, PDN_W + PAT_W:]

    row = lambda w: pl.BlockSpec((1, w), lambda i: (0, 0))
    tile = lambda w: pl.BlockSpec((tm, w), lambda i: (i, 0))
    return pl.pallas_call(
        body, name="pre_proj", grid=(seq // tm,),
        in_specs=[tile(D_MODEL), tile(1), row(128), row(128), row(D_MODEL), row(D_MODEL), row(D_MODEL),
                  pl.BlockSpec((D_MODEL, CAT_W), lambda i: (0, 0))],
        out_specs=[tile(PDN_W), tile(PAT_W), _view_spec(PAT_W, 4), _view_spec(PAT_W, 16), tile(AT_WIDTH), tile(D_MODEL)],
        out_shape=[jax.ShapeDtypeStruct((seq, PDN_W), f32), _view_shape(seq, PAT_W, 1, bf16), _view_shape(seq, PAT_W, 4, bf16),
                   _view_shape(seq, PAT_W, 16, bf16), jax.ShapeDtypeStruct((seq, AT_WIDTH), f32),
                   jax.ShapeDtypeStruct((seq, D_MODEL), bf16)],
        scratch_shapes=[_stage(PAT_W)],
        compiler_params=_params(("parallel",)),
    )(x, pos_col, invf_row, sign_row, norm_w, scale, shift, w_cat)


def _conv_taps(ext_ref, halo, cur, w8):
    rows = cur.shape[0]
    ext_ref[0:8, :] = halo
    ext_ref[8:8 + rows, :] = cur
    out = ext_ref[pl.ds(5, rows), :] * w8[0:1, :]
    for j in range(1, CONV_K):
        out = out + ext_ref[pl.ds(5 + j, rows), :] * w8[j:j + 1, :]
    return out


def _dn_pre(cq, ck, cv, ba, a_row, dt_row):
    sq, sk, v = _silu(cq), _silu(ck), _silu(cv)
    qn = sq * lax.rsqrt(_slab_sums(sq * sq) + EPS)
    kn = sk * lax.rsqrt(_slab_sums(sk * sk) + EPS)
    beta_all = jax.nn.sigmoid(ba)
    g_all = -jnp.exp(a_row) * jax.nn.softplus(ba + dt_row)
    return qn, kn, v, beta_all, g_all


def _dn_intra(qs, ks, vs, betas, gs, known_invs=None):
    ri = lax.broadcasted_iota(jnp.int32, (CHUNK, CHUNK), 0)
    ci = lax.broadcasted_iota(jnp.int32, (CHUNK, CHUNK), 1)
    tril = ri >= ci
    strict = ri > ci
    lower = tril.astype(f32)
    each = _each
    g_wide = each(lambda g: jnp.broadcast_to(g, (CHUNK, DN_DIM)), gs)
    gc = each(lambda g: _sel_left(lower, g), g_wide)
    gc_sq = each(lambda c: c[:, :CHUNK], gc)
    g_end = each(lambda c: jnp.broadcast_to(c[CHUNK - 1:CHUNK, :], (CHUNK, DN_DIM)), gc)
    g_end8 = each(lambda c: jnp.broadcast_to(c[CHUNK - 1:CHUNK, :], (8, DN_DIM)), gc)
    decay = each(lambda s: jnp.exp(jnp.where(tril, s - s.T, -jnp.inf)), gc_sq)
    qs = each(lambda q: q * (DN_DIM ** -0.5), qs)
    kb = each(lambda k, b: k * b, ks, betas)
    vb = each(lambda v, b: v * b, vs, betas)
    kk = each(MM.nt, kb, ks)
    qk = each(MM.nt, qs, ks)
    a_low = each(lambda p, d: jnp.where(strict, p * d, 0.0), kk, decay)
    inv = _unit_lower_inverses(a_low) if known_invs is None else _known_inverses(a_low, known_invs)
    e_gc = each(jnp.exp, gc)
    u = each(MS.nn, inv, vb)
    w = each(lambda x, k, e: MS.nn(x, k * e), inv, kb, e_gc)
    attn = each(lambda p, d: jnp.where(tril, p * d, 0.0), qk, decay)
    q_dec = each(lambda q, e: q * e, qs, e_gc)
    k_dec = each(lambda k, ge, c: k * jnp.exp(ge - c), ks, g_end, gc)
    return u, w, q_dec, k_dec, attn, each(jnp.exp, g_end8), inv


def _dn_step(us, ws, q_decs, k_decs, attns, e_ends, states):
    each = _each
    v_new = each(lambda u, w, s: u - MM.nn(w, s), us, ws, states)
    qs = each(MM.nn, q_decs, states)
    o = each(lambda a, b, c: a + MM.nn(b, c), qs, attns, v_new)
    new_states = each(lambda s, e, k, v: s * e + MM.tn(k, v), states, e_ends, k_decs, v_new)
    return o, new_states


INTRA_CHUNKS = 2
SCAN_CHUNKS = 4


def _intra_specs(nc):
    rows = INTRA_CHUNKS * CHUNK
    cur = pl.BlockSpec((rows, QKV_W), lambda i: (i, 0))
    halo = pl.BlockSpec((8, QKV_W), lambda i: (jnp.maximum(i * (rows // 8) - 1, 0), 0))
    ba = pl.BlockSpec((rows, BA_W), lambda i: (i, (QKV_W + DN_WIDTH) // BA_W))
    conv = pl.BlockSpec((8, QKV_W), lambda i: (0, 0))
    row = pl.BlockSpec((1, BA_W), lambda i: (0, 0))
    wide = pl.BlockSpec((rows, DN_WIDTH), lambda i: (i, 0))
    attn = pl.BlockSpec((INTRA_CHUNKS, DN_HEADS, CHUNK, CHUNK), lambda i: (i, 0, 0, 0))
    e_end = pl.BlockSpec((INTRA_CHUNKS, 8 * DN_HEADS, DN_DIM), lambda i: (i, 0, 0))
    return cur, halo, ba, conv, row, wide, attn, e_end


def _intra_items():
    return [(ci, h, slice(ci * CHUNK, (ci + 1) * CHUNK), slice(h * DN_DIM, (h + 1) * DN_DIM))
            for ci in range(INTRA_CHUNKS) for h in range(DN_HEADS)]


def _intra_inputs(items, qn, kn, v, beta_all, g_all):
    return ([qn[rows, lanes] for _, _, rows, lanes in items], [kn[rows, lanes] for _, _, rows, lanes in items],
            [v[rows, lanes] for _, _, rows, lanes in items], [beta_all[rows, h:h + 1] for _, h, rows, _ in items],
            [g_all[rows, DN_HEADS + h:DN_HEADS + h + 1] for _, h, rows, _ in items])


def _intra_shapes(seq, operand_dtype):
    nc = seq // CHUNK
    wide = jax.ShapeDtypeStruct((seq, DN_WIDTH), f32)
    operand = jax.ShapeDtypeStruct((seq, DN_WIDTH), operand_dtype)
    return [wide, operand, operand, operand, jax.ShapeDtypeStruct((nc, DN_HEADS, CHUNK, CHUNK), f32),
            jax.ShapeDtypeStruct((nc, 8 * DN_HEADS, DN_DIM), f32)]


def _dn_intra_forward(pdn, conv_w8, a_row, dt_row):
    seq = pdn.shape[0]
    nc = seq // CHUNK
    cur, halo, ba, conv, row, wide, attn, e_end = _intra_specs(nc)

    def body(cur_ref, halo_ref, ba_ref, w_ref, a_ref, dt_ref, u_ref, w_out_ref, qd_ref, kd_ref, attn_ref, e_ref, inv_ref,
             ext_ref):
        halo_rows = jnp.where(pl.program_id(0) > 0, halo_ref[...], 0.0)
        c = _conv_taps(ext_ref, halo_rows, cur_ref[...], w_ref[...])
        qn, kn, v, beta_all, g_all = _dn_pre(c[:, :DN_WIDTH], c[:, DN_WIDTH:2 * DN_WIDTH], c[:, 2 * DN_WIDTH:],
                                             ba_ref[...], a_ref[...], dt_ref[...])
        items = _intra_items()
        u, w, qd, kd, at, e8, inv = _dn_intra(*_intra_inputs(items, qn, kn, v, beta_all, g_all))
        for n, (ci, h, rows, lanes) in enumerate(items):
            u_ref[rows, lanes] = u[n]
            w_out_ref[rows, lanes] = w[n].astype(bf16)
            qd_ref[rows, lanes] = qd[n].astype(bf16)
            kd_ref[rows, lanes] = kd[n].astype(bf16)
            attn_ref[ci, h] = at[n]
            e_ref[ci, 8 * h:8 * h + 8, :] = e8[n]
            inv_ref[ci, h] = inv[n]

    return pl.pallas_call(
        body, name="dn_intra_forward", grid=(nc // INTRA_CHUNKS,),
        in_specs=[cur, halo, ba, conv, row, row],
        out_specs=[wide, wide, wide, wide, attn, e_end, attn],
        out_shape=_intra_shapes(seq, bf16) + [jax.ShapeDtypeStruct((nc, DN_HEADS, CHUNK, CHUNK), f32)],
        scratch_shapes=[pltpu.VMEM((INTRA_CHUNKS * CHUNK + 8, QKV_W), f32)],
        compiler_params=_params(("parallel",)),
    )(pdn, pdn, pdn, conv_w8, a_row, dt_row)


def _scan_specs(nc, reverse):
    steps = nc // SCAN_CHUNKS
    at = (lambda i: steps - 1 - i) if reverse else (lambda i: i)
    wide = pl.BlockSpec((SCAN_CHUNKS * CHUNK, DN_WIDTH), lambda i: (at(i), 0))
    attn = pl.BlockSpec((SCAN_CHUNKS, DN_HEADS, CHUNK, CHUNK), lambda i: (at(i), 0, 0, 0))
    e_end = pl.BlockSpec((SCAN_CHUNKS, 8 * DN_HEADS, DN_DIM), lambda i: (at(i), 0, 0))
    states = pl.BlockSpec((SCAN_CHUNKS, DN_HEADS, DN_DIM, DN_DIM), lambda i: (at(i), 0, 0, 0))
    return wide, attn, e_end, states


def _step_inputs(ci, rows, lanes, u_ref, w_ref, qd_ref, kd_ref, attn_ref, e_ref):
    heads = range(DN_HEADS)
    return ([u_ref[rows, lanes[h]] for h in heads], [w_ref[rows, lanes[h]].astype(f32) for h in heads],
            [qd_ref[rows, lanes[h]].astype(f32) for h in heads], [kd_ref[rows, lanes[h]].astype(f32) for h in heads],
            [attn_ref[ci, h] for h in heads], [e_ref[ci, 8 * h:8 * h + 1, :] for h in heads])


def _dn_scan_forward(u, w, q_dec, k_dec, attn, e_end):
    seq = u.shape[0]
    nc = seq // CHUNK
    wide, attn_spec, e_spec, st_spec = _scan_specs(nc, reverse=False)

    def body(u_ref, w_ref, qd_ref, kd_ref, attn_ref, e_ref, o_ref, st_ref, state_ref):
        @pl.when(pl.program_id(0) == 0)
        def _():
            state_ref[...] = jnp.zeros_like(state_ref)

        heads = range(DN_HEADS)
        lanes = [slice(h * DN_DIM, (h + 1) * DN_DIM) for h in heads]
        states = [state_ref[h] for h in heads]
        for ci in range(SCAN_CHUNKS):
            rows = slice(ci * CHUNK, (ci + 1) * CHUNK)
            for h in heads:
                st_ref[ci, h] = states[h].astype(bf16)
            o, states = _dn_step(*_step_inputs(ci, rows, lanes, u_ref, w_ref, qd_ref, kd_ref, attn_ref, e_ref), states)
            for h in heads:
                o_ref[rows, lanes[h]] = o[h]
        for h in heads:
            state_ref[h] = states[h]

    return pl.pallas_call(
        body, name="dn_scan_forward", grid=(nc // SCAN_CHUNKS,),
        in_specs=[wide, wide, wide, wide, attn_spec, e_spec],
        out_specs=[wide, st_spec],
        out_shape=[jax.ShapeDtypeStruct((seq, DN_WIDTH), f32), jax.ShapeDtypeStruct((nc, DN_HEADS, DN_DIM, DN_DIM), bf16)],
        scratch_shapes=[pltpu.VMEM((DN_HEADS, DN_DIM, DN_DIM), f32)],
        compiler_params=_params(("arbitrary",)),
    )(u, w, q_dec, k_dec, attn, e_end)


def _dn_scan_backward(u, w, q_dec, k_dec, attn, e_end, states, d_o):
    seq = u.shape[0]
    nc = seq // CHUNK
    wide, attn_spec, e_spec, st_spec = _scan_specs(nc, reverse=True)

    def body(u_ref, w_ref, qd_ref, kd_ref, attn_ref, e_ref, st_ref, do_ref,
             du_ref, dw_ref, dqd_ref, dkd_ref, dattn_ref, de_ref, dstate_ref):
        @pl.when(pl.program_id(0) == 0)
        def _():
            dstate_ref[...] = jnp.zeros_like(dstate_ref)

        heads = range(DN_HEADS)
        lanes = [slice(h * DN_DIM, (h + 1) * DN_DIM) for h in heads]
        first_row = lax.broadcasted_iota(jnp.int32, (8, DN_DIM), 0) == 0
        dstates = [dstate_ref[h] for h in heads]
        for ci in reversed(range(SCAN_CHUNKS)):
            rows = slice(ci * CHUNK, (ci + 1) * CHUNK)
            _, step_vjp = jax.vjp(_dn_step, *_step_inputs(ci, rows, lanes, u_ref, w_ref, qd_ref, kd_ref, attn_ref, e_ref),
                                  [st_ref[ci, h].astype(f32) for h in heads])
            du, dw, dqd, dkd, dattn, de, dstates = step_vjp(([do_ref[rows, lanes[h]] for h in heads], dstates))
            for h in heads:
                du_ref[rows, lanes[h]] = du[h]
                dw_ref[rows, lanes[h]] = dw[h]
                dqd_ref[rows, lanes[h]] = dqd[h]
                dkd_ref[rows, lanes[h]] = dkd[h]
                dattn_ref[ci, h] = dattn[h]
                de_ref[ci, 8 * h:8 * h + 8, :] = jnp.where(first_row, jnp.broadcast_to(de[h], (8, DN_DIM)), 0.0)
        for h in heads:
            dstate_ref[h] = dstates[h]

    return pl.pallas_call(
        body, name="dn_scan_backward", grid=(nc // SCAN_CHUNKS,),
        in_specs=[wide, wide, wide, wide, attn_spec, e_spec, st_spec, wide],
        out_specs=[wide, wide, wide, wide, attn_spec, e_spec],
        out_shape=_intra_shapes(seq, f32),
        scratch_shapes=[pltpu.VMEM((DN_HEADS, DN_DIM, DN_DIM), f32)],
        compiler_params=_params(("arbitrary",)),
    )(u, w, q_dec, k_dec, attn, e_end, states, d_o)


def _dn_intra_backward(pdn, conv_w8, a_row, dt_row, invs, d_u, d_w, d_qd, d_kd, d_attn, d_e):
    seq = pdn.shape[0]
    nc = seq // CHUNK
    rows_per_step = INTRA_CHUNKS * CHUNK
    cur, halo, ba, conv, row, wide, attn, e_end = _intra_specs(nc)

    def body(cur_ref, halo_ref, ba_ref, w_ref, a_ref, dt_ref, inv_ref, du_ref, dw_ref, dqd_ref, dkd_ref, dattn_ref, de_ref,
             dconv_ref, dba_ref, drow_ref, ext_ref):
        @pl.when(pl.program_id(0) == 0)
        def _():
            drow_ref[...] = jnp.zeros_like(drow_ref)

        halo_rows = jnp.where(pl.program_id(0) > 0, halo_ref[...], 0.0)
        c = _conv_taps(ext_ref, halo_rows, cur_ref[...], w_ref[...])
        (qn, kn, v, beta_all, g_all), pre_vjp = jax.vjp(
            _dn_pre, c[:, :DN_WIDTH], c[:, DN_WIDTH:2 * DN_WIDTH], c[:, 2 * DN_WIDTH:], ba_ref[...], a_ref[...], dt_ref[...])
        lane = lax.broadcasted_iota(jnp.int32, (CHUNK, BA_W), 1)
        items = _intra_items()
        _, intra_vjp = jax.vjp(_dn_intra, *_intra_inputs(items, qn, kn, v, beta_all, g_all),
                               [inv_ref[ci, h] for ci, h, _, _ in items])
        dq, dk, dv, dbeta, dg, _ = intra_vjp((
            [du_ref[rows, lanes] for _, _, rows, lanes in items], [dw_ref[rows, lanes] for _, _, rows, lanes in items],
            [dqd_ref[rows, lanes] for _, _, rows, lanes in items], [dkd_ref[rows, lanes] for _, _, rows, lanes in items],
            [dattn_ref[ci, h] for ci, h, _, _ in items], [de_ref[ci, 8 * h:8 * h + 8, :] for ci, h, _, _ in items],
            [jnp.zeros((CHUNK, CHUNK), f32) for _ in items]))
        dq_rows, dk_rows, dv_rows, dbeta_rows, dg_rows = [], [], [], [], []
        for ci in range(INTRA_CHUNKS):
            of_chunk = [n for n, item in enumerate(items) if item[0] == ci]
            d_beta_all = jnp.zeros((CHUNK, BA_W), f32)
            d_g_all = jnp.zeros((CHUNK, BA_W), f32)
            for n in of_chunk:
                h = items[n][1]
                d_beta_all = d_beta_all + jnp.where(lane == h, dbeta[n], 0.0)
                d_g_all = d_g_all + jnp.where(lane == DN_HEADS + h, dg[n], 0.0)
            dq_rows.append(jnp.concatenate([dq[n] for n in of_chunk], axis=1))
            dk_rows.append(jnp.concatenate([dk[n] for n in of_chunk], axis=1))
            dv_rows.append(jnp.concatenate([dv[n] for n in of_chunk], axis=1))
            dbeta_rows.append(d_beta_all)
            dg_rows.append(d_g_all)
        stack = lambda parts: jnp.concatenate(parts, axis=0)
        dcq, dck, dcv, dba, da_row, ddt_row = pre_vjp(
            (stack(dq_rows), stack(dk_rows), stack(dv_rows), stack(dbeta_rows), stack(dg_rows)))
        dconv_ref[:, :DN_WIDTH] = dcq
        dconv_ref[:, DN_WIDTH:2 * DN_WIDTH] = dck
        dconv_ref[:, 2 * DN_WIDTH:] = dcv
        dba_ref[...] = dba
        drow_ref[0:1, :] += da_row
        drow_ref[1:2, :] += ddt_row

    return pl.pallas_call(
        body, name="dn_intra_backward", grid=(nc // INTRA_CHUNKS,),
        in_specs=[cur, halo, ba, conv, row, row, attn, wide, wide, wide, wide, attn, e_end],
        out_specs=[pl.BlockSpec((rows_per_step, QKV_W), lambda i: (i, 0)),
                   pl.BlockSpec((rows_per_step, BA_W), lambda i: (i, 0)),
                   pl.BlockSpec((8, BA_W), lambda i: (0, 0))],
        out_shape=[jax.ShapeDtypeStruct((seq, QKV_W), f32), jax.ShapeDtypeStruct((seq, BA_W), f32),
                   jax.ShapeDtypeStruct((8, BA_W), f32)],
        scratch_shapes=[pltpu.VMEM((rows_per_step + 8, QKV_W), f32)],
        compiler_params=_params(("arbitrary",)),
    )(pdn, pdn, pdn, conv_w8, a_row, dt_row, invs, d_u, d_w, d_qd, d_kd, d_attn, d_e)


def _band_masks(block_index):
    qi = lax.broadcasted_iota(jnp.int32, (Q_BLOCK, Q_BLOCK), 0)
    kj = lax.broadcasted_iota(jnp.int32, (Q_BLOCK, Q_BLOCK), 1)
    return (kj >= qi) & (block_index > 0), kj <= qi


def _low_half():
    return lax.broadcasted_iota(jnp.int32, (Q_BLOCK, LANES), 1) < AT_DIM


def _head_pairs(ref, split):
    low = _low_half()
    slabs = [ref[:, pair * LANES:(pair + 1) * LANES].astype(bf16) for pair in range(AT_HEADS // 2)]
    if not split:
        return slabs
    zero = jnp.zeros((Q_BLOCK, LANES), bf16)
    return [jnp.where(low, slab, zero) if h == 0 else jnp.where(low, zero, slab) for slab in slabs for h in range(2)]


def _stack_blocks(first, second):
    return [jnp.concatenate([a, b], axis=0) for a, b in zip(first, second)]


def _attn_forward(pat_view, dil):
    length = pat_view.shape[0]
    nb = length // Q_BLOCK
    scale = AT_DIM ** -0.5

    def body(q_ref, kp_ref, kc_ref, vp_ref, vc_ref, o_ref, lse_ref):
        mask = jnp.concatenate(_band_masks(pl.program_id(1)), axis=1)
        low = _low_half()
        heads = range(AT_HEADS)
        q = _head_pairs(q_ref, split=True)
        k = _stack_blocks(_head_pairs(kp_ref, split=False), _head_pairs(kc_ref, split=False))
        v = _stack_blocks(_head_pairs(vp_ref, split=False), _head_pairs(vc_ref, split=False))
        s = [_dot(q[h], k[h // 2], NT, False) for h in heads]
        p, top = [], []
        for h in heads:
            masked = jnp.where(mask, s[h] * scale, NEG)
            m = jnp.max(masked, axis=1, keepdims=True)
            p.append(jnp.exp(masked - m).astype(bf16))
            top.append(m)
        ones = jnp.ones((2 * Q_BLOCK, LANES), bf16)
        l = [_dot(p[h], ones, NN, False) for h in heads]
        o = [_dot(p[h], v[h // 2], NN, False) for h in heads]
        for pair in range(AT_HEADS // 2):
            even, odd = 2 * pair, 2 * pair + 1
            slab = slice(pair * LANES, (pair + 1) * LANES)
            o_ref[:, slab] = jnp.where(low, o[even], o[odd]) / jnp.where(low, l[even], l[odd])
        lane = lax.broadcasted_iota(jnp.int32, (Q_BLOCK, HEAD_LANES), 1)
        lse = jnp.zeros((Q_BLOCK, HEAD_LANES), f32)
        for h in heads:
            lse = jnp.where(lane == h, top[h] + jnp.log(l[h]), lse)
        lse_ref[...] = lse

    def blk(piece, prev):
        if prev:
            return pl.BlockSpec((Q_BLOCK, AT_WIDTH), lambda r, n: (jnp.maximum(n - 1, 0), 3 * r + piece))
        return pl.BlockSpec((Q_BLOCK, AT_WIDTH), lambda r, n: (n, 3 * r + piece))

    out = pl.BlockSpec((Q_BLOCK, AT_WIDTH), lambda r, n: (n, r))
    return pl.pallas_call(
        body, name=f"attn_forward_d{dil}", grid=(dil, nb),
        in_specs=[blk(0, False), blk(1, True), blk(1, False), blk(2, True), blk(2, False)],
        out_specs=[out, pl.BlockSpec((Q_BLOCK, HEAD_LANES), lambda r, n: (n, r))],
        out_shape=[jax.ShapeDtypeStruct((length, dil * AT_WIDTH), f32),
                   jax.ShapeDtypeStruct((length, dil * HEAD_LANES), f32)],
        compiler_params=_params(("parallel", "parallel")),
    )(pat_view, pat_view, pat_view, pat_view, pat_view)


def _attn_backward_q(pat_view, d_out, delta, lse, dil):
    length = pat_view.shape[0]
    nb = length // Q_BLOCK
    scale = AT_DIM ** -0.5

    def body(q_ref, kp_ref, kc_ref, vp_ref, vc_ref, do_ref, dl_ref, lse_ref, dq_ref):
        mask = jnp.concatenate(_band_masks(pl.program_id(1)), axis=1)
        low = _low_half()
        heads = range(AT_HEADS)
        q, do = _head_pairs(q_ref, split=True), _head_pairs(do_ref, split=True)
        k = _stack_blocks(_head_pairs(kp_ref, split=False), _head_pairs(kc_ref, split=False))
        v = _stack_blocks(_head_pairs(vp_ref, split=False), _head_pairs(vc_ref, split=False))
        s = [_dot(q[h], k[h // 2], NT, False) for h in heads]
        dp = [_dot(do[h], v[h // 2], NT, False) for h in heads]
        ds = []
        for h in heads:
            p = jnp.exp(jnp.where(mask, s[h] * scale - lse_ref[:, h:h + 1], NEG))
            ds.append((p * (dp[h] - dl_ref[:, h:h + 1])).astype(bf16))
        dq = [_dot(ds[h], k[h // 2], NN, False) for h in heads]
        for pair in range(AT_HEADS // 2):
            dq_ref[:, pair * LANES:(pair + 1) * LANES] = (
                jnp.where(low, dq[2 * pair], dq[2 * pair + 1]) * scale).astype(bf16)

    def blk(piece, prev):
        if prev:
            return pl.BlockSpec((Q_BLOCK, AT_WIDTH), lambda r, n: (jnp.maximum(n - 1, 0), 3 * r + piece))
        return pl.BlockSpec((Q_BLOCK, AT_WIDTH), lambda r, n: (n, 3 * r + piece))

    one = pl.BlockSpec((Q_BLOCK, AT_WIDTH), lambda r, n: (n, r))
    compact = pl.BlockSpec((Q_BLOCK, HEAD_LANES), lambda r, n: (n, r))
    return pl.pallas_call(
        body, name=f"attn_backward_q_d{dil}", grid=(dil, nb),
        in_specs=[blk(0, False), blk(1, True), blk(1, False), blk(2, True), blk(2, False), one, compact, compact],
        out_specs=one,
        out_shape=jax.ShapeDtypeStruct((length, dil * AT_WIDTH), bf16),
        compiler_params=_params(("parallel", "parallel")),
    )(pat_view, pat_view, pat_view, pat_view, pat_view, d_out, delta, lse)


def _attn_backward_kv(pat_view, d_out, delta, lse, dil):
    length = pat_view.shape[0]
    nb = length // Q_BLOCK
    scale = AT_DIM ** -0.5

    def body(k_ref, v_ref, qa_ref, qb_ref, doa_ref, dob_ref, dla_ref, dlb_ref, lsea_ref, lseb_ref, dk_ref, dv_ref):
        j = pl.program_id(1)
        kj = lax.broadcasted_iota(jnp.int32, (Q_BLOCK, Q_BLOCK), 0)
        qi = lax.broadcasted_iota(jnp.int32, (Q_BLOCK, Q_BLOCK), 1)
        mask = jnp.concatenate([kj <= qi,
                                (kj >= qi) & (j + 1 < nb)],
                               axis=1)
        low = _low_half()
        heads = range(AT_HEADS)
        k, v = _head_pairs(k_ref, split=True), _head_pairs(v_ref, split=True)
        q = _stack_blocks(_head_pairs(qa_ref, split=False), _head_pairs(qb_ref, split=False))
        do = _stack_blocks(_head_pairs(doa_ref, split=False), _head_pairs(dob_ref, split=False))
        row = lax.broadcasted_iota(jnp.int32, (8 * AT_HEADS, HEAD_LANES), 0)
        col = lax.broadcasted_iota(jnp.int32, (8 * AT_HEADS, HEAD_LANES), 1)
        pick = (col == row // 8).astype(f32)
        lse_rows = _dot_sel(pick, jnp.concatenate([lsea_ref[...], lseb_ref[...]], axis=0), NT, 0)
        dl_rows = _dot_sel(pick, jnp.concatenate([dla_ref[...], dlb_ref[...]], axis=0), NT, 0)
        s_t = [_dot(k[h], q[h // 2], NT, False) for h in heads]
        dp_t = [_dot(v[h], do[h // 2], NT, False) for h in heads]
        p_t, ds_t = [], []
        for h in heads:
            prob = jnp.exp(jnp.where(mask, s_t[h] * scale - lse_rows[8 * h:8 * h + 1, :], NEG))
            p_t.append(prob.astype(bf16))
            ds_t.append((prob * (dp_t[h] - dl_rows[8 * h:8 * h + 1, :])).astype(bf16))
        dv = [_dot(p_t[h], do[h // 2], NN, False) for h in heads]
        dk = [_dot(ds_t[h], q[h // 2], NN, False) for h in heads]
        for pair in range(AT_HEADS // 2):
            slab = slice(pair * LANES, (pair + 1) * LANES)
            dk_ref[:, slab] = (jnp.where(low, dk[2 * pair], dk[2 * pair + 1]) * scale).astype(bf16)
            dv_ref[:, slab] = jnp.where(low, dv[2 * pair], dv[2 * pair + 1]).astype(bf16)

    def kv(piece):
        return pl.BlockSpec((Q_BLOCK, AT_WIDTH), lambda r, n: (n, 3 * r + piece))

    q_same = pl.BlockSpec((Q_BLOCK, AT_WIDTH), lambda r, n: (n, 3 * r))
    q_next = pl.BlockSpec((Q_BLOCK, AT_WIDTH), lambda r, n: (jnp.minimum(n + 1, nb - 1), 3 * r))
    same = pl.BlockSpec((Q_BLOCK, AT_WIDTH), lambda r, n: (n, r))
    nxt = pl.BlockSpec((Q_BLOCK, AT_WIDTH), lambda r, n: (jnp.minimum(n + 1, nb - 1), r))
    c_same = pl.BlockSpec((Q_BLOCK, HEAD_LANES), lambda r, n: (n, r))
    c_nxt = pl.BlockSpec((Q_BLOCK, HEAD_LANES), lambda r, n: (jnp.minimum(n + 1, nb - 1), r))
    return pl.pallas_call(
        body, name=f"attn_backward_kv_d{dil}", grid=(dil, nb),
        in_specs=[kv(1), kv(2), q_same, q_next, same, nxt, c_same, c_nxt, c_same, c_nxt],
        out_specs=[same, same],
        out_shape=[jax.ShapeDtypeStruct((length, dil * AT_WIDTH), bf16)] * 2,
        compiler_params=_params(("parallel", "parallel")),
    )(pat_view, pat_view, pat_view, pat_view, d_out, d_out, delta, delta, lse, lse)


def _gated_norms(o_dn, z_dn, o_at, z_at, dn_w, at_w):
    ms_dn = _slab_sums(o_dn * o_dn) * (1.0 / DN_DIM)
    a = o_dn * lax.rsqrt(ms_dn + EPS) * dn_w * _silu(z_dn)
    head_of_lane = (lax.broadcasted_iota(jnp.int32, (AT_WIDTH, HEAD_LANES), 0) // AT_DIM
                    == lax.broadcasted_iota(jnp.int32, (AT_WIDTH, HEAD_LANES), 1)).astype(f32)
    lanes_of_head = (lax.broadcasted_iota(jnp.int32, (HEAD_LANES, AT_WIDTH), 0)
                     == lax.broadcasted_iota(jnp.int32, (HEAD_LANES, AT_WIDTH), 1) // AT_DIM).astype(f32)
    ms_at = _sel_right(_sel_right(o_at * o_at, head_of_lane), lanes_of_head) * (1.0 / AT_DIM)
    b = o_at * lax.rsqrt(ms_at + EPS) * at_w * _silu(z_at)
    return a, b


def _residual_loss(x, mix, gate, fin_w, target):
    x2 = x + gate * mix
    y = x2 * lax.rsqrt(jnp.mean(x2 * x2, axis=-1, keepdims=True) + EPS) * fin_w
    err = y - target
    per_token = jnp.sum(err * err, axis=1, keepdims=True) * (1.0 / D_MODEL)
    return 0.5 * jnp.sum(per_token, axis=0, keepdims=True)


TAIL_ROWS = 8


def _tail(x, target, o_dn, pdn, outs, lses, z_at, gate, w_out, dn_w_row, at_w_row, fin_w):
    seq = x.shape[0]
    tm = TOKEN_TILE

    def body(x_ref, t_ref, odn_ref, zdn_ref, o1_ref, o4_ref, o16_ref, l1_ref, l4_ref, l16_ref, zat_ref, gate_ref,
             wt_ref, wb_ref, dnw_ref, atw_ref, fw_ref,
             dx2_ref, dodn_ref, dzdn_ref, doat1_ref, doat4_ref, doat16_ref, delta1_ref, delta4_ref, delta16_ref,
             lse1_ref, lse4_ref, lse16_ref, dzat_ref, gwt_ref, gwb_ref, rows_ref, *stages):
        @pl.when(pl.program_id(0) == 0)
        def _():
            gwt_ref[...] = jnp.zeros_like(gwt_ref)
            gwb_ref[...] = jnp.zeros_like(gwb_ref)
            rows_ref[...] = jnp.zeros_like(rows_ref)

        l1, l4, l16 = l1_ref[...], _view_to_value(stages[0], l4_ref, 4), _view_to_value(stages[1], l16_ref, 16)
        o4, o16 = _view_to_value(stages[2], o4_ref, 4), _view_to_value(stages[3], o16_ref, 16)
        top = jnp.maximum(jnp.maximum(l1, l4), l16)
        e1, e4, e16 = jnp.exp(l1 - top), jnp.exp(l4 - top), jnp.exp(l16 - top)
        den = e1 + e4 + e16
        lse = top + jnp.log(den)
        o_at = (_repeat_heads(e1 / den) * o1_ref[...] + _repeat_heads(e4 / den) * o4 + _repeat_heads(e16 / den) * o16)

        (a, b), norms_vjp = jax.vjp(_gated_norms, odn_ref[...], zdn_ref[...], o_at, zat_ref[...], dnw_ref[...], atw_ref[...])
        a, b = a.astype(bf16), b.astype(bf16)
        mix = jnp.dot(a, wt_ref[...], preferred_element_type=f32) + jnp.dot(b, wb_ref[...], preferred_element_type=f32)
        loss, loss_vjp = jax.vjp(_residual_loss, x_ref[...], mix, gate_ref[...], fw_ref[...], t_ref[...])
        dx2, dmix, dgate, dfw, _ = loss_vjp(jnp.ones((1, 1), f32))
        dmix = dmix.astype(bf16)
        dwt = lax.dot_general(a, dmix, (TN, ((), ())), preferred_element_type=f32)
        dwb = lax.dot_general(b, dmix, (TN, ((), ())), preferred_element_type=f32)
        da = lax.dot_general(dmix, wt_ref[...], (NT, ((), ())), preferred_element_type=f32)
        db = lax.dot_general(dmix, wb_ref[...], (NT, ((), ())), preferred_element_type=f32)
        dodn, dzdn, doat, dzat, ddnw, datw = norms_vjp((da, db))
        dx2_ref[...] = dx2
        dodn_ref[...] = dodn
        dzdn_ref[...] = dzdn
        dzat_ref[...] = dzat
        lane_head = (lax.broadcasted_iota(jnp.int32, (AT_WIDTH, HEAD_LANES), 0) // AT_DIM
                     == lax.broadcasted_iota(jnp.int32, (AT_WIDTH, HEAD_LANES), 1))
        delta = _dot_sel(lane_head.astype(f32), doat * o_at, NN, 1)
        for stage_ref, value, refs in ((stages[2], doat, (doat1_ref, doat4_ref, doat16_ref)),
                                       (stages[0], delta, (delta1_ref, delta4_ref, delta16_ref)),
                                       (stages[1], lse, (lse1_ref, lse4_ref, lse16_ref))):
            refs[0][...] = value.astype(refs[0].dtype)
            _stage_lanes(stage_ref, 0, value)
            _stage_to_view(stage_ref, refs[1], 4)
            _stage_to_view(stage_ref, refs[2], 16)
        gwt_ref[...] += dwt
        gwb_ref[...] += dwb
        rows_ref[0:1, :] += jnp.broadcast_to(loss, (1, D_MODEL))
        rows_ref[1:2, :] += dgate
        rows_ref[2:3, :] += dfw
        rows_ref[3:4, 0:DN_WIDTH] += ddnw
        rows_ref[4:5, 0:AT_WIDTH] += datw

    tile = lambda w: pl.BlockSpec((tm, w), lambda i: (i, 0))
    row = lambda w: pl.BlockSpec((1, w), lambda i: (0, 0))
    half_w = pl.BlockSpec((DN_WIDTH, D_MODEL), lambda i: (0, 0))
    sds = lambda w: jax.ShapeDtypeStruct((seq, w), f32)
    views = [_view_spec(AT_WIDTH, d) for d in PATTERN_DILATIONS]
    compact = [_view_spec(HEAD_LANES, d) for d in PATTERN_DILATIONS]
    doat_shapes = [_view_shape(seq, AT_WIDTH, d, bf16) for d in PATTERN_DILATIONS]
    compact_shapes = [_view_shape(seq, HEAD_LANES, d) for d in PATTERN_DILATIONS]
    return pl.pallas_call(
        body, name="tail", grid=(seq // tm,),
        in_specs=[tile(D_MODEL), tile(D_MODEL), tile(DN_WIDTH),
                  pl.BlockSpec((tm, DN_WIDTH), lambda i: (i, QKV_W // DN_WIDTH)),
                  *views, *compact, tile(AT_WIDTH), row(D_MODEL), half_w, pl.BlockSpec((AT_WIDTH, D_MODEL), lambda i: (1, 0)),
                  row(DN_WIDTH), row(AT_WIDTH), row(D_MODEL)],
        out_specs=[tile(D_MODEL), tile(DN_WIDTH), tile(DN_WIDTH), *views, *compact, *compact,
                   tile(AT_WIDTH), half_w, half_w, pl.BlockSpec((TAIL_ROWS, D_MODEL), lambda i: (0, 0))],
        out_shape=[sds(D_MODEL), sds(DN_WIDTH), sds(DN_WIDTH), *doat_shapes, *compact_shapes, *compact_shapes,
                   sds(AT_WIDTH), jax.ShapeDtypeStruct((DN_WIDTH, D_MODEL), f32),
                   jax.ShapeDtypeStruct((AT_WIDTH, D_MODEL), f32), jax.ShapeDtypeStruct((TAIL_ROWS, D_MODEL), f32)],
        scratch_shapes=[_stage(HEAD_LANES)] * 2 + [_stage(AT_WIDTH)] * 2,
        compiler_params=_params(("arbitrary",)),
    )(x, target, o_dn, pdn, outs[0], outs[1], outs[2], lses[0], lses[1], lses[2], z_at, gate, w_out, w_out,
      dn_w_row, at_w_row, fin_w)


PRE_ROWS = 8


def _pre_backward(x, dx2, pos_col, invf_row, sign_row, norm_w, scale, shift, w_cat, conv_w8, pdn, d_conv, d_zdn, d_ba,
                  d_q, d_k, d_v, d_zat):
    seq = x.shape[0]
    tm = TOKEN_TILE
    last = seq // tm - 1

    def body(x_ref, dx2_ref, pos_ref, invf_ref, sign_ref, nw_ref, sc_ref, sh_ref, w_ref, cw_ref,
             pre_ref, prehalo_ref, dc_ref, dchalo_ref, dz_ref, dba_ref,
             dq1_ref, dq4_ref, dq16_ref, dk1_ref, dk4_ref, dk16_ref, dv1_ref, dv4_ref, dv16_ref, dzat_ref,
             gx_ref, dproj_ref, rows_ref, crow_ref, ext_ref, *stages):
        i = pl.program_id(0)

        @pl.when(i == 0)
        def _():
            rows_ref[...] = jnp.zeros_like(rows_ref)
            crow_ref[...] = jnp.zeros_like(crow_ref)

        dc = dc_ref[...]
        ext_ref[0:tm, :] = dc
        ext_ref[tm:tm + 8, :] = jnp.where(i < last, dchalo_ref[...], 0.0)
        w8 = cw_ref[...]
        d_pre = ext_ref[pl.ds(3, tm), :] * w8[0:1, :]
        for j in range(1, CONV_K):
            d_pre = d_pre + ext_ref[pl.ds(3 - j, tm), :] * w8[j:j + 1, :]
        ext_ref[0:8, :] = jnp.where(i > 0, prehalo_ref[...], 0.0)
        ext_ref[8:8 + tm, :] = pre_ref[...]
        for j in range(CONV_K):
            crow_ref[j:j + 1, :] += jnp.sum(dc * ext_ref[pl.ds(5 + j, tm), :], axis=0, keepdims=True)

        cos_t, sin_t = _rope_tables(pos_ref[...], invf_ref[...], sign_ref[...])
        dq = dq1_ref[...] + _view_to_value(stages[0], dq4_ref, 4) + _view_to_value(stages[1], dq16_ref, 16)
        dk = dk1_ref[...] + _view_to_value(stages[2], dk4_ref, 4) + _view_to_value(stages[3], dk16_ref, 16)
        dq = dq * cos_t + _rope_partner(dq * sin_t)
        dk = dk * cos_t + _rope_partner(dk * sin_t)
        dv = dv1_ref[...] + _view_to_value(stages[4], dv4_ref, 4) + _view_to_value(stages[5], dv16_ref, 16)
        d_proj = jnp.concatenate([d_pre, dz_ref[...], dba_ref[...], dq, dk, dv, dzat_ref[...]], axis=1).astype(bf16)
        dproj_ref[...] = d_proj
        dh = lax.dot_general(d_proj, w_ref[...], (NT, ((), ())), preferred_element_type=f32)
        _, vjp = jax.vjp(_hnorm, x_ref[...], nw_ref[...], sc_ref[...], sh_ref[...])
        dx, dnw, dsc, dsh = vjp(dh)
        gx_ref[...] = dx + dx2_ref[...]
        rows_ref[0:1, :] += dnw
        rows_ref[1:2, :] += dsc
        rows_ref[2:3, :] += dsh

    tile = lambda w: pl.BlockSpec((tm, w), lambda i: (i, 0))
    row = lambda w: pl.BlockSpec((1, w), lambda i: (0, 0))
    step8 = tm // 8
    return pl.pallas_call(
        body, name="pre_backward", grid=(seq // tm,),
        in_specs=[tile(D_MODEL), tile(D_MODEL), tile(1), row(128), row(128), row(D_MODEL), row(D_MODEL), row(D_MODEL),
                  pl.BlockSpec((D_MODEL, CAT_W), lambda i: (0, 0)), pl.BlockSpec((8, QKV_W), lambda i: (0, 0)),
                  tile(QKV_W), pl.BlockSpec((8, QKV_W), lambda i: (jnp.maximum(i * step8 - 1, 0), 0)),
                  tile(QKV_W), pl.BlockSpec((8, QKV_W), lambda i: (jnp.minimum((i + 1) * step8, seq // 8 - 1), 0)),
                  tile(DN_WIDTH), tile(BA_W)] + [_view_spec(AT_WIDTH, d) for d in PATTERN_DILATIONS] * 3 + [tile(AT_WIDTH)],
        out_specs=[tile(D_MODEL), tile(CAT_W), pl.BlockSpec((PRE_ROWS, D_MODEL), lambda i: (0, 0)),
                   pl.BlockSpec((8, QKV_W), lambda i: (0, 0))],
        out_shape=[jax.ShapeDtypeStruct((seq, D_MODEL), f32), jax.ShapeDtypeStruct((seq, CAT_W), bf16),
                   jax.ShapeDtypeStruct((PRE_ROWS, D_MODEL), f32), jax.ShapeDtypeStruct((8, QKV_W), f32)],
        scratch_shapes=[pltpu.VMEM((tm + 8, QKV_W), f32)] + [_stage(AT_WIDTH)] * 6,
        compiler_params=_params(("arbitrary",)),
    )(x, dx2, pos_col, invf_row, sign_row, norm_w, scale, shift, w_cat, conv_w8, pdn, pdn, d_conv, d_conv, d_zdn, d_ba,
      d_q[0], d_q[1], d_q[2], d_k[0], d_k[1], d_k[2], d_v[0], d_v[1], d_v[2], d_zat)


def _weight_grad(h, d_proj):
    seq = h.shape[0]
    tk, tn = 1024, 384
    n_k = seq // tk

    def body(h_ref, d_ref, o_ref):
        @pl.when(pl.program_id(1) == 0)
        def _():
            o_ref[...] = jnp.zeros_like(o_ref)

        o_ref[...] += lax.dot_general(h_ref[...], d_ref[...], (TN, ((), ())), preferred_element_type=f32)

    return pl.pallas_call(
        body, name="weight_grad", grid=(CAT_W // tn, n_k),
        in_specs=[pl.BlockSpec((tk, D_MODEL), lambda n, k: (k, 0)), pl.BlockSpec((tk, tn), lambda n, k: (k, n))],
        out_specs=pl.BlockSpec((D_MODEL, tn), lambda n, k: (0, n)),
        out_shape=jax.ShapeDtypeStruct((D_MODEL, CAT_W), f32),
        compiler_params=_params(("parallel", "arbitrary")),
    )(h, d_proj)


def _adamw(w, g, m, v):
    m = ADAM_B1 * m + (1.0 - ADAM_B1) * g
    v = ADAM_B2 * v + (1.0 - ADAM_B2) * (g * g)
    m_hat = m / (1.0 - ADAM_B1 ** ADAM_STEP)
    v_hat = v / (1.0 - ADAM_B2 ** ADAM_STEP)
    delta = -ADAM_LR * (m_hat / (jnp.sqrt(v_hat) + ADAM_EPS) + ADAM_WD * w)
    return delta, m, v


def _mod_shard(c_all, w_mod_s, b_mod_s):
    def body(c_ref, w_ref, b_ref, o_ref):
        o_ref[...] = _dot(_silu(c_ref[...]), w_ref[...], NN, True) + b_ref[...]

    return pl.pallas_call(body, name="mod_shard", out_shape=jax.ShapeDtypeStruct((N_DEV, w_mod_s.shape[1]), f32),
                          compiler_params=_params())(c_all, w_mod_s, b_mod_s)


def _mod_update(c_all, d_mod_s, w, m, v):
    def body(c_ref, d_ref, w_ref, m_ref, v_ref, g_ref, dw_ref, nm_ref, nv_ref):
        g = _dot(_silu(c_ref[...]), d_ref[...], TN, True)
        g_ref[...] = g
        dw_ref[...], nm_ref[...], nv_ref[...] = _adamw(w_ref[...], g, m_ref[...], v_ref[...])

    return pl.pallas_call(body, name="mod_update", out_shape=[jax.ShapeDtypeStruct(w.shape, f32)] * 4,
                          compiler_params=_params())(c_all, d_mod_s, w, m, v)


def _adamw_rows(w, g, m, v, name):
    rows, width = w.shape
    tr = PACK_CHUNK if rows % PACK_CHUNK == 0 else rows

    def body(w_ref, g_ref, m_ref, v_ref, dw_ref, nm_ref, nv_ref):
        dw_ref[...], nm_ref[...], nv_ref[...] = _adamw(w_ref[...], g_ref[...], m_ref[...], v_ref[...])

    spec = pl.BlockSpec((tr, width), lambda i: (i, 0))
    return pl.pallas_call(body, name=name, grid=(rows // tr,), in_specs=[spec] * 4, out_specs=[spec] * 3,
                          out_shape=[jax.ShapeDtypeStruct(w.shape, f32)] * 3,
                          compiler_params=_params(("parallel",)))(w, g, m, v)


SM_NORM, SM_FIN, SM_DN, SM_AT, SM_A, SM_DT, SM_LOSS, SM_MOD = 0, 1024, 2048, 2560, 3072, 3200, 3328, 3456
SM_W = SM_MOD + 3 * D_MODEL
RS_NORM, RS_FIN, RS_DN, RS_AT, RS_A, RS_DT, RS_BMOD = 0, 1024, 2048, 2176, 2304, 2432, 2560
RS_W = RS_BMOD + 3 * D_MODEL


def _small_update(gathered, w, m, v):
    def body(g_ref, w_ref, m_ref, v_ref, grad_ref, dw_ref, nm_ref, nv_ref, loss_ref, dmod_ref):
        total = g_ref[0:1, :]
        for dev in range(1, N_DEV):
            total = total + g_ref[8 * dev:8 * dev + 1, :]
        for dev in range(N_DEV):
            dmod_ref[dev:dev + 1, :] = g_ref[8 * dev:8 * dev + 1, SM_MOD:SM_W]
        dn = total[:, SM_DN:SM_DN + DN_DIM]
        for h in range(1, DN_HEADS):
            dn = dn + total[:, SM_DN + h * DN_DIM:SM_DN + (h + 1) * DN_DIM]
        at = total[:, SM_AT:SM_AT + AT_DIM]
        for h in range(1, AT_HEADS):
            at = at + total[:, SM_AT + h * AT_DIM:SM_AT + (h + 1) * AT_DIM]
        grad_ref[:, RS_NORM:RS_FIN] = total[:, SM_NORM:SM_FIN]
        grad_ref[:, RS_FIN:RS_DN] = total[:, SM_FIN:SM_DN]
        grad_ref[:, RS_DN:RS_AT] = dn
        grad_ref[:, RS_AT:RS_A] = jnp.zeros((1, 128), f32)
        grad_ref[:, RS_AT:RS_AT + AT_DIM] = at
        grad_ref[:, RS_A:RS_DT] = total[:, SM_A:SM_DT]
        grad_ref[:, RS_DT:RS_BMOD] = total[:, SM_DT:SM_LOSS]
        grad_ref[:, RS_BMOD:RS_W] = total[:, SM_MOD:SM_W]
        loss_ref[...] = total[:, SM_LOSS:SM_MOD]
        dw_ref[...], nm_ref[...], nv_ref[...] = _adamw(w_ref[...], grad_ref[...], m_ref[...], v_ref[...])

    row = jax.ShapeDtypeStruct((1, RS_W), f32)
    return pl.pallas_call(
        body, name="small_update",
        out_shape=[row, row, row, row, jax.ShapeDtypeStruct((1, 128), f32), jax.ShapeDtypeStruct((N_DEV, 3 * D_MODEL), f32)],
        compiler_params=_params())(gathered, w, m, v)


def _all_gather_rows(block, name):
    m_per, n = block.shape

    def body(x_ref, out_ref, send_sems, recv_sems, local_sem):
        x, y, c = lax.axis_index("x"), lax.axis_index("y"), lax.axis_index("c")
        me, sibling = (x, y, c), (x, y, 1 - c)
        chips = [(1 - x, y), (x, 1 - y), (1 - x, 1 - y)]

        def rows(px, py, pc):
            return out_ref.at[pl.ds((4 * px + 2 * py + pc) * m_per, m_per), :]

        def copy(k, blk, to, src=None):
            return pltpu.make_async_remote_copy(
                src_ref=rows(*blk) if src is None else src, dst_ref=rows(*blk),
                send_sem=send_sems.at[k], recv_sem=recv_sems.at[k], device_id=to, device_id_type=MESH)

        mine = pltpu.make_async_copy(x_ref, rows(*me), local_sem)
        mine.start()
        first = [copy(0, me, sibling, src=x_ref)]
        first += [copy(1 + j, me, (*chip, c), src=x_ref) for j, chip in enumerate(chips)]
        for cp in first:
            cp.start()
        passed = [copy(4 + j, (*chip, c), sibling) for j, chip in enumerate(chips)]
        for j, chip in enumerate(chips):
            copy(1 + j, (*chip, c), me).wait_recv()
            passed[j].start()
        copy(0, sibling, me).wait_recv()
        for j, chip in enumerate(chips):
            copy(4 + j, (*chip, 1 - c), me).wait_recv()
        for cp in first + passed:
            cp.wait_send()
        mine.wait()

    return pl.pallas_call(
        body, name=name,
        out_shape=jax.ShapeDtypeStruct((N_DEV * m_per, n), block.dtype),
        in_specs=[pl.BlockSpec(memory_space=pltpu.VMEM)],
        out_specs=pl.BlockSpec(memory_space=pltpu.VMEM),
        scratch_shapes=[pltpu.SemaphoreType.DMA((7,)), pltpu.SemaphoreType.DMA((7,)), pltpu.SemaphoreType.DMA],
        compiler_params=pltpu.CompilerParams(vmem_limit_bytes=VMEM_LIMIT),
    )(block)


def _gather_weight_shards(packed):
    rows, width = packed.shape

    def body(src_ref, out_ref, send_sems, recv_sems, local_sem):
        x, y, c = lax.axis_index("x"), lax.axis_index("y"), lax.axis_index("c")
        chips = [(1 - x, y), (x, 1 - y), (1 - x, 1 - y)]

        def copy(k, owner, to):
            slot = out_ref.at[2 * owner[0] + owner[1]]
            return pltpu.make_async_remote_copy(
                src_ref=src_ref, dst_ref=slot, send_sem=send_sems.at[k], recv_sem=recv_sems.at[k],
                device_id=(*to, c), device_id_type=MESH)

        mine = pltpu.make_async_copy(src_ref, out_ref.at[2 * x + y], local_sem)
        mine.start()
        sends = [copy(k, (x, y), chip) for k, chip in enumerate(chips)]
        for cp in sends:
            cp.start()
        for k, chip in enumerate(chips):
            copy(k, chip, (x, y)).wait_recv()
        for cp in sends:
            cp.wait_send()
        mine.wait()

    return pl.pallas_call(
        body, name="gather_weight_shards",
        out_shape=jax.ShapeDtypeStruct((N_CHIPS, rows, width), packed.dtype),
        in_specs=[pl.BlockSpec(memory_space=pl.ANY)],
        out_specs=pl.BlockSpec(memory_space=pl.ANY),
        scratch_shapes=[pltpu.SemaphoreType.DMA((3,)), pltpu.SemaphoreType.DMA((3,)), pltpu.SemaphoreType.DMA],
    )(packed)


def _reduce_weight_grads(grads, grads_bf):
    _, rows, width = grads.shape
    n_chunks = rows // PACK_CHUNK

    def body(g_ref, gbf_ref, out_ref, own_ref, land_ref, part_ref, sib_ref, send_sems, recv_sems, local_sem):
        x, y, c = lax.axis_index("x"), lax.axis_index("y"), lax.axis_index("c")
        chips = [(1 - x, y), (x, 1 - y), (1 - x, 1 - y)]

        def copy(k, chip):
            return pltpu.make_async_remote_copy(
                src_ref=gbf_ref.at[2 * chip[0] + chip[1]], dst_ref=land_ref.at[k],
                send_sem=send_sems.at[k], recv_sem=recv_sems.at[k], device_id=(*chip, c), device_id_type=MESH)

        mine = pltpu.make_async_copy(g_ref.at[2 * x + y], own_ref, local_sem)
        mine.start()
        sends = [copy(k, chip) for k, chip in enumerate(chips)]
        for cp in sends:
            cp.start()
        mine.wait()
        for cp in sends:
            cp.wait_recv()

        def add_landed(i, carry):
            r = pl.ds(pl.multiple_of(i * PACK_CHUNK, 16), PACK_CHUNK)
            landed = [land_ref[k, r, :].astype(f32) for k in range(3)]
            part_ref[r, :] = ((own_ref[r, :] + landed[0]) + landed[1]) + landed[2]
            return carry

        lax.fori_loop(0, n_chunks, add_landed, 0)
        swap = pltpu.make_async_remote_copy(
            src_ref=part_ref, dst_ref=sib_ref, send_sem=send_sems.at[3], recv_sem=recv_sems.at[3],
            device_id=(x, y, 1 - c), device_id_type=MESH)
        swap.start()
        swap.wait_recv()

        def add_sibling(i, carry):
            r = pl.ds(pl.multiple_of(i * PACK_CHUNK, 8), PACK_CHUNK)
            out_ref[r, :] = part_ref[r, :] + sib_ref[r, :]
            return carry

        lax.fori_loop(0, n_chunks, add_sibling, 0)
        swap.wait_send()
        for cp in sends:
            cp.wait_send()

    buf = pltpu.VMEM((rows, width), f32)
    return pl.pallas_call(
        body, name="reduce_weight_grads",
        out_shape=jax.ShapeDtypeStruct((rows, width), f32),
        in_specs=[pl.BlockSpec(memory_space=pl.ANY), pl.BlockSpec(memory_space=pl.ANY)],
        out_specs=pl.BlockSpec(memory_space=pltpu.VMEM),
        scratch_shapes=[buf, pltpu.VMEM((3, rows, width), bf16), buf, buf,
                        pltpu.SemaphoreType.DMA((4,)), pltpu.SemaphoreType.DMA((4,)), pltpu.SemaphoreType.DMA],
        compiler_params=pltpu.CompilerParams(vmem_limit_bytes=VMEM_LIMIT),
    )(grads, grads_bf)


def _pack_shards(w_in_s, w_out_s, conv_s):
    lead = w_in_s.shape[:-2]
    conv_rows = jnp.pad(conv_s.reshape(*lead, CONV_K * 384), [(0, 0)] * len(lead) + [(0, 8 * D_MODEL - CONV_K * 384)])
    return jnp.concatenate([
        w_in_s.reshape(*lead, SHARD_IN, D_MODEL), w_out_s, conv_rows.reshape(*lead, 8, D_MODEL),
        jnp.zeros((*lead, PACK_ROWS - SHARD_IN - 256 - 8, D_MODEL), f32)], axis=-2)


CONV_WORDS = 2 * CONV_K * 384


def _pack_weights_bf16(w_in_s, w_out_s, conv_s):
    conv_bits = lax.bitcast_convert_type(conv_s, bf16).reshape(CONV_WORDS)
    return jnp.concatenate([
        w_in_s.astype(bf16).reshape(SHARD_IN, D_MODEL), w_out_s.astype(bf16),
        jnp.pad(conv_bits, (0, 8 * D_MODEL - CONV_WORDS)).reshape(8, D_MODEL),
        jnp.zeros((PACK_ROWS - SHARD_IN - 256 - 8, D_MODEL), bf16)], axis=0)


def _unpack_weights_bf16(shards):
    w_in_s = shards[:, :SHARD_IN].reshape(N_CHIPS, D_MODEL, SHARD_IN)
    w_out = shards[:, SHARD_IN:SHARD_IN + 256].reshape(D_MODEL, D_MODEL)
    conv_bits = shards[:, SHARD_IN + 256:SHARD_IN + 264].reshape(N_CHIPS, 8 * D_MODEL)[:, :CONV_WORDS]
    conv_s = lax.bitcast_convert_type(conv_bits.reshape(N_CHIPS, CONV_K, 384, 2), f32)
    return (jnp.transpose(w_in_s, (1, 0, 2)).reshape(D_MODEL, IN_COLS), w_out,
            jnp.transpose(conv_s, (1, 0, 2)).reshape(CONV_K, QKV_W))


def _unpack_shards(packed):
    lead = packed.shape[:-2]
    w_in_s = packed[..., :SHARD_IN, :].reshape(*lead, D_MODEL, SHARD_IN)
    w_out_s = packed[..., SHARD_IN:SHARD_IN + 256, :]
    conv_s = packed[..., SHARD_IN + 256:SHARD_IN + 264, :].reshape(*lead, 8 * D_MODEL)[..., :CONV_K * 384]
    return w_in_s, w_out_s, conv_s.reshape(*lead, CONV_K, 384)


def _local_step(x, target, pos_col, mod_row, norm_w, w_in, conv_w, a_log, dt_bias, dn_norm_w, at_norm_w, w_out, fin_w):
    shift, scale, gate = mod_row[:, :D_MODEL], mod_row[:, D_MODEL:2 * D_MODEL], mod_row[:, 2 * D_MODEL:]
    half = AT_DIM // 2
    lane = jnp.arange(128)
    inv_freq = ROPE_THETA ** (-jnp.arange(half, dtype=f32) / half)
    invf_row = inv_freq[lane % half].reshape(1, 128)
    sign_row = jnp.where((lane % AT_DIM) < half, -1.0, 1.0).astype(f32).reshape(1, 128)
    ba_w = jnp.pad(w_in[:, 2 * D_MODEL:2 * D_MODEL + 2 * DN_HEADS], ((0, 0), (0, BA_W - 2 * DN_HEADS)))
    w_cat = jnp.concatenate([w_in[:, :2 * D_MODEL], ba_w, w_in[:, 2 * D_MODEL + 2 * DN_HEADS:]], axis=1).astype(bf16)
    conv_w8 = jnp.pad(conv_w, ((0, 8 - CONV_K), (0, 0)))
    a_row = jnp.pad(a_log.reshape(1, DN_HEADS), ((0, 0), (DN_HEADS, BA_W - 2 * DN_HEADS)))
    dt_row = jnp.pad(dt_bias.reshape(1, DN_HEADS), ((0, 0), (DN_HEADS, BA_W - 2 * DN_HEADS)))
    dn_w_row = jnp.tile(dn_norm_w.reshape(1, DN_DIM), (1, DN_HEADS))
    at_w_row = jnp.tile(at_norm_w.reshape(1, AT_DIM), (1, AT_HEADS))
    norm_row = norm_w.reshape(1, D_MODEL)
    fin_row = fin_w.reshape(1, D_MODEL)
    w_out_bf = w_out.astype(bf16)

    pdn, *views, z_at, h = _pre_proj(x, pos_col, invf_row, sign_row, norm_row, scale, shift, w_cat)
    *dn_parts, dn_invs = _dn_intra_forward(pdn, conv_w8, a_row, dt_row)
    o_dn, states = _dn_scan_forward(*dn_parts)
    outs, lses = [], []
    for d, view in zip(PATTERN_DILATIONS, views):
        o, lse = _attn_forward(view, d)
        outs.append(o)
        lses.append(lse)
    (dx2, d_odn, d_zdn, *in_views, d_zat, g_wtop, g_wbot, tail_rows) = _tail(
        x, target, o_dn, pdn, outs, lses, z_at, gate, w_out_bf, dn_w_row, at_w_row, fin_row)
    d_oat, delta, lse_all = in_views[0:3], in_views[3:6], in_views[6:9]
    d_q, d_k, d_v = [], [], []
    for n, (d, view) in enumerate(zip(PATTERN_DILATIONS, views)):
        args = (d_oat[n], delta[n], lse_all[n])
        d_q.append(_attn_backward_q(view, *args, d))
        dk, dv = _attn_backward_kv(view, *args, d)
        d_k.append(dk)
        d_v.append(dv)
    d_parts = _dn_scan_backward(*dn_parts, states, d_odn)
    d_conv, d_ba, dn_rows = _dn_intra_backward(pdn, conv_w8, a_row, dt_row, dn_invs, *d_parts)
    grad_x, d_proj, pre_rows, conv_rows = _pre_backward(
        x, dx2, pos_col, invf_row, sign_row, norm_row, scale, shift, w_cat, conv_w8, pdn, d_conv, d_zdn, d_ba,
        d_q, d_k, d_v, d_zat)
    g_cat = _weight_grad(h, d_proj)
    g_w_in = jnp.concatenate([g_cat[:, :2 * D_MODEL], g_cat[:, 2 * D_MODEL:2 * D_MODEL + 2 * DN_HEADS],
                              g_cat[:, 2 * D_MODEL + BA_W:]], axis=1)
    g_w_out = jnp.concatenate([g_wtop, g_wbot], axis=0)
    g_conv = conv_rows[:CONV_K]
    small = jnp.concatenate([
        pre_rows[0:1], tail_rows[2:3], tail_rows[3:4, :DN_WIDTH], tail_rows[4:5, :AT_WIDTH],
        dn_rows[0:1], dn_rows[1:2], tail_rows[0:1, :128],
        pre_rows[2:3], pre_rows[1:2], tail_rows[1:2]], axis=1)
    return grad_x, g_w_in, g_w_out, g_conv, small


def kernel(x, c, positions, w_mod, b_mod, norm_w, w_in, conv_w, a_log, dt_bias, dn_norm_w, at_norm_w, w_out, final_norm_w, loss_target, m_w_mod, m_b_mod, m_norm_w, m_w_in, m_conv_w, m_a_log, m_dt_bias, m_dn_norm_w, m_at_norm_w, m_w_out, m_final_norm_w, v_w_mod, v_b_mod, v_norm_w, v_w_in, v_conv_w, v_a_log, v_dt_bias, v_dn_norm_w, v_at_norm_w, v_w_out, v_final_norm_w):
    seq = x.shape[1]
    ax, ay, ac = lax.axis_index("x"), lax.axis_index("y"), lax.axis_index("c")
    me = 4 * ax + 2 * ay + ac
    chip = 2 * ax + ay

    c_all = _all_gather_rows(jnp.pad(c, ((0, 7), (0, 0))), "gather_c").reshape(N_DEV, 8, D_MODEL)[:, 0]
    b_mod_s = lax.dynamic_slice_in_dim(b_mod, chip * 768, 768, axis=1)
    mod_part = _mod_shard(c_all, w_mod[0], b_mod_s)
    mod_all = _all_gather_rows(mod_part, "gather_mod").reshape(N_CHIPS, 2, N_DEV, 768)[:, 0]
    mod_row = lax.dynamic_index_in_dim(mod_all, me, axis=1, keepdims=False).reshape(1, 3 * D_MODEL)

    shards = _gather_weight_shards(_pack_weights_bf16(w_in[0], w_out[0], conv_w[0]))
    w_in_full, w_out_full, conv_full = _unpack_weights_bf16(shards)

    grad_x, g_w_in, g_w_out, g_conv, small = _local_step(
        x[0], loss_target[0], positions.reshape(seq, 1), mod_row, norm_w, w_in_full, conv_full, a_log, dt_bias,
        dn_norm_w, at_norm_w, w_out_full, final_norm_w)

    g_pack = _pack_shards(jnp.transpose(g_w_in.reshape(D_MODEL, N_CHIPS, SHARD_IN), (1, 0, 2)),
                          g_w_out.reshape(N_CHIPS, 256, D_MODEL),
                          jnp.transpose(g_conv.reshape(CONV_K, N_CHIPS, 384), (1, 0, 2)))
    g_mine = _reduce_weight_grads(g_pack, g_pack.astype(bf16))
    d_pack, m_pack, v_pack = _adamw_rows(_pack_shards(w_in[0], w_out[0], conv_w[0]), g_mine,
                                         _pack_shards(m_w_in[0], m_w_out[0], m_conv_w[0]),
                                         _pack_shards(v_w_in[0], v_w_out[0], v_conv_w[0]), "adamw_packed")
    grad_w_in, grad_w_out, grad_conv_w = _unpack_shards(g_mine)
    delta_w_in, delta_w_out, delta_conv_w = _unpack_shards(d_pack)
    new_m_w_in, new_m_w_out, new_m_conv_w = _unpack_shards(m_pack)
    new_v_w_in, new_v_w_out, new_v_conv_w = _unpack_shards(v_pack)

    gathered = _all_gather_rows(jnp.pad(small, ((0, 7), (0, 0))), "gather_small")

    def small_row(norm, fin, dn, at, a, dt, bmod):
        z = lambda n: jnp.zeros((1, n), f32)
        return jnp.concatenate([norm.reshape(1, -1), fin.reshape(1, -1), dn.reshape(1, -1), at.reshape(1, -1), z(64),
                                z(4), a.reshape(1, -1), z(120), z(4), dt.reshape(1, -1), z(120), bmod.reshape(1, -1)], axis=1)

    g_small, d_small, m_small, v_small, loss_row, d_mod_all = _small_update(
        gathered,
        small_row(norm_w, final_norm_w, dn_norm_w, at_norm_w, a_log, dt_bias, b_mod),
        small_row(m_norm_w, m_final_norm_w, m_dn_norm_w, m_at_norm_w, m_a_log, m_dt_bias, m_b_mod),
        small_row(v_norm_w, v_final_norm_w, v_dn_norm_w, v_at_norm_w, v_a_log, v_dt_bias, v_b_mod))

    def split_small(r):
        return (r[:, RS_BMOD:RS_W], r[:, RS_NORM:RS_FIN], r[:, RS_A + DN_HEADS:RS_A + 2 * DN_HEADS],
                r[:, RS_DT + DN_HEADS:RS_DT + 2 * DN_HEADS], r[:, RS_DN:RS_AT], r[:, RS_AT:RS_AT + AT_DIM],
                r[0, RS_FIN:RS_DN])

    d_mod_s = lax.dynamic_slice_in_dim(d_mod_all, chip * 768, 768, axis=1)
    pad_rows = lambda a: jnp.pad(a, ((0, 128 - N_DEV), (0, 0)))
    grad_w_mod, delta_w_mod, new_m_w_mod, new_v_w_mod = _mod_update(
        pad_rows(c_all), pad_rows(d_mod_s), w_mod[0], m_w_mod[0], v_w_mod[0])

    def ordered(w_mod_leaf, small_row_leaf, w_in_leaf, conv_leaf, w_out_leaf):
        b, n, a, dt, dn, at, fin = split_small(small_row_leaf)
        return [w_mod_leaf[None], b, n, w_in_leaf[None], conv_leaf[None], a, dt, dn, at, w_out_leaf[None], fin]

    loss = loss_row[0, 0]
    return (loss, grad_x[None],
            *ordered(grad_w_mod, g_small, grad_w_in, grad_conv_w, grad_w_out),
            *ordered(delta_w_mod, d_small, delta_w_in, delta_conv_w, delta_w_out),
            *ordered(new_m_w_mod, m_small, new_m_w_in, new_m_conv_w, new_m_w_out),
            *ordered(new_v_w_mod, v_small, new_v_w_in, new_v_conv_w, new_v_w_out))
```

```python
import functools

import jax
import jax.numpy as jnp
from jax import lax
from jax.experimental import pallas as pl
from jax.experimental.pallas import tpu as pltpu

f32 = jnp.float32
bf16 = jnp.bfloat16
HIGHEST = lax.Precision.HIGHEST
MESH = pl.DeviceIdType.MESH

D_MODEL = 1024
DN_HEADS = 4
DN_DIM = 128
DN_WIDTH = 512
AT_HEADS = 8
AT_DIM = 64
AT_WIDTH = 512
CHUNK = 64
Q_BLOCK = 128
CONV_K = 4
EPS = 1e-6
ROPE_THETA = 10000.0
PATTERN_DILATIONS = (1, 4, 16)
NEG = -1e30

QKV_W = 3 * DN_WIDTH
BA_W = 128
PDN_W = QKV_W + DN_WIDTH + BA_W
PAT_W = 3 * AT_WIDTH
CAT_W = PDN_W + PAT_W + AT_WIDTH
IN_COLS = 4104
N_CHIPS = 4
N_DEV = 8
SHARD_IN = IN_COLS // N_CHIPS
PACK_ROWS = 1296
PACK_CHUNK = 48

ADAM_LR = 0.001
ADAM_B1 = 0.9
ADAM_B2 = 0.999
ADAM_EPS = 1e-08
ADAM_WD = 0.01
ADAM_STEP = 10

VMEM_LIMIT = 56 * 1024 * 1024

NN = ((1,), (0,))
NT = ((1,), (1,))
TN = ((0,), (0,))


def _pieces(a, n):
    out = []
    for _ in range(n - 1):
        p = a.astype(bf16)
        out.append(p)
        a = a - p.astype(f32)
    out.append(a.astype(bf16))
    return out


def _dot(a, b, dims, exact):
    raw = lambda p, q: lax.dot_general(p, q, (dims, ((), ())), preferred_element_type=f32)
    if exact == "split":
        (ah, al), (bh, bl) = _pieces(a, 2), _pieces(b, 2)
        return raw(ah, bh) + (raw(ah, bl) + raw(al, bh))
    if exact:
        return lax.dot_general(a, b, (dims, ((), ())), precision=HIGHEST, preferred_element_type=f32)
    return raw(a.astype(bf16), b.astype(bf16))


def _dot_sel(sel, b, dims, sel_side):
    raw = lambda p, q: lax.dot_general(p, q, (dims, ((), ())), preferred_element_type=f32)
    sel = sel.astype(bf16)
    parts = [raw(sel, p) if sel_side == 0 else raw(p, sel) for p in _pieces(b, 3)]
    return (parts[0] + parts[1]) + parts[2]


@jax.custom_vjp
def _sel_left(sel, b):
    return _dot_sel(sel, b, NN, 0)


def _sel_left_fwd(sel, b):
    return _dot_sel(sel, b, NN, 0), sel


def _sel_left_bwd(sel, g):
    return jnp.zeros_like(sel), _dot_sel(sel, g, TN, 0)


_sel_left.defvjp(_sel_left_fwd, _sel_left_bwd)


@jax.custom_vjp
def _sel_right(a, sel):
    return _dot_sel(sel, a, NN, 1)


def _sel_right_fwd(a, sel):
    return _dot_sel(sel, a, NN, 1), sel


def _sel_right_bwd(sel, g):
    return _dot_sel(sel, g, NT, 1), jnp.zeros_like(sel)


_sel_right.defvjp(_sel_right_fwd, _sel_right_bwd)


class _Matmuls:
    def __init__(self, exact, back):
        @jax.custom_vjp
        def nn(a, b):
            return _dot(a, b, NN, exact)

        def nn_fwd(a, b):
            return _dot(a, b, NN, exact), (a, b)

        def nn_bwd(res, g):
            a, b = res
            return _dot(g, b, NT, back), _dot(a, g, TN, back)

        nn.defvjp(nn_fwd, nn_bwd)

        @jax.custom_vjp
        def nt(a, b):
            return _dot(a, b, NT, exact)

        def nt_fwd(a, b):
            return _dot(a, b, NT, exact), (a, b)

        def nt_bwd(res, g):
            a, b = res
            return _dot(g, b, NN, back), _dot(g, a, TN, back)

        nt.defvjp(nt_fwd, nt_bwd)

        @jax.custom_vjp
        def tn(a, b):
            return _dot(a, b, TN, exact)

        def tn_fwd(a, b):
            return _dot(a, b, TN, exact), (a, b)

        def tn_bwd(res, g):
            a, b = res
            return _dot(b, g, NT, back), _dot(a, g, NN, back)

        tn.defvjp(tn_fwd, tn_bwd)
        self.nn, self.nt, self.tn = nn, nt, tn


MM = _Matmuls(exact=False, back=False)
MS = _Matmuls(exact="split", back=False)


def _each(fn, *lists):
    return [fn(*args) for args in zip(*lists)]


def _inverse_products(a_lows):
    ri = lax.broadcasted_iota(jnp.int32, (CHUNK, CHUNK), 0)
    ci = lax.broadcasted_iota(jnp.int32, (CHUNK, CHUNK), 1)
    eye = (ri == ci).astype(f32)
    power = _each(lambda a: -a, a_lows)
    inv = _each(lambda p: eye + p, power)
    for _ in range(5):
        power = _each(lambda p: _dot(p, p, NN, "split"), power)
        inv = _each(lambda x, p: x + _dot(x, p, NN, "split"), inv, power)
    return inv


def _inverse_cotangents(invs, gs):
    left = _each(lambda t, g: _dot(t, g, TN, "split"), invs, gs)
    return _each(lambda l, t: -_dot(l, t, NT, "split"), left, invs)


@jax.custom_vjp
def _unit_lower_inverses(a_lows):
    return _inverse_products(a_lows)


def _unit_lower_inverses_fwd(a_lows):
    invs = _inverse_products(a_lows)
    return invs, invs


def _unit_lower_inverses_bwd(invs, gs):
    return (_inverse_cotangents(invs, gs),)


_unit_lower_inverses.defvjp(_unit_lower_inverses_fwd, _unit_lower_inverses_bwd)


@jax.custom_vjp
def _known_inverses(a_lows, invs):
    return invs


def _known_inverses_fwd(a_lows, invs):
    return invs, invs


def _known_inverses_bwd(invs, gs):
    return _inverse_cotangents(invs, gs), _each(jnp.zeros_like, invs)


_known_inverses.defvjp(_known_inverses_fwd, _known_inverses_bwd)


def _params(semantics=None):
    return pltpu.CompilerParams(dimension_semantics=semantics, vmem_limit_bytes=VMEM_LIMIT)


def _silu(x):
    return x * jax.nn.sigmoid(x)


def _group_ones(width, group):
    r = lax.broadcasted_iota(jnp.int32, (width, width), 0) // group
    c = lax.broadcasted_iota(jnp.int32, (width, width), 1) // group
    return (r == c).astype(f32)


def _slab_sums(x):
    rows, width = x.shape
    sums = [jnp.sum(x[:, j:j + 128], axis=1, keepdims=True) for j in range(0, width, 128)]
    return jnp.concatenate([jnp.broadcast_to(s, (rows, 128)) for s in sums], axis=1)


def _hnorm(x, nw, scale, shift):
    xn = x * lax.rsqrt(jnp.mean(x * x, axis=-1, keepdims=True) + EPS)
    return xn * nw * (1.0 + scale) + shift


def _rope_tables(pos_col, invf_row, sign_row):
    ang = pos_col.astype(f32) * invf_row
    cos_t = jnp.cos(ang)
    sin_t = jnp.sin(ang) * sign_row
    return jnp.concatenate([cos_t] * 4, axis=1), jnp.concatenate([sin_t] * 4, axis=1)


def _rope_partner(x):
    lane = lax.broadcasted_iota(jnp.int32, x.shape, 1)
    width = x.shape[1]
    return jnp.where((lane % AT_DIM) < AT_DIM // 2, pltpu.roll(x, width - AT_DIM // 2, 1), pltpu.roll(x, AT_DIM // 2, 1))


TOKEN_TILE = 256
LANES = 128


def _stage_lanes(stage_ref, first, value):
    for j in range(value.shape[1] // LANES):
        stage_ref[first + j] = value[:, LANES * j:LANES * (j + 1)]


def _stage_to_view(stage_ref, view_ref, dil):
    chunks, rows, _ = stage_ref.shape
    for r in range(dil):
        for j in range(chunks):
            col = (r * chunks + j) * LANES
            view_ref[:, col:col + LANES] = stage_ref.at[j][pl.ds(r, rows // dil, stride=dil), :].astype(view_ref.dtype)


def _view_to_value(stage_ref, view_ref, dil):
    chunks, rows, _ = stage_ref.shape
    for r in range(dil):
        for j in range(chunks):
            col = (r * chunks + j) * LANES
            stage_ref.at[j][pl.ds(r, rows // dil, stride=dil), :] = view_ref[:, col:col + LANES].astype(f32)
    return jnp.concatenate([stage_ref[j] for j in range(chunks)], axis=1)


def _view_spec(width, dil):
    return pl.BlockSpec((TOKEN_TILE // dil, dil * width), lambda i: (i, 0))


def _view_shape(seq, width, dil, dtype=f32):
    return jax.ShapeDtypeStruct((seq // dil, dil * width), dtype)


HEAD_LANES = 128


def _repeat_heads(compact):
    row = lax.broadcasted_iota(jnp.int32, (HEAD_LANES, AT_WIDTH), 0)
    col = lax.broadcasted_iota(jnp.int32, (HEAD_LANES, AT_WIDTH), 1)
    return _dot_sel((row == col // AT_DIM).astype(f32), compact, NN, 1)


def _stage(width):
    return pltpu.VMEM((width // LANES, TOKEN_TILE, LANES), f32)


def _pre_proj(x, pos_col, invf_row, sign_row, norm_w, scale, shift, w_cat):
    seq = x.shape[0]
    tm = TOKEN_TILE

    def body(x_ref, pos_ref, invf_ref, sign_ref, nw_ref, sc_ref, sh_ref, w_ref,
             pdn_ref, pat1_ref, pat4_ref, pat16_ref, zat_ref, h_ref, stage_ref):
        h = _hnorm(x_ref[...], nw_ref[...], sc_ref[...], sh_ref[...]).astype(bf16)
        h_ref[...] = h
        big = jnp.dot(h, w_ref[...], preferred_element_type=f32)
        pdn_ref[...] = big[:, :PDN_W]
        cos_t, sin_t = _rope_tables(pos_ref[...], invf_ref[...], sign_ref[...])
        q = big[:, PDN_W:PDN_W + AT_WIDTH]
        k = big[:, PDN_W + AT_WIDTH:PDN_W + 2 * AT_WIDTH]
        parts = (q * cos_t + _rope_partner(q) * sin_t, k * cos_t + _rope_partner(k) * sin_t,
                 big[:, PDN_W + 2 * AT_WIDTH:PDN_W + PAT_W])
        for p, part in enumerate(parts):
            pat1_ref[:, p * AT_WIDTH:(p + 1) * AT_WIDTH] = part.astype(bf16)
            _stage_lanes(stage_ref, p * (AT_WIDTH // LANES), part)
        _stage_to_view(stage_ref, pat4_ref, 4)
        _stage_to_view(stage_ref, pat16_ref, 16)
        zat_ref[...] = big[:, PDN_W + PAT_W:]

    row = lambda w: pl.BlockSpec((1, w), lambda i: (0, 0))
    tile = lambda w: pl.BlockSpec((tm, w), lambda i: (i, 0))
    return pl.pallas_call(
        body, name="pre_proj", grid=(seq // tm,),
        in_specs=[tile(D_MODEL), tile(1), row(128), row(128), row(D_MODEL), row(D_MODEL), row(D_MODEL),
                  pl.BlockSpec((D_MODEL, CAT_W), lambda i: (0, 0))],
        out_specs=[tile(PDN_W), tile(PAT_W), _view_spec(PAT_W, 4), _view_spec(PAT_W, 16), tile(AT_WIDTH), tile(D_MODEL)],
        out_shape=[jax.ShapeDtypeStruct((seq, PDN_W), f32), _view_shape(seq, PAT_W, 1, bf16), _view_shape(seq, PAT_W, 4, bf16),
                   _view_shape(seq, PAT_W, 16, bf16), jax.ShapeDtypeStruct((seq, AT_WIDTH), f32),
                   jax.ShapeDtypeStruct((seq, D_MODEL), bf16)],
        scratch_shapes=[_stage(PAT_W)],
        compiler_params=_params(("parallel",)),
    )(x, pos_col, invf_row, sign_row, norm_w, scale, shift, w_cat)


def _conv_taps(ext_ref, halo, cur, w8):
    rows = cur.shape[0]
    ext_ref[0:8, :] = halo
    ext_ref[8:8 + rows, :] = cur
    out = ext_ref[pl.ds(5, rows), :] * w8[0:1, :]
    for j in range(1, CONV_K):
        out = out + ext_ref[pl.ds(5 + j, rows), :] * w8[j:j + 1, :]
    return out


def _dn_pre(cq, ck, cv, ba, a_row, dt_row):
    sq, sk, v = _silu(cq), _silu(ck), _silu(cv)
    qn = sq * lax.rsqrt(_slab_sums(sq * sq) + EPS)
    kn = sk * lax.rsqrt(_slab_sums(sk * sk) + EPS)
    beta_all = jax.nn.sigmoid(ba)
    g_all = -jnp.exp(a_row) * jax.nn.softplus(ba + dt_row)
    return qn, kn, v, beta_all, g_all


def _dn_intra(qs, ks, vs, betas, gs, known_invs=None):
    ri = lax.broadcasted_iota(jnp.int32, (CHUNK, CHUNK), 0)
    ci = lax.broadcasted_iota(jnp.int32, (CHUNK, CHUNK), 1)
    tril = ri >= ci
    strict = ri > ci
    lower = tril.astype(f32)
    each = _each
    g_wide = each(lambda g: jnp.broadcast_to(g, (CHUNK, DN_DIM)), gs)
    gc = each(lambda g: _sel_left(lower, g), g_wide)
    gc_sq = each(lambda c: c[:, :CHUNK], gc)
    g_end = each(lambda c: jnp.broadcast_to(c[CHUNK - 1:CHUNK, :], (CHUNK, DN_DIM)), gc)
    g_end8 = each(lambda c: jnp.broadcast_to(c[CHUNK - 1:CHUNK, :], (8, DN_DIM)), gc)
    decay = each(lambda s: jnp.exp(jnp.where(tril, s - s.T, -jnp.inf)), gc_sq)
    qs = each(lambda q: q * (DN_DIM ** -0.5), qs)
    kb = each(lambda k, b: k * b, ks, betas)
    vb = each(lambda v, b: v * b, vs, betas)
    kk = each(MM.nt, kb, ks)
    qk = each(MM.nt, qs, ks)
    a_low = each(lambda p, d: jnp.where(strict, p * d, 0.0), kk, decay)
    inv = _unit_lower_inverses(a_low) if known_invs is None else _known_inverses(a_low, known_invs)
    e_gc = each(jnp.exp, gc)
    u = each(MS.nn, inv, vb)
    w = each(lambda x, k, e: MS.nn(x, k * e), inv, kb, e_gc)
    attn = each(lambda p, d: jnp.where(tril, p * d, 0.0), qk, decay)
    q_dec = each(lambda q, e: q * e, qs, e_gc)
    k_dec = each(lambda k, ge, c: k * jnp.exp(ge - c), ks, g_end, gc)
    return u, w, q_dec, k_dec, attn, each(jnp.exp, g_end8), inv


def _dn_step(us, ws, q_decs, k_decs, attns, e_ends, states):
    each = _each
    v_new = each(lambda u, w, s: u - MM.nn(w, s), us, ws, states)
    qs = each(MM.nn, q_decs, states)
    o = each(lambda a, b, c: a + MM.nn(b, c), qs, attns, v_new)
    new_states = each(lambda s, e, k, v: s * e + MM.tn(k, v), states, e_ends, k_decs, v_new)
    return o, new_states


INTRA_CHUNKS = 2
SCAN_CHUNKS = 4


def _intra_specs(nc):
    rows = INTRA_CHUNKS * CHUNK
    cur = pl.BlockSpec((rows, QKV_W), lambda i: (i, 0))
    halo = pl.BlockSpec((8, QKV_W), lambda i: (jnp.maximum(i * (rows // 8) - 1, 0), 0))
    ba = pl.BlockSpec((rows, BA_W), lambda i: (i, (QKV_W + DN_WIDTH) // BA_W))
    conv = pl.BlockSpec((8, QKV_W), lambda i: (0, 0))
    row = pl.BlockSpec((1, BA_W), lambda i: (0, 0))
    wide = pl.BlockSpec((rows, DN_WIDTH), lambda i: (i, 0))
    attn = pl.BlockSpec((INTRA_CHUNKS, DN_HEADS, CHUNK, CHUNK), lambda i: (i, 0, 0, 0))
    e_end = pl.BlockSpec((INTRA_CHUNKS, 8 * DN_HEADS, DN_DIM), lambda i: (i, 0, 0))
    return cur, halo, ba, conv, row, wide, attn, e_end


def _intra_items():
    return [(ci, h, slice(ci * CHUNK, (ci + 1) * CHUNK), slice(h * DN_DIM, (h + 1) * DN_DIM))
            for ci in range(INTRA_CHUNKS) for h in range(DN_HEADS)]


def _intra_inputs(items, qn, kn, v, beta_all, g_all):
    return ([qn[rows, lanes] for _, _, rows, lanes in items], [kn[rows, lanes] for _, _, rows, lanes in items],
            [v[rows, lanes] for _, _, rows, lanes in items], [beta_all[rows, h:h + 1] for _, h, rows, _ in items],
            [g_all[rows, DN_HEADS + h:DN_HEADS + h + 1] for _, h, rows, _ in items])


def _intra_shapes(seq, operand_dtype):
    nc = seq // CHUNK
    wide = jax.ShapeDtypeStruct((seq, DN_WIDTH), f32)
    operand = jax.ShapeDtypeStruct((seq, DN_WIDTH), operand_dtype)
    return [wide, operand, operand, operand, jax.ShapeDtypeStruct((nc, DN_HEADS, CHUNK, CHUNK), f32),
            jax.ShapeDtypeStruct((nc, 8 * DN_HEADS, DN_DIM), f32)]


def _dn_intra_forward(pdn, conv_w8, a_row, dt_row):
    seq = pdn.shape[0]
    nc = seq // CHUNK
    cur, halo, ba, conv, row, wide, attn, e_end = _intra_specs(nc)

    def body(cur_ref, halo_ref, ba_ref, w_ref, a_ref, dt_ref, u_ref, w_out_ref, qd_ref, kd_ref, attn_ref, e_ref, inv_ref,
             ext_ref):
        halo_rows = jnp.where(pl.program_id(0) > 0, halo_ref[...], 0.0)
        c = _conv_taps(ext_ref, halo_rows, cur_ref[...], w_ref[...])
        qn, kn, v, beta_all, g_all = _dn_pre(c[:, :DN_WIDTH], c[:, DN_WIDTH:2 * DN_WIDTH], c[:, 2 * DN_WIDTH:],
                                             ba_ref[...], a_ref[...], dt_ref[...])
        items = _intra_items()
        u, w, qd, kd, at, e8, inv = _dn_intra(*_intra_inputs(items, qn, kn, v, beta_all, g_all))
        for n, (ci, h, rows, lanes) in enumerate(items):
            u_ref[rows, lanes] = u[n]
            w_out_ref[rows, lanes] = w[n].astype(bf16)
            qd_ref[rows, lanes] = qd[n].astype(bf16)
            kd_ref[rows, lanes] = kd[n].astype(bf16)
            attn_ref[ci, h] = at[n]
            e_ref[ci, 8 * h:8 * h + 8, :] = e8[n]
            inv_ref[ci, h] = inv[n]

    return pl.pallas_call(
        body, name="dn_intra_forward", grid=(nc // INTRA_CHUNKS,),
        in_specs=[cur, halo, ba, conv, row, row],
        out_specs=[wide, wide, wide, wide, attn, e_end, attn],
        out_shape=_intra_shapes(seq, bf16) + [jax.ShapeDtypeStruct((nc, DN_HEADS, CHUNK, CHUNK), f32)],
        scratch_shapes=[pltpu.VMEM((INTRA_CHUNKS * CHUNK + 8, QKV_W), f32)],
        compiler_params=_params(("parallel",)),
    )(pdn, pdn, pdn, conv_w8, a_row, dt_row)


def _scan_specs(nc, reverse):
    steps = nc // SCAN_CHUNKS
    at = (lambda i: steps - 1 - i) if reverse else (lambda i: i)
    wide = pl.BlockSpec((SCAN_CHUNKS * CHUNK, DN_WIDTH), lambda i: (at(i), 0))
    attn = pl.BlockSpec((SCAN_CHUNKS, DN_HEADS, CHUNK, CHUNK), lambda i: (at(i), 0, 0, 0))
    e_end = pl.BlockSpec((SCAN_CHUNKS, 8 * DN_HEADS, DN_DIM), lambda i: (at(i), 0, 0))
    states = pl.BlockSpec((SCAN_CHUNKS, DN_HEADS, DN_DIM, DN_DIM), lambda i: (at(i), 0, 0, 0))
    return wide, attn, e_end, states


def _step_inputs(ci, rows, lanes, u_ref, w_ref, qd_ref, kd_ref, attn_ref, e_ref):
    heads = range(DN_HEADS)
    return ([u_ref[rows, lanes[h]] for h in heads], [w_ref[rows, lanes[h]].astype(f32) for h in heads],
            [qd_ref[rows, lanes[h]].astype(f32) for h in heads], [kd_ref[rows, lanes[h]].astype(f32) for h in heads],
            [attn_ref[ci, h] for h in heads], [e_ref[ci, 8 * h:8 * h + 1, :] for h in heads])


def _dn_scan_forward(u, w, q_dec, k_dec, attn, e_end):
    seq = u.shape[0]
    nc = seq // CHUNK
    wide, attn_spec, e_spec, st_spec = _scan_specs(nc, reverse=False)

    def body(u_ref, w_ref, qd_ref, kd_ref, attn_ref, e_ref, o_ref, st_ref, state_ref):
        @pl.when(pl.program_id(0) == 0)
        def _():
            state_ref[...] = jnp.zeros_like(state_ref)

        heads = range(DN_HEADS)
        lanes = [slice(h * DN_DIM, (h + 1) * DN_DIM) for h in heads]
        states = [state_ref[h] for h in heads]
        for ci in range(SCAN_CHUNKS):
            rows = slice(ci * CHUNK, (ci + 1) * CHUNK)
            for h in heads:
                st_ref[ci, h] = states[h].astype(bf16)
            o, states = _dn_step(*_step_inputs(ci, rows, lanes, u_ref, w_ref, qd_ref, kd_ref, attn_ref, e_ref), states)
            for h in heads:
                o_ref[rows, lanes[h]] = o[h]
        for h in heads:
            state_ref[h] = states[h]

    return pl.pallas_call(
        body, name="dn_scan_forward", grid=(nc // SCAN_CHUNKS,),
        in_specs=[wide, wide, wide, wide, attn_spec, e_spec],
        out_specs=[wide, st_spec],
        out_shape=[jax.ShapeDtypeStruct((seq, DN_WIDTH), f32), jax.ShapeDtypeStruct((nc, DN_HEADS, DN_DIM, DN_DIM), bf16)],
        scratch_shapes=[pltpu.VMEM((DN_HEADS, DN_DIM, DN_DIM), f32)],
        compiler_params=_params(("arbitrary",)),
    )(u, w, q_dec, k_dec, attn, e_end)


def _dn_scan_backward(u, w, q_dec, k_dec, attn, e_end, states, d_o):
    seq = u.shape[0]
    nc = seq // CHUNK
    wide, attn_spec, e_spec, st_spec = _scan_specs(nc, reverse=True)

    def body(u_ref, w_ref, qd_ref, kd_ref, attn_ref, e_ref, st_ref, do_ref,
             du_ref, dw_ref, dqd_ref, dkd_ref, dattn_ref, de_ref, dstate_ref):
        @pl.when(pl.program_id(0) == 0)
        def _():
            dstate_ref[...] = jnp.zeros_like(dstate_ref)

        heads = range(DN_HEADS)
        lanes = [slice(h * DN_DIM, (h + 1) * DN_DIM) for h in heads]
        first_row = lax.broadcasted_iota(jnp.int32, (8, DN_DIM), 0) == 0
        dstates = [dstate_ref[h] for h in heads]
        for ci in reversed(range(SCAN_CHUNKS)):
            rows = slice(ci * CHUNK, (ci + 1) * CHUNK)
            _, step_vjp = jax.vjp(_dn_step, *_step_inputs(ci, rows, lanes, u_ref, w_ref, qd_ref, kd_ref, attn_ref, e_ref),
                                  [st_ref[ci, h].astype(f32) for h in heads])
            du, dw, dqd, dkd, dattn, de, dstates = step_vjp(([do_ref[rows, lanes[h]] for h in heads], dstates))
            for h in heads:
                du_ref[rows, lanes[h]] = du[h]
                dw_ref[rows, lanes[h]] = dw[h]
                dqd_ref[rows, lanes[h]] = dqd[h]
                dkd_ref[rows, lanes[h]] = dkd[h]
                dattn_ref[ci, h] = dattn[h]
                de_ref[ci, 8 * h:8 * h + 8, :] = jnp.where(first_row, jnp.broadcast_to(de[h], (8, DN_DIM)), 0.0)
        for h in heads:
            dstate_ref[h] = dstates[h]

    return pl.pallas_call(
        body, name="dn_scan_backward", grid=(nc // SCAN_CHUNKS,),
        in_specs=[wide, wide, wide, wide, attn_spec, e_spec, st_spec, wide],
        out_specs=[wide, wide, wide, wide, attn_spec, e_spec],
        out_shape=_intra_shapes(seq, f32),
        scratch_shapes=[pltpu.VMEM((DN_HEADS, DN_DIM, DN_DIM), f32)],
        compiler_params=_params(("arbitrary",)),
    )(u, w, q_dec, k_dec, attn, e_end, states, d_o)


def _dn_intra_backward(pdn, conv_w8, a_row, dt_row, invs, d_u, d_w, d_qd, d_kd, d_attn, d_e):
    seq = pdn.shape[0]
    nc = seq // CHUNK
    rows_per_step = INTRA_CHUNKS * CHUNK
    cur, halo, ba, conv, row, wide, attn, e_end = _intra_specs(nc)

    def body(cur_ref, halo_ref, ba_ref, w_ref, a_ref, dt_ref, inv_ref, du_ref, dw_ref, dqd_ref, dkd_ref, dattn_ref, de_ref,
             dconv_ref, dba_ref, drow_ref, ext_ref):
        @pl.when(pl.program_id(0) == 0)
        def _():
            drow_ref[...] = jnp.zeros_like(drow_ref)

        halo_rows = jnp.where(pl.program_id(0) > 0, halo_ref[...], 0.0)
        c = _conv_taps(ext_ref, halo_rows, cur_ref[...], w_ref[...])
        (qn, kn, v, beta_all, g_all), pre_vjp = jax.vjp(
            _dn_pre, c[:, :DN_WIDTH], c[:, DN_WIDTH:2 * DN_WIDTH], c[:, 2 * DN_WIDTH:], ba_ref[...], a_ref[...], dt_ref[...])
        lane = lax.broadcasted_iota(jnp.int32, (CHUNK, BA_W), 1)
        items = _intra_items()
        _, intra_vjp = jax.vjp(_dn_intra, *_intra_inputs(items, qn, kn, v, beta_all, g_all),
                               [inv_ref[ci, h] for ci, h, _, _ in items])
        dq, dk, dv, dbeta, dg, _ = intra_vjp((
            [du_ref[rows, lanes] for _, _, rows, lanes in items], [dw_ref[rows, lanes] for _, _, rows, lanes in items],
            [dqd_ref[rows, lanes] for _, _, rows, lanes in items], [dkd_ref[rows, lanes] for _, _, rows, lanes in items],
            [dattn_ref[ci, h] for ci, h, _, _ in items], [de_ref[ci, 8 * h:8 * h + 8, :] for ci, h, _, _ in items],
            [jnp.zeros((CHUNK, CHUNK), f32) for _ in items]))
        dq_rows, dk_rows, dv_rows, dbeta_rows, dg_rows = [], [], [], [], []
        for ci in range(INTRA_CHUNKS):
            of_chunk = [n for n, item in enumerate(items) if item[0] == ci]
            d_beta_all = jnp.zeros((CHUNK, BA_W), f32)
            d_g_all = jnp.zeros((CHUNK, BA_W), f32)
            for n in of_chunk:
                h = items[n][1]
                d_beta_all = d_beta_all + jnp.where(lane == h, dbeta[n], 0.0)
                d_g_all = d_g_all + jnp.where(lane == DN_HEADS + h, dg[n], 0.0)
            dq_rows.append(jnp.concatenate([dq[n] for n in of_chunk], axis=1))
            dk_rows.append(jnp.concatenate([dk[n] for n in of_chunk], axis=1))
            dv_rows.append(jnp.concatenate([dv[n] for n in of_chunk], axis=1))
            dbeta_rows.append(d_beta_all)
            dg_rows.append(d_g_all)
        stack = lambda parts: jnp.concatenate(parts, axis=0)
        dcq, dck, dcv, dba, da_row, ddt_row = pre_vjp(
            (stack(dq_rows), stack(dk_rows), stack(dv_rows), stack(dbeta_rows), stack(dg_rows)))
        dconv_ref[:, :DN_WIDTH] = dcq
        dconv_ref[:, DN_WIDTH:2 * DN_WIDTH] = dck
        dconv_ref[:, 2 * DN_WIDTH:] = dcv
        dba_ref[...] = dba
        drow_ref[0:1, :] += da_row
        drow_ref[1:2, :] += ddt_row

    return pl.pallas_call(
        body, name="dn_intra_backward", grid=(nc // INTRA_CHUNKS,),
        in_specs=[cur, halo, ba, conv, row, row, attn, wide, wide, wide, wide, attn, e_end],
        out_specs=[pl.BlockSpec((rows_per_step, QKV_W), lambda i: (i, 0)),
                   pl.BlockSpec((rows_per_step, BA_W), lambda i: (i, 0)),
                   pl.BlockSpec((8, BA_W), lambda i: (0, 0))],
        out_shape=[jax.ShapeDtypeStruct((seq, QKV_W), f32), jax.ShapeDtypeStruct((seq, BA_W), f32),
                   jax.ShapeDtypeStruct((8, BA_W), f32)],
        scratch_shapes=[pltpu.VMEM((rows_per_step + 8, QKV_W), f32)],
        compiler_params=_params(("arbitrary",)),
    )(pdn, pdn, pdn, conv_w8, a_row, dt_row, invs, d_u, d_w, d_qd, d_kd, d_attn, d_e)


def _band_masks(block_index):
    qi = lax.broadcasted_iota(jnp.int32, (Q_BLOCK, Q_BLOCK), 0)
    kj = lax.broadcasted_iota(jnp.int32, (Q_BLOCK, Q_BLOCK), 1)
    return (kj >= qi) & (block_index > 0), kj <= qi


def _low_half():
    return lax.broadcasted_iota(jnp.int32, (Q_BLOCK, LANES), 1) < AT_DIM


def _head_pairs(ref, split):
    low = _low_half()
    slabs = [ref[:, pair * LANES:(pair + 1) * LANES].astype(bf16) for pair in range(AT_HEADS // 2)]
    if not split:
        return slabs
    zero = jnp.zeros((Q_BLOCK, LANES), bf16)
    return [jnp.where(low, slab, zero) if h == 0 else jnp.where(low, zero, slab) for slab in slabs for h in range(2)]


def _stack_blocks(first, second):
    return [jnp.concatenate([a, b], axis=0) for a, b in zip(first, second)]


def _attn_forward(pat_view, dil):
    length = pat_view.shape[0]
    nb = length // Q_BLOCK
    scale = AT_DIM ** -0.5

    def body(q_ref, kp_ref, kc_ref, vp_ref, vc_ref, o_ref, lse_ref):
        mask = jnp.concatenate(_band_masks(pl.program_id(1)), axis=1)
        low = _low_half()
        heads = range(AT_HEADS)
        q = _head_pairs(q_ref, split=True)
        k = _stack_blocks(_head_pairs(kp_ref, split=False), _head_pairs(kc_ref, split=False))
        v = _stack_blocks(_head_pairs(vp_ref, split=False), _head_pairs(vc_ref, split=False))
        s = [_dot(q[h], k[h // 2], NT, False) for h in heads]
        p, top = [], []
        for h in heads:
            masked = jnp.where(mask, s[h] * scale, NEG)
            m = jnp.max(masked, axis=1, keepdims=True)
            p.append(jnp.exp(masked - m).astype(bf16))
            top.append(m)
        ones = jnp.ones((2 * Q_BLOCK, LANES), bf16)
        l = [_dot(p[h], ones, NN, False) for h in heads]
        o = [_dot(p[h], v[h // 2], NN, False) for h in heads]
        for pair in range(AT_HEADS // 2):
            even, odd = 2 * pair, 2 * pair + 1
            slab = slice(pair * LANES, (pair + 1) * LANES)
            o_ref[:, slab] = jnp.where(low, o[even], o[odd]) / jnp.where(low, l[even], l[odd])
        lane = lax.broadcasted_iota(jnp.int32, (Q_BLOCK, HEAD_LANES), 1)
        lse = jnp.zeros((Q_BLOCK, HEAD_LANES), f32)
        for h in heads:
            lse = jnp.where(lane == h, top[h] + jnp.log(l[h]), lse)
        lse_ref[...] = lse

    def blk(piece, prev):
        if prev:
            return pl.BlockSpec((Q_BLOCK, AT_WIDTH), lambda r, n: (jnp.maximum(n - 1, 0), 3 * r + piece))
        return pl.BlockSpec((Q_BLOCK, AT_WIDTH), lambda r, n: (n, 3 * r + piece))

    out = pl.BlockSpec((Q_BLOCK, AT_WIDTH), lambda r, n: (n, r))
    return pl.pallas_call(
        body, name=f"attn_forward_d{dil}", grid=(dil, nb),
        in_specs=[blk(0, False), blk(1, True), blk(1, False), blk(2, True), blk(2, False)],
        out_specs=[out, pl.BlockSpec((Q_BLOCK, HEAD_LANES), lambda r, n: (n, r))],
        out_shape=[jax.ShapeDtypeStruct((length, dil * AT_WIDTH), f32),
                   jax.ShapeDtypeStruct((length, dil * HEAD_LANES), f32)],
        compiler_params=_params(("parallel", "parallel")),
    )(pat_view, pat_view, pat_view, pat_view, pat_view)


def _attn_backward_q(pat_view, d_out, delta, lse, dil):
    length = pat_view.shape[0]
    nb = length // Q_BLOCK
    scale = AT_DIM ** -0.5

    def body(q_ref, kp_ref, kc_ref, vp_ref, vc_ref, do_ref, dl_ref, lse_ref, dq_ref):
        mask = jnp.concatenate(_band_masks(pl.program_id(1)), axis=1)
        low = _low_half()
        heads = range(AT_HEADS)
        q, do = _head_pairs(q_ref, split=True), _head_pairs(do_ref, split=True)
        k = _stack_blocks(_head_pairs(kp_ref, split=False), _head_pairs(kc_ref, split=False))
        v = _stack_blocks(_head_pairs(vp_ref, split=False), _head_pairs(vc_ref, split=False))
        s = [_dot(q[h], k[h // 2], NT, False) for h in heads]
        dp = [_dot(do[h], v[h // 2], NT, False) for h in heads]
        ds = []
        for h in heads:
            p = jnp.exp(jnp.where(mask, s[h] * scale - lse_ref[:, h:h + 1], NEG))
            ds.append((p * (dp[h] - dl_ref[:, h:h + 1])).astype(bf16))
        dq = [_dot(ds[h], k[h // 2], NN, False) for h in heads]
        for pair in range(AT_HEADS // 2):
            dq_ref[:, pair * LANES:(pair + 1) * LANES] = (
                jnp.where(low, dq[2 * pair], dq[2 * pair + 1]) * scale).astype(bf16)

    def blk(piece, prev):
        if prev:
            return pl.BlockSpec((Q_BLOCK, AT_WIDTH), lambda r, n: (jnp.maximum(n - 1, 0), 3 * r + piece))
        return pl.BlockSpec((Q_BLOCK, AT_WIDTH), lambda r, n: (n, 3 * r + piece))

    one = pl.BlockSpec((Q_BLOCK, AT_WIDTH), lambda r, n: (n, r))
    compact = pl.BlockSpec((Q_BLOCK, HEAD_LANES), lambda r, n: (n, r))
    return pl.pallas_call(
        body, name=f"attn_backward_q_d{dil}", grid=(dil, nb),
        in_specs=[blk(0, False), blk(1, True), blk(1, False), blk(2, True), blk(2, False), one, compact, compact],
        out_specs=one,
        out_shape=jax.ShapeDtypeStruct((length, dil * AT_WIDTH), bf16),
        compiler_params=_params(("parallel", "parallel")),
    )(pat_view, pat_view, pat_view, pat_view, pat_view, d_out, delta, lse)


def _attn_backward_kv(pat_view, d_out, delta, lse, dil):
    length = pat_view.shape[0]
    nb = length // Q_BLOCK
    scale = AT_DIM ** -0.5

    def body(k_ref, v_ref, qa_ref, qb_ref, doa_ref, dob_ref, dla_ref, dlb_ref, lsea_ref, lseb_ref, dk_ref, dv_ref):
        j = pl.program_id(1)
        kj = lax.broadcasted_iota(jnp.int32, (Q_BLOCK, Q_BLOCK), 0)
        qi = lax.broadcasted_iota(jnp.int32, (Q_BLOCK, Q_BLOCK), 1)
        mask = jnp.concatenate([kj <= qi,
                                (kj >= qi) & (j + 1 < nb)],
                               axis=1)
        low = _low_half()
        heads = range(AT_HEADS)
        k, v = _head_pairs(k_ref, split=True), _head_pairs(v_ref, split=True)
        q = _stack_blocks(_head_pairs(qa_ref, split=False), _head_pairs(qb_ref, split=False))
        do = _stack_blocks(_head_pairs(doa_ref, split=False), _head_pairs(dob_ref, split=False))
        row = lax.broadcasted_iota(jnp.int32, (8 * AT_HEADS, HEAD_LANES), 0)
        col = lax.broadcasted_iota(jnp.int32, (8 * AT_HEADS, HEAD_LANES), 1)
        pick = (col == row // 8).astype(f32)
        lse_rows = _dot_sel(pick, jnp.concatenate([lsea_ref[...], lseb_ref[...]], axis=0), NT, 0)
        dl_rows = _dot_sel(pick, jnp.concatenate([dla_ref[...], dlb_ref[...]], axis=0), NT, 0)
        s_t = [_dot(k[h], q[h // 2], NT, False) for h in heads]
        dp_t = [_dot(v[h], do[h // 2], NT, False) for h in heads]
        p_t, ds_t = [], []
        for h in heads:
            prob = jnp.exp(jnp.where(mask, s_t[h] * scale - lse_rows[8 * h:8 * h + 1, :], NEG))
            p_t.append(prob.astype(bf16))
            ds_t.append((prob * (dp_t[h] - dl_rows[8 * h:8 * h + 1, :])).astype(bf16))
        dv = [_dot(p_t[h], do[h // 2], NN, False) for h in heads]
        dk = [_dot(ds_t[h], q[h // 2], NN, False) for h in heads]
        for pair in range(AT_HEADS // 2):
            slab = slice(pair * LANES, (pair + 1) * LANES)
            dk_ref[:, slab] = (jnp.where(low, dk[2 * pair], dk[2 * pair + 1]) * scale).astype(bf16)
            dv_ref[:, slab] = jnp.where(low, dv[2 * pair], dv[2 * pair + 1]).astype(bf16)

    def kv(piece):
        return pl.BlockSpec((Q_BLOCK, AT_WIDTH), lambda r, n: (n, 3 * r + piece))

    q_same = pl.BlockSpec((Q_BLOCK, AT_WIDTH), lambda r, n: (n, 3 * r))
    q_next = pl.BlockSpec((Q_BLOCK, AT_WIDTH), lambda r, n: (jnp.minimum(n + 1, nb - 1), 3 * r))
    same = pl.BlockSpec((Q_BLOCK, AT_WIDTH), lambda r, n: (n, r))
    nxt = pl.BlockSpec((Q_BLOCK, AT_WIDTH), lambda r, n: (jnp.minimum(n + 1, nb - 1), r))
    c_same = pl.BlockSpec((Q_BLOCK, HEAD_LANES), lambda r, n: (n, r))
    c_nxt = pl.BlockSpec((Q_BLOCK, HEAD_LANES), lambda r, n: (jnp.minimum(n + 1, nb - 1), r))
    return pl.pallas_call(
        body, name=f"attn_backward_kv_d{dil}", grid=(dil, nb),
        in_specs=[kv(1), kv(2), q_same, q_next, same, nxt, c_same, c_nxt, c_same, c_nxt],
        out_specs=[same, same],
        out_shape=[jax.ShapeDtypeStruct((length, dil * AT_WIDTH), bf16)] * 2,
        compiler_params=_params(("parallel", "parallel")),
    )(pat_view, pat_view, pat_view, pat_view, d_out, d_out, delta, delta, lse, lse)


def _gated_norms(o_dn, z_dn, o_at, z_at, dn_w, at_w):
    ms_dn = _slab_sums(o_dn * o_dn) * (1.0 / DN_DIM)
    a = o_dn * lax.rsqrt(ms_dn + EPS) * dn_w * _silu(z_dn)
    head_of_lane = (lax.broadcasted_iota(jnp.int32, (AT_WIDTH, HEAD_LANES), 0) // AT_DIM
                    == lax.broadcasted_iota(jnp.int32, (AT_WIDTH, HEAD_LANES), 1)).astype(f32)
    lanes_of_head = (lax.broadcasted_iota(jnp.int32, (HEAD_LANES, AT_WIDTH), 0)
                     == lax.broadcasted_iota(jnp.int32, (HEAD_LANES, AT_WIDTH), 1) // AT_DIM).astype(f32)
    ms_at = _sel_right(_sel_right(o_at * o_at, head_of_lane), lanes_of_head) * (1.0 / AT_DIM)
    b = o_at * lax.rsqrt(ms_at + EPS) * at_w * _silu(z_at)
    return a, b


def _residual_loss(x, mix, gate, fin_w, target):
    x2 = x + gate * mix
    y = x2 * lax.rsqrt(jnp.mean(x2 * x2, axis=-1, keepdims=True) + EPS) * fin_w
    err = y - target
    per_token = jnp.sum(err * err, axis=1, keepdims=True) * (1.0 / D_MODEL)
    return 0.5 * jnp.sum(per_token, axis=0, keepdims=True)


TAIL_ROWS = 8


def _tail(x, target, o_dn, pdn, outs, lses, z_at, gate, w_out, dn_w_row, at_w_row, fin_w):
    seq = x.shape[0]
    tm = TOKEN_TILE

    def body(x_ref, t_ref, odn_ref, zdn_ref, o1_ref, o4_ref, o16_ref, l1_ref, l4_ref, l16_ref, zat_ref, gate_ref,
             wt_ref, wb_ref, dnw_ref, atw_ref, fw_ref,
             dx2_ref, dodn_ref, dzdn_ref, doat1_ref, doat4_ref, doat16_ref, delta1_ref, delta4_ref, delta16_ref,
             lse1_ref, lse4_ref, lse16_ref, dzat_ref, gwt_ref, gwb_ref, rows_ref, *stages):
        @pl.when(pl.program_id(0) == 0)
        def _():
            gwt_ref[...] = jnp.zeros_like(gwt_ref)
            gwb_ref[...] = jnp.zeros_like(gwb_ref)
            rows_ref[...] = jnp.zeros_like(rows_ref)

        l1, l4, l16 = l1_ref[...], _view_to_value(stages[0], l4_ref, 4), _view_to_value(stages[1], l16_ref, 16)
        o4, o16 = _view_to_value(stages[2], o4_ref, 4), _view_to_value(stages[3], o16_ref, 16)
        top = jnp.maximum(jnp.maximum(l1, l4), l16)
        e1, e4, e16 = jnp.exp(l1 - top), jnp.exp(l4 - top), jnp.exp(l16 - top)
        den = e1 + e4 + e16
        lse = top + jnp.log(den)
        o_at = (_repeat_heads(e1 / den) * o1_ref[...] + _repeat_heads(e4 / den) * o4 + _repeat_heads(e16 / den) * o16)

        (a, b), norms_vjp = jax.vjp(_gated_norms, odn_ref[...], zdn_ref[...], o_at, zat_ref[...], dnw_ref[...], atw_ref[...])
        a, b = a.astype(bf16), b.astype(bf16)
        mix = jnp.dot(a, wt_ref[...], preferred_element_type=f32) + jnp.dot(b, wb_ref[...], preferred_element_type=f32)
        loss, loss_vjp = jax.vjp(_residual_loss, x_ref[...], mix, gate_ref[...], fw_ref[...], t_ref[...])
        dx2, dmix, dgate, dfw, _ = loss_vjp(jnp.ones((1, 1), f32))
        dmix = dmix.astype(bf16)
        dwt = lax.dot_general(a, dmix, (TN, ((), ())), preferred_element_type=f32)
        dwb = lax.dot_general(b, dmix, (TN, ((), ())), preferred_element_type=f32)
        da = lax.dot_general(dmix, wt_ref[...], (NT, ((), ())), preferred_element_type=f32)
        db = lax.dot_general(dmix, wb_ref[...], (NT, ((), ())), preferred_element_type=f32)
        dodn, dzdn, doat, dzat, ddnw, datw = norms_vjp((da, db))
        dx2_ref[...] = dx2
        dodn_ref[...] = dodn
        dzdn_ref[...] = dzdn
        dzat_ref[...] = dzat
        lane_head = (lax.broadcasted_iota(jnp.int32, (AT_WIDTH, HEAD_LANES), 0) // AT_DIM
                     == lax.broadcasted_iota(jnp.int32, (AT_WIDTH, HEAD_LANES), 1))
        delta = _dot_sel(lane_head.astype(f32), doat * o_at, NN, 1)
        for stage_ref, value, refs in ((stages[2], doat, (doat1_ref, doat4_ref, doat16_ref)),
                                       (stages[0], delta, (delta1_ref, delta4_ref, delta16_ref)),
                                       (stages[1], lse, (lse1_ref, lse4_ref, lse16_ref))):
            refs[0][...] = value.astype(refs[0].dtype)
            _stage_lanes(stage_ref, 0, value)
            _stage_to_view(stage_ref, refs[1], 4)
            _stage_to_view(stage_ref, refs[2], 16)
        gwt_ref[...] += dwt
        gwb_ref[...] += dwb
        rows_ref[0:1, :] += jnp.broadcast_to(loss, (1, D_MODEL))
        rows_ref[1:2, :] += dgate
        rows_ref[2:3, :] += dfw
        rows_ref[3:4, 0:DN_WIDTH] += ddnw
        rows_ref[4:5, 0:AT_WIDTH] += datw

    tile = lambda w: pl.BlockSpec((tm, w), lambda i: (i, 0))
    row = lambda w: pl.BlockSpec((1, w), lambda i: (0, 0))
    half_w = pl.BlockSpec((DN_WIDTH, D_MODEL), lambda i: (0, 0))
    sds = lambda w: jax.ShapeDtypeStruct((seq, w), f32)
    views = [_view_spec(AT_WIDTH, d) for d in PATTERN_DILATIONS]
    compact = [_view_spec(HEAD_LANES, d) for d in PATTERN_DILATIONS]
    doat_shapes = [_view_shape(seq, AT_WIDTH, d, bf16) for d in PATTERN_DILATIONS]
    compact_shapes = [_view_shape(seq, HEAD_LANES, d) for d in PATTERN_DILATIONS]
    return pl.pallas_call(
        body, name="tail", grid=(seq // tm,),
        in_specs=[tile(D_MODEL), tile(D_MODEL), tile(DN_WIDTH),
                  pl.BlockSpec((tm, DN_WIDTH), lambda i: (i, QKV_W // DN_WIDTH)),
                  *views, *compact, tile(AT_WIDTH), row(D_MODEL), half_w, pl.BlockSpec((AT_WIDTH, D_MODEL), lambda i: (1, 0)),
                  row(DN_WIDTH), row(AT_WIDTH), row(D_MODEL)],
        out_specs=[tile(D_MODEL), tile(DN_WIDTH), tile(DN_WIDTH), *views, *compact, *compact,
                   tile(AT_WIDTH), half_w, half_w, pl.BlockSpec((TAIL_ROWS, D_MODEL), lambda i: (0, 0))],
        out_shape=[sds(D_MODEL), sds(DN_WIDTH), sds(DN_WIDTH), *doat_shapes, *compact_shapes, *compact_shapes,
                   sds(AT_WIDTH), jax.ShapeDtypeStruct((DN_WIDTH, D_MODEL), f32),
                   jax.ShapeDtypeStruct((AT_WIDTH, D_MODEL), f32), jax.ShapeDtypeStruct((TAIL_ROWS, D_MODEL), f32)],
        scratch_shapes=[_stage(HEAD_LANES)] * 2 + [_stage(AT_WIDTH)] * 2,
        compiler_params=_params(("arbitrary",)),
    )(x, target, o_dn, pdn, outs[0], outs[1], outs[2], lses[0], lses[1], lses[2], z_at, gate, w_out, w_out,
      dn_w_row, at_w_row, fin_w)


PRE_ROWS = 8


def _pre_backward(x, dx2, pos_col, invf_row, sign_row, norm_w, scale, shift, w_cat, conv_w8, pdn, d_conv, d_zdn, d_ba,
                  d_q, d_k, d_v, d_zat):
    seq = x.shape[0]
    tm = TOKEN_TILE
    last = seq // tm - 1

    def body(x_ref, dx2_ref, pos_ref, invf_ref, sign_ref, nw_ref, sc_ref, sh_ref, w_ref, cw_ref,
             pre_ref, prehalo_ref, dc_ref, dchalo_ref, dz_ref, dba_ref,
             dq1_ref, dq4_ref, dq16_ref, dk1_ref, dk4_ref, dk16_ref, dv1_ref, dv4_ref, dv16_ref, dzat_ref,
             gx_ref, dproj_ref, rows_ref, crow_ref, ext_ref, *stages):
        i = pl.program_id(0)

        @pl.when(i == 0)
        def _():
            rows_ref[...] = jnp.zeros_like(rows_ref)
            crow_ref[...] = jnp.zeros_like(crow_ref)

        dc = dc_ref[...]
        ext_ref[0:tm, :] = dc
        ext_ref[tm:tm + 8, :] = jnp.where(i < last, dchalo_ref[...], 0.0)
        w8 = cw_ref[...]
        d_pre = ext_ref[pl.ds(3, tm), :] * w8[0:1, :]
        for j in range(1, CONV_K):
            d_pre = d_pre + ext_ref[pl.ds(3 - j, tm), :] * w8[j:j + 1, :]
        ext_ref[0:8, :] = jnp.where(i > 0, prehalo_ref[...], 0.0)
        ext_ref[8:8 + tm, :] = pre_ref[...]
        for j in range(CONV_K):
            crow_ref[j:j + 1, :] += jnp.sum(dc * ext_ref[pl.ds(5 + j, tm), :], axis=0, keepdims=True)

        cos_t, sin_t = _rope_tables(pos_ref[...], invf_ref[...], sign_ref[...])
        dq = dq1_ref[...] + _view_to_value(stages[0], dq4_ref, 4) + _view_to_value(stages[1], dq16_ref, 16)
        dk = dk1_ref[...] + _view_to_value(stages[2], dk4_ref, 4) + _view_to_value(stages[3], dk16_ref, 16)
        dq = dq * cos_t + _rope_partner(dq * sin_t)
        dk = dk * cos_t + _rope_partner(dk * sin_t)
        dv = dv1_ref[...] + _view_to_value(stages[4], dv4_ref, 4) + _view_to_value(stages[5], dv16_ref, 16)
        d_proj = jnp.concatenate([d_pre, dz_ref[...], dba_ref[...], dq, dk, dv, dzat_ref[...]], axis=1).astype(bf16)
        dproj_ref[...] = d_proj
        dh = lax.dot_general(d_proj, w_ref[...], (NT, ((), ())), preferred_element_type=f32)
        _, vjp = jax.vjp(_hnorm, x_ref[...], nw_ref[...], sc_ref[...], sh_ref[...])
        dx, dnw, dsc, dsh = vjp(dh)
        gx_ref[...] = dx + dx2_ref[...]
        rows_ref[0:1, :] += dnw
        rows_ref[1:2, :] += dsc
        rows_ref[2:3, :] += dsh

    tile = lambda w: pl.BlockSpec((tm, w), lambda i: (i, 0))
    row = lambda w: pl.BlockSpec((1, w), lambda i: (0, 0))
    step8 = tm // 8
    return pl.pallas_call(
        body, name="pre_backward", grid=(seq // tm,),
        in_specs=[tile(D_MODEL), tile(D_MODEL), tile(1), row(128), row(128), row(D_MODEL), row(D_MODEL), row(D_MODEL),
                  pl.BlockSpec((D_MODEL, CAT_W), lambda i: (0, 0)), pl.BlockSpec((8, QKV_W), lambda i: (0, 0)),
                  tile(QKV_W), pl.BlockSpec((8, QKV_W), lambda i: (jnp.maximum(i * step8 - 1, 0), 0)),
                  tile(QKV_W), pl.BlockSpec((8, QKV_W), lambda i: (jnp.minimum((i + 1) * step8, seq // 8 - 1), 0)),
                  tile(DN_WIDTH), tile(BA_W)] + [_view_spec(AT_WIDTH, d) for d in PATTERN_DILATIONS] * 3 + [tile(AT_WIDTH)],
        out_specs=[tile(D_MODEL), tile(CAT_W), pl.BlockSpec((PRE_ROWS, D_MODEL), lambda i: (0, 0)),
                   pl.BlockSpec((8, QKV_W), lambda i: (0, 0))],
        out_shape=[jax.ShapeDtypeStruct((seq, D_MODEL), f32), jax.ShapeDtypeStruct((seq, CAT_W), bf16),
                   jax.ShapeDtypeStruct((PRE_ROWS, D_MODEL), f32), jax.ShapeDtypeStruct((8, QKV_W), f32)],
        scratch_shapes=[pltpu.VMEM((tm + 8, QKV_W), f32)] + [_stage(AT_WIDTH)] * 6,
        compiler_params=_params(("arbitrary",)),
    )(x, dx2, pos_col, invf_row, sign_row, norm_w, scale, shift, w_cat, conv_w8, pdn, pdn, d_conv, d_conv, d_zdn, d_ba,
      d_q[0], d_q[1], d_q[2], d_k[0], d_k[1], d_k[2], d_v[0], d_v[1], d_v[2], d_zat)


def _weight_grad(h, d_proj):
    seq = h.shape[0]
    tk, tn = 512, CAT_W // 3
    n_k = seq // tk

    def body(h_ref, d_ref, o_ref):
        @pl.when(pl.program_id(1) == 0)
        def _():
            o_ref[...] = jnp.zeros_like(o_ref)

        o_ref[...] += lax.dot_general(h_ref[...], d_ref[...], (TN, ((), ())), preferred_element_type=f32)

    return pl.pallas_call(
        body, name="weight_grad", grid=(CAT_W // tn, n_k),
        in_specs=[pl.BlockSpec((tk, D_MODEL), lambda n, k: (k, 0)), pl.BlockSpec((tk, tn), lambda n, k: (k, n))],
        out_specs=pl.BlockSpec((D_MODEL, tn), lambda n, k: (0, n)),
        out_shape=jax.ShapeDtypeStruct((D_MODEL, CAT_W), f32),
        compiler_params=_params(("parallel", "arbitrary")),
    )(h, d_proj)


def _adamw(w, g, m, v):
    m = ADAM_B1 * m + (1.0 - ADAM_B1) * g
    v = ADAM_B2 * v + (1.0 - ADAM_B2) * (g * g)
    m_hat = m / (1.0 - ADAM_B1 ** ADAM_STEP)
    v_hat = v / (1.0 - ADAM_B2 ** ADAM_STEP)
    delta = -ADAM_LR * (m_hat / (jnp.sqrt(v_hat) + ADAM_EPS) + ADAM_WD * w)
    return delta, m, v


def _mod_shard(c_all, w_mod_s, b_mod_s):
    def body(c_ref, w_ref, b_ref, o_ref):
        o_ref[...] = _dot(_silu(c_ref[...]), w_ref[...], NN, True) + b_ref[...]

    return pl.pallas_call(body, name="mod_shard", out_shape=jax.ShapeDtypeStruct((N_DEV, w_mod_s.shape[1]), f32),
                          compiler_params=_params())(c_all, w_mod_s, b_mod_s)


def _mod_update(c_all, d_mod_s, w, m, v):
    def body(c_ref, d_ref, w_ref, m_ref, v_ref, g_ref, dw_ref, nm_ref, nv_ref):
        g = _dot(_silu(c_ref[...]), d_ref[...], TN, True)
        g_ref[...] = g
        dw_ref[...], nm_ref[...], nv_ref[...] = _adamw(w_ref[...], g, m_ref[...], v_ref[...])

    return pl.pallas_call(body, name="mod_update", out_shape=[jax.ShapeDtypeStruct(w.shape, f32)] * 4,
                          compiler_params=_params())(c_all, d_mod_s, w, m, v)


def _adamw_rows(w, g, m, v, name):
    rows, width = w.shape
    tr = 128 if rows % 128 == 0 else rows

    def body(w_ref, g_ref, m_ref, v_ref, dw_ref, nm_ref, nv_ref):
        dw_ref[...], nm_ref[...], nv_ref[...] = _adamw(w_ref[...], g_ref[...], m_ref[...], v_ref[...])

    spec = pl.BlockSpec((tr, width), lambda i: (i, 0))
    return pl.pallas_call(body, name=name, grid=(rows // tr,), in_specs=[spec] * 4, out_specs=[spec] * 3,
                          out_shape=[jax.ShapeDtypeStruct(w.shape, f32)] * 3,
                          compiler_params=_params(("parallel",)))(w, g, m, v)


SM_NORM, SM_FIN, SM_DN, SM_AT, SM_A, SM_DT, SM_LOSS, SM_MOD = 0, 1024, 2048, 2560, 3072, 3200, 3328, 3456
SM_W = SM_MOD + 3 * D_MODEL
RS_NORM, RS_FIN, RS_DN, RS_AT, RS_A, RS_DT, RS_BMOD = 0, 1024, 2048, 2176, 2304, 2432, 2560
RS_W = RS_BMOD + 3 * D_MODEL


def _small_update(gathered, w, m, v):
    def body(g_ref, w_ref, m_ref, v_ref, grad_ref, dw_ref, nm_ref, nv_ref, loss_ref, dmod_ref):
        total = g_ref[0:1, :]
        for dev in range(1, N_DEV):
            total = total + g_ref[8 * dev:8 * dev + 1, :]
        for dev in range(N_DEV):
            dmod_ref[dev:dev + 1, :] = g_ref[8 * dev:8 * dev + 1, SM_MOD:SM_W]
        dn = total[:, SM_DN:SM_DN + DN_DIM]
        for h in range(1, DN_HEADS):
            dn = dn + total[:, SM_DN + h * DN_DIM:SM_DN + (h + 1) * DN_DIM]
        at = total[:, SM_AT:SM_AT + AT_DIM]
        for h in range(1, AT_HEADS):
            at = at + total[:, SM_AT + h * AT_DIM:SM_AT + (h + 1) * AT_DIM]
        grad_ref[:, RS_NORM:RS_FIN] = total[:, SM_NORM:SM_FIN]
        grad_ref[:, RS_FIN:RS_DN] = total[:, SM_FIN:SM_DN]
        grad_ref[:, RS_DN:RS_AT] = dn
        grad_ref[:, RS_AT:RS_A] = jnp.zeros((1, 128), f32)
        grad_ref[:, RS_AT:RS_AT + AT_DIM] = at
        grad_ref[:, RS_A:RS_DT] = total[:, SM_A:SM_DT]
        grad_ref[:, RS_DT:RS_BMOD] = total[:, SM_DT:SM_LOSS]
        grad_ref[:, RS_BMOD:RS_W] = total[:, SM_MOD:SM_W]
        loss_ref[...] = total[:, SM_LOSS:SM_MOD]
        dw_ref[...], nm_ref[...], nv_ref[...] = _adamw(w_ref[...], grad_ref[...], m_ref[...], v_ref[...])

    row = jax.ShapeDtypeStruct((1, RS_W), f32)
    return pl.pallas_call(
        body, name="small_update",
        out_shape=[row, row, row, row, jax.ShapeDtypeStruct((1, 128), f32), jax.ShapeDtypeStruct((N_DEV, 3 * D_MODEL), f32)],
        compiler_params=_params())(gathered, w, m, v)


def _all_gather_rows(block, name):
    m_per, n = block.shape

    def body(x_ref, out_ref, send_sems, recv_sems, local_sem):
        x, y, c = lax.axis_index("x"), lax.axis_index("y"), lax.axis_index("c")
        me, sibling = (x, y, c), (x, y, 1 - c)
        chips = [(1 - x, y), (x, 1 - y), (1 - x, 1 - y)]

        def rows(px, py, pc):
            return out_ref.at[pl.ds((4 * px + 2 * py + pc) * m_per, m_per), :]

        def copy(k, blk, to, src=None):
            return pltpu.make_async_remote_copy(
                src_ref=rows(*blk) if src is None else src, dst_ref=rows(*blk),
                send_sem=send_sems.at[k], recv_sem=recv_sems.at[k], device_id=to, device_id_type=MESH)

        mine = pltpu.make_async_copy(x_ref, rows(*me), local_sem)
        mine.start()
        first = [copy(0, me, sibling, src=x_ref)]
        first += [copy(1 + j, me, (*chip, c), src=x_ref) for j, chip in enumerate(chips)]
        for cp in first:
            cp.start()
        passed = [copy(4 + j, (*chip, c), sibling) for j, chip in enumerate(chips)]
        for j, chip in enumerate(chips):
            copy(1 + j, (*chip, c), me).wait_recv()
            passed[j].start()
        copy(0, sibling, me).wait_recv()
        for j, chip in enumerate(chips):
            copy(4 + j, (*chip, 1 - c), me).wait_recv()
        for cp in first + passed:
            cp.wait_send()
        mine.wait()

    return pl.pallas_call(
        body, name=name,
        out_shape=jax.ShapeDtypeStruct((N_DEV * m_per, n), block.dtype),
        in_specs=[pl.BlockSpec(memory_space=pltpu.VMEM)],
        out_specs=pl.BlockSpec(memory_space=pltpu.VMEM),
        scratch_shapes=[pltpu.SemaphoreType.DMA((7,)), pltpu.SemaphoreType.DMA((7,)), pltpu.SemaphoreType.DMA],
        compiler_params=pltpu.CompilerParams(vmem_limit_bytes=VMEM_LIMIT),
    )(block)


def _gather_weight_shards(packed):
    rows, width = packed.shape

    def body(src_ref, out_ref, send_sems, recv_sems, local_sem):
        x, y, c = lax.axis_index("x"), lax.axis_index("y"), lax.axis_index("c")
        chips = [(1 - x, y), (x, 1 - y), (1 - x, 1 - y)]

        def copy(k, owner, to):
            slot = out_ref.at[2 * owner[0] + owner[1]]
            return pltpu.make_async_remote_copy(
                src_ref=src_ref, dst_ref=slot, send_sem=send_sems.at[k], recv_sem=recv_sems.at[k],
                device_id=(*to, c), device_id_type=MESH)

        mine = pltpu.make_async_copy(src_ref, out_ref.at[2 * x + y], local_sem)
        mine.start()
        sends = [copy(k, (x, y), chip) for k, chip in enumerate(chips)]
        for cp in sends:
            cp.start()
        for k, chip in enumerate(chips):
            copy(k, chip, (x, y)).wait_recv()
        for cp in sends:
            cp.wait_send()
        mine.wait()

    return pl.pallas_call(
        body, name="gather_weight_shards",
        out_shape=jax.ShapeDtypeStruct((N_CHIPS, rows, width), packed.dtype),
        in_specs=[pl.BlockSpec(memory_space=pl.ANY)],
        out_specs=pl.BlockSpec(memory_space=pl.ANY),
        scratch_shapes=[pltpu.SemaphoreType.DMA((3,)), pltpu.SemaphoreType.DMA((3,)), pltpu.SemaphoreType.DMA],
    )(packed)


def _reduce_weight_grads(grads, grads_bf):
    _, rows, width = grads.shape
    n_chunks = rows // PACK_CHUNK

    def body(g_ref, gbf_ref, out_ref, own_ref, land_ref, part_ref, sib_ref, send_sems, recv_sems, local_sem):
        x, y, c = lax.axis_index("x"), lax.axis_index("y"), lax.axis_index("c")
        chips = [(1 - x, y), (x, 1 - y), (1 - x, 1 - y)]

        def copy(k, chip):
            return pltpu.make_async_remote_copy(
                src_ref=gbf_ref.at[2 * chip[0] + chip[1]], dst_ref=land_ref.at[k],
                send_sem=send_sems.at[k], recv_sem=recv_sems.at[k], device_id=(*chip, c), device_id_type=MESH)

        mine = pltpu.make_async_copy(g_ref.at[2 * x + y], own_ref, local_sem)
        mine.start()
        sends = [copy(k, chip) for k, chip in enumerate(chips)]
        for cp in sends:
            cp.start()
        mine.wait()
        for cp in sends:
            cp.wait_recv()

        def add_landed(i, carry):
            r = pl.ds(pl.multiple_of(i * PACK_CHUNK, 16), PACK_CHUNK)
            landed = [land_ref[k, r, :].astype(f32) for k in range(3)]
            part_ref[r, :] = ((own_ref[r, :] + landed[0]) + landed[1]) + landed[2]
            return carry

        lax.fori_loop(0, n_chunks, add_landed, 0)
        swap = pltpu.make_async_remote_copy(
            src_ref=part_ref, dst_ref=sib_ref, send_sem=send_sems.at[3], recv_sem=recv_sems.at[3],
            device_id=(x, y, 1 - c), device_id_type=MESH)
        swap.start()
        swap.wait_recv()

        def add_sibling(i, carry):
            r = pl.ds(pl.multiple_of(i * PACK_CHUNK, 8), PACK_CHUNK)
            out_ref[r, :] = part_ref[r, :] + sib_ref[r, :]
            return carry

        lax.fori_loop(0, n_chunks, add_sibling, 0)
        swap.wait_send()
        for cp in sends:
            cp.wait_send()

    buf = pltpu.VMEM((rows, width), f32)
    return pl.pallas_call(
        body, name="reduce_weight_grads",
        out_shape=jax.ShapeDtypeStruct((rows, width), f32),
        in_specs=[pl.BlockSpec(memory_space=pl.ANY), pl.BlockSpec(memory_space=pl.ANY)],
        out_specs=pl.BlockSpec(memory_space=pltpu.VMEM),
        scratch_shapes=[buf, pltpu.VMEM((3, rows, width), bf16), buf, buf,
                        pltpu.SemaphoreType.DMA((4,)), pltpu.SemaphoreType.DMA((4,)), pltpu.SemaphoreType.DMA],
        compiler_params=pltpu.CompilerParams(vmem_limit_bytes=VMEM_LIMIT),
    )(grads, grads_bf)


def _pack_shards(w_in_s, w_out_s, conv_s):
    lead = w_in_s.shape[:-2]
    conv_rows = jnp.pad(conv_s.reshape(*lead, CONV_K * 384), [(0, 0)] * len(lead) + [(0, 8 * D_MODEL - CONV_K * 384)])
    return jnp.concatenate([
        w_in_s.reshape(*lead, SHARD_IN, D_MODEL), w_out_s, conv_rows.reshape(*lead, 8, D_MODEL),
        jnp.zeros((*lead, PACK_ROWS - SHARD_IN - 256 - 8, D_MODEL), f32)], axis=-2)


CONV_WORDS = 2 * CONV_K * 384


def _pack_weights_bf16(w_in_s, w_out_s, conv_s):
    conv_bits = lax.bitcast_convert_type(conv_s, bf16).reshape(CONV_WORDS)
    return jnp.concatenate([
        w_in_s.astype(bf16).reshape(SHARD_IN, D_MODEL), w_out_s.astype(bf16),
        jnp.pad(conv_bits, (0, 8 * D_MODEL - CONV_WORDS)).reshape(8, D_MODEL),
        jnp.zeros((PACK_ROWS - SHARD_IN - 256 - 8, D_MODEL), bf16)], axis=0)


def _unpack_weights_bf16(shards):
    w_in_s = shards[:, :SHARD_IN].reshape(N_CHIPS, D_MODEL, SHARD_IN)
    w_out = shards[:, SHARD_IN:SHARD_IN + 256].reshape(D_MODEL, D_MODEL)
    conv_bits = shards[:, SHARD_IN + 256:SHARD_IN + 264].reshape(N_CHIPS, 8 * D_MODEL)[:, :CONV_WORDS]
    conv_s = lax.bitcast_convert_type(conv_bits.reshape(N_CHIPS, CONV_K, 384, 2), f32)
    return (jnp.transpose(w_in_s, (1, 0, 2)).reshape(D_MODEL, IN_COLS), w_out,
            jnp.transpose(conv_s, (1, 0, 2)).reshape(CONV_K, QKV_W))


def _unpack_shards(packed):
    lead = packed.shape[:-2]
    w_in_s = packed[..., :SHARD_IN, :].reshape(*lead, D_MODEL, SHARD_IN)
    w_out_s = packed[..., SHARD_IN:SHARD_IN + 256, :]
    conv_s = packed[..., SHARD_IN + 256:SHARD_IN + 264, :].reshape(*lead, 8 * D_MODEL)[..., :CONV_K * 384]
    return w_in_s, w_out_s, conv_s.reshape(*lead, CONV_K, 384)


def _local_step(x, target, pos_col, mod_row, norm_w, w_in, conv_w, a_log, dt_bias, dn_norm_w, at_norm_w, w_out, fin_w):
    shift, scale, gate = mod_row[:, :D_MODEL], mod_row[:, D_MODEL:2 * D_MODEL], mod_row[:, 2 * D_MODEL:]
    half = AT_DIM // 2
    lane = jnp.arange(128)
    inv_freq = ROPE_THETA ** (-jnp.arange(half, dtype=f32) / half)
    invf_row = inv_freq[lane % half].reshape(1, 128)
    sign_row = jnp.where((lane % AT_DIM) < half, -1.0, 1.0).astype(f32).reshape(1, 128)
    ba_w = jnp.pad(w_in[:, 2 * D_MODEL:2 * D_MODEL + 2 * DN_HEADS], ((0, 0), (0, BA_W - 2 * DN_HEADS)))
    w_cat = jnp.concatenate([w_in[:, :2 * D_MODEL], ba_w, w_in[:, 2 * D_MODEL + 2 * DN_HEADS:]], axis=1).astype(bf16)
    conv_w8 = jnp.pad(conv_w, ((0, 8 - CONV_K), (0, 0)))
    a_row = jnp.pad(a_log.reshape(1, DN_HEADS), ((0, 0), (DN_HEADS, BA_W - 2 * DN_HEADS)))
    dt_row = jnp.pad(dt_bias.reshape(1, DN_HEADS), ((0, 0), (DN_HEADS, BA_W - 2 * DN_HEADS)))
    dn_w_row = jnp.tile(dn_norm_w.reshape(1, DN_DIM), (1, DN_HEADS))
    at_w_row = jnp.tile(at_norm_w.reshape(1, AT_DIM), (1, AT_HEADS))
    norm_row = norm_w.reshape(1, D_MODEL)
    fin_row = fin_w.reshape(1, D_MODEL)
    w_out_bf = w_out.astype(bf16)

    pdn, *views, z_at, h = _pre_proj(x, pos_col, invf_row, sign_row, norm_row, scale, shift, w_cat)
    *dn_parts, dn_invs = _dn_intra_forward(pdn, conv_w8, a_row, dt_row)
    o_dn, states = _dn_scan_forward(*dn_parts)
    outs, lses = [], []
    for d, view in zip(PATTERN_DILATIONS, views):
        o, lse = _attn_forward(view, d)
        outs.append(o)
        lses.append(lse)
    (dx2, d_odn, d_zdn, *in_views, d_zat, g_wtop, g_wbot, tail_rows) = _tail(
        x, target, o_dn, pdn, outs, lses, z_at, gate, w_out_bf, dn_w_row, at_w_row, fin_row)
    d_oat, delta, lse_all = in_views[0:3], in_views[3:6], in_views[6:9]
    d_q, d_k, d_v = [], [], []
    for n, (d, view) in enumerate(zip(PATTERN_DILATIONS, views)):
        args = (d_oat[n], delta[n], lse_all[n])
        d_q.append(_attn_backward_q(view, *args, d))
        dk, dv = _attn_backward_kv(view, *args, d)
        d_k.append(dk)
        d_v.append(dv)
    d_parts = _dn_scan_backward(*dn_parts, states, d_odn)
    d_conv, d_ba, dn_rows = _dn_intra_backward(pdn, conv_w8, a_row, dt_row, dn_invs, *d_parts)
    grad_x, d_proj, pre_rows, conv_rows = _pre_backward(
        x, dx2, pos_col, invf_row, sign_row, norm_row, scale, shift, w_cat, conv_w8, pdn, d_conv, d_zdn, d_ba,
        d_q, d_k, d_v, d_zat)
    g_cat = _weight_grad(h, d_proj)
    g_w_in = jnp.concatenate([g_cat[:, :2 * D_MODEL], g_cat[:, 2 * D_MODEL:2 * D_MODEL + 2 * DN_HEADS],
                              g_cat[:, 2 * D_MODEL + BA_W:]], axis=1)
    g_w_out = jnp.concatenate([g_wtop, g_wbot], axis=0)
    g_conv = conv_rows[:CONV_K]
    small = jnp.concatenate([
        pre_rows[0:1], tail_rows[2:3], tail_rows[3:4, :DN_WIDTH], tail_rows[4:5, :AT_WIDTH],
        dn_rows[0:1], dn_rows[1:2], tail_rows[0:1, :128],
        pre_rows[2:3], pre_rows[1:2], tail_rows[1:2]], axis=1)
    return grad_x, g_w_in, g_w_out, g_conv, small


def kernel(x, c, positions, w_mod, b_mod, norm_w, w_in, conv_w, a_log, dt_bias, dn_norm_w, at_norm_w, w_out, final_norm_w, loss_target, m_w_mod, m_b_mod, m_norm_w, m_w_in, m_conv_w, m_a_log, m_dt_bias, m_dn_norm_w, m_at_norm_w, m_w_out, m_final_norm_w, v_w_mod, v_b_mod, v_norm_w, v_w_in, v_conv_w, v_a_log, v_dt_bias, v_dn_norm_w, v_at_norm_w, v_w_out, v_final_norm_w):
    seq = x.shape[1]
    ax, ay, ac = lax.axis_index("x"), lax.axis_index("y"), lax.axis_index("c")
    me = 4 * ax + 2 * ay + ac
    chip = 2 * ax + ay

    c_all = _all_gather_rows(jnp.pad(c, ((0, 7), (0, 0))), "gather_c").reshape(N_DEV, 8, D_MODEL)[:, 0]
    b_mod_s = lax.dynamic_slice_in_dim(b_mod, chip * 768, 768, axis=1)
    mod_part = _mod_shard(c_all, w_mod[0], b_mod_s)
    mod_all = _all_gather_rows(mod_part, "gather_mod").reshape(N_CHIPS, 2, N_DEV, 768)[:, 0]
    mod_row = lax.dynamic_index_in_dim(mod_all, me, axis=1, keepdims=False).reshape(1, 3 * D_MODEL)

    shards = _gather_weight_shards(_pack_weights_bf16(w_in[0], w_out[0], conv_w[0]))
    w_in_full, w_out_full, conv_full = _unpack_weights_bf16(shards)

    grad_x, g_w_in, g_w_out, g_conv, small = _local_step(
        x[0], loss_target[0], positions.reshape(seq, 1), mod_row, norm_w, w_in_full, conv_full, a_log, dt_bias,
        dn_norm_w, at_norm_w, w_out_full, final_norm_w)

    g_pack = _pack_shards(jnp.transpose(g_w_in.reshape(D_MODEL, N_CHIPS, SHARD_IN), (1, 0, 2)),
                          g_w_out.reshape(N_CHIPS, 256, D_MODEL),
                          jnp.transpose(g_conv.reshape(CONV_K, N_CHIPS, 384), (1, 0, 2)))
    g_mine = _reduce_weight_grads(g_pack, g_pack.astype(bf16))
    grad_w_in, grad_w_out, grad_conv_w = _unpack_shards(g_mine)
    delta_w_in, new_m_w_in, new_v_w_in = _adamw_rows(w_in[0], grad_w_in, m_w_in[0], v_w_in[0], "adamw_w_in")
    delta_w_out, new_m_w_out, new_v_w_out = _adamw_rows(w_out[0], grad_w_out, m_w_out[0], v_w_out[0], "adamw_w_out")
    delta_conv_w, new_m_conv_w, new_v_conv_w = _adamw_rows(conv_w[0], grad_conv_w, m_conv_w[0], v_conv_w[0], "adamw_conv_w")

    gathered = _all_gather_rows(jnp.pad(small, ((0, 7), (0, 0))), "gather_small")

    def small_row(norm, fin, dn, at, a, dt, bmod):
        z = lambda n: jnp.zeros((1, n), f32)
        return jnp.concatenate([norm.reshape(1, -1), fin.reshape(1, -1), dn.reshape(1, -1), at.reshape(1, -1), z(64),
                                z(4), a.reshape(1, -1), z(120), z(4), dt.reshape(1, -1), z(120), bmod.reshape(1, -1)], axis=1)

    g_small, d_small, m_small, v_small, loss_row, d_mod_all = _small_update(
        gathered,
        small_row(norm_w, final_norm_w, dn_norm_w, at_norm_w, a_log, dt_bias, b_mod),
        small_row(m_norm_w, m_final_norm_w, m_dn_norm_w, m_at_norm_w, m_a_log, m_dt_bias, m_b_mod),
        small_row(v_norm_w, v_final_norm_w, v_dn_norm_w, v_at_norm_w, v_a_log, v_dt_bias, v_b_mod))

    def split_small(r):
        return (r[:, RS_BMOD:RS_W], r[:, RS_NORM:RS_FIN], r[:, RS_A + DN_HEADS:RS_A + 2 * DN_HEADS],
                r[:, RS_DT + DN_HEADS:RS_DT + 2 * DN_HEADS], r[:, RS_DN:RS_AT], r[:, RS_AT:RS_AT + AT_DIM],
                r[0, RS_FIN:RS_DN])

    d_mod_s = lax.dynamic_slice_in_dim(d_mod_all, chip * 768, 768, axis=1)
    pad_rows = lambda a: jnp.pad(a, ((0, 128 - N_DEV), (0, 0)))
    grad_w_mod, delta_w_mod, new_m_w_mod, new_v_w_mod = _mod_update(
        pad_rows(c_all), pad_rows(d_mod_s), w_mod[0], m_w_mod[0], v_w_mod[0])

    def ordered(w_mod_leaf, small_row_leaf, w_in_leaf, conv_leaf, w_out_leaf):
        b, n, a, dt, dn, at, fin = split_small(small_row_leaf)
        return [w_mod_leaf[None], b, n, w_in_leaf[None], conv_leaf[None], a, dt, dn, at, w_out_leaf[None], fin]

    loss = loss_row[0, 0]
    return (loss, grad_x[None],
            *ordered(grad_w_mod, g_small, grad_w_in, grad_conv_w, grad_w_out),
            *ordered(delta_w_mod, d_small, delta_w_in, delta_conv_w, delta_w_out),
            *ordered(new_m_w_mod, m_small, new_m_w_in, new_m_conv_w, new_m_w_out),
            *ordered(new_v_w_mod, v_small, new_v_w_in, new_v_conv_w, new_v_w_out))
```

```python
import functools

import jax
import jax.numpy as jnp
from jax import lax
from jax.experimental import pallas as pl
from jax.experimental.pallas import tpu as pltpu

f32 = jnp.float32
bf16 = jnp.bfloat16
HIGHEST = lax.Precision.HIGHEST
MESH = pl.DeviceIdType.MESH

D_MODEL = 1024
DN_HEADS = 4
DN_DIM = 128
DN_WIDTH = 512
AT_HEADS = 8
AT_DIM = 64
AT_WIDTH = 512
CHUNK = 64
Q_BLOCK = 128
CONV_K = 4
EPS = 1e-6
ROPE_THETA = 10000.0
PATTERN_DILATIONS = (1, 4, 16)
NEG = -1e30

QKV_W = 3 * DN_WIDTH
BA_W = 128
PDN_W = QKV_W + DN_WIDTH + BA_W
PAT_W = 3 * AT_WIDTH
CAT_W = PDN_W + PAT_W + AT_WIDTH
IN_COLS = 4104
N_CHIPS = 4
N_DEV = 8
SHARD_IN = IN_COLS // N_CHIPS
OUT_ROWS = D_MODEL // N_CHIPS
PACK_W = 1152
PACK_ROWS = 1312
PACK_CHUNK = 16

ADAM_LR = 0.001
ADAM_B1 = 0.9
ADAM_B2 = 0.999
ADAM_EPS = 1e-08
ADAM_WD = 0.01
ADAM_STEP = 10

VMEM_LIMIT = 56 * 1024 * 1024

NN = ((1,), (0,))
NT = ((1,), (1,))
TN = ((0,), (0,))


def _pieces(a, n):
    out = []
    for _ in range(n - 1):
        p = a.astype(bf16)
        out.append(p)
        a = a - p.astype(f32)
    out.append(a.astype(bf16))
    return out


def _dot(a, b, dims, exact):
    raw = lambda p, q: lax.dot_general(p, q, (dims, ((), ())), preferred_element_type=f32)
    if exact == "split":
        (ah, al), (bh, bl) = _pieces(a, 2), _pieces(b, 2)
        return raw(ah, bh) + (raw(ah, bl) + raw(al, bh))
    if exact:
        return lax.dot_general(a, b, (dims, ((), ())), precision=HIGHEST, preferred_element_type=f32)
    return raw(a.astype(bf16), b.astype(bf16))


def _dot_sel(sel, b, dims, sel_side):
    raw = lambda p, q: lax.dot_general(p, q, (dims, ((), ())), preferred_element_type=f32)
    sel = sel.astype(bf16)
    parts = [raw(sel, p) if sel_side == 0 else raw(p, sel) for p in _pieces(b, 3)]
    return (parts[0] + parts[1]) + parts[2]


@jax.custom_vjp
def _sel_left(sel, b):
    return _dot_sel(sel, b, NN, 0)


def _sel_left_fwd(sel, b):
    return _dot_sel(sel, b, NN, 0), sel


def _sel_left_bwd(sel, g):
    return jnp.zeros_like(sel), _dot_sel(sel, g, TN, 0)


_sel_left.defvjp(_sel_left_fwd, _sel_left_bwd)


@jax.custom_vjp
def _sel_right(a, sel):
    return _dot_sel(sel, a, NN, 1)


def _sel_right_fwd(a, sel):
    return _dot_sel(sel, a, NN, 1), sel


def _sel_right_bwd(sel, g):
    return _dot_sel(sel, g, NT, 1), jnp.zeros_like(sel)


_sel_right.defvjp(_sel_right_fwd, _sel_right_bwd)


class _Matmuls:
    def __init__(self, exact, back):
        @jax.custom_vjp
        def nn(a, b):
            return _dot(a, b, NN, exact)

        def nn_fwd(a, b):
            return _dot(a, b, NN, exact), (a, b)

        def nn_bwd(res, g):
            a, b = res
            return _dot(g, b, NT, back), _dot(a, g, TN, back)

        nn.defvjp(nn_fwd, nn_bwd)

        @jax.custom_vjp
        def nt(a, b):
            return _dot(a, b, NT, exact)

        def nt_fwd(a, b):
            return _dot(a, b, NT, exact), (a, b)

        def nt_bwd(res, g):
            a, b = res
            return _dot(g, b, NN, back), _dot(g, a, TN, back)

        nt.defvjp(nt_fwd, nt_bwd)

        @jax.custom_vjp
        def tn(a, b):
            return _dot(a, b, TN, exact)

        def tn_fwd(a, b):
            return _dot(a, b, TN, exact), (a, b)

        def tn_bwd(res, g):
            a, b = res
            return _dot(b, g, NT, back), _dot(a, g, NN, back)

        tn.defvjp(tn_fwd, tn_bwd)
        self.nn, self.nt, self.tn = nn, nt, tn


MM = _Matmuls(exact=False, back=False)
MS = _Matmuls(exact="split", back=False)


def _each(fn, *lists):
    return [fn(*args) for args in zip(*lists)]


def _inverse_products(a_lows):
    ri = lax.broadcasted_iota(jnp.int32, (CHUNK, CHUNK), 0)
    ci = lax.broadcasted_iota(jnp.int32, (CHUNK, CHUNK), 1)
    eye = (ri == ci).astype(f32)
    power = _each(lambda a: -a, a_lows)
    inv = _each(lambda p: eye + p, power)
    for _ in range(5):
        power = _each(lambda p: _dot(p, p, NN, "split"), power)
        inv = _each(lambda x, p: x + _dot(x, p, NN, "split"), inv, power)
    return inv


def _inverse_cotangents(invs, gs):
    left = _each(lambda t, g: _dot(t, g, TN, "split"), invs, gs)
    return _each(lambda l, t: -_dot(l, t, NT, "split"), left, invs)


@jax.custom_vjp
def _unit_lower_inverses(a_lows):
    return _inverse_products(a_lows)


def _unit_lower_inverses_fwd(a_lows):
    invs = _inverse_products(a_lows)
    return invs, invs


def _unit_lower_inverses_bwd(invs, gs):
    return (_inverse_cotangents(invs, gs),)


_unit_lower_inverses.defvjp(_unit_lower_inverses_fwd, _unit_lower_inverses_bwd)


@jax.custom_vjp
def _known_inverses(a_lows, invs):
    return invs


def _known_inverses_fwd(a_lows, invs):
    return invs, invs


def _known_inverses_bwd(invs, gs):
    return _inverse_cotangents(invs, gs), _each(jnp.zeros_like, invs)


_known_inverses.defvjp(_known_inverses_fwd, _known_inverses_bwd)


def _params(semantics=None):
    return pltpu.CompilerParams(dimension_semantics=semantics, vmem_limit_bytes=VMEM_LIMIT)


def _silu(x):
    return x * jax.nn.sigmoid(x)


def _group_ones(width, group):
    r = lax.broadcasted_iota(jnp.int32, (width, width), 0) // group
    c = lax.broadcasted_iota(jnp.int32, (width, width), 1) // group
    return (r == c).astype(f32)


def _slab_sums(x):
    rows, width = x.shape
    sums = [jnp.sum(x[:, j:j + 128], axis=1, keepdims=True) for j in range(0, width, 128)]
    return jnp.concatenate([jnp.broadcast_to(s, (rows, 128)) for s in sums], axis=1)


def _hnorm(x, nw, scale, shift):
    xn = x * lax.rsqrt(jnp.mean(x * x, axis=-1, keepdims=True) + EPS)
    return xn * nw * (1.0 + scale) + shift


def _rope_tables(pos_col, invf_row, sign_row):
    ang = pos_col.astype(f32) * invf_row
    cos_t = jnp.cos(ang)
    sin_t = jnp.sin(ang) * sign_row
    return jnp.concatenate([cos_t] * 4, axis=1), jnp.concatenate([sin_t] * 4, axis=1)


def _rope_partner(x):
    lane = lax.broadcasted_iota(jnp.int32, x.shape, 1)
    width = x.shape[1]
    return jnp.where((lane % AT_DIM) < AT_DIM // 2, pltpu.roll(x, width - AT_DIM // 2, 1), pltpu.roll(x, AT_DIM // 2, 1))


TOKEN_TILE = 256
LANES = 128


def _stage_lanes(stage_ref, first, value):
    for j in range(value.shape[1] // LANES):
        stage_ref[first + j] = value[:, LANES * j:LANES * (j + 1)]


def _stage_to_view(stage_ref, view_ref, dil):
    chunks, rows, _ = stage_ref.shape
    for r in range(dil):
        for j in range(chunks):
            col = (r * chunks + j) * LANES
            view_ref[:, col:col + LANES] = stage_ref.at[j][pl.ds(r, rows // dil, stride=dil), :].astype(view_ref.dtype)


def _view_to_value(stage_ref, view_ref, dil):
    chunks, rows, _ = stage_ref.shape
    for r in range(dil):
        for j in range(chunks):
            col = (r * chunks + j) * LANES
            stage_ref.at[j][pl.ds(r, rows // dil, stride=dil), :] = view_ref[:, col:col + LANES].astype(f32)
    return jnp.concatenate([stage_ref[j] for j in range(chunks)], axis=1)


def _view_spec(width, dil):
    return pl.BlockSpec((TOKEN_TILE // dil, dil * width), lambda i: (i, 0))


def _view_shape(seq, width, dil, dtype=f32):
    return jax.ShapeDtypeStruct((seq // dil, dil * width), dtype)


HEAD_LANES = 128


def _repeat_heads(compact):
    row = lax.broadcasted_iota(jnp.int32, (HEAD_LANES, AT_WIDTH), 0)
    col = lax.broadcasted_iota(jnp.int32, (HEAD_LANES, AT_WIDTH), 1)
    return _dot_sel((row == col // AT_DIM).astype(f32), compact, NN, 1)


def _stage(width):
    return pltpu.VMEM((width // LANES, TOKEN_TILE, LANES), f32)


def _pre_proj(x, pos_col, invf_row, sign_row, norm_w, scale, shift, w_cat):
    seq = x.shape[0]
    tm = TOKEN_TILE

    def body(x_ref, pos_ref, invf_ref, sign_ref, nw_ref, sc_ref, sh_ref, w_ref,
             pdn_ref, pat1_ref, pat4_ref, pat16_ref, zat_ref, h_ref, stage_ref):
        h = _hnorm(x_ref[...], nw_ref[...], sc_ref[...], sh_ref[...]).astype(bf16)
        h_ref[...] = h
        big = jnp.dot(h, w_ref[...], preferred_element_type=f32)
        pdn_ref[...] = big[:, :PDN_W]
        cos_t, sin_t = _rope_tables(pos_ref[...], invf_ref[...], sign_ref[...])
        q = big[:, PDN_W:PDN_W + AT_WIDTH]
        k = big[:, PDN_W + AT_WIDTH:PDN_W + 2 * AT_WIDTH]
        parts = (q * cos_t + _rope_partner(q) * sin_t, k * cos_t + _rope_partner(k) * sin_t,
                 big[:, PDN_W + 2 * AT_WIDTH:PDN_W + PAT_W])
        for p, part in enumerate(parts):
            pat1_ref[:, p * AT_WIDTH:(p + 1) * AT_WIDTH] = part.astype(bf16)
            _stage_lanes(stage_ref, p * (AT_WIDTH // LANES), part)
        _stage_to_view(stage_ref, pat4_ref, 4)
        _stage_to_view(stage_ref, pat16_ref, 16)
        zat_ref[...] = big[:, PDN_W + PAT_W:]

    row = lambda w: pl.BlockSpec((1, w), lambda i: (0, 0))
    tile = lambda w: pl.BlockSpec((tm, w), lambda i: (i, 0))
    return pl.pallas_call(
        body, name="pre_proj", grid=(seq // tm,),
        in_specs=[tile(D_MODEL), tile(1), row(128), row(128), row(D_MODEL), row(D_MODEL), row(D_MODEL),
                  pl.BlockSpec((D_MODEL, CAT_W), lambda i: (0, 0))],
        out_specs=[tile(PDN_W), tile(PAT_W), _view_spec(PAT_W, 4), _view_spec(PAT_W, 16), tile(AT_WIDTH), tile(D_MODEL)],
        out_shape=[jax.ShapeDtypeStruct((seq, PDN_W), f32), _view_shape(seq, PAT_W, 1, bf16), _view_shape(seq, PAT_W, 4, bf16),
                   _view_shape(seq, PAT_W, 16, bf16), jax.ShapeDtypeStruct((seq, AT_WIDTH), f32),
                   jax.ShapeDtypeStruct((seq, D_MODEL), bf16)],
        scratch_shapes=[_stage(PAT_W)],
        compiler_params=_params(("parallel",)),
    )(x, pos_col, invf_row, sign_row, norm_w, scale, shift, w_cat)


def _conv_taps(ext_ref, halo, cur, w8):
    rows = cur.shape[0]
    ext_ref[0:8, :] = halo
    ext_ref[8:8 + rows, :] = cur
    out = ext_ref[pl.ds(5, rows), :] * w8[0:1, :]
    for j in range(1, CONV_K):
        out = out + ext_ref[pl.ds(5 + j, rows), :] * w8[j:j + 1, :]
    return out


def _dn_pre(cq, ck, cv, ba, a_row, dt_row):
    sq, sk, v = _silu(cq), _silu(ck), _silu(cv)
    qn = sq * lax.rsqrt(_slab_sums(sq * sq) + EPS)
    kn = sk * lax.rsqrt(_slab_sums(sk * sk) + EPS)
    beta_all = jax.nn.sigmoid(ba)
    g_all = -jnp.exp(a_row) * jax.nn.softplus(ba + dt_row)
    return qn, kn, v, beta_all, g_all


def _dn_intra(qs, ks, vs, betas, gs, known_invs=None):
    ri = lax.broadcasted_iota(jnp.int32, (CHUNK, CHUNK), 0)
    ci = lax.broadcasted_iota(jnp.int32, (CHUNK, CHUNK), 1)
    tril = ri >= ci
    strict = ri > ci
    lower = tril.astype(f32)
    each = _each
    g_wide = each(lambda g: jnp.broadcast_to(g, (CHUNK, DN_DIM)), gs)
    gc = each(lambda g: _sel_left(lower, g), g_wide)
    gc_sq = each(lambda c: c[:, :CHUNK], gc)
    g_end = each(lambda c: jnp.broadcast_to(c[CHUNK - 1:CHUNK, :], (CHUNK, DN_DIM)), gc)
    g_end8 = each(lambda c: jnp.broadcast_to(c[CHUNK - 1:CHUNK, :], (8, DN_DIM)), gc)
    decay = each(lambda s: jnp.exp(jnp.where(tril, s - s.T, -jnp.inf)), gc_sq)
    qs = each(lambda q: q * (DN_DIM ** -0.5), qs)
    kb = each(lambda k, b: k * b, ks, betas)
    vb = each(lambda v, b: v * b, vs, betas)
    kk = each(MM.nt, kb, ks)
    qk = each(MM.nt, qs, ks)
    a_low = each(lambda p, d: jnp.where(strict, p * d, 0.0), kk, decay)
    inv = _unit_lower_inverses(a_low) if known_invs is None else _known_inverses(a_low, known_invs)
    e_gc = each(jnp.exp, gc)
    u = each(MS.nn, inv, vb)
    w = each(lambda x, k, e: MS.nn(x, k * e), inv, kb, e_gc)
    attn = each(lambda p, d: jnp.where(tril, p * d, 0.0), qk, decay)
    q_dec = each(lambda q, e: q * e, qs, e_gc)
    k_dec = each(lambda k, ge, c: k * jnp.exp(ge - c), ks, g_end, gc)
    return u, w, q_dec, k_dec, attn, each(jnp.exp, g_end8), inv


def _dn_step(us, ws, q_decs, k_decs, attns, e_ends, states):
    each = _each
    v_new = each(lambda u, w, s: u - MM.nn(w, s), us, ws, states)
    qs = each(MM.nn, q_decs, states)
    o = each(lambda a, b, c: a + MM.nn(b, c), qs, attns, v_new)
    new_states = each(lambda s, e, k, v: s * e + MM.tn(k, v), states, e_ends, k_decs, v_new)
    return o, new_states


INTRA_CHUNKS = 2
SCAN_CHUNKS = 4


def _intra_specs(nc):
    rows = INTRA_CHUNKS * CHUNK
    cur = pl.BlockSpec((rows, QKV_W), lambda i: (i, 0))
    halo = pl.BlockSpec((8, QKV_W), lambda i: (jnp.maximum(i * (rows // 8) - 1, 0), 0))
    ba = pl.BlockSpec((rows, BA_W), lambda i: (i, (QKV_W + DN_WIDTH) // BA_W))
    conv = pl.BlockSpec((8, QKV_W), lambda i: (0, 0))
    row = pl.BlockSpec((1, BA_W), lambda i: (0, 0))
    wide = pl.BlockSpec((rows, DN_WIDTH), lambda i: (i, 0))
    attn = pl.BlockSpec((INTRA_CHUNKS, DN_HEADS, CHUNK, CHUNK), lambda i: (i, 0, 0, 0))
    e_end = pl.BlockSpec((INTRA_CHUNKS, 8 * DN_HEADS, DN_DIM), lambda i: (i, 0, 0))
    return cur, halo, ba, conv, row, wide, attn, e_end


def _intra_items():
    return [(ci, h, slice(ci * CHUNK, (ci + 1) * CHUNK), slice(h * DN_DIM, (h + 1) * DN_DIM))
            for ci in range(INTRA_CHUNKS) for h in range(DN_HEADS)]


def _intra_inputs(items, qn, kn, v, beta_all, g_all):
    return ([qn[rows, lanes] for _, _, rows, lanes in items], [kn[rows, lanes] for _, _, rows, lanes in items],
            [v[rows, lanes] for _, _, rows, lanes in items], [beta_all[rows, h:h + 1] for _, h, rows, _ in items],
            [g_all[rows, DN_HEADS + h:DN_HEADS + h + 1] for _, h, rows, _ in items])


def _intra_shapes(seq, operand_dtype):
    nc = seq // CHUNK
    wide = jax.ShapeDtypeStruct((seq, DN_WIDTH), f32)
    operand = jax.ShapeDtypeStruct((seq, DN_WIDTH), operand_dtype)
    return [wide, operand, operand, operand, jax.ShapeDtypeStruct((nc, DN_HEADS, CHUNK, CHUNK), f32),
            jax.ShapeDtypeStruct((nc, 8 * DN_HEADS, DN_DIM), f32)]


def _dn_intra_forward(pdn, conv_w8, a_row, dt_row):
    seq = pdn.shape[0]
    nc = seq // CHUNK
    cur, halo, ba, conv, row, wide, attn, e_end = _intra_specs(nc)

    def body(cur_ref, halo_ref, ba_ref, w_ref, a_ref, dt_ref, u_ref, w_out_ref, qd_ref, kd_ref, attn_ref, e_ref, inv_ref,
             ext_ref):
        halo_rows = jnp.where(pl.program_id(0) > 0, halo_ref[...], 0.0)
        c = _conv_taps(ext_ref, halo_rows, cur_ref[...], w_ref[...])
        qn, kn, v, beta_all, g_all = _dn_pre(c[:, :DN_WIDTH], c[:, DN_WIDTH:2 * DN_WIDTH], c[:, 2 * DN_WIDTH:],
                                             ba_ref[...], a_ref[...], dt_ref[...])
        items = _intra_items()
        u, w, qd, kd, at, e8, inv = _dn_intra(*_intra_inputs(items, qn, kn, v, beta_all, g_all))
        for n, (ci, h, rows, lanes) in enumerate(items):
            u_ref[rows, lanes] = u[n]
            w_out_ref[rows, lanes] = w[n].astype(bf16)
            qd_ref[rows, lanes] = qd[n].astype(bf16)
            kd_ref[rows, lanes] = kd[n].astype(bf16)
            attn_ref[ci, h] = at[n]
            e_ref[ci, 8 * h:8 * h + 8, :] = e8[n]
            inv_ref[ci, h] = inv[n]

    return pl.pallas_call(
        body, name="dn_intra_forward", grid=(nc // INTRA_CHUNKS,),
        in_specs=[cur, halo, ba, conv, row, row],
        out_specs=[wide, wide, wide, wide, attn, e_end, attn],
        out_shape=_intra_shapes(seq, bf16) + [jax.ShapeDtypeStruct((nc, DN_HEADS, CHUNK, CHUNK), f32)],
        scratch_shapes=[pltpu.VMEM((INTRA_CHUNKS * CHUNK + 8, QKV_W), f32)],
        compiler_params=_params(("parallel",)),
    )(pdn, pdn, pdn, conv_w8, a_row, dt_row)


def _scan_specs(nc, reverse):
    steps = nc // SCAN_CHUNKS
    at = (lambda i: steps - 1 - i) if reverse else (lambda i: i)
    wide = pl.BlockSpec((SCAN_CHUNKS * CHUNK, DN_WIDTH), lambda i: (at(i), 0))
    attn = pl.BlockSpec((SCAN_CHUNKS, DN_HEADS, CHUNK, CHUNK), lambda i: (at(i), 0, 0, 0))
    e_end = pl.BlockSpec((SCAN_CHUNKS, 8 * DN_HEADS, DN_DIM), lambda i: (at(i), 0, 0))
    states = pl.BlockSpec((SCAN_CHUNKS, DN_HEADS, DN_DIM, DN_DIM), lambda i: (at(i), 0, 0, 0))
    return wide, attn, e_end, states


def _step_inputs(ci, rows, lanes, u_ref, w_ref, qd_ref, kd_ref, attn_ref, e_ref):
    heads = range(DN_HEADS)
    return ([u_ref[rows, lanes[h]] for h in heads], [w_ref[rows, lanes[h]].astype(f32) for h in heads],
            [qd_ref[rows, lanes[h]].astype(f32) for h in heads], [kd_ref[rows, lanes[h]].astype(f32) for h in heads],
            [attn_ref[ci, h] for h in heads], [e_ref[ci, 8 * h:8 * h + 1, :] for h in heads])


def _dn_scan_forward(u, w, q_dec, k_dec, attn, e_end):
    seq = u.shape[0]
    nc = seq // CHUNK
    wide, attn_spec, e_spec, st_spec = _scan_specs(nc, reverse=False)

    def body(u_ref, w_ref, qd_ref, kd_ref, attn_ref, e_ref, o_ref, st_ref, state_ref):
        @pl.when(pl.program_id(0) == 0)
        def _():
            state_ref[...] = jnp.zeros_like(state_ref)

        heads = range(DN_HEADS)
        lanes = [slice(h * DN_DIM, (h + 1) * DN_DIM) for h in heads]
        states = [state_ref[h] for h in heads]
        for ci in range(SCAN_CHUNKS):
            rows = slice(ci * CHUNK, (ci + 1) * CHUNK)
            for h in heads:
                st_ref[ci, h] = states[h].astype(bf16)
            o, states = _dn_step(*_step_inputs(ci, rows, lanes, u_ref, w_ref, qd_ref, kd_ref, attn_ref, e_ref), states)
            for h in heads:
                o_ref[rows, lanes[h]] = o[h]
        for h in heads:
            state_ref[h] = states[h]

    return pl.pallas_call(
        body, name="dn_scan_forward", grid=(nc // SCAN_CHUNKS,),
        in_specs=[wide, wide, wide, wide, attn_spec, e_spec],
        out_specs=[wide, st_spec],
        out_shape=[jax.ShapeDtypeStruct((seq, DN_WIDTH), f32), jax.ShapeDtypeStruct((nc, DN_HEADS, DN_DIM, DN_DIM), bf16)],
        scratch_shapes=[pltpu.VMEM((DN_HEADS, DN_DIM, DN_DIM), f32)],
        compiler_params=_params(("arbitrary",)),
    )(u, w, q_dec, k_dec, attn, e_end)


def _dn_scan_backward(u, w, q_dec, k_dec, attn, e_end, states, d_o):
    seq = u.shape[0]
    nc = seq // CHUNK
    wide, attn_spec, e_spec, st_spec = _scan_specs(nc, reverse=True)

    def body(u_ref, w_ref, qd_ref, kd_ref, attn_ref, e_ref, st_ref, do_ref,
             du_ref, dw_ref, dqd_ref, dkd_ref, dattn_ref, de_ref, dstate_ref):
        @pl.when(pl.program_id(0) == 0)
        def _():
            dstate_ref[...] = jnp.zeros_like(dstate_ref)

        heads = range(DN_HEADS)
        lanes = [slice(h * DN_DIM, (h + 1) * DN_DIM) for h in heads]
        first_row = lax.broadcasted_iota(jnp.int32, (8, DN_DIM), 0) == 0
        dstates = [dstate_ref[h] for h in heads]
        for ci in reversed(range(SCAN_CHUNKS)):
            rows = slice(ci * CHUNK, (ci + 1) * CHUNK)
            _, step_vjp = jax.vjp(_dn_step, *_step_inputs(ci, rows, lanes, u_ref, w_ref, qd_ref, kd_ref, attn_ref, e_ref),
                                  [st_ref[ci, h].astype(f32) for h in heads])
            du, dw, dqd, dkd, dattn, de, dstates = step_vjp(([do_ref[rows, lanes[h]] for h in heads], dstates))
            for h in heads:
                du_ref[rows, lanes[h]] = du[h]
                dw_ref[rows, lanes[h]] = dw[h]
                dqd_ref[rows, lanes[h]] = dqd[h]
                dkd_ref[rows, lanes[h]] = dkd[h]
                dattn_ref[ci, h] = dattn[h]
                de_ref[ci, 8 * h:8 * h + 8, :] = jnp.where(first_row, jnp.broadcast_to(de[h], (8, DN_DIM)), 0.0)
        for h in heads:
            dstate_ref[h] = dstates[h]

    return pl.pallas_call(
        body, name="dn_scan_backward", grid=(nc // SCAN_CHUNKS,),
        in_specs=[wide, wide, wide, wide, attn_spec, e_spec, st_spec, wide],
        out_specs=[wide, wide, wide, wide, attn_spec, e_spec],
        out_shape=_intra_shapes(seq, f32),
        scratch_shapes=[pltpu.VMEM((DN_HEADS, DN_DIM, DN_DIM), f32)],
        compiler_params=_params(("arbitrary",)),
    )(u, w, q_dec, k_dec, attn, e_end, states, d_o)


def _dn_intra_backward(pdn, conv_w8, a_row, dt_row, invs, d_u, d_w, d_qd, d_kd, d_attn, d_e):
    seq = pdn.shape[0]
    nc = seq // CHUNK
    rows_per_step = INTRA_CHUNKS * CHUNK
    cur, halo, ba, conv, row, wide, attn, e_end = _intra_specs(nc)

    def body(cur_ref, halo_ref, ba_ref, w_ref, a_ref, dt_ref, inv_ref, du_ref, dw_ref, dqd_ref, dkd_ref, dattn_ref, de_ref,
             dconv_ref, dba_ref, drow_ref, ext_ref):
        @pl.when(pl.program_id(0) == 0)
        def _():
            drow_ref[...] = jnp.zeros_like(drow_ref)

        halo_rows = jnp.where(pl.program_id(0) > 0, halo_ref[...], 0.0)
        c = _conv_taps(ext_ref, halo_rows, cur_ref[...], w_ref[...])
        (qn, kn, v, beta_all, g_all), pre_vjp = jax.vjp(
            _dn_pre, c[:, :DN_WIDTH], c[:, DN_WIDTH:2 * DN_WIDTH], c[:, 2 * DN_WIDTH:], ba_ref[...], a_ref[...], dt_ref[...])
        lane = lax.broadcasted_iota(jnp.int32, (CHUNK, BA_W), 1)
        items = _intra_items()
        _, intra_vjp = jax.vjp(_dn_intra, *_intra_inputs(items, qn, kn, v, beta_all, g_all),
                               [inv_ref[ci, h] for ci, h, _, _ in items])
        dq, dk, dv, dbeta, dg, _ = intra_vjp((
            [du_ref[rows, lanes] for _, _, rows, lanes in items], [dw_ref[rows, lanes] for _, _, rows, lanes in items],
            [dqd_ref[rows, lanes] for _, _, rows, lanes in items], [dkd_ref[rows, lanes] for _, _, rows, lanes in items],
            [dattn_ref[ci, h] for ci, h, _, _ in items], [de_ref[ci, 8 * h:8 * h + 8, :] for ci, h, _, _ in items],
            [jnp.zeros((CHUNK, CHUNK), f32) for _ in items]))
        dq_rows, dk_rows, dv_rows, dbeta_rows, dg_rows = [], [], [], [], []
        for ci in range(INTRA_CHUNKS):
            of_chunk = [n for n, item in enumerate(items) if item[0] == ci]
            d_beta_all = jnp.zeros((CHUNK, BA_W), f32)
            d_g_all = jnp.zeros((CHUNK, BA_W), f32)
            for n in of_chunk:
                h = items[n][1]
                d_beta_all = d_beta_all + jnp.where(lane == h, dbeta[n], 0.0)
                d_g_all = d_g_all + jnp.where(lane == DN_HEADS + h, dg[n], 0.0)
            dq_rows.append(jnp.concatenate([dq[n] for n in of_chunk], axis=1))
            dk_rows.append(jnp.concatenate([dk[n] for n in of_chunk], axis=1))
            dv_rows.append(jnp.concatenate([dv[n] for n in of_chunk], axis=1))
            dbeta_rows.append(d_beta_all)
            dg_rows.append(d_g_all)
        stack = lambda parts: jnp.concatenate(parts, axis=0)
        dcq, dck, dcv, dba, da_row, ddt_row = pre_vjp(
            (stack(dq_rows), stack(dk_rows), stack(dv_rows), stack(dbeta_rows), stack(dg_rows)))
        dconv_ref[:, :DN_WIDTH] = dcq
        dconv_ref[:, DN_WIDTH:2 * DN_WIDTH] = dck
        dconv_ref[:, 2 * DN_WIDTH:] = dcv
        dba_ref[...] = dba
        drow_ref[0:1, :] += da_row
        drow_ref[1:2, :] += ddt_row

    return pl.pallas_call(
        body, name="dn_intra_backward", grid=(nc // INTRA_CHUNKS,),
        in_specs=[cur, halo, ba, conv, row, row, attn, wide, wide, wide, wide, attn, e_end],
        out_specs=[pl.BlockSpec((rows_per_step, QKV_W), lambda i: (i, 0)),
                   pl.BlockSpec((rows_per_step, BA_W), lambda i: (i, 0)),
                   pl.BlockSpec((8, BA_W), lambda i: (0, 0))],
        out_shape=[jax.ShapeDtypeStruct((seq, QKV_W), f32), jax.ShapeDtypeStruct((seq, BA_W), f32),
                   jax.ShapeDtypeStruct((8, BA_W), f32)],
        scratch_shapes=[pltpu.VMEM((rows_per_step + 8, QKV_W), f32)],
        compiler_params=_params(("arbitrary",)),
    )(pdn, pdn, pdn, conv_w8, a_row, dt_row, invs, d_u, d_w, d_qd, d_kd, d_attn, d_e)


def _band_masks(block_index):
    qi = lax.broadcasted_iota(jnp.int32, (Q_BLOCK, Q_BLOCK), 0)
    kj = lax.broadcasted_iota(jnp.int32, (Q_BLOCK, Q_BLOCK), 1)
    return (kj >= qi) & (block_index > 0), kj <= qi


def _low_half():
    return lax.broadcasted_iota(jnp.int32, (Q_BLOCK, LANES), 1) < AT_DIM


def _head_pairs(ref, split):
    low = _low_half()
    slabs = [ref[:, pair * LANES:(pair + 1) * LANES].astype(bf16) for pair in range(AT_HEADS // 2)]
    if not split:
        return slabs
    zero = jnp.zeros((Q_BLOCK, LANES), bf16)
    return [jnp.where(low, slab, zero) if h == 0 else jnp.where(low, zero, slab) for slab in slabs for h in range(2)]


def _stack_blocks(first, second):
    return [jnp.concatenate([a, b], axis=0) for a, b in zip(first, second)]


def _attn_forward(pat_view, dil):
    length = pat_view.shape[0]
    nb = length // Q_BLOCK
    scale = AT_DIM ** -0.5

    def body(q_ref, kp_ref, kc_ref, vp_ref, vc_ref, o_ref, lse_ref):
        mask = jnp.concatenate(_band_masks(pl.program_id(1)), axis=1)
        low = _low_half()
        heads = range(AT_HEADS)
        q = _head_pairs(q_ref, split=True)
        k = _stack_blocks(_head_pairs(kp_ref, split=False), _head_pairs(kc_ref, split=False))
        v = _stack_blocks(_head_pairs(vp_ref, split=False), _head_pairs(vc_ref, split=False))
        s = [_dot(q[h], k[h // 2], NT, False) for h in heads]
        p, top = [], []
        for h in heads:
            masked = jnp.where(mask, s[h] * scale, NEG)
            m = jnp.max(masked, axis=1, keepdims=True)
            p.append(jnp.exp(masked - m).astype(bf16))
            top.append(m)
        ones = jnp.ones((2 * Q_BLOCK, LANES), bf16)
        l = [_dot(p[h], ones, NN, False) for h in heads]
        o = [_dot(p[h], v[h // 2], NN, False) for h in heads]
        for pair in range(AT_HEADS // 2):
            even, odd = 2 * pair, 2 * pair + 1
            slab = slice(pair * LANES, (pair + 1) * LANES)
            o_ref[:, slab] = jnp.where(low, o[even], o[odd]) / jnp.where(low, l[even], l[odd])
        lane = lax.broadcasted_iota(jnp.int32, (Q_BLOCK, HEAD_LANES), 1)
        lse = jnp.zeros((Q_BLOCK, HEAD_LANES), f32)
        for h in heads:
            lse = jnp.where(lane == h, top[h] + jnp.log(l[h]), lse)
        lse_ref[...] = lse

    def blk(piece, prev):
        if prev:
            return pl.BlockSpec((Q_BLOCK, AT_WIDTH), lambda r, n: (jnp.maximum(n - 1, 0), 3 * r + piece))
        return pl.BlockSpec((Q_BLOCK, AT_WIDTH), lambda r, n: (n, 3 * r + piece))

    out = pl.BlockSpec((Q_BLOCK, AT_WIDTH), lambda r, n: (n, r))
    return pl.pallas_call(
        body, name=f"attn_forward_d{dil}", grid=(dil, nb),
        in_specs=[blk(0, False), blk(1, True), blk(1, False), blk(2, True), blk(2, False)],
        out_specs=[out, pl.BlockSpec((Q_BLOCK, HEAD_LANES), lambda r, n: (n, r))],
        out_shape=[jax.ShapeDtypeStruct((length, dil * AT_WIDTH), f32),
                   jax.ShapeDtypeStruct((length, dil * HEAD_LANES), f32)],
        compiler_params=_params(("parallel", "parallel")),
    )(pat_view, pat_view, pat_view, pat_view, pat_view)


def _attn_backward_q(pat_view, d_out, delta, lse, dil):
    length = pat_view.shape[0]
    nb = length // Q_BLOCK
    scale = AT_DIM ** -0.5

    def body(q_ref, kp_ref, kc_ref, vp_ref, vc_ref, do_ref, dl_ref, lse_ref, dq_ref):
        mask = jnp.concatenate(_band_masks(pl.program_id(1)), axis=1)
        low = _low_half()
        heads = range(AT_HEADS)
        q, do = _head_pairs(q_ref, split=True), _head_pairs(do_ref, split=True)
        k = _stack_blocks(_head_pairs(kp_ref, split=False), _head_pairs(kc_ref, split=False))
        v = _stack_blocks(_head_pairs(vp_ref, split=False), _head_pairs(vc_ref, split=False))
        s = [_dot(q[h], k[h // 2], NT, False) for h in heads]
        dp = [_dot(do[h], v[h // 2], NT, False) for h in heads]
        ds = []
        for h in heads:
            p = jnp.exp(jnp.where(mask, s[h] * scale - lse_ref[:, h:h + 1], NEG))
            ds.append((p * (dp[h] - dl_ref[:, h:h + 1])).astype(bf16))
        dq = [_dot(ds[h], k[h // 2], NN, False) for h in heads]
        for pair in range(AT_HEADS // 2):
            dq_ref[:, pair * LANES:(pair + 1) * LANES] = (
                jnp.where(low, dq[2 * pair], dq[2 * pair + 1]) * scale).astype(bf16)

    def blk(piece, prev):
        if prev:
            return pl.BlockSpec((Q_BLOCK, AT_WIDTH), lambda r, n: (jnp.maximum(n - 1, 0), 3 * r + piece))
        return pl.BlockSpec((Q_BLOCK, AT_WIDTH), lambda r, n: (n, 3 * r + piece))

    one = pl.BlockSpec((Q_BLOCK, AT_WIDTH), lambda r, n: (n, r))
    compact = pl.BlockSpec((Q_BLOCK, HEAD_LANES), lambda r, n: (n, r))
    return pl.pallas_call(
        body, name=f"attn_backward_q_d{dil}", grid=(dil, nb),
        in_specs=[blk(0, False), blk(1, True), blk(1, False), blk(2, True), blk(2, False), one, compact, compact],
        out_specs=one,
        out_shape=jax.ShapeDtypeStruct((length, dil * AT_WIDTH), bf16),
        compiler_params=_params(("parallel", "parallel")),
    )(pat_view, pat_view, pat_view, pat_view, pat_view, d_out, delta, lse)


def _attn_backward_kv(pat_view, d_out, delta, lse, dil):
    length = pat_view.shape[0]
    nb = length // Q_BLOCK
    scale = AT_DIM ** -0.5

    def body(k_ref, v_ref, qa_ref, qb_ref, doa_ref, dob_ref, dla_ref, dlb_ref, lsea_ref, lseb_ref, dk_ref, dv_ref):
        j = pl.program_id(1)
        kj = lax.broadcasted_iota(jnp.int32, (Q_BLOCK, Q_BLOCK), 0)
        qi = lax.broadcasted_iota(jnp.int32, (Q_BLOCK, Q_BLOCK), 1)
        mask = jnp.concatenate([kj <= qi,
                                (kj >= qi) & (j + 1 < nb)],
                               axis=1)
        low = _low_half()
        heads = range(AT_HEADS)
        k, v = _head_pairs(k_ref, split=True), _head_pairs(v_ref, split=True)
        q = _stack_blocks(_head_pairs(qa_ref, split=False), _head_pairs(qb_ref, split=False))
        do = _stack_blocks(_head_pairs(doa_ref, split=False), _head_pairs(dob_ref, split=False))
        row = lax.broadcasted_iota(jnp.int32, (8 * AT_HEADS, HEAD_LANES), 0)
        col = lax.broadcasted_iota(jnp.int32, (8 * AT_HEADS, HEAD_LANES), 1)
        pick = (col == row // 8).astype(f32)
        lse_rows = _dot_sel(pick, jnp.concatenate([lsea_ref[...], lseb_ref[...]], axis=0), NT, 0)
        dl_rows = _dot_sel(pick, jnp.concatenate([dla_ref[...], dlb_ref[...]], axis=0), NT, 0)
        s_t = [_dot(k[h], q[h // 2], NT, False) for h in heads]
        dp_t = [_dot(v[h], do[h // 2], NT, False) for h in heads]
        p_t, ds_t = [], []
        for h in heads:
            prob = jnp.exp(jnp.where(mask, s_t[h] * scale - lse_rows[8 * h:8 * h + 1, :], NEG))
            p_t.append(prob.astype(bf16))
            ds_t.append((prob * (dp_t[h] - dl_rows[8 * h:8 * h + 1, :])).astype(bf16))
        dv = [_dot(p_t[h], do[h // 2], NN, False) for h in heads]
        dk = [_dot(ds_t[h], q[h // 2], NN, False) for h in heads]
        for pair in range(AT_HEADS // 2):
            slab = slice(pair * LANES, (pair + 1) * LANES)
            dk_ref[:, slab] = (jnp.where(low, dk[2 * pair], dk[2 * pair + 1]) * scale).astype(bf16)
            dv_ref[:, slab] = jnp.where(low, dv[2 * pair], dv[2 * pair + 1]).astype(bf16)

    def kv(piece):
        return pl.BlockSpec((Q_BLOCK, AT_WIDTH), lambda r, n: (n, 3 * r + piece))

    q_same = pl.BlockSpec((Q_BLOCK, AT_WIDTH), lambda r, n: (n, 3 * r))
    q_next = pl.BlockSpec((Q_BLOCK, AT_WIDTH), lambda r, n: (jnp.minimum(n + 1, nb - 1), 3 * r))
    same = pl.BlockSpec((Q_BLOCK, AT_WIDTH), lambda r, n: (n, r))
    nxt = pl.BlockSpec((Q_BLOCK, AT_WIDTH), lambda r, n: (jnp.minimum(n + 1, nb - 1), r))
    c_same = pl.BlockSpec((Q_BLOCK, HEAD_LANES), lambda r, n: (n, r))
    c_nxt = pl.BlockSpec((Q_BLOCK, HEAD_LANES), lambda r, n: (jnp.minimum(n + 1, nb - 1), r))
    return pl.pallas_call(
        body, name=f"attn_backward_kv_d{dil}", grid=(dil, nb),
        in_specs=[kv(1), kv(2), q_same, q_next, same, nxt, c_same, c_nxt, c_same, c_nxt],
        out_specs=[same, same],
        out_shape=[jax.ShapeDtypeStruct((length, dil * AT_WIDTH), bf16)] * 2,
        compiler_params=_params(("parallel", "parallel")),
    )(pat_view, pat_view, pat_view, pat_view, d_out, d_out, delta, delta, lse, lse)


def _gated_norms(o_dn, z_dn, o_at, z_at, dn_w, at_w):
    ms_dn = _slab_sums(o_dn * o_dn) * (1.0 / DN_DIM)
    a = o_dn * lax.rsqrt(ms_dn + EPS) * dn_w * _silu(z_dn)
    head_of_lane = (lax.broadcasted_iota(jnp.int32, (AT_WIDTH, HEAD_LANES), 0) // AT_DIM
                    == lax.broadcasted_iota(jnp.int32, (AT_WIDTH, HEAD_LANES), 1)).astype(f32)
    lanes_of_head = (lax.broadcasted_iota(jnp.int32, (HEAD_LANES, AT_WIDTH), 0)
                     == lax.broadcasted_iota(jnp.int32, (HEAD_LANES, AT_WIDTH), 1) // AT_DIM).astype(f32)
    ms_at = _sel_right(_sel_right(o_at * o_at, head_of_lane), lanes_of_head) * (1.0 / AT_DIM)
    b = o_at * lax.rsqrt(ms_at + EPS) * at_w * _silu(z_at)
    return a, b


def _residual_loss(x, mix, gate, fin_w, target):
    x2 = x + gate * mix
    y = x2 * lax.rsqrt(jnp.mean(x2 * x2, axis=-1, keepdims=True) + EPS) * fin_w
    err = y - target
    per_token = jnp.sum(err * err, axis=1, keepdims=True) * (1.0 / D_MODEL)
    return 0.5 * jnp.sum(per_token, axis=0, keepdims=True)


TAIL_ROWS = 8


def _tail(x, target, o_dn, pdn, outs, lses, z_at, gate, w_out, dn_w_row, at_w_row, fin_w):
    seq = x.shape[0]
    tm = TOKEN_TILE

    def body(x_ref, t_ref, odn_ref, zdn_ref, o1_ref, o4_ref, o16_ref, l1_ref, l4_ref, l16_ref, zat_ref, gate_ref,
             wt_ref, wb_ref, dnw_ref, atw_ref, fw_ref,
             dx2_ref, dodn_ref, dzdn_ref, doat1_ref, doat4_ref, doat16_ref, delta1_ref, delta4_ref, delta16_ref,
             lse1_ref, lse4_ref, lse16_ref, dzat_ref, gwt_ref, gwb_ref, rows_ref, *stages):
        @pl.when(pl.program_id(0) == 0)
        def _():
            gwt_ref[...] = jnp.zeros_like(gwt_ref)
            gwb_ref[...] = jnp.zeros_like(gwb_ref)
            rows_ref[...] = jnp.zeros_like(rows_ref)

        l1, l4, l16 = l1_ref[...], _view_to_value(stages[0], l4_ref, 4), _view_to_value(stages[1], l16_ref, 16)
        o4, o16 = _view_to_value(stages[2], o4_ref, 4), _view_to_value(stages[3], o16_ref, 16)
        top = jnp.maximum(jnp.maximum(l1, l4), l16)
        e1, e4, e16 = jnp.exp(l1 - top), jnp.exp(l4 - top), jnp.exp(l16 - top)
        den = e1 + e4 + e16
        lse = top + jnp.log(den)
        o_at = (_repeat_heads(e1 / den) * o1_ref[...] + _repeat_heads(e4 / den) * o4 + _repeat_heads(e16 / den) * o16)

        (a, b), norms_vjp = jax.vjp(_gated_norms, odn_ref[...], zdn_ref[...], o_at, zat_ref[...], dnw_ref[...], atw_ref[...])
        a, b = a.astype(bf16), b.astype(bf16)
        mix = jnp.dot(a, wt_ref[...], preferred_element_type=f32) + jnp.dot(b, wb_ref[...], preferred_element_type=f32)
        loss, loss_vjp = jax.vjp(_residual_loss, x_ref[...], mix, gate_ref[...], fw_ref[...], t_ref[...])
        dx2, dmix, dgate, dfw, _ = loss_vjp(jnp.ones((1, 1), f32))
        dmix = dmix.astype(bf16)
        dwt = lax.dot_general(a, dmix, (TN, ((), ())), preferred_element_type=f32)
        dwb = lax.dot_general(b, dmix, (TN, ((), ())), preferred_element_type=f32)
        da = lax.dot_general(dmix, wt_ref[...], (NT, ((), ())), preferred_element_type=f32)
        db = lax.dot_general(dmix, wb_ref[...], (NT, ((), ())), preferred_element_type=f32)
        dodn, dzdn, doat, dzat, ddnw, datw = norms_vjp((da, db))
        dx2_ref[...] = dx2
        dodn_ref[...] = dodn
        dzdn_ref[...] = dzdn
        dzat_ref[...] = dzat
        lane_head = (lax.broadcasted_iota(jnp.int32, (AT_WIDTH, HEAD_LANES), 0) // AT_DIM
                     == lax.broadcasted_iota(jnp.int32, (AT_WIDTH, HEAD_LANES), 1))
        delta = _dot_sel(lane_head.astype(f32), doat * o_at, NN, 1)
        for stage_ref, value, refs in ((stages[2], doat, (doat1_ref, doat4_ref, doat16_ref)),
                                       (stages[0], delta, (delta1_ref, delta4_ref, delta16_ref)),
                                       (stages[1], lse, (lse1_ref, lse4_ref, lse16_ref))):
            refs[0][...] = value.astype(refs[0].dtype)
            _stage_lanes(stage_ref, 0, value)
            _stage_to_view(stage_ref, refs[1], 4)
            _stage_to_view(stage_ref, refs[2], 16)
        gwt_ref[...] += dwt
        gwb_ref[...] += dwb
        rows_ref[0:1, :] += jnp.broadcast_to(loss, (1, D_MODEL))
        rows_ref[1:2, :] += dgate
        rows_ref[2:3, :] += dfw
        rows_ref[3:4, 0:DN_WIDTH] += ddnw
        rows_ref[4:5, 0:AT_WIDTH] += datw

    tile = lambda w: pl.BlockSpec((tm, w), lambda i: (i, 0))
    row = lambda w: pl.BlockSpec((1, w), lambda i: (0, 0))
    half_w = pl.BlockSpec((DN_WIDTH, D_MODEL), lambda i: (0, 0))
    sds = lambda w: jax.ShapeDtypeStruct((seq, w), f32)
    views = [_view_spec(AT_WIDTH, d) for d in PATTERN_DILATIONS]
    compact = [_view_spec(HEAD_LANES, d) for d in PATTERN_DILATIONS]
    doat_shapes = [_view_shape(seq, AT_WIDTH, d, bf16) for d in PATTERN_DILATIONS]
    compact_shapes = [_view_shape(seq, HEAD_LANES, d) for d in PATTERN_DILATIONS]
    return pl.pallas_call(
        body, name="tail", grid=(seq // tm,),
        in_specs=[tile(D_MODEL), tile(D_MODEL), tile(DN_WIDTH),
                  pl.BlockSpec((tm, DN_WIDTH), lambda i: (i, QKV_W // DN_WIDTH)),
                  *views, *compact, tile(AT_WIDTH), row(D_MODEL), half_w, pl.BlockSpec((AT_WIDTH, D_MODEL), lambda i: (1, 0)),
                  row(DN_WIDTH), row(AT_WIDTH), row(D_MODEL)],
        out_specs=[tile(D_MODEL), tile(DN_WIDTH), tile(DN_WIDTH), *views, *compact, *compact,
                   tile(AT_WIDTH), half_w, half_w, pl.BlockSpec((TAIL_ROWS, D_MODEL), lambda i: (0, 0))],
        out_shape=[sds(D_MODEL), sds(DN_WIDTH), sds(DN_WIDTH), *doat_shapes, *compact_shapes, *compact_shapes,
                   sds(AT_WIDTH), jax.ShapeDtypeStruct((DN_WIDTH, D_MODEL), f32),
                   jax.ShapeDtypeStruct((AT_WIDTH, D_MODEL), f32), jax.ShapeDtypeStruct((TAIL_ROWS, D_MODEL), f32)],
        scratch_shapes=[_stage(HEAD_LANES)] * 2 + [_stage(AT_WIDTH)] * 2,
        compiler_params=_params(("arbitrary",)),
    )(x, target, o_dn, pdn, outs[0], outs[1], outs[2], lses[0], lses[1], lses[2], z_at, gate, w_out, w_out,
      dn_w_row, at_w_row, fin_w)


PRE_ROWS = 8


def _pre_backward(x, dx2, pos_col, invf_row, sign_row, norm_w, scale, shift, w_cat, conv_w8, pdn, d_conv, d_zdn, d_ba,
                  d_q, d_k, d_v, d_zat):
    seq = x.shape[0]
    tm = TOKEN_TILE
    last = seq // tm - 1

    def body(x_ref, dx2_ref, pos_ref, invf_ref, sign_ref, nw_ref, sc_ref, sh_ref, w_ref, cw_ref,
             pre_ref, prehalo_ref, dc_ref, dchalo_ref, dz_ref, dba_ref,
             dq1_ref, dq4_ref, dq16_ref, dk1_ref, dk4_ref, dk16_ref, dv1_ref, dv4_ref, dv16_ref, dzat_ref,
             gx_ref, dproj_ref, rows_ref, crow_ref, ext_ref, *stages):
        i = pl.program_id(0)

        @pl.when(i == 0)
        def _():
            rows_ref[...] = jnp.zeros_like(rows_ref)
            crow_ref[...] = jnp.zeros_like(crow_ref)

        dc = dc_ref[...]
        ext_ref[0:tm, :] = dc
        ext_ref[tm:tm + 8, :] = jnp.where(i < last, dchalo_ref[...], 0.0)
        w8 = cw_ref[...]
        d_pre = ext_ref[pl.ds(3, tm), :] * w8[0:1, :]
        for j in range(1, CONV_K):
            d_pre = d_pre + ext_ref[pl.ds(3 - j, tm), :] * w8[j:j + 1, :]
        ext_ref[0:8, :] = jnp.where(i > 0, prehalo_ref[...], 0.0)
        ext_ref[8:8 + tm, :] = pre_ref[...]
        for j in range(CONV_K):
            crow_ref[j:j + 1, :] += jnp.sum(dc * ext_ref[pl.ds(5 + j, tm), :], axis=0, keepdims=True)

        cos_t, sin_t = _rope_tables(pos_ref[...], invf_ref[...], sign_ref[...])
        dq = dq1_ref[...] + _view_to_value(stages[0], dq4_ref, 4) + _view_to_value(stages[1], dq16_ref, 16)
        dk = dk1_ref[...] + _view_to_value(stages[2], dk4_ref, 4) + _view_to_value(stages[3], dk16_ref, 16)
        dq = dq * cos_t + _rope_partner(dq * sin_t)
        dk = dk * cos_t + _rope_partner(dk * sin_t)
        dv = dv1_ref[...] + _view_to_value(stages[4], dv4_ref, 4) + _view_to_value(stages[5], dv16_ref, 16)
        d_proj = jnp.concatenate([d_pre, dz_ref[...], dba_ref[...], dq, dk, dv, dzat_ref[...]], axis=1).astype(bf16)
        dproj_ref[...] = d_proj
        dh = lax.dot_general(d_proj, w_ref[...], (NT, ((), ())), preferred_element_type=f32)
        _, vjp = jax.vjp(_hnorm, x_ref[...], nw_ref[...], sc_ref[...], sh_ref[...])
        dx, dnw, dsc, dsh = vjp(dh)
        gx_ref[...] = dx + dx2_ref[...]
        rows_ref[0:1, :] += dnw
        rows_ref[1:2, :] += dsc
        rows_ref[2:3, :] += dsh

    tile = lambda w: pl.BlockSpec((tm, w), lambda i: (i, 0))
    row = lambda w: pl.BlockSpec((1, w), lambda i: (0, 0))
    step8 = tm // 8
    return pl.pallas_call(
        body, name="pre_backward", grid=(seq // tm,),
        in_specs=[tile(D_MODEL), tile(D_MODEL), tile(1), row(128), row(128), row(D_MODEL), row(D_MODEL), row(D_MODEL),
                  pl.BlockSpec((D_MODEL, CAT_W), lambda i: (0, 0)), pl.BlockSpec((8, QKV_W), lambda i: (0, 0)),
                  tile(QKV_W), pl.BlockSpec((8, QKV_W), lambda i: (jnp.maximum(i * step8 - 1, 0), 0)),
                  tile(QKV_W), pl.BlockSpec((8, QKV_W), lambda i: (jnp.minimum((i + 1) * step8, seq // 8 - 1), 0)),
                  tile(DN_WIDTH), tile(BA_W)] + [_view_spec(AT_WIDTH, d) for d in PATTERN_DILATIONS] * 3 + [tile(AT_WIDTH)],
        out_specs=[tile(D_MODEL), tile(CAT_W), pl.BlockSpec((PRE_ROWS, D_MODEL), lambda i: (0, 0)),
                   pl.BlockSpec((8, QKV_W), lambda i: (0, 0))],
        out_shape=[jax.ShapeDtypeStruct((seq, D_MODEL), f32), jax.ShapeDtypeStruct((seq, CAT_W), bf16),
                   jax.ShapeDtypeStruct((PRE_ROWS, D_MODEL), f32), jax.ShapeDtypeStruct((8, QKV_W), f32)],
        scratch_shapes=[pltpu.VMEM((tm + 8, QKV_W), f32)] + [_stage(AT_WIDTH)] * 6,
        compiler_params=_params(("arbitrary",)),
    )(x, dx2, pos_col, invf_row, sign_row, norm_w, scale, shift, w_cat, conv_w8, pdn, pdn, d_conv, d_conv, d_zdn, d_ba,
      d_q[0], d_q[1], d_q[2], d_k[0], d_k[1], d_k[2], d_v[0], d_v[1], d_v[2], d_zat)


def _weight_grad(h, d_proj):
    seq = h.shape[0]
    tk, tn = 512, CAT_W // 3
    n_k = seq // tk

    def body(h_ref, d_ref, o_ref):
        @pl.when(pl.program_id(1) == 0)
        def _():
            o_ref[...] = jnp.zeros_like(o_ref)

        o_ref[...] += lax.dot_general(h_ref[...], d_ref[...], (TN, ((), ())), preferred_element_type=f32)

    return pl.pallas_call(
        body, name="weight_grad", grid=(CAT_W // tn, n_k),
        in_specs=[pl.BlockSpec((tk, D_MODEL), lambda n, k: (k, 0)), pl.BlockSpec((tk, tn), lambda n, k: (k, n))],
        out_specs=pl.BlockSpec((D_MODEL, tn), lambda n, k: (0, n)),
        out_shape=jax.ShapeDtypeStruct((D_MODEL, CAT_W), f32),
        compiler_params=_params(("parallel", "arbitrary")),
    )(h, d_proj)


def _adamw(w, g, m, v):
    m = ADAM_B1 * m + (1.0 - ADAM_B1) * g
    v = ADAM_B2 * v + (1.0 - ADAM_B2) * (g * g)
    m_hat = m / (1.0 - ADAM_B1 ** ADAM_STEP)
    v_hat = v / (1.0 - ADAM_B2 ** ADAM_STEP)
    delta = -ADAM_LR * (m_hat / (jnp.sqrt(v_hat) + ADAM_EPS) + ADAM_WD * w)
    return delta, m, v


def _mod_shard(c_all, w_mod_s, b_mod_s):
    def body(c_ref, w_ref, b_ref, o_ref):
        o_ref[...] = _dot(_silu(c_ref[...]), w_ref[...], NN, True) + b_ref[...]

    return pl.pallas_call(body, name="mod_shard", out_shape=jax.ShapeDtypeStruct((N_DEV, w_mod_s.shape[1]), f32),
                          compiler_params=_params())(c_all, w_mod_s, b_mod_s)


def _mod_update(c_all, d_mod_s, w, m, v):
    def body(c_ref, d_ref, w_ref, m_ref, v_ref, g_ref, dw_ref, nm_ref, nv_ref):
        g = _dot(_silu(c_ref[...]), d_ref[...], TN, True)
        g_ref[...] = g
        dw_ref[...], nm_ref[...], nv_ref[...] = _adamw(w_ref[...], g, m_ref[...], v_ref[...])

    return pl.pallas_call(body, name="mod_update", out_shape=[jax.ShapeDtypeStruct(w.shape, f32)] * 4,
                          compiler_params=_params())(c_all, d_mod_s, w, m, v)


def _adamw_rows(w, g, m, v, name):
    rows, width = w.shape
    tr = 128 if rows % 128 == 0 else rows

    def body(w_ref, g_ref, m_ref, v_ref, dw_ref, nm_ref, nv_ref):
        dw_ref[...], nm_ref[...], nv_ref[...] = _adamw(w_ref[...], g_ref[...], m_ref[...], v_ref[...])

    spec = pl.BlockSpec((tr, width), lambda i: (i, 0))
    return pl.pallas_call(body, name=name, grid=(rows // tr,), in_specs=[spec] * 4, out_specs=[spec] * 3,
                          out_shape=[jax.ShapeDtypeStruct(w.shape, f32)] * 3,
                          compiler_params=_params(("parallel",)))(w, g, m, v)


SM_NORM, SM_FIN, SM_DN, SM_AT, SM_A, SM_DT, SM_LOSS, SM_MOD = 0, 1024, 2048, 2560, 3072, 3200, 3328, 3456
SM_W = SM_MOD + 3 * D_MODEL
RS_NORM, RS_FIN, RS_DN, RS_AT, RS_A, RS_DT, RS_BMOD = 0, 1024, 2048, 2176, 2304, 2432, 2560
RS_W = RS_BMOD + 3 * D_MODEL


def _small_update(gathered, w, m, v):
    def body(g_ref, w_ref, m_ref, v_ref, grad_ref, dw_ref, nm_ref, nv_ref, loss_ref, dmod_ref):
        total = g_ref[0:1, :]
        for dev in range(1, N_DEV):
            total = total + g_ref[8 * dev:8 * dev + 1, :]
        for dev in range(N_DEV):
            dmod_ref[dev:dev + 1, :] = g_ref[8 * dev:8 * dev + 1, SM_MOD:SM_W]
        dn = total[:, SM_DN:SM_DN + DN_DIM]
        for h in range(1, DN_HEADS):
            dn = dn + total[:, SM_DN + h * DN_DIM:SM_DN + (h + 1) * DN_DIM]
        at = total[:, SM_AT:SM_AT + AT_DIM]
        for h in range(1, AT_HEADS):
            at = at + total[:, SM_AT + h * AT_DIM:SM_AT + (h + 1) * AT_DIM]
        grad_ref[:, RS_NORM:RS_FIN] = total[:, SM_NORM:SM_FIN]
        grad_ref[:, RS_FIN:RS_DN] = total[:, SM_FIN:SM_DN]
        grad_ref[:, RS_DN:RS_AT] = dn
        grad_ref[:, RS_AT:RS_A] = jnp.zeros((1, 128), f32)
        grad_ref[:, RS_AT:RS_AT + AT_DIM] = at
        grad_ref[:, RS_A:RS_DT] = total[:, SM_A:SM_DT]
        grad_ref[:, RS_DT:RS_BMOD] = total[:, SM_DT:SM_LOSS]
        grad_ref[:, RS_BMOD:RS_W] = total[:, SM_MOD:SM_W]
        loss_ref[...] = total[:, SM_LOSS:SM_MOD]
        dw_ref[...], nm_ref[...], nv_ref[...] = _adamw(w_ref[...], grad_ref[...], m_ref[...], v_ref[...])

    row = jax.ShapeDtypeStruct((1, RS_W), f32)
    return pl.pallas_call(
        body, name="small_update",
        out_shape=[row, row, row, row, jax.ShapeDtypeStruct((1, 128), f32), jax.ShapeDtypeStruct((N_DEV, 3 * D_MODEL), f32)],
        compiler_params=_params())(gathered, w, m, v)


def _all_gather_rows(block, name):
    m_per, n = block.shape

    def body(x_ref, out_ref, send_sems, recv_sems, local_sem):
        x, y, c = lax.axis_index("x"), lax.axis_index("y"), lax.axis_index("c")
        me, sibling = (x, y, c), (x, y, 1 - c)
        chips = [(1 - x, y), (x, 1 - y), (1 - x, 1 - y)]

        def rows(px, py, pc):
            return out_ref.at[pl.ds((4 * px + 2 * py + pc) * m_per, m_per), :]

        def copy(k, blk, to, src=None):
            return pltpu.make_async_remote_copy(
                src_ref=rows(*blk) if src is None else src, dst_ref=rows(*blk),
                send_sem=send_sems.at[k], recv_sem=recv_sems.at[k], device_id=to, device_id_type=MESH)

        mine = pltpu.make_async_copy(x_ref, rows(*me), local_sem)
        mine.start()
        first = [copy(0, me, sibling, src=x_ref)]
        first += [copy(1 + j, me, (*chip, c), src=x_ref) for j, chip in enumerate(chips)]
        for cp in first:
            cp.start()
        passed = [copy(4 + j, (*chip, c), sibling) for j, chip in enumerate(chips)]
        for j, chip in enumerate(chips):
            copy(1 + j, (*chip, c), me).wait_recv()
            passed[j].start()
        copy(0, sibling, me).wait_recv()
        for j, chip in enumerate(chips):
            copy(4 + j, (*chip, 1 - c), me).wait_recv()
        for cp in first + passed:
            cp.wait_send()
        mine.wait()

    return pl.pallas_call(
        body, name=name,
        out_shape=jax.ShapeDtypeStruct((N_DEV * m_per, n), block.dtype),
        in_specs=[pl.BlockSpec(memory_space=pltpu.VMEM)],
        out_specs=pl.BlockSpec(memory_space=pltpu.VMEM),
        scratch_shapes=[pltpu.SemaphoreType.DMA((7,)), pltpu.SemaphoreType.DMA((7,)), pltpu.SemaphoreType.DMA],
        compiler_params=pltpu.CompilerParams(vmem_limit_bytes=VMEM_LIMIT),
    )(block)


def _gather_weight_shards(packed):
    rows, width = packed.shape
    half = rows // 2

    def body(src_ref, out_ref, send_sems, recv_sems, local_sem):
        x, y, c = lax.axis_index("x"), lax.axis_index("y"), lax.axis_index("c")
        chips = [(1 - x, y), (x, 1 - y), (1 - x, 1 - y)]
        my_rows = pl.ds(pl.multiple_of(c * half, 16), half)
        sibling_rows = pl.ds(pl.multiple_of((1 - c) * half, 16), half)

        def part(chip, part_rows):
            return out_ref.at[2 * chip[0] + chip[1], part_rows]

        def copy(k, src, dst, to):
            return pltpu.make_async_remote_copy(src_ref=src, dst_ref=dst, send_sem=send_sems.at[k], recv_sem=recv_sems.at[k],
                                                device_id=to, device_id_type=MESH)

        mine = pltpu.make_async_copy(src_ref, out_ref.at[2 * x + y], local_sem)
        mine.start()
        sends = [copy(k, src_ref.at[my_rows], part((x, y), my_rows), (*chip, c)) for k, chip in enumerate(chips)]
        for cp in sends:
            cp.start()
        passed = []
        for k, chip in enumerate(chips):
            landed = part(chip, my_rows)
            copy(k, landed, landed, (*chip, c)).wait_recv()
            passed.append(copy(3 + k, landed, landed, (x, y, 1 - c)))
            passed[k].start()
        for k, chip in enumerate(chips):
            from_sibling = part(chip, sibling_rows)
            copy(3 + k, from_sibling, from_sibling, (x, y, 1 - c)).wait_recv()
        for cp in sends + passed:
            cp.wait_send()
        mine.wait()

    return pl.pallas_call(
        body, name="gather_weight_shards",
        out_shape=jax.ShapeDtypeStruct((N_CHIPS, rows, width), packed.dtype),
        in_specs=[pl.BlockSpec(memory_space=pl.ANY)],
        out_specs=pl.BlockSpec(memory_space=pl.ANY),
        scratch_shapes=[pltpu.SemaphoreType.DMA((6,)), pltpu.SemaphoreType.DMA((6,)), pltpu.SemaphoreType.DMA],
    )(packed)


def _reduce_weight_grads(grads):
    _, rows, width = grads.shape
    half = rows // 2
    n_chunks = half // PACK_CHUNK

    def body(g_ref, out_ref, mine_ref, sibling_ref, send_ref, land_ref, send_sems, recv_sems, local_sem):
        x, y, c = lax.axis_index("x"), lax.axis_index("y"), lax.axis_index("c")
        chips = [(1 - x, y), (x, 1 - y), (1 - x, 1 - y)]
        sibling = (x, y, 1 - c)
        my_rows = pl.ds(pl.multiple_of(c * half, 16), half)
        sibling_rows = pl.ds(pl.multiple_of((1 - c) * half, 16), half)
        all_slots = pl.ds(0, N_CHIPS)

        def copy(k, src, dst, to):
            return pltpu.make_async_remote_copy(src_ref=src, dst_ref=dst, send_sem=send_sems.at[k], recv_sem=recv_sems.at[k],
                                                device_id=to, device_id_type=MESH)

        mine = pltpu.make_async_copy(g_ref.at[all_slots, my_rows], mine_ref, local_sem)
        mine.start()
        swap = copy(0, g_ref.at[all_slots, sibling_rows], sibling_ref, sibling)
        swap.start()
        mine.wait()
        swap.wait_recv()

        def pair_sums(i, carry):
            r = pl.ds(pl.multiple_of(i * PACK_CHUNK, 16), PACK_CHUNK)
            for k, chip in enumerate(chips):
                slot = 2 * chip[0] + chip[1]
                send_ref[k, r, :] = (mine_ref[slot, r, :] + sibling_ref[slot, r, :]).astype(bf16)
            out_rows = pl.ds(pl.multiple_of(c * half + i * PACK_CHUNK, 16), PACK_CHUNK)
            out_ref[out_rows, :] = mine_ref[2 * x + y, r, :] + sibling_ref[2 * x + y, r, :]
            return carry

        lax.fori_loop(0, n_chunks, pair_sums, 0)
        sends = [copy(1 + k, send_ref.at[k], land_ref.at[k], (*chip, c)) for k, chip in enumerate(chips)]
        for cp in sends:
            cp.start()
        for cp in sends:
            cp.wait_recv()

        def add_landed(i, carry):
            r = pl.ds(pl.multiple_of(i * PACK_CHUNK, 16), PACK_CHUNK)
            out_rows = pl.ds(pl.multiple_of(c * half + i * PACK_CHUNK, 16), PACK_CHUNK)
            landed = [land_ref[k, r, :].astype(f32) for k in range(3)]
            out_ref[out_rows, :] = ((out_ref[out_rows, :] + landed[0]) + landed[1]) + landed[2]
            return carry

        lax.fori_loop(0, n_chunks, add_landed, 0)
        finished = copy(4, out_ref.at[my_rows], out_ref.at[my_rows], sibling)
        finished.start()
        copy(4, out_ref.at[sibling_rows], out_ref.at[sibling_rows], sibling).wait_recv()
        for cp in [swap, finished] + sends:
            cp.wait_send()

    return pl.pallas_call(
        body, name="reduce_weight_grads",
        out_shape=jax.ShapeDtypeStruct((rows, width), f32),
        in_specs=[pl.BlockSpec(memory_space=pl.ANY)],
        out_specs=pl.BlockSpec(memory_space=pltpu.VMEM),
        scratch_shapes=[pltpu.VMEM((N_CHIPS, half, width), f32), pltpu.VMEM((N_CHIPS, half, width), f32),
                        pltpu.VMEM((3, half, width), bf16), pltpu.VMEM((3, half, width), bf16),
                        pltpu.SemaphoreType.DMA((5,)), pltpu.SemaphoreType.DMA((5,)), pltpu.SemaphoreType.DMA],
        compiler_params=pltpu.CompilerParams(vmem_limit_bytes=VMEM_LIMIT),
    )(grads)


CONV_SHARD = QKV_W // N_CHIPS


def _pack_shards(w_in_s, w_out_s, conv_rows):
    lead = w_in_s.shape[:-2]

    def widen(a, rows):
        return jnp.pad(a, [(0, 0)] * len(lead) + [(0, rows - a.shape[-2]), (0, PACK_W - a.shape[-1])])

    used = D_MODEL + OUT_ROWS + 8
    return jnp.concatenate([widen(w_in_s, D_MODEL), widen(w_out_s, OUT_ROWS), widen(conv_rows, 8),
                            jnp.zeros((*lead, PACK_ROWS - used, PACK_W), w_in_s.dtype)], axis=-2)


def _unpack_shards(packed):
    return (packed[..., :D_MODEL, :SHARD_IN], packed[..., D_MODEL:D_MODEL + OUT_ROWS, :D_MODEL],
            packed[..., D_MODEL + OUT_ROWS:D_MODEL + OUT_ROWS + CONV_K, :])


def _pack_weights_bf16(w_in_s, w_out_s, conv_s):
    conv_bits = lax.bitcast_convert_type(conv_s, bf16).reshape(CONV_K, 2 * CONV_SHARD)
    return _pack_shards(w_in_s.astype(bf16), w_out_s.astype(bf16), conv_bits)


def _unpack_weights_bf16(shards):
    w_in_s, w_out_s, conv_bits = _unpack_shards(shards)
    conv_s = lax.bitcast_convert_type(conv_bits[..., :2 * CONV_SHARD].reshape(N_CHIPS, CONV_K, CONV_SHARD, 2), f32)
    return (jnp.transpose(w_in_s, (1, 0, 2)).reshape(D_MODEL, IN_COLS), w_out_s.reshape(D_MODEL, D_MODEL),
            jnp.transpose(conv_s, (1, 0, 2)).reshape(CONV_K, QKV_W))


def _local_step(x, target, pos_col, mod_row, norm_w, w_in, conv_w, a_log, dt_bias, dn_norm_w, at_norm_w, w_out, fin_w):
    shift, scale, gate = mod_row[:, :D_MODEL], mod_row[:, D_MODEL:2 * D_MODEL], mod_row[:, 2 * D_MODEL:]
    half = AT_DIM // 2
    lane = jnp.arange(128)
    inv_freq = ROPE_THETA ** (-jnp.arange(half, dtype=f32) / half)
    invf_row = inv_freq[lane % half].reshape(1, 128)
    sign_row = jnp.where((lane % AT_DIM) < half, -1.0, 1.0).astype(f32).reshape(1, 128)
    ba_w = jnp.pad(w_in[:, 2 * D_MODEL:2 * D_MODEL + 2 * DN_HEADS], ((0, 0), (0, BA_W - 2 * DN_HEADS)))
    w_cat = jnp.concatenate([w_in[:, :2 * D_MODEL], ba_w, w_in[:, 2 * D_MODEL + 2 * DN_HEADS:]], axis=1).astype(bf16)
    conv_w8 = jnp.pad(conv_w, ((0, 8 - CONV_K), (0, 0)))
    a_row = jnp.pad(a_log.reshape(1, DN_HEADS), ((0, 0), (DN_HEADS, BA_W - 2 * DN_HEADS)))
    dt_row = jnp.pad(dt_bias.reshape(1, DN_HEADS), ((0, 0), (DN_HEADS, BA_W - 2 * DN_HEADS)))
    dn_w_row = jnp.tile(dn_norm_w.reshape(1, DN_DIM), (1, DN_HEADS))
    at_w_row = jnp.tile(at_norm_w.reshape(1, AT_DIM), (1, AT_HEADS))
    norm_row = norm_w.reshape(1, D_MODEL)
    fin_row = fin_w.reshape(1, D_MODEL)
    w_out_bf = w_out.astype(bf16)

    pdn, *views, z_at, h = _pre_proj(x, pos_col, invf_row, sign_row, norm_row, scale, shift, w_cat)
    *dn_parts, dn_invs = _dn_intra_forward(pdn, conv_w8, a_row, dt_row)
    o_dn, states = _dn_scan_forward(*dn_parts)
    outs, lses = [], []
    for d, view in zip(PATTERN_DILATIONS, views):
        o, lse = _attn_forward(view, d)
        outs.append(o)
        lses.append(lse)
    (dx2, d_odn, d_zdn, *in_views, d_zat, g_wtop, g_wbot, tail_rows) = _tail(
        x, target, o_dn, pdn, outs, lses, z_at, gate, w_out_bf, dn_w_row, at_w_row, fin_row)
    d_oat, delta, lse_all = in_views[0:3], in_views[3:6], in_views[6:9]
    d_q, d_k, d_v = [], [], []
    for n, (d, view) in enumerate(zip(PATTERN_DILATIONS, views)):
        args = (d_oat[n], delta[n], lse_all[n])
        d_q.append(_attn_backward_q(view, *args, d))
        dk, dv = _attn_backward_kv(view, *args, d)
        d_k.append(dk)
        d_v.append(dv)
    d_parts = _dn_scan_backward(*dn_parts, states, d_odn)
    d_conv, d_ba, dn_rows = _dn_intra_backward(pdn, conv_w8, a_row, dt_row, dn_invs, *d_parts)
    grad_x, d_proj, pre_rows, conv_rows = _pre_backward(
        x, dx2, pos_col, invf_row, sign_row, norm_row, scale, shift, w_cat, conv_w8, pdn, d_conv, d_zdn, d_ba,
        d_q, d_k, d_v, d_zat)
    g_cat = _weight_grad(h, d_proj)
    g_w_in = jnp.concatenate([g_cat[:, :2 * D_MODEL], g_cat[:, 2 * D_MODEL:2 * D_MODEL + 2 * DN_HEADS],
                              g_cat[:, 2 * D_MODEL + BA_W:]], axis=1)
    g_w_out = jnp.concatenate([g_wtop, g_wbot], axis=0)
    g_conv = conv_rows[:CONV_K]
    small = jnp.concatenate([
        pre_rows[0:1], tail_rows[2:3], tail_rows[3:4, :DN_WIDTH], tail_rows[4:5, :AT_WIDTH],
        dn_rows[0:1], dn_rows[1:2], tail_rows[0:1, :128],
        pre_rows[2:3], pre_rows[1:2], tail_rows[1:2]], axis=1)
    return grad_x, g_w_in, g_w_out, g_conv, small


def kernel(x, c, positions, w_mod, b_mod, norm_w, w_in, conv_w, a_log, dt_bias, dn_norm_w, at_norm_w, w_out, final_norm_w, loss_target, m_w_mod, m_b_mod, m_norm_w, m_w_in, m_conv_w, m_a_log, m_dt_bias, m_dn_norm_w, m_at_norm_w, m_w_out, m_final_norm_w, v_w_mod, v_b_mod, v_norm_w, v_w_in, v_conv_w, v_a_log, v_dt_bias, v_dn_norm_w, v_at_norm_w, v_w_out, v_final_norm_w):
    seq = x.shape[1]
    ax, ay, ac = lax.axis_index("x"), lax.axis_index("y"), lax.axis_index("c")
    me = 4 * ax + 2 * ay + ac
    chip = 2 * ax + ay

    c_all = _all_gather_rows(jnp.pad(c, ((0, 7), (0, 0))), "gather_c").reshape(N_DEV, 8, D_MODEL)[:, 0]
    b_mod_s = lax.dynamic_slice_in_dim(b_mod, chip * 768, 768, axis=1)
    mod_part = _mod_shard(c_all, w_mod[0], b_mod_s)
    mod_all = _all_gather_rows(mod_part, "gather_mod").reshape(N_CHIPS, 2, N_DEV, 768)[:, 0]
    mod_row = lax.dynamic_index_in_dim(mod_all, me, axis=1, keepdims=False).reshape(1, 3 * D_MODEL)

    shards = _gather_weight_shards(_pack_weights_bf16(w_in[0], w_out[0], conv_w[0]))
    w_in_full, w_out_full, conv_full = _unpack_weights_bf16(shards)

    grad_x, g_w_in, g_w_out, g_conv, small = _local_step(
        x[0], loss_target[0], positions.reshape(seq, 1), mod_row, norm_w, w_in_full, conv_full, a_log, dt_bias,
        dn_norm_w, at_norm_w, w_out_full, final_norm_w)

    g_pack = _pack_shards(jnp.transpose(g_w_in.reshape(D_MODEL, N_CHIPS, SHARD_IN), (1, 0, 2)),
                          g_w_out.reshape(N_CHIPS, OUT_ROWS, D_MODEL),
                          jnp.transpose(g_conv.reshape(CONV_K, N_CHIPS, CONV_SHARD), (1, 0, 2)))
    grad_w_in, grad_w_out, grad_conv_rows = _unpack_shards(_reduce_weight_grads(g_pack))
    grad_conv_w = grad_conv_rows[:, :CONV_SHARD]
    delta_w_in, new_m_w_in, new_v_w_in = _adamw_rows(w_in[0], grad_w_in, m_w_in[0], v_w_in[0], "adamw_w_in")
    delta_w_out, new_m_w_out, new_v_w_out = _adamw_rows(w_out[0], grad_w_out, m_w_out[0], v_w_out[0], "adamw_w_out")
    delta_conv_w, new_m_conv_w, new_v_conv_w = _adamw_rows(conv_w[0], grad_conv_w, m_conv_w[0], v_conv_w[0], "adamw_conv_w")

    gathered = _all_gather_rows(jnp.pad(small, ((0, 7), (0, 0))), "gather_small")

    def small_row(norm, fin, dn, at, a, dt, bmod):
        z = lambda n: jnp.zeros((1, n), f32)
        return jnp.concatenate([norm.reshape(1, -1), fin.reshape(1, -1), dn.reshape(1, -1), at.reshape(1, -1), z(64),
                                z(4), a.reshape(1, -1), z(120), z(4), dt.reshape(1, -1), z(120), bmod.reshape(1, -1)], axis=1)

    g_small, d_small, m_small, v_small, loss_row, d_mod_all = _small_update(
        gathered,
        small_row(norm_w, final_norm_w, dn_norm_w, at_norm_w, a_log, dt_bias, b_mod),
        small_row(m_norm_w, m_final_norm_w, m_dn_norm_w, m_at_norm_w, m_a_log, m_dt_bias, m_b_mod),
        small_row(v_norm_w, v_final_norm_w, v_dn_norm_w, v_at_norm_w, v_a_log, v_dt_bias, v_b_mod))

    def split_small(r):
        return (r[:, RS_BMOD:RS_W], r[:, RS_NORM:RS_FIN], r[:, RS_A + DN_HEADS:RS_A + 2 * DN_HEADS],
                r[:, RS_DT + DN_HEADS:RS_DT + 2 * DN_HEADS], r[:, RS_DN:RS_AT], r[:, RS_AT:RS_AT + AT_DIM],
                r[0, RS_FIN:RS_DN])

    d_mod_s = lax.dynamic_slice_in_dim(d_mod_all, chip * 768, 768, axis=1)
    pad_rows = lambda a: jnp.pad(a, ((0, 128 - N_DEV), (0, 0)))
    grad_w_mod, delta_w_mod, new_m_w_mod, new_v_w_mod = _mod_update(
        pad_rows(c_all), pad_rows(d_mod_s), w_mod[0], m_w_mod[0], v_w_mod[0])

    def ordered(w_mod_leaf, small_row_leaf, w_in_leaf, conv_leaf, w_out_leaf):
        b, n, a, dt, dn, at, fin = split_small(small_row_leaf)
        return [w_mod_leaf[None], b, n, w_in_leaf[None], conv_leaf[None], a, dt, dn, at, w_out_leaf[None], fin]

    loss = loss_row[0, 0]
    return (loss, grad_x[None],
            *ordered(grad_w_mod, g_small, grad_w_in, grad_conv_w, grad_w_out),
            *ordered(delta_w_mod, d_small, delta_w_in, delta_conv_w, delta_w_out),
            *ordered(new_m_w_mod, m_small, new_m_w_in, new_m_conv_w, new_m_w_out),
            *ordered(new_v_w_mod, v_small, new_v_w_in, new_v_conv_w, new_v_w_out))
```

```python
import functools

import jax
import jax.numpy as jnp
from jax import lax
from jax.experimental import pallas as pl
from jax.experimental.pallas import tpu as pltpu

f32 = jnp.float32
bf16 = jnp.bfloat16
HIGHEST = lax.Precision.HIGHEST
MESH = pl.DeviceIdType.MESH

D_MODEL = 1024
DN_HEADS = 4
DN_DIM = 128
DN_WIDTH = 512
AT_HEADS = 8
AT_DIM = 64
AT_WIDTH = 512
CHUNK = 64
Q_BLOCK = 128
CONV_K = 4
EPS = 1e-6
ROPE_THETA = 10000.0
PATTERN_DILATIONS = (1, 4, 16)
NEG = -1e30

QKV_W = 3 * DN_WIDTH
BA_W = 128
PDN_W = QKV_W + DN_WIDTH + BA_W
PAT_W = 3 * AT_WIDTH
CAT_W = PDN_W + PAT_W + AT_WIDTH
IN_COLS = 4104
N_CHIPS = 4
N_DEV = 8
SHARD_IN = IN_COLS // N_CHIPS
OUT_ROWS = D_MODEL // N_CHIPS
PACK_W = 1152
PACK_ROWS = 1312
PACK_CHUNK = 16

ADAM_LR = 0.001
ADAM_B1 = 0.9
ADAM_B2 = 0.999
ADAM_EPS = 1e-08
ADAM_WD = 0.01
ADAM_STEP = 10

VMEM_LIMIT = 56 * 1024 * 1024

NN = ((1,), (0,))
NT = ((1,), (1,))
TN = ((0,), (0,))


def _pieces(a, n):
    out = []
    for _ in range(n - 1):
        p = a.astype(bf16)
        out.append(p)
        a = a - p.astype(f32)
    out.append(a.astype(bf16))
    return out


def _dot(a, b, dims, exact):
    raw = lambda p, q: lax.dot_general(p, q, (dims, ((), ())), preferred_element_type=f32)
    if exact == "split":
        (ah, al), (bh, bl) = _pieces(a, 2), _pieces(b, 2)
        return raw(ah, bh) + (raw(ah, bl) + raw(al, bh))
    if exact:
        return lax.dot_general(a, b, (dims, ((), ())), precision=HIGHEST, preferred_element_type=f32)
    return raw(a.astype(bf16), b.astype(bf16))


def _dot_sel(sel, b, dims, sel_side):
    raw = lambda p, q: lax.dot_general(p, q, (dims, ((), ())), preferred_element_type=f32)
    sel = sel.astype(bf16)
    parts = [raw(sel, p) if sel_side == 0 else raw(p, sel) for p in _pieces(b, 3)]
    return (parts[0] + parts[1]) + parts[2]


@jax.custom_vjp
def _sel_left(sel, b):
    return _dot_sel(sel, b, NN, 0)


def _sel_left_fwd(sel, b):
    return _dot_sel(sel, b, NN, 0), sel


def _sel_left_bwd(sel, g):
    return jnp.zeros_like(sel), _dot_sel(sel, g, TN, 0)


_sel_left.defvjp(_sel_left_fwd, _sel_left_bwd)


@jax.custom_vjp
def _sel_right(a, sel):
    return _dot_sel(sel, a, NN, 1)


def _sel_right_fwd(a, sel):
    return _dot_sel(sel, a, NN, 1), sel


def _sel_right_bwd(sel, g):
    return _dot_sel(sel, g, NT, 1), jnp.zeros_like(sel)


_sel_right.defvjp(_sel_right_fwd, _sel_right_bwd)


class _Matmuls:
    def __init__(self, exact, back):
        @jax.custom_vjp
        def nn(a, b):
            return _dot(a, b, NN, exact)

        def nn_fwd(a, b):
            return _dot(a, b, NN, exact), (a, b)

        def nn_bwd(res, g):
            a, b = res
            return _dot(g, b, NT, back), _dot(a, g, TN, back)

        nn.defvjp(nn_fwd, nn_bwd)

        @jax.custom_vjp
        def nt(a, b):
            return _dot(a, b, NT, exact)

        def nt_fwd(a, b):
            return _dot(a, b, NT, exact), (a, b)

        def nt_bwd(res, g):
            a, b = res
            return _dot(g, b, NN, back), _dot(g, a, TN, back)

        nt.defvjp(nt_fwd, nt_bwd)

        @jax.custom_vjp
        def tn(a, b):
            return _dot(a, b, TN, exact)

        def tn_fwd(a, b):
            return _dot(a, b, TN, exact), (a, b)

        def tn_bwd(res, g):
            a, b = res
            return _dot(b, g, NT, back), _dot(a, g, NN, back)

        tn.defvjp(tn_fwd, tn_bwd)
        self.nn, self.nt, self.tn = nn, nt, tn


MM = _Matmuls(exact=False, back=False)
MS = _Matmuls(exact="split", back=False)


def _each(fn, *lists):
    return [fn(*args) for args in zip(*lists)]


def _inverse_products(a_lows):
    ri = lax.broadcasted_iota(jnp.int32, (CHUNK, CHUNK), 0)
    ci = lax.broadcasted_iota(jnp.int32, (CHUNK, CHUNK), 1)
    eye = (ri == ci).astype(f32)
    power = _each(lambda a: -a, a_lows)
    inv = _each(lambda p: eye + p, power)
    for _ in range(5):
        power = _each(lambda p: _dot(p, p, NN, "split"), power)
        inv = _each(lambda x, p: x + _dot(x, p, NN, "split"), inv, power)
    return inv


def _inverse_cotangents(invs, gs):
    left = _each(lambda t, g: _dot(t, g, TN, "split"), invs, gs)
    return _each(lambda l, t: -_dot(l, t, NT, "split"), left, invs)


@jax.custom_vjp
def _unit_lower_inverses(a_lows):
    return _inverse_products(a_lows)


def _unit_lower_inverses_fwd(a_lows):
    invs = _inverse_products(a_lows)
    return invs, invs


def _unit_lower_inverses_bwd(invs, gs):
    return (_inverse_cotangents(invs, gs),)


_unit_lower_inverses.defvjp(_unit_lower_inverses_fwd, _unit_lower_inverses_bwd)


@jax.custom_vjp
def _known_inverses(a_lows, invs):
    return invs


def _known_inverses_fwd(a_lows, invs):
    return invs, invs


def _known_inverses_bwd(invs, gs):
    return _inverse_cotangents(invs, gs), _each(jnp.zeros_like, invs)


_known_inverses.defvjp(_known_inverses_fwd, _known_inverses_bwd)


def _params(semantics=None):
    return pltpu.CompilerParams(dimension_semantics=semantics, vmem_limit_bytes=VMEM_LIMIT)


def _silu(x):
    return x * jax.nn.sigmoid(x)


def _group_ones(width, group):
    r = lax.broadcasted_iota(jnp.int32, (width, width), 0) // group
    c = lax.broadcasted_iota(jnp.int32, (width, width), 1) // group
    return (r == c).astype(f32)


def _slab_sums(x):
    rows, width = x.shape
    sums = [jnp.sum(x[:, j:j + 128], axis=1, keepdims=True) for j in range(0, width, 128)]
    return jnp.concatenate([jnp.broadcast_to(s, (rows, 128)) for s in sums], axis=1)


def _hnorm(x, nw, scale, shift):
    xn = x * lax.rsqrt(jnp.mean(x * x, axis=-1, keepdims=True) + EPS)
    return xn * nw * (1.0 + scale) + shift


def _rope_tables(pos_col, invf_row, sign_row):
    ang = pos_col.astype(f32) * invf_row
    cos_t = jnp.cos(ang)
    sin_t = jnp.sin(ang) * sign_row
    return jnp.concatenate([cos_t] * 4, axis=1), jnp.concatenate([sin_t] * 4, axis=1)


def _rope_partner(x):
    lane = lax.broadcasted_iota(jnp.int32, x.shape, 1)
    width = x.shape[1]
    return jnp.where((lane % AT_DIM) < AT_DIM // 2, pltpu.roll(x, width - AT_DIM // 2, 1), pltpu.roll(x, AT_DIM // 2, 1))


TOKEN_TILE = 256
LANES = 128


def _stage_lanes(stage_ref, first, value):
    for j in range(value.shape[1] // LANES):
        stage_ref[first + j] = value[:, LANES * j:LANES * (j + 1)]


def _stage_to_view(stage_ref, view_ref, dil):
    chunks, rows, _ = stage_ref.shape
    for r in range(dil):
        for j in range(chunks):
            col = (r * chunks + j) * LANES
            view_ref[:, col:col + LANES] = stage_ref.at[j][pl.ds(r, rows // dil, stride=dil), :].astype(view_ref.dtype)


def _view_to_value(stage_ref, view_ref, dil):
    chunks, rows, _ = stage_ref.shape
    for r in range(dil):
        for j in range(chunks):
            col = (r * chunks + j) * LANES
            stage_ref.at[j][pl.ds(r, rows // dil, stride=dil), :] = view_ref[:, col:col + LANES].astype(f32)
    return jnp.concatenate([stage_ref[j] for j in range(chunks)], axis=1)


def _view_spec(width, dil):
    return pl.BlockSpec((TOKEN_TILE // dil, dil * width), lambda i: (i, 0))


def _view_shape(seq, width, dil, dtype=f32):
    return jax.ShapeDtypeStruct((seq // dil, dil * width), dtype)


HEAD_LANES = 128


def _repeat_heads(compact):
    row = lax.broadcasted_iota(jnp.int32, (HEAD_LANES, AT_WIDTH), 0)
    col = lax.broadcasted_iota(jnp.int32, (HEAD_LANES, AT_WIDTH), 1)
    return _dot_sel((row == col // AT_DIM).astype(f32), compact, NN, 1)


def _stage(width):
    return pltpu.VMEM((width // LANES, TOKEN_TILE, LANES), f32)


def _pre_proj(x, pos_col, invf_row, sign_row, norm_w, scale, shift, w_cat):
    seq = x.shape[0]
    tm = TOKEN_TILE

    def body(x_ref, pos_ref, invf_ref, sign_ref, nw_ref, sc_ref, sh_ref, w_ref,
             pdn_ref, pat1_ref, pat4_ref, pat16_ref, zat_ref, h_ref, stage_ref):
        h = _hnorm(x_ref[...], nw_ref[...], sc_ref[...], sh_ref[...]).astype(bf16)
        h_ref[...] = h
        big = jnp.dot(h, w_ref[...], preferred_element_type=f32)
        pdn_ref[...] = big[:, :PDN_W]
        cos_t, sin_t = _rope_tables(pos_ref[...], invf_ref[...], sign_ref[...])
        q = big[:, PDN_W:PDN_W + AT_WIDTH]
        k = big[:, PDN_W + AT_WIDTH:PDN_W + 2 * AT_WIDTH]
        parts = (q * cos_t + _rope_partner(q) * sin_t, k * cos_t + _rope_partner(k) * sin_t,
                 big[:, PDN_W + 2 * AT_WIDTH:PDN_W + PAT_W])
        for p, part in enumerate(parts):
            pat1_ref[:, p * AT_WIDTH:(p + 1) * AT_WIDTH] = part.astype(bf16)
            _stage_lanes(stage_ref, p * (AT_WIDTH // LANES), part)
        _stage_to_view(stage_ref, pat4_ref, 4)
        _stage_to_view(stage_ref, pat16_ref, 16)
        zat_ref[...] = big[:, PDN_W + PAT_W:]

    row = lambda w: pl.BlockSpec((1, w), lambda i: (0, 0))
    tile = lambda w: pl.BlockSpec((tm, w), lambda i: (i, 0))
    return pl.pallas_call(
        body, name="pre_proj", grid=(seq // tm,),
        in_specs=[tile(D_MODEL), tile(1), row(128), row(128), row(D_MODEL), row(D_MODEL), row(D_MODEL),
                  pl.BlockSpec((D_MODEL, CAT_W), lambda i: (0, 0))],
        out_specs=[tile(PDN_W), tile(PAT_W), _view_spec(PAT_W, 4), _view_spec(PAT_W, 16), tile(AT_WIDTH), tile(D_MODEL)],
        out_shape=[jax.ShapeDtypeStruct((seq, PDN_W), f32), _view_shape(seq, PAT_W, 1, bf16), _view_shape(seq, PAT_W, 4, bf16),
                   _view_shape(seq, PAT_W, 16, bf16), jax.ShapeDtypeStruct((seq, AT_WIDTH), f32),
                   jax.ShapeDtypeStruct((seq, D_MODEL), bf16)],
        scratch_shapes=[_stage(PAT_W)],
        compiler_params=_params(("parallel",)),
    )(x, pos_col, invf_row, sign_row, norm_w, scale, shift, w_cat)


def _conv_taps(ext_ref, halo, cur, w8):
    rows = cur.shape[0]
    ext_ref[0:8, :] = halo
    ext_ref[8:8 + rows, :] = cur
    out = ext_ref[pl.ds(5, rows), :] * w8[0:1, :]
    for j in range(1, CONV_K):
        out = out + ext_ref[pl.ds(5 + j, rows), :] * w8[j:j + 1, :]
    return out


def _dn_pre(cq, ck, cv, ba, a_row, dt_row):
    sq, sk, v = _silu(cq), _silu(ck), _silu(cv)
    qn = sq * lax.rsqrt(_slab_sums(sq * sq) + EPS)
    kn = sk * lax.rsqrt(_slab_sums(sk * sk) + EPS)
    beta_all = jax.nn.sigmoid(ba)
    g_all = -jnp.exp(a_row) * jax.nn.softplus(ba + dt_row)
    return qn, kn, v, beta_all, g_all


def _dn_intra(qs, ks, vs, betas, gs, known_invs=None):
    ri = lax.broadcasted_iota(jnp.int32, (CHUNK, CHUNK), 0)
    ci = lax.broadcasted_iota(jnp.int32, (CHUNK, CHUNK), 1)
    tril = ri >= ci
    strict = ri > ci
    lower = tril.astype(f32)
    each = _each
    g_wide = each(lambda g: jnp.broadcast_to(g, (CHUNK, DN_DIM)), gs)
    gc = each(lambda g: _sel_left(lower, g), g_wide)
    gc_sq = each(lambda c: c[:, :CHUNK], gc)
    g_end = each(lambda c: jnp.broadcast_to(c[CHUNK - 1:CHUNK, :], (CHUNK, DN_DIM)), gc)
    g_end8 = each(lambda c: jnp.broadcast_to(c[CHUNK - 1:CHUNK, :], (8, DN_DIM)), gc)
    decay = each(lambda s: jnp.exp(jnp.where(tril, s - s.T, -jnp.inf)), gc_sq)
    qs = each(lambda q: q * (DN_DIM ** -0.5), qs)
    kb = each(lambda k, b: k * b, ks, betas)
    vb = each(lambda v, b: v * b, vs, betas)
    kk = each(MM.nt, kb, ks)
    qk = each(MM.nt, qs, ks)
    a_low = each(lambda p, d: jnp.where(strict, p * d, 0.0), kk, decay)
    inv = _unit_lower_inverses(a_low) if known_invs is None else _known_inverses(a_low, known_invs)
    e_gc = each(jnp.exp, gc)
    u = each(MS.nn, inv, vb)
    w = each(lambda x, k, e: MS.nn(x, k * e), inv, kb, e_gc)
    attn = each(lambda p, d: jnp.where(tril, p * d, 0.0), qk, decay)
    q_dec = each(lambda q, e: q * e, qs, e_gc)
    k_dec = each(lambda k, ge, c: k * jnp.exp(ge - c), ks, g_end, gc)
    return u, w, q_dec, k_dec, attn, each(jnp.exp, g_end8), inv


def _dn_step(us, ws, q_decs, k_decs, attns, e_ends, states):
    each = _each
    v_new = each(lambda u, w, s: u - MM.nn(w, s), us, ws, states)
    qs = each(MM.nn, q_decs, states)
    o = each(lambda a, b, c: a + MM.nn(b, c), qs, attns, v_new)
    new_states = each(lambda s, e, k, v: s * e + MM.tn(k, v), states, e_ends, k_decs, v_new)
    return o, new_states


INTRA_CHUNKS = 2
SCAN_CHUNKS = 4


def _intra_specs(nc):
    rows = INTRA_CHUNKS * CHUNK
    cur = pl.BlockSpec((rows, QKV_W), lambda i: (i, 0))
    halo = pl.BlockSpec((8, QKV_W), lambda i: (jnp.maximum(i * (rows // 8) - 1, 0), 0))
    ba = pl.BlockSpec((rows, BA_W), lambda i: (i, (QKV_W + DN_WIDTH) // BA_W))
    conv = pl.BlockSpec((8, QKV_W), lambda i: (0, 0))
    row = pl.BlockSpec((1, BA_W), lambda i: (0, 0))
    wide = pl.BlockSpec((rows, DN_WIDTH), lambda i: (i, 0))
    attn = pl.BlockSpec((INTRA_CHUNKS, DN_HEADS, CHUNK, CHUNK), lambda i: (i, 0, 0, 0))
    e_end = pl.BlockSpec((INTRA_CHUNKS, 8 * DN_HEADS, DN_DIM), lambda i: (i, 0, 0))
    return cur, halo, ba, conv, row, wide, attn, e_end


def _intra_items():
    return [(ci, h, slice(ci * CHUNK, (ci + 1) * CHUNK), slice(h * DN_DIM, (h + 1) * DN_DIM))
            for ci in range(INTRA_CHUNKS) for h in range(DN_HEADS)]


def _intra_inputs(items, qn, kn, v, beta_all, g_all):
    return ([qn[rows, lanes] for _, _, rows, lanes in items], [kn[rows, lanes] for _, _, rows, lanes in items],
            [v[rows, lanes] for _, _, rows, lanes in items], [beta_all[rows, h:h + 1] for _, h, rows, _ in items],
            [g_all[rows, DN_HEADS + h:DN_HEADS + h + 1] for _, h, rows, _ in items])


def _intra_shapes(seq, operand_dtype):
    nc = seq // CHUNK
    wide = jax.ShapeDtypeStruct((seq, DN_WIDTH), f32)
    operand = jax.ShapeDtypeStruct((seq, DN_WIDTH), operand_dtype)
    return [wide, operand, operand, operand, jax.ShapeDtypeStruct((nc, DN_HEADS, CHUNK, CHUNK), f32),
            jax.ShapeDtypeStruct((nc, 8 * DN_HEADS, DN_DIM), f32)]


def _dn_intra_forward(pdn, conv_w8, a_row, dt_row):
    seq = pdn.shape[0]
    nc = seq // CHUNK
    cur, halo, ba, conv, row, wide, attn, e_end = _intra_specs(nc)

    def body(cur_ref, halo_ref, ba_ref, w_ref, a_ref, dt_ref, u_ref, w_out_ref, qd_ref, kd_ref, attn_ref, e_ref, inv_ref,
             ext_ref):
        halo_rows = jnp.where(pl.program_id(0) > 0, halo_ref[...], 0.0)
        c = _conv_taps(ext_ref, halo_rows, cur_ref[...], w_ref[...])
        qn, kn, v, beta_all, g_all = _dn_pre(c[:, :DN_WIDTH], c[:, DN_WIDTH:2 * DN_WIDTH], c[:, 2 * DN_WIDTH:],
                                             ba_ref[...], a_ref[...], dt_ref[...])
        items = _intra_items()
        u, w, qd, kd, at, e8, inv = _dn_intra(*_intra_inputs(items, qn, kn, v, beta_all, g_all))
        for n, (ci, h, rows, lanes) in enumerate(items):
            u_ref[rows, lanes] = u[n]
            w_out_ref[rows, lanes] = w[n].astype(bf16)
            qd_ref[rows, lanes] = qd[n].astype(bf16)
            kd_ref[rows, lanes] = kd[n].astype(bf16)
            attn_ref[ci, h] = at[n]
            e_ref[ci, 8 * h:8 * h + 8, :] = e8[n]
            inv_ref[ci, h] = inv[n]

    return pl.pallas_call(
        body, name="dn_intra_forward", grid=(nc // INTRA_CHUNKS,),
        in_specs=[cur, halo, ba, conv, row, row],
        out_specs=[wide, wide, wide, wide, attn, e_end, attn],
        out_shape=_intra_shapes(seq, bf16) + [jax.ShapeDtypeStruct((nc, DN_HEADS, CHUNK, CHUNK), f32)],
        scratch_shapes=[pltpu.VMEM((INTRA_CHUNKS * CHUNK + 8, QKV_W), f32)],
        compiler_params=_params(("parallel",)),
    )(pdn, pdn, pdn, conv_w8, a_row, dt_row)


def _scan_specs(nc, reverse):
    steps = nc // SCAN_CHUNKS
    at = (lambda i: steps - 1 - i) if reverse else (lambda i: i)
    wide = pl.BlockSpec((SCAN_CHUNKS * CHUNK, DN_WIDTH), lambda i: (at(i), 0))
    attn = pl.BlockSpec((SCAN_CHUNKS, DN_HEADS, CHUNK, CHUNK), lambda i: (at(i), 0, 0, 0))
    e_end = pl.BlockSpec((SCAN_CHUNKS, 8 * DN_HEADS, DN_DIM), lambda i: (at(i), 0, 0))
    states = pl.BlockSpec((SCAN_CHUNKS, DN_HEADS, DN_DIM, DN_DIM), lambda i: (at(i), 0, 0, 0))
    return wide, attn, e_end, states


def _step_inputs(ci, rows, lanes, u_ref, w_ref, qd_ref, kd_ref, attn_ref, e_ref):
    heads = range(DN_HEADS)
    return ([u_ref[rows, lanes[h]] for h in heads], [w_ref[rows, lanes[h]].astype(f32) for h in heads],
            [qd_ref[rows, lanes[h]].astype(f32) for h in heads], [kd_ref[rows, lanes[h]].astype(f32) for h in heads],
            [attn_ref[ci, h] for h in heads], [e_ref[ci, 8 * h:8 * h + 1, :] for h in heads])


def _dn_scan_forward(u, w, q_dec, k_dec, attn, e_end):
    seq = u.shape[0]
    nc = seq // CHUNK
    wide, attn_spec, e_spec, st_spec = _scan_specs(nc, reverse=False)

    def body(u_ref, w_ref, qd_ref, kd_ref, attn_ref, e_ref, o_ref, st_ref, state_ref):
        @pl.when(pl.program_id(0) == 0)
        def _():
            state_ref[...] = jnp.zeros_like(state_ref)

        heads = range(DN_HEADS)
        lanes = [slice(h * DN_DIM, (h + 1) * DN_DIM) for h in heads]
        states = [state_ref[h] for h in heads]
        for ci in range(SCAN_CHUNKS):
            rows = slice(ci * CHUNK, (ci + 1) * CHUNK)
            for h in heads:
                st_ref[ci, h] = states[h].astype(bf16)
            o, states = _dn_step(*_step_inputs(ci, rows, lanes, u_ref, w_ref, qd_ref, kd_ref, attn_ref, e_ref), states)
            for h in heads:
                o_ref[rows, lanes[h]] = o[h]
        for h in heads:
            state_ref[h] = states[h]

    return pl.pallas_call(
        body, name="dn_scan_forward", grid=(nc // SCAN_CHUNKS,),
        in_specs=[wide, wide, wide, wide, attn_spec, e_spec],
        out_specs=[wide, st_spec],
        out_shape=[jax.ShapeDtypeStruct((seq, DN_WIDTH), f32), jax.ShapeDtypeStruct((nc, DN_HEADS, DN_DIM, DN_DIM), bf16)],
        scratch_shapes=[pltpu.VMEM((DN_HEADS, DN_DIM, DN_DIM), f32)],
        compiler_params=_params(("arbitrary",)),
    )(u, w, q_dec, k_dec, attn, e_end)


def _dn_scan_backward(u, w, q_dec, k_dec, attn, e_end, states, d_o):
    seq = u.shape[0]
    nc = seq // CHUNK
    wide, attn_spec, e_spec, st_spec = _scan_specs(nc, reverse=True)

    def body(u_ref, w_ref, qd_ref, kd_ref, attn_ref, e_ref, st_ref, do_ref,
             du_ref, dw_ref, dqd_ref, dkd_ref, dattn_ref, de_ref, dstate_ref):
        @pl.when(pl.program_id(0) == 0)
        def _():
            dstate_ref[...] = jnp.zeros_like(dstate_ref)

        heads = range(DN_HEADS)
        lanes = [slice(h * DN_DIM, (h + 1) * DN_DIM) for h in heads]
        first_row = lax.broadcasted_iota(jnp.int32, (8, DN_DIM), 0) == 0
        dstates = [dstate_ref[h] for h in heads]
        for ci in reversed(range(SCAN_CHUNKS)):
            rows = slice(ci * CHUNK, (ci + 1) * CHUNK)
            _, step_vjp = jax.vjp(_dn_step, *_step_inputs(ci, rows, lanes, u_ref, w_ref, qd_ref, kd_ref, attn_ref, e_ref),
                                  [st_ref[ci, h].astype(f32) for h in heads])
            du, dw, dqd, dkd, dattn, de, dstates = step_vjp(([do_ref[rows, lanes[h]] for h in heads], dstates))
            for h in heads:
                du_ref[rows, lanes[h]] = du[h]
                dw_ref[rows, lanes[h]] = dw[h]
                dqd_ref[rows, lanes[h]] = dqd[h]
                dkd_ref[rows, lanes[h]] = dkd[h]
                dattn_ref[ci, h] = dattn[h]
                de_ref[ci, 8 * h:8 * h + 8, :] = jnp.where(first_row, jnp.broadcast_to(de[h], (8, DN_DIM)), 0.0)
        for h in heads:
            dstate_ref[h] = dstates[h]

    return pl.pallas_call(
        body, name="dn_scan_backward", grid=(nc // SCAN_CHUNKS,),
        in_specs=[wide, wide, wide, wide, attn_spec, e_spec, st_spec, wide],
        out_specs=[wide, wide, wide, wide, attn_spec, e_spec],
        out_shape=_intra_shapes(seq, f32),
        scratch_shapes=[pltpu.VMEM((DN_HEADS, DN_DIM, DN_DIM), f32)],
        compiler_params=_params(("arbitrary",)),
    )(u, w, q_dec, k_dec, attn, e_end, states, d_o)


def _dn_intra_backward(pdn, conv_w8, a_row, dt_row, invs, d_u, d_w, d_qd, d_kd, d_attn, d_e):
    seq = pdn.shape[0]
    nc = seq // CHUNK
    rows_per_step = INTRA_CHUNKS * CHUNK
    cur, halo, ba, conv, row, wide, attn, e_end = _intra_specs(nc)

    def body(cur_ref, halo_ref, ba_ref, w_ref, a_ref, dt_ref, inv_ref, du_ref, dw_ref, dqd_ref, dkd_ref, dattn_ref, de_ref,
             dconv_ref, dba_ref, drow_ref, ext_ref):
        @pl.when(pl.program_id(0) == 0)
        def _():
            drow_ref[...] = jnp.zeros_like(drow_ref)

        halo_rows = jnp.where(pl.program_id(0) > 0, halo_ref[...], 0.0)
        c = _conv_taps(ext_ref, halo_rows, cur_ref[...], w_ref[...])
        (qn, kn, v, beta_all, g_all), pre_vjp = jax.vjp(
            _dn_pre, c[:, :DN_WIDTH], c[:, DN_WIDTH:2 * DN_WIDTH], c[:, 2 * DN_WIDTH:], ba_ref[...], a_ref[...], dt_ref[...])
        lane = lax.broadcasted_iota(jnp.int32, (CHUNK, BA_W), 1)
        items = _intra_items()
        _, intra_vjp = jax.vjp(_dn_intra, *_intra_inputs(items, qn, kn, v, beta_all, g_all),
                               [inv_ref[ci, h] for ci, h, _, _ in items])
        dq, dk, dv, dbeta, dg, _ = intra_vjp((
            [du_ref[rows, lanes] for _, _, rows, lanes in items], [dw_ref[rows, lanes] for _, _, rows, lanes in items],
            [dqd_ref[rows, lanes] for _, _, rows, lanes in items], [dkd_ref[rows, lanes] for _, _, rows, lanes in items],
            [dattn_ref[ci, h] for ci, h, _, _ in items], [de_ref[ci, 8 * h:8 * h + 8, :] for ci, h, _, _ in items],
            [jnp.zeros((CHUNK, CHUNK), f32) for _ in items]))
        dq_rows, dk_rows, dv_rows, dbeta_rows, dg_rows = [], [], [], [], []
        for ci in range(INTRA_CHUNKS):
            of_chunk = [n for n, item in enumerate(items) if item[0] == ci]
            d_beta_all = jnp.zeros((CHUNK, BA_W), f32)
            d_g_all = jnp.zeros((CHUNK, BA_W), f32)
            for n in of_chunk:
                h = items[n][1]
                d_beta_all = d_beta_all + jnp.where(lane == h, dbeta[n], 0.0)
                d_g_all = d_g_all + jnp.where(lane == DN_HEADS + h, dg[n], 0.0)
            dq_rows.append(jnp.concatenate([dq[n] for n in of_chunk], axis=1))
            dk_rows.append(jnp.concatenate([dk[n] for n in of_chunk], axis=1))
            dv_rows.append(jnp.concatenate([dv[n] for n in of_chunk], axis=1))
            dbeta_rows.append(d_beta_all)
            dg_rows.append(d_g_all)
        stack = lambda parts: jnp.concatenate(parts, axis=0)
        dcq, dck, dcv, dba, da_row, ddt_row = pre_vjp(
            (stack(dq_rows), stack(dk_rows), stack(dv_rows), stack(dbeta_rows), stack(dg_rows)))
        dconv_ref[:, :DN_WIDTH] = dcq
        dconv_ref[:, DN_WIDTH:2 * DN_WIDTH] = dck
        dconv_ref[:, 2 * DN_WIDTH:] = dcv
        dba_ref[...] = dba
        drow_ref[0:1, :] += da_row
        drow_ref[1:2, :] += ddt_row

    return pl.pallas_call(
        body, name="dn_intra_backward", grid=(nc // INTRA_CHUNKS,),
        in_specs=[cur, halo, ba, conv, row, row, attn, wide, wide, wide, wide, attn, e_end],
        out_specs=[pl.BlockSpec((rows_per_step, QKV_W), lambda i: (i, 0)),
                   pl.BlockSpec((rows_per_step, BA_W), lambda i: (i, 0)),
                   pl.BlockSpec((8, BA_W), lambda i: (0, 0))],
        out_shape=[jax.ShapeDtypeStruct((seq, QKV_W), f32), jax.ShapeDtypeStruct((seq, BA_W), f32),
                   jax.ShapeDtypeStruct((8, BA_W), f32)],
        scratch_shapes=[pltpu.VMEM((rows_per_step + 8, QKV_W), f32)],
        compiler_params=_params(("arbitrary",)),
    )(pdn, pdn, pdn, conv_w8, a_row, dt_row, invs, d_u, d_w, d_qd, d_kd, d_attn, d_e)


def _band_masks(block_index):
    qi = lax.broadcasted_iota(jnp.int32, (Q_BLOCK, Q_BLOCK), 0)
    kj = lax.broadcasted_iota(jnp.int32, (Q_BLOCK, Q_BLOCK), 1)
    return (kj >= qi) & (block_index > 0), kj <= qi


def _low_half():
    return lax.broadcasted_iota(jnp.int32, (Q_BLOCK, LANES), 1) < AT_DIM


def _head_pairs(ref, split):
    low = _low_half()
    slabs = [ref[:, pair * LANES:(pair + 1) * LANES].astype(bf16) for pair in range(AT_HEADS // 2)]
    if not split:
        return slabs
    zero = jnp.zeros((Q_BLOCK, LANES), bf16)
    return [jnp.where(low, slab, zero) if h == 0 else jnp.where(low, zero, slab) for slab in slabs for h in range(2)]


def _stack_blocks(first, second):
    return [jnp.concatenate([a, b], axis=0) for a, b in zip(first, second)]


def _two_block_spec(width, pieces, piece, which, n_blocks=None):
    if which == "own":
        return pl.BlockSpec((2 * Q_BLOCK, width), lambda r, n: (n, pieces * r + piece))
    if which == "before":
        return pl.BlockSpec((Q_BLOCK, width), lambda r, n: (jnp.maximum(2 * n - 1, 0), pieces * r + piece))
    return pl.BlockSpec((Q_BLOCK, width), lambda r, n: (jnp.minimum(2 * n + 2, n_blocks - 1), pieces * r + piece))


def _attn_forward(pat_view, dil):
    length = pat_view.shape[0]
    nb = length // Q_BLOCK
    scale = AT_DIM ** -0.5

    def body(q2_ref, kp_ref, k2_ref, vp_ref, v2_ref, o2_ref, lse2_ref):
        first, second = pl.ds(0, Q_BLOCK), pl.ds(Q_BLOCK, Q_BLOCK)
        n = pl.program_id(1)
        one_block(2 * n, q2_ref.at[first], kp_ref, k2_ref.at[first], vp_ref, v2_ref.at[first],
                  o2_ref.at[first], lse2_ref.at[first])
        one_block(2 * n + 1, q2_ref.at[second], k2_ref.at[first], k2_ref.at[second], v2_ref.at[first], v2_ref.at[second],
                  o2_ref.at[second], lse2_ref.at[second])

    def one_block(block_index, q_ref, kp_ref, kc_ref, vp_ref, vc_ref, o_ref, lse_ref):
        mask = jnp.concatenate(_band_masks(block_index), axis=1)
        low = _low_half()
        heads = range(AT_HEADS)
        q = _head_pairs(q_ref, split=True)
        k = _stack_blocks(_head_pairs(kp_ref, split=False), _head_pairs(kc_ref, split=False))
        v = _stack_blocks(_head_pairs(vp_ref, split=False), _head_pairs(vc_ref, split=False))
        s = [_dot(q[h], k[h // 2], NT, False) for h in heads]
        p, top = [], []
        for h in heads:
            masked = jnp.where(mask, s[h] * scale, NEG)
            m = jnp.max(masked, axis=1, keepdims=True)
            p.append(jnp.exp(masked - m).astype(bf16))
            top.append(m)
        ones = jnp.ones((2 * Q_BLOCK, LANES), bf16)
        l = [_dot(p[h], ones, NN, False) for h in heads]
        o = [_dot(p[h], v[h // 2], NN, False) for h in heads]
        for pair in range(AT_HEADS // 2):
            even, odd = 2 * pair, 2 * pair + 1
            slab = slice(pair * LANES, (pair + 1) * LANES)
            o_ref[:, slab] = jnp.where(low, o[even], o[odd]) / jnp.where(low, l[even], l[odd])
        lane = lax.broadcasted_iota(jnp.int32, (Q_BLOCK, HEAD_LANES), 1)
        lse = jnp.zeros((Q_BLOCK, HEAD_LANES), f32)
        for h in heads:
            lse = jnp.where(lane == h, top[h] + jnp.log(l[h]), lse)
        lse_ref[...] = lse

    blk = functools.partial(_two_block_spec, AT_WIDTH, 3)
    return pl.pallas_call(
        body, name=f"attn_forward_d{dil}", grid=(dil, nb // 2),
        in_specs=[blk(0, "own"), blk(1, "before"), blk(1, "own"), blk(2, "before"), blk(2, "own")],
        out_specs=[_two_block_spec(AT_WIDTH, 1, 0, "own"), _two_block_spec(HEAD_LANES, 1, 0, "own")],
        out_shape=[jax.ShapeDtypeStruct((length, dil * AT_WIDTH), f32),
                   jax.ShapeDtypeStruct((length, dil * HEAD_LANES), f32)],
        compiler_params=_params(("parallel", "parallel")),
    )(pat_view, pat_view, pat_view, pat_view, pat_view)


def _attn_backward_q(pat_view, d_out, delta, lse, dil):
    length = pat_view.shape[0]
    nb = length // Q_BLOCK
    scale = AT_DIM ** -0.5

    def body(q2_ref, kp_ref, k2_ref, vp_ref, v2_ref, do2_ref, dl2_ref, lse2_ref, dq2_ref):
        first, second = pl.ds(0, Q_BLOCK), pl.ds(Q_BLOCK, Q_BLOCK)
        n = pl.program_id(1)
        one_block(2 * n, q2_ref.at[first], kp_ref, k2_ref.at[first], vp_ref, v2_ref.at[first],
                  do2_ref.at[first], dl2_ref.at[first], lse2_ref.at[first], dq2_ref.at[first])
        one_block(2 * n + 1, q2_ref.at[second], k2_ref.at[first], k2_ref.at[second], v2_ref.at[first], v2_ref.at[second],
                  do2_ref.at[second], dl2_ref.at[second], lse2_ref.at[second], dq2_ref.at[second])

    def one_block(block_index, q_ref, kp_ref, kc_ref, vp_ref, vc_ref, do_ref, dl_ref, lse_ref, dq_ref):
        mask = jnp.concatenate(_band_masks(block_index), axis=1)
        low = _low_half()
        heads = range(AT_HEADS)
        q, do = _head_pairs(q_ref, split=True), _head_pairs(do_ref, split=True)
        k = _stack_blocks(_head_pairs(kp_ref, split=False), _head_pairs(kc_ref, split=False))
        v = _stack_blocks(_head_pairs(vp_ref, split=False), _head_pairs(vc_ref, split=False))
        s = [_dot(q[h], k[h // 2], NT, False) for h in heads]
        dp = [_dot(do[h], v[h // 2], NT, False) for h in heads]
        ds = []
        for h in heads:
            p = jnp.exp(jnp.where(mask, s[h] * scale - lse_ref[:, h:h + 1], NEG))
            ds.append((p * (dp[h] - dl_ref[:, h:h + 1])).astype(bf16))
        dq = [_dot(ds[h], k[h // 2], NN, False) for h in heads]
        for pair in range(AT_HEADS // 2):
            dq_ref[:, pair * LANES:(pair + 1) * LANES] = (
                jnp.where(low, dq[2 * pair], dq[2 * pair + 1]) * scale).astype(bf16)

    blk = functools.partial(_two_block_spec, AT_WIDTH, 3)
    one = _two_block_spec(AT_WIDTH, 1, 0, "own")
    compact = _two_block_spec(HEAD_LANES, 1, 0, "own")
    return pl.pallas_call(
        body, name=f"attn_backward_q_d{dil}", grid=(dil, nb // 2),
        in_specs=[blk(0, "own"), blk(1, "before"), blk(1, "own"), blk(2, "before"), blk(2, "own"), one, compact, compact],
        out_specs=one,
        out_shape=jax.ShapeDtypeStruct((length, dil * AT_WIDTH), bf16),
        compiler_params=_params(("parallel", "parallel")),
    )(pat_view, pat_view, pat_view, pat_view, pat_view, d_out, delta, lse)


def _attn_backward_kv(pat_view, d_out, delta, lse, dil):
    length = pat_view.shape[0]
    nb = length // Q_BLOCK
    scale = AT_DIM ** -0.5

    def body(k2_ref, v2_ref, q2_ref, qn_ref, do2_ref, don_ref, dl2_ref, dln_ref, lse2_ref, lsen_ref, dk2_ref, dv2_ref):
        first, second = pl.ds(0, Q_BLOCK), pl.ds(Q_BLOCK, Q_BLOCK)
        n = pl.program_id(1)
        one_block(2 * n, k2_ref.at[first], v2_ref.at[first], q2_ref.at[first], q2_ref.at[second],
                  do2_ref.at[first], do2_ref.at[second], dl2_ref.at[first], dl2_ref.at[second],
                  lse2_ref.at[first], lse2_ref.at[second], dk2_ref.at[first], dv2_ref.at[first])
        one_block(2 * n + 1, k2_ref.at[second], v2_ref.at[second], q2_ref.at[second], qn_ref,
                  do2_ref.at[second], don_ref, dl2_ref.at[second], dln_ref,
                  lse2_ref.at[second], lsen_ref, dk2_ref.at[second], dv2_ref.at[second])

    def one_block(j, k_ref, v_ref, qa_ref, qb_ref, doa_ref, dob_ref, dla_ref, dlb_ref, lsea_ref, lseb_ref, dk_ref, dv_ref):
        kj = lax.broadcasted_iota(jnp.int32, (Q_BLOCK, Q_BLOCK), 0)
        qi = lax.broadcasted_iota(jnp.int32, (Q_BLOCK, Q_BLOCK), 1)
        mask = jnp.concatenate([kj <= qi,
                                (kj >= qi) & (j + 1 < nb)],
                               axis=1)
        low = _low_half()
        heads = range(AT_HEADS)
        k, v = _head_pairs(k_ref, split=True), _head_pairs(v_ref, split=True)
        q = _stack_blocks(_head_pairs(qa_ref, split=False), _head_pairs(qb_ref, split=False))
        do = _stack_blocks(_head_pairs(doa_ref, split=False), _head_pairs(dob_ref, split=False))
        row = lax.broadcasted_iota(jnp.int32, (8 * AT_HEADS, HEAD_LANES), 0)
        col = lax.broadcasted_iota(jnp.int32, (8 * AT_HEADS, HEAD_LANES), 1)
        pick = (col == row // 8).astype(f32)
        lse_rows = _dot_sel(pick, jnp.concatenate([lsea_ref[...], lseb_ref[...]], axis=0), NT, 0)
        dl_rows = _dot_sel(pick, jnp.concatenate([dla_ref[...], dlb_ref[...]], axis=0), NT, 0)
        s_t = [_dot(k[h], q[h // 2], NT, False) for h in heads]
        dp_t = [_dot(v[h], do[h // 2], NT, False) for h in heads]
        p_t, ds_t = [], []
        for h in heads:
            prob = jnp.exp(jnp.where(mask, s_t[h] * scale - lse_rows[8 * h:8 * h + 1, :], NEG))
            p_t.append(prob.astype(bf16))
            ds_t.append((prob * (dp_t[h] - dl_rows[8 * h:8 * h + 1, :])).astype(bf16))
        dv = [_dot(p_t[h], do[h // 2], NN, False) for h in heads]
        dk = [_dot(ds_t[h], q[h // 2], NN, False) for h in heads]
        for pair in range(AT_HEADS // 2):
            slab = slice(pair * LANES, (pair + 1) * LANES)
            dk_ref[:, slab] = (jnp.where(low, dk[2 * pair], dk[2 * pair + 1]) * scale).astype(bf16)
            dv_ref[:, slab] = jnp.where(low, dv[2 * pair], dv[2 * pair + 1]).astype(bf16)

    blk = functools.partial(_two_block_spec, AT_WIDTH, 3)
    same = _two_block_spec(AT_WIDTH, 1, 0, "own")
    nxt = _two_block_spec(AT_WIDTH, 1, 0, "after", nb)
    c_same = _two_block_spec(HEAD_LANES, 1, 0, "own")
    c_nxt = _two_block_spec(HEAD_LANES, 1, 0, "after", nb)
    return pl.pallas_call(
        body, name=f"attn_backward_kv_d{dil}", grid=(dil, nb // 2),
        in_specs=[blk(1, "own"), blk(2, "own"), blk(0, "own"), blk(0, "after", nb), same, nxt, c_same, c_nxt, c_same, c_nxt],
        out_specs=[same, same],
        out_shape=[jax.ShapeDtypeStruct((length, dil * AT_WIDTH), bf16)] * 2,
        compiler_params=_params(("parallel", "parallel")),
    )(pat_view, pat_view, pat_view, pat_view, d_out, d_out, delta, delta, lse, lse)


def _gated_norms(o_dn, z_dn, o_at, z_at, dn_w, at_w):
    ms_dn = _slab_sums(o_dn * o_dn) * (1.0 / DN_DIM)
    a = o_dn * lax.rsqrt(ms_dn + EPS) * dn_w * _silu(z_dn)
    head_of_lane = (lax.broadcasted_iota(jnp.int32, (AT_WIDTH, HEAD_LANES), 0) // AT_DIM
                    == lax.broadcasted_iota(jnp.int32, (AT_WIDTH, HEAD_LANES), 1)).astype(f32)
    lanes_of_head = (lax.broadcasted_iota(jnp.int32, (HEAD_LANES, AT_WIDTH), 0)
                     == lax.broadcasted_iota(jnp.int32, (HEAD_LANES, AT_WIDTH), 1) // AT_DIM).astype(f32)
    ms_at = _sel_right(_sel_right(o_at * o_at, head_of_lane), lanes_of_head) * (1.0 / AT_DIM)
    b = o_at * lax.rsqrt(ms_at + EPS) * at_w * _silu(z_at)
    return a, b


def _residual_loss(x, mix, gate, fin_w, target):
    x2 = x + gate * mix
    y = x2 * lax.rsqrt(jnp.mean(x2 * x2, axis=-1, keepdims=True) + EPS) * fin_w
    err = y - target
    per_token = jnp.sum(err * err, axis=1, keepdims=True) * (1.0 / D_MODEL)
    return 0.5 * jnp.sum(per_token, axis=0, keepdims=True)


TAIL_ROWS = 8


def _tail(x, target, o_dn, pdn, outs, lses, z_at, gate, w_out, dn_w_row, at_w_row, fin_w):
    seq = x.shape[0]
    tm = TOKEN_TILE

    def body(x_ref, t_ref, odn_ref, zdn_ref, o1_ref, o4_ref, o16_ref, l1_ref, l4_ref, l16_ref, zat_ref, gate_ref,
             wt_ref, wb_ref, dnw_ref, atw_ref, fw_ref,
             dx2_ref, dodn_ref, dzdn_ref, doat1_ref, doat4_ref, doat16_ref, delta1_ref, delta4_ref, delta16_ref,
             lse1_ref, lse4_ref, lse16_ref, dzat_ref, gwt_ref, gwb_ref, rows_ref, *stages):
        @pl.when(pl.program_id(0) == 0)
        def _():
            gwt_ref[...] = jnp.zeros_like(gwt_ref)
            gwb_ref[...] = jnp.zeros_like(gwb_ref)
            rows_ref[...] = jnp.zeros_like(rows_ref)

        l1, l4, l16 = l1_ref[...], _view_to_value(stages[0], l4_ref, 4), _view_to_value(stages[1], l16_ref, 16)
        o4, o16 = _view_to_value(stages[2], o4_ref, 4), _view_to_value(stages[3], o16_ref, 16)
        top = jnp.maximum(jnp.maximum(l1, l4), l16)
        e1, e4, e16 = jnp.exp(l1 - top), jnp.exp(l4 - top), jnp.exp(l16 - top)
        den = e1 + e4 + e16
        lse = top + jnp.log(den)
        o_at = (_repeat_heads(e1 / den) * o1_ref[...] + _repeat_heads(e4 / den) * o4 + _repeat_heads(e16 / den) * o16)

        (a, b), norms_vjp = jax.vjp(_gated_norms, odn_ref[...], zdn_ref[...], o_at, zat_ref[...], dnw_ref[...], atw_ref[...])
        a, b = a.astype(bf16), b.astype(bf16)
        mix = jnp.dot(a, wt_ref[...], preferred_element_type=f32) + jnp.dot(b, wb_ref[...], preferred_element_type=f32)
        loss, loss_vjp = jax.vjp(_residual_loss, x_ref[...], mix, gate_ref[...], fw_ref[...], t_ref[...])
        dx2, dmix, dgate, dfw, _ = loss_vjp(jnp.ones((1, 1), f32))
        dmix = dmix.astype(bf16)
        dwt = lax.dot_general(a, dmix, (TN, ((), ())), preferred_element_type=f32)
        dwb = lax.dot_general(b, dmix, (TN, ((), ())), preferred_element_type=f32)
        da = lax.dot_general(dmix, wt_ref[...], (NT, ((), ())), preferred_element_type=f32)
        db = lax.dot_general(dmix, wb_ref[...], (NT, ((), ())), preferred_element_type=f32)
        dodn, dzdn, doat, dzat, ddnw, datw = norms_vjp((da, db))
        dx2_ref[...] = dx2
        dodn_ref[...] = dodn
        dzdn_ref[...] = dzdn
        dzat_ref[...] = dzat
        lane_head = (lax.broadcasted_iota(jnp.int32, (AT_WIDTH, HEAD_LANES), 0) // AT_DIM
                     == lax.broadcasted_iota(jnp.int32, (AT_WIDTH, HEAD_LANES), 1))
        delta = _dot_sel(lane_head.astype(f32), doat * o_at, NN, 1)
        for stage_ref, value, refs in ((stages[2], doat, (doat1_ref, doat4_ref, doat16_ref)),
                                       (stages[0], delta, (delta1_ref, delta4_ref, delta16_ref)),
                                       (stages[1], lse, (lse1_ref, lse4_ref, lse16_ref))):
            refs[0][...] = value.astype(refs[0].dtype)
            _stage_lanes(stage_ref, 0, value)
            _stage_to_view(stage_ref, refs[1], 4)
            _stage_to_view(stage_ref, refs[2], 16)
        gwt_ref[...] += dwt
        gwb_ref[...] += dwb
        rows_ref[0:1, :] += jnp.broadcast_to(loss, (1, D_MODEL))
        rows_ref[1:2, :] += dgate
        rows_ref[2:3, :] += dfw
        rows_ref[3:4, 0:DN_WIDTH] += ddnw
        rows_ref[4:5, 0:AT_WIDTH] += datw

    tile = lambda w: pl.BlockSpec((tm, w), lambda i: (i, 0))
    row = lambda w: pl.BlockSpec((1, w), lambda i: (0, 0))
    half_w = pl.BlockSpec((DN_WIDTH, D_MODEL), lambda i: (0, 0))
    sds = lambda w: jax.ShapeDtypeStruct((seq, w), f32)
    views = [_view_spec(AT_WIDTH, d) for d in PATTERN_DILATIONS]
    compact = [_view_spec(HEAD_LANES, d) for d in PATTERN_DILATIONS]
    doat_shapes = [_view_shape(seq, AT_WIDTH, d, bf16) for d in PATTERN_DILATIONS]
    compact_shapes = [_view_shape(seq, HEAD_LANES, d) for d in PATTERN_DILATIONS]
    return pl.pallas_call(
        body, name="tail", grid=(seq // tm,),
        in_specs=[tile(D_MODEL), tile(D_MODEL), tile(DN_WIDTH),
                  pl.BlockSpec((tm, DN_WIDTH), lambda i: (i, QKV_W // DN_WIDTH)),
                  *views, *compact, tile(AT_WIDTH), row(D_MODEL), half_w, pl.BlockSpec((AT_WIDTH, D_MODEL), lambda i: (1, 0)),
                  row(DN_WIDTH), row(AT_WIDTH), row(D_MODEL)],
        out_specs=[tile(D_MODEL), tile(DN_WIDTH), tile(DN_WIDTH), *views, *compact, *compact,
                   tile(AT_WIDTH), half_w, half_w, pl.BlockSpec((TAIL_ROWS, D_MODEL), lambda i: (0, 0))],
        out_shape=[sds(D_MODEL), sds(DN_WIDTH), sds(DN_WIDTH), *doat_shapes, *compact_shapes, *compact_shapes,
                   sds(AT_WIDTH), jax.ShapeDtypeStruct((DN_WIDTH, D_MODEL), f32),
                   jax.ShapeDtypeStruct((AT_WIDTH, D_MODEL), f32), jax.ShapeDtypeStruct((TAIL_ROWS, D_MODEL), f32)],
        scratch_shapes=[_stage(HEAD_LANES)] * 2 + [_stage(AT_WIDTH)] * 2,
        compiler_params=_params(("arbitrary",)),
    )(x, target, o_dn, pdn, outs[0], outs[1], outs[2], lses[0], lses[1], lses[2], z_at, gate, w_out, w_out,
      dn_w_row, at_w_row, fin_w)


PRE_ROWS = 8


def _pre_backward(x, dx2, pos_col, invf_row, sign_row, norm_w, scale, shift, w_cat, conv_w8, pdn, d_conv, d_zdn, d_ba,
                  d_q, d_k, d_v, d_zat):
    seq = x.shape[0]
    tm = TOKEN_TILE
    last = seq // tm - 1

    def body(x_ref, dx2_ref, pos_ref, invf_ref, sign_ref, nw_ref, sc_ref, sh_ref, w_ref, cw_ref,
             pre_ref, prehalo_ref, dc_ref, dchalo_ref, dz_ref, dba_ref,
             dq1_ref, dq4_ref, dq16_ref, dk1_ref, dk4_ref, dk16_ref, dv1_ref, dv4_ref, dv16_ref, dzat_ref,
             gx_ref, dproj_ref, rows_ref, crow_ref, ext_ref, *stages):
        i = pl.program_id(0)

        @pl.when(i == 0)
        def _():
            rows_ref[...] = jnp.zeros_like(rows_ref)
            crow_ref[...] = jnp.zeros_like(crow_ref)

        dc = dc_ref[...]
        ext_ref[0:tm, :] = dc
        ext_ref[tm:tm + 8, :] = jnp.where(i < last, dchalo_ref[...], 0.0)
        w8 = cw_ref[...]
        d_pre = ext_ref[pl.ds(3, tm), :] * w8[0:1, :]
        for j in range(1, CONV_K):
            d_pre = d_pre + ext_ref[pl.ds(3 - j, tm), :] * w8[j:j + 1, :]
        ext_ref[0:8, :] = jnp.where(i > 0, prehalo_ref[...], 0.0)
        ext_ref[8:8 + tm, :] = pre_ref[...]
        for j in range(CONV_K):
            crow_ref[j:j + 1, :] += jnp.sum(dc * ext_ref[pl.ds(5 + j, tm), :], axis=0, keepdims=True)

        cos_t, sin_t = _rope_tables(pos_ref[...], invf_ref[...], sign_ref[...])
        dq = dq1_ref[...] + _view_to_value(stages[0], dq4_ref, 4) + _view_to_value(stages[1], dq16_ref, 16)
        dk = dk1_ref[...] + _view_to_value(stages[2], dk4_ref, 4) + _view_to_value(stages[3], dk16_ref, 16)
        dq = dq * cos_t + _rope_partner(dq * sin_t)
        dk = dk * cos_t + _rope_partner(dk * sin_t)
        dv = dv1_ref[...] + _view_to_value(stages[4], dv4_ref, 4) + _view_to_value(stages[5], dv16_ref, 16)
        d_proj = jnp.concatenate([d_pre, dz_ref[...], dba_ref[...], dq, dk, dv, dzat_ref[...]], axis=1).astype(bf16)
        dproj_ref[...] = d_proj
        dh = lax.dot_general(d_proj, w_ref[...], (NT, ((), ())), preferred_element_type=f32)
        _, vjp = jax.vjp(_hnorm, x_ref[...], nw_ref[...], sc_ref[...], sh_ref[...])
        dx, dnw, dsc, dsh = vjp(dh)
        gx_ref[...] = dx + dx2_ref[...]
        rows_ref[0:1, :] += dnw
        rows_ref[1:2, :] += dsc
        rows_ref[2:3, :] += dsh

    tile = lambda w: pl.BlockSpec((tm, w), lambda i: (i, 0))
    row = lambda w: pl.BlockSpec((1, w), lambda i: (0, 0))
    step8 = tm // 8
    return pl.pallas_call(
        body, name="pre_backward", grid=(seq // tm,),
        in_specs=[tile(D_MODEL), tile(D_MODEL), tile(1), row(128), row(128), row(D_MODEL), row(D_MODEL), row(D_MODEL),
                  pl.BlockSpec((D_MODEL, CAT_W), lambda i: (0, 0)), pl.BlockSpec((8, QKV_W), lambda i: (0, 0)),
                  tile(QKV_W), pl.BlockSpec((8, QKV_W), lambda i: (jnp.maximum(i * step8 - 1, 0), 0)),
                  tile(QKV_W), pl.BlockSpec((8, QKV_W), lambda i: (jnp.minimum((i + 1) * step8, seq // 8 - 1), 0)),
                  tile(DN_WIDTH), tile(BA_W)] + [_view_spec(AT_WIDTH, d) for d in PATTERN_DILATIONS] * 3 + [tile(AT_WIDTH)],
        out_specs=[tile(D_MODEL), tile(CAT_W), pl.BlockSpec((PRE_ROWS, D_MODEL), lambda i: (0, 0)),
                   pl.BlockSpec((8, QKV_W), lambda i: (0, 0))],
        out_shape=[jax.ShapeDtypeStruct((seq, D_MODEL), f32), jax.ShapeDtypeStruct((seq, CAT_W), bf16),
                   jax.ShapeDtypeStruct((PRE_ROWS, D_MODEL), f32), jax.ShapeDtypeStruct((8, QKV_W), f32)],
        scratch_shapes=[pltpu.VMEM((tm + 8, QKV_W), f32)] + [_stage(AT_WIDTH)] * 6,
        compiler_params=_params(("arbitrary",)),
    )(x, dx2, pos_col, invf_row, sign_row, norm_w, scale, shift, w_cat, conv_w8, pdn, pdn, d_conv, d_conv, d_zdn, d_ba,
      d_q[0], d_q[1], d_q[2], d_k[0], d_k[1], d_k[2], d_v[0], d_v[1], d_v[2], d_zat)


def _weight_grad(h, d_proj):
    seq = h.shape[0]
    tk, tn = 512, CAT_W // 3
    n_k = seq // tk

    def body(h_ref, d_ref, o_ref):
        @pl.when(pl.program_id(1) == 0)
        def _():
            o_ref[...] = jnp.zeros_like(o_ref)

        o_ref[...] += lax.dot_general(h_ref[...], d_ref[...], (TN, ((), ())), preferred_element_type=f32)

    return pl.pallas_call(
        body, name="weight_grad", grid=(CAT_W // tn, n_k),
        in_specs=[pl.BlockSpec((tk, D_MODEL), lambda n, k: (k, 0)), pl.BlockSpec((tk, tn), lambda n, k: (k, n))],
        out_specs=pl.BlockSpec((D_MODEL, tn), lambda n, k: (0, n)),
        out_shape=jax.ShapeDtypeStruct((D_MODEL, CAT_W), f32),
        compiler_params=_params(("parallel", "arbitrary")),
    )(h, d_proj)


def _adamw(w, g, m, v):
    m = ADAM_B1 * m + (1.0 - ADAM_B1) * g
    v = ADAM_B2 * v + (1.0 - ADAM_B2) * (g * g)
    m_hat = m / (1.0 - ADAM_B1 ** ADAM_STEP)
    v_hat = v / (1.0 - ADAM_B2 ** ADAM_STEP)
    delta = -ADAM_LR * (m_hat / (jnp.sqrt(v_hat) + ADAM_EPS) + ADAM_WD * w)
    return delta, m, v


def _mod_shard(c_all, w_mod_s, b_mod_s):
    def body(c_ref, w_ref, b_ref, o_ref):
        o_ref[...] = _dot(_silu(c_ref[...]), w_ref[...], NN, True) + b_ref[...]

    return pl.pallas_call(body, name="mod_shard", out_shape=jax.ShapeDtypeStruct((N_DEV, w_mod_s.shape[1]), f32),
                          compiler_params=_params())(c_all, w_mod_s, b_mod_s)


def _mod_update(c_all, d_mod_s, w, m, v):
    def body(c_ref, d_ref, w_ref, m_ref, v_ref, g_ref, dw_ref, nm_ref, nv_ref):
        g = _dot(_silu(c_ref[...]), d_ref[...], TN, True)
        g_ref[...] = g
        dw_ref[...], nm_ref[...], nv_ref[...] = _adamw(w_ref[...], g, m_ref[...], v_ref[...])

    return pl.pallas_call(body, name="mod_update", out_shape=[jax.ShapeDtypeStruct(w.shape, f32)] * 4,
                          compiler_params=_params())(c_all, d_mod_s, w, m, v)


def _adamw_rows(w, g, m, v, name):
    rows, width = w.shape
    tr = 128 if rows % 128 == 0 else rows

    def body(w_ref, g_ref, m_ref, v_ref, dw_ref, nm_ref, nv_ref):
        dw_ref[...], nm_ref[...], nv_ref[...] = _adamw(w_ref[...], g_ref[...], m_ref[...], v_ref[...])

    spec = pl.BlockSpec((tr, width), lambda i: (i, 0))
    return pl.pallas_call(body, name=name, grid=(rows // tr,), in_specs=[spec] * 4, out_specs=[spec] * 3,
                          out_shape=[jax.ShapeDtypeStruct(w.shape, f32)] * 3,
                          compiler_params=_params(("parallel",)))(w, g, m, v)


SM_NORM, SM_FIN, SM_DN, SM_AT, SM_A, SM_DT, SM_LOSS, SM_MOD = 0, 1024, 2048, 2560, 3072, 3200, 3328, 3456
SM_W = SM_MOD + 3 * D_MODEL
RS_NORM, RS_FIN, RS_DN, RS_AT, RS_A, RS_DT, RS_BMOD = 0, 1024, 2048, 2176, 2304, 2432, 2560
RS_W = RS_BMOD + 3 * D_MODEL


def _small_update(gathered, w, m, v):
    def body(g_ref, w_ref, m_ref, v_ref, grad_ref, dw_ref, nm_ref, nv_ref, loss_ref, dmod_ref):
        total = g_ref[0:1, :]
        for dev in range(1, N_DEV):
            total = total + g_ref[8 * dev:8 * dev + 1, :]
        for dev in range(N_DEV):
            dmod_ref[dev:dev + 1, :] = g_ref[8 * dev:8 * dev + 1, SM_MOD:SM_W]
        dn = total[:, SM_DN:SM_DN + DN_DIM]
        for h in range(1, DN_HEADS):
            dn = dn + total[:, SM_DN + h * DN_DIM:SM_DN + (h + 1) * DN_DIM]
        at = total[:, SM_AT:SM_AT + AT_DIM]
        for h in range(1, AT_HEADS):
            at = at + total[:, SM_AT + h * AT_DIM:SM_AT + (h + 1) * AT_DIM]
        grad_ref[:, RS_NORM:RS_FIN] = total[:, SM_NORM:SM_FIN]
        grad_ref[:, RS_FIN:RS_DN] = total[:, SM_FIN:SM_DN]
        grad_ref[:, RS_DN:RS_AT] = dn
        grad_ref[:, RS_AT:RS_A] = jnp.zeros((1, 128), f32)
        grad_ref[:, RS_AT:RS_AT + AT_DIM] = at
        grad_ref[:, RS_A:RS_DT] = total[:, SM_A:SM_DT]
        grad_ref[:, RS_DT:RS_BMOD] = total[:, SM_DT:SM_LOSS]
        grad_ref[:, RS_BMOD:RS_W] = total[:, SM_MOD:SM_W]
        loss_ref[...] = total[:, SM_LOSS:SM_MOD]
        dw_ref[...], nm_ref[...], nv_ref[...] = _adamw(w_ref[...], grad_ref[...], m_ref[...], v_ref[...])

    row = jax.ShapeDtypeStruct((1, RS_W), f32)
    return pl.pallas_call(
        body, name="small_update",
        out_shape=[row, row, row, row, jax.ShapeDtypeStruct((1, 128), f32), jax.ShapeDtypeStruct((N_DEV, 3 * D_MODEL), f32)],
        compiler_params=_params())(gathered, w, m, v)


def _all_gather_rows(block, name):
    m_per, n = block.shape

    def body(x_ref, out_ref, send_sems, recv_sems, local_sem):
        x, y, c = lax.axis_index("x"), lax.axis_index("y"), lax.axis_index("c")
        me, sibling = (x, y, c), (x, y, 1 - c)
        chips = [(1 - x, y), (x, 1 - y), (1 - x, 1 - y)]

        def rows(px, py, pc):
            return out_ref.at[pl.ds((4 * px + 2 * py + pc) * m_per, m_per), :]

        def copy(k, blk, to, src=None):
            return pltpu.make_async_remote_copy(
                src_ref=rows(*blk) if src is None else src, dst_ref=rows(*blk),
                send_sem=send_sems.at[k], recv_sem=recv_sems.at[k], device_id=to, device_id_type=MESH)

        mine = pltpu.make_async_copy(x_ref, rows(*me), local_sem)
        mine.start()
        first = [copy(0, me, sibling, src=x_ref)]
        first += [copy(1 + j, me, (*chip, c), src=x_ref) for j, chip in enumerate(chips)]
        for cp in first:
            cp.start()
        passed = [copy(4 + j, (*chip, c), sibling) for j, chip in enumerate(chips)]
        for j, chip in enumerate(chips):
            copy(1 + j, (*chip, c), me).wait_recv()
            passed[j].start()
        copy(0, sibling, me).wait_recv()
        for j, chip in enumerate(chips):
            copy(4 + j, (*chip, 1 - c), me).wait_recv()
        for cp in first + passed:
            cp.wait_send()
        mine.wait()

    return pl.pallas_call(
        body, name=name,
        out_shape=jax.ShapeDtypeStruct((N_DEV * m_per, n), block.dtype),
        in_specs=[pl.BlockSpec(memory_space=pltpu.VMEM)],
        out_specs=pl.BlockSpec(memory_space=pltpu.VMEM),
        scratch_shapes=[pltpu.SemaphoreType.DMA((7,)), pltpu.SemaphoreType.DMA((7,)), pltpu.SemaphoreType.DMA],
        compiler_params=pltpu.CompilerParams(vmem_limit_bytes=VMEM_LIMIT),
    )(block)


def _gather_weight_shards(packed):
    rows, width = packed.shape
    half = rows // 2

    def body(src_ref, out_ref, send_sems, recv_sems, local_sem):
        x, y, c = lax.axis_index("x"), lax.axis_index("y"), lax.axis_index("c")
        chips = [(1 - x, y), (x, 1 - y), (1 - x, 1 - y)]
        my_rows = pl.ds(pl.multiple_of(c * half, 16), half)
        sibling_rows = pl.ds(pl.multiple_of((1 - c) * half, 16), half)

        def part(chip, part_rows):
            return out_ref.at[2 * chip[0] + chip[1], part_rows]

        def copy(k, src, dst, to):
            return pltpu.make_async_remote_copy(src_ref=src, dst_ref=dst, send_sem=send_sems.at[k], recv_sem=recv_sems.at[k],
                                                device_id=to, device_id_type=MESH)

        mine = pltpu.make_async_copy(src_ref, out_ref.at[2 * x + y], local_sem)
        mine.start()
        sends = [copy(k, src_ref.at[my_rows], part((x, y), my_rows), (*chip, c)) for k, chip in enumerate(chips)]
        for cp in sends:
            cp.start()
        passed = []
        for k, chip in enumerate(chips):
            landed = part(chip, my_rows)
            copy(k, landed, landed, (*chip, c)).wait_recv()
            passed.append(copy(3 + k, landed, landed, (x, y, 1 - c)))
            passed[k].start()
        for k, chip in enumerate(chips):
            from_sibling = part(chip, sibling_rows)
            copy(3 + k, from_sibling, from_sibling, (x, y, 1 - c)).wait_recv()
        for cp in sends + passed:
            cp.wait_send()
        mine.wait()

    return pl.pallas_call(
        body, name="gather_weight_shards",
        out_shape=jax.ShapeDtypeStruct((N_CHIPS, rows, width), packed.dtype),
        in_specs=[pl.BlockSpec(memory_space=pl.ANY)],
        out_specs=pl.BlockSpec(memory_space=pl.ANY),
        scratch_shapes=[pltpu.SemaphoreType.DMA((6,)), pltpu.SemaphoreType.DMA((6,)), pltpu.SemaphoreType.DMA],
    )(packed)


def _reduce_weight_grads(grads):
    _, rows, width = grads.shape
    half = rows // 2
    n_chunks = half // PACK_CHUNK

    def body(g_ref, out_ref, mine_ref, sibling_ref, send_ref, land_ref, send_sems, recv_sems, local_sem):
        x, y, c = lax.axis_index("x"), lax.axis_index("y"), lax.axis_index("c")
        chips = [(1 - x, y), (x, 1 - y), (1 - x, 1 - y)]
        sibling = (x, y, 1 - c)
        my_rows = pl.ds(pl.multiple_of(c * half, 16), half)
        sibling_rows = pl.ds(pl.multiple_of((1 - c) * half, 16), half)
        all_slots = pl.ds(0, N_CHIPS)

        def copy(k, src, dst, to):
            return pltpu.make_async_remote_copy(src_ref=src, dst_ref=dst, send_sem=send_sems.at[k], recv_sem=recv_sems.at[k],
                                                device_id=to, device_id_type=MESH)

        mine = pltpu.make_async_copy(g_ref.at[all_slots, my_rows], mine_ref, local_sem)
        mine.start()
        swap = copy(0, g_ref.at[all_slots, sibling_rows], sibling_ref, sibling)
        swap.start()
        mine.wait()
        swap.wait_recv()

        def pair_sums(i, carry):
            r = pl.ds(pl.multiple_of(i * PACK_CHUNK, 16), PACK_CHUNK)
            for k, chip in enumerate(chips):
                slot = 2 * chip[0] + chip[1]
                send_ref[k, r, :] = (mine_ref[slot, r, :] + sibling_ref[slot, r, :]).astype(bf16)
            out_rows = pl.ds(pl.multiple_of(c * half + i * PACK_CHUNK, 16), PACK_CHUNK)
            out_ref[out_rows, :] = mine_ref[2 * x + y, r, :] + sibling_ref[2 * x + y, r, :]
            return carry

        lax.fori_loop(0, n_chunks, pair_sums, 0)
        sends = [copy(1 + k, send_ref.at[k], land_ref.at[k], (*chip, c)) for k, chip in enumerate(chips)]
        for cp in sends:
            cp.start()
        for cp in sends:
            cp.wait_recv()

        def add_landed(i, carry):
            r = pl.ds(pl.multiple_of(i * PACK_CHUNK, 16), PACK_CHUNK)
            out_rows = pl.ds(pl.multiple_of(c * half + i * PACK_CHUNK, 16), PACK_CHUNK)
            landed = [land_ref[k, r, :].astype(f32) for k in range(3)]
            out_ref[out_rows, :] = ((out_ref[out_rows, :] + landed[0]) + landed[1]) + landed[2]
            return carry

        lax.fori_loop(0, n_chunks, add_landed, 0)
        finished = copy(4, out_ref.at[my_rows], out_ref.at[my_rows], sibling)
        finished.start()
        copy(4, out_ref.at[sibling_rows], out_ref.at[sibling_rows], sibling).wait_recv()
        for cp in [swap, finished] + sends:
            cp.wait_send()

    return pl.pallas_call(
        body, name="reduce_weight_grads",
        out_shape=jax.ShapeDtypeStruct((rows, width), f32),
        in_specs=[pl.BlockSpec(memory_space=pl.ANY)],
        out_specs=pl.BlockSpec(memory_space=pltpu.VMEM),
        scratch_shapes=[pltpu.VMEM((N_CHIPS, half, width), f32), pltpu.VMEM((N_CHIPS, half, width), f32),
                        pltpu.VMEM((3, half, width), bf16), pltpu.VMEM((3, half, width), bf16),
                        pltpu.SemaphoreType.DMA((5,)), pltpu.SemaphoreType.DMA((5,)), pltpu.SemaphoreType.DMA],
        compiler_params=pltpu.CompilerParams(vmem_limit_bytes=VMEM_LIMIT),
    )(grads)


CONV_SHARD = QKV_W // N_CHIPS


def _pack_shards(w_in_s, w_out_s, conv_rows):
    lead = w_in_s.shape[:-2]

    def widen(a, rows):
        return jnp.pad(a, [(0, 0)] * len(lead) + [(0, rows - a.shape[-2]), (0, PACK_W - a.shape[-1])])

    used = D_MODEL + OUT_ROWS + 8
    return jnp.concatenate([widen(w_in_s, D_MODEL), widen(w_out_s, OUT_ROWS), widen(conv_rows, 8),
                            jnp.zeros((*lead, PACK_ROWS - used, PACK_W), w_in_s.dtype)], axis=-2)


def _unpack_shards(packed):
    return (packed[..., :D_MODEL, :SHARD_IN], packed[..., D_MODEL:D_MODEL + OUT_ROWS, :D_MODEL],
            packed[..., D_MODEL + OUT_ROWS:D_MODEL + OUT_ROWS + CONV_K, :])


def _pack_weights_bf16(w_in_s, w_out_s, conv_s):
    conv_bits = lax.bitcast_convert_type(conv_s, bf16).reshape(CONV_K, 2 * CONV_SHARD)
    return _pack_shards(w_in_s.astype(bf16), w_out_s.astype(bf16), conv_bits)


def _unpack_weights_bf16(shards):
    w_in_s, w_out_s, conv_bits = _unpack_shards(shards)
    conv_s = lax.bitcast_convert_type(conv_bits[..., :2 * CONV_SHARD].reshape(N_CHIPS, CONV_K, CONV_SHARD, 2), f32)
    return (jnp.transpose(w_in_s, (1, 0, 2)).reshape(D_MODEL, IN_COLS), w_out_s.reshape(D_MODEL, D_MODEL),
            jnp.transpose(conv_s, (1, 0, 2)).reshape(CONV_K, QKV_W))


def _local_step(x, target, pos_col, mod_row, norm_w, w_in, conv_w, a_log, dt_bias, dn_norm_w, at_norm_w, w_out, fin_w):
    shift, scale, gate = mod_row[:, :D_MODEL], mod_row[:, D_MODEL:2 * D_MODEL], mod_row[:, 2 * D_MODEL:]
    half = AT_DIM // 2
    lane = jnp.arange(128)
    inv_freq = ROPE_THETA ** (-jnp.arange(half, dtype=f32) / half)
    invf_row = inv_freq[lane % half].reshape(1, 128)
    sign_row = jnp.where((lane % AT_DIM) < half, -1.0, 1.0).astype(f32).reshape(1, 128)
    ba_w = jnp.pad(w_in[:, 2 * D_MODEL:2 * D_MODEL + 2 * DN_HEADS], ((0, 0), (0, BA_W - 2 * DN_HEADS)))
    w_cat = jnp.concatenate([w_in[:, :2 * D_MODEL], ba_w, w_in[:, 2 * D_MODEL + 2 * DN_HEADS:]], axis=1).astype(bf16)
    conv_w8 = jnp.pad(conv_w, ((0, 8 - CONV_K), (0, 0)))
    a_row = jnp.pad(a_log.reshape(1, DN_HEADS), ((0, 0), (DN_HEADS, BA_W - 2 * DN_HEADS)))
    dt_row = jnp.pad(dt_bias.reshape(1, DN_HEADS), ((0, 0), (DN_HEADS, BA_W - 2 * DN_HEADS)))
    dn_w_row = jnp.tile(dn_norm_w.reshape(1, DN_DIM), (1, DN_HEADS))
    at_w_row = jnp.tile(at_norm_w.reshape(1, AT_DIM), (1, AT_HEADS))
    norm_row = norm_w.reshape(1, D_MODEL)
    fin_row = fin_w.reshape(1, D_MODEL)
    w_out_bf = w_out.astype(bf16)

    pdn, *views, z_at, h = _pre_proj(x, pos_col, invf_row, sign_row, norm_row, scale, shift, w_cat)
    *dn_parts, dn_invs = _dn_intra_forward(pdn, conv_w8, a_row, dt_row)
    o_dn, states = _dn_scan_forward(*dn_parts)
    outs, lses = [], []
    for d, view in zip(PATTERN_DILATIONS, views):
        o, lse = _attn_forward(view, d)
        outs.append(o)
        lses.append(lse)
    (dx2, d_odn, d_zdn, *in_views, d_zat, g_wtop, g_wbot, tail_rows) = _tail(
        x, target, o_dn, pdn, outs, lses, z_at, gate, w_out_bf, dn_w_row, at_w_row, fin_row)
    d_oat, delta, lse_all = in_views[0:3], in_views[3:6], in_views[6:9]
    d_q, d_k, d_v = [], [], []
    for n, (d, view) in enumerate(zip(PATTERN_DILATIONS, views)):
        args = (d_oat[n], delta[n], lse_all[n])
        d_q.append(_attn_backward_q(view, *args, d))
        dk, dv = _attn_backward_kv(view, *args, d)
        d_k.append(dk)
        d_v.append(dv)
    d_parts = _dn_scan_backward(*dn_parts, states, d_odn)
    d_conv, d_ba, dn_rows = _dn_intra_backward(pdn, conv_w8, a_row, dt_row, dn_invs, *d_parts)
    grad_x, d_proj, pre_rows, conv_rows = _pre_backward(
        x, dx2, pos_col, invf_row, sign_row, norm_row, scale, shift, w_cat, conv_w8, pdn, d_conv, d_zdn, d_ba,
        d_q, d_k, d_v, d_zat)
    g_cat = _weight_grad(h, d_proj)
    g_w_in = jnp.concatenate([g_cat[:, :2 * D_MODEL], g_cat[:, 2 * D_MODEL:2 * D_MODEL + 2 * DN_HEADS],
                              g_cat[:, 2 * D_MODEL + BA_W:]], axis=1)
    g_w_out = jnp.concatenate([g_wtop, g_wbot], axis=0)
    g_conv = conv_rows[:CONV_K]
    small = jnp.concatenate([
        pre_rows[0:1], tail_rows[2:3], tail_rows[3:4, :DN_WIDTH], tail_rows[4:5, :AT_WIDTH],
        dn_rows[0:1], dn_rows[1:2], tail_rows[0:1, :128],
        pre_rows[2:3], pre_rows[1:2], tail_rows[1:2]], axis=1)
    return grad_x, g_w_in, g_w_out, g_conv, small


def kernel(x, c, positions, w_mod, b_mod, norm_w, w_in, conv_w, a_log, dt_bias, dn_norm_w, at_norm_w, w_out, final_norm_w, loss_target, m_w_mod, m_b_mod, m_norm_w, m_w_in, m_conv_w, m_a_log, m_dt_bias, m_dn_norm_w, m_at_norm_w, m_w_out, m_final_norm_w, v_w_mod, v_b_mod, v_norm_w, v_w_in, v_conv_w, v_a_log, v_dt_bias, v_dn_norm_w, v_at_norm_w, v_w_out, v_final_norm_w):
    seq = x.shape[1]
    ax, ay, ac = lax.axis_index("x"), lax.axis_index("y"), lax.axis_index("c")
    me = 4 * ax + 2 * ay + ac
    chip = 2 * ax + ay

    c_all = _all_gather_rows(jnp.pad(c, ((0, 7), (0, 0))), "gather_c").reshape(N_DEV, 8, D_MODEL)[:, 0]
    b_mod_s = lax.dynamic_slice_in_dim(b_mod, chip * 768, 768, axis=1)
    mod_part = _mod_shard(c_all, w_mod[0], b_mod_s)
    mod_all = _all_gather_rows(mod_part, "gather_mod").reshape(N_CHIPS, 2, N_DEV, 768)[:, 0]
    mod_row = lax.dynamic_index_in_dim(mod_all, me, axis=1, keepdims=False).reshape(1, 3 * D_MODEL)

    shards = _gather_weight_shards(_pack_weights_bf16(w_in[0], w_out[0], conv_w[0]))
    w_in_full, w_out_full, conv_full = _unpack_weights_bf16(shards)

    grad_x, g_w_in, g_w_out, g_conv, small = _local_step(
        x[0], loss_target[0], positions.reshape(seq, 1), mod_row, norm_w, w_in_full, conv_full, a_log, dt_bias,
        dn_norm_w, at_norm_w, w_out_full, final_norm_w)

    g_pack = _pack_shards(jnp.transpose(g_w_in.reshape(D_MODEL, N_CHIPS, SHARD_IN), (1, 0, 2)),
                          g_w_out.reshape(N_CHIPS, OUT_ROWS, D_MODEL),
                          jnp.transpose(g_conv.reshape(CONV_K, N_CHIPS, CONV_SHARD), (1, 0, 2)))
    grad_w_in, grad_w_out, grad_conv_rows = _unpack_shards(_reduce_weight_grads(g_pack))
    grad_conv_w = grad_conv_rows[:, :CONV_SHARD]
    delta_w_in, new_m_w_in, new_v_w_in = _adamw_rows(w_in[0], grad_w_in, m_w_in[0], v_w_in[0], "adamw_w_in")
    delta_w_out, new_m_w_out, new_v_w_out = _adamw_rows(w_out[0], grad_w_out, m_w_out[0], v_w_out[0], "adamw_w_out")
    delta_conv_w, new_m_conv_w, new_v_conv_w = _adamw_rows(conv_w[0], grad_conv_w, m_conv_w[0], v_conv_w[0], "adamw_conv_w")

    gathered = _all_gather_rows(jnp.pad(small, ((0, 7), (0, 0))), "gather_small")

    def small_row(norm, fin, dn, at, a, dt, bmod):
        z = lambda n: jnp.zeros((1, n), f32)
        return jnp.concatenate([norm.reshape(1, -1), fin.reshape(1, -1), dn.reshape(1, -1), at.reshape(1, -1), z(64),
                                z(4), a.reshape(1, -1), z(120), z(4), dt.reshape(1, -1), z(120), bmod.reshape(1, -1)], axis=1)

    g_small, d_small, m_small, v_small, loss_row, d_mod_all = _small_update(
        gathered,
        small_row(norm_w, final_norm_w, dn_norm_w, at_norm_w, a_log, dt_bias, b_mod),
        small_row(m_norm_w, m_final_norm_w, m_dn_norm_w, m_at_norm_w, m_a_log, m_dt_bias, m_b_mod),
        small_row(v_norm_w, v_final_norm_w, v_dn_norm_w, v_at_norm_w, v_a_log, v_dt_bias, v_b_mod))

    def split_small(r):
        return (r[:, RS_BMOD:RS_W], r[:, RS_NORM:RS_FIN], r[:, RS_A + DN_HEADS:RS_A + 2 * DN_HEADS],
                r[:, RS_DT + DN_HEADS:RS_DT + 2 * DN_HEADS], r[:, RS_DN:RS_AT], r[:, RS_AT:RS_AT + AT_DIM],
                r[0, RS_FIN:RS_DN])

    d_mod_s = lax.dynamic_slice_in_dim(d_mod_all, chip * 768, 768, axis=1)
    pad_rows = lambda a: jnp.pad(a, ((0, 128 - N_DEV), (0, 0)))
    grad_w_mod, delta_w_mod, new_m_w_mod, new_v_w_mod = _mod_update(
        pad_rows(c_all), pad_rows(d_mod_s), w_mod[0], m_w_mod[0], v_w_mod[0])

    def ordered(w_mod_leaf, small_row_leaf, w_in_leaf, conv_leaf, w_out_leaf):
        b, n, a, dt, dn, at, fin = split_small(small_row_leaf)
        return [w_mod_leaf[None], b, n, w_in_leaf[None], conv_leaf[None], a, dt, dn, at, w_out_leaf[None], fin]

    loss = loss_row[0, 0]
    return (loss, grad_x[None],
            *ordered(grad_w_mod, g_small, grad_w_in, grad_conv_w, grad_w_out),
            *ordered(delta_w_mod, d_small, delta_w_in, delta_conv_w, delta_w_out),
            *ordered(new_m_w_mod, m_small, new_m_w_in, new_m_conv_w, new_m_w_out),
            *ordered(new_v_w_mod, v_small, new_v_w_in, new_v_conv_w, new_v_w_out))
```

```python
import functools

import jax
import jax.numpy as jnp
from jax import lax
from jax.experimental import pallas as pl
from jax.experimental.pallas import tpu as pltpu

f32 = jnp.float32
bf16 = jnp.bfloat16
HIGHEST = lax.Precision.HIGHEST
MESH = pl.DeviceIdType.MESH

D_MODEL = 1024
DN_HEADS = 4
DN_DIM = 128
DN_WIDTH = 512
AT_HEADS = 8
AT_DIM = 64
AT_WIDTH = 512
CHUNK = 64
Q_BLOCK = 128
CONV_K = 4
EPS = 1e-6
ROPE_THETA = 10000.0
PATTERN_DILATIONS = (1, 4, 16)
NEG = -1e30

QKV_W = 3 * DN_WIDTH
BA_W = 128
PDN_W = QKV_W + DN_WIDTH + BA_W
PAT_W = 3 * AT_WIDTH
CAT_W = PDN_W + PAT_W + AT_WIDTH
IN_COLS = 4104
N_CHIPS = 4
N_DEV = 8
SHARD_IN = IN_COLS // N_CHIPS
OUT_ROWS = D_MODEL // N_CHIPS
PACK_W = 1152
PACK_ROWS = 1312
PACK_CHUNK = 16

ADAM_LR = 0.001
ADAM_B1 = 0.9
ADAM_B2 = 0.999
ADAM_EPS = 1e-08
ADAM_WD = 0.01
ADAM_STEP = 10

VMEM_LIMIT = 56 * 1024 * 1024

NN = ((1,), (0,))
NT = ((1,), (1,))
TN = ((0,), (0,))


def _pieces(a, n):
    out = []
    for _ in range(n - 1):
        p = a.astype(bf16)
        out.append(p)
        a = a - p.astype(f32)
    out.append(a.astype(bf16))
    return out


def _dot(a, b, dims, exact):
    raw = lambda p, q: lax.dot_general(p, q, (dims, ((), ())), preferred_element_type=f32)
    if exact == "split":
        (ah, al), (bh, bl) = _pieces(a, 2), _pieces(b, 2)
        return raw(ah, bh) + (raw(ah, bl) + raw(al, bh))
    if exact:
        return lax.dot_general(a, b, (dims, ((), ())), precision=HIGHEST, preferred_element_type=f32)
    return raw(a.astype(bf16), b.astype(bf16))


def _dot_sel(sel, b, dims, sel_side):
    raw = lambda p, q: lax.dot_general(p, q, (dims, ((), ())), preferred_element_type=f32)
    sel = sel.astype(bf16)
    parts = [raw(sel, p) if sel_side == 0 else raw(p, sel) for p in _pieces(b, 3)]
    return (parts[0] + parts[1]) + parts[2]


@jax.custom_vjp
def _sel_left(sel, b):
    return _dot_sel(sel, b, NN, 0)


def _sel_left_fwd(sel, b):
    return _dot_sel(sel, b, NN, 0), sel


def _sel_left_bwd(sel, g):
    return jnp.zeros_like(sel), _dot_sel(sel, g, TN, 0)


_sel_left.defvjp(_sel_left_fwd, _sel_left_bwd)


@jax.custom_vjp
def _sel_right(a, sel):
    return _dot_sel(sel, a, NN, 1)


def _sel_right_fwd(a, sel):
    return _dot_sel(sel, a, NN, 1), sel


def _sel_right_bwd(sel, g):
    return _dot_sel(sel, g, NT, 1), jnp.zeros_like(sel)


_sel_right.defvjp(_sel_right_fwd, _sel_right_bwd)


class _Matmuls:
    def __init__(self, exact, back):
        @jax.custom_vjp
        def nn(a, b):
            return _dot(a, b, NN, exact)

        def nn_fwd(a, b):
            return _dot(a, b, NN, exact), (a, b)

        def nn_bwd(res, g):
            a, b = res
            return _dot(g, b, NT, back), _dot(a, g, TN, back)

        nn.defvjp(nn_fwd, nn_bwd)

        @jax.custom_vjp
        def nt(a, b):
            return _dot(a, b, NT, exact)

        def nt_fwd(a, b):
            return _dot(a, b, NT, exact), (a, b)

        def nt_bwd(res, g):
            a, b = res
            return _dot(g, b, NN, back), _dot(g, a, TN, back)

        nt.defvjp(nt_fwd, nt_bwd)

        @jax.custom_vjp
        def tn(a, b):
            return _dot(a, b, TN, exact)

        def tn_fwd(a, b):
            return _dot(a, b, TN, exact), (a, b)

        def tn_bwd(res, g):
            a, b = res
            return _dot(b, g, NT, back), _dot(a, g, NN, back)

        tn.defvjp(tn_fwd, tn_bwd)
        self.nn, self.nt, self.tn = nn, nt, tn


MM = _Matmuls(exact=False, back=False)
MS = _Matmuls(exact="split", back=False)


def _each(fn, *lists):
    return [fn(*args) for args in zip(*lists)]


def _inverse_products(a_lows):
    ri = lax.broadcasted_iota(jnp.int32, (CHUNK, CHUNK), 0)
    ci = lax.broadcasted_iota(jnp.int32, (CHUNK, CHUNK), 1)
    eye = (ri == ci).astype(f32)
    power = _each(lambda a: -a, a_lows)
    inv = _each(lambda p: eye + p, power)
    for _ in range(5):
        power = _each(lambda p: _dot(p, p, NN, "split"), power)
        inv = _each(lambda x, p: x + _dot(x, p, NN, "split"), inv, power)
    return inv


def _inverse_cotangents(invs, gs):
    left = _each(lambda t, g: _dot(t, g, TN, "split"), invs, gs)
    return _each(lambda l, t: -_dot(l, t, NT, "split"), left, invs)


@jax.custom_vjp
def _unit_lower_inverses(a_lows):
    return _inverse_products(a_lows)


def _unit_lower_inverses_fwd(a_lows):
    invs = _inverse_products(a_lows)
    return invs, invs


def _unit_lower_inverses_bwd(invs, gs):
    return (_inverse_cotangents(invs, gs),)


_unit_lower_inverses.defvjp(_unit_lower_inverses_fwd, _unit_lower_inverses_bwd)


@jax.custom_vjp
def _known_inverses(a_lows, invs):
    return invs


def _known_inverses_fwd(a_lows, invs):
    return invs, invs


def _known_inverses_bwd(invs, gs):
    return _inverse_cotangents(invs, gs), _each(jnp.zeros_like, invs)


_known_inverses.defvjp(_known_inverses_fwd, _known_inverses_bwd)


def _params(semantics=None):
    return pltpu.CompilerParams(dimension_semantics=semantics, vmem_limit_bytes=VMEM_LIMIT)


def _silu(x):
    return x * jax.nn.sigmoid(x)


def _group_ones(width, group):
    r = lax.broadcasted_iota(jnp.int32, (width, width), 0) // group
    c = lax.broadcasted_iota(jnp.int32, (width, width), 1) // group
    return (r == c).astype(f32)


def _slab_sums(x):
    rows, width = x.shape
    sums = [jnp.sum(x[:, j:j + 128], axis=1, keepdims=True) for j in range(0, width, 128)]
    return jnp.concatenate([jnp.broadcast_to(s, (rows, 128)) for s in sums], axis=1)


def _hnorm(x, nw, scale, shift):
    xn = x * lax.rsqrt(jnp.mean(x * x, axis=-1, keepdims=True) + EPS)
    return xn * nw * (1.0 + scale) + shift


def _rope_tables(pos_col, invf_row, sign_row):
    ang = pos_col.astype(f32) * invf_row
    cos_t = jnp.cos(ang)
    sin_t = jnp.sin(ang) * sign_row
    return jnp.concatenate([cos_t] * 4, axis=1), jnp.concatenate([sin_t] * 4, axis=1)


def _rope_partner(x):
    lane = lax.broadcasted_iota(jnp.int32, x.shape, 1)
    width = x.shape[1]
    return jnp.where((lane % AT_DIM) < AT_DIM // 2, pltpu.roll(x, width - AT_DIM // 2, 1), pltpu.roll(x, AT_DIM // 2, 1))


TOKEN_TILE = 256
LANES = 128


def _stage_lanes(stage_ref, first, value):
    for j in range(value.shape[1] // LANES):
        stage_ref[first + j] = value[:, LANES * j:LANES * (j + 1)]


def _stage_to_view(stage_ref, view_ref, dil):
    chunks, rows, _ = stage_ref.shape
    for r in range(dil):
        for j in range(chunks):
            col = (r * chunks + j) * LANES
            view_ref[:, col:col + LANES] = stage_ref.at[j][pl.ds(r, rows // dil, stride=dil), :].astype(view_ref.dtype)


def _view_to_value(stage_ref, view_ref, dil):
    chunks, rows, _ = stage_ref.shape
    for r in range(dil):
        for j in range(chunks):
            col = (r * chunks + j) * LANES
            stage_ref.at[j][pl.ds(r, rows // dil, stride=dil), :] = view_ref[:, col:col + LANES].astype(f32)
    return jnp.concatenate([stage_ref[j] for j in range(chunks)], axis=1)


def _view_spec(width, dil):
    return pl.BlockSpec((TOKEN_TILE // dil, dil * width), lambda i: (i, 0))


def _view_shape(seq, width, dil, dtype=f32):
    return jax.ShapeDtypeStruct((seq // dil, dil * width), dtype)


HEAD_LANES = 128


def _repeat_heads(compact):
    row = lax.broadcasted_iota(jnp.int32, (HEAD_LANES, AT_WIDTH), 0)
    col = lax.broadcasted_iota(jnp.int32, (HEAD_LANES, AT_WIDTH), 1)
    return _dot_sel((row == col // AT_DIM).astype(f32), compact, NN, 1)


def _stage(width):
    return pltpu.VMEM((width // LANES, TOKEN_TILE, LANES), f32)


def _pre_proj(x, pos_col, invf_row, sign_row, norm_w, scale, shift, w_cat):
    seq = x.shape[0]
    tm = TOKEN_TILE

    def body(x_ref, pos_ref, invf_ref, sign_ref, nw_ref, sc_ref, sh_ref, w_ref,
             pdn_ref, pat1_ref, pat4_ref, pat16_ref, zat_ref, h_ref, stage_ref):
        h = _hnorm(x_ref[...], nw_ref[...], sc_ref[...], sh_ref[...]).astype(bf16)
        h_ref[...] = h
        big = jnp.dot(h, w_ref[...], preferred_element_type=f32)
        pdn_ref[...] = big[:, :PDN_W]
        cos_t, sin_t = _rope_tables(pos_ref[...], invf_ref[...], sign_ref[...])
        q = big[:, PDN_W:PDN_W + AT_WIDTH]
        k = big[:, PDN_W + AT_WIDTH:PDN_W + 2 * AT_WIDTH]
        parts = (q * cos_t + _rope_partner(q) * sin_t, k * cos_t + _rope_partner(k) * sin_t,
                 big[:, PDN_W + 2 * AT_WIDTH:PDN_W + PAT_W])
        for p, part in enumerate(parts):
            pat1_ref[:, p * AT_WIDTH:(p + 1) * AT_WIDTH] = part.astype(bf16)
            _stage_lanes(stage_ref, p * (AT_WIDTH // LANES), part)
        _stage_to_view(stage_ref, pat4_ref, 4)
        _stage_to_view(stage_ref, pat16_ref, 16)
        zat_ref[...] = big[:, PDN_W + PAT_W:]

    row = lambda w: pl.BlockSpec((1, w), lambda i: (0, 0))
    tile = lambda w: pl.BlockSpec((tm, w), lambda i: (i, 0))
    return pl.pallas_call(
        body, name="pre_proj", grid=(seq // tm,),
        in_specs=[tile(D_MODEL), tile(1), row(128), row(128), row(D_MODEL), row(D_MODEL), row(D_MODEL),
                  pl.BlockSpec((D_MODEL, CAT_W), lambda i: (0, 0))],
        out_specs=[tile(PDN_W), tile(PAT_W), _view_spec(PAT_W, 4), _view_spec(PAT_W, 16), tile(AT_WIDTH), tile(D_MODEL)],
        out_shape=[jax.ShapeDtypeStruct((seq, PDN_W), f32), _view_shape(seq, PAT_W, 1, bf16), _view_shape(seq, PAT_W, 4, bf16),
                   _view_shape(seq, PAT_W, 16, bf16), jax.ShapeDtypeStruct((seq, AT_WIDTH), f32),
                   jax.ShapeDtypeStruct((seq, D_MODEL), bf16)],
        scratch_shapes=[_stage(PAT_W)],
        compiler_params=_params(("parallel",)),
    )(x, pos_col, invf_row, sign_row, norm_w, scale, shift, w_cat)


def _conv_taps(ext_ref, halo, cur, w8):
    rows = cur.shape[0]
    ext_ref[0:8, :] = halo
    ext_ref[8:8 + rows, :] = cur
    out = ext_ref[pl.ds(5, rows), :] * w8[0:1, :]
    for j in range(1, CONV_K):
        out = out + ext_ref[pl.ds(5 + j, rows), :] * w8[j:j + 1, :]
    return out


def _dn_pre(cq, ck, cv, ba, a_row, dt_row):
    sq, sk, v = _silu(cq), _silu(ck), _silu(cv)
    qn = sq * lax.rsqrt(_slab_sums(sq * sq) + EPS)
    kn = sk * lax.rsqrt(_slab_sums(sk * sk) + EPS)
    beta_all = jax.nn.sigmoid(ba)
    g_all = -jnp.exp(a_row) * jax.nn.softplus(ba + dt_row)
    return qn, kn, v, beta_all, g_all


def _dn_intra(qs, ks, vs, betas, gs, known_invs=None):
    ri = lax.broadcasted_iota(jnp.int32, (CHUNK, CHUNK), 0)
    ci = lax.broadcasted_iota(jnp.int32, (CHUNK, CHUNK), 1)
    tril = ri >= ci
    strict = ri > ci
    lower = tril.astype(f32)
    each = _each
    g_wide = each(lambda g: jnp.broadcast_to(g, (CHUNK, DN_DIM)), gs)
    gc = each(lambda g: _sel_left(lower, g), g_wide)
    gc_sq = each(lambda c: c[:, :CHUNK], gc)
    g_end = each(lambda c: jnp.broadcast_to(c[CHUNK - 1:CHUNK, :], (CHUNK, DN_DIM)), gc)
    g_end8 = each(lambda c: jnp.broadcast_to(c[CHUNK - 1:CHUNK, :], (8, DN_DIM)), gc)
    decay = each(lambda s: jnp.exp(jnp.where(tril, s - s.T, -jnp.inf)), gc_sq)
    qs = each(lambda q: q * (DN_DIM ** -0.5), qs)
    kb = each(lambda k, b: k * b, ks, betas)
    vb = each(lambda v, b: v * b, vs, betas)
    kk = each(MM.nt, kb, ks)
    qk = each(MM.nt, qs, ks)
    a_low = each(lambda p, d: jnp.where(strict, p * d, 0.0), kk, decay)
    inv = _unit_lower_inverses(a_low) if known_invs is None else _known_inverses(a_low, known_invs)
    e_gc = each(jnp.exp, gc)
    u = each(MS.nn, inv, vb)
    w = each(lambda x, k, e: MS.nn(x, k * e), inv, kb, e_gc)
    attn = each(lambda p, d: jnp.where(tril, p * d, 0.0), qk, decay)
    q_dec = each(lambda q, e: q * e, qs, e_gc)
    k_dec = each(lambda k, ge, c: k * jnp.exp(ge - c), ks, g_end, gc)
    return u, w, q_dec, k_dec, attn, each(jnp.exp, g_end8), inv


def _dn_step(us, ws, q_decs, k_decs, attns, e_ends, states):
    each = _each
    v_new = each(lambda u, w, s: u - MM.nn(w, s), us, ws, states)
    qs = each(MM.nn, q_decs, states)
    o = each(lambda a, b, c: a + MM.nn(b, c), qs, attns, v_new)
    new_states = each(lambda s, e, k, v: s * e + MM.tn(k, v), states, e_ends, k_decs, v_new)
    return o, new_states


INTRA_CHUNKS = 4
SCAN_CHUNKS = 8


def _intra_specs(nc):
    rows = INTRA_CHUNKS * CHUNK
    cur = pl.BlockSpec((rows, QKV_W), lambda i: (i, 0))
    halo = pl.BlockSpec((8, QKV_W), lambda i: (jnp.maximum(i * (rows // 8) - 1, 0), 0))
    ba = pl.BlockSpec((rows, BA_W), lambda i: (i, (QKV_W + DN_WIDTH) // BA_W))
    conv = pl.BlockSpec((8, QKV_W), lambda i: (0, 0))
    row = pl.BlockSpec((1, BA_W), lambda i: (0, 0))
    wide = pl.BlockSpec((rows, DN_WIDTH), lambda i: (i, 0))
    attn = pl.BlockSpec((INTRA_CHUNKS, DN_HEADS, CHUNK, CHUNK), lambda i: (i, 0, 0, 0))
    e_end = pl.BlockSpec((INTRA_CHUNKS, 8 * DN_HEADS, DN_DIM), lambda i: (i, 0, 0))
    return cur, halo, ba, conv, row, wide, attn, e_end


def _intra_items():
    return [(ci, h, slice(ci * CHUNK, (ci + 1) * CHUNK), slice(h * DN_DIM, (h + 1) * DN_DIM))
            for ci in range(INTRA_CHUNKS) for h in range(DN_HEADS)]


def _intra_inputs(items, qn, kn, v, beta_all, g_all):
    return ([qn[rows, lanes] for _, _, rows, lanes in items], [kn[rows, lanes] for _, _, rows, lanes in items],
            [v[rows, lanes] for _, _, rows, lanes in items], [beta_all[rows, h:h + 1] for _, h, rows, _ in items],
            [g_all[rows, DN_HEADS + h:DN_HEADS + h + 1] for _, h, rows, _ in items])


def _intra_shapes(seq, operand_dtype):
    nc = seq // CHUNK
    wide = jax.ShapeDtypeStruct((seq, DN_WIDTH), f32)
    operand = jax.ShapeDtypeStruct((seq, DN_WIDTH), operand_dtype)
    return [wide, operand, operand, operand, jax.ShapeDtypeStruct((nc, DN_HEADS, CHUNK, CHUNK), f32),
            jax.ShapeDtypeStruct((nc, 8 * DN_HEADS, DN_DIM), f32)]


def _dn_intra_forward(pdn, conv_w8, a_row, dt_row):
    seq = pdn.shape[0]
    nc = seq // CHUNK
    cur, halo, ba, conv, row, wide, attn, e_end = _intra_specs(nc)

    def body(cur_ref, halo_ref, ba_ref, w_ref, a_ref, dt_ref, u_ref, w_out_ref, qd_ref, kd_ref, attn_ref, e_ref, inv_ref,
             ext_ref):
        halo_rows = jnp.where(pl.program_id(0) > 0, halo_ref[...], 0.0)
        c = _conv_taps(ext_ref, halo_rows, cur_ref[...], w_ref[...])
        qn, kn, v, beta_all, g_all = _dn_pre(c[:, :DN_WIDTH], c[:, DN_WIDTH:2 * DN_WIDTH], c[:, 2 * DN_WIDTH:],
                                             ba_ref[...], a_ref[...], dt_ref[...])
        items = _intra_items()
        u, w, qd, kd, at, e8, inv = _dn_intra(*_intra_inputs(items, qn, kn, v, beta_all, g_all))
        for n, (ci, h, rows, lanes) in enumerate(items):
            u_ref[rows, lanes] = u[n]
            w_out_ref[rows, lanes] = w[n].astype(bf16)
            qd_ref[rows, lanes] = qd[n].astype(bf16)
            kd_ref[rows, lanes] = kd[n].astype(bf16)
            attn_ref[ci, h] = at[n]
            e_ref[ci, 8 * h:8 * h + 8, :] = e8[n]
            inv_ref[ci, h] = inv[n]

    return pl.pallas_call(
        body, name="dn_intra_forward", grid=(nc // INTRA_CHUNKS,),
        in_specs=[cur, halo, ba, conv, row, row],
        out_specs=[wide, wide, wide, wide, attn, e_end, attn],
        out_shape=_intra_shapes(seq, bf16) + [jax.ShapeDtypeStruct((nc, DN_HEADS, CHUNK, CHUNK), f32)],
        scratch_shapes=[pltpu.VMEM((INTRA_CHUNKS * CHUNK + 8, QKV_W), f32)],
        compiler_params=_params(("parallel",)),
    )(pdn, pdn, pdn, conv_w8, a_row, dt_row)


def _scan_specs(nc, reverse):
    steps = nc // SCAN_CHUNKS
    at = (lambda i: steps - 1 - i) if reverse else (lambda i: i)
    wide = pl.BlockSpec((SCAN_CHUNKS * CHUNK, DN_WIDTH), lambda i: (at(i), 0))
    attn = pl.BlockSpec((SCAN_CHUNKS, DN_HEADS, CHUNK, CHUNK), lambda i: (at(i), 0, 0, 0))
    e_end = pl.BlockSpec((SCAN_CHUNKS, 8 * DN_HEADS, DN_DIM), lambda i: (at(i), 0, 0))
    states = pl.BlockSpec((SCAN_CHUNKS, DN_HEADS, DN_DIM, DN_DIM), lambda i: (at(i), 0, 0, 0))
    return wide, attn, e_end, states


def _step_inputs(ci, rows, lanes, u_ref, w_ref, qd_ref, kd_ref, attn_ref, e_ref):
    heads = range(DN_HEADS)
    return ([u_ref[rows, lanes[h]] for h in heads], [w_ref[rows, lanes[h]].astype(f32) for h in heads],
            [qd_ref[rows, lanes[h]].astype(f32) for h in heads], [kd_ref[rows, lanes[h]].astype(f32) for h in heads],
            [attn_ref[ci, h] for h in heads], [e_ref[ci, 8 * h:8 * h + 1, :] for h in heads])


def _dn_scan_forward(u, w, q_dec, k_dec, attn, e_end):
    seq = u.shape[0]
    nc = seq // CHUNK
    wide, attn_spec, e_spec, st_spec = _scan_specs(nc, reverse=False)

    def body(u_ref, w_ref, qd_ref, kd_ref, attn_ref, e_ref, o_ref, st_ref, state_ref):
        @pl.when(pl.program_id(0) == 0)
        def _():
            state_ref[...] = jnp.zeros_like(state_ref)

        heads = range(DN_HEADS)
        lanes = [slice(h * DN_DIM, (h + 1) * DN_DIM) for h in heads]
        states = [state_ref[h] for h in heads]
        for ci in range(SCAN_CHUNKS):
            rows = slice(ci * CHUNK, (ci + 1) * CHUNK)
            for h in heads:
                st_ref[ci, h] = states[h].astype(bf16)
            o, states = _dn_step(*_step_inputs(ci, rows, lanes, u_ref, w_ref, qd_ref, kd_ref, attn_ref, e_ref), states)
            for h in heads:
                o_ref[rows, lanes[h]] = o[h]
        for h in heads:
            state_ref[h] = states[h]

    return pl.pallas_call(
        body, name="dn_scan_forward", grid=(nc // SCAN_CHUNKS,),
        in_specs=[wide, wide, wide, wide, attn_spec, e_spec],
        out_specs=[wide, st_spec],
        out_shape=[jax.ShapeDtypeStruct((seq, DN_WIDTH), f32), jax.ShapeDtypeStruct((nc, DN_HEADS, DN_DIM, DN_DIM), bf16)],
        scratch_shapes=[pltpu.VMEM((DN_HEADS, DN_DIM, DN_DIM), f32)],
        compiler_params=_params(("arbitrary",)),
    )(u, w, q_dec, k_dec, attn, e_end)


def _dn_scan_backward(u, w, q_dec, k_dec, attn, e_end, states, d_o):
    seq = u.shape[0]
    nc = seq // CHUNK
    wide, attn_spec, e_spec, st_spec = _scan_specs(nc, reverse=True)

    def body(u_ref, w_ref, qd_ref, kd_ref, attn_ref, e_ref, st_ref, do_ref,
             du_ref, dw_ref, dqd_ref, dkd_ref, dattn_ref, de_ref, dstate_ref):
        @pl.when(pl.program_id(0) == 0)
        def _():
            dstate_ref[...] = jnp.zeros_like(dstate_ref)

        heads = range(DN_HEADS)
        lanes = [slice(h * DN_DIM, (h + 1) * DN_DIM) for h in heads]
        first_row = lax.broadcasted_iota(jnp.int32, (8, DN_DIM), 0) == 0
        dstates = [dstate_ref[h] for h in heads]
        for ci in reversed(range(SCAN_CHUNKS)):
            rows = slice(ci * CHUNK, (ci + 1) * CHUNK)
            _, step_vjp = jax.vjp(_dn_step, *_step_inputs(ci, rows, lanes, u_ref, w_ref, qd_ref, kd_ref, attn_ref, e_ref),
                                  [st_ref[ci, h].astype(f32) for h in heads])
            du, dw, dqd, dkd, dattn, de, dstates = step_vjp(([do_ref[rows, lanes[h]] for h in heads], dstates))
            for h in heads:
                du_ref[rows, lanes[h]] = du[h]
                dw_ref[rows, lanes[h]] = dw[h]
                dqd_ref[rows, lanes[h]] = dqd[h]
                dkd_ref[rows, lanes[h]] = dkd[h]
                dattn_ref[ci, h] = dattn[h]
                de_ref[ci, 8 * h:8 * h + 8, :] = jnp.where(first_row, jnp.broadcast_to(de[h], (8, DN_DIM)), 0.0)
        for h in heads:
            dstate_ref[h] = dstates[h]

    return pl.pallas_call(
        body, name="dn_scan_backward", grid=(nc // SCAN_CHUNKS,),
        in_specs=[wide, wide, wide, wide, attn_spec, e_spec, st_spec, wide],
        out_specs=[wide, wide, wide, wide, attn_spec, e_spec],
        out_shape=_intra_shapes(seq, f32),
        scratch_shapes=[pltpu.VMEM((DN_HEADS, DN_DIM, DN_DIM), f32)],
        compiler_params=_params(("arbitrary",)),
    )(u, w, q_dec, k_dec, attn, e_end, states, d_o)


def _dn_intra_backward(pdn, conv_w8, a_row, dt_row, invs, d_u, d_w, d_qd, d_kd, d_attn, d_e):
    seq = pdn.shape[0]
    nc = seq // CHUNK
    rows_per_step = INTRA_CHUNKS * CHUNK
    cur, halo, ba, conv, row, wide, attn, e_end = _intra_specs(nc)

    def body(cur_ref, halo_ref, ba_ref, w_ref, a_ref, dt_ref, inv_ref, du_ref, dw_ref, dqd_ref, dkd_ref, dattn_ref, de_ref,
             dconv_ref, dba_ref, drow_ref, ext_ref):
        @pl.when(pl.program_id(0) == 0)
        def _():
            drow_ref[...] = jnp.zeros_like(drow_ref)

        halo_rows = jnp.where(pl.program_id(0) > 0, halo_ref[...], 0.0)
        c = _conv_taps(ext_ref, halo_rows, cur_ref[...], w_ref[...])
        (qn, kn, v, beta_all, g_all), pre_vjp = jax.vjp(
            _dn_pre, c[:, :DN_WIDTH], c[:, DN_WIDTH:2 * DN_WIDTH], c[:, 2 * DN_WIDTH:], ba_ref[...], a_ref[...], dt_ref[...])
        lane = lax.broadcasted_iota(jnp.int32, (CHUNK, BA_W), 1)
        items = _intra_items()
        _, intra_vjp = jax.vjp(_dn_intra, *_intra_inputs(items, qn, kn, v, beta_all, g_all),
                               [inv_ref[ci, h] for ci, h, _, _ in items])
        dq, dk, dv, dbeta, dg, _ = intra_vjp((
            [du_ref[rows, lanes] for _, _, rows, lanes in items], [dw_ref[rows, lanes] for _, _, rows, lanes in items],
            [dqd_ref[rows, lanes] for _, _, rows, lanes in items], [dkd_ref[rows, lanes] for _, _, rows, lanes in items],
            [dattn_ref[ci, h] for ci, h, _, _ in items], [de_ref[ci, 8 * h:8 * h + 8, :] for ci, h, _, _ in items],
            [jnp.zeros((CHUNK, CHUNK), f32) for _ in items]))
        dq_rows, dk_rows, dv_rows, dbeta_rows, dg_rows = [], [], [], [], []
        for ci in range(INTRA_CHUNKS):
            of_chunk = [n for n, item in enumerate(items) if item[0] == ci]
            d_beta_all = jnp.zeros((CHUNK, BA_W), f32)
            d_g_all = jnp.zeros((CHUNK, BA_W), f32)
            for n in of_chunk:
                h = items[n][1]
                d_beta_all = d_beta_all + jnp.where(lane == h, dbeta[n], 0.0)
                d_g_all = d_g_all + jnp.where(lane == DN_HEADS + h, dg[n], 0.0)
            dq_rows.append(jnp.concatenate([dq[n] for n in of_chunk], axis=1))
            dk_rows.append(jnp.concatenate([dk[n] for n in of_chunk], axis=1))
            dv_rows.append(jnp.concatenate([dv[n] for n in of_chunk], axis=1))
            dbeta_rows.append(d_beta_all)
            dg_rows.append(d_g_all)
        stack = lambda parts: jnp.concatenate(parts, axis=0)
        dcq, dck, dcv, dba, da_row, ddt_row = pre_vjp(
            (stack(dq_rows), stack(dk_rows), stack(dv_rows), stack(dbeta_rows), stack(dg_rows)))
        dconv_ref[:, :DN_WIDTH] = dcq
        dconv_ref[:, DN_WIDTH:2 * DN_WIDTH] = dck
        dconv_ref[:, 2 * DN_WIDTH:] = dcv
        dba_ref[...] = dba
        drow_ref[0:1, :] += da_row
        drow_ref[1:2, :] += ddt_row

    return pl.pallas_call(
        body, name="dn_intra_backward", grid=(nc // INTRA_CHUNKS,),
        in_specs=[cur, halo, ba, conv, row, row, attn, wide, wide, wide, wide, attn, e_end],
        out_specs=[pl.BlockSpec((rows_per_step, QKV_W), lambda i: (i, 0)),
                   pl.BlockSpec((rows_per_step, BA_W), lambda i: (i, 0)),
                   pl.BlockSpec((8, BA_W), lambda i: (0, 0))],
        out_shape=[jax.ShapeDtypeStruct((seq, QKV_W), f32), jax.ShapeDtypeStruct((seq, BA_W), f32),
                   jax.ShapeDtypeStruct((8, BA_W), f32)],
        scratch_shapes=[pltpu.VMEM((rows_per_step + 8, QKV_W), f32)],
        compiler_params=_params(("arbitrary",)),
    )(pdn, pdn, pdn, conv_w8, a_row, dt_row, invs, d_u, d_w, d_qd, d_kd, d_attn, d_e)


def _band_masks(block_index):
    qi = lax.broadcasted_iota(jnp.int32, (Q_BLOCK, Q_BLOCK), 0)
    kj = lax.broadcasted_iota(jnp.int32, (Q_BLOCK, Q_BLOCK), 1)
    return (kj >= qi) & (block_index > 0), kj <= qi


def _low_half():
    return lax.broadcasted_iota(jnp.int32, (Q_BLOCK, LANES), 1) < AT_DIM


def _head_pairs(ref, split):
    low = _low_half()
    slabs = [ref[:, pair * LANES:(pair + 1) * LANES].astype(bf16) for pair in range(AT_HEADS // 2)]
    if not split:
        return slabs
    zero = jnp.zeros((Q_BLOCK, LANES), bf16)
    return [jnp.where(low, slab, zero) if h == 0 else jnp.where(low, zero, slab) for slab in slabs for h in range(2)]


def _stack_blocks(first, second):
    return [jnp.concatenate([a, b], axis=0) for a, b in zip(first, second)]


ATTN_BLOCKS = 4


def _two_block_spec(width, pieces, piece, which, n_blocks=None):
    b = ATTN_BLOCKS
    if which == "own":
        return pl.BlockSpec((b * Q_BLOCK, width), lambda r, n: (n, pieces * r + piece))
    if which == "before":
        return pl.BlockSpec((Q_BLOCK, width), lambda r, n: (jnp.maximum(b * n - 1, 0), pieces * r + piece))
    return pl.BlockSpec((Q_BLOCK, width), lambda r, n: (jnp.minimum(b * n + b, n_blocks - 1), pieces * r + piece))


def _block_rows(i):
    return pl.ds(i * Q_BLOCK, Q_BLOCK)


def _attn_forward(pat_view, dil):
    length = pat_view.shape[0]
    nb = length // Q_BLOCK
    scale = AT_DIM ** -0.5

    def body(q2_ref, kp_ref, k2_ref, vp_ref, v2_ref, o2_ref, lse2_ref):
        n = pl.program_id(1)
        for i in range(ATTN_BLOCKS):
            rows, before = _block_rows(i), _block_rows(i - 1)
            one_block(ATTN_BLOCKS * n + i, q2_ref.at[rows], kp_ref if i == 0 else k2_ref.at[before], k2_ref.at[rows],
                      vp_ref if i == 0 else v2_ref.at[before], v2_ref.at[rows], o2_ref.at[rows], lse2_ref.at[rows])

    def one_block(block_index, q_ref, kp_ref, kc_ref, vp_ref, vc_ref, o_ref, lse_ref):
        mask = jnp.concatenate(_band_masks(block_index), axis=1)
        low = _low_half()
        heads = range(AT_HEADS)
        q = _head_pairs(q_ref, split=True)
        k = _stack_blocks(_head_pairs(kp_ref, split=False), _head_pairs(kc_ref, split=False))
        v = _stack_blocks(_head_pairs(vp_ref, split=False), _head_pairs(vc_ref, split=False))
        s = [_dot(q[h], k[h // 2], NT, False) for h in heads]
        p, top = [], []
        for h in heads:
            masked = jnp.where(mask, s[h] * scale, NEG)
            m = jnp.max(masked, axis=1, keepdims=True)
            p.append(jnp.exp(masked - m).astype(bf16))
            top.append(m)
        ones = jnp.ones((2 * Q_BLOCK, LANES), bf16)
        l = [_dot(p[h], ones, NN, False) for h in heads]
        o = [_dot(p[h], v[h // 2], NN, False) for h in heads]
        for pair in range(AT_HEADS // 2):
            even, odd = 2 * pair, 2 * pair + 1
            slab = slice(pair * LANES, (pair + 1) * LANES)
            o_ref[:, slab] = jnp.where(low, o[even], o[odd]) / jnp.where(low, l[even], l[odd])
        lane = lax.broadcasted_iota(jnp.int32, (Q_BLOCK, HEAD_LANES), 1)
        lse = jnp.zeros((Q_BLOCK, HEAD_LANES), f32)
        for h in heads:
            lse = jnp.where(lane == h, top[h] + jnp.log(l[h]), lse)
        lse_ref[...] = lse

    blk = functools.partial(_two_block_spec, AT_WIDTH, 3)
    return pl.pallas_call(
        body, name=f"attn_forward_d{dil}", grid=(dil, nb // ATTN_BLOCKS),
        in_specs=[blk(0, "own"), blk(1, "before"), blk(1, "own"), blk(2, "before"), blk(2, "own")],
        out_specs=[_two_block_spec(AT_WIDTH, 1, 0, "own"), _two_block_spec(HEAD_LANES, 1, 0, "own")],
        out_shape=[jax.ShapeDtypeStruct((length, dil * AT_WIDTH), f32),
                   jax.ShapeDtypeStruct((length, dil * HEAD_LANES), f32)],
        compiler_params=_params(("parallel", "parallel")),
    )(pat_view, pat_view, pat_view, pat_view, pat_view)


def _attn_backward_q(pat_view, d_out, delta, lse, dil):
    length = pat_view.shape[0]
    nb = length // Q_BLOCK
    scale = AT_DIM ** -0.5

    def body(q2_ref, kp_ref, k2_ref, vp_ref, v2_ref, do2_ref, dl2_ref, lse2_ref, dq2_ref):
        n = pl.program_id(1)
        for i in range(ATTN_BLOCKS):
            rows, before = _block_rows(i), _block_rows(i - 1)
            one_block(ATTN_BLOCKS * n + i, q2_ref.at[rows], kp_ref if i == 0 else k2_ref.at[before], k2_ref.at[rows],
                      vp_ref if i == 0 else v2_ref.at[before], v2_ref.at[rows],
                      do2_ref.at[rows], dl2_ref.at[rows], lse2_ref.at[rows], dq2_ref.at[rows])

    def one_block(block_index, q_ref, kp_ref, kc_ref, vp_ref, vc_ref, do_ref, dl_ref, lse_ref, dq_ref):
        mask = jnp.concatenate(_band_masks(block_index), axis=1)
        low = _low_half()
        heads = range(AT_HEADS)
        q, do = _head_pairs(q_ref, split=True), _head_pairs(do_ref, split=True)
        k = _stack_blocks(_head_pairs(kp_ref, split=False), _head_pairs(kc_ref, split=False))
        v = _stack_blocks(_head_pairs(vp_ref, split=False), _head_pairs(vc_ref, split=False))
        s = [_dot(q[h], k[h // 2], NT, False) for h in heads]
        dp = [_dot(do[h], v[h // 2], NT, False) for h in heads]
        ds = []
        for h in heads:
            p = jnp.exp(jnp.where(mask, s[h] * scale - lse_ref[:, h:h + 1], NEG))
            ds.append((p * (dp[h] - dl_ref[:, h:h + 1])).astype(bf16))
        dq = [_dot(ds[h], k[h // 2], NN, False) for h in heads]
        for pair in range(AT_HEADS // 2):
            dq_ref[:, pair * LANES:(pair + 1) * LANES] = (
                jnp.where(low, dq[2 * pair], dq[2 * pair + 1]) * scale).astype(bf16)

    blk = functools.partial(_two_block_spec, AT_WIDTH, 3)
    one = _two_block_spec(AT_WIDTH, 1, 0, "own")
    compact = _two_block_spec(HEAD_LANES, 1, 0, "own")
    return pl.pallas_call(
        body, name=f"attn_backward_q_d{dil}", grid=(dil, nb // ATTN_BLOCKS),
        in_specs=[blk(0, "own"), blk(1, "before"), blk(1, "own"), blk(2, "before"), blk(2, "own"), one, compact, compact],
        out_specs=one,
        out_shape=jax.ShapeDtypeStruct((length, dil * AT_WIDTH), bf16),
        compiler_params=_params(("parallel", "parallel")),
    )(pat_view, pat_view, pat_view, pat_view, pat_view, d_out, delta, lse)


def _attn_backward_kv(pat_view, d_out, delta, lse, dil):
    length = pat_view.shape[0]
    nb = length // Q_BLOCK
    scale = AT_DIM ** -0.5

    def body(k2_ref, v2_ref, q2_ref, qn_ref, do2_ref, don_ref, dl2_ref, dln_ref, lse2_ref, lsen_ref, dk2_ref, dv2_ref):
        n = pl.program_id(1)
        for i in range(ATTN_BLOCKS):
            rows, after = _block_rows(i), _block_rows(i + 1)
            last = i == ATTN_BLOCKS - 1
            one_block(ATTN_BLOCKS * n + i, k2_ref.at[rows], v2_ref.at[rows],
                      q2_ref.at[rows], qn_ref if last else q2_ref.at[after],
                      do2_ref.at[rows], don_ref if last else do2_ref.at[after],
                      dl2_ref.at[rows], dln_ref if last else dl2_ref.at[after],
                      lse2_ref.at[rows], lsen_ref if last else lse2_ref.at[after], dk2_ref.at[rows], dv2_ref.at[rows])

    def one_block(j, k_ref, v_ref, qa_ref, qb_ref, doa_ref, dob_ref, dla_ref, dlb_ref, lsea_ref, lseb_ref, dk_ref, dv_ref):
        kj = lax.broadcasted_iota(jnp.int32, (Q_BLOCK, Q_BLOCK), 0)
        qi = lax.broadcasted_iota(jnp.int32, (Q_BLOCK, Q_BLOCK), 1)
        mask = jnp.concatenate([kj <= qi,
                                (kj >= qi) & (j + 1 < nb)],
                               axis=1)
        low = _low_half()
        heads = range(AT_HEADS)
        k, v = _head_pairs(k_ref, split=True), _head_pairs(v_ref, split=True)
        q = _stack_blocks(_head_pairs(qa_ref, split=False), _head_pairs(qb_ref, split=False))
        do = _stack_blocks(_head_pairs(doa_ref, split=False), _head_pairs(dob_ref, split=False))
        row = lax.broadcasted_iota(jnp.int32, (8 * AT_HEADS, HEAD_LANES), 0)
        col = lax.broadcasted_iota(jnp.int32, (8 * AT_HEADS, HEAD_LANES), 1)
        pick = (col == row // 8).astype(f32)
        lse_rows = _dot_sel(pick, jnp.concatenate([lsea_ref[...], lseb_ref[...]], axis=0), NT, 0)
        dl_rows = _dot_sel(pick, jnp.concatenate([dla_ref[...], dlb_ref[...]], axis=0), NT, 0)
        s_t = [_dot(k[h], q[h // 2], NT, False) for h in heads]
        dp_t = [_dot(v[h], do[h // 2], NT, False) for h in heads]
        p_t, ds_t = [], []
        for h in heads:
            prob = jnp.exp(jnp.where(mask, s_t[h] * scale - lse_rows[8 * h:8 * h + 1, :], NEG))
            p_t.append(prob.astype(bf16))
            ds_t.append((prob * (dp_t[h] - dl_rows[8 * h:8 * h + 1, :])).astype(bf16))
        dv = [_dot(p_t[h], do[h // 2], NN, False) for h in heads]
        dk = [_dot(ds_t[h], q[h // 2], NN, False) for h in heads]
        for pair in range(AT_HEADS // 2):
            slab = slice(pair * LANES, (pair + 1) * LANES)
            dk_ref[:, slab] = (jnp.where(low, dk[2 * pair], dk[2 * pair + 1]) * scale).astype(bf16)
            dv_ref[:, slab] = jnp.where(low, dv[2 * pair], dv[2 * pair + 1]).astype(bf16)

    blk = functools.partial(_two_block_spec, AT_WIDTH, 3)
    same = _two_block_spec(AT_WIDTH, 1, 0, "own")
    nxt = _two_block_spec(AT_WIDTH, 1, 0, "after", nb)
    c_same = _two_block_spec(HEAD_LANES, 1, 0, "own")
    c_nxt = _two_block_spec(HEAD_LANES, 1, 0, "after", nb)
    return pl.pallas_call(
        body, name=f"attn_backward_kv_d{dil}", grid=(dil, nb // ATTN_BLOCKS),
        in_specs=[blk(1, "own"), blk(2, "own"), blk(0, "own"), blk(0, "after", nb), same, nxt, c_same, c_nxt, c_same, c_nxt],
        out_specs=[same, same],
        out_shape=[jax.ShapeDtypeStruct((length, dil * AT_WIDTH), bf16)] * 2,
        compiler_params=_params(("parallel", "parallel")),
    )(pat_view, pat_view, pat_view, pat_view, d_out, d_out, delta, delta, lse, lse)


def _gated_norms(o_dn, z_dn, o_at, z_at, dn_w, at_w):
    ms_dn = _slab_sums(o_dn * o_dn) * (1.0 / DN_DIM)
    a = o_dn * lax.rsqrt(ms_dn + EPS) * dn_w * _silu(z_dn)
    head_of_lane = (lax.broadcasted_iota(jnp.int32, (AT_WIDTH, HEAD_LANES), 0) // AT_DIM
                    == lax.broadcasted_iota(jnp.int32, (AT_WIDTH, HEAD_LANES), 1)).astype(f32)
    lanes_of_head = (lax.broadcasted_iota(jnp.int32, (HEAD_LANES, AT_WIDTH), 0)
                     == lax.broadcasted_iota(jnp.int32, (HEAD_LANES, AT_WIDTH), 1) // AT_DIM).astype(f32)
    ms_at = _sel_right(_sel_right(o_at * o_at, head_of_lane), lanes_of_head) * (1.0 / AT_DIM)
    b = o_at * lax.rsqrt(ms_at + EPS) * at_w * _silu(z_at)
    return a, b


def _residual_loss(x, mix, gate, fin_w, target):
    x2 = x + gate * mix
    y = x2 * lax.rsqrt(jnp.mean(x2 * x2, axis=-1, keepdims=True) + EPS) * fin_w
    err = y - target
    per_token = jnp.sum(err * err, axis=1, keepdims=True) * (1.0 / D_MODEL)
    return 0.5 * jnp.sum(per_token, axis=0, keepdims=True)


TAIL_ROWS = 8


def _tail(x, target, o_dn, pdn, outs, lses, z_at, gate, w_out, dn_w_row, at_w_row, fin_w):
    seq = x.shape[0]
    tm = TOKEN_TILE

    def body(x_ref, t_ref, odn_ref, zdn_ref, o1_ref, o4_ref, o16_ref, l1_ref, l4_ref, l16_ref, zat_ref, gate_ref,
             wt_ref, wb_ref, dnw_ref, atw_ref, fw_ref,
             dx2_ref, dodn_ref, dzdn_ref, doat1_ref, doat4_ref, doat16_ref, delta1_ref, delta4_ref, delta16_ref,
             lse1_ref, lse4_ref, lse16_ref, dzat_ref, gwt_ref, gwb_ref, rows_ref, *stages):
        @pl.when(pl.program_id(0) == 0)
        def _():
            gwt_ref[...] = jnp.zeros_like(gwt_ref)
            gwb_ref[...] = jnp.zeros_like(gwb_ref)
            rows_ref[...] = jnp.zeros_like(rows_ref)

        l1, l4, l16 = l1_ref[...], _view_to_value(stages[0], l4_ref, 4), _view_to_value(stages[1], l16_ref, 16)
        o4, o16 = _view_to_value(stages[2], o4_ref, 4), _view_to_value(stages[3], o16_ref, 16)
        top = jnp.maximum(jnp.maximum(l1, l4), l16)
        e1, e4, e16 = jnp.exp(l1 - top), jnp.exp(l4 - top), jnp.exp(l16 - top)
        den = e1 + e4 + e16
        lse = top + jnp.log(den)
        o_at = (_repeat_heads(e1 / den) * o1_ref[...] + _repeat_heads(e4 / den) * o4 + _repeat_heads(e16 / den) * o16)

        (a, b), norms_vjp = jax.vjp(_gated_norms, odn_ref[...], zdn_ref[...], o_at, zat_ref[...], dnw_ref[...], atw_ref[...])
        a, b = a.astype(bf16), b.astype(bf16)
        mix = jnp.dot(a, wt_ref[...], preferred_element_type=f32) + jnp.dot(b, wb_ref[...], preferred_element_type=f32)
        loss, loss_vjp = jax.vjp(_residual_loss, x_ref[...], mix, gate_ref[...], fw_ref[...], t_ref[...])
        dx2, dmix, dgate, dfw, _ = loss_vjp(jnp.ones((1, 1), f32))
        dmix = dmix.astype(bf16)
        dwt = lax.dot_general(a, dmix, (TN, ((), ())), preferred_element_type=f32)
        dwb = lax.dot_general(b, dmix, (TN, ((), ())), preferred_element_type=f32)
        da = lax.dot_general(dmix, wt_ref[...], (NT, ((), ())), preferred_element_type=f32)
        db = lax.dot_general(dmix, wb_ref[...], (NT, ((), ())), preferred_element_type=f32)
        dodn, dzdn, doat, dzat, ddnw, datw = norms_vjp((da, db))
        dx2_ref[...] = dx2
        dodn_ref[...] = dodn
        dzdn_ref[...] = dzdn
        dzat_ref[...] = dzat
        lane_head = (lax.broadcasted_iota(jnp.int32, (AT_WIDTH, HEAD_LANES), 0) // AT_DIM
                     == lax.broadcasted_iota(jnp.int32, (AT_WIDTH, HEAD_LANES), 1))
        delta = _dot_sel(lane_head.astype(f32), doat * o_at, NN, 1)
        for stage_ref, value, refs in ((stages[2], doat, (doat1_ref, doat4_ref, doat16_ref)),
                                       (stages[0], delta, (delta1_ref, delta4_ref, delta16_ref)),
                                       (stages[1], lse, (lse1_ref, lse4_ref, lse16_ref))):
            refs[0][...] = value.astype(refs[0].dtype)
            _stage_lanes(stage_ref, 0, value)
            _stage_to_view(stage_ref, refs[1], 4)
            _stage_to_view(stage_ref, refs[2], 16)
        gwt_ref[...] += dwt
        gwb_ref[...] += dwb
        rows_ref[0:1, :] += jnp.broadcast_to(loss, (1, D_MODEL))
        rows_ref[1:2, :] += dgate
        rows_ref[2:3, :] += dfw
        rows_ref[3:4, 0:DN_WIDTH] += ddnw
        rows_ref[4:5, 0:AT_WIDTH] += datw

    tile = lambda w: pl.BlockSpec((tm, w), lambda i: (i, 0))
    row = lambda w: pl.BlockSpec((1, w), lambda i: (0, 0))
    half_w = pl.BlockSpec((DN_WIDTH, D_MODEL), lambda i: (0, 0))
    sds = lambda w: jax.ShapeDtypeStruct((seq, w), f32)
    views = [_view_spec(AT_WIDTH, d) for d in PATTERN_DILATIONS]
    compact = [_view_spec(HEAD_LANES, d) for d in PATTERN_DILATIONS]
    doat_shapes = [_view_shape(seq, AT_WIDTH, d, bf16) for d in PATTERN_DILATIONS]
    compact_shapes = [_view_shape(seq, HEAD_LANES, d) for d in PATTERN_DILATIONS]
    return pl.pallas_call(
        body, name="tail", grid=(seq // tm,),
        in_specs=[tile(D_MODEL), tile(D_MODEL), tile(DN_WIDTH),
                  pl.BlockSpec((tm, DN_WIDTH), lambda i: (i, QKV_W // DN_WIDTH)),
                  *views, *compact, tile(AT_WIDTH), row(D_MODEL), half_w, pl.BlockSpec((AT_WIDTH, D_MODEL), lambda i: (1, 0)),
                  row(DN_WIDTH), row(AT_WIDTH), row(D_MODEL)],
        out_specs=[tile(D_MODEL), tile(DN_WIDTH), tile(DN_WIDTH), *views, *compact, *compact,
                   tile(AT_WIDTH), half_w, half_w, pl.BlockSpec((TAIL_ROWS, D_MODEL), lambda i: (0, 0))],
        out_shape=[sds(D_MODEL), sds(DN_WIDTH), sds(DN_WIDTH), *doat_shapes, *compact_shapes, *compact_shapes,
                   sds(AT_WIDTH), jax.ShapeDtypeStruct((DN_WIDTH, D_MODEL), f32),
                   jax.ShapeDtypeStruct((AT_WIDTH, D_MODEL), f32), jax.ShapeDtypeStruct((TAIL_ROWS, D_MODEL), f32)],
        scratch_shapes=[_stage(HEAD_LANES)] * 2 + [_stage(AT_WIDTH)] * 2,
        compiler_params=_params(("arbitrary",)),
    )(x, target, o_dn, pdn, outs[0], outs[1], outs[2], lses[0], lses[1], lses[2], z_at, gate, w_out, w_out,
      dn_w_row, at_w_row, fin_w)


PRE_ROWS = 8


def _pre_backward(x, dx2, pos_col, invf_row, sign_row, norm_w, scale, shift, w_cat, conv_w8, pdn, d_conv, d_zdn, d_ba,
                  d_q, d_k, d_v, d_zat):
    seq = x.shape[0]
    tm = TOKEN_TILE
    last = seq // tm - 1

    def body(x_ref, dx2_ref, pos_ref, invf_ref, sign_ref, nw_ref, sc_ref, sh_ref, w_ref, cw_ref,
             pre_ref, prehalo_ref, dc_ref, dchalo_ref, dz_ref, dba_ref,
             dq1_ref, dq4_ref, dq16_ref, dk1_ref, dk4_ref, dk16_ref, dv1_ref, dv4_ref, dv16_ref, dzat_ref,
             gx_ref, dproj_ref, rows_ref, crow_ref, ext_ref, *stages):
        i = pl.program_id(0)

        @pl.when(i == 0)
        def _():
            rows_ref[...] = jnp.zeros_like(rows_ref)
            crow_ref[...] = jnp.zeros_like(crow_ref)

        dc = dc_ref[...]
        ext_ref[0:tm, :] = dc
        ext_ref[tm:tm + 8, :] = jnp.where(i < last, dchalo_ref[...], 0.0)
        w8 = cw_ref[...]
        d_pre = ext_ref[pl.ds(3, tm), :] * w8[0:1, :]
        for j in range(1, CONV_K):
            d_pre = d_pre + ext_ref[pl.ds(3 - j, tm), :] * w8[j:j + 1, :]
        ext_ref[0:8, :] = jnp.where(i > 0, prehalo_ref[...], 0.0)
        ext_ref[8:8 + tm, :] = pre_ref[...]
        for j in range(CONV_K):
            crow_ref[j:j + 1, :] += jnp.sum(dc * ext_ref[pl.ds(5 + j, tm), :], axis=0, keepdims=True)

        cos_t, sin_t = _rope_tables(pos_ref[...], invf_ref[...], sign_ref[...])
        dq = dq1_ref[...] + _view_to_value(stages[0], dq4_ref, 4) + _view_to_value(stages[1], dq16_ref, 16)
        dk = dk1_ref[...] + _view_to_value(stages[2], dk4_ref, 4) + _view_to_value(stages[3], dk16_ref, 16)
        dq = dq * cos_t + _rope_partner(dq * sin_t)
        dk = dk * cos_t + _rope_partner(dk * sin_t)
        dv = dv1_ref[...] + _view_to_value(stages[4], dv4_ref, 4) + _view_to_value(stages[5], dv16_ref, 16)
        d_proj = jnp.concatenate([d_pre, dz_ref[...], dba_ref[...], dq, dk, dv, dzat_ref[...]], axis=1).astype(bf16)
        dproj_ref[...] = d_proj
        dh = lax.dot_general(d_proj, w_ref[...], (NT, ((), ())), preferred_element_type=f32)
        _, vjp = jax.vjp(_hnorm, x_ref[...], nw_ref[...], sc_ref[...], sh_ref[...])
        dx, dnw, dsc, dsh = vjp(dh)
        gx_ref[...] = dx + dx2_ref[...]
        rows_ref[0:1, :] += dnw
        rows_ref[1:2, :] += dsc
        rows_ref[2:3, :] += dsh

    tile = lambda w: pl.BlockSpec((tm, w), lambda i: (i, 0))
    row = lambda w: pl.BlockSpec((1, w), lambda i: (0, 0))
    step8 = tm // 8
    return pl.pallas_call(
        body, name="pre_backward", grid=(seq // tm,),
        in_specs=[tile(D_MODEL), tile(D_MODEL), tile(1), row(128), row(128), row(D_MODEL), row(D_MODEL), row(D_MODEL),
                  pl.BlockSpec((D_MODEL, CAT_W), lambda i: (0, 0)), pl.BlockSpec((8, QKV_W), lambda i: (0, 0)),
                  tile(QKV_W), pl.BlockSpec((8, QKV_W), lambda i: (jnp.maximum(i * step8 - 1, 0), 0)),
                  tile(QKV_W), pl.BlockSpec((8, QKV_W), lambda i: (jnp.minimum((i + 1) * step8, seq // 8 - 1), 0)),
                  tile(DN_WIDTH), tile(BA_W)] + [_view_spec(AT_WIDTH, d) for d in PATTERN_DILATIONS] * 3 + [tile(AT_WIDTH)],
        out_specs=[tile(D_MODEL), tile(CAT_W), pl.BlockSpec((PRE_ROWS, D_MODEL), lambda i: (0, 0)),
                   pl.BlockSpec((8, QKV_W), lambda i: (0, 0))],
        out_shape=[jax.ShapeDtypeStruct((seq, D_MODEL), f32), jax.ShapeDtypeStruct((seq, CAT_W), bf16),
                   jax.ShapeDtypeStruct((PRE_ROWS, D_MODEL), f32), jax.ShapeDtypeStruct((8, QKV_W), f32)],
        scratch_shapes=[pltpu.VMEM((tm + 8, QKV_W), f32)] + [_stage(AT_WIDTH)] * 6,
        compiler_params=_params(("arbitrary",)),
    )(x, dx2, pos_col, invf_row, sign_row, norm_w, scale, shift, w_cat, conv_w8, pdn, pdn, d_conv, d_conv, d_zdn, d_ba,
      d_q[0], d_q[1], d_q[2], d_k[0], d_k[1], d_k[2], d_v[0], d_v[1], d_v[2], d_zat)


def _weight_grad(h, d_proj):
    seq = h.shape[0]
    tk, tn = 512, CAT_W // 3
    n_k = seq // tk

    def body(h_ref, d_ref, o_ref):
        @pl.when(pl.program_id(1) == 0)
        def _():
            o_ref[...] = jnp.zeros_like(o_ref)

        o_ref[...] += lax.dot_general(h_ref[...], d_ref[...], (TN, ((), ())), preferred_element_type=f32)

    return pl.pallas_call(
        body, name="weight_grad", grid=(CAT_W // tn, n_k),
        in_specs=[pl.BlockSpec((tk, D_MODEL), lambda n, k: (k, 0)), pl.BlockSpec((tk, tn), lambda n, k: (k, n))],
        out_specs=pl.BlockSpec((D_MODEL, tn), lambda n, k: (0, n)),
        out_shape=jax.ShapeDtypeStruct((D_MODEL, CAT_W), f32),
        compiler_params=_params(("parallel", "arbitrary")),
    )(h, d_proj)


def _adamw(w, g, m, v):
    m = ADAM_B1 * m + (1.0 - ADAM_B1) * g
    v = ADAM_B2 * v + (1.0 - ADAM_B2) * (g * g)
    m_hat = m / (1.0 - ADAM_B1 ** ADAM_STEP)
    v_hat = v / (1.0 - ADAM_B2 ** ADAM_STEP)
    delta = -ADAM_LR * (m_hat / (jnp.sqrt(v_hat) + ADAM_EPS) + ADAM_WD * w)
    return delta, m, v


def _mod_shard(c_all, w_mod_s, b_mod_s):
    def body(c_ref, w_ref, b_ref, o_ref):
        o_ref[...] = _dot(_silu(c_ref[...]), w_ref[...], NN, True) + b_ref[...]

    return pl.pallas_call(body, name="mod_shard", out_shape=jax.ShapeDtypeStruct((N_DEV, w_mod_s.shape[1]), f32),
                          compiler_params=_params())(c_all, w_mod_s, b_mod_s)


def _mod_update(c_all, d_mod_s, w, m, v):
    def body(c_ref, d_ref, w_ref, m_ref, v_ref, g_ref, dw_ref, nm_ref, nv_ref):
        g = _dot(_silu(c_ref[...]), d_ref[...], TN, True)
        g_ref[...] = g
        dw_ref[...], nm_ref[...], nv_ref[...] = _adamw(w_ref[...], g, m_ref[...], v_ref[...])

    return pl.pallas_call(body, name="mod_update", out_shape=[jax.ShapeDtypeStruct(w.shape, f32)] * 4,
                          compiler_params=_params())(c_all, d_mod_s, w, m, v)


def _adamw_rows(w, g, m, v, name):
    rows, width = w.shape
    tr = 128 if rows % 128 == 0 else rows

    def body(w_ref, g_ref, m_ref, v_ref, dw_ref, nm_ref, nv_ref):
        dw_ref[...], nm_ref[...], nv_ref[...] = _adamw(w_ref[...], g_ref[...], m_ref[...], v_ref[...])

    spec = pl.BlockSpec((tr, width), lambda i: (i, 0))
    return pl.pallas_call(body, name=name, grid=(rows // tr,), in_specs=[spec] * 4, out_specs=[spec] * 3,
                          out_shape=[jax.ShapeDtypeStruct(w.shape, f32)] * 3,
                          compiler_params=_params(("parallel",)))(w, g, m, v)


SM_NORM, SM_FIN, SM_DN, SM_AT, SM_A, SM_DT, SM_LOSS, SM_MOD = 0, 1024, 2048, 2560, 3072, 3200, 3328, 3456
SM_W = SM_MOD + 3 * D_MODEL
RS_NORM, RS_FIN, RS_DN, RS_AT, RS_A, RS_DT, RS_BMOD = 0, 1024, 2048, 2176, 2304, 2432, 2560
RS_W = RS_BMOD + 3 * D_MODEL


def _small_update(gathered, w, m, v):
    def body(g_ref, w_ref, m_ref, v_ref, grad_ref, dw_ref, nm_ref, nv_ref, loss_ref, dmod_ref):
        total = g_ref[0:1, :]
        for dev in range(1, N_DEV):
            total = total + g_ref[8 * dev:8 * dev + 1, :]
        for dev in range(N_DEV):
            dmod_ref[dev:dev + 1, :] = g_ref[8 * dev:8 * dev + 1, SM_MOD:SM_W]
        dn = total[:, SM_DN:SM_DN + DN_DIM]
        for h in range(1, DN_HEADS):
            dn = dn + total[:, SM_DN + h * DN_DIM:SM_DN + (h + 1) * DN_DIM]
        at = total[:, SM_AT:SM_AT + AT_DIM]
        for h in range(1, AT_HEADS):
            at = at + total[:, SM_AT + h * AT_DIM:SM_AT + (h + 1) * AT_DIM]
        grad_ref[:, RS_NORM:RS_FIN] = total[:, SM_NORM:SM_FIN]
        grad_ref[:, RS_FIN:RS_DN] = total[:, SM_FIN:SM_DN]
        grad_ref[:, RS_DN:RS_AT] = dn
        grad_ref[:, RS_AT:RS_A] = jnp.zeros((1, 128), f32)
        grad_ref[:, RS_AT:RS_AT + AT_DIM] = at
        grad_ref[:, RS_A:RS_DT] = total[:, SM_A:SM_DT]
        grad_ref[:, RS_DT:RS_BMOD] = total[:, SM_DT:SM_LOSS]
        grad_ref[:, RS_BMOD:RS_W] = total[:, SM_MOD:SM_W]
        loss_ref[...] = total[:, SM_LOSS:SM_MOD]
        dw_ref[...], nm_ref[...], nv_ref[...] = _adamw(w_ref[...], grad_ref[...], m_ref[...], v_ref[...])

    row = jax.ShapeDtypeStruct((1, RS_W), f32)
    return pl.pallas_call(
        body, name="small_update",
        out_shape=[row, row, row, row, jax.ShapeDtypeStruct((1, 128), f32), jax.ShapeDtypeStruct((N_DEV, 3 * D_MODEL), f32)],
        compiler_params=_params())(gathered, w, m, v)


def _all_gather_rows(block, name):
    m_per, n = block.shape

    def body(x_ref, out_ref, send_sems, recv_sems, local_sem):
        x, y, c = lax.axis_index("x"), lax.axis_index("y"), lax.axis_index("c")
        me, sibling = (x, y, c), (x, y, 1 - c)
        chips = [(1 - x, y), (x, 1 - y), (1 - x, 1 - y)]

        def rows(px, py, pc):
            return out_ref.at[pl.ds((4 * px + 2 * py + pc) * m_per, m_per), :]

        def copy(k, blk, to, src=None):
            return pltpu.make_async_remote_copy(
                src_ref=rows(*blk) if src is None else src, dst_ref=rows(*blk),
                send_sem=send_sems.at[k], recv_sem=recv_sems.at[k], device_id=to, device_id_type=MESH)

        mine = pltpu.make_async_copy(x_ref, rows(*me), local_sem)
        mine.start()
        first = [copy(0, me, sibling, src=x_ref)]
        first += [copy(1 + j, me, (*chip, c), src=x_ref) for j, chip in enumerate(chips)]
        for cp in first:
            cp.start()
        passed = [copy(4 + j, (*chip, c), sibling) for j, chip in enumerate(chips)]
        for j, chip in enumerate(chips):
            copy(1 + j, (*chip, c), me).wait_recv()
            passed[j].start()
        copy(0, sibling, me).wait_recv()
        for j, chip in enumerate(chips):
            copy(4 + j, (*chip, 1 - c), me).wait_recv()
        for cp in first + passed:
            cp.wait_send()
        mine.wait()

    return pl.pallas_call(
        body, name=name,
        out_shape=jax.ShapeDtypeStruct((N_DEV * m_per, n), block.dtype),
        in_specs=[pl.BlockSpec(memory_space=pltpu.VMEM)],
        out_specs=pl.BlockSpec(memory_space=pltpu.VMEM),
        scratch_shapes=[pltpu.SemaphoreType.DMA((7,)), pltpu.SemaphoreType.DMA((7,)), pltpu.SemaphoreType.DMA],
        compiler_params=pltpu.CompilerParams(vmem_limit_bytes=VMEM_LIMIT),
    )(block)


def _gather_weight_shards(packed):
    rows, width = packed.shape
    half = rows // 2

    def body(src_ref, out_ref, send_sems, recv_sems, local_sem):
        x, y, c = lax.axis_index("x"), lax.axis_index("y"), lax.axis_index("c")
        chips = [(1 - x, y), (x, 1 - y), (1 - x, 1 - y)]
        my_rows = pl.ds(pl.multiple_of(c * half, 16), half)
        sibling_rows = pl.ds(pl.multiple_of((1 - c) * half, 16), half)

        def part(chip, part_rows):
            return out_ref.at[2 * chip[0] + chip[1], part_rows]

        def copy(k, src, dst, to):
            return pltpu.make_async_remote_copy(src_ref=src, dst_ref=dst, send_sem=send_sems.at[k], recv_sem=recv_sems.at[k],
                                                device_id=to, device_id_type=MESH)

        mine = pltpu.make_async_copy(src_ref, out_ref.at[2 * x + y], local_sem)
        mine.start()
        sends = [copy(k, src_ref.at[my_rows], part((x, y), my_rows), (*chip, c)) for k, chip in enumerate(chips)]
        for cp in sends:
            cp.start()
        passed = []
        for k, chip in enumerate(chips):
            landed = part(chip, my_rows)
            copy(k, landed, landed, (*chip, c)).wait_recv()
            passed.append(copy(3 + k, landed, landed, (x, y, 1 - c)))
            passed[k].start()
        for k, chip in enumerate(chips):
            from_sibling = part(chip, sibling_rows)
            copy(3 + k, from_sibling, from_sibling, (x, y, 1 - c)).wait_recv()
        for cp in sends + passed:
            cp.wait_send()
        mine.wait()

    return pl.pallas_call(
        body, name="gather_weight_shards",
        out_shape=jax.ShapeDtypeStruct((N_CHIPS, rows, width), packed.dtype),
        in_specs=[pl.BlockSpec(memory_space=pl.ANY)],
        out_specs=pl.BlockSpec(memory_space=pl.ANY),
        scratch_shapes=[pltpu.SemaphoreType.DMA((6,)), pltpu.SemaphoreType.DMA((6,)), pltpu.SemaphoreType.DMA],
    )(packed)


def _reduce_weight_grads(grads):
    _, rows, width = grads.shape
    half = rows // 2
    n_chunks = half // PACK_CHUNK

    def body(g_ref, out_ref, mine_ref, sibling_ref, send_ref, land_ref, send_sems, recv_sems, local_sem):
        x, y, c = lax.axis_index("x"), lax.axis_index("y"), lax.axis_index("c")
        chips = [(1 - x, y), (x, 1 - y), (1 - x, 1 - y)]
        sibling = (x, y, 1 - c)
        my_rows = pl.ds(pl.multiple_of(c * half, 16), half)
        sibling_rows = pl.ds(pl.multiple_of((1 - c) * half, 16), half)
        all_slots = pl.ds(0, N_CHIPS)

        def copy(k, src, dst, to):
            return pltpu.make_async_remote_copy(src_ref=src, dst_ref=dst, send_sem=send_sems.at[k], recv_sem=recv_sems.at[k],
                                                device_id=to, device_id_type=MESH)

        mine = pltpu.make_async_copy(g_ref.at[all_slots, my_rows], mine_ref, local_sem)
        mine.start()
        swap = copy(0, g_ref.at[all_slots, sibling_rows], sibling_ref, sibling)
        swap.start()
        mine.wait()
        swap.wait_recv()

        def pair_sums(i, carry):
            r = pl.ds(pl.multiple_of(i * PACK_CHUNK, 16), PACK_CHUNK)
            for k, chip in enumerate(chips):
                slot = 2 * chip[0] + chip[1]
                send_ref[k, r, :] = (mine_ref[slot, r, :] + sibling_ref[slot, r, :]).astype(bf16)
            out_rows = pl.ds(pl.multiple_of(c * half + i * PACK_CHUNK, 16), PACK_CHUNK)
            out_ref[out_rows, :] = mine_ref[2 * x + y, r, :] + sibling_ref[2 * x + y, r, :]
            return carry

        lax.fori_loop(0, n_chunks, pair_sums, 0)
        sends = [copy(1 + k, send_ref.at[k], land_ref.at[k], (*chip, c)) for k, chip in enumerate(chips)]
        for cp in sends:
            cp.start()
        for cp in sends:
            cp.wait_recv()

        def add_landed(i, carry):
            r = pl.ds(pl.multiple_of(i * PACK_CHUNK, 16), PACK_CHUNK)
            out_rows = pl.ds(pl.multiple_of(c * half + i * PACK_CHUNK, 16), PACK_CHUNK)
            landed = [land_ref[k, r, :].astype(f32) for k in range(3)]
            out_ref[out_rows, :] = ((out_ref[out_rows, :] + landed[0]) + landed[1]) + landed[2]
            return carry

        lax.fori_loop(0, n_chunks, add_landed, 0)
        finished = copy(4, out_ref.at[my_rows], out_ref.at[my_rows], sibling)
        finished.start()
        copy(4, out_ref.at[sibling_rows], out_ref.at[sibling_rows], sibling).wait_recv()
        for cp in [swap, finished] + sends:
            cp.wait_send()

    return pl.pallas_call(
        body, name="reduce_weight_grads",
        out_shape=jax.ShapeDtypeStruct((rows, width), f32),
        in_specs=[pl.BlockSpec(memory_space=pl.ANY)],
        out_specs=pl.BlockSpec(memory_space=pltpu.VMEM),
        scratch_shapes=[pltpu.VMEM((N_CHIPS, half, width), f32), pltpu.VMEM((N_CHIPS, half, width), f32),
                        pltpu.VMEM((3, half, width), bf16), pltpu.VMEM((3, half, width), bf16),
                        pltpu.SemaphoreType.DMA((5,)), pltpu.SemaphoreType.DMA((5,)), pltpu.SemaphoreType.DMA],
        compiler_params=pltpu.CompilerParams(vmem_limit_bytes=VMEM_LIMIT),
    )(grads)


CONV_SHARD = QKV_W // N_CHIPS


def _pack_shards(w_in_s, w_out_s, conv_rows):
    lead = w_in_s.shape[:-2]

    def widen(a, rows):
        return jnp.pad(a, [(0, 0)] * len(lead) + [(0, rows - a.shape[-2]), (0, PACK_W - a.shape[-1])])

    used = D_MODEL + OUT_ROWS + 8
    return jnp.concatenate([widen(w_in_s, D_MODEL), widen(w_out_s, OUT_ROWS), widen(conv_rows, 8),
                            jnp.zeros((*lead, PACK_ROWS - used, PACK_W), w_in_s.dtype)], axis=-2)


def _unpack_shards(packed):
    return (packed[..., :D_MODEL, :SHARD_IN], packed[..., D_MODEL:D_MODEL + OUT_ROWS, :D_MODEL],
            packed[..., D_MODEL + OUT_ROWS:D_MODEL + OUT_ROWS + CONV_K, :])


def _pack_weights_bf16(w_in_s, w_out_s, conv_s):
    conv_bits = lax.bitcast_convert_type(conv_s, bf16).reshape(CONV_K, 2 * CONV_SHARD)
    return _pack_shards(w_in_s.astype(bf16), w_out_s.astype(bf16), conv_bits)


def _unpack_weights_bf16(shards):
    w_in_s, w_out_s, conv_bits = _unpack_shards(shards)
    conv_s = lax.bitcast_convert_type(conv_bits[..., :2 * CONV_SHARD].reshape(N_CHIPS, CONV_K, CONV_SHARD, 2), f32)
    return (jnp.transpose(w_in_s, (1, 0, 2)).reshape(D_MODEL, IN_COLS), w_out_s.reshape(D_MODEL, D_MODEL),
            jnp.transpose(conv_s, (1, 0, 2)).reshape(CONV_K, QKV_W))


def _local_step(x, target, pos_col, mod_row, norm_w, w_in, conv_w, a_log, dt_bias, dn_norm_w, at_norm_w, w_out, fin_w):
    shift, scale, gate = mod_row[:, :D_MODEL], mod_row[:, D_MODEL:2 * D_MODEL], mod_row[:, 2 * D_MODEL:]
    half = AT_DIM // 2
    lane = jnp.arange(128)
    inv_freq = ROPE_THETA ** (-jnp.arange(half, dtype=f32) / half)
    invf_row = inv_freq[lane % half].reshape(1, 128)
    sign_row = jnp.where((lane % AT_DIM) < half, -1.0, 1.0).astype(f32).reshape(1, 128)
    ba_w = jnp.pad(w_in[:, 2 * D_MODEL:2 * D_MODEL + 2 * DN_HEADS], ((0, 0), (0, BA_W - 2 * DN_HEADS)))
    w_cat = jnp.concatenate([w_in[:, :2 * D_MODEL], ba_w, w_in[:, 2 * D_MODEL + 2 * DN_HEADS:]], axis=1).astype(bf16)
    conv_w8 = jnp.pad(conv_w, ((0, 8 - CONV_K), (0, 0)))
    a_row = jnp.pad(a_log.reshape(1, DN_HEADS), ((0, 0), (DN_HEADS, BA_W - 2 * DN_HEADS)))
    dt_row = jnp.pad(dt_bias.reshape(1, DN_HEADS), ((0, 0), (DN_HEADS, BA_W - 2 * DN_HEADS)))
    dn_w_row = jnp.tile(dn_norm_w.reshape(1, DN_DIM), (1, DN_HEADS))
    at_w_row = jnp.tile(at_norm_w.reshape(1, AT_DIM), (1, AT_HEADS))
    norm_row = norm_w.reshape(1, D_MODEL)
    fin_row = fin_w.reshape(1, D_MODEL)
    w_out_bf = w_out.astype(bf16)

    pdn, *views, z_at, h = _pre_proj(x, pos_col, invf_row, sign_row, norm_row, scale, shift, w_cat)
    *dn_parts, dn_invs = _dn_intra_forward(pdn, conv_w8, a_row, dt_row)
    o_dn, states = _dn_scan_forward(*dn_parts)
    outs, lses = [], []
    for d, view in zip(PATTERN_DILATIONS, views):
        o, lse = _attn_forward(view, d)
        outs.append(o)
        lses.append(lse)
    (dx2, d_odn, d_zdn, *in_views, d_zat, g_wtop, g_wbot, tail_rows) = _tail(
        x, target, o_dn, pdn, outs, lses, z_at, gate, w_out_bf, dn_w_row, at_w_row, fin_row)
    d_oat, delta, lse_all = in_views[0:3], in_views[3:6], in_views[6:9]
    d_q, d_k, d_v = [], [], []
    for n, (d, view) in enumerate(zip(PATTERN_DILATIONS, views)):
        args = (d_oat[n], delta[n], lse_all[n])
        d_q.append(_attn_backward_q(view, *args, d))
        dk, dv = _attn_backward_kv(view, *args, d)
        d_k.append(dk)
        d_v.append(dv)
    d_parts = _dn_scan_backward(*dn_parts, states, d_odn)
    d_conv, d_ba, dn_rows = _dn_intra_backward(pdn, conv_w8, a_row, dt_row, dn_invs, *d_parts)
    grad_x, d_proj, pre_rows, conv_rows = _pre_backward(
        x, dx2, pos_col, invf_row, sign_row, norm_row, scale, shift, w_cat, conv_w8, pdn, d_conv, d_zdn, d_ba,
        d_q, d_k, d_v, d_zat)
    g_cat = _weight_grad(h, d_proj)
    g_w_in = jnp.concatenate([g_cat[:, :2 * D_MODEL], g_cat[:, 2 * D_MODEL:2 * D_MODEL + 2 * DN_HEADS],
                              g_cat[:, 2 * D_MODEL + BA_W:]], axis=1)
    g_w_out = jnp.concatenate([g_wtop, g_wbot], axis=0)
    g_conv = conv_rows[:CONV_K]
    small = jnp.concatenate([
        pre_rows[0:1], tail_rows[2:3], tail_rows[3:4, :DN_WIDTH], tail_rows[4:5, :AT_WIDTH],
        dn_rows[0:1], dn_rows[1:2], tail_rows[0:1, :128],
        pre_rows[2:3], pre_rows[1:2], tail_rows[1:2]], axis=1)
    return grad_x, g_w_in, g_w_out, g_conv, small


def kernel(x, c, positions, w_mod, b_mod, norm_w, w_in, conv_w, a_log, dt_bias, dn_norm_w, at_norm_w, w_out, final_norm_w, loss_target, m_w_mod, m_b_mod, m_norm_w, m_w_in, m_conv_w, m_a_log, m_dt_bias, m_dn_norm_w, m_at_norm_w, m_w_out, m_final_norm_w, v_w_mod, v_b_mod, v_norm_w, v_w_in, v_conv_w, v_a_log, v_dt_bias, v_dn_norm_w, v_at_norm_w, v_w_out, v_final_norm_w):
    seq = x.shape[1]
    ax, ay, ac = lax.axis_index("x"), lax.axis_index("y"), lax.axis_index("c")
    me = 4 * ax + 2 * ay + ac
    chip = 2 * ax + ay

    c_all = _all_gather_rows(jnp.pad(c, ((0, 7), (0, 0))), "gather_c").reshape(N_DEV, 8, D_MODEL)[:, 0]
    b_mod_s = lax.dynamic_slice_in_dim(b_mod, chip * 768, 768, axis=1)
    mod_part = _mod_shard(c_all, w_mod[0], b_mod_s)
    mod_all = _all_gather_rows(mod_part, "gather_mod").reshape(N_CHIPS, 2, N_DEV, 768)[:, 0]
    mod_row = lax.dynamic_index_in_dim(mod_all, me, axis=1, keepdims=False).reshape(1, 3 * D_MODEL)

    shards = _gather_weight_shards(_pack_weights_bf16(w_in[0], w_out[0], conv_w[0]))
    w_in_full, w_out_full, conv_full = _unpack_weights_bf16(shards)

    grad_x, g_w_in, g_w_out, g_conv, small = _local_step(
        x[0], loss_target[0], positions.reshape(seq, 1), mod_row, norm_w, w_in_full, conv_full, a_log, dt_bias,
        dn_norm_w, at_norm_w, w_out_full, final_norm_w)

    g_pack = _pack_shards(jnp.transpose(g_w_in.reshape(D_MODEL, N_CHIPS, SHARD_IN), (1, 0, 2)),
                          g_w_out.reshape(N_CHIPS, OUT_ROWS, D_MODEL),
                          jnp.transpose(g_conv.reshape(CONV_K, N_CHIPS, CONV_SHARD), (1, 0, 2)))
    grad_w_in, grad_w_out, grad_conv_rows = _unpack_shards(_reduce_weight_grads(g_pack))
    grad_conv_w = grad_conv_rows[:, :CONV_SHARD]
    delta_w_in, new_m_w_in, new_v_w_in = _adamw_rows(w_in[0], grad_w_in, m_w_in[0], v_w_in[0], "adamw_w_in")
    delta_w_out, new_m_w_out, new_v_w_out = _adamw_rows(w_out[0], grad_w_out, m_w_out[0], v_w_out[0], "adamw_w_out")
    delta_conv_w, new_m_conv_w, new_v_conv_w = _adamw_rows(conv_w[0], grad_conv_w, m_conv_w[0], v_conv_w[0], "adamw_conv_w")

    gathered = _all_gather_rows(jnp.pad(small, ((0, 7), (0, 0))), "gather_small")

    def small_row(norm, fin, dn, at, a, dt, bmod):
        z = lambda n: jnp.zeros((1, n), f32)
        return jnp.concatenate([norm.reshape(1, -1), fin.reshape(1, -1), dn.reshape(1, -1), at.reshape(1, -1), z(64),
                                z(4), a.reshape(1, -1), z(120), z(4), dt.reshape(1, -1), z(120), bmod.reshape(1, -1)], axis=1)

    g_small, d_small, m_small, v_small, loss_row, d_mod_all = _small_update(
        gathered,
        small_row(norm_w, final_norm_w, dn_norm_w, at_norm_w, a_log, dt_bias, b_mod),
        small_row(m_norm_w, m_final_norm_w, m_dn_norm_w, m_at_norm_w, m_a_log, m_dt_bias, m_b_mod),
        small_row(v_norm_w, v_final_norm_w, v_dn_norm_w, v_at_norm_w, v_a_log, v_dt_bias, v_b_mod))

    def split_small(r):
        return (r[:, RS_BMOD:RS_W], r[:, RS_NORM:RS_FIN], r[:, RS_A + DN_HEADS:RS_A + 2 * DN_HEADS],
                r[:, RS_DT + DN_HEADS:RS_DT + 2 * DN_HEADS], r[:, RS_DN:RS_AT], r[:, RS_AT:RS_AT + AT_DIM],
                r[0, RS_FIN:RS_DN])

    d_mod_s = lax.dynamic_slice_in_dim(d_mod_all, chip * 768, 768, axis=1)
    pad_rows = lambda a: jnp.pad(a, ((0, 128 - N_DEV), (0, 0)))
    grad_w_mod, delta_w_mod, new_m_w_mod, new_v_w_mod = _mod_update(
        pad_rows(c_all), pad_rows(d_mod_s), w_mod[0], m_w_mod[0], v_w_mod[0])

    def ordered(w_mod_leaf, small_row_leaf, w_in_leaf, conv_leaf, w_out_leaf):
        b, n, a, dt, dn, at, fin = split_small(small_row_leaf)
        return [w_mod_leaf[None], b, n, w_in_leaf[None], conv_leaf[None], a, dt, dn, at, w_out_leaf[None], fin]

    loss = loss_row[0, 0]
    return (loss, grad_x[None],
            *ordered(grad_w_mod, g_small, grad_w_in, grad_conv_w, grad_w_out),
            *ordered(delta_w_mod, d_small, delta_w_in, delta_conv_w, delta_w_out),
            *ordered(new_m_w_mod, m_small, new_m_w_in, new_m_conv_w, new_m_w_out),
            *ordered(new_v_w_mod, v_small, new_v_w_in, new_v_conv_w, new_v_w_out))
```

```python
import functools

import jax
import jax.numpy as jnp
from jax import lax
from jax.experimental import pallas as pl
from jax.experimental.pallas import tpu as pltpu

f32 = jnp.float32
bf16 = jnp.bfloat16
HIGHEST = lax.Precision.HIGHEST
MESH = pl.DeviceIdType.MESH

D_MODEL = 1024
DN_HEADS = 4
DN_DIM = 128
DN_WIDTH = 512
AT_HEADS = 8
AT_DIM = 64
AT_WIDTH = 512
CHUNK = 64
Q_BLOCK = 128
CONV_K = 4
EPS = 1e-6
ROPE_THETA = 10000.0
PATTERN_DILATIONS = (1, 4, 16)
NEG = -1e30

QKV_W = 3 * DN_WIDTH
BA_W = 128
PDN_W = QKV_W + DN_WIDTH + BA_W
PAT_W = 3 * AT_WIDTH
CAT_W = PDN_W + PAT_W + AT_WIDTH
IN_COLS = 4104
N_CHIPS = 4
N_DEV = 8
SHARD_IN = IN_COLS // N_CHIPS
OUT_ROWS = D_MODEL // N_CHIPS
PACK_W = 1152
PACK_ROWS = 1312
PACK_CHUNK = 16

ADAM_LR = 0.001
ADAM_B1 = 0.9
ADAM_B2 = 0.999
ADAM_EPS = 1e-08
ADAM_WD = 0.01
ADAM_STEP = 10

VMEM_LIMIT = 56 * 1024 * 1024

NN = ((1,), (0,))
NT = ((1,), (1,))
TN = ((0,), (0,))


def _pieces(a, n):
    out = []
    for _ in range(n - 1):
        p = a.astype(bf16)
        out.append(p)
        a = a - p.astype(f32)
    out.append(a.astype(bf16))
    return out


def _dot(a, b, dims, exact):
    raw = lambda p, q: lax.dot_general(p, q, (dims, ((), ())), preferred_element_type=f32)
    if exact == "split":
        (ah, al), (bh, bl) = _pieces(a, 2), _pieces(b, 2)
        return raw(ah, bh) + (raw(ah, bl) + raw(al, bh))
    if exact:
        return lax.dot_general(a, b, (dims, ((), ())), precision=HIGHEST, preferred_element_type=f32)
    return raw(a.astype(bf16), b.astype(bf16))


def _dot_sel(sel, b, dims, sel_side):
    raw = lambda p, q: lax.dot_general(p, q, (dims, ((), ())), preferred_element_type=f32)
    sel = sel.astype(bf16)
    parts = [raw(sel, p) if sel_side == 0 else raw(p, sel) for p in _pieces(b, 3)]
    return (parts[0] + parts[1]) + parts[2]


@jax.custom_vjp
def _sel_left(sel, b):
    return _dot_sel(sel, b, NN, 0)


def _sel_left_fwd(sel, b):
    return _dot_sel(sel, b, NN, 0), sel


def _sel_left_bwd(sel, g):
    return jnp.zeros_like(sel), _dot_sel(sel, g, TN, 0)


_sel_left.defvjp(_sel_left_fwd, _sel_left_bwd)


@jax.custom_vjp
def _sel_right(a, sel):
    return _dot_sel(sel, a, NN, 1)


def _sel_right_fwd(a, sel):
    return _dot_sel(sel, a, NN, 1), sel


def _sel_right_bwd(sel, g):
    return _dot_sel(sel, g, NT, 1), jnp.zeros_like(sel)


_sel_right.defvjp(_sel_right_fwd, _sel_right_bwd)


class _Matmuls:
    def __init__(self, exact, back):
        @jax.custom_vjp
        def nn(a, b):
            return _dot(a, b, NN, exact)

        def nn_fwd(a, b):
            return _dot(a, b, NN, exact), (a, b)

        def nn_bwd(res, g):
            a, b = res
            return _dot(g, b, NT, back), _dot(a, g, TN, back)

        nn.defvjp(nn_fwd, nn_bwd)

        @jax.custom_vjp
        def nt(a, b):
            return _dot(a, b, NT, exact)

        def nt_fwd(a, b):
            return _dot(a, b, NT, exact), (a, b)

        def nt_bwd(res, g):
            a, b = res
            return _dot(g, b, NN, back), _dot(g, a, TN, back)

        nt.defvjp(nt_fwd, nt_bwd)

        @jax.custom_vjp
        def tn(a, b):
            return _dot(a, b, TN, exact)

        def tn_fwd(a, b):
            return _dot(a, b, TN, exact), (a, b)

        def tn_bwd(res, g):
            a, b = res
            return _dot(b, g, NT, back), _dot(a, g, NN, back)

        tn.defvjp(tn_fwd, tn_bwd)
        self.nn, self.nt, self.tn = nn, nt, tn


MM = _Matmuls(exact=False, back=False)
MS = _Matmuls(exact="split", back=False)


def _each(fn, *lists):
    return [fn(*args) for args in zip(*lists)]


def _inverse_products(a_lows):
    ri = lax.broadcasted_iota(jnp.int32, (CHUNK, CHUNK), 0)
    ci = lax.broadcasted_iota(jnp.int32, (CHUNK, CHUNK), 1)
    eye = (ri == ci).astype(f32)
    power = _each(lambda a: -a, a_lows)
    inv = _each(lambda p: eye + p, power)
    for _ in range(5):
        power = _each(lambda p: _dot(p, p, NN, "split"), power)
        inv = _each(lambda x, p: x + _dot(x, p, NN, "split"), inv, power)
    return inv


def _inverse_cotangents(invs, gs):
    left = _each(lambda t, g: _dot(t, g, TN, "split"), invs, gs)
    return _each(lambda l, t: -_dot(l, t, NT, "split"), left, invs)


@jax.custom_vjp
def _unit_lower_inverses(a_lows):
    return _inverse_products(a_lows)


def _unit_lower_inverses_fwd(a_lows):
    invs = _inverse_products(a_lows)
    return invs, invs


def _unit_lower_inverses_bwd(invs, gs):
    return (_inverse_cotangents(invs, gs),)


_unit_lower_inverses.defvjp(_unit_lower_inverses_fwd, _unit_lower_inverses_bwd)


@jax.custom_vjp
def _known_inverses(a_lows, invs):
    return invs


def _known_inverses_fwd(a_lows, invs):
    return invs, invs


def _known_inverses_bwd(invs, gs):
    return _inverse_cotangents(invs, gs), _each(jnp.zeros_like, invs)


_known_inverses.defvjp(_known_inverses_fwd, _known_inverses_bwd)


def _params(semantics=None):
    return pltpu.CompilerParams(dimension_semantics=semantics, vmem_limit_bytes=VMEM_LIMIT)


def _silu(x):
    return x * jax.nn.sigmoid(x)


def _group_ones(width, group):
    r = lax.broadcasted_iota(jnp.int32, (width, width), 0) // group
    c = lax.broadcasted_iota(jnp.int32, (width, width), 1) // group
    return (r == c).astype(f32)


def _slab_sums(x):
    rows, width = x.shape
    sums = [jnp.sum(x[:, j:j + 128], axis=1, keepdims=True) for j in range(0, width, 128)]
    return jnp.concatenate([jnp.broadcast_to(s, (rows, 128)) for s in sums], axis=1)


def _hnorm(x, nw, scale, shift):
    xn = x * lax.rsqrt(jnp.mean(x * x, axis=-1, keepdims=True) + EPS)
    return xn * nw * (1.0 + scale) + shift


def _rope_tables(pos_col, invf_row, sign_row):
    ang = pos_col.astype(f32) * invf_row
    cos_t = jnp.cos(ang)
    sin_t = jnp.sin(ang) * sign_row
    return jnp.concatenate([cos_t] * 4, axis=1), jnp.concatenate([sin_t] * 4, axis=1)


def _rope_partner(x):
    lane = lax.broadcasted_iota(jnp.int32, x.shape, 1)
    width = x.shape[1]
    return jnp.where((lane % AT_DIM) < AT_DIM // 2, pltpu.roll(x, width - AT_DIM // 2, 1), pltpu.roll(x, AT_DIM // 2, 1))


TOKEN_TILE = 256
LANES = 128


def _stage_lanes(stage_ref, first, value):
    for j in range(value.shape[1] // LANES):
        stage_ref[first + j] = value[:, LANES * j:LANES * (j + 1)]


def _stage_to_view(stage_ref, view_ref, dil):
    chunks, rows, _ = stage_ref.shape
    for r in range(dil):
        for j in range(chunks):
            col = (r * chunks + j) * LANES
            view_ref[:, col:col + LANES] = stage_ref.at[j][pl.ds(r, rows // dil, stride=dil), :].astype(view_ref.dtype)


def _view_to_value(stage_ref, view_ref, dil):
    chunks, rows, _ = stage_ref.shape
    for r in range(dil):
        for j in range(chunks):
            col = (r * chunks + j) * LANES
            stage_ref.at[j][pl.ds(r, rows // dil, stride=dil), :] = view_ref[:, col:col + LANES].astype(f32)
    return jnp.concatenate([stage_ref[j] for j in range(chunks)], axis=1)


def _view_spec(width, dil):
    return pl.BlockSpec((TOKEN_TILE // dil, dil * width), lambda i: (i, 0))


def _view_shape(seq, width, dil, dtype=f32):
    return jax.ShapeDtypeStruct((seq // dil, dil * width), dtype)


HEAD_LANES = 128


def _repeat_heads(compact):
    row = lax.broadcasted_iota(jnp.int32, (HEAD_LANES, AT_WIDTH), 0)
    col = lax.broadcasted_iota(jnp.int32, (HEAD_LANES, AT_WIDTH), 1)
    return _dot_sel((row == col // AT_DIM).astype(f32), compact, NN, 1)


def _stage(width):
    return pltpu.VMEM((width // LANES, TOKEN_TILE, LANES), f32)


def _pre_proj(x, pos_col, invf_row, sign_row, norm_w, scale, shift, w_cat):
    seq = x.shape[0]
    tm = TOKEN_TILE

    def body(x_ref, pos_ref, invf_ref, sign_ref, nw_ref, sc_ref, sh_ref, w_ref,
             pdn_ref, pat1_ref, pat4_ref, pat16_ref, zat_ref, h_ref, stage_ref):
        h = _hnorm(x_ref[...], nw_ref[...], sc_ref[...], sh_ref[...]).astype(bf16)
        h_ref[...] = h
        cols = lambda start, width: jnp.dot(h, w_ref[:, start:start + width], preferred_element_type=f32)
        pdn_ref[...] = cols(0, PDN_W)
        cos_t, sin_t = _rope_tables(pos_ref[...], invf_ref[...], sign_ref[...])
        q = cols(PDN_W, AT_WIDTH)
        k = cols(PDN_W + AT_WIDTH, AT_WIDTH)
        parts = (q * cos_t + _rope_partner(q) * sin_t, k * cos_t + _rope_partner(k) * sin_t,
                 cols(PDN_W + 2 * AT_WIDTH, AT_WIDTH))
        for p, part in enumerate(parts):
            pat1_ref[:, p * AT_WIDTH:(p + 1) * AT_WIDTH] = part.astype(bf16)
            _stage_lanes(stage_ref, p * (AT_WIDTH // LANES), part)
        _stage_to_view(stage_ref, pat4_ref, 4)
        _stage_to_view(stage_ref, pat16_ref, 16)
        zat_ref[...] = cols(PDN_W + PAT_W, AT_WIDTH)

    row = lambda w: pl.BlockSpec((1, w), lambda i: (0, 0))
    tile = lambda w: pl.BlockSpec((tm, w), lambda i: (i, 0))
    return pl.pallas_call(
        body, name="pre_proj", grid=(seq // tm,),
        in_specs=[tile(D_MODEL), tile(1), row(128), row(128), row(D_MODEL), row(D_MODEL), row(D_MODEL),
                  pl.BlockSpec((D_MODEL, CAT_W), lambda i: (0, 0))],
        out_specs=[tile(PDN_W), tile(PAT_W), _view_spec(PAT_W, 4), _view_spec(PAT_W, 16), tile(AT_WIDTH), tile(D_MODEL)],
        out_shape=[jax.ShapeDtypeStruct((seq, PDN_W), f32), _view_shape(seq, PAT_W, 1, bf16), _view_shape(seq, PAT_W, 4, bf16),
                   _view_shape(seq, PAT_W, 16, bf16), jax.ShapeDtypeStruct((seq, AT_WIDTH), f32),
                   jax.ShapeDtypeStruct((seq, D_MODEL), bf16)],
        scratch_shapes=[_stage(PAT_W)],
        compiler_params=_params(("parallel",)),
    )(x, pos_col, invf_row, sign_row, norm_w, scale, shift, w_cat)


def _conv_taps(ext_ref, halo, cur, w8):
    rows = cur.shape[0]
    ext_ref[0:8, :] = halo
    ext_ref[8:8 + rows, :] = cur
    out = ext_ref[pl.ds(5, rows), :] * w8[0:1, :]
    for j in range(1, CONV_K):
        out = out + ext_ref[pl.ds(5 + j, rows), :] * w8[j:j + 1, :]
    return out


def _dn_pre(cq, ck, cv, ba, a_row, dt_row):
    sq, sk, v = _silu(cq), _silu(ck), _silu(cv)
    qn = sq * lax.rsqrt(_slab_sums(sq * sq) + EPS)
    kn = sk * lax.rsqrt(_slab_sums(sk * sk) + EPS)
    beta_all = jax.nn.sigmoid(ba)
    g_all = -jnp.exp(a_row) * jax.nn.softplus(ba + dt_row)
    return qn, kn, v, beta_all, g_all


def _dn_intra(qs, ks, vs, betas, gs, known_invs=None):
    ri = lax.broadcasted_iota(jnp.int32, (CHUNK, CHUNK), 0)
    ci = lax.broadcasted_iota(jnp.int32, (CHUNK, CHUNK), 1)
    tril = ri >= ci
    strict = ri > ci
    lower = tril.astype(f32)
    each = _each
    g_wide = each(lambda g: jnp.broadcast_to(g, (CHUNK, DN_DIM)), gs)
    gc = each(lambda g: _sel_left(lower, g), g_wide)
    gc_sq = each(lambda c: c[:, :CHUNK], gc)
    g_end = each(lambda c: jnp.broadcast_to(c[CHUNK - 1:CHUNK, :], (CHUNK, DN_DIM)), gc)
    g_end8 = each(lambda c: jnp.broadcast_to(c[CHUNK - 1:CHUNK, :], (8, DN_DIM)), gc)
    decay = each(lambda s: jnp.exp(jnp.where(tril, s - s.T, -jnp.inf)), gc_sq)
    qs = each(lambda q: q * (DN_DIM ** -0.5), qs)
    kb = each(lambda k, b: k * b, ks, betas)
    vb = each(lambda v, b: v * b, vs, betas)
    kk = each(MM.nt, kb, ks)
    qk = each(MM.nt, qs, ks)
    a_low = each(lambda p, d: jnp.where(strict, p * d, 0.0), kk, decay)
    inv = _unit_lower_inverses(a_low) if known_invs is None else _known_inverses(a_low, known_invs)
    e_gc = each(jnp.exp, gc)
    u = each(MS.nn, inv, vb)
    w = each(lambda x, k, e: MS.nn(x, k * e), inv, kb, e_gc)
    attn = each(lambda p, d: jnp.where(tril, p * d, 0.0), qk, decay)
    q_dec = each(lambda q, e: q * e, qs, e_gc)
    k_dec = each(lambda k, ge, c: k * jnp.exp(ge - c), ks, g_end, gc)
    return u, w, q_dec, k_dec, attn, each(jnp.exp, g_end8), inv


def _dn_step(us, ws, q_decs, k_decs, attns, e_ends, states):
    each = _each
    v_new = each(lambda u, w, s: u - MM.nn(w, s), us, ws, states)
    qs = each(MM.nn, q_decs, states)
    o = each(lambda a, b, c: a + MM.nn(b, c), qs, attns, v_new)
    new_states = each(lambda s, e, k, v: s * e + MM.tn(k, v), states, e_ends, k_decs, v_new)
    return o, new_states


INTRA_CHUNKS = 4
SCAN_CHUNKS = 8


def _intra_specs(nc):
    rows = INTRA_CHUNKS * CHUNK
    cur = pl.BlockSpec((rows, QKV_W), lambda i: (i, 0))
    halo = pl.BlockSpec((8, QKV_W), lambda i: (jnp.maximum(i * (rows // 8) - 1, 0), 0))
    ba = pl.BlockSpec((rows, BA_W), lambda i: (i, (QKV_W + DN_WIDTH) // BA_W))
    conv = pl.BlockSpec((8, QKV_W), lambda i: (0, 0))
    row = pl.BlockSpec((1, BA_W), lambda i: (0, 0))
    wide = pl.BlockSpec((rows, DN_WIDTH), lambda i: (i, 0))
    attn = pl.BlockSpec((INTRA_CHUNKS, DN_HEADS, CHUNK, CHUNK), lambda i: (i, 0, 0, 0))
    e_end = pl.BlockSpec((INTRA_CHUNKS, 8 * DN_HEADS, DN_DIM), lambda i: (i, 0, 0))
    return cur, halo, ba, conv, row, wide, attn, e_end


def _intra_items():
    return [(ci, h, slice(ci * CHUNK, (ci + 1) * CHUNK), slice(h * DN_DIM, (h + 1) * DN_DIM))
            for ci in range(INTRA_CHUNKS) for h in range(DN_HEADS)]


def _intra_inputs(items, qn, kn, v, beta_all, g_all):
    return ([qn[rows, lanes] for _, _, rows, lanes in items], [kn[rows, lanes] for _, _, rows, lanes in items],
            [v[rows, lanes] for _, _, rows, lanes in items], [beta_all[rows, h:h + 1] for _, h, rows, _ in items],
            [g_all[rows, DN_HEADS + h:DN_HEADS + h + 1] for _, h, rows, _ in items])


def _intra_shapes(seq, operand_dtype):
    nc = seq // CHUNK
    wide = jax.ShapeDtypeStruct((seq, DN_WIDTH), f32)
    operand = jax.ShapeDtypeStruct((seq, DN_WIDTH), operand_dtype)
    return [wide, operand, operand, operand, jax.ShapeDtypeStruct((nc, DN_HEADS, CHUNK, CHUNK), f32),
            jax.ShapeDtypeStruct((nc, 8 * DN_HEADS, DN_DIM), f32)]


def _dn_intra_forward(pdn, conv_w8, a_row, dt_row):
    seq = pdn.shape[0]
    nc = seq // CHUNK
    cur, halo, ba, conv, row, wide, attn, e_end = _intra_specs(nc)

    def body(cur_ref, halo_ref, ba_ref, w_ref, a_ref, dt_ref, u_ref, w_out_ref, qd_ref, kd_ref, attn_ref, e_ref, inv_ref,
             ext_ref):
        halo_rows = jnp.where(pl.program_id(0) > 0, halo_ref[...], 0.0)
        c = _conv_taps(ext_ref, halo_rows, cur_ref[...], w_ref[...])
        qn, kn, v, beta_all, g_all = _dn_pre(c[:, :DN_WIDTH], c[:, DN_WIDTH:2 * DN_WIDTH], c[:, 2 * DN_WIDTH:],
                                             ba_ref[...], a_ref[...], dt_ref[...])
        items = _intra_items()
        u, w, qd, kd, at, e8, inv = _dn_intra(*_intra_inputs(items, qn, kn, v, beta_all, g_all))
        for n, (ci, h, rows, lanes) in enumerate(items):
            u_ref[rows, lanes] = u[n]
            w_out_ref[rows, lanes] = w[n].astype(bf16)
            qd_ref[rows, lanes] = qd[n].astype(bf16)
            kd_ref[rows, lanes] = kd[n].astype(bf16)
            attn_ref[ci, h] = at[n]
            e_ref[ci, 8 * h:8 * h + 8, :] = e8[n]
            inv_ref[ci, h] = inv[n]

    return pl.pallas_call(
        body, name="dn_intra_forward", grid=(nc // INTRA_CHUNKS,),
        in_specs=[cur, halo, ba, conv, row, row],
        out_specs=[wide, wide, wide, wide, attn, e_end, attn],
        out_shape=_intra_shapes(seq, bf16) + [jax.ShapeDtypeStruct((nc, DN_HEADS, CHUNK, CHUNK), f32)],
        scratch_shapes=[pltpu.VMEM((INTRA_CHUNKS * CHUNK + 8, QKV_W), f32)],
        compiler_params=_params(("parallel",)),
    )(pdn, pdn, pdn, conv_w8, a_row, dt_row)


def _scan_specs(nc, reverse):
    steps = nc // SCAN_CHUNKS
    at = (lambda i: steps - 1 - i) if reverse else (lambda i: i)
    wide = pl.BlockSpec((SCAN_CHUNKS * CHUNK, DN_WIDTH), lambda i: (at(i), 0))
    attn = pl.BlockSpec((SCAN_CHUNKS, DN_HEADS, CHUNK, CHUNK), lambda i: (at(i), 0, 0, 0))
    e_end = pl.BlockSpec((SCAN_CHUNKS, 8 * DN_HEADS, DN_DIM), lambda i: (at(i), 0, 0))
    states = pl.BlockSpec((SCAN_CHUNKS, DN_HEADS, DN_DIM, DN_DIM), lambda i: (at(i), 0, 0, 0))
    return wide, attn, e_end, states


def _step_inputs(ci, rows, lanes, u_ref, w_ref, qd_ref, kd_ref, attn_ref, e_ref):
    heads = range(DN_HEADS)
    return ([u_ref[rows, lanes[h]] for h in heads], [w_ref[rows, lanes[h]].astype(f32) for h in heads],
            [qd_ref[rows, lanes[h]].astype(f32) for h in heads], [kd_ref[rows, lanes[h]].astype(f32) for h in heads],
            [attn_ref[ci, h] for h in heads], [e_ref[ci, 8 * h:8 * h + 1, :] for h in heads])


def _dn_scan_forward(u, w, q_dec, k_dec, attn, e_end):
    seq = u.shape[0]
    nc = seq // CHUNK
    wide, attn_spec, e_spec, st_spec = _scan_specs(nc, reverse=False)

    def body(u_ref, w_ref, qd_ref, kd_ref, attn_ref, e_ref, o_ref, st_ref, state_ref):
        @pl.when(pl.program_id(0) == 0)
        def _():
            state_ref[...] = jnp.zeros_like(state_ref)

        heads = range(DN_HEADS)
        lanes = [slice(h * DN_DIM, (h + 1) * DN_DIM) for h in heads]
        states = [state_ref[h] for h in heads]
        for ci in range(SCAN_CHUNKS):
            rows = slice(ci * CHUNK, (ci + 1) * CHUNK)
            for h in heads:
                st_ref[ci, h] = states[h].astype(bf16)
            o, states = _dn_step(*_step_inputs(ci, rows, lanes, u_ref, w_ref, qd_ref, kd_ref, attn_ref, e_ref), states)
            for h in heads:
                o_ref[rows, lanes[h]] = o[h]
        for h in heads:
            state_ref[h] = states[h]

    return pl.pallas_call(
        body, name="dn_scan_forward", grid=(nc // SCAN_CHUNKS,),
        in_specs=[wide, wide, wide, wide, attn_spec, e_spec],
        out_specs=[wide, st_spec],
        out_shape=[jax.ShapeDtypeStruct((seq, DN_WIDTH), f32), jax.ShapeDtypeStruct((nc, DN_HEADS, DN_DIM, DN_DIM), bf16)],
        scratch_shapes=[pltpu.VMEM((DN_HEADS, DN_DIM, DN_DIM), f32)],
        compiler_params=_params(("arbitrary",)),
    )(u, w, q_dec, k_dec, attn, e_end)


def _dn_scan_backward(u, w, q_dec, k_dec, attn, e_end, states, d_o):
    seq = u.shape[0]
    nc = seq // CHUNK
    wide, attn_spec, e_spec, st_spec = _scan_specs(nc, reverse=True)

    def body(u_ref, w_ref, qd_ref, kd_ref, attn_ref, e_ref, st_ref, do_ref,
             du_ref, dw_ref, dqd_ref, dkd_ref, dattn_ref, de_ref, dstate_ref):
        @pl.when(pl.program_id(0) == 0)
        def _():
            dstate_ref[...] = jnp.zeros_like(dstate_ref)

        heads = range(DN_HEADS)
        lanes = [slice(h * DN_DIM, (h + 1) * DN_DIM) for h in heads]
        first_row = lax.broadcasted_iota(jnp.int32, (8, DN_DIM), 0) == 0
        dstates = [dstate_ref[h] for h in heads]
        for ci in reversed(range(SCAN_CHUNKS)):
            rows = slice(ci * CHUNK, (ci + 1) * CHUNK)
            _, step_vjp = jax.vjp(_dn_step, *_step_inputs(ci, rows, lanes, u_ref, w_ref, qd_ref, kd_ref, attn_ref, e_ref),
                                  [st_ref[ci, h].astype(f32) for h in heads])
            du, dw, dqd, dkd, dattn, de, dstates = step_vjp(([do_ref[rows, lanes[h]] for h in heads], dstates))
            for h in heads:
                du_ref[rows, lanes[h]] = du[h]
                dw_ref[rows, lanes[h]] = dw[h]
                dqd_ref[rows, lanes[h]] = dqd[h]
                dkd_ref[rows, lanes[h]] = dkd[h]
                dattn_ref[ci, h] = dattn[h]
                de_ref[ci, 8 * h:8 * h + 8, :] = jnp.where(first_row, jnp.broadcast_to(de[h], (8, DN_DIM)), 0.0)
        for h in heads:
            dstate_ref[h] = dstates[h]

    return pl.pallas_call(
        body, name="dn_scan_backward", grid=(nc // SCAN_CHUNKS,),
        in_specs=[wide, wide, wide, wide, attn_spec, e_spec, st_spec, wide],
        out_specs=[wide, wide, wide, wide, attn_spec, e_spec],
        out_shape=_intra_shapes(seq, f32),
        scratch_shapes=[pltpu.VMEM((DN_HEADS, DN_DIM, DN_DIM), f32)],
        compiler_params=_params(("arbitrary",)),
    )(u, w, q_dec, k_dec, attn, e_end, states, d_o)


def _dn_intra_backward(pdn, conv_w8, a_row, dt_row, invs, d_u, d_w, d_qd, d_kd, d_attn, d_e):
    seq = pdn.shape[0]
    nc = seq // CHUNK
    rows_per_step = INTRA_CHUNKS * CHUNK
    cur, halo, ba, conv, row, wide, attn, e_end = _intra_specs(nc)

    def body(cur_ref, halo_ref, ba_ref, w_ref, a_ref, dt_ref, inv_ref, du_ref, dw_ref, dqd_ref, dkd_ref, dattn_ref, de_ref,
             dconv_ref, dba_ref, drow_ref, ext_ref):
        @pl.when(pl.program_id(0) == 0)
        def _():
            drow_ref[...] = jnp.zeros_like(drow_ref)

        halo_rows = jnp.where(pl.program_id(0) > 0, halo_ref[...], 0.0)
        c = _conv_taps(ext_ref, halo_rows, cur_ref[...], w_ref[...])
        (qn, kn, v, beta_all, g_all), pre_vjp = jax.vjp(
            _dn_pre, c[:, :DN_WIDTH], c[:, DN_WIDTH:2 * DN_WIDTH], c[:, 2 * DN_WIDTH:], ba_ref[...], a_ref[...], dt_ref[...])
        lane = lax.broadcasted_iota(jnp.int32, (CHUNK, BA_W), 1)
        items = _intra_items()
        _, intra_vjp = jax.vjp(_dn_intra, *_intra_inputs(items, qn, kn, v, beta_all, g_all),
                               [inv_ref[ci, h] for ci, h, _, _ in items])
        dq, dk, dv, dbeta, dg, _ = intra_vjp((
            [du_ref[rows, lanes] for _, _, rows, lanes in items], [dw_ref[rows, lanes] for _, _, rows, lanes in items],
            [dqd_ref[rows, lanes] for _, _, rows, lanes in items], [dkd_ref[rows, lanes] for _, _, rows, lanes in items],
            [dattn_ref[ci, h] for ci, h, _, _ in items], [de_ref[ci, 8 * h:8 * h + 8, :] for ci, h, _, _ in items],
            [jnp.zeros((CHUNK, CHUNK), f32) for _ in items]))
        dq_rows, dk_rows, dv_rows, dbeta_rows, dg_rows = [], [], [], [], []
        for ci in range(INTRA_CHUNKS):
            of_chunk = [n for n, item in enumerate(items) if item[0] == ci]
            d_beta_all = jnp.zeros((CHUNK, BA_W), f32)
            d_g_all = jnp.zeros((CHUNK, BA_W), f32)
            for n in of_chunk:
                h = items[n][1]
                d_beta_all = d_beta_all + jnp.where(lane == h, dbeta[n], 0.0)
                d_g_all = d_g_all + jnp.where(lane == DN_HEADS + h, dg[n], 0.0)
            dq_rows.append(jnp.concatenate([dq[n] for n in of_chunk], axis=1))
            dk_rows.append(jnp.concatenate([dk[n] for n in of_chunk], axis=1))
            dv_rows.append(jnp.concatenate([dv[n] for n in of_chunk], axis=1))
            dbeta_rows.append(d_beta_all)
            dg_rows.append(d_g_all)
        stack = lambda parts: jnp.concatenate(parts, axis=0)
        dcq, dck, dcv, dba, da_row, ddt_row = pre_vjp(
            (stack(dq_rows), stack(dk_rows), stack(dv_rows), stack(dbeta_rows), stack(dg_rows)))
        dconv_ref[:, :DN_WIDTH] = dcq
        dconv_ref[:, DN_WIDTH:2 * DN_WIDTH] = dck
        dconv_ref[:, 2 * DN_WIDTH:] = dcv
        dba_ref[...] = dba
        drow_ref[0:1, :] += da_row
        drow_ref[1:2, :] += ddt_row

    return pl.pallas_call(
        body, name="dn_intra_backward", grid=(nc // INTRA_CHUNKS,),
        in_specs=[cur, halo, ba, conv, row, row, attn, wide, wide, wide, wide, attn, e_end],
        out_specs=[pl.BlockSpec((rows_per_step, QKV_W), lambda i: (i, 0)),
                   pl.BlockSpec((rows_per_step, BA_W), lambda i: (i, 0)),
                   pl.BlockSpec((8, BA_W), lambda i: (0, 0))],
        out_shape=[jax.ShapeDtypeStruct((seq, QKV_W), f32), jax.ShapeDtypeStruct((seq, BA_W), f32),
                   jax.ShapeDtypeStruct((8, BA_W), f32)],
        scratch_shapes=[pltpu.VMEM((rows_per_step + 8, QKV_W), f32)],
        compiler_params=_params(("arbitrary",)),
    )(pdn, pdn, pdn, conv_w8, a_row, dt_row, invs, d_u, d_w, d_qd, d_kd, d_attn, d_e)


def _band_masks(block_index):
    qi = lax.broadcasted_iota(jnp.int32, (Q_BLOCK, Q_BLOCK), 0)
    kj = lax.broadcasted_iota(jnp.int32, (Q_BLOCK, Q_BLOCK), 1)
    return (kj >= qi) & (block_index > 0), kj <= qi


def _low_half():
    return lax.broadcasted_iota(jnp.int32, (Q_BLOCK, LANES), 1) < AT_DIM


def _head_pairs(ref, split):
    low = _low_half()
    slabs = [ref[:, pair * LANES:(pair + 1) * LANES].astype(bf16) for pair in range(AT_HEADS // 2)]
    if not split:
        return slabs
    zero = jnp.zeros((Q_BLOCK, LANES), bf16)
    return [jnp.where(low, slab, zero) if h == 0 else jnp.where(low, zero, slab) for slab in slabs for h in range(2)]


def _stack_blocks(first, second):
    return [jnp.concatenate([a, b], axis=0) for a, b in zip(first, second)]


ATTN_BLOCKS = 4


def _two_block_spec(width, pieces, piece, which, n_blocks=None):
    b = ATTN_BLOCKS
    if which == "own":
        return pl.BlockSpec((b * Q_BLOCK, width), lambda r, n: (n, pieces * r + piece))
    if which == "before":
        return pl.BlockSpec((Q_BLOCK, width), lambda r, n: (jnp.maximum(b * n - 1, 0), pieces * r + piece))
    return pl.BlockSpec((Q_BLOCK, width), lambda r, n: (jnp.minimum(b * n + b, n_blocks - 1), pieces * r + piece))


def _block_rows(i):
    return pl.ds(i * Q_BLOCK, Q_BLOCK)


def _attn_forward(pat_view, dil):
    length = pat_view.shape[0]
    nb = length // Q_BLOCK
    scale = AT_DIM ** -0.5

    def body(q2_ref, kp_ref, k2_ref, vp_ref, v2_ref, o2_ref, lse2_ref):
        n = pl.program_id(1)
        for i in range(ATTN_BLOCKS):
            rows, before = _block_rows(i), _block_rows(i - 1)
            one_block(ATTN_BLOCKS * n + i, q2_ref.at[rows], kp_ref if i == 0 else k2_ref.at[before], k2_ref.at[rows],
                      vp_ref if i == 0 else v2_ref.at[before], v2_ref.at[rows], o2_ref.at[rows], lse2_ref.at[rows])

    def one_block(block_index, q_ref, kp_ref, kc_ref, vp_ref, vc_ref, o_ref, lse_ref):
        mask = jnp.concatenate(_band_masks(block_index), axis=1)
        low = _low_half()
        heads = range(AT_HEADS)
        q = _head_pairs(q_ref, split=True)
        k = _stack_blocks(_head_pairs(kp_ref, split=False), _head_pairs(kc_ref, split=False))
        v = _stack_blocks(_head_pairs(vp_ref, split=False), _head_pairs(vc_ref, split=False))
        s = [_dot(q[h], k[h // 2], NT, False) for h in heads]
        p, top = [], []
        for h in heads:
            masked = jnp.where(mask, s[h] * scale, NEG)
            m = jnp.max(masked, axis=1, keepdims=True)
            p.append(jnp.exp(masked - m).astype(bf16))
            top.append(m)
        ones = jnp.ones((2 * Q_BLOCK, LANES), bf16)
        l = [_dot(p[h], ones, NN, False) for h in heads]
        o = [_dot(p[h], v[h // 2], NN, False) for h in heads]
        for pair in range(AT_HEADS // 2):
            even, odd = 2 * pair, 2 * pair + 1
            slab = slice(pair * LANES, (pair + 1) * LANES)
            o_ref[:, slab] = jnp.where(low, o[even], o[odd]) / jnp.where(low, l[even], l[odd])
        lane = lax.broadcasted_iota(jnp.int32, (Q_BLOCK, HEAD_LANES), 1)
        lse = jnp.zeros((Q_BLOCK, HEAD_LANES), f32)
        for h in heads:
            lse = jnp.where(lane == h, top[h] + jnp.log(l[h]), lse)
        lse_ref[...] = lse

    blk = functools.partial(_two_block_spec, AT_WIDTH, 3)
    return pl.pallas_call(
        body, name=f"attn_forward_d{dil}", grid=(dil, nb // ATTN_BLOCKS),
        in_specs=[blk(0, "own"), blk(1, "before"), blk(1, "own"), blk(2, "before"), blk(2, "own")],
        out_specs=[_two_block_spec(AT_WIDTH, 1, 0, "own"), _two_block_spec(HEAD_LANES, 1, 0, "own")],
        out_shape=[jax.ShapeDtypeStruct((length, dil * AT_WIDTH), f32),
                   jax.ShapeDtypeStruct((length, dil * HEAD_LANES), f32)],
        compiler_params=_params(("parallel", "parallel")),
    )(pat_view, pat_view, pat_view, pat_view, pat_view)


def _attn_backward_q(pat_view, d_out, delta, lse, dil):
    length = pat_view.shape[0]
    nb = length // Q_BLOCK
    scale = AT_DIM ** -0.5

    def body(q2_ref, kp_ref, k2_ref, vp_ref, v2_ref, do2_ref, dl2_ref, lse2_ref, dq2_ref):
        n = pl.program_id(1)
        for i in range(ATTN_BLOCKS):
            rows, before = _block_rows(i), _block_rows(i - 1)
            one_block(ATTN_BLOCKS * n + i, q2_ref.at[rows], kp_ref if i == 0 else k2_ref.at[before], k2_ref.at[rows],
                      vp_ref if i == 0 else v2_ref.at[before], v2_ref.at[rows],
                      do2_ref.at[rows], dl2_ref.at[rows], lse2_ref.at[rows], dq2_ref.at[rows])

    def one_block(block_index, q_ref, kp_ref, kc_ref, vp_ref, vc_ref, do_ref, dl_ref, lse_ref, dq_ref):
        mask = jnp.concatenate(_band_masks(block_index), axis=1)
        low = _low_half()
        heads = range(AT_HEADS)
        q, do = _head_pairs(q_ref, split=True), _head_pairs(do_ref, split=True)
        k = _stack_blocks(_head_pairs(kp_ref, split=False), _head_pairs(kc_ref, split=False))
        v = _stack_blocks(_head_pairs(vp_ref, split=False), _head_pairs(vc_ref, split=False))
        s = [_dot(q[h], k[h // 2], NT, False) for h in heads]
        dp = [_dot(do[h], v[h // 2], NT, False) for h in heads]
        ds = []
        for h in heads:
            p = jnp.exp(jnp.where(mask, s[h] * scale - lse_ref[:, h:h + 1], NEG))
            ds.append((p * (dp[h] - dl_ref[:, h:h + 1])).astype(bf16))
        dq = [_dot(ds[h], k[h // 2], NN, False) for h in heads]
        for pair in range(AT_HEADS // 2):
            dq_ref[:, pair * LANES:(pair + 1) * LANES] = (
                jnp.where(low, dq[2 * pair], dq[2 * pair + 1]) * scale).astype(bf16)

    blk = functools.partial(_two_block_spec, AT_WIDTH, 3)
    one = _two_block_spec(AT_WIDTH, 1, 0, "own")
    compact = _two_block_spec(HEAD_LANES, 1, 0, "own")
    return pl.pallas_call(
        body, name=f"attn_backward_q_d{dil}", grid=(dil, nb // ATTN_BLOCKS),
        in_specs=[blk(0, "own"), blk(1, "before"), blk(1, "own"), blk(2, "before"), blk(2, "own"), one, compact, compact],
        out_specs=one,
        out_shape=jax.ShapeDtypeStruct((length, dil * AT_WIDTH), bf16),
        compiler_params=_params(("parallel", "parallel")),
    )(pat_view, pat_view, pat_view, pat_view, pat_view, d_out, delta, lse)


def _attn_backward_kv(pat_view, d_out, delta, lse, dil):
    length = pat_view.shape[0]
    nb = length // Q_BLOCK
    scale = AT_DIM ** -0.5

    def body(k2_ref, v2_ref, q2_ref, qn_ref, do2_ref, don_ref, dl2_ref, dln_ref, lse2_ref, lsen_ref, dk2_ref, dv2_ref):
        n = pl.program_id(1)
        for i in range(ATTN_BLOCKS):
            rows, after = _block_rows(i), _block_rows(i + 1)
            last = i == ATTN_BLOCKS - 1
            one_block(ATTN_BLOCKS * n + i, k2_ref.at[rows], v2_ref.at[rows],
                      q2_ref.at[rows], qn_ref if last else q2_ref.at[after],
                      do2_ref.at[rows], don_ref if last else do2_ref.at[after],
                      dl2_ref.at[rows], dln_ref if last else dl2_ref.at[after],
                      lse2_ref.at[rows], lsen_ref if last else lse2_ref.at[after], dk2_ref.at[rows], dv2_ref.at[rows])

    def one_block(j, k_ref, v_ref, qa_ref, qb_ref, doa_ref, dob_ref, dla_ref, dlb_ref, lsea_ref, lseb_ref, dk_ref, dv_ref):
        kj = lax.broadcasted_iota(jnp.int32, (Q_BLOCK, Q_BLOCK), 0)
        qi = lax.broadcasted_iota(jnp.int32, (Q_BLOCK, Q_BLOCK), 1)
        mask = jnp.concatenate([kj <= qi,
                                (kj >= qi) & (j + 1 < nb)],
                               axis=1)
        low = _low_half()
        heads = range(AT_HEADS)
        k, v = _head_pairs(k_ref, split=True), _head_pairs(v_ref, split=True)
        q = _stack_blocks(_head_pairs(qa_ref, split=False), _head_pairs(qb_ref, split=False))
        do = _stack_blocks(_head_pairs(doa_ref, split=False), _head_pairs(dob_ref, split=False))
        row = lax.broadcasted_iota(jnp.int32, (8 * AT_HEADS, HEAD_LANES), 0)
        col = lax.broadcasted_iota(jnp.int32, (8 * AT_HEADS, HEAD_LANES), 1)
        pick = (col == row // 8).astype(f32)
        lse_rows = _dot_sel(pick, jnp.concatenate([lsea_ref[...], lseb_ref[...]], axis=0), NT, 0)
        dl_rows = _dot_sel(pick, jnp.concatenate([dla_ref[...], dlb_ref[...]], axis=0), NT, 0)
        s_t = [_dot(k[h], q[h // 2], NT, False) for h in heads]
        dp_t = [_dot(v[h], do[h // 2], NT, False) for h in heads]
        p_t, ds_t = [], []
        for h in heads:
            prob = jnp.exp(jnp.where(mask, s_t[h] * scale - lse_rows[8 * h:8 * h + 1, :], NEG))
            p_t.append(prob.astype(bf16))
            ds_t.append((prob * (dp_t[h] - dl_rows[8 * h:8 * h + 1, :])).astype(bf16))
        dv = [_dot(p_t[h], do[h // 2], NN, False) for h in heads]
        dk = [_dot(ds_t[h], q[h // 2], NN, False) for h in heads]
        for pair in range(AT_HEADS // 2):
            slab = slice(pair * LANES, (pair + 1) * LANES)
            dk_ref[:, slab] = (jnp.where(low, dk[2 * pair], dk[2 * pair + 1]) * scale).astype(bf16)
            dv_ref[:, slab] = jnp.where(low, dv[2 * pair], dv[2 * pair + 1]).astype(bf16)

    blk = functools.partial(_two_block_spec, AT_WIDTH, 3)
    same = _two_block_spec(AT_WIDTH, 1, 0, "own")
    nxt = _two_block_spec(AT_WIDTH, 1, 0, "after", nb)
    c_same = _two_block_spec(HEAD_LANES, 1, 0, "own")
    c_nxt = _two_block_spec(HEAD_LANES, 1, 0, "after", nb)
    return pl.pallas_call(
        body, name=f"attn_backward_kv_d{dil}", grid=(dil, nb // ATTN_BLOCKS),
        in_specs=[blk(1, "own"), blk(2, "own"), blk(0, "own"), blk(0, "after", nb), same, nxt, c_same, c_nxt, c_same, c_nxt],
        out_specs=[same, same],
        out_shape=[jax.ShapeDtypeStruct((length, dil * AT_WIDTH), bf16)] * 2,
        compiler_params=_params(("parallel", "parallel")),
    )(pat_view, pat_view, pat_view, pat_view, d_out, d_out, delta, delta, lse, lse)


def _gated_norms(o_dn, z_dn, o_at, z_at, dn_w, at_w):
    ms_dn = _slab_sums(o_dn * o_dn) * (1.0 / DN_DIM)
    a = o_dn * lax.rsqrt(ms_dn + EPS) * dn_w * _silu(z_dn)
    head_of_lane = (lax.broadcasted_iota(jnp.int32, (AT_WIDTH, HEAD_LANES), 0) // AT_DIM
                    == lax.broadcasted_iota(jnp.int32, (AT_WIDTH, HEAD_LANES), 1)).astype(f32)
    lanes_of_head = (lax.broadcasted_iota(jnp.int32, (HEAD_LANES, AT_WIDTH), 0)
                     == lax.broadcasted_iota(jnp.int32, (HEAD_LANES, AT_WIDTH), 1) // AT_DIM).astype(f32)
    ms_at = _sel_right(_sel_right(o_at * o_at, head_of_lane), lanes_of_head) * (1.0 / AT_DIM)
    b = o_at * lax.rsqrt(ms_at + EPS) * at_w * _silu(z_at)
    return a, b


def _residual_loss(x, mix, gate, fin_w, target):
    x2 = x + gate * mix
    y = x2 * lax.rsqrt(jnp.mean(x2 * x2, axis=-1, keepdims=True) + EPS) * fin_w
    err = y - target
    per_token = jnp.sum(err * err, axis=1, keepdims=True) * (1.0 / D_MODEL)
    return 0.5 * jnp.sum(per_token, axis=0, keepdims=True)


TAIL_ROWS = 8


def _tail(x, target, o_dn, pdn, outs, lses, z_at, gate, w_out, dn_w_row, at_w_row, fin_w):
    seq = x.shape[0]
    tm = TOKEN_TILE

    def body(x_ref, t_ref, odn_ref, zdn_ref, o1_ref, o4_ref, o16_ref, l1_ref, l4_ref, l16_ref, zat_ref, gate_ref,
             wt_ref, wb_ref, dnw_ref, atw_ref, fw_ref,
             dx2_ref, dodn_ref, dzdn_ref, doat1_ref, doat4_ref, doat16_ref, delta1_ref, delta4_ref, delta16_ref,
             lse1_ref, lse4_ref, lse16_ref, dzat_ref, gwt_ref, gwb_ref, rows_ref, *stages):
        @pl.when(pl.program_id(0) == 0)
        def _():
            gwt_ref[...] = jnp.zeros_like(gwt_ref)
            gwb_ref[...] = jnp.zeros_like(gwb_ref)
            rows_ref[...] = jnp.zeros_like(rows_ref)

        l1, l4, l16 = l1_ref[...], _view_to_value(stages[0], l4_ref, 4), _view_to_value(stages[1], l16_ref, 16)
        o4, o16 = _view_to_value(stages[2], o4_ref, 4), _view_to_value(stages[3], o16_ref, 16)
        top = jnp.maximum(jnp.maximum(l1, l4), l16)
        e1, e4, e16 = jnp.exp(l1 - top), jnp.exp(l4 - top), jnp.exp(l16 - top)
        den = e1 + e4 + e16
        lse = top + jnp.log(den)
        o_at = (_repeat_heads(e1 / den) * o1_ref[...] + _repeat_heads(e4 / den) * o4 + _repeat_heads(e16 / den) * o16)

        (a, b), norms_vjp = jax.vjp(_gated_norms, odn_ref[...], zdn_ref[...], o_at, zat_ref[...], dnw_ref[...], atw_ref[...])
        a, b = a.astype(bf16), b.astype(bf16)
        mix = jnp.dot(a, wt_ref[...], preferred_element_type=f32) + jnp.dot(b, wb_ref[...], preferred_element_type=f32)
        loss, loss_vjp = jax.vjp(_residual_loss, x_ref[...], mix, gate_ref[...], fw_ref[...], t_ref[...])
        dx2, dmix, dgate, dfw, _ = loss_vjp(jnp.ones((1, 1), f32))
        dmix = dmix.astype(bf16)
        dwt = lax.dot_general(a, dmix, (TN, ((), ())), preferred_element_type=f32)
        dwb = lax.dot_general(b, dmix, (TN, ((), ())), preferred_element_type=f32)
        da = lax.dot_general(dmix, wt_ref[...], (NT, ((), ())), preferred_element_type=f32)
        db = lax.dot_general(dmix, wb_ref[...], (NT, ((), ())), preferred_element_type=f32)
        dodn, dzdn, doat, dzat, ddnw, datw = norms_vjp((da, db))
        dx2_ref[...] = dx2
        dodn_ref[...] = dodn
        dzdn_ref[...] = dzdn
        dzat_ref[...] = dzat
        lane_head = (lax.broadcasted_iota(jnp.int32, (AT_WIDTH, HEAD_LANES), 0) // AT_DIM
                     == lax.broadcasted_iota(jnp.int32, (AT_WIDTH, HEAD_LANES), 1))
        delta = _dot_sel(lane_head.astype(f32), doat * o_at, NN, 1)
        for stage_ref, value, refs in ((stages[2], doat, (doat1_ref, doat4_ref, doat16_ref)),
                                       (stages[0], delta, (delta1_ref, delta4_ref, delta16_ref)),
                                       (stages[1], lse, (lse1_ref, lse4_ref, lse16_ref))):
            refs[0][...] = value.astype(refs[0].dtype)
            _stage_lanes(stage_ref, 0, value)
            _stage_to_view(stage_ref, refs[1], 4)
            _stage_to_view(stage_ref, refs[2], 16)
        gwt_ref[...] += dwt
        gwb_ref[...] += dwb
        rows_ref[0:1, :] += jnp.broadcast_to(loss, (1, D_MODEL))
        rows_ref[1:2, :] += dgate
        rows_ref[2:3, :] += dfw
        rows_ref[3:4, 0:DN_WIDTH] += ddnw
        rows_ref[4:5, 0:AT_WIDTH] += datw

    tile = lambda w: pl.BlockSpec((tm, w), lambda i: (i, 0))
    row = lambda w: pl.BlockSpec((1, w), lambda i: (0, 0))
    half_w = pl.BlockSpec((DN_WIDTH, D_MODEL), lambda i: (0, 0))
    sds = lambda w: jax.ShapeDtypeStruct((seq, w), f32)
    views = [_view_spec(AT_WIDTH, d) for d in PATTERN_DILATIONS]
    compact = [_view_spec(HEAD_LANES, d) for d in PATTERN_DILATIONS]
    doat_shapes = [_view_shape(seq, AT_WIDTH, d, bf16) for d in PATTERN_DILATIONS]
    compact_shapes = [_view_shape(seq, HEAD_LANES, d) for d in PATTERN_DILATIONS]
    return pl.pallas_call(
        body, name="tail", grid=(seq // tm,),
        in_specs=[tile(D_MODEL), tile(D_MODEL), tile(DN_WIDTH),
                  pl.BlockSpec((tm, DN_WIDTH), lambda i: (i, QKV_W // DN_WIDTH)),
                  *views, *compact, tile(AT_WIDTH), row(D_MODEL), half_w, pl.BlockSpec((AT_WIDTH, D_MODEL), lambda i: (1, 0)),
                  row(DN_WIDTH), row(AT_WIDTH), row(D_MODEL)],
        out_specs=[tile(D_MODEL), tile(DN_WIDTH), tile(DN_WIDTH), *views, *compact, *compact,
                   tile(AT_WIDTH), half_w, half_w, pl.BlockSpec((TAIL_ROWS, D_MODEL), lambda i: (0, 0))],
        out_shape=[sds(D_MODEL), sds(DN_WIDTH), sds(DN_WIDTH), *doat_shapes, *compact_shapes, *compact_shapes,
                   sds(AT_WIDTH), jax.ShapeDtypeStruct((DN_WIDTH, D_MODEL), f32),
                   jax.ShapeDtypeStruct((AT_WIDTH, D_MODEL), f32), jax.ShapeDtypeStruct((TAIL_ROWS, D_MODEL), f32)],
        scratch_shapes=[_stage(HEAD_LANES)] * 2 + [_stage(AT_WIDTH)] * 2,
        compiler_params=_params(("arbitrary",)),
    )(x, target, o_dn, pdn, outs[0], outs[1], outs[2], lses[0], lses[1], lses[2], z_at, gate, w_out, w_out,
      dn_w_row, at_w_row, fin_w)


PRE_ROWS = 8


def _pre_backward(x, dx2, pos_col, invf_row, sign_row, norm_w, scale, shift, w_cat, conv_w8, pdn, d_conv, d_zdn, d_ba,
                  d_q, d_k, d_v, d_zat):
    seq = x.shape[0]
    tm = TOKEN_TILE
    last = seq // tm - 1

    def body(x_ref, dx2_ref, pos_ref, invf_ref, sign_ref, nw_ref, sc_ref, sh_ref, w_ref, cw_ref,
             pre_ref, prehalo_ref, dc_ref, dchalo_ref, dz_ref, dba_ref,
             dq1_ref, dq4_ref, dq16_ref, dk1_ref, dk4_ref, dk16_ref, dv1_ref, dv4_ref, dv16_ref, dzat_ref,
             gx_ref, dproj_ref, rows_ref, crow_ref, ext_ref, *stages):
        i = pl.program_id(0)

        @pl.when(i == 0)
        def _():
            rows_ref[...] = jnp.zeros_like(rows_ref)
            crow_ref[...] = jnp.zeros_like(crow_ref)

        dc = dc_ref[...]
        ext_ref[0:tm, :] = dc
        ext_ref[tm:tm + 8, :] = jnp.where(i < last, dchalo_ref[...], 0.0)
        w8 = cw_ref[...]
        d_pre = ext_ref[pl.ds(3, tm), :] * w8[0:1, :]
        for j in range(1, CONV_K):
            d_pre = d_pre + ext_ref[pl.ds(3 - j, tm), :] * w8[j:j + 1, :]
        ext_ref[0:8, :] = jnp.where(i > 0, prehalo_ref[...], 0.0)
        ext_ref[8:8 + tm, :] = pre_ref[...]
        for j in range(CONV_K):
            crow_ref[j:j + 1, :] += jnp.sum(dc * ext_ref[pl.ds(5 + j, tm), :], axis=0, keepdims=True)

        cos_t, sin_t = _rope_tables(pos_ref[...], invf_ref[...], sign_ref[...])
        dq = dq1_ref[...] + _view_to_value(stages[0], dq4_ref, 4) + _view_to_value(stages[1], dq16_ref, 16)
        dk = dk1_ref[...] + _view_to_value(stages[2], dk4_ref, 4) + _view_to_value(stages[3], dk16_ref, 16)
        dq = dq * cos_t + _rope_partner(dq * sin_t)
        dk = dk * cos_t + _rope_partner(dk * sin_t)
        dv = dv1_ref[...] + _view_to_value(stages[4], dv4_ref, 4) + _view_to_value(stages[5], dv16_ref, 16)
        dh = jnp.zeros((tm, D_MODEL), f32)
        start = 0
        for piece in (d_pre, dz_ref[...], dba_ref[...], dq, dk, dv, dzat_ref[...]):
            stop = start + piece.shape[1]
            piece = piece.astype(bf16)
            dproj_ref[:, start:stop] = piece
            dh = dh + lax.dot_general(piece, w_ref[:, start:stop], (NT, ((), ())), preferred_element_type=f32)
            start = stop
        _, vjp = jax.vjp(_hnorm, x_ref[...], nw_ref[...], sc_ref[...], sh_ref[...])
        dx, dnw, dsc, dsh = vjp(dh)
        gx_ref[...] = dx + dx2_ref[...]
        rows_ref[0:1, :] += dnw
        rows_ref[1:2, :] += dsc
        rows_ref[2:3, :] += dsh

    tile = lambda w: pl.BlockSpec((tm, w), lambda i: (i, 0))
    row = lambda w: pl.BlockSpec((1, w), lambda i: (0, 0))
    step8 = tm // 8
    return pl.pallas_call(
        body, name="pre_backward", grid=(seq // tm,),
        in_specs=[tile(D_MODEL), tile(D_MODEL), tile(1), row(128), row(128), row(D_MODEL), row(D_MODEL), row(D_MODEL),
                  pl.BlockSpec((D_MODEL, CAT_W), lambda i: (0, 0)), pl.BlockSpec((8, QKV_W), lambda i: (0, 0)),
                  tile(QKV_W), pl.BlockSpec((8, QKV_W), lambda i: (jnp.maximum(i * step8 - 1, 0), 0)),
                  tile(QKV_W), pl.BlockSpec((8, QKV_W), lambda i: (jnp.minimum((i + 1) * step8, seq // 8 - 1), 0)),
                  tile(DN_WIDTH), tile(BA_W)] + [_view_spec(AT_WIDTH, d) for d in PATTERN_DILATIONS] * 3 + [tile(AT_WIDTH)],
        out_specs=[tile(D_MODEL), tile(CAT_W), pl.BlockSpec((PRE_ROWS, D_MODEL), lambda i: (0, 0)),
                   pl.BlockSpec((8, QKV_W), lambda i: (0, 0))],
        out_shape=[jax.ShapeDtypeStruct((seq, D_MODEL), f32), jax.ShapeDtypeStruct((seq, CAT_W), bf16),
                   jax.ShapeDtypeStruct((PRE_ROWS, D_MODEL), f32), jax.ShapeDtypeStruct((8, QKV_W), f32)],
        scratch_shapes=[pltpu.VMEM((tm + 8, QKV_W), f32)] + [_stage(AT_WIDTH)] * 6,
        compiler_params=_params(("arbitrary",)),
    )(x, dx2, pos_col, invf_row, sign_row, norm_w, scale, shift, w_cat, conv_w8, pdn, pdn, d_conv, d_conv, d_zdn, d_ba,
      d_q[0], d_q[1], d_q[2], d_k[0], d_k[1], d_k[2], d_v[0], d_v[1], d_v[2], d_zat)


def _weight_grad(h, d_proj):
    seq = h.shape[0]
    tk, tn = 512, CAT_W // 3
    n_k = seq // tk

    def body(h_ref, d_ref, o_ref):
        @pl.when(pl.program_id(1) == 0)
        def _():
            o_ref[...] = jnp.zeros_like(o_ref)

        o_ref[...] += lax.dot_general(h_ref[...], d_ref[...], (TN, ((), ())), preferred_element_type=f32)

    return pl.pallas_call(
        body, name="weight_grad", grid=(CAT_W // tn, n_k),
        in_specs=[pl.BlockSpec((tk, D_MODEL), lambda n, k: (k, 0)), pl.BlockSpec((tk, tn), lambda n, k: (k, n))],
        out_specs=pl.BlockSpec((D_MODEL, tn), lambda n, k: (0, n)),
        out_shape=jax.ShapeDtypeStruct((D_MODEL, CAT_W), f32),
        compiler_params=_params(("parallel", "arbitrary")),
    )(h, d_proj)


def _adamw(w, g, m, v):
    m = ADAM_B1 * m + (1.0 - ADAM_B1) * g
    v = ADAM_B2 * v + (1.0 - ADAM_B2) * (g * g)
    m_hat = m / (1.0 - ADAM_B1 ** ADAM_STEP)
    v_hat = v / (1.0 - ADAM_B2 ** ADAM_STEP)
    delta = -ADAM_LR * (m_hat / (jnp.sqrt(v_hat) + ADAM_EPS) + ADAM_WD * w)
    return delta, m, v


def _mod_shard(c_all, w_mod_s, b_mod_s):
    def body(c_ref, w_ref, b_ref, o_ref):
        o_ref[...] = _dot(_silu(c_ref[...]), w_ref[...], NN, True) + b_ref[...]

    return pl.pallas_call(body, name="mod_shard", out_shape=jax.ShapeDtypeStruct((N_DEV, w_mod_s.shape[1]), f32),
                          compiler_params=_params())(c_all, w_mod_s, b_mod_s)


def _mod_update(c_all, d_mod_s, w, m, v):
    def body(c_ref, d_ref, w_ref, m_ref, v_ref, g_ref, dw_ref, nm_ref, nv_ref):
        g = _dot(_silu(c_ref[...]), d_ref[...], TN, True)
        g_ref[...] = g
        dw_ref[...], nm_ref[...], nv_ref[...] = _adamw(w_ref[...], g, m_ref[...], v_ref[...])

    return pl.pallas_call(body, name="mod_update", out_shape=[jax.ShapeDtypeStruct(w.shape, f32)] * 4,
                          compiler_params=_params())(c_all, d_mod_s, w, m, v)


def _adamw_rows(w, g, m, v, name):
    rows, width = w.shape
    tr = 128 if rows % 128 == 0 else rows

    def body(w_ref, g_ref, m_ref, v_ref, dw_ref, nm_ref, nv_ref):
        dw_ref[...], nm_ref[...], nv_ref[...] = _adamw(w_ref[...], g_ref[...], m_ref[...], v_ref[...])

    spec = pl.BlockSpec((tr, width), lambda i: (i, 0))
    return pl.pallas_call(body, name=name, grid=(rows // tr,), in_specs=[spec] * 4, out_specs=[spec] * 3,
                          out_shape=[jax.ShapeDtypeStruct(w.shape, f32)] * 3,
                          compiler_params=_params(("parallel",)))(w, g, m, v)


SM_NORM, SM_FIN, SM_DN, SM_AT, SM_A, SM_DT, SM_LOSS, SM_MOD = 0, 1024, 2048, 2560, 3072, 3200, 3328, 3456
SM_W = SM_MOD + 3 * D_MODEL
RS_NORM, RS_FIN, RS_DN, RS_AT, RS_A, RS_DT, RS_BMOD = 0, 1024, 2048, 2176, 2304, 2432, 2560
RS_W = RS_BMOD + 3 * D_MODEL


def _small_update(gathered, w, m, v):
    def body(g_ref, w_ref, m_ref, v_ref, grad_ref, dw_ref, nm_ref, nv_ref, loss_ref, dmod_ref):
        total = g_ref[0:1, :]
        for dev in range(1, N_DEV):
            total = total + g_ref[8 * dev:8 * dev + 1, :]
        for dev in range(N_DEV):
            dmod_ref[dev:dev + 1, :] = g_ref[8 * dev:8 * dev + 1, SM_MOD:SM_W]
        dn = total[:, SM_DN:SM_DN + DN_DIM]
        for h in range(1, DN_HEADS):
            dn = dn + total[:, SM_DN + h * DN_DIM:SM_DN + (h + 1) * DN_DIM]
        at = total[:, SM_AT:SM_AT + AT_DIM]
        for h in range(1, AT_HEADS):
            at = at + total[:, SM_AT + h * AT_DIM:SM_AT + (h + 1) * AT_DIM]
        grad_ref[:, RS_NORM:RS_FIN] = total[:, SM_NORM:SM_FIN]
        grad_ref[:, RS_FIN:RS_DN] = total[:, SM_FIN:SM_DN]
        grad_ref[:, RS_DN:RS_AT] = dn
        grad_ref[:, RS_AT:RS_A] = jnp.zeros((1, 128), f32)
        grad_ref[:, RS_AT:RS_AT + AT_DIM] = at
        grad_ref[:, RS_A:RS_DT] = total[:, SM_A:SM_DT]
        grad_ref[:, RS_DT:RS_BMOD] = total[:, SM_DT:SM_LOSS]
        grad_ref[:, RS_BMOD:RS_W] = total[:, SM_MOD:SM_W]
        loss_ref[...] = total[:, SM_LOSS:SM_MOD]
        dw_ref[...], nm_ref[...], nv_ref[...] = _adamw(w_ref[...], grad_ref[...], m_ref[...], v_ref[...])

    row = jax.ShapeDtypeStruct((1, RS_W), f32)
    return pl.pallas_call(
        body, name="small_update",
        out_shape=[row, row, row, row, jax.ShapeDtypeStruct((1, 128), f32), jax.ShapeDtypeStruct((N_DEV, 3 * D_MODEL), f32)],
        compiler_params=_params())(gathered, w, m, v)


def _all_gather_rows(block, name):
    m_per, n = block.shape

    def body(x_ref, out_ref, send_sems, recv_sems, local_sem):
        x, y, c = lax.axis_index("x"), lax.axis_index("y"), lax.axis_index("c")
        me, sibling = (x, y, c), (x, y, 1 - c)
        chips = [(1 - x, y), (x, 1 - y), (1 - x, 1 - y)]

        def rows(px, py, pc):
            return out_ref.at[pl.ds((4 * px + 2 * py + pc) * m_per, m_per), :]

        def copy(k, blk, to, src=None):
            return pltpu.make_async_remote_copy(
                src_ref=rows(*blk) if src is None else src, dst_ref=rows(*blk),
                send_sem=send_sems.at[k], recv_sem=recv_sems.at[k], device_id=to, device_id_type=MESH)

        mine = pltpu.make_async_copy(x_ref, rows(*me), local_sem)
        mine.start()
        first = [copy(0, me, sibling, src=x_ref)]
        first += [copy(1 + j, me, (*chip, c), src=x_ref) for j, chip in enumerate(chips)]
        for cp in first:
            cp.start()
        passed = [copy(4 + j, (*chip, c), sibling) for j, chip in enumerate(chips)]
        for j, chip in enumerate(chips):
            copy(1 + j, (*chip, c), me).wait_recv()
            passed[j].start()
        copy(0, sibling, me).wait_recv()
        for j, chip in enumerate(chips):
            copy(4 + j, (*chip, 1 - c), me).wait_recv()
        for cp in first + passed:
            cp.wait_send()
        mine.wait()

    return pl.pallas_call(
        body, name=name,
        out_shape=jax.ShapeDtypeStruct((N_DEV * m_per, n), block.dtype),
        in_specs=[pl.BlockSpec(memory_space=pltpu.VMEM)],
        out_specs=pl.BlockSpec(memory_space=pltpu.VMEM),
        scratch_shapes=[pltpu.SemaphoreType.DMA((7,)), pltpu.SemaphoreType.DMA((7,)), pltpu.SemaphoreType.DMA],
        compiler_params=pltpu.CompilerParams(vmem_limit_bytes=VMEM_LIMIT),
    )(block)


def _gather_weight_shards(packed):
    rows, width = packed.shape
    half = rows // 2

    def body(src_ref, out_ref, send_sems, recv_sems, local_sem):
        x, y, c = lax.axis_index("x"), lax.axis_index("y"), lax.axis_index("c")
        chips = [(1 - x, y), (x, 1 - y), (1 - x, 1 - y)]
        my_rows = pl.ds(pl.multiple_of(c * half, 16), half)
        sibling_rows = pl.ds(pl.multiple_of((1 - c) * half, 16), half)

        def part(chip, part_rows):
            return out_ref.at[2 * chip[0] + chip[1], part_rows]

        def copy(k, src, dst, to):
            return pltpu.make_async_remote_copy(src_ref=src, dst_ref=dst, send_sem=send_sems.at[k], recv_sem=recv_sems.at[k],
                                                device_id=to, device_id_type=MESH)

        mine = pltpu.make_async_copy(src_ref, out_ref.at[2 * x + y], local_sem)
        mine.start()
        sends = [copy(k, src_ref.at[my_rows], part((x, y), my_rows), (*chip, c)) for k, chip in enumerate(chips)]
        for cp in sends:
            cp.start()
        passed = []
        for k, chip in enumerate(chips):
            landed = part(chip, my_rows)
            copy(k, landed, landed, (*chip, c)).wait_recv()
            passed.append(copy(3 + k, landed, landed, (x, y, 1 - c)))
            passed[k].start()
        for k, chip in enumerate(chips):
            from_sibling = part(chip, sibling_rows)
            copy(3 + k, from_sibling, from_sibling, (x, y, 1 - c)).wait_recv()
        for cp in sends + passed:
            cp.wait_send()
        mine.wait()

    return pl.pallas_call(
        body, name="gather_weight_shards",
        out_shape=jax.ShapeDtypeStruct((N_CHIPS, rows, width), packed.dtype),
        in_specs=[pl.BlockSpec(memory_space=pl.ANY)],
        out_specs=pl.BlockSpec(memory_space=pl.ANY),
        scratch_shapes=[pltpu.SemaphoreType.DMA((6,)), pltpu.SemaphoreType.DMA((6,)), pltpu.SemaphoreType.DMA],
    )(packed)


def _reduce_weight_grads(grads):
    _, rows, width = grads.shape
    half = rows // 2
    n_chunks = half // PACK_CHUNK

    def body(g_ref, out_ref, mine_ref, sibling_ref, send_ref, land_ref, send_sems, recv_sems, local_sem):
        x, y, c = lax.axis_index("x"), lax.axis_index("y"), lax.axis_index("c")
        chips = [(1 - x, y), (x, 1 - y), (1 - x, 1 - y)]
        sibling = (x, y, 1 - c)
        my_rows = pl.ds(pl.multiple_of(c * half, 16), half)
        sibling_rows = pl.ds(pl.multiple_of((1 - c) * half, 16), half)
        all_slots = pl.ds(0, N_CHIPS)

        def copy(k, src, dst, to):
            return pltpu.make_async_remote_copy(src_ref=src, dst_ref=dst, send_sem=send_sems.at[k], recv_sem=recv_sems.at[k],
                                                device_id=to, device_id_type=MESH)

        mine = pltpu.make_async_copy(g_ref.at[all_slots, my_rows], mine_ref, local_sem)
        mine.start()
        swap = copy(0, g_ref.at[all_slots, sibling_rows], sibling_ref, sibling)
        swap.start()
        mine.wait()
        swap.wait_recv()

        def pair_sums(i, carry):
            r = pl.ds(pl.multiple_of(i * PACK_CHUNK, 16), PACK_CHUNK)
            for k, chip in enumerate(chips):
                slot = 2 * chip[0] + chip[1]
                send_ref[k, r, :] = (mine_ref[slot, r, :] + sibling_ref[slot, r, :]).astype(bf16)
            out_rows = pl.ds(pl.multiple_of(c * half + i * PACK_CHUNK, 16), PACK_CHUNK)
            out_ref[out_rows, :] = mine_ref[2 * x + y, r, :] + sibling_ref[2 * x + y, r, :]
            return carry

        lax.fori_loop(0, n_chunks, pair_sums, 0)
        sends = [copy(1 + k, send_ref.at[k], land_ref.at[k], (*chip, c)) for k, chip in enumerate(chips)]
        for cp in sends:
            cp.start()
        for cp in sends:
            cp.wait_recv()

        def add_landed(i, carry):
            r = pl.ds(pl.multiple_of(i * PACK_CHUNK, 16), PACK_CHUNK)
            out_rows = pl.ds(pl.multiple_of(c * half + i * PACK_CHUNK, 16), PACK_CHUNK)
            landed = [land_ref[k, r, :].astype(f32) for k in range(3)]
            out_ref[out_rows, :] = ((out_ref[out_rows, :] + landed[0]) + landed[1]) + landed[2]
            return carry

        lax.fori_loop(0, n_chunks, add_landed, 0)
        finished = copy(4, out_ref.at[my_rows], out_ref.at[my_rows], sibling)
        finished.start()
        copy(4, out_ref.at[sibling_rows], out_ref.at[sibling_rows], sibling).wait_recv()
        for cp in [swap, finished] + sends:
            cp.wait_send()

    return pl.pallas_call(
        body, name="reduce_weight_grads",
        out_shape=jax.ShapeDtypeStruct((rows, width), f32),
        in_specs=[pl.BlockSpec(memory_space=pl.ANY)],
        out_specs=pl.BlockSpec(memory_space=pltpu.VMEM),
        scratch_shapes=[pltpu.VMEM((N_CHIPS, half, width), f32), pltpu.VMEM((N_CHIPS, half, width), f32),
                        pltpu.VMEM((3, half, width), bf16), pltpu.VMEM((3, half, width), bf16),
                        pltpu.SemaphoreType.DMA((5,)), pltpu.SemaphoreType.DMA((5,)), pltpu.SemaphoreType.DMA],
        compiler_params=pltpu.CompilerParams(vmem_limit_bytes=VMEM_LIMIT),
    )(grads)


CONV_SHARD = QKV_W // N_CHIPS


def _pack_shards(w_in_s, w_out_s, conv_rows):
    lead = w_in_s.shape[:-2]

    def widen(a, rows):
        return jnp.pad(a, [(0, 0)] * len(lead) + [(0, rows - a.shape[-2]), (0, PACK_W - a.shape[-1])])

    used = D_MODEL + OUT_ROWS + 8
    return jnp.concatenate([widen(w_in_s, D_MODEL), widen(w_out_s, OUT_ROWS), widen(conv_rows, 8),
                            jnp.zeros((*lead, PACK_ROWS - used, PACK_W), w_in_s.dtype)], axis=-2)


def _unpack_shards(packed):
    return (packed[..., :D_MODEL, :SHARD_IN], packed[..., D_MODEL:D_MODEL + OUT_ROWS, :D_MODEL],
            packed[..., D_MODEL + OUT_ROWS:D_MODEL + OUT_ROWS + CONV_K, :])


def _pack_weights_bf16(w_in_s, w_out_s, conv_s):
    conv_bits = lax.bitcast_convert_type(conv_s, bf16).reshape(CONV_K, 2 * CONV_SHARD)
    return _pack_shards(w_in_s.astype(bf16), w_out_s.astype(bf16), conv_bits)


def _unpack_weights_bf16(shards):
    w_in_s, w_out_s, conv_bits = _unpack_shards(shards)
    conv_s = lax.bitcast_convert_type(conv_bits[..., :2 * CONV_SHARD].reshape(N_CHIPS, CONV_K, CONV_SHARD, 2), f32)
    return (jnp.transpose(w_in_s, (1, 0, 2)).reshape(D_MODEL, IN_COLS), w_out_s.reshape(D_MODEL, D_MODEL),
            jnp.transpose(conv_s, (1, 0, 2)).reshape(CONV_K, QKV_W))


def _local_step(x, target, pos_col, mod_row, norm_w, w_in, conv_w, a_log, dt_bias, dn_norm_w, at_norm_w, w_out, fin_w):
    shift, scale, gate = mod_row[:, :D_MODEL], mod_row[:, D_MODEL:2 * D_MODEL], mod_row[:, 2 * D_MODEL:]
    half = AT_DIM // 2
    lane = jnp.arange(128)
    inv_freq = ROPE_THETA ** (-jnp.arange(half, dtype=f32) / half)
    invf_row = inv_freq[lane % half].reshape(1, 128)
    sign_row = jnp.where((lane % AT_DIM) < half, -1.0, 1.0).astype(f32).reshape(1, 128)
    ba_w = jnp.pad(w_in[:, 2 * D_MODEL:2 * D_MODEL + 2 * DN_HEADS], ((0, 0), (0, BA_W - 2 * DN_HEADS)))
    w_cat = jnp.concatenate([w_in[:, :2 * D_MODEL], ba_w, w_in[:, 2 * D_MODEL + 2 * DN_HEADS:]], axis=1).astype(bf16)
    conv_w8 = jnp.pad(conv_w, ((0, 8 - CONV_K), (0, 0)))
    a_row = jnp.pad(a_log.reshape(1, DN_HEADS), ((0, 0), (DN_HEADS, BA_W - 2 * DN_HEADS)))
    dt_row = jnp.pad(dt_bias.reshape(1, DN_HEADS), ((0, 0), (DN_HEADS, BA_W - 2 * DN_HEADS)))
    dn_w_row = jnp.tile(dn_norm_w.reshape(1, DN_DIM), (1, DN_HEADS))
    at_w_row = jnp.tile(at_norm_w.reshape(1, AT_DIM), (1, AT_HEADS))
    norm_row = norm_w.reshape(1, D_MODEL)
    fin_row = fin_w.reshape(1, D_MODEL)
    w_out_bf = w_out.astype(bf16)

    pdn, *views, z_at, h = _pre_proj(x, pos_col, invf_row, sign_row, norm_row, scale, shift, w_cat)
    *dn_parts, dn_invs = _dn_intra_forward(pdn, conv_w8, a_row, dt_row)
    o_dn, states = _dn_scan_forward(*dn_parts)
    outs, lses = [], []
    for d, view in zip(PATTERN_DILATIONS, views):
        o, lse = _attn_forward(view, d)
        outs.append(o)
        lses.append(lse)
    (dx2, d_odn, d_zdn, *in_views, d_zat, g_wtop, g_wbot, tail_rows) = _tail(
        x, target, o_dn, pdn, outs, lses, z_at, gate, w_out_bf, dn_w_row, at_w_row, fin_row)
    d_oat, delta, lse_all = in_views[0:3], in_views[3:6], in_views[6:9]
    d_q, d_k, d_v = [], [], []
    for n, (d, view) in enumerate(zip(PATTERN_DILATIONS, views)):
        args = (d_oat[n], delta[n], lse_all[n])
        d_q.append(_attn_backward_q(view, *args, d))
        dk, dv = _attn_backward_kv(view, *args, d)
        d_k.append(dk)
        d_v.append(dv)
    d_parts = _dn_scan_backward(*dn_parts, states, d_odn)
    d_conv, d_ba, dn_rows = _dn_intra_backward(pdn, conv_w8, a_row, dt_row, dn_invs, *d_parts)
    grad_x, d_proj, pre_rows, conv_rows = _pre_backward(
        x, dx2, pos_col, invf_row, sign_row, norm_row, scale, shift, w_cat, conv_w8, pdn, d_conv, d_zdn, d_ba,
        d_q, d_k, d_v, d_zat)
    g_cat = _weight_grad(h, d_proj)
    g_w_in = jnp.concatenate([g_cat[:, :2 * D_MODEL], g_cat[:, 2 * D_MODEL:2 * D_MODEL + 2 * DN_HEADS],
                              g_cat[:, 2 * D_MODEL + BA_W:]], axis=1)
    g_w_out = jnp.concatenate([g_wtop, g_wbot], axis=0)
    g_conv = conv_rows[:CONV_K]
    small = jnp.concatenate([
        pre_rows[0:1], tail_rows[2:3], tail_rows[3:4, :DN_WIDTH], tail_rows[4:5, :AT_WIDTH],
        dn_rows[0:1], dn_rows[1:2], tail_rows[0:1, :128],
        pre_rows[2:3], pre_rows[1:2], tail_rows[1:2]], axis=1)
    return grad_x, g_w_in, g_w_out, g_conv, small


def kernel(x, c, positions, w_mod, b_mod, norm_w, w_in, conv_w, a_log, dt_bias, dn_norm_w, at_norm_w, w_out, final_norm_w, loss_target, m_w_mod, m_b_mod, m_norm_w, m_w_in, m_conv_w, m_a_log, m_dt_bias, m_dn_norm_w, m_at_norm_w, m_w_out, m_final_norm_w, v_w_mod, v_b_mod, v_norm_w, v_w_in, v_conv_w, v_a_log, v_dt_bias, v_dn_norm_w, v_at_norm_w, v_w_out, v_final_norm_w):
    seq = x.shape[1]
    ax, ay, ac = lax.axis_index("x"), lax.axis_index("y"), lax.axis_index("c")
    me = 4 * ax + 2 * ay + ac
    chip = 2 * ax + ay

    c_all = _all_gather_rows(jnp.pad(c, ((0, 7), (0, 0))), "gather_c").reshape(N_DEV, 8, D_MODEL)[:, 0]
    b_mod_s = lax.dynamic_slice_in_dim(b_mod, chip * 768, 768, axis=1)
    mod_part = _mod_shard(c_all, w_mod[0], b_mod_s)
    mod_all = _all_gather_rows(mod_part, "gather_mod").reshape(N_CHIPS, 2, N_DEV, 768)[:, 0]
    mod_row = lax.dynamic_index_in_dim(mod_all, me, axis=1, keepdims=False).reshape(1, 3 * D_MODEL)

    shards = _gather_weight_shards(_pack_weights_bf16(w_in[0], w_out[0], conv_w[0]))
    w_in_full, w_out_full, conv_full = _unpack_weights_bf16(shards)

    grad_x, g_w_in, g_w_out, g_conv, small = _local_step(
        x[0], loss_target[0], positions.reshape(seq, 1), mod_row, norm_w, w_in_full, conv_full, a_log, dt_bias,
        dn_norm_w, at_norm_w, w_out_full, final_norm_w)

    g_pack = _pack_shards(jnp.transpose(g_w_in.reshape(D_MODEL, N_CHIPS, SHARD_IN), (1, 0, 2)),
                          g_w_out.reshape(N_CHIPS, OUT_ROWS, D_MODEL),
                          jnp.transpose(g_conv.reshape(CONV_K, N_CHIPS, CONV_SHARD), (1, 0, 2)))
    grad_w_in, grad_w_out, grad_conv_rows = _unpack_shards(_reduce_weight_grads(g_pack))
    grad_conv_w = grad_conv_rows[:, :CONV_SHARD]
    delta_w_in, new_m_w_in, new_v_w_in = _adamw_rows(w_in[0], grad_w_in, m_w_in[0], v_w_in[0], "adamw_w_in")
    delta_w_out, new_m_w_out, new_v_w_out = _adamw_rows(w_out[0], grad_w_out, m_w_out[0], v_w_out[0], "adamw_w_out")
    delta_conv_w, new_m_conv_w, new_v_conv_w = _adamw_rows(conv_w[0], grad_conv_w, m_conv_w[0], v_conv_w[0], "adamw_conv_w")

    gathered = _all_gather_rows(jnp.pad(small, ((0, 7), (0, 0))), "gather_small")

    def small_row(norm, fin, dn, at, a, dt, bmod):
        z = lambda n: jnp.zeros((1, n), f32)
        return jnp.concatenate([norm.reshape(1, -1), fin.reshape(1, -1), dn.reshape(1, -1), at.reshape(1, -1), z(64),
                                z(4), a.reshape(1, -1), z(120), z(4), dt.reshape(1, -1), z(120), bmod.reshape(1, -1)], axis=1)

    g_small, d_small, m_small, v_small, loss_row, d_mod_all = _small_update(
        gathered,
        small_row(norm_w, final_norm_w, dn_norm_w, at_norm_w, a_log, dt_bias, b_mod),
        small_row(m_norm_w, m_final_norm_w, m_dn_norm_w, m_at_norm_w, m_a_log, m_dt_bias, m_b_mod),
        small_row(v_norm_w, v_final_norm_w, v_dn_norm_w, v_at_norm_w, v_a_log, v_dt_bias, v_b_mod))

    def split_small(r):
        return (r[:, RS_BMOD:RS_W], r[:, RS_NORM:RS_FIN], r[:, RS_A + DN_HEADS:RS_A + 2 * DN_HEADS],
                r[:, RS_DT + DN_HEADS:RS_DT + 2 * DN_HEADS], r[:, RS_DN:RS_AT], r[:, RS_AT:RS_AT + AT_DIM],
                r[0, RS_FIN:RS_DN])

    d_mod_s = lax.dynamic_slice_in_dim(d_mod_all, chip * 768, 768, axis=1)
    pad_rows = lambda a: jnp.pad(a, ((0, 128 - N_DEV), (0, 0)))
    grad_w_mod, delta_w_mod, new_m_w_mod, new_v_w_mod = _mod_update(
        pad_rows(c_all), pad_rows(d_mod_s), w_mod[0], m_w_mod[0], v_w_mod[0])

    def ordered(w_mod_leaf, small_row_leaf, w_in_leaf, conv_leaf, w_out_leaf):
        b, n, a, dt, dn, at, fin = split_small(small_row_leaf)
        return [w_mod_leaf[None], b, n, w_in_leaf[None], conv_leaf[None], a, dt, dn, at, w_out_leaf[None], fin]

    loss = loss_row[0, 0]
    return (loss, grad_x[None],
            *ordered(grad_w_mod, g_small, grad_w_in, grad_conv_w, grad_w_out),
            *ordered(delta_w_mod, d_small, delta_w_in, delta_conv_w, delta_w_out),
            *ordered(new_m_w_mod, m_small, new_m_w_in, new_m_conv_w, new_m_w_out),
            *ordered(new_v_w_mod, v_small, new_v_w_in, new_v_conv_w, new_v_w_out))
```

```python
import functools

import jax
import jax.numpy as jnp
from jax import lax
from jax.experimental import pallas as pl
from jax.experimental.pallas import tpu as pltpu

f32 = jnp.float32
bf16 = jnp.bfloat16
HIGHEST = lax.Precision.HIGHEST
MESH = pl.DeviceIdType.MESH

D_MODEL = 1024
DN_HEADS = 4
DN_DIM = 128
DN_WIDTH = 512
AT_HEADS = 8
AT_DIM = 64
AT_WIDTH = 512
CHUNK = 64
Q_BLOCK = 128
CONV_K = 4
EPS = 1e-6
ROPE_THETA = 10000.0
PATTERN_DILATIONS = (1, 4, 16)
NEG = -1e30

QKV_W = 3 * DN_WIDTH
BA_W = 128
PDN_W = QKV_W + DN_WIDTH + BA_W
PAT_W = 3 * AT_WIDTH
CAT_W = PDN_W + PAT_W + AT_WIDTH
IN_COLS = 4104
N_CHIPS = 4
N_DEV = 8
SHARD_IN = IN_COLS // N_CHIPS
OUT_ROWS = D_MODEL // N_CHIPS
PACK_W = 1152
PACK_ROWS = 1312
PACK_CHUNK = 16

ADAM_LR = 0.001
ADAM_B1 = 0.9
ADAM_B2 = 0.999
ADAM_EPS = 1e-08
ADAM_WD = 0.01
ADAM_STEP = 10

VMEM_LIMIT = 56 * 1024 * 1024

NN = ((1,), (0,))
NT = ((1,), (1,))
TN = ((0,), (0,))


def _pieces(a, n):
    out = []
    for _ in range(n - 1):
        p = a.astype(bf16)
        out.append(p)
        a = a - p.astype(f32)
    out.append(a.astype(bf16))
    return out


def _dot(a, b, dims, exact):
    raw = lambda p, q: lax.dot_general(p, q, (dims, ((), ())), preferred_element_type=f32)
    if exact == "split":
        (ah, al), (bh, bl) = _pieces(a, 2), _pieces(b, 2)
        return raw(ah, bh) + (raw(ah, bl) + raw(al, bh))
    if exact:
        return lax.dot_general(a, b, (dims, ((), ())), precision=HIGHEST, preferred_element_type=f32)
    return raw(a.astype(bf16), b.astype(bf16))


def _dot_sel(sel, b, dims, sel_side):
    raw = lambda p, q: lax.dot_general(p, q, (dims, ((), ())), preferred_element_type=f32)
    sel = sel.astype(bf16)
    parts = [raw(sel, p) if sel_side == 0 else raw(p, sel) for p in _pieces(b, 3)]
    return (parts[0] + parts[1]) + parts[2]


@jax.custom_vjp
def _sel_left(sel, b):
    return _dot_sel(sel, b, NN, 0)


def _sel_left_fwd(sel, b):
    return _dot_sel(sel, b, NN, 0), sel


def _sel_left_bwd(sel, g):
    return jnp.zeros_like(sel), _dot_sel(sel, g, TN, 0)


_sel_left.defvjp(_sel_left_fwd, _sel_left_bwd)


@jax.custom_vjp
def _sel_right(a, sel):
    return _dot_sel(sel, a, NN, 1)


def _sel_right_fwd(a, sel):
    return _dot_sel(sel, a, NN, 1), sel


def _sel_right_bwd(sel, g):
    return _dot_sel(sel, g, NT, 1), jnp.zeros_like(sel)


_sel_right.defvjp(_sel_right_fwd, _sel_right_bwd)


class _Matmuls:
    def __init__(self, exact, back):
        @jax.custom_vjp
        def nn(a, b):
            return _dot(a, b, NN, exact)

        def nn_fwd(a, b):
            return _dot(a, b, NN, exact), (a, b)

        def nn_bwd(res, g):
            a, b = res
            return _dot(g, b, NT, back), _dot(a, g, TN, back)

        nn.defvjp(nn_fwd, nn_bwd)

        @jax.custom_vjp
        def nt(a, b):
            return _dot(a, b, NT, exact)

        def nt_fwd(a, b):
            return _dot(a, b, NT, exact), (a, b)

        def nt_bwd(res, g):
            a, b = res
            return _dot(g, b, NN, back), _dot(g, a, TN, back)

        nt.defvjp(nt_fwd, nt_bwd)

        @jax.custom_vjp
        def tn(a, b):
            return _dot(a, b, TN, exact)

        def tn_fwd(a, b):
            return _dot(a, b, TN, exact), (a, b)

        def tn_bwd(res, g):
            a, b = res
            return _dot(b, g, NT, back), _dot(a, g, NN, back)

        tn.defvjp(tn_fwd, tn_bwd)
        self.nn, self.nt, self.tn = nn, nt, tn


MM = _Matmuls(exact=False, back=False)
MS = _Matmuls(exact="split", back=False)


def _each(fn, *lists):
    return [fn(*args) for args in zip(*lists)]


def _inverse_products(a_lows):
    ri = lax.broadcasted_iota(jnp.int32, (CHUNK, CHUNK), 0)
    ci = lax.broadcasted_iota(jnp.int32, (CHUNK, CHUNK), 1)
    eye = (ri == ci).astype(f32)
    power = _each(lambda a: -a, a_lows)
    inv = _each(lambda p: eye + p, power)
    for _ in range(5):
        power = _each(lambda p: _dot(p, p, NN, "split"), power)
        inv = _each(lambda x, p: x + _dot(x, p, NN, "split"), inv, power)
    return inv


def _inverse_cotangents(invs, gs):
    left = _each(lambda t, g: _dot(t, g, TN, "split"), invs, gs)
    return _each(lambda l, t: -_dot(l, t, NT, "split"), left, invs)


@jax.custom_vjp
def _unit_lower_inverses(a_lows):
    return _inverse_products(a_lows)


def _unit_lower_inverses_fwd(a_lows):
    invs = _inverse_products(a_lows)
    return invs, invs


def _unit_lower_inverses_bwd(invs, gs):
    return (_inverse_cotangents(invs, gs),)


_unit_lower_inverses.defvjp(_unit_lower_inverses_fwd, _unit_lower_inverses_bwd)


@jax.custom_vjp
def _known_inverses(a_lows, invs):
    return invs


def _known_inverses_fwd(a_lows, invs):
    return invs, invs


def _known_inverses_bwd(invs, gs):
    return _inverse_cotangents(invs, gs), _each(jnp.zeros_like, invs)


_known_inverses.defvjp(_known_inverses_fwd, _known_inverses_bwd)


def _params(semantics=None):
    return pltpu.CompilerParams(dimension_semantics=semantics, vmem_limit_bytes=VMEM_LIMIT)


def _silu(x):
    return x * jax.nn.sigmoid(x)


def _slab_sums(x):
    rows, width = x.shape
    sums = [jnp.sum(x[:, j:j + 128], axis=1, keepdims=True) for j in range(0, width, 128)]
    return jnp.concatenate([jnp.broadcast_to(s, (rows, 128)) for s in sums], axis=1)


def _hnorm(x, nw, scale, shift):
    xn = x * lax.rsqrt(jnp.mean(x * x, axis=-1, keepdims=True) + EPS)
    return xn * nw * (1.0 + scale) + shift


def _rope_tables(pos_col, invf_row, sign_row):
    ang = pos_col.astype(f32) * invf_row
    cos_t = jnp.cos(ang)
    sin_t = jnp.sin(ang) * sign_row
    return jnp.concatenate([cos_t] * 4, axis=1), jnp.concatenate([sin_t] * 4, axis=1)


def _rope_partner(x):
    lane = lax.broadcasted_iota(jnp.int32, x.shape, 1)
    width = x.shape[1]
    return jnp.where((lane % AT_DIM) < AT_DIM // 2, pltpu.roll(x, width - AT_DIM // 2, 1), pltpu.roll(x, AT_DIM // 2, 1))


TOKEN_TILE = 256
LANES = 128


def _stage_lanes(stage_ref, first, value):
    for j in range(value.shape[1] // LANES):
        stage_ref[first + j] = value[:, LANES * j:LANES * (j + 1)]


def _stage_to_view(stage_ref, view_ref, dil):
    chunks, rows, _ = stage_ref.shape
    for r in range(dil):
        for j in range(chunks):
            col = (r * chunks + j) * LANES
            view_ref[:, col:col + LANES] = stage_ref.at[j][pl.ds(r, rows // dil, stride=dil), :].astype(view_ref.dtype)


def _view_to_value(stage_ref, view_ref, dil):
    chunks, rows, _ = stage_ref.shape
    for r in range(dil):
        for j in range(chunks):
            col = (r * chunks + j) * LANES
            stage_ref.at[j][pl.ds(r, rows // dil, stride=dil), :] = view_ref[:, col:col + LANES].astype(f32)
    return jnp.concatenate([stage_ref[j] for j in range(chunks)], axis=1)


def _view_spec(width, dil):
    return pl.BlockSpec((TOKEN_TILE // dil, dil * width), lambda i: (i, 0))


def _view_shape(seq, width, dil, dtype=f32):
    return jax.ShapeDtypeStruct((seq // dil, dil * width), dtype)


HEAD_LANES = 128


def _repeat_heads(compact):
    row = lax.broadcasted_iota(jnp.int32, (HEAD_LANES, AT_WIDTH), 0)
    col = lax.broadcasted_iota(jnp.int32, (HEAD_LANES, AT_WIDTH), 1)
    return _dot_sel((row == col // AT_DIM).astype(f32), compact, NN, 1)


def _stage(width):
    return pltpu.VMEM((width // LANES, TOKEN_TILE, LANES), f32)


def _pre_proj(x, pos_col, invf_row, sign_row, norm_w, scale, shift, w_cat):
    seq = x.shape[0]
    tm = TOKEN_TILE

    def body(x_ref, pos_ref, invf_ref, sign_ref, nw_ref, sc_ref, sh_ref, w_ref,
             pdn_ref, pat1_ref, pat4_ref, pat16_ref, zat_ref, h_ref, stage_ref):
        h = _hnorm(x_ref[...], nw_ref[...], sc_ref[...], sh_ref[...]).astype(bf16)
        h_ref[...] = h
        cols = lambda start, width: jnp.dot(h, w_ref[:, start:start + width], preferred_element_type=f32)
        pdn_ref[...] = cols(0, PDN_W)
        cos_t, sin_t = _rope_tables(pos_ref[...], invf_ref[...], sign_ref[...])
        q = cols(PDN_W, AT_WIDTH)
        k = cols(PDN_W + AT_WIDTH, AT_WIDTH)
        parts = (q * cos_t + _rope_partner(q) * sin_t, k * cos_t + _rope_partner(k) * sin_t,
                 cols(PDN_W + 2 * AT_WIDTH, AT_WIDTH))
        for p, part in enumerate(parts):
            pat1_ref[:, p * AT_WIDTH:(p + 1) * AT_WIDTH] = part.astype(bf16)
            _stage_lanes(stage_ref, p * (AT_WIDTH // LANES), part)
        _stage_to_view(stage_ref, pat4_ref, 4)
        _stage_to_view(stage_ref, pat16_ref, 16)
        zat_ref[...] = cols(PDN_W + PAT_W, AT_WIDTH)

    row = lambda w: pl.BlockSpec((1, w), lambda i: (0, 0))
    tile = lambda w: pl.BlockSpec((tm, w), lambda i: (i, 0))
    return pl.pallas_call(
        body, name="pre_proj", grid=(seq // tm,),
        in_specs=[tile(D_MODEL), tile(1), row(128), row(128), row(D_MODEL), row(D_MODEL), row(D_MODEL),
                  pl.BlockSpec((D_MODEL, CAT_W), lambda i: (0, 0))],
        out_specs=[tile(PDN_W), tile(PAT_W), _view_spec(PAT_W, 4), _view_spec(PAT_W, 16), tile(AT_WIDTH), tile(D_MODEL)],
        out_shape=[jax.ShapeDtypeStruct((seq, PDN_W), f32), _view_shape(seq, PAT_W, 1, bf16), _view_shape(seq, PAT_W, 4, bf16),
                   _view_shape(seq, PAT_W, 16, bf16), jax.ShapeDtypeStruct((seq, AT_WIDTH), f32),
                   jax.ShapeDtypeStruct((seq, D_MODEL), bf16)],
        scratch_shapes=[_stage(PAT_W)],
        compiler_params=_params(("parallel",)),
    )(x, pos_col, invf_row, sign_row, norm_w, scale, shift, w_cat)


def _conv_taps(ext_ref, halo, cur, w8):
    rows = cur.shape[0]
    ext_ref[0:8, :] = halo
    ext_ref[8:8 + rows, :] = cur
    out = ext_ref[pl.ds(5, rows), :] * w8[0:1, :]
    for j in range(1, CONV_K):
        out = out + ext_ref[pl.ds(5 + j, rows), :] * w8[j:j + 1, :]
    return out


def _dn_pre(cq, ck, cv, ba, a_row, dt_row):
    sq, sk, v = _silu(cq), _silu(ck), _silu(cv)
    qn = sq * lax.rsqrt(_slab_sums(sq * sq) + EPS)
    kn = sk * lax.rsqrt(_slab_sums(sk * sk) + EPS)
    beta_all = jax.nn.sigmoid(ba)
    g_all = -jnp.exp(a_row) * jax.nn.softplus(ba + dt_row)
    return qn, kn, v, beta_all, g_all


def _dn_intra(qs, ks, vs, betas, gs, known_invs=None):
    ri = lax.broadcasted_iota(jnp.int32, (CHUNK, CHUNK), 0)
    ci = lax.broadcasted_iota(jnp.int32, (CHUNK, CHUNK), 1)
    tril = ri >= ci
    strict = ri > ci
    lower = tril.astype(f32)
    each = _each
    g_wide = each(lambda g: jnp.broadcast_to(g, (CHUNK, DN_DIM)), gs)
    gc = each(lambda g: _sel_left(lower, g), g_wide)
    gc_sq = each(lambda c: c[:, :CHUNK], gc)
    g_end = each(lambda c: jnp.broadcast_to(c[CHUNK - 1:CHUNK, :], (CHUNK, DN_DIM)), gc)
    g_end8 = each(lambda c: jnp.broadcast_to(c[CHUNK - 1:CHUNK, :], (8, DN_DIM)), gc)
    decay = each(lambda s: jnp.exp(jnp.where(tril, s - s.T, -jnp.inf)), gc_sq)
    qs = each(lambda q: q * (DN_DIM ** -0.5), qs)
    kb = each(lambda k, b: k * b, ks, betas)
    vb = each(lambda v, b: v * b, vs, betas)
    kk = each(MM.nt, kb, ks)
    qk = each(MM.nt, qs, ks)
    a_low = each(lambda p, d: jnp.where(strict, p * d, 0.0), kk, decay)
    inv = _unit_lower_inverses(a_low) if known_invs is None else _known_inverses(a_low, known_invs)
    e_gc = each(jnp.exp, gc)
    u = each(MS.nn, inv, vb)
    w = each(lambda x, k, e: MS.nn(x, k * e), inv, kb, e_gc)
    attn = each(lambda p, d: jnp.where(tril, p * d, 0.0), qk, decay)
    q_dec = each(lambda q, e: q * e, qs, e_gc)
    k_dec = each(lambda k, ge, c: k * jnp.exp(ge - c), ks, g_end, gc)
    return u, w, q_dec, k_dec, attn, each(jnp.exp, g_end8), inv


def _dn_step(us, ws, q_decs, k_decs, attns, e_ends, states):
    each = _each
    v_new = each(lambda u, w, s: u - MM.nn(w, s), us, ws, states)
    qs = each(MM.nn, q_decs, states)
    o = each(lambda a, b, c: a + MM.nn(b, c), qs, attns, v_new)
    new_states = each(lambda s, e, k, v: s * e + MM.tn(k, v), states, e_ends, k_decs, v_new)
    return o, new_states


INTRA_CHUNKS = 4
SCAN_CHUNKS = 8


def _intra_specs(nc):
    rows = INTRA_CHUNKS * CHUNK
    cur = pl.BlockSpec((rows, QKV_W), lambda i: (i, 0))
    halo = pl.BlockSpec((8, QKV_W), lambda i: (jnp.maximum(i * (rows // 8) - 1, 0), 0))
    ba = pl.BlockSpec((rows, BA_W), lambda i: (i, (QKV_W + DN_WIDTH) // BA_W))
    conv = pl.BlockSpec((8, QKV_W), lambda i: (0, 0))
    row = pl.BlockSpec((1, BA_W), lambda i: (0, 0))
    wide = pl.BlockSpec((rows, DN_WIDTH), lambda i: (i, 0))
    attn = pl.BlockSpec((INTRA_CHUNKS, DN_HEADS, CHUNK, CHUNK), lambda i: (i, 0, 0, 0))
    e_end = pl.BlockSpec((INTRA_CHUNKS, 8 * DN_HEADS, DN_DIM), lambda i: (i, 0, 0))
    return cur, halo, ba, conv, row, wide, attn, e_end


def _intra_items():
    return [(ci, h, slice(ci * CHUNK, (ci + 1) * CHUNK), slice(h * DN_DIM, (h + 1) * DN_DIM))
            for ci in range(INTRA_CHUNKS) for h in range(DN_HEADS)]


def _intra_inputs(items, qn, kn, v, beta_all, g_all):
    return ([qn[rows, lanes] for _, _, rows, lanes in items], [kn[rows, lanes] for _, _, rows, lanes in items],
            [v[rows, lanes] for _, _, rows, lanes in items], [beta_all[rows, h:h + 1] for _, h, rows, _ in items],
            [g_all[rows, DN_HEADS + h:DN_HEADS + h + 1] for _, h, rows, _ in items])


def _intra_shapes(seq, operand_dtype):
    nc = seq // CHUNK
    wide = jax.ShapeDtypeStruct((seq, DN_WIDTH), f32)
    operand = jax.ShapeDtypeStruct((seq, DN_WIDTH), operand_dtype)
    return [wide, operand, operand, operand, jax.ShapeDtypeStruct((nc, DN_HEADS, CHUNK, CHUNK), f32),
            jax.ShapeDtypeStruct((nc, 8 * DN_HEADS, DN_DIM), f32)]


def _dn_intra_forward(pdn, conv_w8, a_row, dt_row):
    seq = pdn.shape[0]
    nc = seq // CHUNK
    cur, halo, ba, conv, row, wide, attn, e_end = _intra_specs(nc)

    def body(cur_ref, halo_ref, ba_ref, w_ref, a_ref, dt_ref, u_ref, w_out_ref, qd_ref, kd_ref, attn_ref, e_ref, inv_ref,
             ext_ref):
        halo_rows = jnp.where(pl.program_id(0) > 0, halo_ref[...], 0.0)
        c = _conv_taps(ext_ref, halo_rows, cur_ref[...], w_ref[...])
        qn, kn, v, beta_all, g_all = _dn_pre(c[:, :DN_WIDTH], c[:, DN_WIDTH:2 * DN_WIDTH], c[:, 2 * DN_WIDTH:],
                                             ba_ref[...], a_ref[...], dt_ref[...])
        items = _intra_items()
        u, w, qd, kd, at, e8, inv = _dn_intra(*_intra_inputs(items, qn, kn, v, beta_all, g_all))
        for n, (ci, h, rows, lanes) in enumerate(items):
            u_ref[rows, lanes] = u[n]
            w_out_ref[rows, lanes] = w[n].astype(bf16)
            qd_ref[rows, lanes] = qd[n].astype(bf16)
            kd_ref[rows, lanes] = kd[n].astype(bf16)
            attn_ref[ci, h] = at[n]
            e_ref[ci, 8 * h:8 * h + 8, :] = e8[n]
            inv_ref[ci, h] = inv[n]

    return pl.pallas_call(
        body, name="dn_intra_forward", grid=(nc // INTRA_CHUNKS,),
        in_specs=[cur, halo, ba, conv, row, row],
        out_specs=[wide, wide, wide, wide, attn, e_end, attn],
        out_shape=_intra_shapes(seq, bf16) + [jax.ShapeDtypeStruct((nc, DN_HEADS, CHUNK, CHUNK), f32)],
        scratch_shapes=[pltpu.VMEM((INTRA_CHUNKS * CHUNK + 8, QKV_W), f32)],
        compiler_params=_params(("parallel",)),
    )(pdn, pdn, pdn, conv_w8, a_row, dt_row)


def _scan_specs(nc, reverse):
    steps = nc // SCAN_CHUNKS
    at = (lambda i: steps - 1 - i) if reverse else (lambda i: i)
    wide = pl.BlockSpec((SCAN_CHUNKS * CHUNK, DN_WIDTH), lambda i: (at(i), 0))
    attn = pl.BlockSpec((SCAN_CHUNKS, DN_HEADS, CHUNK, CHUNK), lambda i: (at(i), 0, 0, 0))
    e_end = pl.BlockSpec((SCAN_CHUNKS, 8 * DN_HEADS, DN_DIM), lambda i: (at(i), 0, 0))
    states = pl.BlockSpec((SCAN_CHUNKS, DN_HEADS, DN_DIM, DN_DIM), lambda i: (at(i), 0, 0, 0))
    return wide, attn, e_end, states


def _step_inputs(ci, rows, lanes, u_ref, w_ref, qd_ref, kd_ref, attn_ref, e_ref):
    heads = range(DN_HEADS)
    return ([u_ref[rows, lanes[h]] for h in heads], [w_ref[rows, lanes[h]].astype(f32) for h in heads],
            [qd_ref[rows, lanes[h]].astype(f32) for h in heads], [kd_ref[rows, lanes[h]].astype(f32) for h in heads],
            [attn_ref[ci, h] for h in heads], [e_ref[ci, 8 * h:8 * h + 1, :] for h in heads])


def _dn_scan_forward(u, w, q_dec, k_dec, attn, e_end):
    seq = u.shape[0]
    nc = seq // CHUNK
    wide, attn_spec, e_spec, st_spec = _scan_specs(nc, reverse=False)

    def body(u_ref, w_ref, qd_ref, kd_ref, attn_ref, e_ref, o_ref, st_ref, state_ref):
        @pl.when(pl.program_id(0) == 0)
        def _():
            state_ref[...] = jnp.zeros_like(state_ref)

        heads = range(DN_HEADS)
        lanes = [slice(h * DN_DIM, (h + 1) * DN_DIM) for h in heads]
        states = [state_ref[h] for h in heads]
        for ci in range(SCAN_CHUNKS):
            rows = slice(ci * CHUNK, (ci + 1) * CHUNK)
            for h in heads:
                st_ref[ci, h] = states[h].astype(bf16)
            o, states = _dn_step(*_step_inputs(ci, rows, lanes, u_ref, w_ref, qd_ref, kd_ref, attn_ref, e_ref), states)
            for h in heads:
                o_ref[rows, lanes[h]] = o[h]
        for h in heads:
            state_ref[h] = states[h]

    return pl.pallas_call(
        body, name="dn_scan_forward", grid=(nc // SCAN_CHUNKS,),
        in_specs=[wide, wide, wide, wide, attn_spec, e_spec],
        out_specs=[wide, st_spec],
        out_shape=[jax.ShapeDtypeStruct((seq, DN_WIDTH), f32), jax.ShapeDtypeStruct((nc, DN_HEADS, DN_DIM, DN_DIM), bf16)],
        scratch_shapes=[pltpu.VMEM((DN_HEADS, DN_DIM, DN_DIM), f32)],
        compiler_params=_params(("arbitrary",)),
    )(u, w, q_dec, k_dec, attn, e_end)


def _dn_scan_backward(u, w, q_dec, k_dec, attn, e_end, states, d_o):
    seq = u.shape[0]
    nc = seq // CHUNK
    wide, attn_spec, e_spec, st_spec = _scan_specs(nc, reverse=True)

    def body(u_ref, w_ref, qd_ref, kd_ref, attn_ref, e_ref, st_ref, do_ref,
             du_ref, dw_ref, dqd_ref, dkd_ref, dattn_ref, de_ref, dstate_ref):
        @pl.when(pl.program_id(0) == 0)
        def _():
            dstate_ref[...] = jnp.zeros_like(dstate_ref)

        heads = range(DN_HEADS)
        lanes = [slice(h * DN_DIM, (h + 1) * DN_DIM) for h in heads]
        first_row = lax.broadcasted_iota(jnp.int32, (8, DN_DIM), 0) == 0
        dstates = [dstate_ref[h] for h in heads]
        for ci in reversed(range(SCAN_CHUNKS)):
            rows = slice(ci * CHUNK, (ci + 1) * CHUNK)
            _, step_vjp = jax.vjp(_dn_step, *_step_inputs(ci, rows, lanes, u_ref, w_ref, qd_ref, kd_ref, attn_ref, e_ref),
                                  [st_ref[ci, h].astype(f32) for h in heads])
            du, dw, dqd, dkd, dattn, de, dstates = step_vjp(([do_ref[rows, lanes[h]] for h in heads], dstates))
            for h in heads:
                du_ref[rows, lanes[h]] = du[h]
                dw_ref[rows, lanes[h]] = dw[h]
                dqd_ref[rows, lanes[h]] = dqd[h]
                dkd_ref[rows, lanes[h]] = dkd[h]
                dattn_ref[ci, h] = dattn[h]
                de_ref[ci, 8 * h:8 * h + 8, :] = jnp.where(first_row, jnp.broadcast_to(de[h], (8, DN_DIM)), 0.0)
        for h in heads:
            dstate_ref[h] = dstates[h]

    return pl.pallas_call(
        body, name="dn_scan_backward", grid=(nc // SCAN_CHUNKS,),
        in_specs=[wide, wide, wide, wide, attn_spec, e_spec, st_spec, wide],
        out_specs=[wide, wide, wide, wide, attn_spec, e_spec],
        out_shape=_intra_shapes(seq, f32),
        scratch_shapes=[pltpu.VMEM((DN_HEADS, DN_DIM, DN_DIM), f32)],
        compiler_params=_params(("arbitrary",)),
    )(u, w, q_dec, k_dec, attn, e_end, states, d_o)


def _dn_intra_backward(pdn, conv_w8, a_row, dt_row, invs, d_u, d_w, d_qd, d_kd, d_attn, d_e):
    seq = pdn.shape[0]
    nc = seq // CHUNK
    rows_per_step = INTRA_CHUNKS * CHUNK
    cur, halo, ba, conv, row, wide, attn, e_end = _intra_specs(nc)

    def body(cur_ref, halo_ref, ba_ref, w_ref, a_ref, dt_ref, inv_ref, du_ref, dw_ref, dqd_ref, dkd_ref, dattn_ref, de_ref,
             dconv_ref, dba_ref, drow_ref, ext_ref):
        @pl.when(pl.program_id(0) == 0)
        def _():
            drow_ref[...] = jnp.zeros_like(drow_ref)

        halo_rows = jnp.where(pl.program_id(0) > 0, halo_ref[...], 0.0)
        c = _conv_taps(ext_ref, halo_rows, cur_ref[...], w_ref[...])
        (qn, kn, v, beta_all, g_all), pre_vjp = jax.vjp(
            _dn_pre, c[:, :DN_WIDTH], c[:, DN_WIDTH:2 * DN_WIDTH], c[:, 2 * DN_WIDTH:], ba_ref[...], a_ref[...], dt_ref[...])
        lane = lax.broadcasted_iota(jnp.int32, (CHUNK, BA_W), 1)
        items = _intra_items()
        _, intra_vjp = jax.vjp(_dn_intra, *_intra_inputs(items, qn, kn, v, beta_all, g_all),
                               [inv_ref[ci, h] for ci, h, _, _ in items])
        dq, dk, dv, dbeta, dg, _ = intra_vjp((
            [du_ref[rows, lanes] for _, _, rows, lanes in items], [dw_ref[rows, lanes] for _, _, rows, lanes in items],
            [dqd_ref[rows, lanes] for _, _, rows, lanes in items], [dkd_ref[rows, lanes] for _, _, rows, lanes in items],
            [dattn_ref[ci, h] for ci, h, _, _ in items], [de_ref[ci, 8 * h:8 * h + 8, :] for ci, h, _, _ in items],
            [jnp.zeros((CHUNK, CHUNK), f32) for _ in items]))
        dq_rows, dk_rows, dv_rows, dbeta_rows, dg_rows = [], [], [], [], []
        for ci in range(INTRA_CHUNKS):
            of_chunk = [n for n, item in enumerate(items) if item[0] == ci]
            d_beta_all = jnp.zeros((CHUNK, BA_W), f32)
            d_g_all = jnp.zeros((CHUNK, BA_W), f32)
            for n in of_chunk:
                h = items[n][1]
                d_beta_all = d_beta_all + jnp.where(lane == h, dbeta[n], 0.0)
                d_g_all = d_g_all + jnp.where(lane == DN_HEADS + h, dg[n], 0.0)
            dq_rows.append(jnp.concatenate([dq[n] for n in of_chunk], axis=1))
            dk_rows.append(jnp.concatenate([dk[n] for n in of_chunk], axis=1))
            dv_rows.append(jnp.concatenate([dv[n] for n in of_chunk], axis=1))
            dbeta_rows.append(d_beta_all)
            dg_rows.append(d_g_all)
        stack = lambda parts: jnp.concatenate(parts, axis=0)
        dcq, dck, dcv, dba, da_row, ddt_row = pre_vjp(
            (stack(dq_rows), stack(dk_rows), stack(dv_rows), stack(dbeta_rows), stack(dg_rows)))
        dconv_ref[:, :DN_WIDTH] = dcq
        dconv_ref[:, DN_WIDTH:2 * DN_WIDTH] = dck
        dconv_ref[:, 2 * DN_WIDTH:] = dcv
        dba_ref[...] = dba
        drow_ref[0:1, :] += da_row
        drow_ref[1:2, :] += ddt_row

    return pl.pallas_call(
        body, name="dn_intra_backward", grid=(nc // INTRA_CHUNKS,),
        in_specs=[cur, halo, ba, conv, row, row, attn, wide, wide, wide, wide, attn, e_end],
        out_specs=[pl.BlockSpec((rows_per_step, QKV_W), lambda i: (i, 0)),
                   pl.BlockSpec((rows_per_step, BA_W), lambda i: (i, 0)),
                   pl.BlockSpec((8, BA_W), lambda i: (0, 0))],
        out_shape=[jax.ShapeDtypeStruct((seq, QKV_W), f32), jax.ShapeDtypeStruct((seq, BA_W), f32),
                   jax.ShapeDtypeStruct((8, BA_W), f32)],
        scratch_shapes=[pltpu.VMEM((rows_per_step + 8, QKV_W), f32)],
        compiler_params=_params(("arbitrary",)),
    )(pdn, pdn, pdn, conv_w8, a_row, dt_row, invs, d_u, d_w, d_qd, d_kd, d_attn, d_e)


def _band_masks(block_index):
    qi = lax.broadcasted_iota(jnp.int32, (Q_BLOCK, Q_BLOCK), 0)
    kj = lax.broadcasted_iota(jnp.int32, (Q_BLOCK, Q_BLOCK), 1)
    return (kj >= qi) & (block_index > 0), kj <= qi


def _low_half():
    return lax.broadcasted_iota(jnp.int32, (Q_BLOCK, LANES), 1) < AT_DIM


def _head_pairs(ref, split):
    low = _low_half()
    slabs = [ref[:, pair * LANES:(pair + 1) * LANES].astype(bf16) for pair in range(AT_HEADS // 2)]
    if not split:
        return slabs
    zero = jnp.zeros((Q_BLOCK, LANES), bf16)
    return [jnp.where(low, slab, zero) if h == 0 else jnp.where(low, zero, slab) for slab in slabs for h in range(2)]


def _stack_blocks(first, second):
    return [jnp.concatenate([a, b], axis=0) for a, b in zip(first, second)]


ATTN_BLOCKS = 8


def _two_block_spec(width, pieces, piece, which, n_blocks=None):
    b = ATTN_BLOCKS
    if which == "own":
        return pl.BlockSpec((b * Q_BLOCK, width), lambda r, n: (n, pieces * r + piece))
    if which == "before":
        return pl.BlockSpec((Q_BLOCK, width), lambda r, n: (jnp.maximum(b * n - 1, 0), pieces * r + piece))
    return pl.BlockSpec((Q_BLOCK, width), lambda r, n: (jnp.minimum(b * n + b, n_blocks - 1), pieces * r + piece))


def _block_rows(i):
    return pl.ds(i * Q_BLOCK, Q_BLOCK)


def _attn_forward(pat_view, dil):
    length = pat_view.shape[0]
    nb = length // Q_BLOCK
    scale = AT_DIM ** -0.5

    def body(q2_ref, kp_ref, k2_ref, vp_ref, v2_ref, o2_ref, lse2_ref):
        n = pl.program_id(1)
        for i in range(ATTN_BLOCKS):
            rows, before = _block_rows(i), _block_rows(i - 1)
            one_block(ATTN_BLOCKS * n + i, q2_ref.at[rows], kp_ref if i == 0 else k2_ref.at[before], k2_ref.at[rows],
                      vp_ref if i == 0 else v2_ref.at[before], v2_ref.at[rows], o2_ref.at[rows], lse2_ref.at[rows])

    def one_block(block_index, q_ref, kp_ref, kc_ref, vp_ref, vc_ref, o_ref, lse_ref):
        mask = jnp.concatenate(_band_masks(block_index), axis=1)
        low = _low_half()
        heads = range(AT_HEADS)
        q = _head_pairs(q_ref, split=True)
        k = _stack_blocks(_head_pairs(kp_ref, split=False), _head_pairs(kc_ref, split=False))
        v = _stack_blocks(_head_pairs(vp_ref, split=False), _head_pairs(vc_ref, split=False))
        s = [_dot(q[h], k[h // 2], NT, False) for h in heads]
        p, top = [], []
        for h in heads:
            masked = jnp.where(mask, s[h] * scale, NEG)
            m = jnp.max(masked, axis=1, keepdims=True)
            p.append(jnp.exp(masked - m).astype(bf16))
            top.append(m)
        ones = jnp.ones((2 * Q_BLOCK, LANES), bf16)
        l = [_dot(p[h], ones, NN, False) for h in heads]
        o = [_dot(p[h], v[h // 2], NN, False) for h in heads]
        for pair in range(AT_HEADS // 2):
            even, odd = 2 * pair, 2 * pair + 1
            slab = slice(pair * LANES, (pair + 1) * LANES)
            o_ref[:, slab] = jnp.where(low, o[even], o[odd]) / jnp.where(low, l[even], l[odd])
        lane = lax.broadcasted_iota(jnp.int32, (Q_BLOCK, HEAD_LANES), 1)
        lse = jnp.zeros((Q_BLOCK, HEAD_LANES), f32)
        for h in heads:
            lse = jnp.where(lane == h, top[h] + jnp.log(l[h]), lse)
        lse_ref[...] = lse

    blk = functools.partial(_two_block_spec, AT_WIDTH, 3)
    return pl.pallas_call(
        body, name=f"attn_forward_d{dil}", grid=(dil, nb // ATTN_BLOCKS),
        in_specs=[blk(0, "own"), blk(1, "before"), blk(1, "own"), blk(2, "before"), blk(2, "own")],
        out_specs=[_two_block_spec(AT_WIDTH, 1, 0, "own"), _two_block_spec(HEAD_LANES, 1, 0, "own")],
        out_shape=[jax.ShapeDtypeStruct((length, dil * AT_WIDTH), f32),
                   jax.ShapeDtypeStruct((length, dil * HEAD_LANES), f32)],
        compiler_params=_params(("parallel", "parallel")),
    )(pat_view, pat_view, pat_view, pat_view, pat_view)


def _attn_backward_q(pat_view, d_out, delta, lse, dil):
    length = pat_view.shape[0]
    nb = length // Q_BLOCK
    scale = AT_DIM ** -0.5

    def body(q2_ref, kp_ref, k2_ref, vp_ref, v2_ref, do2_ref, dl2_ref, lse2_ref, dq2_ref):
        n = pl.program_id(1)
        for i in range(ATTN_BLOCKS):
            rows, before = _block_rows(i), _block_rows(i - 1)
            one_block(ATTN_BLOCKS * n + i, q2_ref.at[rows], kp_ref if i == 0 else k2_ref.at[before], k2_ref.at[rows],
                      vp_ref if i == 0 else v2_ref.at[before], v2_ref.at[rows],
                      do2_ref.at[rows], dl2_ref.at[rows], lse2_ref.at[rows], dq2_ref.at[rows])

    def one_block(block_index, q_ref, kp_ref, kc_ref, vp_ref, vc_ref, do_ref, dl_ref, lse_ref, dq_ref):
        mask = jnp.concatenate(_band_masks(block_index), axis=1)
        low = _low_half()
        heads = range(AT_HEADS)
        q, do = _head_pairs(q_ref, split=True), _head_pairs(do_ref, split=True)
        k = _stack_blocks(_head_pairs(kp_ref, split=False), _head_pairs(kc_ref, split=False))
        v = _stack_blocks(_head_pairs(vp_ref, split=False), _head_pairs(vc_ref, split=False))
        s = [_dot(q[h], k[h // 2], NT, False) for h in heads]
        dp = [_dot(do[h], v[h // 2], NT, False) for h in heads]
        ds = []
        for h in heads:
            p = jnp.exp(jnp.where(mask, s[h] * scale - lse_ref[:, h:h + 1], NEG))
            ds.append((p * (dp[h] - dl_ref[:, h:h + 1])).astype(bf16))
        dq = [_dot(ds[h], k[h // 2], NN, False) for h in heads]
        for pair in range(AT_HEADS // 2):
            dq_ref[:, pair * LANES:(pair + 1) * LANES] = (
                jnp.where(low, dq[2 * pair], dq[2 * pair + 1]) * scale).astype(bf16)

    blk = functools.partial(_two_block_spec, AT_WIDTH, 3)
    one = _two_block_spec(AT_WIDTH, 1, 0, "own")
    compact = _two_block_spec(HEAD_LANES, 1, 0, "own")
    return pl.pallas_call(
        body, name=f"attn_backward_q_d{dil}", grid=(dil, nb // ATTN_BLOCKS),
        in_specs=[blk(0, "own"), blk(1, "before"), blk(1, "own"), blk(2, "before"), blk(2, "own"), one, compact, compact],
        out_specs=one,
        out_shape=jax.ShapeDtypeStruct((length, dil * AT_WIDTH), bf16),
        compiler_params=_params(("parallel", "parallel")),
    )(pat_view, pat_view, pat_view, pat_view, pat_view, d_out, delta, lse)


def _attn_backward_kv(pat_view, d_out, delta, lse, dil):
    length = pat_view.shape[0]
    nb = length // Q_BLOCK
    scale = AT_DIM ** -0.5

    def body(k2_ref, v2_ref, q2_ref, qn_ref, do2_ref, don_ref, dl2_ref, dln_ref, lse2_ref, lsen_ref, dk2_ref, dv2_ref):
        n = pl.program_id(1)
        for i in range(ATTN_BLOCKS):
            rows, after = _block_rows(i), _block_rows(i + 1)
            last = i == ATTN_BLOCKS - 1
            one_block(ATTN_BLOCKS * n + i, k2_ref.at[rows], v2_ref.at[rows],
                      q2_ref.at[rows], qn_ref if last else q2_ref.at[after],
                      do2_ref.at[rows], don_ref if last else do2_ref.at[after],
                      dl2_ref.at[rows], dln_ref if last else dl2_ref.at[after],
                      lse2_ref.at[rows], lsen_ref if last else lse2_ref.at[after], dk2_ref.at[rows], dv2_ref.at[rows])

    def one_block(j, k_ref, v_ref, qa_ref, qb_ref, doa_ref, dob_ref, dla_ref, dlb_ref, lsea_ref, lseb_ref, dk_ref, dv_ref):
        kj = lax.broadcasted_iota(jnp.int32, (Q_BLOCK, Q_BLOCK), 0)
        qi = lax.broadcasted_iota(jnp.int32, (Q_BLOCK, Q_BLOCK), 1)
        mask = jnp.concatenate([kj <= qi,
                                (kj >= qi) & (j + 1 < nb)],
                               axis=1)
        low = _low_half()
        heads = range(AT_HEADS)
        k, v = _head_pairs(k_ref, split=True), _head_pairs(v_ref, split=True)
        q = _stack_blocks(_head_pairs(qa_ref, split=False), _head_pairs(qb_ref, split=False))
        do = _stack_blocks(_head_pairs(doa_ref, split=False), _head_pairs(dob_ref, split=False))
        row = lax.broadcasted_iota(jnp.int32, (8 * AT_HEADS, HEAD_LANES), 0)
        col = lax.broadcasted_iota(jnp.int32, (8 * AT_HEADS, HEAD_LANES), 1)
        pick = (col == row // 8).astype(f32)
        lse_rows = _dot_sel(pick, jnp.concatenate([lsea_ref[...], lseb_ref[...]], axis=0), NT, 0)
        dl_rows = _dot_sel(pick, jnp.concatenate([dla_ref[...], dlb_ref[...]], axis=0), NT, 0)
        s_t = [_dot(k[h], q[h // 2], NT, False) for h in heads]
        dp_t = [_dot(v[h], do[h // 2], NT, False) for h in heads]
        p_t, ds_t = [], []
        for h in heads:
            prob = jnp.exp(jnp.where(mask, s_t[h] * scale - lse_rows[8 * h:8 * h + 1, :], NEG))
            p_t.append(prob.astype(bf16))
            ds_t.append((prob * (dp_t[h] - dl_rows[8 * h:8 * h + 1, :])).astype(bf16))
        dv = [_dot(p_t[h], do[h // 2], NN, False) for h in heads]
        dk = [_dot(ds_t[h], q[h // 2], NN, False) for h in heads]
        for pair in range(AT_HEADS // 2):
            slab = slice(pair * LANES, (pair + 1) * LANES)
            dk_ref[:, slab] = (jnp.where(low, dk[2 * pair], dk[2 * pair + 1]) * scale).astype(bf16)
            dv_ref[:, slab] = jnp.where(low, dv[2 * pair], dv[2 * pair + 1]).astype(bf16)

    blk = functools.partial(_two_block_spec, AT_WIDTH, 3)
    same = _two_block_spec(AT_WIDTH, 1, 0, "own")
    nxt = _two_block_spec(AT_WIDTH, 1, 0, "after", nb)
    c_same = _two_block_spec(HEAD_LANES, 1, 0, "own")
    c_nxt = _two_block_spec(HEAD_LANES, 1, 0, "after", nb)
    return pl.pallas_call(
        body, name=f"attn_backward_kv_d{dil}", grid=(dil, nb // ATTN_BLOCKS),
        in_specs=[blk(1, "own"), blk(2, "own"), blk(0, "own"), blk(0, "after", nb), same, nxt, c_same, c_nxt, c_same, c_nxt],
        out_specs=[same, same],
        out_shape=[jax.ShapeDtypeStruct((length, dil * AT_WIDTH), bf16)] * 2,
        compiler_params=_params(("parallel", "parallel")),
    )(pat_view, pat_view, pat_view, pat_view, d_out, d_out, delta, delta, lse, lse)


def _gated_norms(o_dn, z_dn, o_at, z_at, dn_w, at_w):
    ms_dn = _slab_sums(o_dn * o_dn) * (1.0 / DN_DIM)
    a = o_dn * lax.rsqrt(ms_dn + EPS) * dn_w * _silu(z_dn)
    head_of_lane = (lax.broadcasted_iota(jnp.int32, (AT_WIDTH, HEAD_LANES), 0) // AT_DIM
                    == lax.broadcasted_iota(jnp.int32, (AT_WIDTH, HEAD_LANES), 1)).astype(f32)
    lanes_of_head = (lax.broadcasted_iota(jnp.int32, (HEAD_LANES, AT_WIDTH), 0)
                     == lax.broadcasted_iota(jnp.int32, (HEAD_LANES, AT_WIDTH), 1) // AT_DIM).astype(f32)
    ms_at = _sel_right(_sel_right(o_at * o_at, head_of_lane), lanes_of_head) * (1.0 / AT_DIM)
    b = o_at * lax.rsqrt(ms_at + EPS) * at_w * _silu(z_at)
    return a, b


def _residual_loss(x, mix, gate, fin_w, target):
    x2 = x + gate * mix
    y = x2 * lax.rsqrt(jnp.mean(x2 * x2, axis=-1, keepdims=True) + EPS) * fin_w
    err = y - target
    per_token = jnp.sum(err * err, axis=1, keepdims=True) * (1.0 / D_MODEL)
    return 0.5 * jnp.sum(per_token, axis=0, keepdims=True)


TAIL_ROWS = 8


def _tail(x, target, o_dn, pdn, outs, lses, z_at, gate, w_out, dn_w_row, at_w_row, fin_w):
    seq = x.shape[0]
    tm = TOKEN_TILE

    def body(x_ref, t_ref, odn_ref, zdn_ref, o1_ref, o4_ref, o16_ref, l1_ref, l4_ref, l16_ref, zat_ref, gate_ref,
             wt_ref, wb_ref, dnw_ref, atw_ref, fw_ref,
             dx2_ref, dodn_ref, dzdn_ref, doat1_ref, doat4_ref, doat16_ref, delta1_ref, delta4_ref, delta16_ref,
             lse1_ref, lse4_ref, lse16_ref, dzat_ref, gwt_ref, gwb_ref, rows_ref, *stages):
        @pl.when(pl.program_id(0) == 0)
        def _():
            gwt_ref[...] = jnp.zeros_like(gwt_ref)
            gwb_ref[...] = jnp.zeros_like(gwb_ref)
            rows_ref[...] = jnp.zeros_like(rows_ref)

        l1, l4, l16 = l1_ref[...], _view_to_value(stages[0], l4_ref, 4), _view_to_value(stages[1], l16_ref, 16)
        o4, o16 = _view_to_value(stages[2], o4_ref, 4), _view_to_value(stages[3], o16_ref, 16)
        top = jnp.maximum(jnp.maximum(l1, l4), l16)
        e1, e4, e16 = jnp.exp(l1 - top), jnp.exp(l4 - top), jnp.exp(l16 - top)
        den = e1 + e4 + e16
        lse = top + jnp.log(den)
        o_at = (_repeat_heads(e1 / den) * o1_ref[...] + _repeat_heads(e4 / den) * o4 + _repeat_heads(e16 / den) * o16)

        (a, b), norms_vjp = jax.vjp(_gated_norms, odn_ref[...], zdn_ref[...], o_at, zat_ref[...], dnw_ref[...], atw_ref[...])
        a, b = a.astype(bf16), b.astype(bf16)
        mix = jnp.dot(a, wt_ref[...], preferred_element_type=f32) + jnp.dot(b, wb_ref[...], preferred_element_type=f32)
        loss, loss_vjp = jax.vjp(_residual_loss, x_ref[...], mix, gate_ref[...], fw_ref[...], t_ref[...])
        dx2, dmix, dgate, dfw, _ = loss_vjp(jnp.ones((1, 1), f32))
        dmix = dmix.astype(bf16)
        dwt = lax.dot_general(a, dmix, (TN, ((), ())), preferred_element_type=f32)
        dwb = lax.dot_general(b, dmix, (TN, ((), ())), preferred_element_type=f32)
        da = lax.dot_general(dmix, wt_ref[...], (NT, ((), ())), preferred_element_type=f32)
        db = lax.dot_general(dmix, wb_ref[...], (NT, ((), ())), preferred_element_type=f32)
        dodn, dzdn, doat, dzat, ddnw, datw = norms_vjp((da, db))
        dx2_ref[...] = dx2
        dodn_ref[...] = dodn
        dzdn_ref[...] = dzdn
        dzat_ref[...] = dzat
        lane_head = (lax.broadcasted_iota(jnp.int32, (AT_WIDTH, HEAD_LANES), 0) // AT_DIM
                     == lax.broadcasted_iota(jnp.int32, (AT_WIDTH, HEAD_LANES), 1))
        delta = _dot_sel(lane_head.astype(f32), doat * o_at, NN, 1)
        for stage_ref, value, refs in ((stages[2], doat, (doat1_ref, doat4_ref, doat16_ref)),
                                       (stages[0], delta, (delta1_ref, delta4_ref, delta16_ref)),
                                       (stages[1], lse, (lse1_ref, lse4_ref, lse16_ref))):
            refs[0][...] = value.astype(refs[0].dtype)
            _stage_lanes(stage_ref, 0, value)
            _stage_to_view(stage_ref, refs[1], 4)
            _stage_to_view(stage_ref, refs[2], 16)
        gwt_ref[...] += dwt
        gwb_ref[...] += dwb
        rows_ref[0:1, :] += jnp.broadcast_to(loss, (1, D_MODEL))
        rows_ref[1:2, :] += dgate
        rows_ref[2:3, :] += dfw
        rows_ref[3:4, 0:DN_WIDTH] += ddnw
        rows_ref[4:5, 0:AT_WIDTH] += datw

    tile = lambda w: pl.BlockSpec((tm, w), lambda i: (i, 0))
    row = lambda w: pl.BlockSpec((1, w), lambda i: (0, 0))
    half_w = pl.BlockSpec((DN_WIDTH, D_MODEL), lambda i: (0, 0))
    sds = lambda w: jax.ShapeDtypeStruct((seq, w), f32)
    views = [_view_spec(AT_WIDTH, d) for d in PATTERN_DILATIONS]
    compact = [_view_spec(HEAD_LANES, d) for d in PATTERN_DILATIONS]
    doat_shapes = [_view_shape(seq, AT_WIDTH, d, bf16) for d in PATTERN_DILATIONS]
    compact_shapes = [_view_shape(seq, HEAD_LANES, d) for d in PATTERN_DILATIONS]
    return pl.pallas_call(
        body, name="tail", grid=(seq // tm,),
        in_specs=[tile(D_MODEL), tile(D_MODEL), tile(DN_WIDTH),
                  pl.BlockSpec((tm, DN_WIDTH), lambda i: (i, QKV_W // DN_WIDTH)),
                  *views, *compact, tile(AT_WIDTH), row(D_MODEL), half_w, pl.BlockSpec((AT_WIDTH, D_MODEL), lambda i: (1, 0)),
                  row(DN_WIDTH), row(AT_WIDTH), row(D_MODEL)],
        out_specs=[tile(D_MODEL), tile(DN_WIDTH), tile(DN_WIDTH), *views, *compact, *compact,
                   tile(AT_WIDTH), half_w, half_w, pl.BlockSpec((TAIL_ROWS, D_MODEL), lambda i: (0, 0))],
        out_shape=[sds(D_MODEL), sds(DN_WIDTH), sds(DN_WIDTH), *doat_shapes, *compact_shapes, *compact_shapes,
                   sds(AT_WIDTH), jax.ShapeDtypeStruct((DN_WIDTH, D_MODEL), f32),
                   jax.ShapeDtypeStruct((AT_WIDTH, D_MODEL), f32), jax.ShapeDtypeStruct((TAIL_ROWS, D_MODEL), f32)],
        scratch_shapes=[_stage(HEAD_LANES)] * 2 + [_stage(AT_WIDTH)] * 2,
        compiler_params=_params(("arbitrary",)),
    )(x, target, o_dn, pdn, outs[0], outs[1], outs[2], lses[0], lses[1], lses[2], z_at, gate, w_out, w_out,
      dn_w_row, at_w_row, fin_w)


PRE_ROWS = 8


def _pre_backward(x, dx2, pos_col, invf_row, sign_row, norm_w, scale, shift, w_cat, conv_w8, pdn, d_conv, d_zdn, d_ba,
                  d_q, d_k, d_v, d_zat):
    seq = x.shape[0]
    tm = TOKEN_TILE
    last = seq // tm - 1

    def body(x_ref, dx2_ref, pos_ref, invf_ref, sign_ref, nw_ref, sc_ref, sh_ref, w_ref, cw_ref,
             pre_ref, prehalo_ref, dc_ref, dchalo_ref, dz_ref, dba_ref,
             dq1_ref, dq4_ref, dq16_ref, dk1_ref, dk4_ref, dk16_ref, dv1_ref, dv4_ref, dv16_ref, dzat_ref,
             gx_ref, dproj_ref, rows_ref, crow_ref, ext_ref, *stages):
        i = pl.program_id(0)

        @pl.when(i == 0)
        def _():
            rows_ref[...] = jnp.zeros_like(rows_ref)
            crow_ref[...] = jnp.zeros_like(crow_ref)

        dc = dc_ref[...]
        ext_ref[0:tm, :] = dc
        ext_ref[tm:tm + 8, :] = jnp.where(i < last, dchalo_ref[...], 0.0)
        w8 = cw_ref[...]
        d_pre = ext_ref[pl.ds(3, tm), :] * w8[0:1, :]
        for j in range(1, CONV_K):
            d_pre = d_pre + ext_ref[pl.ds(3 - j, tm), :] * w8[j:j + 1, :]
        ext_ref[0:8, :] = jnp.where(i > 0, prehalo_ref[...], 0.0)
        ext_ref[8:8 + tm, :] = pre_ref[...]
        for j in range(CONV_K):
            crow_ref[j:j + 1, :] += jnp.sum(dc * ext_ref[pl.ds(5 + j, tm), :], axis=0, keepdims=True)

        cos_t, sin_t = _rope_tables(pos_ref[...], invf_ref[...], sign_ref[...])
        dq = dq1_ref[...] + _view_to_value(stages[0], dq4_ref, 4) + _view_to_value(stages[1], dq16_ref, 16)
        dk = dk1_ref[...] + _view_to_value(stages[2], dk4_ref, 4) + _view_to_value(stages[3], dk16_ref, 16)
        dq = dq * cos_t + _rope_partner(dq * sin_t)
        dk = dk * cos_t + _rope_partner(dk * sin_t)
        dv = dv1_ref[...] + _view_to_value(stages[4], dv4_ref, 4) + _view_to_value(stages[5], dv16_ref, 16)
        dh = jnp.zeros((tm, D_MODEL), f32)
        start = 0
        for piece in (d_pre, dz_ref[...], dba_ref[...], dq, dk, dv, dzat_ref[...]):
            stop = start + piece.shape[1]
            piece = piece.astype(bf16)
            dproj_ref[:, start:stop] = piece
            dh = dh + lax.dot_general(piece, w_ref[:, start:stop], (NT, ((), ())), preferred_element_type=f32)
            start = stop
        _, vjp = jax.vjp(_hnorm, x_ref[...], nw_ref[...], sc_ref[...], sh_ref[...])
        dx, dnw, dsc, dsh = vjp(dh)
        gx_ref[...] = dx + dx2_ref[...]
        rows_ref[0:1, :] += dnw
        rows_ref[1:2, :] += dsc
        rows_ref[2:3, :] += dsh

    tile = lambda w: pl.BlockSpec((tm, w), lambda i: (i, 0))
    row = lambda w: pl.BlockSpec((1, w), lambda i: (0, 0))
    step8 = tm // 8
    return pl.pallas_call(
        body, name="pre_backward", grid=(seq // tm,),
        in_specs=[tile(D_MODEL), tile(D_MODEL), tile(1), row(128), row(128), row(D_MODEL), row(D_MODEL), row(D_MODEL),
                  pl.BlockSpec((D_MODEL, CAT_W), lambda i: (0, 0)), pl.BlockSpec((8, QKV_W), lambda i: (0, 0)),
                  tile(QKV_W), pl.BlockSpec((8, QKV_W), lambda i: (jnp.maximum(i * step8 - 1, 0), 0)),
                  tile(QKV_W), pl.BlockSpec((8, QKV_W), lambda i: (jnp.minimum((i + 1) * step8, seq // 8 - 1), 0)),
                  tile(DN_WIDTH), tile(BA_W)] + [_view_spec(AT_WIDTH, d) for d in PATTERN_DILATIONS] * 3 + [tile(AT_WIDTH)],
        out_specs=[tile(D_MODEL), tile(CAT_W), pl.BlockSpec((PRE_ROWS, D_MODEL), lambda i: (0, 0)),
                   pl.BlockSpec((8, QKV_W), lambda i: (0, 0))],
        out_shape=[jax.ShapeDtypeStruct((seq, D_MODEL), f32), jax.ShapeDtypeStruct((seq, CAT_W), bf16),
                   jax.ShapeDtypeStruct((PRE_ROWS, D_MODEL), f32), jax.ShapeDtypeStruct((8, QKV_W), f32)],
        scratch_shapes=[pltpu.VMEM((tm + 8, QKV_W), f32)] + [_stage(AT_WIDTH)] * 6,
        compiler_params=_params(("arbitrary",)),
    )(x, dx2, pos_col, invf_row, sign_row, norm_w, scale, shift, w_cat, conv_w8, pdn, pdn, d_conv, d_conv, d_zdn, d_ba,
      d_q[0], d_q[1], d_q[2], d_k[0], d_k[1], d_k[2], d_v[0], d_v[1], d_v[2], d_zat)


def _weight_grad(h, d_proj):
    seq = h.shape[0]
    tk, tn = 512, CAT_W // 3
    n_k = seq // tk

    def body(h_ref, d_ref, o_ref):
        @pl.when(pl.program_id(1) == 0)
        def _():
            o_ref[...] = jnp.zeros_like(o_ref)

        o_ref[...] += lax.dot_general(h_ref[...], d_ref[...], (TN, ((), ())), preferred_element_type=f32)

    return pl.pallas_call(
        body, name="weight_grad", grid=(CAT_W // tn, n_k),
        in_specs=[pl.BlockSpec((tk, D_MODEL), lambda n, k: (k, 0)), pl.BlockSpec((tk, tn), lambda n, k: (k, n))],
        out_specs=pl.BlockSpec((D_MODEL, tn), lambda n, k: (0, n)),
        out_shape=jax.ShapeDtypeStruct((D_MODEL, CAT_W), f32),
        compiler_params=_params(("parallel", "arbitrary")),
    )(h, d_proj)


def _adamw(w, g, m, v):
    m = ADAM_B1 * m + (1.0 - ADAM_B1) * g
    v = ADAM_B2 * v + (1.0 - ADAM_B2) * (g * g)
    m_hat = m / (1.0 - ADAM_B1 ** ADAM_STEP)
    v_hat = v / (1.0 - ADAM_B2 ** ADAM_STEP)
    delta = -ADAM_LR * (m_hat / (jnp.sqrt(v_hat) + ADAM_EPS) + ADAM_WD * w)
    return delta, m, v


def _mod_shard(c_all, w_mod_s, b_mod_s):
    def body(c_ref, w_ref, b_ref, o_ref):
        o_ref[...] = _dot(_silu(c_ref[...]), w_ref[...], NN, True) + b_ref[...]

    return pl.pallas_call(body, name="mod_shard", out_shape=jax.ShapeDtypeStruct((N_DEV, w_mod_s.shape[1]), f32),
                          compiler_params=_params())(c_all, w_mod_s, b_mod_s)


def _mod_update(c_all, d_mod_s, w, m, v):
    def body(c_ref, d_ref, w_ref, m_ref, v_ref, g_ref, dw_ref, nm_ref, nv_ref):
        g = _dot(_silu(c_ref[...]), d_ref[...], TN, True)
        g_ref[...] = g
        dw_ref[...], nm_ref[...], nv_ref[...] = _adamw(w_ref[...], g, m_ref[...], v_ref[...])

    return pl.pallas_call(body, name="mod_update", out_shape=[jax.ShapeDtypeStruct(w.shape, f32)] * 4,
                          compiler_params=_params())(c_all, d_mod_s, w, m, v)


def _adamw_rows(w, g, m, v, name):
    rows, width = w.shape
    tr = 128 if rows % 128 == 0 else rows

    def body(w_ref, g_ref, m_ref, v_ref, dw_ref, nm_ref, nv_ref):
        dw_ref[...], nm_ref[...], nv_ref[...] = _adamw(w_ref[...], g_ref[...], m_ref[...], v_ref[...])

    spec = pl.BlockSpec((tr, width), lambda i: (i, 0))
    return pl.pallas_call(body, name=name, grid=(rows // tr,), in_specs=[spec] * 4, out_specs=[spec] * 3,
                          out_shape=[jax.ShapeDtypeStruct(w.shape, f32)] * 3,
                          compiler_params=_params(("parallel",)))(w, g, m, v)


SM_NORM, SM_FIN, SM_DN, SM_AT, SM_A, SM_DT, SM_LOSS, SM_MOD = 0, 1024, 2048, 2560, 3072, 3200, 3328, 3456
SM_W = SM_MOD + 3 * D_MODEL
RS_NORM, RS_FIN, RS_DN, RS_AT, RS_A, RS_DT, RS_BMOD = 0, 1024, 2048, 2176, 2304, 2432, 2560
RS_W = RS_BMOD + 3 * D_MODEL


def _small_update(gathered, w, m, v):
    def body(g_ref, w_ref, m_ref, v_ref, grad_ref, dw_ref, nm_ref, nv_ref, loss_ref, dmod_ref):
        total = g_ref[0:1, :]
        for dev in range(1, N_DEV):
            total = total + g_ref[8 * dev:8 * dev + 1, :]
        for dev in range(N_DEV):
            dmod_ref[dev:dev + 1, :] = g_ref[8 * dev:8 * dev + 1, SM_MOD:SM_W]
        dn = total[:, SM_DN:SM_DN + DN_DIM]
        for h in range(1, DN_HEADS):
            dn = dn + total[:, SM_DN + h * DN_DIM:SM_DN + (h + 1) * DN_DIM]
        at = total[:, SM_AT:SM_AT + AT_DIM]
        for h in range(1, AT_HEADS):
            at = at + total[:, SM_AT + h * AT_DIM:SM_AT + (h + 1) * AT_DIM]
        grad_ref[:, RS_NORM:RS_FIN] = total[:, SM_NORM:SM_FIN]
        grad_ref[:, RS_FIN:RS_DN] = total[:, SM_FIN:SM_DN]
        grad_ref[:, RS_DN:RS_AT] = dn
        grad_ref[:, RS_AT:RS_A] = jnp.zeros((1, 128), f32)
        grad_ref[:, RS_AT:RS_AT + AT_DIM] = at
        grad_ref[:, RS_A:RS_DT] = total[:, SM_A:SM_DT]
        grad_ref[:, RS_DT:RS_BMOD] = total[:, SM_DT:SM_LOSS]
        grad_ref[:, RS_BMOD:RS_W] = total[:, SM_MOD:SM_W]
        loss_ref[...] = total[:, SM_LOSS:SM_MOD]
        dw_ref[...], nm_ref[...], nv_ref[...] = _adamw(w_ref[...], grad_ref[...], m_ref[...], v_ref[...])

    row = jax.ShapeDtypeStruct((1, RS_W), f32)
    return pl.pallas_call(
        body, name="small_update",
        out_shape=[row, row, row, row, jax.ShapeDtypeStruct((1, 128), f32), jax.ShapeDtypeStruct((N_DEV, 3 * D_MODEL), f32)],
        compiler_params=_params())(gathered, w, m, v)


def _all_gather_rows(block, name):
    m_per, n = block.shape

    def body(x_ref, out_ref, send_sems, recv_sems, local_sem):
        x, y, c = lax.axis_index("x"), lax.axis_index("y"), lax.axis_index("c")
        me, sibling = (x, y, c), (x, y, 1 - c)
        chips = [(1 - x, y), (x, 1 - y), (1 - x, 1 - y)]

        def rows(px, py, pc):
            return out_ref.at[pl.ds((4 * px + 2 * py + pc) * m_per, m_per), :]

        def copy(k, blk, to, src=None):
            return pltpu.make_async_remote_copy(
                src_ref=rows(*blk) if src is None else src, dst_ref=rows(*blk),
                send_sem=send_sems.at[k], recv_sem=recv_sems.at[k], device_id=to, device_id_type=MESH)

        mine = pltpu.make_async_copy(x_ref, rows(*me), local_sem)
        mine.start()
        first = [copy(0, me, sibling, src=x_ref)]
        first += [copy(1 + j, me, (*chip, c), src=x_ref) for j, chip in enumerate(chips)]
        for cp in first:
            cp.start()
        passed = [copy(4 + j, (*chip, c), sibling) for j, chip in enumerate(chips)]
        for j, chip in enumerate(chips):
            copy(1 + j, (*chip, c), me).wait_recv()
            passed[j].start()
        copy(0, sibling, me).wait_recv()
        for j, chip in enumerate(chips):
            copy(4 + j, (*chip, 1 - c), me).wait_recv()
        for cp in first + passed:
            cp.wait_send()
        mine.wait()

    return pl.pallas_call(
        body, name=name,
        out_shape=jax.ShapeDtypeStruct((N_DEV * m_per, n), block.dtype),
        in_specs=[pl.BlockSpec(memory_space=pltpu.VMEM)],
        out_specs=pl.BlockSpec(memory_space=pltpu.VMEM),
        scratch_shapes=[pltpu.SemaphoreType.DMA((7,)), pltpu.SemaphoreType.DMA((7,)), pltpu.SemaphoreType.DMA],
        compiler_params=pltpu.CompilerParams(vmem_limit_bytes=VMEM_LIMIT),
    )(block)


def _gather_weight_shards(packed):
    rows, width = packed.shape
    half = rows // 2

    def body(src_ref, out_ref, send_sems, recv_sems, local_sem):
        x, y, c = lax.axis_index("x"), lax.axis_index("y"), lax.axis_index("c")
        chips = [(1 - x, y), (x, 1 - y), (1 - x, 1 - y)]
        my_rows = pl.ds(pl.multiple_of(c * half, 16), half)
        sibling_rows = pl.ds(pl.multiple_of((1 - c) * half, 16), half)

        def part(chip, part_rows):
            return out_ref.at[2 * chip[0] + chip[1], part_rows]

        def copy(k, src, dst, to):
            return pltpu.make_async_remote_copy(src_ref=src, dst_ref=dst, send_sem=send_sems.at[k], recv_sem=recv_sems.at[k],
                                                device_id=to, device_id_type=MESH)

        mine = pltpu.make_async_copy(src_ref, out_ref.at[2 * x + y], local_sem)
        mine.start()
        sends = [copy(k, src_ref.at[my_rows], part((x, y), my_rows), (*chip, c)) for k, chip in enumerate(chips)]
        for cp in sends:
            cp.start()
        passed = []
        for k, chip in enumerate(chips):
            landed = part(chip, my_rows)
            copy(k, landed, landed, (*chip, c)).wait_recv()
            passed.append(copy(3 + k, landed, landed, (x, y, 1 - c)))
            passed[k].start()
        for k, chip in enumerate(chips):
            from_sibling = part(chip, sibling_rows)
            copy(3 + k, from_sibling, from_sibling, (x, y, 1 - c)).wait_recv()
        for cp in sends + passed:
            cp.wait_send()
        mine.wait()

    return pl.pallas_call(
        body, name="gather_weight_shards",
        out_shape=jax.ShapeDtypeStruct((N_CHIPS, rows, width), packed.dtype),
        in_specs=[pl.BlockSpec(memory_space=pl.ANY)],
        out_specs=pl.BlockSpec(memory_space=pl.ANY),
        scratch_shapes=[pltpu.SemaphoreType.DMA((6,)), pltpu.SemaphoreType.DMA((6,)), pltpu.SemaphoreType.DMA],
    )(packed)


def _reduce_weight_grads(grads):
    _, rows, width = grads.shape
    half = rows // 2
    n_chunks = half // PACK_CHUNK

    def body(g_ref, out_ref, mine_ref, sibling_ref, send_ref, land_ref, send_sems, recv_sems, local_sem):
        x, y, c = lax.axis_index("x"), lax.axis_index("y"), lax.axis_index("c")
        chips = [(1 - x, y), (x, 1 - y), (1 - x, 1 - y)]
        sibling = (x, y, 1 - c)
        my_rows = pl.ds(pl.multiple_of(c * half, 16), half)
        sibling_rows = pl.ds(pl.multiple_of((1 - c) * half, 16), half)
        all_slots = pl.ds(0, N_CHIPS)

        def copy(k, src, dst, to):
            return pltpu.make_async_remote_copy(src_ref=src, dst_ref=dst, send_sem=send_sems.at[k], recv_sem=recv_sems.at[k],
                                                device_id=to, device_id_type=MESH)

        mine = pltpu.make_async_copy(g_ref.at[all_slots, my_rows], mine_ref, local_sem)
        mine.start()
        swap = copy(0, g_ref.at[all_slots, sibling_rows], sibling_ref, sibling)
        swap.start()
        mine.wait()
        swap.wait_recv()

        def pair_sums(i, carry):
            r = pl.ds(pl.multiple_of(i * PACK_CHUNK, 16), PACK_CHUNK)
            for k, chip in enumerate(chips):
                slot = 2 * chip[0] + chip[1]
                send_ref[k, r, :] = (mine_ref[slot, r, :] + sibling_ref[slot, r, :]).astype(bf16)
            out_rows = pl.ds(pl.multiple_of(c * half + i * PACK_CHUNK, 16), PACK_CHUNK)
            out_ref[out_rows, :] = mine_ref[2 * x + y, r, :] + sibling_ref[2 * x + y, r, :]
            return carry

        lax.fori_loop(0, n_chunks, pair_sums, 0)
        sends = [copy(1 + k, send_ref.at[k], land_ref.at[k], (*chip, c)) for k, chip in enumerate(chips)]
        for cp in sends:
            cp.start()
        for cp in sends:
            cp.wait_recv()

        def add_landed(i, carry):
            r = pl.ds(pl.multiple_of(i * PACK_CHUNK, 16), PACK_CHUNK)
            out_rows = pl.ds(pl.multiple_of(c * half + i * PACK_CHUNK, 16), PACK_CHUNK)
            landed = [land_ref[k, r, :].astype(f32) for k in range(3)]
            out_ref[out_rows, :] = ((out_ref[out_rows, :] + landed[0]) + landed[1]) + landed[2]
            return carry

        lax.fori_loop(0, n_chunks, add_landed, 0)
        finished = copy(4, out_ref.at[my_rows], out_ref.at[my_rows], sibling)
        finished.start()
        copy(4, out_ref.at[sibling_rows], out_ref.at[sibling_rows], sibling).wait_recv()
        for cp in [swap, finished] + sends:
            cp.wait_send()

    return pl.pallas_call(
        body, name="reduce_weight_grads",
        out_shape=jax.ShapeDtypeStruct((rows, width), f32),
        in_specs=[pl.BlockSpec(memory_space=pl.ANY)],
        out_specs=pl.BlockSpec(memory_space=pltpu.VMEM),
        scratch_shapes=[pltpu.VMEM((N_CHIPS, half, width), f32), pltpu.VMEM((N_CHIPS, half, width), f32),
                        pltpu.VMEM((3, half, width), bf16), pltpu.VMEM((3, half, width), bf16),
                        pltpu.SemaphoreType.DMA((5,)), pltpu.SemaphoreType.DMA((5,)), pltpu.SemaphoreType.DMA],
        compiler_params=pltpu.CompilerParams(vmem_limit_bytes=VMEM_LIMIT),
    )(grads)


CONV_SHARD = QKV_W // N_CHIPS


def _pack_shards(w_in_s, w_out_s, conv_rows):
    lead = w_in_s.shape[:-2]

    def widen(a, rows):
        return jnp.pad(a, [(0, 0)] * len(lead) + [(0, rows - a.shape[-2]), (0, PACK_W - a.shape[-1])])

    used = D_MODEL + OUT_ROWS + 8
    return jnp.concatenate([widen(w_in_s, D_MODEL), widen(w_out_s, OUT_ROWS), widen(conv_rows, 8),
                            jnp.zeros((*lead, PACK_ROWS - used, PACK_W), w_in_s.dtype)], axis=-2)


def _unpack_shards(packed):
    return (packed[..., :D_MODEL, :SHARD_IN], packed[..., D_MODEL:D_MODEL + OUT_ROWS, :D_MODEL],
            packed[..., D_MODEL + OUT_ROWS:D_MODEL + OUT_ROWS + CONV_K, :])


def _pack_weights_bf16(w_in_s, w_out_s, conv_s):
    conv_bits = lax.bitcast_convert_type(conv_s, bf16).reshape(CONV_K, 2 * CONV_SHARD)
    return _pack_shards(w_in_s.astype(bf16), w_out_s.astype(bf16), conv_bits)


def _unpack_weights_bf16(shards):
    w_in_s, w_out_s, conv_bits = _unpack_shards(shards)
    conv_s = lax.bitcast_convert_type(conv_bits[..., :2 * CONV_SHARD].reshape(N_CHIPS, CONV_K, CONV_SHARD, 2), f32)
    return (jnp.transpose(w_in_s, (1, 0, 2)).reshape(D_MODEL, IN_COLS), w_out_s.reshape(D_MODEL, D_MODEL),
            jnp.transpose(conv_s, (1, 0, 2)).reshape(CONV_K, QKV_W))


def _local_step(x, target, pos_col, mod_row, norm_w, w_in, conv_w, a_log, dt_bias, dn_norm_w, at_norm_w, w_out, fin_w):
    shift, scale, gate = mod_row[:, :D_MODEL], mod_row[:, D_MODEL:2 * D_MODEL], mod_row[:, 2 * D_MODEL:]
    half = AT_DIM // 2
    lane = jnp.arange(128)
    inv_freq = ROPE_THETA ** (-jnp.arange(half, dtype=f32) / half)
    invf_row = inv_freq[lane % half].reshape(1, 128)
    sign_row = jnp.where((lane % AT_DIM) < half, -1.0, 1.0).astype(f32).reshape(1, 128)
    ba_w = jnp.pad(w_in[:, 2 * D_MODEL:2 * D_MODEL + 2 * DN_HEADS], ((0, 0), (0, BA_W - 2 * DN_HEADS)))
    w_cat = jnp.concatenate([w_in[:, :2 * D_MODEL], ba_w, w_in[:, 2 * D_MODEL + 2 * DN_HEADS:]], axis=1).astype(bf16)
    conv_w8 = jnp.pad(conv_w, ((0, 8 - CONV_K), (0, 0)))
    a_row = jnp.pad(a_log.reshape(1, DN_HEADS), ((0, 0), (DN_HEADS, BA_W - 2 * DN_HEADS)))
    dt_row = jnp.pad(dt_bias.reshape(1, DN_HEADS), ((0, 0), (DN_HEADS, BA_W - 2 * DN_HEADS)))
    dn_w_row = jnp.tile(dn_norm_w.reshape(1, DN_DIM), (1, DN_HEADS))
    at_w_row = jnp.tile(at_norm_w.reshape(1, AT_DIM), (1, AT_HEADS))
    norm_row = norm_w.reshape(1, D_MODEL)
    fin_row = fin_w.reshape(1, D_MODEL)
    w_out_bf = w_out.astype(bf16)

    pdn, *views, z_at, h = _pre_proj(x, pos_col, invf_row, sign_row, norm_row, scale, shift, w_cat)
    *dn_parts, dn_invs = _dn_intra_forward(pdn, conv_w8, a_row, dt_row)
    o_dn, states = _dn_scan_forward(*dn_parts)
    outs, lses = [], []
    for d, view in zip(PATTERN_DILATIONS, views):
        o, lse = _attn_forward(view, d)
        outs.append(o)
        lses.append(lse)
    (dx2, d_odn, d_zdn, *in_views, d_zat, g_wtop, g_wbot, tail_rows) = _tail(
        x, target, o_dn, pdn, outs, lses, z_at, gate, w_out_bf, dn_w_row, at_w_row, fin_row)
    d_oat, delta, lse_all = in_views[0:3], in_views[3:6], in_views[6:9]
    d_q, d_k, d_v = [], [], []
    for n, (d, view) in enumerate(zip(PATTERN_DILATIONS, views)):
        args = (d_oat[n], delta[n], lse_all[n])
        d_q.append(_attn_backward_q(view, *args, d))
        dk, dv = _attn_backward_kv(view, *args, d)
        d_k.append(dk)
        d_v.append(dv)
    d_parts = _dn_scan_backward(*dn_parts, states, d_odn)
    d_conv, d_ba, dn_rows = _dn_intra_backward(pdn, conv_w8, a_row, dt_row, dn_invs, *d_parts)
    grad_x, d_proj, pre_rows, conv_rows = _pre_backward(
        x, dx2, pos_col, invf_row, sign_row, norm_row, scale, shift, w_cat, conv_w8, pdn, d_conv, d_zdn, d_ba,
        d_q, d_k, d_v, d_zat)
    g_cat = _weight_grad(h, d_proj)
    g_w_in = jnp.concatenate([g_cat[:, :2 * D_MODEL], g_cat[:, 2 * D_MODEL:2 * D_MODEL + 2 * DN_HEADS],
                              g_cat[:, 2 * D_MODEL + BA_W:]], axis=1)
    g_w_out = jnp.concatenate([g_wtop, g_wbot], axis=0)
    g_conv = conv_rows[:CONV_K]
    small = jnp.concatenate([
        pre_rows[0:1], tail_rows[2:3], tail_rows[3:4, :DN_WIDTH], tail_rows[4:5, :AT_WIDTH],
        dn_rows[0:1], dn_rows[1:2], tail_rows[0:1, :128],
        pre_rows[2:3], pre_rows[1:2], tail_rows[1:2]], axis=1)
    return grad_x, g_w_in, g_w_out, g_conv, small


def kernel(x, c, positions, w_mod, b_mod, norm_w, w_in, conv_w, a_log, dt_bias, dn_norm_w, at_norm_w, w_out, final_norm_w, loss_target, m_w_mod, m_b_mod, m_norm_w, m_w_in, m_conv_w, m_a_log, m_dt_bias, m_dn_norm_w, m_at_norm_w, m_w_out, m_final_norm_w, v_w_mod, v_b_mod, v_norm_w, v_w_in, v_conv_w, v_a_log, v_dt_bias, v_dn_norm_w, v_at_norm_w, v_w_out, v_final_norm_w):
    seq = x.shape[1]
    ax, ay, ac = lax.axis_index("x"), lax.axis_index("y"), lax.axis_index("c")
    me = 4 * ax + 2 * ay + ac
    chip = 2 * ax + ay

    c_all = _all_gather_rows(jnp.pad(c, ((0, 7), (0, 0))), "gather_c").reshape(N_DEV, 8, D_MODEL)[:, 0]
    b_mod_s = lax.dynamic_slice_in_dim(b_mod, chip * 768, 768, axis=1)
    mod_part = _mod_shard(c_all, w_mod[0], b_mod_s)
    mod_all = _all_gather_rows(mod_part, "gather_mod").reshape(N_CHIPS, 2, N_DEV, 768)[:, 0]
    mod_row = lax.dynamic_index_in_dim(mod_all, me, axis=1, keepdims=False).reshape(1, 3 * D_MODEL)

    shards = _gather_weight_shards(_pack_weights_bf16(w_in[0], w_out[0], conv_w[0]))
    w_in_full, w_out_full, conv_full = _unpack_weights_bf16(shards)

    grad_x, g_w_in, g_w_out, g_conv, small = _local_step(
        x[0], loss_target[0], positions.reshape(seq, 1), mod_row, norm_w, w_in_full, conv_full, a_log, dt_bias,
        dn_norm_w, at_norm_w, w_out_full, final_norm_w)

    g_pack = _pack_shards(jnp.transpose(g_w_in.reshape(D_MODEL, N_CHIPS, SHARD_IN), (1, 0, 2)),
                          g_w_out.reshape(N_CHIPS, OUT_ROWS, D_MODEL),
                          jnp.transpose(g_conv.reshape(CONV_K, N_CHIPS, CONV_SHARD), (1, 0, 2)))
    grad_w_in, grad_w_out, grad_conv_rows = _unpack_shards(_reduce_weight_grads(g_pack))
    grad_conv_w = grad_conv_rows[:, :CONV_SHARD]
    delta_w_in, new_m_w_in, new_v_w_in = _adamw_rows(w_in[0], grad_w_in, m_w_in[0], v_w_in[0], "adamw_w_in")
    delta_w_out, new_m_w_out, new_v_w_out = _adamw_rows(w_out[0], grad_w_out, m_w_out[0], v_w_out[0], "adamw_w_out")
    delta_conv_w, new_m_conv_w, new_v_conv_w = _adamw_rows(conv_w[0], grad_conv_w, m_conv_w[0], v_conv_w[0], "adamw_conv_w")

    gathered = _all_gather_rows(jnp.pad(small, ((0, 7), (0, 0))), "gather_small")

    def small_row(norm, fin, dn, at, a, dt, bmod):
        z = lambda n: jnp.zeros((1, n), f32)
        return jnp.concatenate([norm.reshape(1, -1), fin.reshape(1, -1), dn.reshape(1, -1), at.reshape(1, -1), z(64),
                                z(4), a.reshape(1, -1), z(120), z(4), dt.reshape(1, -1), z(120), bmod.reshape(1, -1)], axis=1)

    g_small, d_small, m_small, v_small, loss_row, d_mod_all = _small_update(
        gathered,
        small_row(norm_w, final_norm_w, dn_norm_w, at_norm_w, a_log, dt_bias, b_mod),
        small_row(m_norm_w, m_final_norm_w, m_dn_norm_w, m_at_norm_w, m_a_log, m_dt_bias, m_b_mod),
        small_row(v_norm_w, v_final_norm_w, v_dn_norm_w, v_at_norm_w, v_a_log, v_dt_bias, v_b_mod))

    def split_small(r):
        return (r[:, RS_BMOD:RS_W], r[:, RS_NORM:RS_FIN], r[:, RS_A + DN_HEADS:RS_A + 2 * DN_HEADS],
                r[:, RS_DT + DN_HEADS:RS_DT + 2 * DN_HEADS], r[:, RS_DN:RS_AT], r[:, RS_AT:RS_AT + AT_DIM],
                r[0, RS_FIN:RS_DN])

    d_mod_s = lax.dynamic_slice_in_dim(d_mod_all, chip * 768, 768, axis=1)
    pad_rows = lambda a: jnp.pad(a, ((0, 128 - N_DEV), (0, 0)))
    grad_w_mod, delta_w_mod, new_m_w_mod, new_v_w_mod = _mod_update(
        pad_rows(c_all), pad_rows(d_mod_s), w_mod[0], m_w_mod[0], v_w_mod[0])

    def ordered(w_mod_leaf, small_row_leaf, w_in_leaf, conv_leaf, w_out_leaf):
        b, n, a, dt, dn, at, fin = split_small(small_row_leaf)
        return [w_mod_leaf[None], b, n, w_in_leaf[None], conv_leaf[None], a, dt, dn, at, w_out_leaf[None], fin]

    loss = loss_row[0, 0]
    return (loss, grad_x[None],
            *ordered(grad_w_mod, g_small, grad_w_in, grad_conv_w, grad_w_out),
            *ordered(delta_w_mod, d_small, delta_w_in, delta_conv_w, delta_w_out),
            *ordered(new_m_w_mod, m_small, new_m_w_in, new_m_conv_w, new_m_w_out),
            *ordered(new_v_w_mod, v_small, new_v_w_in, new_v_conv_w, new_v_w_out))
```

```python
import functools

import jax
import jax.numpy as jnp
from jax import lax
from jax.experimental import pallas as pl
from jax.experimental.pallas import tpu as pltpu

f32 = jnp.float32
bf16 = jnp.bfloat16
HIGHEST = lax.Precision.HIGHEST
MESH = pl.DeviceIdType.MESH

D_MODEL = 1024
DN_HEADS = 4
DN_DIM = 128
DN_WIDTH = 512
AT_HEADS = 8
AT_DIM = 64
AT_WIDTH = 512
CHUNK = 64
Q_BLOCK = 128
CONV_K = 4
EPS = 1e-6
ROPE_THETA = 10000.0
PATTERN_DILATIONS = (1, 4, 16)
NEG = -1e30

QKV_W = 3 * DN_WIDTH
BA_W = 128
PDN_W = QKV_W + DN_WIDTH + BA_W
PAT_W = 3 * AT_WIDTH
CAT_W = PDN_W + PAT_W + AT_WIDTH
IN_COLS = 4104
N_CHIPS = 4
N_DEV = 8
SHARD_IN = IN_COLS // N_CHIPS
OUT_ROWS = D_MODEL // N_CHIPS
PACK_W = 1152
PACK_ROWS = 1312
PACK_CHUNK = 16

ADAM_LR = 0.001
ADAM_B1 = 0.9
ADAM_B2 = 0.999
ADAM_EPS = 1e-08
ADAM_WD = 0.01
ADAM_STEP = 10

VMEM_LIMIT = 56 * 1024 * 1024

NN = ((1,), (0,))
NT = ((1,), (1,))
TN = ((0,), (0,))


def _pieces(a, n):
    out = []
    for _ in range(n - 1):
        p = a.astype(bf16)
        out.append(p)
        a = a - p.astype(f32)
    out.append(a.astype(bf16))
    return out


def _dot(a, b, dims, exact):
    raw = lambda p, q: lax.dot_general(p, q, (dims, ((), ())), preferred_element_type=f32)
    if exact == "split":
        (ah, al), (bh, bl) = _pieces(a, 2), _pieces(b, 2)
        return raw(ah, bh) + (raw(ah, bl) + raw(al, bh))
    if exact:
        return lax.dot_general(a, b, (dims, ((), ())), precision=HIGHEST, preferred_element_type=f32)
    return raw(a.astype(bf16), b.astype(bf16))


def _dot_sel(sel, b, dims, sel_side):
    raw = lambda p, q: lax.dot_general(p, q, (dims, ((), ())), preferred_element_type=f32)
    sel = sel.astype(bf16)
    parts = [raw(sel, p) if sel_side == 0 else raw(p, sel) for p in _pieces(b, 3)]
    return (parts[0] + parts[1]) + parts[2]


@jax.custom_vjp
def _sel_left(sel, b):
    return _dot_sel(sel, b, NN, 0)


def _sel_left_fwd(sel, b):
    return _dot_sel(sel, b, NN, 0), sel


def _sel_left_bwd(sel, g):
    return jnp.zeros_like(sel), _dot_sel(sel, g, TN, 0)


_sel_left.defvjp(_sel_left_fwd, _sel_left_bwd)


@jax.custom_vjp
def _sel_right(a, sel):
    return _dot_sel(sel, a, NN, 1)


def _sel_right_fwd(a, sel):
    return _dot_sel(sel, a, NN, 1), sel


def _sel_right_bwd(sel, g):
    return _dot_sel(sel, g, NT, 1), jnp.zeros_like(sel)


_sel_right.defvjp(_sel_right_fwd, _sel_right_bwd)


class _Matmuls:
    def __init__(self, exact, back):
        @jax.custom_vjp
        def nn(a, b):
            return _dot(a, b, NN, exact)

        def nn_fwd(a, b):
            return _dot(a, b, NN, exact), (a, b)

        def nn_bwd(res, g):
            a, b = res
            return _dot(g, b, NT, back), _dot(a, g, TN, back)

        nn.defvjp(nn_fwd, nn_bwd)

        @jax.custom_vjp
        def nt(a, b):
            return _dot(a, b, NT, exact)

        def nt_fwd(a, b):
            return _dot(a, b, NT, exact), (a, b)

        def nt_bwd(res, g):
            a, b = res
            return _dot(g, b, NN, back), _dot(g, a, TN, back)

        nt.defvjp(nt_fwd, nt_bwd)

        @jax.custom_vjp
        def tn(a, b):
            return _dot(a, b, TN, exact)

        def tn_fwd(a, b):
            return _dot(a, b, TN, exact), (a, b)

        def tn_bwd(res, g):
            a, b = res
            return _dot(b, g, NT, back), _dot(a, g, NN, back)

        tn.defvjp(tn_fwd, tn_bwd)
        self.nn, self.nt, self.tn = nn, nt, tn


MM = _Matmuls(exact=False, back=False)
MS = _Matmuls(exact="split", back=False)


def _each(fn, *lists):
    return [fn(*args) for args in zip(*lists)]


def _inverse_products(a_lows):
    ri = lax.broadcasted_iota(jnp.int32, (CHUNK, CHUNK), 0)
    ci = lax.broadcasted_iota(jnp.int32, (CHUNK, CHUNK), 1)
    eye = (ri == ci).astype(f32)
    power = _each(lambda a: -a, a_lows)
    inv = _each(lambda p: eye + p, power)
    for _ in range(5):
        power = _each(lambda p: _dot(p, p, NN, "split"), power)
        inv = _each(lambda x, p: x + _dot(x, p, NN, "split"), inv, power)
    return inv


def _inverse_cotangents(invs, gs):
    left = _each(lambda t, g: _dot(t, g, TN, "split"), invs, gs)
    return _each(lambda l, t: -_dot(l, t, NT, "split"), left, invs)


@jax.custom_vjp
def _unit_lower_inverses(a_lows):
    return _inverse_products(a_lows)


def _unit_lower_inverses_fwd(a_lows):
    invs = _inverse_products(a_lows)
    return invs, invs


def _unit_lower_inverses_bwd(invs, gs):
    return (_inverse_cotangents(invs, gs),)


_unit_lower_inverses.defvjp(_unit_lower_inverses_fwd, _unit_lower_inverses_bwd)


@jax.custom_vjp
def _known_inverses(a_lows, invs):
    return invs


def _known_inverses_fwd(a_lows, invs):
    return invs, invs


def _known_inverses_bwd(invs, gs):
    return _inverse_cotangents(invs, gs), _each(jnp.zeros_like, invs)


_known_inverses.defvjp(_known_inverses_fwd, _known_inverses_bwd)


def _params(semantics=None):
    return pltpu.CompilerParams(dimension_semantics=semantics, vmem_limit_bytes=VMEM_LIMIT)


def _silu(x):
    return x * jax.nn.sigmoid(x)


def _slab_sums(x):
    rows, width = x.shape
    sums = [jnp.sum(x[:, j:j + 128], axis=1, keepdims=True) for j in range(0, width, 128)]
    return jnp.concatenate([jnp.broadcast_to(s, (rows, 128)) for s in sums], axis=1)


def _hnorm(x, nw, scale, shift):
    xn = x * lax.rsqrt(jnp.mean(x * x, axis=-1, keepdims=True) + EPS)
    return xn * nw * (1.0 + scale) + shift


def _rope_tables(pos_col, invf_row, sign_row):
    ang = pos_col.astype(f32) * invf_row
    cos_t = jnp.cos(ang)
    sin_t = jnp.sin(ang) * sign_row
    return jnp.concatenate([cos_t] * 4, axis=1), jnp.concatenate([sin_t] * 4, axis=1)


def _rope_partner(x):
    lane = lax.broadcasted_iota(jnp.int32, x.shape, 1)
    width = x.shape[1]
    return jnp.where((lane % AT_DIM) < AT_DIM // 2, pltpu.roll(x, width - AT_DIM // 2, 1), pltpu.roll(x, AT_DIM // 2, 1))


TOKEN_TILE = 256
LANES = 128


def _stage_lanes(stage_ref, first, value):
    for j in range(value.shape[1] // LANES):
        stage_ref[first + j] = value[:, LANES * j:LANES * (j + 1)]


def _stage_to_view(stage_ref, view_ref, dil):
    chunks, rows, _ = stage_ref.shape
    for r in range(dil):
        for j in range(chunks):
            col = (r * chunks + j) * LANES
            view_ref[:, col:col + LANES] = stage_ref.at[j][pl.ds(r, rows // dil, stride=dil), :].astype(view_ref.dtype)


def _view_to_value(stage_ref, view_ref, dil):
    chunks, rows, _ = stage_ref.shape
    for r in range(dil):
        for j in range(chunks):
            col = (r * chunks + j) * LANES
            stage_ref.at[j][pl.ds(r, rows // dil, stride=dil), :] = view_ref[:, col:col + LANES].astype(f32)
    return jnp.concatenate([stage_ref[j] for j in range(chunks)], axis=1)


def _view_spec(width, dil):
    return pl.BlockSpec((TOKEN_TILE // dil, dil * width), lambda i: (i, 0))


def _view_shape(seq, width, dil, dtype=f32):
    return jax.ShapeDtypeStruct((seq // dil, dil * width), dtype)


HEAD_LANES = 128


def _repeat_heads(compact):
    row = lax.broadcasted_iota(jnp.int32, (HEAD_LANES, AT_WIDTH), 0)
    col = lax.broadcasted_iota(jnp.int32, (HEAD_LANES, AT_WIDTH), 1)
    return _dot_sel((row == col // AT_DIM).astype(f32), compact, NN, 1)


def _stage(width):
    return pltpu.VMEM((width // LANES, TOKEN_TILE, LANES), f32)


def _pre_proj(x, pos_col, invf_row, sign_row, norm_w, scale, shift, w_cat):
    seq = x.shape[0]
    tm = TOKEN_TILE

    def body(x_ref, pos_ref, invf_ref, sign_ref, nw_ref, sc_ref, sh_ref, w_ref,
             pdn_ref, pat1_ref, pat4_ref, pat16_ref, zat_ref, h_ref, stage_ref):
        h = _hnorm(x_ref[...], nw_ref[...], sc_ref[...], sh_ref[...]).astype(bf16)
        h_ref[...] = h
        cols = lambda start, width: jnp.dot(h, w_ref[:, start:start + width], preferred_element_type=f32)
        pdn_ref[...] = cols(0, PDN_W)
        cos_t, sin_t = _rope_tables(pos_ref[...], invf_ref[...], sign_ref[...])
        q = cols(PDN_W, AT_WIDTH)
        k = cols(PDN_W + AT_WIDTH, AT_WIDTH)
        parts = (q * cos_t + _rope_partner(q) * sin_t, k * cos_t + _rope_partner(k) * sin_t,
                 cols(PDN_W + 2 * AT_WIDTH, AT_WIDTH))
        for p, part in enumerate(parts):
            pat1_ref[:, p * AT_WIDTH:(p + 1) * AT_WIDTH] = part.astype(bf16)
            _stage_lanes(stage_ref, p * (AT_WIDTH // LANES), part)
        _stage_to_view(stage_ref, pat4_ref, 4)
        _stage_to_view(stage_ref, pat16_ref, 16)
        zat_ref[...] = cols(PDN_W + PAT_W, AT_WIDTH)

    row = lambda w: pl.BlockSpec((1, w), lambda i: (0, 0))
    tile = lambda w: pl.BlockSpec((tm, w), lambda i: (i, 0))
    return pl.pallas_call(
        body, name="pre_proj", grid=(seq // tm,),
        in_specs=[tile(D_MODEL), tile(1), row(128), row(128), row(D_MODEL), row(D_MODEL), row(D_MODEL),
                  pl.BlockSpec((D_MODEL, CAT_W), lambda i: (0, 0))],
        out_specs=[tile(PDN_W), tile(PAT_W), _view_spec(PAT_W, 4), _view_spec(PAT_W, 16), tile(AT_WIDTH), tile(D_MODEL)],
        out_shape=[jax.ShapeDtypeStruct((seq, PDN_W), f32), _view_shape(seq, PAT_W, 1, bf16), _view_shape(seq, PAT_W, 4, bf16),
                   _view_shape(seq, PAT_W, 16, bf16), jax.ShapeDtypeStruct((seq, AT_WIDTH), f32),
                   jax.ShapeDtypeStruct((seq, D_MODEL), bf16)],
        scratch_shapes=[_stage(PAT_W)],
        compiler_params=_params(("parallel",)),
    )(x, pos_col, invf_row, sign_row, norm_w, scale, shift, w_cat)


def _conv_taps(ext_ref, halo, cur, w8):
    rows = cur.shape[0]
    ext_ref[0:8, :] = halo
    ext_ref[8:8 + rows, :] = cur
    out = ext_ref[pl.ds(5, rows), :] * w8[0:1, :]
    for j in range(1, CONV_K):
        out = out + ext_ref[pl.ds(5 + j, rows), :] * w8[j:j + 1, :]
    return out


def _dn_pre(cq, ck, cv, ba, a_row, dt_row):
    sq, sk, v = _silu(cq), _silu(ck), _silu(cv)
    qn = sq * lax.rsqrt(_slab_sums(sq * sq) + EPS)
    kn = sk * lax.rsqrt(_slab_sums(sk * sk) + EPS)
    beta_all = jax.nn.sigmoid(ba)
    g_all = -jnp.exp(a_row) * jax.nn.softplus(ba + dt_row)
    return qn, kn, v, beta_all, g_all


def _dn_intra(qs, ks, vs, betas, gs, known_invs=None):
    ri = lax.broadcasted_iota(jnp.int32, (CHUNK, CHUNK), 0)
    ci = lax.broadcasted_iota(jnp.int32, (CHUNK, CHUNK), 1)
    tril = ri >= ci
    strict = ri > ci
    lower = tril.astype(f32)
    each = _each
    g_wide = each(lambda g: jnp.broadcast_to(g, (CHUNK, DN_DIM)), gs)
    gc = each(lambda g: _sel_left(lower, g), g_wide)
    gc_sq = each(lambda c: c[:, :CHUNK], gc)
    g_end = each(lambda c: jnp.broadcast_to(c[CHUNK - 1:CHUNK, :], (CHUNK, DN_DIM)), gc)
    g_end8 = each(lambda c: jnp.broadcast_to(c[CHUNK - 1:CHUNK, :], (8, DN_DIM)), gc)
    decay = each(lambda s: jnp.exp(jnp.where(tril, s - s.T, -jnp.inf)), gc_sq)
    qs = each(lambda q: q * (DN_DIM ** -0.5), qs)
    kb = each(lambda k, b: k * b, ks, betas)
    vb = each(lambda v, b: v * b, vs, betas)
    kk = each(MM.nt, kb, ks)
    qk = each(MM.nt, qs, ks)
    a_low = each(lambda p, d: jnp.where(strict, p * d, 0.0), kk, decay)
    inv = _unit_lower_inverses(a_low) if known_invs is None else _known_inverses(a_low, known_invs)
    e_gc = each(jnp.exp, gc)
    u = each(MS.nn, inv, vb)
    w = each(lambda x, k, e: MS.nn(x, k * e), inv, kb, e_gc)
    attn = each(lambda p, d: jnp.where(tril, p * d, 0.0), qk, decay)
    q_dec = each(lambda q, e: q * e, qs, e_gc)
    k_dec = each(lambda k, ge, c: k * jnp.exp(ge - c), ks, g_end, gc)
    return u, w, q_dec, k_dec, attn, each(jnp.exp, g_end8), inv


def _dn_step(us, ws, q_decs, k_decs, attns, e_ends, states):
    each = _each
    v_new = each(lambda u, w, s: u - MM.nn(w, s), us, ws, states)
    qs = each(MM.nn, q_decs, states)
    o = each(lambda a, b, c: a + MM.nn(b, c), qs, attns, v_new)
    new_states = each(lambda s, e, k, v: s * e + MM.tn(k, v), states, e_ends, k_decs, v_new)
    return o, new_states


INTRA_CHUNKS = 4
SCAN_CHUNKS = 8


def _intra_specs(nc):
    rows = INTRA_CHUNKS * CHUNK
    cur = pl.BlockSpec((rows, QKV_W), lambda i: (i, 0))
    halo = pl.BlockSpec((8, QKV_W), lambda i: (jnp.maximum(i * (rows // 8) - 1, 0), 0))
    ba = pl.BlockSpec((rows, BA_W), lambda i: (i, (QKV_W + DN_WIDTH) // BA_W))
    conv = pl.BlockSpec((8, QKV_W), lambda i: (0, 0))
    row = pl.BlockSpec((1, BA_W), lambda i: (0, 0))
    wide = pl.BlockSpec((rows, DN_WIDTH), lambda i: (i, 0))
    attn = pl.BlockSpec((INTRA_CHUNKS, DN_HEADS, CHUNK, CHUNK), lambda i: (i, 0, 0, 0))
    e_end = pl.BlockSpec((INTRA_CHUNKS, 8 * DN_HEADS, DN_DIM), lambda i: (i, 0, 0))
    return cur, halo, ba, conv, row, wide, attn, e_end


def _intra_items():
    return [(ci, h, slice(ci * CHUNK, (ci + 1) * CHUNK), slice(h * DN_DIM, (h + 1) * DN_DIM))
            for ci in range(INTRA_CHUNKS) for h in range(DN_HEADS)]


def _intra_inputs(items, qn, kn, v, beta_all, g_all):
    return ([qn[rows, lanes] for _, _, rows, lanes in items], [kn[rows, lanes] for _, _, rows, lanes in items],
            [v[rows, lanes] for _, _, rows, lanes in items], [beta_all[rows, h:h + 1] for _, h, rows, _ in items],
            [g_all[rows, DN_HEADS + h:DN_HEADS + h + 1] for _, h, rows, _ in items])


def _intra_shapes(seq, operand_dtype):
    nc = seq // CHUNK
    wide = jax.ShapeDtypeStruct((seq, DN_WIDTH), f32)
    operand = jax.ShapeDtypeStruct((seq, DN_WIDTH), operand_dtype)
    return [wide, operand, operand, operand, jax.ShapeDtypeStruct((nc, DN_HEADS, CHUNK, CHUNK), f32),
            jax.ShapeDtypeStruct((nc, 8 * DN_HEADS, DN_DIM), f32)]


def _dn_intra_forward(pdn, conv_w8, a_row, dt_row):
    seq = pdn.shape[0]
    nc = seq // CHUNK
    cur, halo, ba, conv, row, wide, attn, e_end = _intra_specs(nc)

    def body(cur_ref, halo_ref, ba_ref, w_ref, a_ref, dt_ref, u_ref, w_out_ref, qd_ref, kd_ref, attn_ref, e_ref, inv_ref,
             ext_ref):
        halo_rows = jnp.where(pl.program_id(0) > 0, halo_ref[...], 0.0)
        c = _conv_taps(ext_ref, halo_rows, cur_ref[...], w_ref[...])
        qn, kn, v, beta_all, g_all = _dn_pre(c[:, :DN_WIDTH], c[:, DN_WIDTH:2 * DN_WIDTH], c[:, 2 * DN_WIDTH:],
                                             ba_ref[...], a_ref[...], dt_ref[...])
        items = _intra_items()
        u, w, qd, kd, at, e8, inv = _dn_intra(*_intra_inputs(items, qn, kn, v, beta_all, g_all))
        for n, (ci, h, rows, lanes) in enumerate(items):
            u_ref[rows, lanes] = u[n]
            w_out_ref[rows, lanes] = w[n].astype(bf16)
            qd_ref[rows, lanes] = qd[n].astype(bf16)
            kd_ref[rows, lanes] = kd[n].astype(bf16)
            attn_ref[ci, h] = at[n]
            e_ref[ci, 8 * h:8 * h + 8, :] = e8[n]
            inv_ref[ci, h] = inv[n]

    return pl.pallas_call(
        body, name="dn_intra_forward", grid=(nc // INTRA_CHUNKS,),
        in_specs=[cur, halo, ba, conv, row, row],
        out_specs=[wide, wide, wide, wide, attn, e_end, attn],
        out_shape=_intra_shapes(seq, bf16) + [jax.ShapeDtypeStruct((nc, DN_HEADS, CHUNK, CHUNK), f32)],
        scratch_shapes=[pltpu.VMEM((INTRA_CHUNKS * CHUNK + 8, QKV_W), f32)],
        compiler_params=_params(("parallel",)),
    )(pdn, pdn, pdn, conv_w8, a_row, dt_row)


def _scan_specs(nc, reverse):
    steps = nc // SCAN_CHUNKS
    at = (lambda i: steps - 1 - i) if reverse else (lambda i: i)
    wide = pl.BlockSpec((SCAN_CHUNKS * CHUNK, DN_WIDTH), lambda i: (at(i), 0))
    attn = pl.BlockSpec((SCAN_CHUNKS, DN_HEADS, CHUNK, CHUNK), lambda i: (at(i), 0, 0, 0))
    e_end = pl.BlockSpec((SCAN_CHUNKS, 8 * DN_HEADS, DN_DIM), lambda i: (at(i), 0, 0))
    states = pl.BlockSpec((SCAN_CHUNKS, DN_HEADS, DN_DIM, DN_DIM), lambda i: (at(i), 0, 0, 0))
    return wide, attn, e_end, states


def _step_inputs(ci, rows, lanes, u_ref, w_ref, qd_ref, kd_ref, attn_ref, e_ref):
    heads = range(DN_HEADS)
    return ([u_ref[rows, lanes[h]] for h in heads], [w_ref[rows, lanes[h]].astype(f32) for h in heads],
            [qd_ref[rows, lanes[h]].astype(f32) for h in heads], [kd_ref[rows, lanes[h]].astype(f32) for h in heads],
            [attn_ref[ci, h] for h in heads], [e_ref[ci, 8 * h:8 * h + 1, :] for h in heads])


def _dn_scan_forward(u, w, q_dec, k_dec, attn, e_end):
    seq = u.shape[0]
    nc = seq // CHUNK
    wide, attn_spec, e_spec, st_spec = _scan_specs(nc, reverse=False)

    def body(u_ref, w_ref, qd_ref, kd_ref, attn_ref, e_ref, o_ref, st_ref, state_ref):
        @pl.when(pl.program_id(0) == 0)
        def _():
            state_ref[...] = jnp.zeros_like(state_ref)

        heads = range(DN_HEADS)
        lanes = [slice(h * DN_DIM, (h + 1) * DN_DIM) for h in heads]
        states = [state_ref[h] for h in heads]
        for ci in range(SCAN_CHUNKS):
            rows = slice(ci * CHUNK, (ci + 1) * CHUNK)
            for h in heads:
                st_ref[ci, h] = states[h].astype(bf16)
            o, states = _dn_step(*_step_inputs(ci, rows, lanes, u_ref, w_ref, qd_ref, kd_ref, attn_ref, e_ref), states)
            for h in heads:
                o_ref[rows, lanes[h]] = o[h]
        for h in heads:
            state_ref[h] = states[h]

    return pl.pallas_call(
        body, name="dn_scan_forward", grid=(nc // SCAN_CHUNKS,),
        in_specs=[wide, wide, wide, wide, attn_spec, e_spec],
        out_specs=[wide, st_spec],
        out_shape=[jax.ShapeDtypeStruct((seq, DN_WIDTH), f32), jax.ShapeDtypeStruct((nc, DN_HEADS, DN_DIM, DN_DIM), bf16)],
        scratch_shapes=[pltpu.VMEM((DN_HEADS, DN_DIM, DN_DIM), f32)],
        compiler_params=_params(("arbitrary",)),
    )(u, w, q_dec, k_dec, attn, e_end)


def _dn_scan_backward(u, w, q_dec, k_dec, attn, e_end, states, d_o):
    seq = u.shape[0]
    nc = seq // CHUNK
    wide, attn_spec, e_spec, st_spec = _scan_specs(nc, reverse=True)

    def body(u_ref, w_ref, qd_ref, kd_ref, attn_ref, e_ref, st_ref, do_ref,
             du_ref, dw_ref, dqd_ref, dkd_ref, dattn_ref, de_ref, dstate_ref):
        @pl.when(pl.program_id(0) == 0)
        def _():
            dstate_ref[...] = jnp.zeros_like(dstate_ref)

        heads = range(DN_HEADS)
        lanes = [slice(h * DN_DIM, (h + 1) * DN_DIM) for h in heads]
        first_row = lax.broadcasted_iota(jnp.int32, (8, DN_DIM), 0) == 0
        dstates = [dstate_ref[h] for h in heads]
        for ci in reversed(range(SCAN_CHUNKS)):
            rows = slice(ci * CHUNK, (ci + 1) * CHUNK)
            _, step_vjp = jax.vjp(_dn_step, *_step_inputs(ci, rows, lanes, u_ref, w_ref, qd_ref, kd_ref, attn_ref, e_ref),
                                  [st_ref[ci, h].astype(f32) for h in heads])
            du, dw, dqd, dkd, dattn, de, dstates = step_vjp(([do_ref[rows, lanes[h]] for h in heads], dstates))
            for h in heads:
                du_ref[rows, lanes[h]] = du[h]
                dw_ref[rows, lanes[h]] = dw[h]
                dqd_ref[rows, lanes[h]] = dqd[h]
                dkd_ref[rows, lanes[h]] = dkd[h]
                dattn_ref[ci, h] = dattn[h]
                de_ref[ci, 8 * h:8 * h + 8, :] = jnp.where(first_row, jnp.broadcast_to(de[h], (8, DN_DIM)), 0.0)
        for h in heads:
            dstate_ref[h] = dstates[h]

    return pl.pallas_call(
        body, name="dn_scan_backward", grid=(nc // SCAN_CHUNKS,),
        in_specs=[wide, wide, wide, wide, attn_spec, e_spec, st_spec, wide],
        out_specs=[wide, wide, wide, wide, attn_spec, e_spec],
        out_shape=_intra_shapes(seq, f32),
        scratch_shapes=[pltpu.VMEM((DN_HEADS, DN_DIM, DN_DIM), f32)],
        compiler_params=_params(("arbitrary",)),
    )(u, w, q_dec, k_dec, attn, e_end, states, d_o)


def _dn_intra_backward(pdn, conv_w8, a_row, dt_row, invs, d_u, d_w, d_qd, d_kd, d_attn, d_e):
    seq = pdn.shape[0]
    nc = seq // CHUNK
    rows_per_step = INTRA_CHUNKS * CHUNK
    cur, halo, ba, conv, row, wide, attn, e_end = _intra_specs(nc)

    def body(cur_ref, halo_ref, ba_ref, w_ref, a_ref, dt_ref, inv_ref, du_ref, dw_ref, dqd_ref, dkd_ref, dattn_ref, de_ref,
             dconv_ref, dba_ref, drow_ref, ext_ref):
        @pl.when(pl.program_id(0) == 0)
        def _():
            drow_ref[...] = jnp.zeros_like(drow_ref)

        halo_rows = jnp.where(pl.program_id(0) > 0, halo_ref[...], 0.0)
        c = _conv_taps(ext_ref, halo_rows, cur_ref[...], w_ref[...])
        (qn, kn, v, beta_all, g_all), pre_vjp = jax.vjp(
            _dn_pre, c[:, :DN_WIDTH], c[:, DN_WIDTH:2 * DN_WIDTH], c[:, 2 * DN_WIDTH:], ba_ref[...], a_ref[...], dt_ref[...])
        lane = lax.broadcasted_iota(jnp.int32, (CHUNK, BA_W), 1)
        items = _intra_items()
        _, intra_vjp = jax.vjp(_dn_intra, *_intra_inputs(items, qn, kn, v, beta_all, g_all),
                               [inv_ref[ci, h] for ci, h, _, _ in items])
        dq, dk, dv, dbeta, dg, _ = intra_vjp((
            [du_ref[rows, lanes] for _, _, rows, lanes in items], [dw_ref[rows, lanes] for _, _, rows, lanes in items],
            [dqd_ref[rows, lanes] for _, _, rows, lanes in items], [dkd_ref[rows, lanes] for _, _, rows, lanes in items],
            [dattn_ref[ci, h] for ci, h, _, _ in items], [de_ref[ci, 8 * h:8 * h + 8, :] for ci, h, _, _ in items],
            [jnp.zeros((CHUNK, CHUNK), f32) for _ in items]))
        dq_rows, dk_rows, dv_rows, dbeta_rows, dg_rows = [], [], [], [], []
        for ci in range(INTRA_CHUNKS):
            of_chunk = [n for n, item in enumerate(items) if item[0] == ci]
            d_beta_all = jnp.zeros((CHUNK, BA_W), f32)
            d_g_all = jnp.zeros((CHUNK, BA_W), f32)
            for n in of_chunk:
                h = items[n][1]
                d_beta_all = d_beta_all + jnp.where(lane == h, dbeta[n], 0.0)
                d_g_all = d_g_all + jnp.where(lane == DN_HEADS + h, dg[n], 0.0)
            dq_rows.append(jnp.concatenate([dq[n] for n in of_chunk], axis=1))
            dk_rows.append(jnp.concatenate([dk[n] for n in of_chunk], axis=1))
            dv_rows.append(jnp.concatenate([dv[n] for n in of_chunk], axis=1))
            dbeta_rows.append(d_beta_all)
            dg_rows.append(d_g_all)
        stack = lambda parts: jnp.concatenate(parts, axis=0)
        dcq, dck, dcv, dba, da_row, ddt_row = pre_vjp(
            (stack(dq_rows), stack(dk_rows), stack(dv_rows), stack(dbeta_rows), stack(dg_rows)))
        dconv_ref[:, :DN_WIDTH] = dcq
        dconv_ref[:, DN_WIDTH:2 * DN_WIDTH] = dck
        dconv_ref[:, 2 * DN_WIDTH:] = dcv
        dba_ref[...] = dba
        drow_ref[0:1, :] += da_row
        drow_ref[1:2, :] += ddt_row

    return pl.pallas_call(
        body, name="dn_intra_backward", grid=(nc // INTRA_CHUNKS,),
        in_specs=[cur, halo, ba, conv, row, row, attn, wide, wide, wide, wide, attn, e_end],
        out_specs=[pl.BlockSpec((rows_per_step, QKV_W), lambda i: (i, 0)),
                   pl.BlockSpec((rows_per_step, BA_W), lambda i: (i, 0)),
                   pl.BlockSpec((8, BA_W), lambda i: (0, 0))],
        out_shape=[jax.ShapeDtypeStruct((seq, QKV_W), f32), jax.ShapeDtypeStruct((seq, BA_W), f32),
                   jax.ShapeDtypeStruct((8, BA_W), f32)],
        scratch_shapes=[pltpu.VMEM((rows_per_step + 8, QKV_W), f32)],
        compiler_params=_params(("arbitrary",)),
    )(pdn, pdn, pdn, conv_w8, a_row, dt_row, invs, d_u, d_w, d_qd, d_kd, d_attn, d_e)


def _band_masks(block_index):
    qi = lax.broadcasted_iota(jnp.int32, (Q_BLOCK, Q_BLOCK), 0)
    kj = lax.broadcasted_iota(jnp.int32, (Q_BLOCK, Q_BLOCK), 1)
    return (kj >= qi) & (block_index > 0), kj <= qi


def _low_half():
    return lax.broadcasted_iota(jnp.int32, (Q_BLOCK, LANES), 1) < AT_DIM


def _head_pairs(ref, split):
    low = _low_half()
    slabs = [ref[:, pair * LANES:(pair + 1) * LANES].astype(bf16) for pair in range(AT_HEADS // 2)]
    if not split:
        return slabs
    zero = jnp.zeros((Q_BLOCK, LANES), bf16)
    return [jnp.where(low, slab, zero) if h == 0 else jnp.where(low, zero, slab) for slab in slabs for h in range(2)]


def _stack_blocks(first, second):
    return [jnp.concatenate([a, b], axis=0) for a, b in zip(first, second)]


ATTN_BLOCKS = 8


def _two_block_spec(width, pieces, piece, which, n_blocks=None):
    b = ATTN_BLOCKS
    if which == "own":
        return pl.BlockSpec((b * Q_BLOCK, width), lambda r, n: (n, pieces * r + piece))
    if which == "before":
        return pl.BlockSpec((Q_BLOCK, width), lambda r, n: (jnp.maximum(b * n - 1, 0), pieces * r + piece))
    return pl.BlockSpec((Q_BLOCK, width), lambda r, n: (jnp.minimum(b * n + b, n_blocks - 1), pieces * r + piece))


def _block_rows(i):
    return pl.ds(i * Q_BLOCK, Q_BLOCK)


def _attn_forward(pat_view, dil):
    length = pat_view.shape[0]
    nb = length // Q_BLOCK
    scale = AT_DIM ** -0.5

    def body(q2_ref, kp_ref, k2_ref, vp_ref, v2_ref, o2_ref, lse2_ref):
        n = pl.program_id(1)
        for i in range(ATTN_BLOCKS):
            rows, before = _block_rows(i), _block_rows(i - 1)
            one_block(ATTN_BLOCKS * n + i, q2_ref.at[rows], kp_ref if i == 0 else k2_ref.at[before], k2_ref.at[rows],
                      vp_ref if i == 0 else v2_ref.at[before], v2_ref.at[rows], o2_ref.at[rows], lse2_ref.at[rows])

    def one_block(block_index, q_ref, kp_ref, kc_ref, vp_ref, vc_ref, o_ref, lse_ref):
        mask = jnp.concatenate(_band_masks(block_index), axis=1)
        low = _low_half()
        heads = range(AT_HEADS)
        q = _head_pairs(q_ref, split=True)
        k = _stack_blocks(_head_pairs(kp_ref, split=False), _head_pairs(kc_ref, split=False))
        v = _stack_blocks(_head_pairs(vp_ref, split=False), _head_pairs(vc_ref, split=False))
        s = [_dot(q[h], k[h // 2], NT, False) for h in heads]
        p, top = [], []
        for h in heads:
            masked = jnp.where(mask, s[h] * scale, NEG)
            m = jnp.max(masked, axis=1, keepdims=True)
            p.append(jnp.exp(masked - m).astype(bf16))
            top.append(m)
        ones = jnp.ones((2 * Q_BLOCK, LANES), bf16)
        l = [_dot(p[h], ones, NN, False) for h in heads]
        o = [_dot(p[h], v[h // 2], NN, False) for h in heads]
        for pair in range(AT_HEADS // 2):
            even, odd = 2 * pair, 2 * pair + 1
            slab = slice(pair * LANES, (pair + 1) * LANES)
            o_ref[:, slab] = jnp.where(low, o[even], o[odd]) / jnp.where(low, l[even], l[odd])
        lane = lax.broadcasted_iota(jnp.int32, (Q_BLOCK, HEAD_LANES), 1)
        lse = jnp.zeros((Q_BLOCK, HEAD_LANES), f32)
        for h in heads:
            lse = jnp.where(lane == h, top[h] + jnp.log(l[h]), lse)
        lse_ref[...] = lse

    blk = functools.partial(_two_block_spec, AT_WIDTH, 3)
    return pl.pallas_call(
        body, name=f"attn_forward_d{dil}", grid=(dil, nb // ATTN_BLOCKS),
        in_specs=[blk(0, "own"), blk(1, "before"), blk(1, "own"), blk(2, "before"), blk(2, "own")],
        out_specs=[_two_block_spec(AT_WIDTH, 1, 0, "own"), _two_block_spec(HEAD_LANES, 1, 0, "own")],
        out_shape=[jax.ShapeDtypeStruct((length, dil * AT_WIDTH), f32),
                   jax.ShapeDtypeStruct((length, dil * HEAD_LANES), f32)],
        compiler_params=_params(("parallel", "parallel")),
    )(pat_view, pat_view, pat_view, pat_view, pat_view)


def _attn_backward_q(pat_view, d_out, delta, lse, dil):
    length = pat_view.shape[0]
    nb = length // Q_BLOCK
    scale = AT_DIM ** -0.5

    def body(q2_ref, kp_ref, k2_ref, vp_ref, v2_ref, do2_ref, dl2_ref, lse2_ref, dq2_ref):
        n = pl.program_id(1)
        for i in range(ATTN_BLOCKS):
            rows, before = _block_rows(i), _block_rows(i - 1)
            one_block(ATTN_BLOCKS * n + i, q2_ref.at[rows], kp_ref if i == 0 else k2_ref.at[before], k2_ref.at[rows],
                      vp_ref if i == 0 else v2_ref.at[before], v2_ref.at[rows],
                      do2_ref.at[rows], dl2_ref.at[rows], lse2_ref.at[rows], dq2_ref.at[rows])

    def one_block(block_index, q_ref, kp_ref, kc_ref, vp_ref, vc_ref, do_ref, dl_ref, lse_ref, dq_ref):
        mask = jnp.concatenate(_band_masks(block_index), axis=1)
        low = _low_half()
        heads = range(AT_HEADS)
        q, do = _head_pairs(q_ref, split=True), _head_pairs(do_ref, split=True)
        k = _stack_blocks(_head_pairs(kp_ref, split=False), _head_pairs(kc_ref, split=False))
        v = _stack_blocks(_head_pairs(vp_ref, split=False), _head_pairs(vc_ref, split=False))
        s = [_dot(q[h], k[h // 2], NT, False) for h in heads]
        dp = [_dot(do[h], v[h // 2], NT, False) for h in heads]
        ds = []
        for h in heads:
            p = jnp.exp(jnp.where(mask, s[h] * scale - lse_ref[:, h:h + 1], NEG))
            ds.append((p * (dp[h] - dl_ref[:, h:h + 1])).astype(bf16))
        dq = [_dot(ds[h], k[h // 2], NN, False) for h in heads]
        for pair in range(AT_HEADS // 2):
            dq_ref[:, pair * LANES:(pair + 1) * LANES] = (
                jnp.where(low, dq[2 * pair], dq[2 * pair + 1]) * scale).astype(bf16)

    blk = functools.partial(_two_block_spec, AT_WIDTH, 3)
    one = _two_block_spec(AT_WIDTH, 1, 0, "own")
    compact = _two_block_spec(HEAD_LANES, 1, 0, "own")
    return pl.pallas_call(
        body, name=f"attn_backward_q_d{dil}", grid=(dil, nb // ATTN_BLOCKS),
        in_specs=[blk(0, "own"), blk(1, "before"), blk(1, "own"), blk(2, "before"), blk(2, "own"), one, compact, compact],
        out_specs=one,
        out_shape=jax.ShapeDtypeStruct((length, dil * AT_WIDTH), bf16),
        compiler_params=_params(("parallel", "parallel")),
    )(pat_view, pat_view, pat_view, pat_view, pat_view, d_out, delta, lse)


def _attn_backward_kv(pat_view, d_out, delta, lse, dil):
    length = pat_view.shape[0]
    nb = length // Q_BLOCK
    scale = AT_DIM ** -0.5

    def body(k2_ref, v2_ref, q2_ref, qn_ref, do2_ref, don_ref, dl2_ref, dln_ref, lse2_ref, lsen_ref, dk2_ref, dv2_ref):
        n = pl.program_id(1)
        for i in range(ATTN_BLOCKS):
            rows, after = _block_rows(i), _block_rows(i + 1)
            last = i == ATTN_BLOCKS - 1
            one_block(ATTN_BLOCKS * n + i, k2_ref.at[rows], v2_ref.at[rows],
                      q2_ref.at[rows], qn_ref if last else q2_ref.at[after],
                      do2_ref.at[rows], don_ref if last else do2_ref.at[after],
                      dl2_ref.at[rows], dln_ref if last else dl2_ref.at[after],
                      lse2_ref.at[rows], lsen_ref if last else lse2_ref.at[after], dk2_ref.at[rows], dv2_ref.at[rows])

    def one_block(j, k_ref, v_ref, qa_ref, qb_ref, doa_ref, dob_ref, dla_ref, dlb_ref, lsea_ref, lseb_ref, dk_ref, dv_ref):
        kj = lax.broadcasted_iota(jnp.int32, (Q_BLOCK, Q_BLOCK), 0)
        qi = lax.broadcasted_iota(jnp.int32, (Q_BLOCK, Q_BLOCK), 1)
        mask = jnp.concatenate([kj <= qi,
                                (kj >= qi) & (j + 1 < nb)],
                               axis=1)
        low = _low_half()
        heads = range(AT_HEADS)
        k, v = _head_pairs(k_ref, split=True), _head_pairs(v_ref, split=True)
        q = _stack_blocks(_head_pairs(qa_ref, split=False), _head_pairs(qb_ref, split=False))
        do = _stack_blocks(_head_pairs(doa_ref, split=False), _head_pairs(dob_ref, split=False))
        row = lax.broadcasted_iota(jnp.int32, (8 * AT_HEADS, HEAD_LANES), 0)
        col = lax.broadcasted_iota(jnp.int32, (8 * AT_HEADS, HEAD_LANES), 1)
        pick = (col == row // 8).astype(f32)
        lse_rows = _dot_sel(pick, jnp.concatenate([lsea_ref[...], lseb_ref[...]], axis=0), NT, 0)
        dl_rows = _dot_sel(pick, jnp.concatenate([dla_ref[...], dlb_ref[...]], axis=0), NT, 0)
        s_t = [_dot(k[h], q[h // 2], NT, False) for h in heads]
        dp_t = [_dot(v[h], do[h // 2], NT, False) for h in heads]
        p_t, ds_t = [], []
        for h in heads:
            prob = jnp.exp(jnp.where(mask, s_t[h] * scale - lse_rows[8 * h:8 * h + 1, :], NEG))
            p_t.append(prob.astype(bf16))
            ds_t.append((prob * (dp_t[h] - dl_rows[8 * h:8 * h + 1, :])).astype(bf16))
        dv = [_dot(p_t[h], do[h // 2], NN, False) for h in heads]
        dk = [_dot(ds_t[h], q[h // 2], NN, False) for h in heads]
        for pair in range(AT_HEADS // 2):
            slab = slice(pair * LANES, (pair + 1) * LANES)
            dk_ref[:, slab] = (jnp.where(low, dk[2 * pair], dk[2 * pair + 1]) * scale).astype(bf16)
            dv_ref[:, slab] = jnp.where(low, dv[2 * pair], dv[2 * pair + 1]).astype(bf16)

    blk = functools.partial(_two_block_spec, AT_WIDTH, 3)
    same = _two_block_spec(AT_WIDTH, 1, 0, "own")
    nxt = _two_block_spec(AT_WIDTH, 1, 0, "after", nb)
    c_same = _two_block_spec(HEAD_LANES, 1, 0, "own")
    c_nxt = _two_block_spec(HEAD_LANES, 1, 0, "after", nb)
    return pl.pallas_call(
        body, name=f"attn_backward_kv_d{dil}", grid=(dil, nb // ATTN_BLOCKS),
        in_specs=[blk(1, "own"), blk(2, "own"), blk(0, "own"), blk(0, "after", nb), same, nxt, c_same, c_nxt, c_same, c_nxt],
        out_specs=[same, same],
        out_shape=[jax.ShapeDtypeStruct((length, dil * AT_WIDTH), bf16)] * 2,
        compiler_params=_params(("parallel", "parallel")),
    )(pat_view, pat_view, pat_view, pat_view, d_out, d_out, delta, delta, lse, lse)


def _gated_norms(o_dn, z_dn, o_at, z_at, dn_w, at_w):
    ms_dn = _slab_sums(o_dn * o_dn) * (1.0 / DN_DIM)
    a = o_dn * lax.rsqrt(ms_dn + EPS) * dn_w * _silu(z_dn)
    head_of_lane = (lax.broadcasted_iota(jnp.int32, (AT_WIDTH, HEAD_LANES), 0) // AT_DIM
                    == lax.broadcasted_iota(jnp.int32, (AT_WIDTH, HEAD_LANES), 1)).astype(f32)
    lanes_of_head = (lax.broadcasted_iota(jnp.int32, (HEAD_LANES, AT_WIDTH), 0)
                     == lax.broadcasted_iota(jnp.int32, (HEAD_LANES, AT_WIDTH), 1) // AT_DIM).astype(f32)
    ms_at = _sel_right(_sel_right(o_at * o_at, head_of_lane), lanes_of_head) * (1.0 / AT_DIM)
    b = o_at * lax.rsqrt(ms_at + EPS) * at_w * _silu(z_at)
    return a, b


def _residual_loss(x, mix, gate, fin_w, target):
    x2 = x + gate * mix
    y = x2 * lax.rsqrt(jnp.mean(x2 * x2, axis=-1, keepdims=True) + EPS) * fin_w
    err = y - target
    per_token = jnp.sum(err * err, axis=1, keepdims=True) * (1.0 / D_MODEL)
    return 0.5 * jnp.sum(per_token, axis=0, keepdims=True)


TAIL_ROWS = 8


def _tail(x, target, o_dn, pdn, outs, lses, z_at, gate, w_out, dn_w_row, at_w_row, fin_w):
    seq = x.shape[0]
    tm = TOKEN_TILE

    def body(x_ref, t_ref, odn_ref, zdn_ref, o1_ref, o4_ref, o16_ref, l1_ref, l4_ref, l16_ref, zat_ref, gate_ref,
             wt_ref, wb_ref, dnw_ref, atw_ref, fw_ref,
             dx2_ref, dodn_ref, dzdn_ref, doat1_ref, doat4_ref, doat16_ref, delta1_ref, delta4_ref, delta16_ref,
             lse1_ref, lse4_ref, lse16_ref, dzat_ref, gwt_ref, gwb_ref, rows_ref, *stages):
        @pl.when(pl.program_id(0) == 0)
        def _():
            gwt_ref[...] = jnp.zeros_like(gwt_ref)
            gwb_ref[...] = jnp.zeros_like(gwb_ref)
            rows_ref[...] = jnp.zeros_like(rows_ref)

        l1, l4, l16 = l1_ref[...], _view_to_value(stages[0], l4_ref, 4), _view_to_value(stages[1], l16_ref, 16)
        o4, o16 = _view_to_value(stages[2], o4_ref, 4), _view_to_value(stages[3], o16_ref, 16)
        top = jnp.maximum(jnp.maximum(l1, l4), l16)
        e1, e4, e16 = jnp.exp(l1 - top), jnp.exp(l4 - top), jnp.exp(l16 - top)
        den = e1 + e4 + e16
        lse = top + jnp.log(den)
        o_at = (_repeat_heads(e1 / den) * o1_ref[...] + _repeat_heads(e4 / den) * o4 + _repeat_heads(e16 / den) * o16)

        (a, b), norms_vjp = jax.vjp(_gated_norms, odn_ref[...], zdn_ref[...], o_at, zat_ref[...], dnw_ref[...], atw_ref[...])
        a, b = a.astype(bf16), b.astype(bf16)
        mix = jnp.dot(a, wt_ref[...], preferred_element_type=f32) + jnp.dot(b, wb_ref[...], preferred_element_type=f32)
        loss, loss_vjp = jax.vjp(_residual_loss, x_ref[...], mix, gate_ref[...], fw_ref[...], t_ref[...])
        dx2, dmix, dgate, dfw, _ = loss_vjp(jnp.ones((1, 1), f32))
        dmix = dmix.astype(bf16)
        dwt = lax.dot_general(a, dmix, (TN, ((), ())), preferred_element_type=f32)
        dwb = lax.dot_general(b, dmix, (TN, ((), ())), preferred_element_type=f32)
        da = lax.dot_general(dmix, wt_ref[...], (NT, ((), ())), preferred_element_type=f32)
        db = lax.dot_general(dmix, wb_ref[...], (NT, ((), ())), preferred_element_type=f32)
        dodn, dzdn, doat, dzat, ddnw, datw = norms_vjp((da, db))
        dx2_ref[...] = dx2
        dodn_ref[...] = dodn
        dzdn_ref[...] = dzdn
        dzat_ref[...] = dzat
        lane_head = (lax.broadcasted_iota(jnp.int32, (AT_WIDTH, HEAD_LANES), 0) // AT_DIM
                     == lax.broadcasted_iota(jnp.int32, (AT_WIDTH, HEAD_LANES), 1))
        delta = _dot_sel(lane_head.astype(f32), doat * o_at, NN, 1)
        for stage_ref, value, refs in ((stages[2], doat, (doat1_ref, doat4_ref, doat16_ref)),
                                       (stages[0], delta, (delta1_ref, delta4_ref, delta16_ref)),
                                       (stages[1], lse, (lse1_ref, lse4_ref, lse16_ref))):
            refs[0][...] = value.astype(refs[0].dtype)
            _stage_lanes(stage_ref, 0, value)
            _stage_to_view(stage_ref, refs[1], 4)
            _stage_to_view(stage_ref, refs[2], 16)
        gwt_ref[...] += dwt
        gwb_ref[...] += dwb
        rows_ref[0:1, :] += jnp.broadcast_to(loss, (1, D_MODEL))
        rows_ref[1:2, :] += dgate
        rows_ref[2:3, :] += dfw
        rows_ref[3:4, 0:DN_WIDTH] += ddnw
        rows_ref[4:5, 0:AT_WIDTH] += datw

    tile = lambda w: pl.BlockSpec((tm, w), lambda i: (i, 0))
    row = lambda w: pl.BlockSpec((1, w), lambda i: (0, 0))
    half_w = pl.BlockSpec((DN_WIDTH, D_MODEL), lambda i: (0, 0))
    sds = lambda w: jax.ShapeDtypeStruct((seq, w), f32)
    views = [_view_spec(AT_WIDTH, d) for d in PATTERN_DILATIONS]
    compact = [_view_spec(HEAD_LANES, d) for d in PATTERN_DILATIONS]
    doat_shapes = [_view_shape(seq, AT_WIDTH, d, bf16) for d in PATTERN_DILATIONS]
    compact_shapes = [_view_shape(seq, HEAD_LANES, d) for d in PATTERN_DILATIONS]
    return pl.pallas_call(
        body, name="tail", grid=(seq // tm,),
        in_specs=[tile(D_MODEL), tile(D_MODEL), tile(DN_WIDTH),
                  pl.BlockSpec((tm, DN_WIDTH), lambda i: (i, QKV_W // DN_WIDTH)),
                  *views, *compact, tile(AT_WIDTH), row(D_MODEL), half_w, pl.BlockSpec((AT_WIDTH, D_MODEL), lambda i: (1, 0)),
                  row(DN_WIDTH), row(AT_WIDTH), row(D_MODEL)],
        out_specs=[tile(D_MODEL), tile(DN_WIDTH), tile(DN_WIDTH), *views, *compact, *compact,
                   tile(AT_WIDTH), half_w, half_w, pl.BlockSpec((TAIL_ROWS, D_MODEL), lambda i: (0, 0))],
        out_shape=[sds(D_MODEL), sds(DN_WIDTH), sds(DN_WIDTH), *doat_shapes, *compact_shapes, *compact_shapes,
                   sds(AT_WIDTH), jax.ShapeDtypeStruct((DN_WIDTH, D_MODEL), f32),
                   jax.ShapeDtypeStruct((AT_WIDTH, D_MODEL), f32), jax.ShapeDtypeStruct((TAIL_ROWS, D_MODEL), f32)],
        scratch_shapes=[_stage(HEAD_LANES)] * 2 + [_stage(AT_WIDTH)] * 2,
        compiler_params=_params(("arbitrary",)),
    )(x, target, o_dn, pdn, outs[0], outs[1], outs[2], lses[0], lses[1], lses[2], z_at, gate, w_out, w_out,
      dn_w_row, at_w_row, fin_w)


PRE_ROWS = 8


def _pre_backward(x, dx2, pos_col, invf_row, sign_row, norm_w, scale, shift, w_cat, conv_w8, pdn, d_conv, d_zdn, d_ba,
                  d_q, d_k, d_v, d_zat):
    seq = x.shape[0]
    tm = TOKEN_TILE
    last = seq // tm - 1

    def body(x_ref, dx2_ref, pos_ref, invf_ref, sign_ref, nw_ref, sc_ref, sh_ref, w_ref, cw_ref,
             pre_ref, prehalo_ref, dc_ref, dchalo_ref, dz_ref, dba_ref,
             dq1_ref, dq4_ref, dq16_ref, dk1_ref, dk4_ref, dk16_ref, dv1_ref, dv4_ref, dv16_ref, dzat_ref,
             gx_ref, dproj_ref, rows_ref, crow_ref, ext_ref, *stages):
        i = pl.program_id(0)

        @pl.when(i == 0)
        def _():
            rows_ref[...] = jnp.zeros_like(rows_ref)
            crow_ref[...] = jnp.zeros_like(crow_ref)

        dc = dc_ref[...]
        ext_ref[0:tm, :] = dc
        ext_ref[tm:tm + 8, :] = jnp.where(i < last, dchalo_ref[...], 0.0)
        w8 = cw_ref[...]
        d_pre = ext_ref[pl.ds(3, tm), :] * w8[0:1, :]
        for j in range(1, CONV_K):
            d_pre = d_pre + ext_ref[pl.ds(3 - j, tm), :] * w8[j:j + 1, :]
        ext_ref[0:8, :] = jnp.where(i > 0, prehalo_ref[...], 0.0)
        ext_ref[8:8 + tm, :] = pre_ref[...]
        for j in range(CONV_K):
            crow_ref[j:j + 1, :] += jnp.sum(dc * ext_ref[pl.ds(5 + j, tm), :], axis=0, keepdims=True)

        cos_t, sin_t = _rope_tables(pos_ref[...], invf_ref[...], sign_ref[...])
        dq = dq1_ref[...] + _view_to_value(stages[0], dq4_ref, 4) + _view_to_value(stages[1], dq16_ref, 16)
        dk = dk1_ref[...] + _view_to_value(stages[2], dk4_ref, 4) + _view_to_value(stages[3], dk16_ref, 16)
        dq = dq * cos_t + _rope_partner(dq * sin_t)
        dk = dk * cos_t + _rope_partner(dk * sin_t)
        dv = dv1_ref[...] + _view_to_value(stages[4], dv4_ref, 4) + _view_to_value(stages[5], dv16_ref, 16)
        dh = jnp.zeros((tm, D_MODEL), f32)
        start = 0
        for piece in (d_pre, dz_ref[...], dba_ref[...], dq, dk, dv, dzat_ref[...]):
            stop = start + piece.shape[1]
            piece = piece.astype(bf16)
            dproj_ref[:, start:stop] = piece
            dh = dh + lax.dot_general(piece, w_ref[:, start:stop], (NT, ((), ())), preferred_element_type=f32)
            start = stop
        _, vjp = jax.vjp(_hnorm, x_ref[...], nw_ref[...], sc_ref[...], sh_ref[...])
        dx, dnw, dsc, dsh = vjp(dh)
        gx_ref[...] = dx + dx2_ref[...]
        rows_ref[0:1, :] += dnw
        rows_ref[1:2, :] += dsc
        rows_ref[2:3, :] += dsh

    tile = lambda w: pl.BlockSpec((tm, w), lambda i: (i, 0))
    row = lambda w: pl.BlockSpec((1, w), lambda i: (0, 0))
    step8 = tm // 8
    return pl.pallas_call(
        body, name="pre_backward", grid=(seq // tm,),
        in_specs=[tile(D_MODEL), tile(D_MODEL), tile(1), row(128), row(128), row(D_MODEL), row(D_MODEL), row(D_MODEL),
                  pl.BlockSpec((D_MODEL, CAT_W), lambda i: (0, 0)), pl.BlockSpec((8, QKV_W), lambda i: (0, 0)),
                  tile(QKV_W), pl.BlockSpec((8, QKV_W), lambda i: (jnp.maximum(i * step8 - 1, 0), 0)),
                  tile(QKV_W), pl.BlockSpec((8, QKV_W), lambda i: (jnp.minimum((i + 1) * step8, seq // 8 - 1), 0)),
                  tile(DN_WIDTH), tile(BA_W)] + [_view_spec(AT_WIDTH, d) for d in PATTERN_DILATIONS] * 3 + [tile(AT_WIDTH)],
        out_specs=[tile(D_MODEL), tile(CAT_W), pl.BlockSpec((PRE_ROWS, D_MODEL), lambda i: (0, 0)),
                   pl.BlockSpec((8, QKV_W), lambda i: (0, 0))],
        out_shape=[jax.ShapeDtypeStruct((seq, D_MODEL), f32), jax.ShapeDtypeStruct((seq, CAT_W), bf16),
                   jax.ShapeDtypeStruct((PRE_ROWS, D_MODEL), f32), jax.ShapeDtypeStruct((8, QKV_W), f32)],
        scratch_shapes=[pltpu.VMEM((tm + 8, QKV_W), f32)] + [_stage(AT_WIDTH)] * 6,
        compiler_params=_params(("arbitrary",)),
    )(x, dx2, pos_col, invf_row, sign_row, norm_w, scale, shift, w_cat, conv_w8, pdn, pdn, d_conv, d_conv, d_zdn, d_ba,
      d_q[0], d_q[1], d_q[2], d_k[0], d_k[1], d_k[2], d_v[0], d_v[1], d_v[2], d_zat)


def _weight_grad(h, d_proj):
    seq = h.shape[0]
    tk, tn = 1024, CAT_W // 3
    n_k = seq // tk

    def body(h_ref, d_ref, o_ref):
        @pl.when(pl.program_id(1) == 0)
        def _():
            o_ref[...] = jnp.zeros_like(o_ref)

        o_ref[...] += lax.dot_general(h_ref[...], d_ref[...], (TN, ((), ())), preferred_element_type=f32)

    return pl.pallas_call(
        body, name="weight_grad", grid=(CAT_W // tn, n_k),
        in_specs=[pl.BlockSpec((tk, D_MODEL), lambda n, k: (k, 0)), pl.BlockSpec((tk, tn), lambda n, k: (k, n))],
        out_specs=pl.BlockSpec((D_MODEL, tn), lambda n, k: (0, n)),
        out_shape=jax.ShapeDtypeStruct((D_MODEL, CAT_W), f32),
        compiler_params=_params(("parallel", "arbitrary")),
    )(h, d_proj)


def _adamw(w, g, m, v):
    m = ADAM_B1 * m + (1.0 - ADAM_B1) * g
    v = ADAM_B2 * v + (1.0 - ADAM_B2) * (g * g)
    m_hat = m / (1.0 - ADAM_B1 ** ADAM_STEP)
    v_hat = v / (1.0 - ADAM_B2 ** ADAM_STEP)
    delta = -ADAM_LR * (m_hat / (jnp.sqrt(v_hat) + ADAM_EPS) + ADAM_WD * w)
    return delta, m, v


def _mod_shard(c_all, w_mod_s, b_mod_s):
    def body(c_ref, w_ref, b_ref, o_ref):
        o_ref[...] = _dot(_silu(c_ref[...]), w_ref[...], NN, True) + b_ref[...]

    return pl.pallas_call(body, name="mod_shard", out_shape=jax.ShapeDtypeStruct((N_DEV, w_mod_s.shape[1]), f32),
                          compiler_params=_params())(c_all, w_mod_s, b_mod_s)


def _mod_update(c_all, d_mod_s, w, m, v):
    def body(c_ref, d_ref, w_ref, m_ref, v_ref, g_ref, dw_ref, nm_ref, nv_ref):
        g = _dot(_silu(c_ref[...]), d_ref[...], TN, True)
        g_ref[...] = g
        dw_ref[...], nm_ref[...], nv_ref[...] = _adamw(w_ref[...], g, m_ref[...], v_ref[...])

    return pl.pallas_call(body, name="mod_update", out_shape=[jax.ShapeDtypeStruct(w.shape, f32)] * 4,
                          compiler_params=_params())(c_all, d_mod_s, w, m, v)


def _adamw_rows(w, g, m, v, name):
    rows, width = w.shape
    tr = 128 if rows % 128 == 0 else rows

    def body(w_ref, g_ref, m_ref, v_ref, dw_ref, nm_ref, nv_ref):
        dw_ref[...], nm_ref[...], nv_ref[...] = _adamw(w_ref[...], g_ref[...], m_ref[...], v_ref[...])

    spec = pl.BlockSpec((tr, width), lambda i: (i, 0))
    return pl.pallas_call(body, name=name, grid=(rows // tr,), in_specs=[spec] * 4, out_specs=[spec] * 3,
                          out_shape=[jax.ShapeDtypeStruct(w.shape, f32)] * 3,
                          compiler_params=_params(("parallel",)))(w, g, m, v)


SM_NORM, SM_FIN, SM_DN, SM_AT, SM_A, SM_DT, SM_LOSS, SM_MOD = 0, 1024, 2048, 2560, 3072, 3200, 3328, 3456
SM_W = SM_MOD + 3 * D_MODEL
RS_NORM, RS_FIN, RS_DN, RS_AT, RS_A, RS_DT, RS_BMOD = 0, 1024, 2048, 2176, 2304, 2432, 2560
RS_W = RS_BMOD + 3 * D_MODEL


def _small_update(gathered, w, m, v):
    def body(g_ref, w_ref, m_ref, v_ref, grad_ref, dw_ref, nm_ref, nv_ref, loss_ref, dmod_ref):
        total = g_ref[0:1, :]
        for dev in range(1, N_DEV):
            total = total + g_ref[8 * dev:8 * dev + 1, :]
        for dev in range(N_DEV):
            dmod_ref[dev:dev + 1, :] = g_ref[8 * dev:8 * dev + 1, SM_MOD:SM_W]
        dn = total[:, SM_DN:SM_DN + DN_DIM]
        for h in range(1, DN_HEADS):
            dn = dn + total[:, SM_DN + h * DN_DIM:SM_DN + (h + 1) * DN_DIM]
        at = total[:, SM_AT:SM_AT + AT_DIM]
        for h in range(1, AT_HEADS):
            at = at + total[:, SM_AT + h * AT_DIM:SM_AT + (h + 1) * AT_DIM]
        grad_ref[:, RS_NORM:RS_FIN] = total[:, SM_NORM:SM_FIN]
        grad_ref[:, RS_FIN:RS_DN] = total[:, SM_FIN:SM_DN]
        grad_ref[:, RS_DN:RS_AT] = dn
        grad_ref[:, RS_AT:RS_A] = jnp.zeros((1, 128), f32)
        grad_ref[:, RS_AT:RS_AT + AT_DIM] = at
        grad_ref[:, RS_A:RS_DT] = total[:, SM_A:SM_DT]
        grad_ref[:, RS_DT:RS_BMOD] = total[:, SM_DT:SM_LOSS]
        grad_ref[:, RS_BMOD:RS_W] = total[:, SM_MOD:SM_W]
        loss_ref[...] = total[:, SM_LOSS:SM_MOD]
        dw_ref[...], nm_ref[...], nv_ref[...] = _adamw(w_ref[...], grad_ref[...], m_ref[...], v_ref[...])

    row = jax.ShapeDtypeStruct((1, RS_W), f32)
    return pl.pallas_call(
        body, name="small_update",
        out_shape=[row, row, row, row, jax.ShapeDtypeStruct((1, 128), f32), jax.ShapeDtypeStruct((N_DEV, 3 * D_MODEL), f32)],
        compiler_params=_params())(gathered, w, m, v)


def _all_gather_rows(block, name):
    m_per, n = block.shape

    def body(x_ref, out_ref, send_sems, recv_sems, local_sem):
        x, y, c = lax.axis_index("x"), lax.axis_index("y"), lax.axis_index("c")
        me, sibling = (x, y, c), (x, y, 1 - c)
        chips = [(1 - x, y), (x, 1 - y), (1 - x, 1 - y)]

        def rows(px, py, pc):
            return out_ref.at[pl.ds((4 * px + 2 * py + pc) * m_per, m_per), :]

        def copy(k, blk, to, src=None):
            return pltpu.make_async_remote_copy(
                src_ref=rows(*blk) if src is None else src, dst_ref=rows(*blk),
                send_sem=send_sems.at[k], recv_sem=recv_sems.at[k], device_id=to, device_id_type=MESH)

        mine = pltpu.make_async_copy(x_ref, rows(*me), local_sem)
        mine.start()
        first = [copy(0, me, sibling, src=x_ref)]
        first += [copy(1 + j, me, (*chip, c), src=x_ref) for j, chip in enumerate(chips)]
        for cp in first:
            cp.start()
        passed = [copy(4 + j, (*chip, c), sibling) for j, chip in enumerate(chips)]
        for j, chip in enumerate(chips):
            copy(1 + j, (*chip, c), me).wait_recv()
            passed[j].start()
        copy(0, sibling, me).wait_recv()
        for j, chip in enumerate(chips):
            copy(4 + j, (*chip, 1 - c), me).wait_recv()
        for cp in first + passed:
            cp.wait_send()
        mine.wait()

    return pl.pallas_call(
        body, name=name,
        out_shape=jax.ShapeDtypeStruct((N_DEV * m_per, n), block.dtype),
        in_specs=[pl.BlockSpec(memory_space=pltpu.VMEM)],
        out_specs=pl.BlockSpec(memory_space=pltpu.VMEM),
        scratch_shapes=[pltpu.SemaphoreType.DMA((7,)), pltpu.SemaphoreType.DMA((7,)), pltpu.SemaphoreType.DMA],
        compiler_params=pltpu.CompilerParams(vmem_limit_bytes=VMEM_LIMIT),
    )(block)


def _gather_weight_shards(packed):
    rows, width = packed.shape
    half = rows // 2

    def body(src_ref, out_ref, send_sems, recv_sems, local_sem):
        x, y, c = lax.axis_index("x"), lax.axis_index("y"), lax.axis_index("c")
        chips = [(1 - x, y), (x, 1 - y), (1 - x, 1 - y)]
        my_rows = pl.ds(pl.multiple_of(c * half, 16), half)
        sibling_rows = pl.ds(pl.multiple_of((1 - c) * half, 16), half)

        def part(chip, part_rows):
            return out_ref.at[2 * chip[0] + chip[1], part_rows]

        def copy(k, src, dst, to):
            return pltpu.make_async_remote_copy(src_ref=src, dst_ref=dst, send_sem=send_sems.at[k], recv_sem=recv_sems.at[k],
                                                device_id=to, device_id_type=MESH)

        mine = pltpu.make_async_copy(src_ref, out_ref.at[2 * x + y], local_sem)
        mine.start()
        sends = [copy(k, src_ref.at[my_rows], part((x, y), my_rows), (*chip, c)) for k, chip in enumerate(chips)]
        for cp in sends:
            cp.start()
        passed = []
        for k, chip in enumerate(chips):
            landed = part(chip, my_rows)
            copy(k, landed, landed, (*chip, c)).wait_recv()
            passed.append(copy(3 + k, landed, landed, (x, y, 1 - c)))
            passed[k].start()
        for k, chip in enumerate(chips):
            from_sibling = part(chip, sibling_rows)
            copy(3 + k, from_sibling, from_sibling, (x, y, 1 - c)).wait_recv()
        for cp in sends + passed:
            cp.wait_send()
        mine.wait()

    return pl.pallas_call(
        body, name="gather_weight_shards",
        out_shape=jax.ShapeDtypeStruct((N_CHIPS, rows, width), packed.dtype),
        in_specs=[pl.BlockSpec(memory_space=pl.ANY)],
        out_specs=pl.BlockSpec(memory_space=pl.ANY),
        scratch_shapes=[pltpu.SemaphoreType.DMA((6,)), pltpu.SemaphoreType.DMA((6,)), pltpu.SemaphoreType.DMA],
    )(packed)


def _reduce_weight_grads(grads):
    _, rows, width = grads.shape
    half = rows // 2
    n_chunks = half // PACK_CHUNK

    def body(g_ref, out_ref, mine_ref, sibling_ref, send_ref, land_ref, send_sems, recv_sems, local_sem):
        x, y, c = lax.axis_index("x"), lax.axis_index("y"), lax.axis_index("c")
        chips = [(1 - x, y), (x, 1 - y), (1 - x, 1 - y)]
        sibling = (x, y, 1 - c)
        my_rows = pl.ds(pl.multiple_of(c * half, 16), half)
        sibling_rows = pl.ds(pl.multiple_of((1 - c) * half, 16), half)
        all_slots = pl.ds(0, N_CHIPS)

        def copy(k, src, dst, to):
            return pltpu.make_async_remote_copy(src_ref=src, dst_ref=dst, send_sem=send_sems.at[k], recv_sem=recv_sems.at[k],
                                                device_id=to, device_id_type=MESH)

        mine = pltpu.make_async_copy(g_ref.at[all_slots, my_rows], mine_ref, local_sem)
        mine.start()
        swap = copy(0, g_ref.at[all_slots, sibling_rows], sibling_ref, sibling)
        swap.start()
        mine.wait()
        swap.wait_recv()

        def pair_sums(i, carry):
            r = pl.ds(pl.multiple_of(i * PACK_CHUNK, 16), PACK_CHUNK)
            for k, chip in enumerate(chips):
                slot = 2 * chip[0] + chip[1]
                send_ref[k, r, :] = (mine_ref[slot, r, :] + sibling_ref[slot, r, :]).astype(bf16)
            out_rows = pl.ds(pl.multiple_of(c * half + i * PACK_CHUNK, 16), PACK_CHUNK)
            out_ref[out_rows, :] = mine_ref[2 * x + y, r, :] + sibling_ref[2 * x + y, r, :]
            return carry

        lax.fori_loop(0, n_chunks, pair_sums, 0)
        sends = [copy(1 + k, send_ref.at[k], land_ref.at[k], (*chip, c)) for k, chip in enumerate(chips)]
        for cp in sends:
            cp.start()
        for cp in sends:
            cp.wait_recv()

        def add_landed(i, carry):
            r = pl.ds(pl.multiple_of(i * PACK_CHUNK, 16), PACK_CHUNK)
            out_rows = pl.ds(pl.multiple_of(c * half + i * PACK_CHUNK, 16), PACK_CHUNK)
            landed = [land_ref[k, r, :].astype(f32) for k in range(3)]
            out_ref[out_rows, :] = ((out_ref[out_rows, :] + landed[0]) + landed[1]) + landed[2]
            return carry

        lax.fori_loop(0, n_chunks, add_landed, 0)
        finished = copy(4, out_ref.at[my_rows], out_ref.at[my_rows], sibling)
        finished.start()
        copy(4, out_ref.at[sibling_rows], out_ref.at[sibling_rows], sibling).wait_recv()
        for cp in [swap, finished] + sends:
            cp.wait_send()

    return pl.pallas_call(
        body, name="reduce_weight_grads",
        out_shape=jax.ShapeDtypeStruct((rows, width), f32),
        in_specs=[pl.BlockSpec(memory_space=pl.ANY)],
        out_specs=pl.BlockSpec(memory_space=pltpu.VMEM),
        scratch_shapes=[pltpu.VMEM((N_CHIPS, half, width), f32), pltpu.VMEM((N_CHIPS, half, width), f32),
                        pltpu.VMEM((3, half, width), bf16), pltpu.VMEM((3, half, width), bf16),
                        pltpu.SemaphoreType.DMA((5,)), pltpu.SemaphoreType.DMA((5,)), pltpu.SemaphoreType.DMA],
        compiler_params=pltpu.CompilerParams(vmem_limit_bytes=VMEM_LIMIT),
    )(grads)


CONV_SHARD = QKV_W // N_CHIPS


def _pack_shards(w_in_s, w_out_s, conv_rows):
    lead = w_in_s.shape[:-2]

    def widen(a, rows):
        return jnp.pad(a, [(0, 0)] * len(lead) + [(0, rows - a.shape[-2]), (0, PACK_W - a.shape[-1])])

    used = D_MODEL + OUT_ROWS + 8
    return jnp.concatenate([widen(w_in_s, D_MODEL), widen(w_out_s, OUT_ROWS), widen(conv_rows, 8),
                            jnp.zeros((*lead, PACK_ROWS - used, PACK_W), w_in_s.dtype)], axis=-2)


def _unpack_shards(packed):
    return (packed[..., :D_MODEL, :SHARD_IN], packed[..., D_MODEL:D_MODEL + OUT_ROWS, :D_MODEL],
            packed[..., D_MODEL + OUT_ROWS:D_MODEL + OUT_ROWS + CONV_K, :])


def _pack_weights_bf16(w_in_s, w_out_s, conv_s):
    conv_bits = lax.bitcast_convert_type(conv_s, bf16).reshape(CONV_K, 2 * CONV_SHARD)
    return _pack_shards(w_in_s.astype(bf16), w_out_s.astype(bf16), conv_bits)


def _unpack_weights_bf16(shards):
    w_in_s, w_out_s, conv_bits = _unpack_shards(shards)
    conv_s = lax.bitcast_convert_type(conv_bits[..., :2 * CONV_SHARD].reshape(N_CHIPS, CONV_K, CONV_SHARD, 2), f32)
    return (jnp.transpose(w_in_s, (1, 0, 2)).reshape(D_MODEL, IN_COLS), w_out_s.reshape(D_MODEL, D_MODEL),
            jnp.transpose(conv_s, (1, 0, 2)).reshape(CONV_K, QKV_W))


def _local_step(x, target, pos_col, mod_row, norm_w, w_in, conv_w, a_log, dt_bias, dn_norm_w, at_norm_w, w_out, fin_w):
    shift, scale, gate = mod_row[:, :D_MODEL], mod_row[:, D_MODEL:2 * D_MODEL], mod_row[:, 2 * D_MODEL:]
    half = AT_DIM // 2
    lane = jnp.arange(128)
    inv_freq = ROPE_THETA ** (-jnp.arange(half, dtype=f32) / half)
    invf_row = inv_freq[lane % half].reshape(1, 128)
    sign_row = jnp.where((lane % AT_DIM) < half, -1.0, 1.0).astype(f32).reshape(1, 128)
    ba_w = jnp.pad(w_in[:, 2 * D_MODEL:2 * D_MODEL + 2 * DN_HEADS], ((0, 0), (0, BA_W - 2 * DN_HEADS)))
    w_cat = jnp.concatenate([w_in[:, :2 * D_MODEL], ba_w, w_in[:, 2 * D_MODEL + 2 * DN_HEADS:]], axis=1).astype(bf16)
    conv_w8 = jnp.pad(conv_w, ((0, 8 - CONV_K), (0, 0)))
    a_row = jnp.pad(a_log.reshape(1, DN_HEADS), ((0, 0), (DN_HEADS, BA_W - 2 * DN_HEADS)))
    dt_row = jnp.pad(dt_bias.reshape(1, DN_HEADS), ((0, 0), (DN_HEADS, BA_W - 2 * DN_HEADS)))
    dn_w_row = jnp.tile(dn_norm_w.reshape(1, DN_DIM), (1, DN_HEADS))
    at_w_row = jnp.tile(at_norm_w.reshape(1, AT_DIM), (1, AT_HEADS))
    norm_row = norm_w.reshape(1, D_MODEL)
    fin_row = fin_w.reshape(1, D_MODEL)
    w_out_bf = w_out.astype(bf16)

    pdn, *views, z_at, h = _pre_proj(x, pos_col, invf_row, sign_row, norm_row, scale, shift, w_cat)
    *dn_parts, dn_invs = _dn_intra_forward(pdn, conv_w8, a_row, dt_row)
    o_dn, states = _dn_scan_forward(*dn_parts)
    outs, lses = [], []
    for d, view in zip(PATTERN_DILATIONS, views):
        o, lse = _attn_forward(view, d)
        outs.append(o)
        lses.append(lse)
    (dx2, d_odn, d_zdn, *in_views, d_zat, g_wtop, g_wbot, tail_rows) = _tail(
        x, target, o_dn, pdn, outs, lses, z_at, gate, w_out_bf, dn_w_row, at_w_row, fin_row)
    d_oat, delta, lse_all = in_views[0:3], in_views[3:6], in_views[6:9]
    d_q, d_k, d_v = [], [], []
    for n, (d, view) in enumerate(zip(PATTERN_DILATIONS, views)):
        args = (d_oat[n], delta[n], lse_all[n])
        d_q.append(_attn_backward_q(view, *args, d))
        dk, dv = _attn_backward_kv(view, *args, d)
        d_k.append(dk)
        d_v.append(dv)
    d_parts = _dn_scan_backward(*dn_parts, states, d_odn)
    d_conv, d_ba, dn_rows = _dn_intra_backward(pdn, conv_w8, a_row, dt_row, dn_invs, *d_parts)
    grad_x, d_proj, pre_rows, conv_rows = _pre_backward(
        x, dx2, pos_col, invf_row, sign_row, norm_row, scale, shift, w_cat, conv_w8, pdn, d_conv, d_zdn, d_ba,
        d_q, d_k, d_v, d_zat)
    g_cat = _weight_grad(h, d_proj)
    g_w_in = jnp.concatenate([g_cat[:, :2 * D_MODEL], g_cat[:, 2 * D_MODEL:2 * D_MODEL + 2 * DN_HEADS],
                              g_cat[:, 2 * D_MODEL + BA_W:]], axis=1)
    g_w_out = jnp.concatenate([g_wtop, g_wbot], axis=0)
    g_conv = conv_rows[:CONV_K]
    small = jnp.concatenate([
        pre_rows[0:1], tail_rows[2:3], tail_rows[3:4, :DN_WIDTH], tail_rows[4:5, :AT_WIDTH],
        dn_rows[0:1], dn_rows[1:2], tail_rows[0:1, :128],
        pre_rows[2:3], pre_rows[1:2], tail_rows[1:2]], axis=1)
    return grad_x, g_w_in, g_w_out, g_conv, small


def kernel(x, c, positions, w_mod, b_mod, norm_w, w_in, conv_w, a_log, dt_bias, dn_norm_w, at_norm_w, w_out, final_norm_w, loss_target, m_w_mod, m_b_mod, m_norm_w, m_w_in, m_conv_w, m_a_log, m_dt_bias, m_dn_norm_w, m_at_norm_w, m_w_out, m_final_norm_w, v_w_mod, v_b_mod, v_norm_w, v_w_in, v_conv_w, v_a_log, v_dt_bias, v_dn_norm_w, v_at_norm_w, v_w_out, v_final_norm_w):
    seq = x.shape[1]
    ax, ay, ac = lax.axis_index("x"), lax.axis_index("y"), lax.axis_index("c")
    me = 4 * ax + 2 * ay + ac
    chip = 2 * ax + ay

    c_all = _all_gather_rows(jnp.pad(c, ((0, 7), (0, 0))), "gather_c").reshape(N_DEV, 8, D_MODEL)[:, 0]
    b_mod_s = lax.dynamic_slice_in_dim(b_mod, chip * 768, 768, axis=1)
    mod_part = _mod_shard(c_all, w_mod[0], b_mod_s)
    mod_all = _all_gather_rows(mod_part, "gather_mod").reshape(N_CHIPS, 2, N_DEV, 768)[:, 0]
    mod_row = lax.dynamic_index_in_dim(mod_all, me, axis=1, keepdims=False).reshape(1, 3 * D_MODEL)

    shards = _gather_weight_shards(_pack_weights_bf16(w_in[0], w_out[0], conv_w[0]))
    w_in_full, w_out_full, conv_full = _unpack_weights_bf16(shards)

    grad_x, g_w_in, g_w_out, g_conv, small = _local_step(
        x[0], loss_target[0], positions.reshape(seq, 1), mod_row, norm_w, w_in_full, conv_full, a_log, dt_bias,
        dn_norm_w, at_norm_w, w_out_full, final_norm_w)

    g_pack = _pack_shards(jnp.transpose(g_w_in.reshape(D_MODEL, N_CHIPS, SHARD_IN), (1, 0, 2)),
                          g_w_out.reshape(N_CHIPS, OUT_ROWS, D_MODEL),
                          jnp.transpose(g_conv.reshape(CONV_K, N_CHIPS, CONV_SHARD), (1, 0, 2)))
    grad_w_in, grad_w_out, grad_conv_rows = _unpack_shards(_reduce_weight_grads(g_pack))
    grad_conv_w = grad_conv_rows[:, :CONV_SHARD]
    delta_w_in, new_m_w_in, new_v_w_in = _adamw_rows(w_in[0], grad_w_in, m_w_in[0], v_w_in[0], "adamw_w_in")
    delta_w_out, new_m_w_out, new_v_w_out = _adamw_rows(w_out[0], grad_w_out, m_w_out[0], v_w_out[0], "adamw_w_out")
    delta_conv_w, new_m_conv_w, new_v_conv_w = _adamw_rows(conv_w[0], grad_conv_w, m_conv_w[0], v_conv_w[0], "adamw_conv_w")

    gathered = _all_gather_rows(jnp.pad(small, ((0, 7), (0, 0))), "gather_small")

    def small_row(norm, fin, dn, at, a, dt, bmod):
        z = lambda n: jnp.zeros((1, n), f32)
        return jnp.concatenate([norm.reshape(1, -1), fin.reshape(1, -1), dn.reshape(1, -1), at.reshape(1, -1), z(64),
                                z(4), a.reshape(1, -1), z(120), z(4), dt.reshape(1, -1), z(120), bmod.reshape(1, -1)], axis=1)

    g_small, d_small, m_small, v_small, loss_row, d_mod_all = _small_update(
        gathered,
        small_row(norm_w, final_norm_w, dn_norm_w, at_norm_w, a_log, dt_bias, b_mod),
        small_row(m_norm_w, m_final_norm_w, m_dn_norm_w, m_at_norm_w, m_a_log, m_dt_bias, m_b_mod),
        small_row(v_norm_w, v_final_norm_w, v_dn_norm_w, v_at_norm_w, v_a_log, v_dt_bias, v_b_mod))

    def split_small(r):
        return (r[:, RS_BMOD:RS_W], r[:, RS_NORM:RS_FIN], r[:, RS_A + DN_HEADS:RS_A + 2 * DN_HEADS],
                r[:, RS_DT + DN_HEADS:RS_DT + 2 * DN_HEADS], r[:, RS_DN:RS_AT], r[:, RS_AT:RS_AT + AT_DIM],
                r[0, RS_FIN:RS_DN])

    d_mod_s = lax.dynamic_slice_in_dim(d_mod_all, chip * 768, 768, axis=1)
    pad_rows = lambda a: jnp.pad(a, ((0, 128 - N_DEV), (0, 0)))
    grad_w_mod, delta_w_mod, new_m_w_mod, new_v_w_mod = _mod_update(
        pad_rows(c_all), pad_rows(d_mod_s), w_mod[0], m_w_mod[0], v_w_mod[0])

    def ordered(w_mod_leaf, small_row_leaf, w_in_leaf, conv_leaf, w_out_leaf):
        b, n, a, dt, dn, at, fin = split_small(small_row_leaf)
        return [w_mod_leaf[None], b, n, w_in_leaf[None], conv_leaf[None], a, dt, dn, at, w_out_leaf[None], fin]

    loss = loss_row[0, 0]
    return (loss, grad_x[None],
            *ordered(grad_w_mod, g_small, grad_w_in, grad_conv_w, grad_w_out),
            *ordered(delta_w_mod, d_small, delta_w_in, delta_conv_w, delta_w_out),
            *ordered(new_m_w_mod, m_small, new_m_w_in, new_m_conv_w, new_m_w_out),
            *ordered(new_v_w_mod, v_small, new_v_w_in, new_v_conv_w, new_v_w_out))
```
